```python
import jax, jax.numpy as jnp
from jax import lax
import numpy as np

D_MODEL = 1024
BATCH = 8
SEQ = 8192
DEPTH = 1

PLE_DIM = 256
MIX_WIDTH = D_MODEL
HEAD_DIM = 64
ATT_WIDTH = MIX_WIDTH // 2
N_ATT_HEADS = ATT_WIDTH // HEAD_DIM
SG_WIDTH = MIX_WIDTH - ATT_WIDTH
N_SG_GROUPS = 8
SG_GROUP_DIM = SG_WIDTH // N_SG_GROUPS
CHUNK = 128
Q_BLOCK = 128
D_FF = 4 * D_MODEL
IN_PROJ_WIDTH = 3 * ATT_WIDTH + N_ATT_HEADS + 2 * SG_WIDTH
EPS = 1e-6

kernel_name = "hybrid_fox_gmlp_sandwich_ple"


def rms_norm(x, g):
    xf = x.astype(jnp.float32)
    y = xf * lax.rsqrt(jnp.mean(xf * xf, axis=-1, keepdims=True) + EPS)
    return (y * g.astype(jnp.float32)).astype(x.dtype)


def layer_norm(x, g, b):
    xf = x.astype(jnp.float32)
    mu = jnp.mean(xf, axis=-1, keepdims=True)
    xc = xf - mu
    y = xc * lax.rsqrt(jnp.mean(xc * xc, axis=-1, keepdims=True) + EPS)
    return (y * g.astype(jnp.float32) + b.astype(jnp.float32)).astype(x.dtype)


def forgetting_attention(q, k, v, f_logit):
    B, S, H, Dh = q.shape
    nb = S // Q_BLOCK
    log_f = jax.nn.log_sigmoid(f_logit.astype(jnp.float32))
    c = jnp.transpose(jnp.cumsum(log_f, axis=1), (0, 2, 1))
    qh = jnp.transpose(q, (0, 2, 1, 3)).astype(jnp.float32) * (Dh ** -0.5)
    kh = jnp.transpose(k, (0, 2, 1, 3)).astype(jnp.float32)
    vh = jnp.transpose(v, (0, 2, 1, 3)).astype(jnp.float32)
    q_blocks = jnp.transpose(qh.reshape(B, H, nb, Q_BLOCK, Dh), (2, 0, 1, 3, 4))
    cq_blocks = jnp.transpose(c.reshape(B, H, nb, Q_BLOCK), (2, 0, 1, 3))
    k_pos = jnp.arange(S)

    def one_block(args):
        qb, cqb, bi = args
        q_pos = bi * Q_BLOCK + jnp.arange(Q_BLOCK)
        logits = (jnp.einsum('bhqd,bhkd->bhqk', qb, kh)
                  + cqb[..., :, None] - c[..., None, :])
        causal = k_pos[None, :] <= q_pos[:, None]
        logits = jnp.where(causal, logits, -jnp.inf)
        probs = jax.nn.softmax(logits, axis=-1)
        return jnp.einsum('bhqk,bhkd->bhqd', probs, vh)

    out = lax.map(one_block, (q_blocks, cq_blocks, jnp.arange(nb)))
    out = jnp.transpose(out, (1, 0, 3, 2, 4)).reshape(B, S, H * Dh)
    return out.astype(q.dtype)


def chunked_spatial_gating(u, v, ln_g, ln_b, w_s, b_s):
    B, S, _ = u.shape
    nc = S // CHUNK
    u = jax.nn.gelu(u)
    v = layer_norm(jax.nn.gelu(v), ln_g, ln_b)
    vc = v.reshape(B, nc, CHUNK, N_SG_GROUPS, SG_GROUP_DIM)
    mask = jnp.tril(jnp.ones((CHUNK, CHUNK), dtype=w_s.dtype))
    w = w_s * mask[None]
    mixed = (jnp.einsum('gts,bcsgd->bctgd', w, vc)
             + jnp.transpose(b_s)[None, None, :, :, None])
    return u * mixed.reshape(B, S, SG_WIDTH)


def _fwd_setup_inputs(seed: int = 0) -> dict:
    key = jax.random.key(seed)
    ks = jax.random.split(key, 24)
    n = jax.random.normal
    f32 = jnp.float32

    def gain(k, shape):
        return 1.0 + 0.05 * n(k, shape, f32)

    x = n(ks[0], (BATCH, SEQ, D_MODEL), f32)
    p = n(ks[1], (DEPTH, BATCH, SEQ, PLE_DIM), f32)
    w_in = n(ks[2], (DEPTH, D_MODEL, IN_PROJ_WIDTH), f32) * D_MODEL ** -0.5
    f_bias = (jnp.linspace(1.0, 5.0, N_ATT_HEADS, dtype=f32)[None, :]
              + 0.1 * n(ks[3], (DEPTH, N_ATT_HEADS), f32))
    sg_ln_g = gain(ks[4], (DEPTH, SG_WIDTH))
    sg_ln_b = 0.02 * n(ks[5], (DEPTH, SG_WIDTH), f32)
    sg_w = n(ks[6], (DEPTH, N_SG_GROUPS, CHUNK, CHUNK), f32) * CHUNK ** -0.5
    sg_b = 1.0 + 0.02 * n(ks[7], (DEPTH, N_SG_GROUPS, CHUNK), f32)
    att_out_g = gain(ks[8], (DEPTH, ATT_WIDTH))
    sg_out_g = gain(ks[9], (DEPTH, SG_WIDTH))
    w_out = n(ks[10], (DEPTH, MIX_WIDTH, D_MODEL), f32) * MIX_WIDTH ** -0.5
    pre_mix_g = gain(ks[11], (DEPTH, D_MODEL))
    post_mix_g = gain(ks[12], (DEPTH, D_MODEL))
    pre_ffn_g = gain(ks[13], (DEPTH, D_MODEL))
    post_ffn_g = gain(ks[14], (DEPTH, D_MODEL))
    w_ff1 = n(ks[15], (DEPTH, D_MODEL, D_FF), f32) * D_MODEL ** -0.5
    w_ff2 = n(ks[16], (DEPTH, D_FF, D_MODEL), f32) * D_FF ** -0.5
    ple_w = n(ks[17], (DEPTH, PLE_DIM, D_MODEL), f32) * PLE_DIM ** -0.5
    ple_gate_w = n(ks[18], (DEPTH, D_MODEL, D_MODEL), f32) * D_MODEL ** -0.5
    ple_gate_b = 0.02 * n(ks[19], (DEPTH, D_MODEL), f32)
    return {"x": x, "p": p, "w_in": w_in, "f_bias": f_bias,
            "sg_ln_g": sg_ln_g, "sg_ln_b": sg_ln_b, "sg_w": sg_w, "sg_b": sg_b,
            "att_out_g": att_out_g, "sg_out_g": sg_out_g, "w_out": w_out,
            "pre_mix_g": pre_mix_g, "post_mix_g": post_mix_g,
            "pre_ffn_g": pre_ffn_g, "post_ffn_g": post_ffn_g,
            "w_ff1": w_ff1, "w_ff2": w_ff2,
            "ple_w": ple_w, "ple_gate_w": ple_gate_w, "ple_gate_b": ple_gate_b}


def _fwd_reference(x, p, w_in, f_bias, sg_ln_g, sg_ln_b, sg_w, sg_b, att_out_g, sg_out_g,
              w_out, pre_mix_g, post_mix_g, pre_ffn_g, post_ffn_g, w_ff1, w_ff2,
              ple_w, ple_gate_w, ple_gate_b):
    B, S, _ = x.shape
    splits = [ATT_WIDTH, 2 * ATT_WIDTH, 3 * ATT_WIDTH,
              3 * ATT_WIDTH + N_ATT_HEADS, 3 * ATT_WIDTH + N_ATT_HEADS + SG_WIDTH]
    h = x
    for i in range(DEPTH):
        a = rms_norm(h, pre_mix_g[i])
        z = a @ w_in[i]
        q, k, v, f, u_sg, v_sg = jnp.split(z, splits, axis=-1)
        q = q.reshape(B, S, N_ATT_HEADS, HEAD_DIM)
        k = k.reshape(B, S, N_ATT_HEADS, HEAD_DIM)
        v = v.reshape(B, S, N_ATT_HEADS, HEAD_DIM)
        y_att = forgetting_attention(q, k, v, f + f_bias[i])
        y_sg = chunked_spatial_gating(u_sg, v_sg, sg_ln_g[i], sg_ln_b[i],
                                      sg_w[i], sg_b[i])
        y = jnp.concatenate([rms_norm(y_att, att_out_g[i]),
                             rms_norm(y_sg, sg_out_g[i])], axis=-1)
        h = h + rms_norm(y @ w_out[i], post_mix_g[i])
        c = rms_norm(h, pre_ffn_g[i])
        ff = jnp.square(jax.nn.relu(c @ w_ff1[i])) @ w_ff2[i]
        h = h + rms_norm(ff, post_ffn_g[i])
        gate = jax.nn.sigmoid(h @ ple_gate_w[i] + ple_gate_b[i])
        h = h + gate * (p[i] @ ple_w[i])
    return h


import jax as _jax
import jax.numpy as _jnp

TWIN_FORMAT = 'train_step'
FWD_PARAMS = ['x', 'p', 'w_in', 'f_bias', 'sg_ln_g', 'sg_ln_b', 'sg_w', 'sg_b', 'att_out_g', 'sg_out_g', 'w_out', 'pre_mix_g', 'post_mix_g', 'pre_ffn_g', 'post_ffn_g', 'w_ff1', 'w_ff2', 'ple_w', 'ple_gate_w', 'ple_gate_b']
TWIN_WEIGHTS = ['w_in', 'f_bias', 'sg_ln_g', 'sg_ln_b', 'sg_w', 'sg_b', 'att_out_g', 'sg_out_g', 'w_out', 'pre_mix_g', 'post_mix_g', 'pre_ffn_g', 'post_ffn_g', 'w_ff1', 'w_ff2', 'ple_w', 'ple_gate_w', 'ple_gate_b']
TWIN_DIFF_INPUT = 'x'
TWIN_INPUTS = ['x', 'p', 'w_in', 'f_bias', 'sg_ln_g', 'sg_ln_b', 'sg_w', 'sg_b', 'att_out_g', 'sg_out_g', 'w_out', 'pre_mix_g', 'post_mix_g', 'pre_ffn_g', 'post_ffn_g', 'w_ff1', 'w_ff2', 'ple_w', 'ple_gate_w', 'ple_gate_b', 'loss_target', 'm_w_in', 'm_f_bias', 'm_sg_ln_g', 'm_sg_ln_b', 'm_sg_w', 'm_sg_b', 'm_att_out_g', 'm_sg_out_g', 'm_w_out', 'm_pre_mix_g', 'm_post_mix_g', 'm_pre_ffn_g', 'm_post_ffn_g', 'm_w_ff1', 'm_w_ff2', 'm_ple_w', 'm_ple_gate_w', 'm_ple_gate_b', 'v_w_in', 'v_f_bias', 'v_sg_ln_g', 'v_sg_ln_b', 'v_sg_w', 'v_sg_b', 'v_att_out_g', 'v_sg_out_g', 'v_w_out', 'v_pre_mix_g', 'v_post_mix_g', 'v_pre_ffn_g', 'v_post_ffn_g', 'v_w_ff1', 'v_w_ff2', 'v_ple_w', 'v_ple_gate_w', 'v_ple_gate_b']
TWIN_OUTPUTS = ['loss', 'grad_x', 'grad_w_in', 'grad_f_bias', 'grad_sg_ln_g', 'grad_sg_ln_b', 'grad_sg_w', 'grad_sg_b', 'grad_att_out_g', 'grad_sg_out_g', 'grad_w_out', 'grad_pre_mix_g', 'grad_post_mix_g', 'grad_pre_ffn_g', 'grad_post_ffn_g', 'grad_w_ff1', 'grad_w_ff2', 'grad_ple_w', 'grad_ple_gate_w', 'grad_ple_gate_b', 'delta_w_in', 'delta_f_bias', 'delta_sg_ln_g', 'delta_sg_ln_b', 'delta_sg_w', 'delta_sg_b', 'delta_att_out_g', 'delta_sg_out_g', 'delta_w_out', 'delta_pre_mix_g', 'delta_post_mix_g', 'delta_pre_ffn_g', 'delta_post_ffn_g', 'delta_w_ff1', 'delta_w_ff2', 'delta_ple_w', 'delta_ple_gate_w', 'delta_ple_gate_b', 'new_m_w_in', 'new_m_f_bias', 'new_m_sg_ln_g', 'new_m_sg_ln_b', 'new_m_sg_w', 'new_m_sg_b', 'new_m_att_out_g', 'new_m_sg_out_g', 'new_m_w_out', 'new_m_pre_mix_g', 'new_m_post_mix_g', 'new_m_pre_ffn_g', 'new_m_post_ffn_g', 'new_m_w_ff1', 'new_m_w_ff2', 'new_m_ple_w', 'new_m_ple_gate_w', 'new_m_ple_gate_b', 'new_v_w_in', 'new_v_f_bias', 'new_v_sg_ln_g', 'new_v_sg_ln_b', 'new_v_sg_w', 'new_v_sg_b', 'new_v_att_out_g', 'new_v_sg_out_g', 'new_v_w_out', 'new_v_pre_mix_g', 'new_v_post_mix_g', 'new_v_pre_ffn_g', 'new_v_post_ffn_g', 'new_v_w_ff1', 'new_v_w_ff2', 'new_v_ple_w', 'new_v_ple_gate_w', 'new_v_ple_gate_b']
TWIN_LEAF_KINDS = {'loss': 'loss', 'grad_x': 'grad_x', 'grad_w_in': 'grad_w', 'grad_f_bias': 'grad_w', 'grad_sg_ln_g': 'grad_w', 'grad_sg_ln_b': 'grad_w', 'grad_sg_w': 'grad_w', 'grad_sg_b': 'grad_w', 'grad_att_out_g': 'grad_w', 'grad_sg_out_g': 'grad_w', 'grad_w_out': 'grad_w', 'grad_pre_mix_g': 'grad_w', 'grad_post_mix_g': 'grad_w', 'grad_pre_ffn_g': 'grad_w', 'grad_post_ffn_g': 'grad_w', 'grad_w_ff1': 'grad_w', 'grad_w_ff2': 'grad_w', 'grad_ple_w': 'grad_w', 'grad_ple_gate_w': 'grad_w', 'grad_ple_gate_b': 'grad_w', 'delta_w_in': 'delta_w', 'delta_f_bias': 'delta_w', 'delta_sg_ln_g': 'delta_w', 'delta_sg_ln_b': 'delta_w', 'delta_sg_w': 'delta_w', 'delta_sg_b': 'delta_w', 'delta_att_out_g': 'delta_w', 'delta_sg_out_g': 'delta_w', 'delta_w_out': 'delta_w', 'delta_pre_mix_g': 'delta_w', 'delta_post_mix_g': 'delta_w', 'delta_pre_ffn_g': 'delta_w', 'delta_post_ffn_g': 'delta_w', 'delta_w_ff1': 'delta_w', 'delta_w_ff2': 'delta_w', 'delta_ple_w': 'delta_w', 'delta_ple_gate_w': 'delta_w', 'delta_ple_gate_b': 'delta_w', 'new_m_w_in': 'new_m', 'new_m_f_bias': 'new_m', 'new_m_sg_ln_g': 'new_m', 'new_m_sg_ln_b': 'new_m', 'new_m_sg_w': 'new_m', 'new_m_sg_b': 'new_m', 'new_m_att_out_g': 'new_m', 'new_m_sg_out_g': 'new_m', 'new_m_w_out': 'new_m', 'new_m_pre_mix_g': 'new_m', 'new_m_post_mix_g': 'new_m', 'new_m_pre_ffn_g': 'new_m', 'new_m_post_ffn_g': 'new_m', 'new_m_w_ff1': 'new_m', 'new_m_w_ff2': 'new_m', 'new_m_ple_w': 'new_m', 'new_m_ple_gate_w': 'new_m', 'new_m_ple_gate_b': 'new_m', 'new_v_w_in': 'new_v', 'new_v_f_bias': 'new_v', 'new_v_sg_ln_g': 'new_v', 'new_v_sg_ln_b': 'new_v', 'new_v_sg_w': 'new_v', 'new_v_sg_b': 'new_v', 'new_v_att_out_g': 'new_v', 'new_v_sg_out_g': 'new_v', 'new_v_w_out': 'new_v', 'new_v_pre_mix_g': 'new_v', 'new_v_post_mix_g': 'new_v', 'new_v_pre_ffn_g': 'new_v', 'new_v_post_ffn_g': 'new_v', 'new_v_w_ff1': 'new_v', 'new_v_w_ff2': 'new_v', 'new_v_ple_w': 'new_v', 'new_v_ple_gate_w': 'new_v', 'new_v_ple_gate_b': 'new_v'}


def _forward(args):
    return _fwd_reference(*[args[k] for k in FWD_PARAMS])


def _output_shape():
    def fwd():
        inp = _fwd_setup_inputs(0)
        return _fwd_reference(*[inp[k] for k in FWD_PARAMS])
    out = _jax.eval_shape(fwd)
    return out.shape, out.dtype

N_MICROBATCH = 1
ADAM_LR = 0.001
ADAM_B1 = 0.9
ADAM_B2 = 0.999
ADAM_EPS = 1e-08
ADAM_WD = 0.01
ADAM_STEP = 10
PER_EXAMPLE_BATCH_AXIS = {'x': 0, 'p': 1, 'loss_target': 0}
SHARED_INPUTS = []
_WEIGHT_DTYPES = {'w_in': _jnp.float32, 'f_bias': _jnp.float32, 'sg_ln_g': _jnp.float32, 'sg_ln_b': _jnp.float32, 'sg_w': _jnp.float32, 'sg_b': _jnp.float32, 'att_out_g': _jnp.float32, 'sg_out_g': _jnp.float32, 'w_out': _jnp.float32, 'pre_mix_g': _jnp.float32, 'post_mix_g': _jnp.float32, 'pre_ffn_g': _jnp.float32, 'post_ffn_g': _jnp.float32, 'w_ff1': _jnp.float32, 'w_ff2': _jnp.float32, 'ple_w': _jnp.float32, 'ple_gate_w': _jnp.float32, 'ple_gate_b': _jnp.float32}
MOMENT_SCALE = {'w_in': 7.688897e-01, 'f_bias': 3.136410e+00, 'sg_ln_g': 4.464527e-01, 'sg_ln_b': 5.332497e-01, 'sg_w': 2.455234e-01, 'sg_b': 5.040169e-01, 'att_out_g': 1.969372e+00, 'sg_out_g': 1.027345e+01, 'w_out': 7.103927e+00, 'pre_mix_g': 1.269906e+00, 'post_mix_g': 6.588852e+01, 'pre_ffn_g': 2.575019e+00, 'post_ffn_g': 6.721574e+01, 'w_ff1': 1.235269e+00, 'w_ff2': 7.038766e+00, 'ple_w': 1.025286e+00, 'ple_gate_w': 2.455522e+00, 'ple_gate_b': 5.303998e+00}


def _to_microbatches(a, axis):
    t = _jnp.moveaxis(a, axis, 0)
    t = t.reshape((N_MICROBATCH, t.shape[0] // N_MICROBATCH) + t.shape[1:])
    return _jnp.moveaxis(t, 1, axis + 1)


def setup_inputs(seed: int = 0) -> dict:
    inp = _fwd_setup_inputs(seed)
    key = _jax.random.fold_in(_jax.random.key(seed), 7919)
    shape, _ = _output_shape()
    out = dict(inp)
    out["loss_target"] = _jax.random.normal(_jax.random.fold_in(key, 0), shape, _jnp.float32)
    for i, name in enumerate(TWIN_WEIGHTS):
        w = inp[name].astype(_jnp.float32)
        if MOMENT_SCALE is None:
            s = _jnp.sqrt(_jnp.mean(_jnp.square(w)) + 1e-30)
        else:
            s = MOMENT_SCALE[name]
        km, kv = _jax.random.split(_jax.random.fold_in(key, i + 1))
        out[name] = w
        out["m_" + name] = s * _jax.random.normal(km, w.shape, _jnp.float32)
        out["v_" + name] = (s * s) * _jax.random.uniform(kv, w.shape, _jnp.float32, 0.5, 1.5)
    if N_MICROBATCH > 1:
        for name, axis in PER_EXAMPLE_BATCH_AXIS.items():
            out[name] = _to_microbatches(out[name], axis)
    return {'x': out['x'], 'p': out['p'], 'w_in': out['w_in'], 'f_bias': out['f_bias'], 'sg_ln_g': out['sg_ln_g'], 'sg_ln_b': out['sg_ln_b'], 'sg_w': out['sg_w'], 'sg_b': out['sg_b'], 'att_out_g': out['att_out_g'], 'sg_out_g': out['sg_out_g'], 'w_out': out['w_out'], 'pre_mix_g': out['pre_mix_g'], 'post_mix_g': out['post_mix_g'], 'pre_ffn_g': out['pre_ffn_g'], 'post_ffn_g': out['post_ffn_g'], 'w_ff1': out['w_ff1'], 'w_ff2': out['w_ff2'], 'ple_w': out['ple_w'], 'ple_gate_w': out['ple_gate_w'], 'ple_gate_b': out['ple_gate_b'], 'loss_target': out['loss_target'], 'm_w_in': out['m_w_in'], 'm_f_bias': out['m_f_bias'], 'm_sg_ln_g': out['m_sg_ln_g'], 'm_sg_ln_b': out['m_sg_ln_b'], 'm_sg_w': out['m_sg_w'], 'm_sg_b': out['m_sg_b'], 'm_att_out_g': out['m_att_out_g'], 'm_sg_out_g': out['m_sg_out_g'], 'm_w_out': out['m_w_out'], 'm_pre_mix_g': out['m_pre_mix_g'], 'm_post_mix_g': out['m_post_mix_g'], 'm_pre_ffn_g': out['m_pre_ffn_g'], 'm_post_ffn_g': out['m_post_ffn_g'], 'm_w_ff1': out['m_w_ff1'], 'm_w_ff2': out['m_w_ff2'], 'm_ple_w': out['m_ple_w'], 'm_ple_gate_w': out['m_ple_gate_w'], 'm_ple_gate_b': out['m_ple_gate_b'], 'v_w_in': out['v_w_in'], 'v_f_bias': out['v_f_bias'], 'v_sg_ln_g': out['v_sg_ln_g'], 'v_sg_ln_b': out['v_sg_ln_b'], 'v_sg_w': out['v_sg_w'], 'v_sg_b': out['v_sg_b'], 'v_att_out_g': out['v_att_out_g'], 'v_sg_out_g': out['v_sg_out_g'], 'v_w_out': out['v_w_out'], 'v_pre_mix_g': out['v_pre_mix_g'], 'v_post_mix_g': out['v_post_mix_g'], 'v_pre_ffn_g': out['v_pre_ffn_g'], 'v_post_ffn_g': out['v_post_ffn_g'], 'v_w_ff1': out['v_w_ff1'], 'v_w_ff2': out['v_w_ff2'], 'v_ple_w': out['v_ple_w'], 'v_ple_gate_w': out['v_ple_gate_w'], 'v_ple_gate_b': out['v_ple_gate_b']}


def _loss(weights, diff, rest, loss_target):
    with _jax.named_scope("forward"):
        args = {**rest, TWIN_DIFF_INPUT: diff, **{k: w.astype(_WEIGHT_DTYPES[k]) for k, w in weights.items()}}
        y = _forward(args)
    with _jax.named_scope("loss_head"):
        err = _jnp.square(y.astype(_jnp.float32) - loss_target)
        return 0.5 * _jnp.sum(_jnp.mean(err, axis=-1)) if err.ndim else 0.5 * err


def _adamw(w, g, m, v):
    m = ADAM_B1 * m + (1.0 - ADAM_B1) * g
    v = ADAM_B2 * v + (1.0 - ADAM_B2) * _jnp.square(g)
    m_hat = m / (1.0 - ADAM_B1 ** ADAM_STEP)
    v_hat = v / (1.0 - ADAM_B2 ** ADAM_STEP)
    delta = -ADAM_LR * (m_hat / (_jnp.sqrt(v_hat) + ADAM_EPS) + ADAM_WD * w)
    return delta, m, v


def reference(x, p, w_in, f_bias, sg_ln_g, sg_ln_b, sg_w, sg_b, att_out_g, sg_out_g, w_out, pre_mix_g, post_mix_g, pre_ffn_g, post_ffn_g, w_ff1, w_ff2, ple_w, ple_gate_w, ple_gate_b, loss_target, m_w_in, m_f_bias, m_sg_ln_g, m_sg_ln_b, m_sg_w, m_sg_b, m_att_out_g, m_sg_out_g, m_w_out, m_pre_mix_g, m_post_mix_g, m_pre_ffn_g, m_post_ffn_g, m_w_ff1, m_w_ff2, m_ple_w, m_ple_gate_w, m_ple_gate_b, v_w_in, v_f_bias, v_sg_ln_g, v_sg_ln_b, v_sg_w, v_sg_b, v_att_out_g, v_sg_out_g, v_w_out, v_pre_mix_g, v_post_mix_g, v_pre_ffn_g, v_post_ffn_g, v_w_ff1, v_w_ff2, v_ple_w, v_ple_gate_w, v_ple_gate_b):
    given = dict(x=x, p=p, w_in=w_in, f_bias=f_bias, sg_ln_g=sg_ln_g, sg_ln_b=sg_ln_b, sg_w=sg_w, sg_b=sg_b, att_out_g=att_out_g, sg_out_g=sg_out_g, w_out=w_out, pre_mix_g=pre_mix_g, post_mix_g=post_mix_g, pre_ffn_g=pre_ffn_g, post_ffn_g=post_ffn_g, w_ff1=w_ff1, w_ff2=w_ff2, ple_w=ple_w, ple_gate_w=ple_gate_w, ple_gate_b=ple_gate_b, loss_target=loss_target, m_w_in=m_w_in, m_f_bias=m_f_bias, m_sg_ln_g=m_sg_ln_g, m_sg_ln_b=m_sg_ln_b, m_sg_w=m_sg_w, m_sg_b=m_sg_b, m_att_out_g=m_att_out_g, m_sg_out_g=m_sg_out_g, m_w_out=m_w_out, m_pre_mix_g=m_pre_mix_g, m_post_mix_g=m_post_mix_g, m_pre_ffn_g=m_pre_ffn_g, m_post_ffn_g=m_post_ffn_g, m_w_ff1=m_w_ff1, m_w_ff2=m_w_ff2, m_ple_w=m_ple_w, m_ple_gate_w=m_ple_gate_w, m_ple_gate_b=m_ple_gate_b, v_w_in=v_w_in, v_f_bias=v_f_bias, v_sg_ln_g=v_sg_ln_g, v_sg_ln_b=v_sg_ln_b, v_sg_w=v_sg_w, v_sg_b=v_sg_b, v_att_out_g=v_att_out_g, v_sg_out_g=v_sg_out_g, v_w_out=v_w_out, v_pre_mix_g=v_pre_mix_g, v_post_mix_g=v_post_mix_g, v_pre_ffn_g=v_pre_ffn_g, v_post_ffn_g=v_post_ffn_g, v_w_ff1=v_w_ff1, v_w_ff2=v_w_ff2, v_ple_w=v_ple_w, v_ple_gate_w=v_ple_gate_w, v_ple_gate_b=v_ple_gate_b)
    weights = {n: given[n] for n in TWIN_WEIGHTS}
    shared = {n: given[n] for n in SHARED_INPUTS}
    per_example = {n: given[n] for n in ['x', 'p']}
    grad_fn = _jax.value_and_grad(_loss, argnums=(0, 1))

    def one_microbatch(ex, loss_target):
        ex = dict(ex)
        diff = ex.pop(TWIN_DIFF_INPUT)
        return grad_fn(weights, diff, {**shared, **ex}, loss_target)

    if N_MICROBATCH == 1:
        loss, (grad_w, grad_x) = one_microbatch(per_example, given["loss_target"])
    else:
        def body(carry, xs):
            loss_sum, grad_sum = carry
            l_k, (gw_k, gx_k) = one_microbatch(xs[0], xs[1])
            with _jax.named_scope("update"):
                return (loss_sum + l_k, _jax.tree.map(_jnp.add, grad_sum, gw_k)), gx_k

        init = (_jnp.zeros((), _jnp.float32), _jax.tree.map(_jnp.zeros_like, weights))
        (loss, grad_w), grad_x = _jax.lax.scan(body, init, (per_example, given["loss_target"]))
    with _jax.named_scope("update"):
        delta_w, new_m, new_v = {}, {}, {}
        for n in TWIN_WEIGHTS:
            delta_w[n], new_m[n], new_v[n] = _adamw(weights[n], grad_w[n], given["m_" + n], given["v_" + n])
    return (loss, grad_x, *[grad_w[n] for n in TWIN_WEIGHTS], *[delta_w[n] for n in TWIN_WEIGHTS],
            *[new_m[n] for n in TWIN_WEIGHTS], *[new_v[n] for n in TWIN_WEIGHTS])
```

```python
import functools
import math

import jax
import jax.numpy as jnp
from jax import lax
from jax.experimental import pallas as pl
from jax.experimental.pallas import tpu as pltpu

F32 = jnp.float32
BF16 = jnp.bfloat16
MESH = pl.DeviceIdType.MESH

D = 1024
DH = 64
NH = 8
AW = 512
CH = 128
DFF = 4096
PLE = 256
ZW = 5 * AW + 128
EPS = 1e-6
NEG = -1e30

TM = 256
TQ = 256

LR, B1, B2, AEPS, WD, STEP = 0.001, 0.9, 0.999, 1e-08, 0.01, 10
BC1 = 1.0 - B1 ** STEP
BC2 = 1.0 - B2 ** STEP

SHARD_ROWS = 768 + 256 + 1024 + 1024 + 64 + 256
HALF_ROWS = SHARD_ROWS // 2
RED_ROWS = HALF_ROWS // 4
SMALL_ROWS = 144
VEC_NAMES = ("f_bias", "sg_ln_g", "sg_ln_b", "sg_b", "att_out_g", "sg_out_g", "pre_mix_g",
             "post_mix_g", "pre_ffn_g", "post_ffn_g", "ple_gate_b")
LOSS_ROW = 128 + len(VEC_NAMES)


def _dot(a, b):
    return jnp.dot(a, b, preferred_element_type=F32)


def _dot_nt(a, b):
    return lax.dot_general(a, b, (((1,), (1,)), ((), ())), preferred_element_type=F32)


def _dot_tn(a, b):
    return lax.dot_general(a, b, (((0,), (0,)), ((), ())), preferred_element_type=F32)


def _split3(x):
    h = x.astype(BF16)
    r = x - h.astype(F32)
    m = r.astype(BF16)
    l = (r - m.astype(F32)).astype(BF16)
    return h, m, l


def _dot01(sel, x):
    h, m, l = _split3(x)
    return _dot(sel, h) + _dot(sel, m) + _dot(sel, l)


def _dot01_r(x, sel):
    h, m, l = _split3(x)
    return _dot(h, sel) + _dot(m, sel) + _dot(l, sel)


def _rs(x, n):
    return lax.rsqrt(jnp.sum(x * x, axis=-1, keepdims=True) * (1.0 / n) + EPS)


def _rms_bwd(dn, x, rs, g, n):
    w = dn * g
    dx = rs * w - x * ((rs * rs * rs) * (1.0 / n) * jnp.sum(w * x, axis=-1, keepdims=True))
    return dx, jnp.sum(dn * x * rs, axis=0, keepdims=True)


_GC = math.sqrt(2.0 / math.pi)


def _gelu(x):
    t = jnp.tanh(_GC * (x + 0.044715 * x * x * x))
    return 0.5 * x * (1.0 + t), t


def _gelu_grad(x, t):
    return 0.5 * (1.0 + t) + 0.5 * x * (1.0 - t * t) * (_GC * (1.0 + 3.0 * 0.044715 * x * x))


def _params(vmem_mb, sem=("arbitrary",)):
    return pltpu.CompilerParams(dimension_semantics=sem, vmem_limit_bytes=vmem_mb * 1024 * 1024)


def _row_call(name, body, T, tm, tiled, resident, outs, accs, scratch=(), reverse=False, vmem_mb=48):
    nt = T // tm
    n_t, n_r, n_o, n_a, n_s = len(tiled), len(resident), len(outs), len(accs), len(scratch)

    def kern(*refs):
        t_refs = refs[:n_t]
        r_hbm = refs[n_t:n_t + n_r]
        o_refs = refs[n_t + n_r:n_t + n_r + n_o]
        a_refs = refs[n_t + n_r + n_o:n_t + n_r + n_o + n_a]
        r_vmem = refs[n_t + n_r + n_o + n_a:n_t + 2 * n_r + n_o + n_a]
        s_refs = refs[n_t + 2 * n_r + n_o + n_a:]

        @pl.when(pl.program_id(0) == 0)
        def _():
            for h, v in zip(r_hbm, r_vmem):
                pltpu.sync_copy(h, v)
            for a in a_refs + s_refs:
                a[...] = jnp.zeros(a.shape, a.dtype)

        body(t_refs, r_vmem, o_refs, a_refs, s_refs)

    if reverse:
        idx = lambda i: (nt - 1 - i, 0)
    else:
        idx = lambda i: (i, 0)
    in_specs = [pl.BlockSpec((tm, a.shape[1]), idx) for a in tiled]
    in_specs += [pl.BlockSpec(memory_space=pl.ANY) for _ in resident]
    out_shape = [jax.ShapeDtypeStruct((T, w), dt) for w, dt in outs]
    out_shape += [jax.ShapeDtypeStruct(s, F32) for s in accs]
    out_specs = [pl.BlockSpec((tm, w), idx) for w, _ in outs]
    out_specs += [pl.BlockSpec(s, lambda i, n=len(s): (0,) * n) for s in accs]
    scratch_shapes = [pltpu.VMEM(r.shape, r.dtype) for r in resident]
    scratch_shapes += [pltpu.VMEM(s, F32) for s in scratch]
    return pl.pallas_call(
        kern, name=name, grid=(nt,), in_specs=in_specs, out_specs=out_specs, out_shape=out_shape,
        scratch_shapes=scratch_shapes, compiler_params=_params(vmem_mb),
    )(*tiled, *resident)


def _sg_forward(zu, zv, wm_ref, bsg, lng, lnb, mixed_ref, tm):
    gu, tu = _gelu(zu)
    vg, tv = _gelu(zv)
    mu = jnp.sum(vg, axis=-1, keepdims=True) * (1.0 / AW)
    xc = vg - mu
    rstd = lax.rsqrt(jnp.sum(xc * xc, axis=-1, keepdims=True) * (1.0 / AW) + EPS)
    xhat = xc * rstd
    vvb = (xhat * lng + lnb).astype(BF16)
    lane = lax.broadcasted_iota(jnp.int32, (CH, 128), 1)
    for c in range(tm // CH):
        for j in range(4):
            blk = vvb[c * CH:(c + 1) * CH, j * 128:(j + 1) * 128]
            m0 = _dot(wm_ref[2 * j], blk)
            m1 = _dot(wm_ref[2 * j + 1], blk)
            mixed_ref[c * CH:(c + 1) * CH, j * 128:(j + 1) * 128] = (
                jnp.where(lane < DH, m0, m1) + bsg[:, j * 128:(j + 1) * 128])
    return gu, tu, tv, xhat, rstd, vvb, mixed_ref[...]


def _masked_sg_w(sg_w):
    r = lax.broadcasted_iota(jnp.int32, (CH, CH), 0)
    c = lax.broadcasted_iota(jnp.int32, (CH, CH), 1)
    return jnp.where((c <= r)[None], sg_w, 0.0)


def _pre_attn_fwd(x, gpre, win, fbias, lng, lnb, wm, bsg, gsg):
    T = x.shape[0]
    tm = TM

    def body(t, r, o, a, s):
        (x_ref,) = t
        gpre_r, win_r, fb_r, lng_r, lnb_r, wm_r, bsg_r, gsg_r = r
        a_o, qkv_o, flog_o, ccol_o, zuv_o, ysgn_o = o
        carry_ref, mixed_ref = s
        xv = x_ref[...]
        av = (xv * _rs(xv, D) * gpre_r[...]).astype(BF16)
        a_o[...] = av
        z = _dot(av, win_r[...])
        qkv_o[:, 0:AW] = (z[:, 0:AW] * (DH ** -0.5)).astype(BF16)
        qkv_o[:, AW:3 * AW] = z[:, AW:3 * AW].astype(BF16)
        zu = z[:, 3 * AW:4 * AW]
        zv = z[:, 4 * AW:5 * AW]
        zuv_o[:, 0:AW] = zu
        zuv_o[:, AW:2 * AW] = zv
        zf = z[:, 5 * AW:] + fb_r[...]
        flog_o[...] = zf
        lane = lax.broadcasted_iota(jnp.int32, (tm, 128), 1)
        logf = jnp.where(lane < NH, jnp.minimum(zf, 0.0) - jnp.log(1.0 + jnp.exp(-jnp.abs(zf))), 0.0)
        rr = lax.broadcasted_iota(jnp.int32, (tm, tm), 0)
        cc = lax.broadcasted_iota(jnp.int32, (tm, tm), 1)
        tri = (cc <= rr).astype(BF16)
        cum = _dot01(tri, logf) + carry_ref[...]
        ccol_o[...] = cum
        carry_ref[...] = cum[tm - 1:tm, :]
        gu, _, _, _, _, _, mixed = _sg_forward(zu, zv, wm_r, bsg_r[...], lng_r[...], lnb_r[...], mixed_ref, tm)
        ysg = gu * mixed
        ysgn_o[...] = (ysg * _rs(ysg, AW) * gsg_r[...]).astype(BF16)

    return _row_call(
        "pre_attn_fwd", body, T, tm, [x], [gpre, win, fbias, lng, lnb, wm, bsg, gsg],
        [(D, BF16), (3 * AW, BF16), (128, F32), (128, F32), (2 * AW, F32), (AW, BF16)], [],
        scratch=[(1, 128), (tm, AW)], vmem_mb=48)


def _flash_fwd(qkv, crow):
    T = qkv.shape[0]
    nq = T // TQ

    def body(q_ref, k_ref, v_ref, c_ref, o_ref, l_ref):
        qi = pl.program_id(1)
        lane = lax.broadcasted_iota(jnp.int32, (1, 128), 1)
        lo = lane < DH
        mk = (jnp.where(lo, 1.0, 0.0).astype(BF16), jnp.where(lo, 0.0, 1.0).astype(BF16))
        q = q_ref[...]
        qm = (q * mk[0], q * mk[1])

        def step(kb, carry, diag):
            m, l, acc = carry
            off = pl.multiple_of(kb * TQ, TQ)
            k = k_ref[pl.ds(off, TQ), :]
            v = v_ref[pl.ds(off, TQ), :]
            m_new, l_new, alphas, pv = [], [], [], None
            for j in (0, 1):
                u = _dot_nt(qm[j], k) - c_ref[j:j + 1, pl.ds(off, TQ)]
                if diag:
                    row = lax.broadcasted_iota(jnp.int32, (TQ, TQ), 0)
                    col = lax.broadcasted_iota(jnp.int32, (TQ, TQ), 1)
                    u = jnp.where(col <= row, u, NEG)
                mj = jnp.maximum(m[j], jnp.max(u, axis=-1, keepdims=True))
                alpha = jnp.exp(m[j] - mj)
                p = jnp.exp(u - mj)
                m_new.append(mj)
                l_new.append(alpha * l[j] + jnp.sum(p, axis=-1, keepdims=True))
                alphas.append(alpha)
                d = _dot(p.astype(BF16), v * mk[j])
                pv = d if pv is None else pv + d
            acc = jnp.where(lo, alphas[0], alphas[1]) * acc + pv
            return (tuple(m_new), tuple(l_new), acc)

        init = ((jnp.full((TQ, 1), NEG, F32),) * 2, (jnp.zeros((TQ, 1), F32),) * 2,
                jnp.zeros((TQ, 128), F32))
        carry = lax.fori_loop(0, qi, lambda kb, c: step(kb, c, False), init)
        m, l, acc = step(qi, carry, True)
        o_ref[...] = acc * jnp.where(lo, 1.0 / l[0], 1.0 / l[1])
        l_ref[...] = jnp.where(lo, m[0] + jnp.log(l[0]), m[1] + jnp.log(l[1]))

    return pl.pallas_call(
        body, name="flash_fwd", grid=(4, nq),
        in_specs=[pl.BlockSpec((TQ, 128), lambda h, i: (i, h)),
                  pl.BlockSpec((T, 128), lambda h, i: (0, 4 + h)),
                  pl.BlockSpec((T, 128), lambda h, i: (0, 8 + h)),
                  pl.BlockSpec((None, 8, T), lambda h, i: (h, 0, 0))],
        out_specs=[pl.BlockSpec((TQ, 128), lambda h, i: (i, h)),
                   pl.BlockSpec((TQ, 128), lambda h, i: (i, h))],
        out_shape=[jax.ShapeDtypeStruct((T, AW), F32), jax.ShapeDtypeStruct((T, AW), F32)],
        compiler_params=_params(40, ("arbitrary", "arbitrary")),
    )(qkv, qkv, qkv, crow)


def _flash_bwd(qkv, dyatt, lse, dlt, crow):
    T = qkv.shape[0]
    nk = T // TQ

    def body(q_ref, k_ref, v_ref, do_ref, l_ref, d_ref, c_ref, dq_ref, dk_ref, dv_ref, dc_ref, dcr_ref):
        kb = pl.program_id(1)

        @pl.when(kb == 0)
        def _():
            dq_ref[...] = jnp.zeros(dq_ref.shape, F32)
            dcr_ref[...] = jnp.zeros(dcr_ref.shape, F32)

        lane = lax.broadcasted_iota(jnp.int32, (1, 128), 1)
        lo = lane < DH
        mk = (jnp.where(lo, 1.0, 0.0).astype(BF16), jnp.where(lo, 0.0, 1.0).astype(BF16))
        k = k_ref[...]
        v = v_ref[...]
        km = (k * mk[0], k * mk[1])
        vm = (v * mk[0], v * mk[1])
        cb = c_ref[...]

        def step(i, carry, diag):
            dk, dv, dc = carry
            off = pl.multiple_of(i * TQ, TQ)
            qv = q_ref[pl.ds(off, TQ), :]
            dov = do_ref[pl.ds(off, TQ), :]
            lv = l_ref[pl.ds(off, TQ), :]
            dl = d_ref[pl.ds(off, TQ), :]
            dq = None
            dk_n, dv_n, dc_n, rsum = [], [], [], []
            for j in (0, 1):
                u = _dot_nt(qv, km[j]) - cb[j:j + 1, :]
                p = jnp.exp(u - lv[:, j * DH:j * DH + 1])
                if diag:
                    row = lax.broadcasted_iota(jnp.int32, (TQ, TQ), 0)
                    col = lax.broadcasted_iota(jnp.int32, (TQ, TQ), 1)
                    p = jnp.where(col <= row, p, 0.0)
                dp = _dot_nt(dov, vm[j])
                ds = p * (dp - dl[:, j * DH:j * DH + 1])
                pb = p.astype(BF16)
                dsb = ds.astype(BF16)
                dv_n.append(dv[j] + _dot_tn(pb, dov))
                dk_n.append(dk[j] + _dot_tn(dsb, qv))
                dc_n.append(dc[j] - jnp.sum(ds, axis=0, keepdims=True))
                rsum.append(jnp.sum(ds, axis=-1, keepdims=True))
                d = _dot(dsb, km[j])
                dq = d if dq is None else dq + d
            dq_ref[pl.ds(off, TQ), :] += dq
            dcr_ref[pl.ds(off, TQ), :] += jnp.where(lo, rsum[0], rsum[1])
            return (tuple(dk_n), tuple(dv_n), tuple(dc_n))

        z = jnp.zeros((TQ, 128), F32)
        zc = jnp.zeros((1, TQ), F32)
        carry = step(kb, ((z, z), (z, z), (zc, zc)), True)
        dk, dv, dc = lax.fori_loop(kb + 1, nk, lambda i, c: step(i, c, False), carry)
        dk_ref[...] = jnp.where(lo, dk[0], dk[1])
        dv_ref[...] = jnp.where(lo, dv[0], dv[1])
        dc_ref[...] = jnp.concatenate([dc[0], dc[1], jnp.zeros((6, TQ), F32)], axis=0)

    return pl.pallas_call(
        body, name="flash_bwd", grid=(4, nk),
        in_specs=[pl.BlockSpec((T, 128), lambda h, i: (0, h)),
                  pl.BlockSpec((TQ, 128), lambda h, i: (i, 4 + h)),
                  pl.BlockSpec((TQ, 128), lambda h, i: (i, 8 + h)),
                  pl.BlockSpec((T, 128), lambda h, i: (0, h)),
                  pl.BlockSpec((T, 128), lambda h, i: (0, h)),
                  pl.BlockSpec((T, 128), lambda h, i: (0, h)),
                  pl.BlockSpec((None, 8, TQ), lambda h, i: (h, 0, i))],
        out_specs=[pl.BlockSpec((T, 128), lambda h, i: (0, h)),
                   pl.BlockSpec((TQ, 128), lambda h, i: (i, h)),
                   pl.BlockSpec((TQ, 128), lambda h, i: (i, h)),
                   pl.BlockSpec((None, 8, TQ), lambda h, i: (h, 0, i)),
                   pl.BlockSpec((T, 128), lambda h, i: (0, h))],
        out_shape=[jax.ShapeDtypeStruct((T, AW), F32), jax.ShapeDtypeStruct((T, AW), F32),
                   jax.ShapeDtypeStruct((T, AW), F32), jax.ShapeDtypeStruct((4, 8, T), F32),
                   jax.ShapeDtypeStruct((T, AW), F32)],
        compiler_params=_params(56, ("arbitrary", "arbitrary")),
    )(qkv, qkv, qkv, dyatt, lse, dlt, crow)


def _tail_fwd1(x, yatt, ysgn, gatt, wout, gpm, gpf, w1):
    T = x.shape[0]

    def body(t, r, o, a, s):
        x_ref, ya_ref, ys_ref = t
        gatt_r, wout_r, gpm_r, gpf_r, w1_r = r
        y_o, o_o, h1_o, c2_o, s_o, rr_o = o
        ya = ya_ref[...]
        yan = (ya * _rs(ya, AW) * gatt_r[...]).astype(BF16)
        y_o[:, 0:AW] = yan
        y_o[:, AW:] = ys_ref[...]
        ov = _dot(yan, wout_r[0:AW, :]) + _dot(ys_ref[...], wout_r[AW:, :])
        o_o[...] = ov
        h1 = x_ref[...] + ov * _rs(ov, D) * gpm_r[...]
        h1_o[...] = h1
        c2 = (h1 * _rs(h1, D) * gpf_r[...]).astype(BF16)
        c2_o[...] = c2
        rr = jnp.maximum(_dot(c2, w1_r[...]), 0.0)
        rr_o[...] = rr.astype(BF16)
        s_o[...] = (rr * rr).astype(BF16)

    return _row_call(
        "tail_fwd1", body, T, TM, [x, yatt, ysgn], [gatt, wout, gpm, gpf, w1],
        [(D, BF16), (D, F32), (D, F32), (D, BF16), (DFF, BF16), (DFF, BF16)], [], vmem_mb=48)


def _tail_fwd2(sact, h1, p, tgt, w2, gpff, wg, bg, wpe):
    T = h1.shape[0]

    def body(t, r, o, a, s):
        s_ref, h1_ref, p_ref, t_ref = t
        w2_r, gpff_r, wg_r, bg_r, wpe_r = r
        ff_o, h2b_o, de_o, dpre_o, dh2_o = o
        loss_a, dbg_a = a
        ff = _dot(s_ref[...], w2_r[...])
        ff_o[...] = ff
        h2 = h1_ref[...] + ff * _rs(ff, D) * gpff_r[...]
        h2b = h2.astype(BF16)
        h2b_o[...] = h2b
        gate = 1.0 / (1.0 + jnp.exp(-(_dot(h2b, wg_r[...]) + bg_r[...])))
        e = _dot(p_ref[...].astype(BF16), wpe_r[...])
        diff = h2 + gate * e - t_ref[...]
        loss_a[...] += jnp.sum(diff * diff, axis=0, keepdims=True)
        dh3 = diff * (1.0 / D)
        de_o[...] = (dh3 * gate).astype(BF16)
        dpre = dh3 * e * gate * (1.0 - gate)
        dbg_a[...] += jnp.sum(dpre, axis=0, keepdims=True)
        dpb = dpre.astype(BF16)
        dpre_o[...] = dpb
        dh2_o[...] = dh3 + _dot_nt(dpb, wg_r[...])

    return _row_call(
        "tail_fwd2", body, T, TM, [sact, h1, p, tgt], [w2, gpff, wg, bg, wpe],
        [(D, F32), (D, BF16), (D, BF16), (D, BF16), (D, F32)], [(1, D), (1, D)], vmem_mb=48)


def _tail_bwd(dh2, ff, rr, h1, ov, yatt, w2, w1, wout, gpff, gpf, gpm, gatt, hsel):
    T = dh2.shape[0]

    def body(t, r, o, a, s):
        dh2_ref, ff_ref, rr_ref, h1_ref, o_ref, ya_ref = t
        w2_r, w1_r, wout_r, gpff_r, gpf_r, gpm_r, gatt_r, hsel_r = r
        dff_o, dr_o, do_o, dya_o, dlt_o, dysg_o, dh1_o = o
        dgpff_a, dgpf_a, dgpm_a, dgatt_a = a
        dh2v = dh2_ref[...]
        ffv = ff_ref[...]
        dff, dg = _rms_bwd(dh2v, ffv, _rs(ffv, D), gpff_r[...], D)
        dgpff_a[...] += dg
        dffb = dff.astype(BF16)
        dff_o[...] = dffb
        drb = (_dot_nt(dffb, w2_r[...]) * (2.0 * rr_ref[...].astype(F32))).astype(BF16)
        dr_o[...] = drb
        dc2 = _dot_nt(drb, w1_r[...])
        h1v = h1_ref[...]
        d1, dg = _rms_bwd(dc2, h1v, _rs(h1v, D), gpf_r[...], D)
        dgpf_a[...] += dg
        dh1 = dh2v + d1
        dh1_o[...] = dh1
        ovv = o_ref[...]
        dov, dg = _rms_bwd(dh1, ovv, _rs(ovv, D), gpm_r[...], D)
        dgpm_a[...] += dg
        dob = dov.astype(BF16)
        do_o[...] = dob
        dysg_o[...] = _dot_nt(dob, wout_r[AW:, :])
        dyan = _dot_nt(dob, wout_r[0:AW, :])
        ya = ya_ref[...]
        dya, dg = _rms_bwd(dyan, ya, _rs(ya, AW), gatt_r[...], AW)
        dgatt_a[...] += dg
        dya_o[...] = dya.astype(BF16)
        dlt_o[...] = _dot01_r(dya * ya, hsel_r[...])

    return _row_call(
        "tail_bwd", body, T, TM, [dh2, ff, rr, h1, ov, yatt],
        [w2, w1, wout, gpff, gpf, gpm, gatt, hsel],
        [(D, BF16), (DFF, BF16), (D, BF16), (AW, BF16), (AW, F32), (AW, F32), (D, F32)],
        [(1, D), (1, D), (1, D), (1, AW)], vmem_mb=56)


def _pre_attn_bwd(x, dh1, dq, dk, dv, dccol, flog, zuv, dysg, gpre, win, lng, lnb, wm, wmt, bsg, gsg, gsel):
    T = x.shape[0]
    tm = TM

    def body(t, r, o, a, s):
        x_ref, dh1_ref, dq_ref, dk_ref, dv_ref, dc_ref, fl_ref, zuv_ref, dys_ref = t
        gpre_r, win_r, lng_r, lnb_r, wm_r, wmt_r, bsg_r, gsg_r, gsel_r = r
        dx_o, dz_o = o
        dgpre_a, dfb_a, dgsg_a, dlng_a, dlnb_a, dws_a, dbs_a, dsb_a = a
        carry_ref, mixed_ref, dvv_ref = s
        dcv = dc_ref[...]
        rr = lax.broadcasted_iota(jnp.int32, (tm, tm), 0)
        cc = lax.broadcasted_iota(jnp.int32, (tm, tm), 1)
        triu = (cc >= rr).astype(BF16)
        dlogf = _dot01(triu, dcv) + carry_ref[...]
        carry_ref[...] = dlogf[0:1, :]
        dzf = dlogf * (1.0 / (1.0 + jnp.exp(fl_ref[...])))
        dfb_a[...] += jnp.sum(dzf, axis=0, keepdims=True)
        dz_o[:, 5 * AW:] = dzf.astype(BF16)
        zu = zuv_ref[:, 0:AW]
        zv = zuv_ref[:, AW:]
        gu, tu, tv, xhat, rstd, vvb, mixed = _sg_forward(
            zu, zv, wm_r, bsg_r[...], lng_r[...], lnb_r[...], mixed_ref, tm)
        ysg = gu * mixed
        dysg_n = dys_ref[...]
        dys, dg = _rms_bwd(dysg_n, ysg, _rs(ysg, AW), gsg_r[...], AW)
        dgsg_a[...] += dg
        dgu = dys * mixed
        dmix = dys * gu
        dmb = dmix.astype(BF16)
        lane = lax.broadcasted_iota(jnp.int32, (CH, 128), 1)
        lo = lane < DH
        for c in range(tm // CH):
            rows = slice(c * CH, (c + 1) * CH)
            dbs_a[...] += dmix[rows, :]
            for j in range(4):
                cols = slice(j * 128, (j + 1) * 128)
                dmblk = dmb[rows, cols]
                vblk = vvb[rows, cols]
                d0 = _dot(wmt_r[2 * j], dmblk)
                d1 = _dot(wmt_r[2 * j + 1], dmblk)
                dvv_ref[rows, cols] = jnp.where(lo, d0, d1)
                dws_a[2 * j] += _dot_nt(jnp.where(lo, dmblk, jnp.zeros_like(dmblk)), vblk)
                dws_a[2 * j + 1] += _dot_nt(jnp.where(lo, jnp.zeros_like(dmblk), dmblk), vblk)
        dvv = dvv_ref[...]
        dlng_a[...] += jnp.sum(dvv * xhat, axis=0, keepdims=True)
        dlnb_a[...] += jnp.sum(dvv, axis=0, keepdims=True)
        dxh = dvv * lng_r[...]
        dvg = rstd * (dxh - jnp.sum(dxh, axis=-1, keepdims=True) * (1.0 / AW)
                      - xhat * (jnp.sum(dxh * xhat, axis=-1, keepdims=True) * (1.0 / AW)))
        dz_o[:, 3 * AW:4 * AW] = (dgu * _gelu_grad(zu, tu)).astype(BF16)
        dz_o[:, 4 * AW:5 * AW] = (dvg * _gelu_grad(zv, tv)).astype(BF16)
        dz_o[:, 0:AW] = (dq_ref[...] * (DH ** -0.5)).astype(BF16)
        dz_o[:, AW:2 * AW] = dk_ref[...].astype(BF16)
        dz_o[:, 2 * AW:3 * AW] = dv_ref[...].astype(BF16)
        da = _dot_nt(dz_o[...], win_r[...])
        xv = x_ref[...]
        dxa, dg = _rms_bwd(da, xv, _rs(xv, D), gpre_r[...], D)
        dgpre_a[...] += dg
        dx_o[...] = dh1_ref[...] + dxa

        @pl.when(pl.program_id(0) == T // tm - 1)
        def _():
            dsb_a[...] = _dot01_r(dbs_a[...], gsel_r[...])

    outs = _row_call(
        "pre_attn_bwd", body, T, tm, [x, dh1, dq, dk, dv, dccol, flog, zuv, dysg],
        [gpre, win, lng, lnb, wm, wmt, bsg, gsg, gsel],
        [(D, F32), (ZW, BF16)],
        [(1, D), (1, 128), (1, AW), (1, AW), (1, AW), (8, CH, CH), (CH, AW), (CH, 128)],
        scratch=[(1, 128), (tm, AW), (tm, AW)], reverse=True, vmem_mb=48)
    return outs


def _matmul_tn(name, a, b, tn=512, tt=512):
    T, K = a.shape
    N = b.shape[1]
    tk = min(K, 1024)
    tn = min(tn, N)
    tt = min(tt, T)

    def body(a_ref, b_ref, o_ref):
        @pl.when(pl.program_id(2) == 0)
        def _():
            o_ref[...] = jnp.zeros(o_ref.shape, F32)

        o_ref[...] += _dot_tn(a_ref[...].astype(BF16), b_ref[...].astype(BF16))

    return pl.pallas_call(
        body, name=name, grid=(K // tk, N // tn, T // tt),
        in_specs=[pl.BlockSpec((tt, tk), lambda i, j, t: (t, i)),
                  pl.BlockSpec((tt, tn), lambda i, j, t: (t, j))],
        out_specs=pl.BlockSpec((tk, tn), lambda i, j, t: (i, j)),
        out_shape=jax.ShapeDtypeStruct((K, N), F32),
        compiler_params=_params(32, ("arbitrary", "arbitrary", "arbitrary")),
    )(a, b)


def _me():
    return lax.axis_index("x"), lax.axis_index("y"), lax.axis_index("c")


HBM_SPEC = pl.BlockSpec(memory_space=pltpu.HBM)


def _gather_weights(mine):
    def body(mine_ref, out_ref, send_sems, recv_sems, local_sem):
        x, y, c = _me()
        k_me = 2 * x + y
        chips = [(1 - x, y), (x, 1 - y), (1 - x, 1 - y)]
        own = pltpu.make_async_copy(mine_ref, out_ref.at[k_me], local_sem)
        own.start()
        sends = [pltpu.make_async_remote_copy(
            src_ref=mine_ref, dst_ref=out_ref.at[k_me], send_sem=send_sems.at[j], recv_sem=recv_sems.at[j],
            device_id=(cx, cy, c), device_id_type=MESH) for j, (cx, cy) in enumerate(chips)]
        for s in sends:
            s.start()
        for j, (cx, cy) in enumerate(chips):
            pltpu.make_async_remote_copy(
                src_ref=mine_ref, dst_ref=out_ref.at[2 * cx + cy], send_sem=send_sems.at[j],
                recv_sem=recv_sems.at[j], device_id=(cx, cy, c), device_id_type=MESH).wait_recv()
        for s in sends:
            s.wait_send()
        own.wait()

    return pl.pallas_call(
        body, name="gather_weights", in_specs=[HBM_SPEC], out_specs=HBM_SPEC,
        out_shape=jax.ShapeDtypeStruct((4,) + mine.shape, mine.dtype),
        scratch_shapes=[pltpu.SemaphoreType.DMA((3,)), pltpu.SemaphoreType.DMA((3,)), pltpu.SemaphoreType.DMA],
    )(mine)


def _swap_halves(g):
    def body(g_ref, got_ref, send_sem, recv_sem):
        x, y, c = _me()
        theirs = pl.multiple_of((1 - c) * HALF_ROWS, 8)
        cp = pltpu.make_async_remote_copy(
            src_ref=g_ref.at[:, pl.ds(theirs, HALF_ROWS), :], dst_ref=got_ref, send_sem=send_sem,
            recv_sem=recv_sem, device_id=(x, y, 1 - c), device_id_type=MESH)
        cp.start()
        cp.wait()

    return pl.pallas_call(
        body, name="swap_halves", in_specs=[HBM_SPEC], out_specs=HBM_SPEC,
        out_shape=jax.ShapeDtypeStruct((4, HALF_ROWS, 1024), F32),
        scratch_shapes=[pltpu.SemaphoreType.DMA, pltpu.SemaphoreType.DMA],
    )(g)


def _pair_sum(mine_half, got):
    def body(a_ref, b_ref, o_ref):
        o_ref[...] = a_ref[...] + b_ref[...]

    spec = pl.BlockSpec((1, RED_ROWS, 1024), lambda k, i: (k, i, 0))
    return pl.pallas_call(
        body, name="pair_sum", grid=(4, HALF_ROWS // RED_ROWS), in_specs=[spec, spec], out_specs=spec,
        out_shape=jax.ShapeDtypeStruct(got.shape, F32),
        compiler_params=_params(32, ("arbitrary", "arbitrary")),
    )(mine_half, got)


def _exchange_chips(ps):
    def body(ps_ref, out_ref, send_sems, recv_sems, local_sem):
        x, y, c = _me()
        k_me = 2 * x + y
        chips = [(1 - x, y), (x, 1 - y), (1 - x, 1 - y)]
        own = pltpu.make_async_copy(ps_ref.at[k_me], out_ref.at[k_me], local_sem)
        own.start()
        sends = [pltpu.make_async_remote_copy(
            src_ref=ps_ref.at[2 * cx + cy], dst_ref=out_ref.at[k_me], send_sem=send_sems.at[j],
            recv_sem=recv_sems.at[j], device_id=(cx, cy, c), device_id_type=MESH)
            for j, (cx, cy) in enumerate(chips)]
        for s in sends:
            s.start()
        for j, (cx, cy) in enumerate(chips):
            pltpu.make_async_remote_copy(
                src_ref=ps_ref.at[k_me], dst_ref=out_ref.at[2 * cx + cy], send_sem=send_sems.at[j],
                recv_sem=recv_sems.at[j], device_id=(cx, cy, c), device_id_type=MESH).wait_recv()
        for s in sends:
            s.wait_send()
        own.wait()

    return pl.pallas_call(
        body, name="exchange_chips", in_specs=[HBM_SPEC], out_specs=HBM_SPEC,
        out_shape=jax.ShapeDtypeStruct(ps.shape, F32),
        scratch_shapes=[pltpu.SemaphoreType.DMA((3,)), pltpu.SemaphoreType.DMA((3,)), pltpu.SemaphoreType.DMA],
    )(ps)


def _adamw(w, g, m, v):
    m = B1 * m + (1.0 - B1) * g
    v = B2 * v + (1.0 - B2) * (g * g)
    delta = -LR * ((m / BC1) / (jnp.sqrt(v / BC2) + AEPS) + WD * w)
    return delta, m, v


def _reduce_update(parts, w, m, v):
    def body(p_ref, w_ref, m_ref, v_ref, o_ref):
        g = ((p_ref[0] + p_ref[1]) + p_ref[2]) + p_ref[3]
        delta, mn, vn = _adamw(w_ref[...], g, m_ref[...], v_ref[...])
        o_ref[0] = g
        o_ref[1] = delta
        o_ref[2] = mn
        o_ref[3] = vn

    s4 = pl.BlockSpec((4, RED_ROWS, 1024), lambda i: (0, i, 0))
    s1 = pl.BlockSpec((RED_ROWS, 1024), lambda i: (i, 0))
    return pl.pallas_call(
        body, name="reduce_update", grid=(HALF_ROWS // RED_ROWS,), in_specs=[s4, s1, s1, s1], out_specs=s4,
        out_shape=jax.ShapeDtypeStruct(parts.shape, F32), compiler_params=_params(48),
    )(parts, w, m, v)


def _share_results(res):
    def body(res_ref, out_ref, send_sem, recv_sem, local_sem):
        x, y, c = _me()
        mine = pl.multiple_of(c * HALF_ROWS, 8)
        theirs = pl.multiple_of((1 - c) * HALF_ROWS, 8)
        own = pltpu.make_async_copy(res_ref, out_ref.at[:, pl.ds(mine, HALF_ROWS), :], local_sem)
        own.start()
        cp = pltpu.make_async_remote_copy(
            src_ref=res_ref, dst_ref=out_ref.at[:, pl.ds(mine, HALF_ROWS), :], send_sem=send_sem,
            recv_sem=recv_sem, device_id=(x, y, 1 - c), device_id_type=MESH)
        cp.start()
        pltpu.make_async_remote_copy(
            src_ref=res_ref, dst_ref=out_ref.at[:, pl.ds(theirs, HALF_ROWS), :], send_sem=send_sem,
            recv_sem=recv_sem, device_id=(x, y, 1 - c), device_id_type=MESH).wait_recv()
        cp.wait_send()
        own.wait()

    return pl.pallas_call(
        body, name="share_results", in_specs=[HBM_SPEC], out_specs=HBM_SPEC,
        out_shape=jax.ShapeDtypeStruct((4, SHARD_ROWS, 1024), F32),
        scratch_shapes=[pltpu.SemaphoreType.DMA, pltpu.SemaphoreType.DMA, pltpu.SemaphoreType.DMA],
    )(res)


def _small_allreduce_update(g, w, m, v):
    def body(g_ref, w_ref, m_ref, v_ref, o_ref, buf, send_sems, recv_sems):
        x, y, c = _me()
        me = 4 * x + 2 * y + c
        buf[me] = g_ref[...]
        rels = [(rx, ry, rc) for rx in (0, 1) for ry in (0, 1) for rc in (0, 1)][1:]

        def peer(r):
            return ((x + r[0]) % 2, (y + r[1]) % 2, (c + r[2]) % 2)

        sends = [pltpu.make_async_remote_copy(
            src_ref=g_ref, dst_ref=buf.at[me], send_sem=send_sems.at[j], recv_sem=recv_sems.at[j],
            device_id=peer(r), device_id_type=MESH) for j, r in enumerate(rels)]
        for s in sends:
            s.start()
        for j, r in enumerate(rels):
            px, py, pc = peer(r)
            pltpu.make_async_remote_copy(
                src_ref=g_ref, dst_ref=buf.at[4 * px + 2 * py + pc], send_sem=send_sems.at[j],
                recv_sem=recv_sems.at[j], device_id=peer(r), device_id_type=MESH).wait_recv()
        for s in sends:
            s.wait_send()
        tot = buf[0]
        for d in range(1, 8):
            tot = tot + buf[d]
        delta, mn, vn = _adamw(w_ref[...], tot, m_ref[...], v_ref[...])
        o_ref[0] = tot
        o_ref[1] = delta
        o_ref[2] = mn
        o_ref[3] = vn

    vm = pl.BlockSpec(memory_space=pltpu.VMEM)
    return pl.pallas_call(
        body, name="small_allreduce_update", in_specs=[vm, vm, vm, vm], out_specs=vm,
        out_shape=jax.ShapeDtypeStruct((4, SMALL_ROWS, 1024), F32),
        scratch_shapes=[pltpu.VMEM((8, SMALL_ROWS, 1024), F32), pltpu.SemaphoreType.DMA((7,)),
                        pltpu.SemaphoreType.DMA((7,))],
        compiler_params=pltpu.CompilerParams(vmem_limit_bytes=32 * 1024 * 1024),
    )(g, w, m, v)


def _pack_shard(w_in, w_out, w1, w2, plew, wg):
    return jnp.concatenate([
        jnp.pad(w_in, ((0, 0), (0, 768 - 642))).reshape(768, 1024), w_out, w1, w2,
        plew.reshape(64, 1024), wg], axis=0)


def _unpack_shard(pk):
    r = 0
    out = []
    for rows, shape in ((768, (1024, 768)), (256, (256, 1024)), (1024, (1024, 1024)), (1024, (1024, 1024)),
                        (64, (256, 256)), (256, (256, 1024))):
        out.append(pk[r:r + rows].reshape(shape))
        r += rows
    out[0] = out[0][:, :642]
    return out


def _full_weights(gathered):
    parts = [_unpack_shard(gathered[k]) for k in range(4)]
    w_in = jnp.concatenate([p[0] for p in parts], axis=1)
    w_in = jnp.concatenate([w_in[:, :3 * AW], w_in[:, 3 * AW + NH:], w_in[:, 3 * AW:3 * AW + NH],
                            jnp.zeros((D, 128 - NH), w_in.dtype)], axis=1)
    return (w_in, jnp.concatenate([p[1] for p in parts], axis=0), jnp.concatenate([p[2] for p in parts], axis=1),
            jnp.concatenate([p[3] for p in parts], axis=0), jnp.concatenate([p[4] for p in parts], axis=1),
            jnp.concatenate([p[5] for p in parts], axis=0))


def _pack_grads(dwin_k, dwout, dw1, dw2, dplew, dwg):
    dwin = jnp.concatenate([dwin_k[:, :3 * AW], dwin_k[:, 5 * AW:5 * AW + NH], dwin_k[:, 3 * AW:5 * AW]], axis=1)
    return jnp.stack([
        _pack_shard(dwin[:, 642 * k:642 * (k + 1)], dwout[256 * k:256 * (k + 1)], dw1[:, 1024 * k:1024 * (k + 1)],
                    dw2[1024 * k:1024 * (k + 1)], dplew[:, 256 * k:256 * (k + 1)], dwg[256 * k:256 * (k + 1)])
        for k in range(4)])


def _row1024(v):
    v = v.reshape(-1)
    return jnp.pad(v, (0, 1024 - v.shape[0])).reshape(1, 1024)


def _pack_small(sg_w, vecs, loss_row=None):
    rows = [sg_w.reshape(128, 1024)] + [_row1024(v) for v in vecs]
    rows.append(jnp.zeros((1, 1024), F32) if loss_row is None else loss_row)
    n = sum(r.shape[0] for r in rows)
    rows.append(jnp.zeros((SMALL_ROWS - n, 1024), F32))
    return jnp.concatenate(rows, axis=0)


def _local_step(x, p, tgt, win_k, wout, w1, w2, plew, wg, small):
    T = x.shape[0]
    row = lambda n: small[n].reshape(1, -1)
    fbias = jnp.pad(row("f_bias"), ((0, 0), (0, 128 - NH)))
    wm = _masked_sg_w(small["sg_w"].reshape(8, CH, CH))
    wmb = wm.astype(BF16)
    wmt = jnp.swapaxes(wm, 1, 2).astype(BF16)
    bsg = jnp.repeat(small["sg_b"].reshape(8, CH).T, DH, axis=1)
    ln_g, ln_b, gsg, gatt = row("sg_ln_g"), row("sg_ln_b"), row("sg_out_g"), row("att_out_g")
    gpre, gpm, gpf, gpff, bg = row("pre_mix_g"), row("post_mix_g"), row("pre_ffn_g"), row("post_ffn_g"), row("ple_gate_b")
    hsel = (jnp.arange(AW)[:, None] // DH == jnp.arange(AW)[None, :] // DH).astype(BF16)
    gsel = (jnp.arange(AW)[:, None] // DH == jnp.arange(128)[None, :]).astype(BF16)

    a, qkv, flog, ccol, zuv, ysgn = _pre_attn_fwd(x, gpre, win_k, fbias, ln_g, ln_b, wmb, bsg, gsg)
    crow = jnp.pad(ccol[:, :NH].T.reshape(4, 2, T), ((0, 0), (0, 6), (0, 0)))
    yatt, lse = _flash_fwd(qkv, crow)
    y, ov, h1, c2, sact, rr = _tail_fwd1(x, yatt, ysgn, gatt, wout, gpm, gpf, w1)
    ff, h2b, de, dpre, dh2, loss_l, dbg = _tail_fwd2(sact, h1, p, tgt, w2, gpff, wg, bg, plew)
    dff, dr, do, dya, dlt, dysg, dh1, dgpff, dgpf, dgpm, dgatt = _tail_bwd(
        dh2, ff, rr, h1, ov, yatt, w2, w1, wout, gpff, gpf, gpm, gatt, hsel)
    dq, dk, dv, dc, dcr = _flash_bwd(qkv, dya, lse, dlt, crow)
    dccol = jnp.pad(dc[:, :2, :].reshape(NH, T).T + dcr[:, ::DH], ((0, 0), (0, 128 - NH)))
    dx, dz, dgpre, dfb, dgsg, dlng, dlnb, dws, _, dsbt = _pre_attn_bwd(
        x, dh1, dq, dk, dv, dccol, flog, zuv, dysg, gpre, win_k, ln_g, ln_b, wmb, wmt, bsg, gsg, gsel)

    dwin_k = _matmul_tn("grad_w_in", a, dz, tn=384)
    dwout = _matmul_tn("grad_w_out", y, do)
    dw1 = _matmul_tn("grad_w_ff1", c2, dr)
    dw2 = _matmul_tn("grad_w_ff2", sact, dff)
    dwg = _matmul_tn("grad_ple_gate_w", h2b, dpre)
    dplew = _matmul_tn("grad_ple_w", p, de)

    dsb = dsbt[:, :8].T
    gsmall = {"sg_w": _masked_sg_w(dws), "f_bias": dfb[:, :NH], "sg_ln_g": dlng, "sg_ln_b": dlnb, "sg_b": dsb,
              "att_out_g": dgatt, "sg_out_g": dgsg, "pre_mix_g": dgpre, "post_mix_g": dgpm, "pre_ffn_g": dgpf,
              "post_ffn_g": dgpff, "ple_gate_b": dbg}
    return loss_l, dx, (dwin_k, dwout, dw1, dw2, dplew, dwg), gsmall


def kernel(x, p, w_in, f_bias, sg_ln_g, sg_ln_b, sg_w, sg_b, att_out_g, sg_out_g, w_out, pre_mix_g, post_mix_g, pre_ffn_g, post_ffn_g, w_ff1, w_ff2, ple_w, ple_gate_w, ple_gate_b, loss_target, m_w_in, m_f_bias, m_sg_ln_g, m_sg_ln_b, m_sg_w, m_sg_b, m_att_out_g, m_sg_out_g, m_w_out, m_pre_mix_g, m_post_mix_g, m_pre_ffn_g, m_post_ffn_g, m_w_ff1, m_w_ff2, m_ple_w, m_ple_gate_w, m_ple_gate_b, v_w_in, v_f_bias, v_sg_ln_g, v_sg_ln_b, v_sg_w, v_sg_b, v_att_out_g, v_sg_out_g, v_w_out, v_pre_mix_g, v_post_mix_g, v_pre_ffn_g, v_post_ffn_g, v_w_ff1, v_w_ff2, v_ple_w, v_ple_gate_w, v_ple_gate_b):
    c = lax.axis_index("c")
    big = lambda t: (t[0][0], t[1][0], t[2][0], t[3][0], t[4][0], t[5][0])
    w_big = big((w_in, w_out, w_ff1, w_ff2, ple_w, ple_gate_w))
    m_big = big((m_w_in, m_w_out, m_w_ff1, m_w_ff2, m_ple_w, m_ple_gate_w))
    v_big = big((v_w_in, v_w_out, v_w_ff1, v_w_ff2, v_ple_w, v_ple_gate_w))
    small = {"sg_w": sg_w, "f_bias": f_bias, "sg_ln_g": sg_ln_g, "sg_ln_b": sg_ln_b, "sg_b": sg_b,
             "att_out_g": att_out_g, "sg_out_g": sg_out_g, "pre_mix_g": pre_mix_g, "post_mix_g": post_mix_g,
             "pre_ffn_g": pre_ffn_g, "post_ffn_g": post_ffn_g, "ple_gate_b": ple_gate_b}
    m_small = {"sg_w": m_sg_w, "f_bias": m_f_bias, "sg_ln_g": m_sg_ln_g, "sg_ln_b": m_sg_ln_b, "sg_b": m_sg_b,
               "att_out_g": m_att_out_g, "sg_out_g": m_sg_out_g, "pre_mix_g": m_pre_mix_g,
               "post_mix_g": m_post_mix_g, "pre_ffn_g": m_pre_ffn_g, "post_ffn_g": m_post_ffn_g,
               "ple_gate_b": m_ple_gate_b}
    v_small = {"sg_w": v_sg_w, "f_bias": v_f_bias, "sg_ln_g": v_sg_ln_g, "sg_ln_b": v_sg_ln_b, "sg_b": v_sg_b,
               "att_out_g": v_att_out_g, "sg_out_g": v_sg_out_g, "pre_mix_g": v_pre_mix_g,
               "post_mix_g": v_post_mix_g, "pre_ffn_g": v_pre_ffn_g, "post_ffn_g": v_post_ffn_g,
               "ple_gate_b": v_ple_gate_b}

    w_pk = _pack_shard(*w_big)
    gathered = _gather_weights(w_pk.astype(BF16))
    win_k, wout, w1, w2, plew, wg = _full_weights(gathered)

    loss_l, dx, gbig, gsmall = _local_step(x[0], p[0, 0], loss_target[0], win_k, wout, w1, w2, plew, wg, small)

    gp = _pack_grads(*gbig)
    got = _swap_halves(gp)
    mine_half = lax.dynamic_slice_in_dim(gp, c * HALF_ROWS, HALF_ROWS, axis=1)
    parts = _exchange_chips(_pair_sum(mine_half, got))
    half = lambda t: lax.dynamic_slice_in_dim(t, c * HALF_ROWS, HALF_ROWS, axis=0)
    res = _reduce_update(parts, half(w_pk), half(_pack_shard(*m_big)), half(_pack_shard(*v_big)))
    full = _share_results(res)
    big_out = [_unpack_shard(full[i]) for i in range(4)]

    vec = lambda d: [d[n] for n in VEC_NAMES]
    loss_row = loss_l * (0.5 / D)
    res_s = _small_allreduce_update(
        _pack_small(gsmall["sg_w"], vec(gsmall), loss_row), _pack_small(small["sg_w"], vec(small)),
        _pack_small(m_small["sg_w"], vec(m_small)), _pack_small(v_small["sg_w"], vec(v_small)))
    loss = jnp.sum(res_s[0, LOSS_ROW])

    def small_out(i, name):
        ref = small[name]
        if name == "sg_w":
            return res_s[i, 0:128].reshape(ref.shape)
        r = 128 + VEC_NAMES.index(name)
        return res_s[i, r, :ref.size].reshape(ref.shape)

    order = ["w_in", "f_bias", "sg_ln_g", "sg_ln_b", "sg_w", "sg_b", "att_out_g", "sg_out_g", "w_out",
             "pre_mix_g", "post_mix_g", "pre_ffn_g", "post_ffn_g", "w_ff1", "w_ff2", "ple_w", "ple_gate_w",
             "ple_gate_b"]
    big_idx = {"w_in": 0, "w_out": 1, "w_ff1": 2, "w_ff2": 3, "ple_w": 4, "ple_gate_w": 5}
    outs = [loss, dx[None]]
    for i in range(4):
        for name in order:
            if name in big_idx:
                outs.append(big_out[i][big_idx[name]][None])
            else:
                outs.append(small_out(i, name))
    return tuple(outs)
```

```python
import functools
import math

import jax
import jax.numpy as jnp
from jax import lax
from jax.experimental import pallas as pl
from jax.experimental.pallas import tpu as pltpu

F32 = jnp.float32
BF16 = jnp.bfloat16
MESH = pl.DeviceIdType.MESH

D = 1024
DH = 64
NH = 8
AW = 512
CH = 128
DFF = 4096
PLE = 256
ZW = 5 * AW + 128
EPS = 1e-6
NEG = -1e30

TM = 256
TQ = 256

LR, B1, B2, AEPS, WD, STEP = 0.001, 0.9, 0.999, 1e-08, 0.01, 10
BC1 = 1.0 - B1 ** STEP
BC2 = 1.0 - B2 ** STEP

SHARD_ROWS = 768 + 256 + 1024 + 1024 + 64 + 256
HALF_ROWS = SHARD_ROWS // 2
RED_ROWS = HALF_ROWS // 4
SMALL_ROWS = 144
VEC_NAMES = ("f_bias", "sg_ln_g", "sg_ln_b", "sg_b", "att_out_g", "sg_out_g", "pre_mix_g",
             "post_mix_g", "pre_ffn_g", "post_ffn_g", "ple_gate_b")
LOSS_ROW = 128 + len(VEC_NAMES)


def _dot(a, b):
    return jnp.dot(a, b, preferred_element_type=F32)


def _dot_nt(a, b):
    return lax.dot_general(a, b, (((1,), (1,)), ((), ())), preferred_element_type=F32)


def _dot_tn(a, b):
    return lax.dot_general(a, b, (((0,), (0,)), ((), ())), preferred_element_type=F32)


def _split3(x):
    h = x.astype(BF16)
    r = x - h.astype(F32)
    m = r.astype(BF16)
    l = (r - m.astype(F32)).astype(BF16)
    return h, m, l


def _dot01(sel, x):
    h, m, l = _split3(x)
    return _dot(sel, h) + _dot(sel, m) + _dot(sel, l)


def _dot01_r(x, sel):
    h, m, l = _split3(x)
    return _dot(h, sel) + _dot(m, sel) + _dot(l, sel)


def _rs(x, n):
    return lax.rsqrt(jnp.sum(x * x, axis=-1, keepdims=True) * (1.0 / n) + EPS)


def _rms_bwd(dn, x, rs, g, n):
    w = dn * g
    dx = rs * w - x * ((rs * rs * rs) * (1.0 / n) * jnp.sum(w * x, axis=-1, keepdims=True))
    return dx, jnp.sum(dn * x * rs, axis=0, keepdims=True)


_GC = math.sqrt(2.0 / math.pi)


def _gelu(x):
    t = jnp.tanh(_GC * (x + 0.044715 * x * x * x))
    return 0.5 * x * (1.0 + t), t


def _gelu_grad(x, t):
    return 0.5 * (1.0 + t) + 0.5 * x * (1.0 - t * t) * (_GC * (1.0 + 3.0 * 0.044715 * x * x))


def _params(vmem_mb, sem=("arbitrary",)):
    return pltpu.CompilerParams(dimension_semantics=sem, vmem_limit_bytes=vmem_mb * 1024 * 1024)


def _row_call(name, body, T, tm, tiled, resident, outs, accs, scratch=(), reverse=False, vmem_mb=48):
    nt = T // tm
    n_t, n_r, n_o, n_a, n_s = len(tiled), len(resident), len(outs), len(accs), len(scratch)

    def kern(*refs):
        t_refs = refs[:n_t]
        r_hbm = refs[n_t:n_t + n_r]
        o_refs = refs[n_t + n_r:n_t + n_r + n_o]
        a_refs = refs[n_t + n_r + n_o:n_t + n_r + n_o + n_a]
        r_vmem = refs[n_t + n_r + n_o + n_a:n_t + 2 * n_r + n_o + n_a]
        s_refs = refs[n_t + 2 * n_r + n_o + n_a:]

        @pl.when(pl.program_id(0) == 0)
        def _():
            for h, v in zip(r_hbm, r_vmem):
                pltpu.sync_copy(h, v)
            for a in a_refs + s_refs:
                a[...] = jnp.zeros(a.shape, a.dtype)

        body(t_refs, r_vmem, o_refs, a_refs, s_refs)

    if reverse:
        idx = lambda i: (nt - 1 - i, 0)
    else:
        idx = lambda i: (i, 0)
    in_specs = [pl.BlockSpec((tm, a.shape[1]), idx) for a in tiled]
    in_specs += [pl.BlockSpec(memory_space=pl.ANY) for _ in resident]
    out_shape = [jax.ShapeDtypeStruct((T, w), dt) for w, dt in outs]
    out_shape += [jax.ShapeDtypeStruct(s, F32) for s in accs]
    out_specs = [pl.BlockSpec((tm, w), idx) for w, _ in outs]
    out_specs += [pl.BlockSpec(s, lambda i, n=len(s): (0,) * n) for s in accs]
    scratch_shapes = [pltpu.VMEM(r.shape, r.dtype) for r in resident]
    scratch_shapes += [pltpu.VMEM(s, F32) for s in scratch]
    return pl.pallas_call(
        kern, name=name, grid=(nt,), in_specs=in_specs, out_specs=out_specs, out_shape=out_shape,
        scratch_shapes=scratch_shapes, compiler_params=_params(vmem_mb),
    )(*tiled, *resident)


def _sg_forward(zu, zv, wm_ref, bsg, lng, lnb, mixed_ref, tm):
    gu, tu = _gelu(zu)
    vg, tv = _gelu(zv)
    mu = jnp.sum(vg, axis=-1, keepdims=True) * (1.0 / AW)
    xc = vg - mu
    rstd = lax.rsqrt(jnp.sum(xc * xc, axis=-1, keepdims=True) * (1.0 / AW) + EPS)
    xhat = xc * rstd
    vvb = (xhat * lng + lnb).astype(BF16)
    lane = lax.broadcasted_iota(jnp.int32, (CH, 128), 1)
    for c in range(tm // CH):
        for j in range(4):
            blk = vvb[c * CH:(c + 1) * CH, j * 128:(j + 1) * 128]
            m0 = _dot(wm_ref[2 * j], blk)
            m1 = _dot(wm_ref[2 * j + 1], blk)
            mixed_ref[c * CH:(c + 1) * CH, j * 128:(j + 1) * 128] = (
                jnp.where(lane < DH, m0, m1) + bsg[:, j * 128:(j + 1) * 128])
    return gu, tu, tv, xhat, rstd, vvb, mixed_ref[...]


def _masked_sg_w(sg_w):
    r = lax.broadcasted_iota(jnp.int32, (CH, CH), 0)
    c = lax.broadcasted_iota(jnp.int32, (CH, CH), 1)
    return jnp.where((c <= r)[None], sg_w, 0.0)


def _pre_attn_fwd(x, gpre, win, fbias, lng, lnb, wm, bsg, gsg):
    T = x.shape[0]
    tm = TM

    def body(t, r, o, a, s):
        (x_ref,) = t
        gpre_r, win_r, fb_r, lng_r, lnb_r, wm_r, bsg_r, gsg_r = r
        a_o, qkv_o, flog_o, ccol_o, zuv_o, ysgn_o = o
        carry_ref, mixed_ref = s
        xv = x_ref[...]
        av = (xv * _rs(xv, D) * gpre_r[...]).astype(BF16)
        a_o[...] = av
        z = _dot(av, win_r[...])
        qkv_o[:, 0:AW] = (z[:, 0:AW] * (DH ** -0.5)).astype(BF16)
        qkv_o[:, AW:3 * AW] = z[:, AW:3 * AW].astype(BF16)
        zu = z[:, 3 * AW:4 * AW]
        zv = z[:, 4 * AW:5 * AW]
        zuv_o[:, 0:AW] = zu
        zuv_o[:, AW:2 * AW] = zv
        zf = z[:, 5 * AW:] + fb_r[...]
        flog_o[...] = zf
        lane = lax.broadcasted_iota(jnp.int32, (tm, 128), 1)
        logf = jnp.where(lane < NH, jnp.minimum(zf, 0.0) - jnp.log(1.0 + jnp.exp(-jnp.abs(zf))), 0.0)
        rr = lax.broadcasted_iota(jnp.int32, (tm, tm), 0)
        cc = lax.broadcasted_iota(jnp.int32, (tm, tm), 1)
        tri = (cc <= rr).astype(BF16)
        cum = _dot01(tri, logf) + carry_ref[...]
        ccol_o[...] = cum
        carry_ref[...] = cum[tm - 1:tm, :]
        gu, _, _, _, _, _, mixed = _sg_forward(zu, zv, wm_r, bsg_r[...], lng_r[...], lnb_r[...], mixed_ref, tm)
        ysg = gu * mixed
        ysgn_o[...] = (ysg * _rs(ysg, AW) * gsg_r[...]).astype(BF16)

    return _row_call(
        "pre_attn_fwd", body, T, tm, [x], [gpre, win, fbias, lng, lnb, wm, bsg, gsg],
        [(D, BF16), (3 * AW, BF16), (128, F32), (128, F32), (2 * AW, F32), (AW, BF16)], [],
        scratch=[(1, 128), (tm, AW)], vmem_mb=48)


def _flash_fwd(qkv, crow):
    T = qkv.shape[0]
    nq = T // TQ

    def body(q_ref, k_ref, v_ref, c_ref, o_ref, l_ref):
        qi = pl.program_id(1)
        lane = lax.broadcasted_iota(jnp.int32, (1, 128), 1)
        lo = lane < DH
        mk = (jnp.where(lo, 1.0, 0.0).astype(BF16), jnp.where(lo, 0.0, 1.0).astype(BF16))
        q = q_ref[...]
        qm = (q * mk[0], q * mk[1])

        def step(kb, carry, diag):
            m, l, acc = carry
            off = pl.multiple_of(kb * TQ, TQ)
            k = k_ref[pl.ds(off, TQ), :]
            v = v_ref[pl.ds(off, TQ), :]
            m_new, l_new, alphas, pv = [], [], [], None
            for j in (0, 1):
                u = _dot_nt(qm[j], k) - c_ref[j:j + 1, pl.ds(off, TQ)]
                if diag:
                    row = lax.broadcasted_iota(jnp.int32, (TQ, TQ), 0)
                    col = lax.broadcasted_iota(jnp.int32, (TQ, TQ), 1)
                    u = jnp.where(col <= row, u, NEG)
                mj = jnp.maximum(m[j], jnp.max(u, axis=-1, keepdims=True))
                alpha = jnp.exp(m[j] - mj)
                p = jnp.exp(u - mj)
                m_new.append(mj)
                l_new.append(alpha * l[j] + jnp.sum(p, axis=-1, keepdims=True))
                alphas.append(alpha)
                d = _dot(p.astype(BF16), v * mk[j])
                pv = d if pv is None else pv + d
            acc = jnp.where(lo, alphas[0], alphas[1]) * acc + pv
            return (tuple(m_new), tuple(l_new), acc)

        init = ((jnp.full((TQ, 1), NEG, F32),) * 2, (jnp.zeros((TQ, 1), F32),) * 2,
                jnp.zeros((TQ, 128), F32))
        carry = lax.fori_loop(0, qi, lambda kb, c: step(kb, c, False), init)
        m, l, acc = step(qi, carry, True)
        o_ref[...] = acc * jnp.where(lo, 1.0 / l[0], 1.0 / l[1])
        l_ref[...] = jnp.where(lo, m[0] + jnp.log(l[0]), m[1] + jnp.log(l[1]))

    return pl.pallas_call(
        body, name="flash_fwd", grid=(4, nq),
        in_specs=[pl.BlockSpec((TQ, 128), lambda h, i: (i, h)),
                  pl.BlockSpec((T, 128), lambda h, i: (0, 4 + h)),
                  pl.BlockSpec((T, 128), lambda h, i: (0, 8 + h)),
                  pl.BlockSpec((None, 8, T), lambda h, i: (h, 0, 0))],
        out_specs=[pl.BlockSpec((TQ, 128), lambda h, i: (i, h)),
                   pl.BlockSpec((TQ, 128), lambda h, i: (i, h))],
        out_shape=[jax.ShapeDtypeStruct((T, AW), F32), jax.ShapeDtypeStruct((T, AW), F32)],
        compiler_params=_params(40, ("arbitrary", "arbitrary")),
    )(qkv, qkv, qkv, crow)


def _flash_bwd(qkv, dyatt, lse, dlt, crow):
    T = qkv.shape[0]
    nk = T // TQ

    def body(q_ref, k_ref, v_ref, do_ref, l_ref, d_ref, c_ref, dq_ref, dk_ref, dv_ref, dc_ref, dcr_ref):
        kb = pl.program_id(1)

        @pl.when(kb == 0)
        def _():
            dq_ref[...] = jnp.zeros(dq_ref.shape, F32)
            dcr_ref[...] = jnp.zeros(dcr_ref.shape, F32)

        lane = lax.broadcasted_iota(jnp.int32, (1, 128), 1)
        lo = lane < DH
        mk = (jnp.where(lo, 1.0, 0.0).astype(BF16), jnp.where(lo, 0.0, 1.0).astype(BF16))
        k = k_ref[...]
        v = v_ref[...]
        km = (k * mk[0], k * mk[1])
        vm = (v * mk[0], v * mk[1])
        cb = c_ref[...]

        def step(i, carry, diag):
            dk, dv, dc = carry
            off = pl.multiple_of(i * TQ, TQ)
            qv = q_ref[pl.ds(off, TQ), :]
            dov = do_ref[pl.ds(off, TQ), :]
            lv = l_ref[pl.ds(off, TQ), :]
            dl = d_ref[pl.ds(off, TQ), :]
            dq = None
            dk_n, dv_n, dc_n, rsum = [], [], [], []
            for j in (0, 1):
                u = _dot_nt(qv, km[j]) - cb[j:j + 1, :]
                p = jnp.exp(u - lv[:, j * DH:j * DH + 1])
                if diag:
                    row = lax.broadcasted_iota(jnp.int32, (TQ, TQ), 0)
                    col = lax.broadcasted_iota(jnp.int32, (TQ, TQ), 1)
                    p = jnp.where(col <= row, p, 0.0)
                dp = _dot_nt(dov, vm[j])
                ds = p * (dp - dl[:, j * DH:j * DH + 1])
                pb = p.astype(BF16)
                dsb = ds.astype(BF16)
                dv_n.append(dv[j] + _dot_tn(pb, dov))
                dk_n.append(dk[j] + _dot_tn(dsb, qv))
                dc_n.append(dc[j] - jnp.sum(ds, axis=0, keepdims=True))
                rsum.append(jnp.sum(ds, axis=-1, keepdims=True))
                d = _dot(dsb, km[j])
                dq = d if dq is None else dq + d
            dq_ref[pl.ds(off, TQ), :] += dq
            dcr_ref[pl.ds(off, TQ), :] += jnp.where(lo, rsum[0], rsum[1])
            return (tuple(dk_n), tuple(dv_n), tuple(dc_n))

        z = jnp.zeros((TQ, 128), F32)
        zc = jnp.zeros((1, TQ), F32)
        carry = step(kb, ((z, z), (z, z), (zc, zc)), True)
        dk, dv, dc = lax.fori_loop(kb + 1, nk, lambda i, c: step(i, c, False), carry)
        dk_ref[...] = jnp.where(lo, dk[0], dk[1])
        dv_ref[...] = jnp.where(lo, dv[0], dv[1])
        dc_ref[...] = jnp.concatenate([dc[0], dc[1], jnp.zeros((6, TQ), F32)], axis=0)

    return pl.pallas_call(
        body, name="flash_bwd", grid=(4, nk),
        in_specs=[pl.BlockSpec((T, 128), lambda h, i: (0, h)),
                  pl.BlockSpec((TQ, 128), lambda h, i: (i, 4 + h)),
                  pl.BlockSpec((TQ, 128), lambda h, i: (i, 8 + h)),
                  pl.BlockSpec((T, 128), lambda h, i: (0, h)),
                  pl.BlockSpec((T, 128), lambda h, i: (0, h)),
                  pl.BlockSpec((T, 128), lambda h, i: (0, h)),
                  pl.BlockSpec((None, 8, TQ), lambda h, i: (h, 0, i))],
        out_specs=[pl.BlockSpec((T, 128), lambda h, i: (0, h)),
                   pl.BlockSpec((TQ, 128), lambda h, i: (i, h)),
                   pl.BlockSpec((TQ, 128), lambda h, i: (i, h)),
                   pl.BlockSpec((None, 8, TQ), lambda h, i: (h, 0, i)),
                   pl.BlockSpec((T, 128), lambda h, i: (0, h))],
        out_shape=[jax.ShapeDtypeStruct((T, AW), F32), jax.ShapeDtypeStruct((T, AW), F32),
                   jax.ShapeDtypeStruct((T, AW), F32), jax.ShapeDtypeStruct((4, 8, T), F32),
                   jax.ShapeDtypeStruct((T, AW), F32)],
        compiler_params=_params(56, ("arbitrary", "arbitrary")),
    )(qkv, qkv, qkv, dyatt, lse, dlt, crow)


def _tail_fwd1(x, yatt, ysgn, gatt, wout, gpm, gpf, w1):
    T = x.shape[0]

    def body(t, r, o, a, s):
        x_ref, ya_ref, ys_ref = t
        gatt_r, wout_r, gpm_r, gpf_r, w1_r = r
        y_o, o_o, h1_o, c2_o, s_o, rr_o = o
        ya = ya_ref[...]
        yan = (ya * _rs(ya, AW) * gatt_r[...]).astype(BF16)
        y_o[:, 0:AW] = yan
        y_o[:, AW:] = ys_ref[...]
        ov = _dot(yan, wout_r[0:AW, :]) + _dot(ys_ref[...], wout_r[AW:, :])
        o_o[...] = ov
        h1 = x_ref[...] + ov * _rs(ov, D) * gpm_r[...]
        h1_o[...] = h1
        c2 = (h1 * _rs(h1, D) * gpf_r[...]).astype(BF16)
        c2_o[...] = c2
        rr = jnp.maximum(_dot(c2, w1_r[...]), 0.0)
        rr_o[...] = rr.astype(BF16)
        s_o[...] = (rr * rr).astype(BF16)

    return _row_call(
        "tail_fwd1", body, T, TM, [x, yatt, ysgn], [gatt, wout, gpm, gpf, w1],
        [(D, BF16), (D, F32), (D, F32), (D, BF16), (DFF, BF16), (DFF, BF16)], [], vmem_mb=48)


def _tail_fwd2(sact, h1, p, tgt, w2, gpff, wg, bg, wpe):
    T = h1.shape[0]

    def body(t, r, o, a, s):
        s_ref, h1_ref, p_ref, t_ref = t
        w2_r, gpff_r, wg_r, bg_r, wpe_r = r
        ff_o, h2b_o, de_o, dpre_o, dh2_o = o
        loss_a, dbg_a = a
        ff = _dot(s_ref[...], w2_r[...])
        ff_o[...] = ff
        h2 = h1_ref[...] + ff * _rs(ff, D) * gpff_r[...]
        h2b = h2.astype(BF16)
        h2b_o[...] = h2b
        gate = 1.0 / (1.0 + jnp.exp(-(_dot(h2b, wg_r[...]) + bg_r[...])))
        e = _dot(p_ref[...].astype(BF16), wpe_r[...])
        diff = h2 + gate * e - t_ref[...]
        loss_a[...] += jnp.sum(diff * diff, axis=0, keepdims=True)
        dh3 = diff * (1.0 / D)
        de_o[...] = (dh3 * gate).astype(BF16)
        dpre = dh3 * e * gate * (1.0 - gate)
        dbg_a[...] += jnp.sum(dpre, axis=0, keepdims=True)
        dpb = dpre.astype(BF16)
        dpre_o[...] = dpb
        dh2_o[...] = dh3 + _dot_nt(dpb, wg_r[...])

    return _row_call(
        "tail_fwd2", body, T, TM, [sact, h1, p, tgt], [w2, gpff, wg, bg, wpe],
        [(D, F32), (D, BF16), (D, BF16), (D, BF16), (D, F32)], [(1, D), (1, D)], vmem_mb=48)


def _tail_bwd(dh2, ff, rr, h1, ov, yatt, w2, w1, wout, gpff, gpf, gpm, gatt, hsel):
    T = dh2.shape[0]

    def body(t, r, o, a, s):
        dh2_ref, ff_ref, rr_ref, h1_ref, o_ref, ya_ref = t
        w2_r, w1_r, wout_r, gpff_r, gpf_r, gpm_r, gatt_r, hsel_r = r
        dff_o, dr_o, do_o, dya_o, dlt_o, dysg_o, dh1_o = o
        dgpff_a, dgpf_a, dgpm_a, dgatt_a = a
        dh2v = dh2_ref[...]
        ffv = ff_ref[...]
        dff, dg = _rms_bwd(dh2v, ffv, _rs(ffv, D), gpff_r[...], D)
        dgpff_a[...] += dg
        dffb = dff.astype(BF16)
        dff_o[...] = dffb
        drb = (_dot_nt(dffb, w2_r[...]) * (2.0 * rr_ref[...].astype(F32))).astype(BF16)
        dr_o[...] = drb
        dc2 = _dot_nt(drb, w1_r[...])
        h1v = h1_ref[...]
        d1, dg = _rms_bwd(dc2, h1v, _rs(h1v, D), gpf_r[...], D)
        dgpf_a[...] += dg
        dh1 = dh2v + d1
        dh1_o[...] = dh1
        ovv = o_ref[...]
        dov, dg = _rms_bwd(dh1, ovv, _rs(ovv, D), gpm_r[...], D)
        dgpm_a[...] += dg
        dob = dov.astype(BF16)
        do_o[...] = dob
        dysg_o[...] = _dot_nt(dob, wout_r[AW:, :])
        dyan = _dot_nt(dob, wout_r[0:AW, :])
        ya = ya_ref[...]
        dya, dg = _rms_bwd(dyan, ya, _rs(ya, AW), gatt_r[...], AW)
        dgatt_a[...] += dg
        dya_o[...] = dya.astype(BF16)
        dlt_o[...] = _dot01_r(dya * ya, hsel_r[...])

    return _row_call(
        "tail_bwd", body, T, TM, [dh2, ff, rr, h1, ov, yatt],
        [w2, w1, wout, gpff, gpf, gpm, gatt, hsel],
        [(D, BF16), (DFF, BF16), (D, BF16), (AW, BF16), (AW, F32), (AW, F32), (D, F32)],
        [(1, D), (1, D), (1, D), (1, AW)], vmem_mb=56)


def _pre_attn_bwd(x, dh1, dq, dk, dv, dccol, flog, zuv, dysg, gpre, win, lng, lnb, wm, wmt, bsg, gsg, gsel):
    T = x.shape[0]
    tm = TM

    def body(t, r, o, a, s):
        x_ref, dh1_ref, dq_ref, dk_ref, dv_ref, dc_ref, fl_ref, zuv_ref, dys_ref = t
        gpre_r, win_r, lng_r, lnb_r, wm_r, wmt_r, bsg_r, gsg_r, gsel_r = r
        dx_o, dz_o = o
        dgpre_a, dfb_a, dgsg_a, dlng_a, dlnb_a, dws_a, dbs_a, dsb_a = a
        carry_ref, mixed_ref, dvv_ref = s
        dcv = dc_ref[...]
        rr = lax.broadcasted_iota(jnp.int32, (tm, tm), 0)
        cc = lax.broadcasted_iota(jnp.int32, (tm, tm), 1)
        triu = (cc >= rr).astype(BF16)
        dlogf = _dot01(triu, dcv) + carry_ref[...]
        carry_ref[...] = dlogf[0:1, :]
        dzf = dlogf * (1.0 / (1.0 + jnp.exp(fl_ref[...])))
        dfb_a[...] += jnp.sum(dzf, axis=0, keepdims=True)
        dz_o[:, 5 * AW:] = dzf.astype(BF16)
        zu = zuv_ref[:, 0:AW]
        zv = zuv_ref[:, AW:]
        gu, tu, tv, xhat, rstd, vvb, mixed = _sg_forward(
            zu, zv, wm_r, bsg_r[...], lng_r[...], lnb_r[...], mixed_ref, tm)
        ysg = gu * mixed
        dysg_n = dys_ref[...]
        dys, dg = _rms_bwd(dysg_n, ysg, _rs(ysg, AW), gsg_r[...], AW)
        dgsg_a[...] += dg
        dgu = dys * mixed
        dmix = dys * gu
        dmb = dmix.astype(BF16)
        lane = lax.broadcasted_iota(jnp.int32, (CH, 128), 1)
        lo = lane < DH
        for c in range(tm // CH):
            rows = slice(c * CH, (c + 1) * CH)
            dbs_a[...] += dmix[rows, :]
            for j in range(4):
                cols = slice(j * 128, (j + 1) * 128)
                dmblk = dmb[rows, cols]
                vblk = vvb[rows, cols]
                d0 = _dot(wmt_r[2 * j], dmblk)
                d1 = _dot(wmt_r[2 * j + 1], dmblk)
                dvv_ref[rows, cols] = jnp.where(lo, d0, d1)
                dws_a[2 * j] += _dot_nt(jnp.where(lo, dmblk, jnp.zeros_like(dmblk)), vblk)
                dws_a[2 * j + 1] += _dot_nt(jnp.where(lo, jnp.zeros_like(dmblk), dmblk), vblk)
        dvv = dvv_ref[...]
        dlng_a[...] += jnp.sum(dvv * xhat, axis=0, keepdims=True)
        dlnb_a[...] += jnp.sum(dvv, axis=0, keepdims=True)
        dxh = dvv * lng_r[...]
        dvg = rstd * (dxh - jnp.sum(dxh, axis=-1, keepdims=True) * (1.0 / AW)
                      - xhat * (jnp.sum(dxh * xhat, axis=-1, keepdims=True) * (1.0 / AW)))
        dz_o[:, 3 * AW:4 * AW] = (dgu * _gelu_grad(zu, tu)).astype(BF16)
        dz_o[:, 4 * AW:5 * AW] = (dvg * _gelu_grad(zv, tv)).astype(BF16)
        dz_o[:, 0:AW] = (dq_ref[...] * (DH ** -0.5)).astype(BF16)
        dz_o[:, AW:2 * AW] = dk_ref[...].astype(BF16)
        dz_o[:, 2 * AW:3 * AW] = dv_ref[...].astype(BF16)
        da = _dot_nt(dz_o[...], win_r[...])
        xv = x_ref[...]
        dxa, dg = _rms_bwd(da, xv, _rs(xv, D), gpre_r[...], D)
        dgpre_a[...] += dg
        dx_o[...] = dh1_ref[...] + dxa

        @pl.when(pl.program_id(0) == T // tm - 1)
        def _():
            dsb_a[...] = _dot01_r(dbs_a[...], gsel_r[...])

    outs = _row_call(
        "pre_attn_bwd", body, T, tm, [x, dh1, dq, dk, dv, dccol, flog, zuv, dysg],
        [gpre, win, lng, lnb, wm, wmt, bsg, gsg, gsel],
        [(D, F32), (ZW, BF16)],
        [(1, D), (1, 128), (1, AW), (1, AW), (1, AW), (8, CH, CH), (CH, AW), (CH, 128)],
        scratch=[(1, 128), (tm, AW), (tm, AW)], reverse=True, vmem_mb=48)
    return outs


def _matmul_tn(name, a, b, tn=512, tt=512):
    T, K = a.shape
    N = b.shape[1]
    tk = min(K, 1024)
    tn = min(tn, N)
    tt = min(tt, T)

    def body(a_ref, b_ref, o_ref):
        @pl.when(pl.program_id(2) == 0)
        def _():
            o_ref[...] = jnp.zeros(o_ref.shape, F32)

        o_ref[...] += _dot_tn(a_ref[...].astype(BF16), b_ref[...].astype(BF16))

    return pl.pallas_call(
        body, name=name, grid=(K // tk, N // tn, T // tt),
        in_specs=[pl.BlockSpec((tt, tk), lambda i, j, t: (t, i)),
                  pl.BlockSpec((tt, tn), lambda i, j, t: (t, j))],
        out_specs=pl.BlockSpec((tk, tn), lambda i, j, t: (i, j)),
        out_shape=jax.ShapeDtypeStruct((K, N), F32),
        compiler_params=_params(32, ("arbitrary", "arbitrary", "arbitrary")),
    )(a, b)


def _me():
    return lax.axis_index("x"), lax.axis_index("y"), lax.axis_index("c")


HBM_SPEC = pl.BlockSpec(memory_space=pltpu.HBM)


def _gather_weights(mine):
    def body(mine_ref, out_ref, send_sems, recv_sems, local_sem):
        x, y, c = _me()
        k_me = 2 * x + y
        chips = [(1 - x, y), (x, 1 - y), (1 - x, 1 - y)]
        own = pltpu.make_async_copy(mine_ref, out_ref.at[k_me], local_sem)
        own.start()
        sends = [pltpu.make_async_remote_copy(
            src_ref=mine_ref, dst_ref=out_ref.at[k_me], send_sem=send_sems.at[j], recv_sem=recv_sems.at[j],
            device_id=(cx, cy, c), device_id_type=MESH) for j, (cx, cy) in enumerate(chips)]
        for s in sends:
            s.start()
        for j, (cx, cy) in enumerate(chips):
            pltpu.make_async_remote_copy(
                src_ref=mine_ref, dst_ref=out_ref.at[2 * cx + cy], send_sem=send_sems.at[j],
                recv_sem=recv_sems.at[j], device_id=(cx, cy, c), device_id_type=MESH).wait_recv()
        for s in sends:
            s.wait_send()
        own.wait()

    return pl.pallas_call(
        body, name="gather_weights", in_specs=[HBM_SPEC], out_specs=HBM_SPEC,
        out_shape=jax.ShapeDtypeStruct((4,) + mine.shape, mine.dtype),
        scratch_shapes=[pltpu.SemaphoreType.DMA((3,)), pltpu.SemaphoreType.DMA((3,)), pltpu.SemaphoreType.DMA],
    )(mine)


def _swap_halves(g):
    def body(g_ref, got_ref, send_sem, recv_sem):
        x, y, c = _me()
        theirs = pl.multiple_of((1 - c) * HALF_ROWS, 8)
        cp = pltpu.make_async_remote_copy(
            src_ref=g_ref.at[:, pl.ds(theirs, HALF_ROWS), :], dst_ref=got_ref, send_sem=send_sem,
            recv_sem=recv_sem, device_id=(x, y, 1 - c), device_id_type=MESH)
        cp.start()
        cp.wait()

    return pl.pallas_call(
        body, name="swap_halves", in_specs=[HBM_SPEC], out_specs=HBM_SPEC,
        out_shape=jax.ShapeDtypeStruct((4, HALF_ROWS, 1024), F32),
        scratch_shapes=[pltpu.SemaphoreType.DMA, pltpu.SemaphoreType.DMA],
    )(g)


def _pair_sum(mine_half, got):
    def body(a_ref, b_ref, o_ref):
        o_ref[...] = a_ref[...] + b_ref[...]

    spec = pl.BlockSpec((1, RED_ROWS, 1024), lambda k, i: (k, i, 0))
    return pl.pallas_call(
        body, name="pair_sum", grid=(4, HALF_ROWS // RED_ROWS), in_specs=[spec, spec], out_specs=spec,
        out_shape=jax.ShapeDtypeStruct(got.shape, F32),
        compiler_params=_params(32, ("arbitrary", "arbitrary")),
    )(mine_half, got)


def _exchange_chips(ps):
    def body(ps_ref, out_ref, send_sems, recv_sems, local_sem):
        x, y, c = _me()
        k_me = 2 * x + y
        chips = [(1 - x, y), (x, 1 - y), (1 - x, 1 - y)]
        own = pltpu.make_async_copy(ps_ref.at[k_me], out_ref.at[k_me], local_sem)
        own.start()
        sends = [pltpu.make_async_remote_copy(
            src_ref=ps_ref.at[2 * cx + cy], dst_ref=out_ref.at[k_me], send_sem=send_sems.at[j],
            recv_sem=recv_sems.at[j], device_id=(cx, cy, c), device_id_type=MESH)
            for j, (cx, cy) in enumerate(chips)]
        for s in sends:
            s.start()
        for j, (cx, cy) in enumerate(chips):
            pltpu.make_async_remote_copy(
                src_ref=ps_ref.at[k_me], dst_ref=out_ref.at[2 * cx + cy], send_sem=send_sems.at[j],
                recv_sem=recv_sems.at[j], device_id=(cx, cy, c), device_id_type=MESH).wait_recv()
        for s in sends:
            s.wait_send()
        own.wait()

    return pl.pallas_call(
        body, name="exchange_chips", in_specs=[HBM_SPEC], out_specs=HBM_SPEC,
        out_shape=jax.ShapeDtypeStruct(ps.shape, F32),
        scratch_shapes=[pltpu.SemaphoreType.DMA((3,)), pltpu.SemaphoreType.DMA((3,)), pltpu.SemaphoreType.DMA],
    )(ps)


def _adamw(w, g, m, v):
    m = B1 * m + (1.0 - B1) * g
    v = B2 * v + (1.0 - B2) * (g * g)
    delta = -LR * ((m / BC1) / (jnp.sqrt(v / BC2) + AEPS) + WD * w)
    return delta, m, v


def _reduce_chips(parts):
    def body(p_ref, o_ref):
        o_ref[...] = ((p_ref[0] + p_ref[1]) + p_ref[2]) + p_ref[3]

    return pl.pallas_call(
        body, name="reduce_chips", grid=(HALF_ROWS // RED_ROWS,),
        in_specs=[pl.BlockSpec((4, RED_ROWS, 1024), lambda i: (0, i, 0))],
        out_specs=pl.BlockSpec((RED_ROWS, 1024), lambda i: (i, 0)),
        out_shape=jax.ShapeDtypeStruct((HALF_ROWS, 1024), F32), compiler_params=_params(32),
    )(parts)


def _share_grad(gh):
    def body(g_ref, out_ref, send_sem, recv_sem, local_sem):
        x, y, c = _me()
        mine = pl.multiple_of(c * HALF_ROWS, 8)
        theirs = pl.multiple_of((1 - c) * HALF_ROWS, 8)
        own = pltpu.make_async_copy(g_ref, out_ref.at[pl.ds(mine, HALF_ROWS), :], local_sem)
        own.start()
        cp = pltpu.make_async_remote_copy(
            src_ref=g_ref, dst_ref=out_ref.at[pl.ds(mine, HALF_ROWS), :], send_sem=send_sem,
            recv_sem=recv_sem, device_id=(x, y, 1 - c), device_id_type=MESH)
        cp.start()
        pltpu.make_async_remote_copy(
            src_ref=g_ref, dst_ref=out_ref.at[pl.ds(theirs, HALF_ROWS), :], send_sem=send_sem,
            recv_sem=recv_sem, device_id=(x, y, 1 - c), device_id_type=MESH).wait_recv()
        cp.wait_send()
        own.wait()

    return pl.pallas_call(
        body, name="share_grad", in_specs=[HBM_SPEC], out_specs=HBM_SPEC,
        out_shape=jax.ShapeDtypeStruct((SHARD_ROWS, 1024), F32),
        scratch_shapes=[pltpu.SemaphoreType.DMA, pltpu.SemaphoreType.DMA, pltpu.SemaphoreType.DMA],
    )(gh)


def _update(g, w, m, v):
    def body(g_ref, w_ref, m_ref, v_ref, o_ref):
        delta, mn, vn = _adamw(w_ref[...], g_ref[...], m_ref[...], v_ref[...])
        o_ref[0] = delta
        o_ref[1] = mn
        o_ref[2] = vn

    s1 = pl.BlockSpec((RED_ROWS, 1024), lambda i: (i, 0))
    return pl.pallas_call(
        body, name="update", grid=(SHARD_ROWS // RED_ROWS,), in_specs=[s1, s1, s1, s1],
        out_specs=pl.BlockSpec((3, RED_ROWS, 1024), lambda i: (0, i, 0)),
        out_shape=jax.ShapeDtypeStruct((3, SHARD_ROWS, 1024), F32), compiler_params=_params(32),
    )(g, w, m, v)


def _small_allreduce_update(g, w, m, v):
    def body(g_ref, w_ref, m_ref, v_ref, o_ref, buf, send_sems, recv_sems):
        x, y, c = _me()
        me = 4 * x + 2 * y + c
        buf[me] = g_ref[...]
        rels = [(rx, ry, rc) for rx in (0, 1) for ry in (0, 1) for rc in (0, 1)][1:]

        def peer(r):
            return ((x + r[0]) % 2, (y + r[1]) % 2, (c + r[2]) % 2)

        sends = [pltpu.make_async_remote_copy(
            src_ref=g_ref, dst_ref=buf.at[me], send_sem=send_sems.at[j], recv_sem=recv_sems.at[j],
            device_id=peer(r), device_id_type=MESH) for j, r in enumerate(rels)]
        for s in sends:
            s.start()
        for j, r in enumerate(rels):
            px, py, pc = peer(r)
            pltpu.make_async_remote_copy(
                src_ref=g_ref, dst_ref=buf.at[4 * px + 2 * py + pc], send_sem=send_sems.at[j],
                recv_sem=recv_sems.at[j], device_id=peer(r), device_id_type=MESH).wait_recv()
        for s in sends:
            s.wait_send()
        tot = buf[0]
        for d in range(1, 8):
            tot = tot + buf[d]
        delta, mn, vn = _adamw(w_ref[...], tot, m_ref[...], v_ref[...])
        o_ref[0] = tot
        o_ref[1] = delta
        o_ref[2] = mn
        o_ref[3] = vn

    vm = pl.BlockSpec(memory_space=pltpu.VMEM)
    return pl.pallas_call(
        body, name="small_allreduce_update", in_specs=[vm, vm, vm, vm], out_specs=vm,
        out_shape=jax.ShapeDtypeStruct((4, SMALL_ROWS, 1024), F32),
        scratch_shapes=[pltpu.VMEM((8, SMALL_ROWS, 1024), F32), pltpu.SemaphoreType.DMA((7,)),
                        pltpu.SemaphoreType.DMA((7,))],
        compiler_params=pltpu.CompilerParams(vmem_limit_bytes=32 * 1024 * 1024),
    )(g, w, m, v)


def _pack_shard(w_in, w_out, w1, w2, plew, wg):
    return jnp.concatenate([
        jnp.pad(w_in, ((0, 0), (0, 768 - 642))).reshape(768, 1024), w_out, w1, w2,
        plew.reshape(64, 1024), wg], axis=0)


def _unpack_shard(pk):
    r = 0
    out = []
    for rows, shape in ((768, (1024, 768)), (256, (256, 1024)), (1024, (1024, 1024)), (1024, (1024, 1024)),
                        (64, (256, 256)), (256, (256, 1024))):
        out.append(pk[r:r + rows].reshape(shape))
        r += rows
    out[0] = out[0][:, :642]
    return out


def _full_weights(gathered):
    parts = [_unpack_shard(gathered[k]) for k in range(4)]
    w_in = jnp.concatenate([p[0] for p in parts], axis=1)
    w_in = jnp.concatenate([w_in[:, :3 * AW], w_in[:, 3 * AW + NH:], w_in[:, 3 * AW:3 * AW + NH],
                            jnp.zeros((D, 128 - NH), w_in.dtype)], axis=1)
    return (w_in, jnp.concatenate([p[1] for p in parts], axis=0), jnp.concatenate([p[2] for p in parts], axis=1),
            jnp.concatenate([p[3] for p in parts], axis=0), jnp.concatenate([p[4] for p in parts], axis=1),
            jnp.concatenate([p[5] for p in parts], axis=0))


def _pack_grads(dwin_k, dwout, dw1, dw2, dplew, dwg):
    dwin = jnp.concatenate([dwin_k[:, :3 * AW], dwin_k[:, 5 * AW:5 * AW + NH], dwin_k[:, 3 * AW:5 * AW]], axis=1)
    return jnp.stack([
        _pack_shard(dwin[:, 642 * k:642 * (k + 1)], dwout[256 * k:256 * (k + 1)], dw1[:, 1024 * k:1024 * (k + 1)],
                    dw2[1024 * k:1024 * (k + 1)], dplew[:, 256 * k:256 * (k + 1)], dwg[256 * k:256 * (k + 1)])
        for k in range(4)])


def _row1024(v):
    v = v.reshape(-1)
    return jnp.pad(v, (0, 1024 - v.shape[0])).reshape(1, 1024)


def _pack_small(sg_w, vecs, loss_row=None):
    rows = [sg_w.reshape(128, 1024)] + [_row1024(v) for v in vecs]
    rows.append(jnp.zeros((1, 1024), F32) if loss_row is None else loss_row)
    n = sum(r.shape[0] for r in rows)
    rows.append(jnp.zeros((SMALL_ROWS - n, 1024), F32))
    return jnp.concatenate(rows, axis=0)


def _local_step(x, p, tgt, win_k, wout, w1, w2, plew, wg, small):
    T = x.shape[0]
    row = lambda n: small[n].reshape(1, -1)
    fbias = jnp.pad(row("f_bias"), ((0, 0), (0, 128 - NH)))
    wm = _masked_sg_w(small["sg_w"].reshape(8, CH, CH))
    wmb = wm.astype(BF16)
    wmt = jnp.swapaxes(wm, 1, 2).astype(BF16)
    bsg = jnp.repeat(small["sg_b"].reshape(8, CH).T, DH, axis=1)
    ln_g, ln_b, gsg, gatt = row("sg_ln_g"), row("sg_ln_b"), row("sg_out_g"), row("att_out_g")
    gpre, gpm, gpf, gpff, bg = row("pre_mix_g"), row("post_mix_g"), row("pre_ffn_g"), row("post_ffn_g"), row("ple_gate_b")
    hsel = (jnp.arange(AW)[:, None] // DH == jnp.arange(AW)[None, :] // DH).astype(BF16)
    gsel = (jnp.arange(AW)[:, None] // DH == jnp.arange(128)[None, :]).astype(BF16)

    a, qkv, flog, ccol, zuv, ysgn = _pre_attn_fwd(x, gpre, win_k, fbias, ln_g, ln_b, wmb, bsg, gsg)
    crow = jnp.pad(ccol[:, :NH].T.reshape(4, 2, T), ((0, 0), (0, 6), (0, 0)))
    yatt, lse = _flash_fwd(qkv, crow)
    y, ov, h1, c2, sact, rr = _tail_fwd1(x, yatt, ysgn, gatt, wout, gpm, gpf, w1)
    ff, h2b, de, dpre, dh2, loss_l, dbg = _tail_fwd2(sact, h1, p, tgt, w2, gpff, wg, bg, plew)
    dff, dr, do, dya, dlt, dysg, dh1, dgpff, dgpf, dgpm, dgatt = _tail_bwd(
        dh2, ff, rr, h1, ov, yatt, w2, w1, wout, gpff, gpf, gpm, gatt, hsel)
    dq, dk, dv, dc, dcr = _flash_bwd(qkv, dya, lse, dlt, crow)
    dccol = jnp.pad(dc[:, :2, :].reshape(NH, T).T + dcr[:, ::DH], ((0, 0), (0, 128 - NH)))
    dx, dz, dgpre, dfb, dgsg, dlng, dlnb, dws, _, dsbt = _pre_attn_bwd(
        x, dh1, dq, dk, dv, dccol, flog, zuv, dysg, gpre, win_k, ln_g, ln_b, wmb, wmt, bsg, gsg, gsel)

    dwin_k = _matmul_tn("grad_w_in", a, dz, tn=384)
    dwout = _matmul_tn("grad_w_out", y, do)
    dw1 = _matmul_tn("grad_w_ff1", c2, dr)
    dw2 = _matmul_tn("grad_w_ff2", sact, dff)
    dwg = _matmul_tn("grad_ple_gate_w", h2b, dpre)
    dplew = _matmul_tn("grad_ple_w", p, de)

    dsb = dsbt[:, :8].T
    gsmall = {"sg_w": _masked_sg_w(dws), "f_bias": dfb[:, :NH], "sg_ln_g": dlng, "sg_ln_b": dlnb, "sg_b": dsb,
              "att_out_g": dgatt, "sg_out_g": dgsg, "pre_mix_g": dgpre, "post_mix_g": dgpm, "pre_ffn_g": dgpf,
              "post_ffn_g": dgpff, "ple_gate_b": dbg}
    return loss_l, dx, (dwin_k, dwout, dw1, dw2, dplew, dwg), gsmall


def kernel(x, p, w_in, f_bias, sg_ln_g, sg_ln_b, sg_w, sg_b, att_out_g, sg_out_g, w_out, pre_mix_g, post_mix_g, pre_ffn_g, post_ffn_g, w_ff1, w_ff2, ple_w, ple_gate_w, ple_gate_b, loss_target, m_w_in, m_f_bias, m_sg_ln_g, m_sg_ln_b, m_sg_w, m_sg_b, m_att_out_g, m_sg_out_g, m_w_out, m_pre_mix_g, m_post_mix_g, m_pre_ffn_g, m_post_ffn_g, m_w_ff1, m_w_ff2, m_ple_w, m_ple_gate_w, m_ple_gate_b, v_w_in, v_f_bias, v_sg_ln_g, v_sg_ln_b, v_sg_w, v_sg_b, v_att_out_g, v_sg_out_g, v_w_out, v_pre_mix_g, v_post_mix_g, v_pre_ffn_g, v_post_ffn_g, v_w_ff1, v_w_ff2, v_ple_w, v_ple_gate_w, v_ple_gate_b):
    c = lax.axis_index("c")
    big = lambda t: (t[0][0], t[1][0], t[2][0], t[3][0], t[4][0], t[5][0])
    w_big = big((w_in, w_out, w_ff1, w_ff2, ple_w, ple_gate_w))
    m_big = big((m_w_in, m_w_out, m_w_ff1, m_w_ff2, m_ple_w, m_ple_gate_w))
    v_big = big((v_w_in, v_w_out, v_w_ff1, v_w_ff2, v_ple_w, v_ple_gate_w))
    small = {"sg_w": sg_w, "f_bias": f_bias, "sg_ln_g": sg_ln_g, "sg_ln_b": sg_ln_b, "sg_b": sg_b,
             "att_out_g": att_out_g, "sg_out_g": sg_out_g, "pre_mix_g": pre_mix_g, "post_mix_g": post_mix_g,
             "pre_ffn_g": pre_ffn_g, "post_ffn_g": post_ffn_g, "ple_gate_b": ple_gate_b}
    m_small = {"sg_w": m_sg_w, "f_bias": m_f_bias, "sg_ln_g": m_sg_ln_g, "sg_ln_b": m_sg_ln_b, "sg_b": m_sg_b,
               "att_out_g": m_att_out_g, "sg_out_g": m_sg_out_g, "pre_mix_g": m_pre_mix_g,
               "post_mix_g": m_post_mix_g, "pre_ffn_g": m_pre_ffn_g, "post_ffn_g": m_post_ffn_g,
               "ple_gate_b": m_ple_gate_b}
    v_small = {"sg_w": v_sg_w, "f_bias": v_f_bias, "sg_ln_g": v_sg_ln_g, "sg_ln_b": v_sg_ln_b, "sg_b": v_sg_b,
               "att_out_g": v_att_out_g, "sg_out_g": v_sg_out_g, "pre_mix_g": v_pre_mix_g,
               "post_mix_g": v_post_mix_g, "pre_ffn_g": v_pre_ffn_g, "post_ffn_g": v_post_ffn_g,
               "ple_gate_b": v_ple_gate_b}

    w_pk = _pack_shard(*w_big)
    gathered = _gather_weights(w_pk.astype(BF16))
    win_k, wout, w1, w2, plew, wg = _full_weights(gathered)

    loss_l, dx, gbig, gsmall = _local_step(x[0], p[0, 0], loss_target[0], win_k, wout, w1, w2, plew, wg, small)

    gp = _pack_grads(*gbig)
    got = _swap_halves(gp)
    mine_half = lax.dynamic_slice_in_dim(gp, c * HALF_ROWS, HALF_ROWS, axis=1)
    parts = _exchange_chips(_pair_sum(mine_half, got))
    g_full = _share_grad(_reduce_chips(parts))
    upd = _update(g_full, w_pk, _pack_shard(*m_big), _pack_shard(*v_big))
    big_out = [_unpack_shard(t) for t in (g_full, upd[0], upd[1], upd[2])]

    vec = lambda d: [d[n] for n in VEC_NAMES]
    loss_row = loss_l * (0.5 / D)
    res_s = _small_allreduce_update(
        _pack_small(gsmall["sg_w"], vec(gsmall), loss_row), _pack_small(small["sg_w"], vec(small)),
        _pack_small(m_small["sg_w"], vec(m_small)), _pack_small(v_small["sg_w"], vec(v_small)))
    loss = jnp.sum(res_s[0, LOSS_ROW])

    def small_out(i, name):
        ref = small[name]
        if name == "sg_w":
            return res_s[i, 0:128].reshape(ref.shape)
        r = 128 + VEC_NAMES.index(name)
        return res_s[i, r, :ref.size].reshape(ref.shape)

    order = ["w_in", "f_bias", "sg_ln_g", "sg_ln_b", "sg_w", "sg_b", "att_out_g", "sg_out_g", "w_out",
             "pre_mix_g", "post_mix_g", "pre_ffn_g", "post_ffn_g", "w_ff1", "w_ff2", "ple_w", "ple_gate_w",
             "ple_gate_b"]
    big_idx = {"w_in": 0, "w_out": 1, "w_ff1": 2, "w_ff2": 3, "ple_w": 4, "ple_gate_w": 5}
    outs = [loss, dx[None]]
    for i in range(4):
        for name in order:
            if name in big_idx:
                outs.append(big_out[i][big_idx[name]][None])
            else:
                outs.append(small_out(i, name))
    return tuple(outs)
```

```python
import math

import jax
import jax.numpy as jnp
from jax import lax
from jax.experimental import pallas as pl
from jax.experimental.pallas import tpu as pltpu

F32 = jnp.float32
BF16 = jnp.bfloat16
MESH = pl.DeviceIdType.MESH

D = 1024
DH = 64
NH = 8
AW = 512
CH = 128
DFF = 4096
ZW = 5 * AW + 128
EPS = 1e-6
NEG = -1e30
MASKED = -2e30

TM = 256
TQ = 256

LR, B1, B2, AEPS, WD, STEP = 0.001, 0.9, 0.999, 1e-08, 0.01, 10
BC1 = 1.0 - B1 ** STEP
BC2 = 1.0 - B2 ** STEP

SHARD_ROWS = 768 + 256 + 1024 + 1024 + 64 + 256
HALF_ROWS = SHARD_ROWS // 2
RED_ROWS = HALF_ROWS // 4
SMALL_ROWS = 144
VEC_NAMES = ("f_bias", "sg_ln_g", "sg_ln_b", "sg_b", "att_out_g", "sg_out_g", "pre_mix_g",
             "post_mix_g", "pre_ffn_g", "post_ffn_g", "ple_gate_b")
LOSS_ROW = 128 + len(VEC_NAMES)


def _dot(a, b):
    return jnp.dot(a, b, preferred_element_type=F32)


def _dot_nt(a, b):
    return lax.dot_general(a, b, (((1,), (1,)), ((), ())), preferred_element_type=F32)


def _dot_tn(a, b):
    return lax.dot_general(a, b, (((0,), (0,)), ((), ())), preferred_element_type=F32)


def _split3(x):
    h = x.astype(BF16)
    r = x - h.astype(F32)
    m = r.astype(BF16)
    l = (r - m.astype(F32)).astype(BF16)
    return h, m, l


def _dot01(sel, x):
    h, m, l = _split3(x)
    return _dot(sel, h) + _dot(sel, m) + _dot(sel, l)


def _dot01_r(x, sel):
    h, m, l = _split3(x)
    return _dot(h, sel) + _dot(m, sel) + _dot(l, sel)


def _dot01_tn(x, sel):
    h, m, l = _split3(x)
    return _dot_tn(h, sel) + _dot_tn(m, sel) + _dot_tn(l, sel)


def _rs(x, n):
    return lax.rsqrt(jnp.sum(x * x, axis=-1, keepdims=True) * (1.0 / n) + EPS)


def _rms_bwd(dn, x, rs, g, n):
    w = dn * g
    dx = rs * w - x * ((rs * rs * rs) * (1.0 / n) * jnp.sum(w * x, axis=-1, keepdims=True))
    return dx, jnp.sum(dn * x * rs, axis=0, keepdims=True)


_GC = math.sqrt(2.0 / math.pi)


def _gelu(x):
    t = jnp.tanh(_GC * (x + 0.044715 * x * x * x))
    return 0.5 * x * (1.0 + t), t


def _gelu_grad(x, t):
    return 0.5 * (1.0 + t) + 0.5 * x * (1.0 - t * t) * (_GC * (1.0 + 3.0 * 0.044715 * x * x))


def _params(vmem_mb, sem=("arbitrary",)):
    return pltpu.CompilerParams(dimension_semantics=sem, vmem_limit_bytes=vmem_mb * 1024 * 1024)


def _row_call(name, body, T, tm, tiled, resident, outs, accs, scratch=(), reverse=False, vmem_mb=48):
    nt = T // tm
    n_t, n_r, n_o, n_a = len(tiled), len(resident), len(outs), len(accs)

    def kern(*refs):
        t_refs = refs[:n_t]
        r_hbm = refs[n_t:n_t + n_r]
        o_refs = refs[n_t + n_r:n_t + n_r + n_o]
        a_refs = refs[n_t + n_r + n_o:n_t + n_r + n_o + n_a]
        r_vmem = refs[n_t + n_r + n_o + n_a:n_t + 2 * n_r + n_o + n_a]
        s_refs = refs[n_t + 2 * n_r + n_o + n_a:]

        @pl.when(pl.program_id(0) == 0)
        def _():
            for h, v in zip(r_hbm, r_vmem):
                pltpu.sync_copy(h, v)
            for a in a_refs + s_refs:
                a[...] = jnp.zeros(a.shape, a.dtype)

        body(t_refs, r_vmem, o_refs, a_refs, s_refs)

    if reverse:
        idx = lambda i: (nt - 1 - i, 0)
        idx_t = lambda i: (nt - 1 - i, 0, 0)
    else:
        idx = lambda i: (i, 0)
        idx_t = lambda i: (i, 0, 0)
    arrays, in_specs = [], []
    for a in tiled:
        if isinstance(a, tuple):
            arrays.append(a[0])
            in_specs.append(pl.BlockSpec((None, a[0].shape[1], tm), idx_t))
        else:
            arrays.append(a)
            in_specs.append(pl.BlockSpec((tm, a.shape[1]), idx))
    in_specs += [pl.BlockSpec(memory_space=pl.ANY) for _ in resident]
    out_shape, out_specs = [], []
    for o in outs:
        if len(o) == 3:
            out_shape.append(jax.ShapeDtypeStruct((nt, o[0], tm), o[1]))
            out_specs.append(pl.BlockSpec((None, o[0], tm), idx_t))
        else:
            out_shape.append(jax.ShapeDtypeStruct((T, o[0]), o[1]))
            out_specs.append(pl.BlockSpec((tm, o[0]), idx))
    out_shape += [jax.ShapeDtypeStruct(s, F32) for s in accs]
    out_specs += [pl.BlockSpec(s, lambda i, n=len(s): (0,) * n) for s in accs]
    scratch_shapes = [pltpu.VMEM(r.shape, r.dtype) for r in resident]
    scratch_shapes += [pltpu.VMEM(s, F32) for s in scratch]
    return pl.pallas_call(
        kern, name=name, grid=(nt,), in_specs=in_specs, out_specs=out_specs, out_shape=out_shape,
        scratch_shapes=scratch_shapes, compiler_params=_params(vmem_mb),
    )(*arrays, *resident)


def _sg_forward(zu, zv, wm_ref, bsg, lng, lnb, mixed_ref, tm):
    gu, tu = _gelu(zu)
    vg, tv = _gelu(zv)
    mu = jnp.sum(vg, axis=-1, keepdims=True) * (1.0 / AW)
    xc = vg - mu
    rstd = lax.rsqrt(jnp.sum(xc * xc, axis=-1, keepdims=True) * (1.0 / AW) + EPS)
    xhat = xc * rstd
    vvb = (xhat * lng + lnb).astype(BF16)
    lane = lax.broadcasted_iota(jnp.int32, (CH, 128), 1)
    for c in range(tm // CH):
        for j in range(4):
            blk = vvb[c * CH:(c + 1) * CH, j * 128:(j + 1) * 128]
            m0 = _dot(wm_ref[2 * j], blk)
            m1 = _dot(wm_ref[2 * j + 1], blk)
            mixed_ref[c * CH:(c + 1) * CH, j * 128:(j + 1) * 128] = (
                jnp.where(lane < DH, m0, m1) + bsg[:, j * 128:(j + 1) * 128])
    return gu, tu, tv, xhat, rstd, vvb, mixed_ref[...]


def _head_consts():
    src = jnp.arange(AW)
    dst = (src // DH) * 128 + src % DH
    wide = jnp.arange(NH * 128)
    expand = (dst[:, None] == wide[None, :]).astype(BF16)
    heads = jnp.arange(128)
    pieces = jnp.stack([((heads[:, None] * 128 + DH + i == wide[None, :]) & (heads[:, None] < NH)).astype(BF16)
                        for i in range(3)])
    spare = wide % 128 - DH
    qconst = jnp.where((spare >= 0) & (spare < 3), -1.0, 0.0).astype(F32)[None, :]
    one64 = jnp.where(spare == 0, 1.0, 0.0).astype(F32)[None, :]
    pick64 = ((wide[:, None] == heads[None, :] * 128 + DH) & (heads[None, :] < NH)).astype(BF16)
    return expand, expand.T, pieces, qconst, one64, pick64


def _masked_sg_w(sg_w):
    r = lax.broadcasted_iota(jnp.int32, (CH, CH), 0)
    c = lax.broadcasted_iota(jnp.int32, (CH, CH), 1)
    return jnp.where((c <= r)[None], sg_w, 0.0)


def _pre_attn_fwd(x, gpre, win, fbias, lng, lnb, wm, bsg, gsg):
    T = x.shape[0]
    tm = TM

    def body(t, r, o, a, s):
        (x_ref,) = t
        gpre_r, win_r, fb_r, lng_r, lnb_r, wm_r, bsg_r, gsg_r = r
        a_o, qkv_o, flog_o, ccol_o, zuv_o, ysgn_o = o
        carry_ref, mixed_ref = s
        xv = x_ref[...]
        av = (xv * _rs(xv, D) * gpre_r[...]).astype(BF16)
        a_o[...] = av
        z = _dot(av, win_r[...])
        zu = z[:, 3 * AW:4 * AW]
        zv = z[:, 4 * AW:5 * AW]
        zuv_o[:, 0:AW] = zu
        zuv_o[:, AW:2 * AW] = zv
        zf = z[:, 5 * AW:] + fb_r[...]
        flog_o[...] = zf
        lane = lax.broadcasted_iota(jnp.int32, (tm, 128), 1)
        logf = jnp.where(lane < NH, jnp.minimum(zf, 0.0) - jnp.log(1.0 + jnp.exp(-jnp.abs(zf))), 0.0)
        rr = lax.broadcasted_iota(jnp.int32, (tm, tm), 0)
        cc = lax.broadcasted_iota(jnp.int32, (tm, tm), 1)
        tri = (cc <= rr).astype(BF16)
        cum = _dot01(tri, logf) + carry_ref[...]
        carry_ref[...] = cum[tm - 1:tm, :]
        ccol_o[...] = cum
        qkv_o[:, 0:AW] = (z[:, 0:AW] * (DH ** -0.5)).astype(BF16)
        qkv_o[:, AW:3 * AW] = z[:, AW:3 * AW].astype(BF16)
        gu, _, _, _, _, _, mixed = _sg_forward(zu, zv, wm_r, bsg_r[...], lng_r[...], lnb_r[...], mixed_ref, tm)
        ysg = gu * mixed
        ysgn_o[...] = (ysg * _rs(ysg, AW) * gsg_r[...]).astype(BF16)

    return _row_call(
        "pre_attn_fwd", body, T, tm, [x], [gpre, win, fbias, lng, lnb, wm, bsg, gsg],
        [(D, BF16), (3 * AW, BF16), (128, F32), (128, F32), (2 * AW, F32), (AW, BF16)], [],
        scratch=[(1, 128), (tm, AW)], vmem_mb=48)


def _flash_fwd(qt8, k8, vt8, sel):
    T = k8.shape[0]
    nq = T // TQ

    def body(qt_ref, k_ref, vt_ref, sel_ref, o_ref, l_ref, u_scr, p_scr):
        qi = pl.program_id(1)
        qts = (qt_ref[0:128, :], qt_ref[128:256, :])
        dmat = (lax.broadcasted_iota(jnp.int32, (TQ, TQ), 0) - lax.broadcasted_iota(jnp.int32, (TQ, TQ), 1))
        u_scr[1] = jnp.full((2, TQ, TQ), MASKED, F32)
        p_scr[...] = jnp.zeros(p_scr.shape, BF16)

        def sub(t, carry, sc, sb):
            blk_c = jnp.clip(t - 2, 0, qi)
            off_a = pl.multiple_of(jnp.minimum(t, qi) * TQ, TQ)
            lim = (qi - t) * TQ
            new = []
            for j in (0, 1):
                m, al, acc = carry[j]
                acc = al * acc + _dot(vt_ref[blk_c, j * 128:(j + 1) * 128, :], p_scr[sc, j])
                m_new = jnp.maximum(m, jnp.max(u_scr[sb, j], axis=0, keepdims=True))
                p_scr[sb, j] = jnp.exp(u_scr[sb, j] - m_new).astype(BF16)
                u_scr[sc, j] = jnp.where(
                    dmat <= lim, _dot(k_ref[pl.ds(off_a, TQ), j * 128:(j + 1) * 128], qts[j]), MASKED)
                new.append((m_new, jnp.exp(m - m_new), acc))
            return tuple(new)

        def it(t2, carry):
            return sub(2 * t2 + 1, sub(2 * t2, carry, 0, 1), 1, 0)

        init = tuple((jnp.full((1, TQ), NEG, F32), jnp.ones((1, TQ), F32), jnp.zeros((128, TQ), F32))
                     for _ in (0, 1))
        (m0, _, a0), (m1, _, a1) = lax.fori_loop(0, (qi + 4) // 2, it, init)
        l0 = a0[DH:DH + 1, :]
        l1 = a1[DH:DH + 1, :]
        o_ref[...] = _dot01_tn(a0 * (1.0 / l0), sel_ref[0]) + _dot01_tn(a1 * (1.0 / l1), sel_ref[1])
        l_ref[0:1, :] = m0 + jnp.log(l0)
        l_ref[1:2, :] = m1 + jnp.log(l1)
        l_ref[2:8, :] = jnp.zeros((6, TQ), F32)

    return pl.pallas_call(
        body, name="flash_fwd", grid=(4, nq),
        in_specs=[pl.BlockSpec((None, 256, TQ), lambda h, i: (i, h, 0)),
                  pl.BlockSpec((T, 256), lambda h, i: (0, h)),
                  pl.BlockSpec((nq, 256, TQ), lambda h, i: (0, h, 0)),
                  pl.BlockSpec((2, 128, 128), lambda h, i: (0, 0, 0))],
        out_specs=[pl.BlockSpec((TQ, 128), lambda h, i: (i, h)),
                   pl.BlockSpec((None, 8, TQ), lambda h, i: (h, 0, i))],
        out_shape=[jax.ShapeDtypeStruct((T, AW), F32), jax.ShapeDtypeStruct((4, 8, T), F32)],
        scratch_shapes=[pltpu.VMEM((2, 2, TQ, TQ), F32), pltpu.VMEM((2, 2, TQ, TQ), BF16)],
        compiler_params=_params(40, ("arbitrary", "arbitrary")),
    )(qt8, k8, vt8, sel)


def _flash_bwd(q8, qt8, k8, kt8, v8, do8, dot8, lse, dlt):
    T = q8.shape[0]
    nk = T // TQ

    def body(q_ref, qt_ref, k_ref, kt_ref, v_ref, do_ref, dot_ref, l_ref, d_ref, dqt_ref, dk_ref, dv_ref,
             u_scr, dp_scr, p_scr, ds_scr):
        kb = pl.program_id(1)
        n = nk - kb

        @pl.when(kb == 0)
        def _():
            dqt_ref[...] = jnp.zeros(dqt_ref.shape, F32)

        dk_ref[...] = jnp.zeros(dk_ref.shape, F32)
        dv_ref[...] = jnp.zeros(dv_ref.shape, F32)
        u_scr[1] = jnp.full((2, TQ, TQ), MASKED, F32)
        dp_scr[1] = jnp.zeros((2, TQ, TQ), F32)
        p_scr[...] = jnp.zeros(p_scr.shape, BF16)
        ds_scr[...] = jnp.zeros(ds_scr.shape, BF16)
        dmat = (lax.broadcasted_iota(jnp.int32, (TQ, TQ), 0) - lax.broadcasted_iota(jnp.int32, (TQ, TQ), 1))
        ks = (k_ref[:, 0:128], k_ref[:, 128:256])
        vs = (v_ref[:, 0:128], v_ref[:, 128:256])
        kts = (kt_ref[0:128, :], kt_ref[128:256, :])

        def sub(t, sc, sb):
            blk_a = kb + jnp.minimum(t, n - 1)
            blk_c = kb + jnp.clip(t - 2, 0, n - 1)
            off_b = pl.multiple_of((kb + jnp.clip(t - 1, 0, n - 1)) * TQ, TQ)
            off_c = pl.multiple_of(blk_c * TQ, TQ)
            lim = jnp.where(t < n, t * TQ, -TQ)
            for j in (0, 1):
                hl = slice(j * 128, (j + 1) * 128)
                dqt_ref[blk_c, hl, :] += _dot(kts[j], ds_scr[sc, j])
                dk_ref[:, hl] += _dot(ds_scr[sc, j], q_ref[pl.ds(off_c, TQ), hl])
                dv_ref[:, hl] += _dot(p_scr[sc, j], do_ref[pl.ds(off_c, TQ), hl])
                p = jnp.exp(u_scr[sb, j] - l_ref[j:j + 1, pl.ds(off_b, TQ)])
                p_scr[sb, j] = p.astype(BF16)
                ds_scr[sb, j] = (p * (dp_scr[sb, j] - d_ref[j:j + 1, pl.ds(off_b, TQ)])).astype(BF16)
                u_scr[sc, j] = jnp.where(dmat <= lim, _dot(ks[j], qt_ref[blk_a, hl, :]), MASKED)
                dp_scr[sc, j] = _dot(vs[j], dot_ref[blk_a, hl, :])

        def it(t2, carry):
            sub(2 * t2, 0, 1)
            sub(2 * t2 + 1, 1, 0)
            return carry

        lax.fori_loop(0, (n + 3) // 2, it, 0)

    return pl.pallas_call(
        body, name="flash_bwd", grid=(4, nk),
        in_specs=[pl.BlockSpec((T, 256), lambda h, i: (0, h)),
                  pl.BlockSpec((nk, 256, TQ), lambda h, i: (0, h, 0)),
                  pl.BlockSpec((TQ, 256), lambda h, i: (i, h)),
                  pl.BlockSpec((None, 256, TQ), lambda h, i: (i, h, 0)),
                  pl.BlockSpec((TQ, 256), lambda h, i: (i, h)),
                  pl.BlockSpec((T, 256), lambda h, i: (0, h)),
                  pl.BlockSpec((nk, 256, TQ), lambda h, i: (0, h, 0)),
                  pl.BlockSpec((None, 8, T), lambda h, i: (h, 0, 0)),
                  pl.BlockSpec((None, 8, T), lambda h, i: (h, 0, 0))],
        out_specs=[pl.BlockSpec((nk, 256, TQ), lambda h, i: (0, h, 0)),
                   pl.BlockSpec((TQ, 256), lambda h, i: (i, h)),
                   pl.BlockSpec((TQ, 256), lambda h, i: (i, h))],
        out_shape=[jax.ShapeDtypeStruct((nk, NH * 128, TQ), F32), jax.ShapeDtypeStruct((T, NH * 128), F32),
                   jax.ShapeDtypeStruct((T, NH * 128), F32)],
        scratch_shapes=[pltpu.VMEM((2, 2, TQ, TQ), F32), pltpu.VMEM((2, 2, TQ, TQ), F32),
                        pltpu.VMEM((2, 2, TQ, TQ), BF16), pltpu.VMEM((2, 2, TQ, TQ), BF16)],
        compiler_params=_params(56, ("arbitrary", "arbitrary")),
    )(q8, qt8, k8, kt8, v8, do8, dot8, lse, dlt)


def _tail_fwd1(x, yatt, ysgn, gatt, wout, gpm, gpf, w1):
    T = x.shape[0]

    def body(t, r, o, a, s):
        x_ref, ya_ref, ys_ref = t
        gatt_r, wout_r, gpm_r, gpf_r, w1_r = r
        y_o, o_o, h1_o, c2_o, s_o, rr_o = o
        ya = ya_ref[...]
        yan = (ya * _rs(ya, AW) * gatt_r[...]).astype(BF16)
        y_o[:, 0:AW] = yan
        y_o[:, AW:] = ys_ref[...]
        ov = _dot(yan, wout_r[0:AW, :]) + _dot(ys_ref[...], wout_r[AW:, :])
        o_o[...] = ov
        h1 = x_ref[...] + ov * _rs(ov, D) * gpm_r[...]
        h1_o[...] = h1
        c2 = (h1 * _rs(h1, D) * gpf_r[...]).astype(BF16)
        c2_o[...] = c2
        rr = jnp.maximum(_dot(c2, w1_r[...]), 0.0)
        rr_o[...] = rr.astype(BF16)
        s_o[...] = (rr * rr).astype(BF16)

    return _row_call(
        "tail_fwd1", body, T, TM, [x, yatt, ysgn], [gatt, wout, gpm, gpf, w1],
        [(D, BF16), (D, F32), (D, F32), (D, BF16), (DFF, BF16), (DFF, BF16)], [], vmem_mb=48)


def _tail_fwd2(sact, h1, p, tgt, w2, gpff, wg, bg, wpe):
    T = h1.shape[0]

    def body(t, r, o, a, s):
        s_ref, h1_ref, p_ref, t_ref = t
        w2_r, gpff_r, wg_r, bg_r, wpe_r = r
        ff_o, h2b_o, de_o, dpre_o, dh2_o = o
        loss_a, dbg_a = a
        ff = _dot(s_ref[...], w2_r[...])
        ff_o[...] = ff
        h2 = h1_ref[...] + ff * _rs(ff, D) * gpff_r[...]
        h2b = h2.astype(BF16)
        h2b_o[...] = h2b
        gate = 1.0 / (1.0 + jnp.exp(-(_dot(h2b, wg_r[...]) + bg_r[...])))
        e = _dot(p_ref[...].astype(BF16), wpe_r[...])
        diff = h2 + gate * e - t_ref[...]
        loss_a[...] += jnp.sum(diff * diff, axis=0, keepdims=True)
        dh3 = diff * (1.0 / D)
        de_o[...] = (dh3 * gate).astype(BF16)
        dpre = dh3 * e * gate * (1.0 - gate)
        dbg_a[...] += jnp.sum(dpre, axis=0, keepdims=True)
        dpb = dpre.astype(BF16)
        dpre_o[...] = dpb
        dh2_o[...] = dh3 + _dot_nt(dpb, wg_r[...])

    return _row_call(
        "tail_fwd2", body, T, TM, [sact, h1, p, tgt], [w2, gpff, wg, bg, wpe],
        [(D, F32), (D, BF16), (D, BF16), (D, BF16), (D, F32)], [(1, D), (1, D)], vmem_mb=48)


def _tail_bwd(dh2, ff, rr, h1, ov, yatt, w2, w1, wout, gpff, gpf, gpm, gatt, hsel):
    T = dh2.shape[0]

    def body(t, r, o, a, s):
        dh2_ref, ff_ref, rr_ref, h1_ref, o_ref, ya_ref = t
        w2_r, w1_r, wout_r, gpff_r, gpf_r, gpm_r, gatt_r, hsel_r = r
        dff_o, dr_o, do_o, dya_o, dlt_o, dysg_o, dh1_o = o
        dgpff_a, dgpf_a, dgpm_a, dgatt_a = a
        dh2v = dh2_ref[...]
        ffv = ff_ref[...]
        dff, dg = _rms_bwd(dh2v, ffv, _rs(ffv, D), gpff_r[...], D)
        dgpff_a[...] += dg
        dffb = dff.astype(BF16)
        dff_o[...] = dffb
        drb = (_dot_nt(dffb, w2_r[...]) * (2.0 * rr_ref[...].astype(F32))).astype(BF16)
        dr_o[...] = drb
        dc2 = _dot_nt(drb, w1_r[...])
        h1v = h1_ref[...]
        d1, dg = _rms_bwd(dc2, h1v, _rs(h1v, D), gpf_r[...], D)
        dgpf_a[...] += dg
        dh1 = dh2v + d1
        dh1_o[...] = dh1
        ovv = o_ref[...]
        dov, dg = _rms_bwd(dh1, ovv, _rs(ovv, D), gpm_r[...], D)
        dgpm_a[...] += dg
        dob = dov.astype(BF16)
        do_o[...] = dob
        dysg_o[...] = _dot_nt(dob, wout_r[AW:, :])
        dyan = _dot_nt(dob, wout_r[0:AW, :])
        ya = ya_ref[...]
        dya, dg = _rms_bwd(dyan, ya, _rs(ya, AW), gatt_r[...], AW)
        dgatt_a[...] += dg
        dya_o[...] = dya.astype(BF16)
        dlt_o[...] = _dot01_r(dya * ya, hsel_r[...])

    return _row_call(
        "tail_bwd", body, T, TM, [dh2, ff, rr, h1, ov, yatt],
        [w2, w1, wout, gpff, gpf, gpm, gatt, hsel],
        [(D, BF16), (DFF, BF16), (D, BF16), (AW, BF16), (AW, F32), (AW, F32), (D, F32)],
        [(1, D), (1, D), (1, D), (1, AW)], vmem_mb=56)


def _pre_attn_bwd(x, dh1, dq, dk, dv, dccol, flog, zuv, dysg, gpre, win, lng, lnb, wm, wmt, bsg, gsg, gsel):
    T = x.shape[0]
    tm = TM

    def body(t, r, o, a, s):
        x_ref, dh1_ref, dq_ref, dk_ref, dv_ref, dc_ref, fl_ref, zuv_ref, dys_ref = t
        gpre_r, win_r, lng_r, lnb_r, wm_r, wmt_r, bsg_r, gsg_r, gsel_r = r
        dx_o, dz_o = o
        dgpre_a, dfb_a, dgsg_a, dlng_a, dlnb_a, dws_a, dbs_a, dsb_a = a
        carry_ref, mixed_ref, dvv_ref = s
        dcv = dc_ref[...]
        rr = lax.broadcasted_iota(jnp.int32, (tm, tm), 0)
        cc = lax.broadcasted_iota(jnp.int32, (tm, tm), 1)
        triu = (cc >= rr).astype(BF16)
        dlogf = _dot01(triu, dcv) + carry_ref[...]
        carry_ref[...] = dlogf[0:1, :]
        dzf = dlogf * (1.0 / (1.0 + jnp.exp(fl_ref[...])))
        dfb_a[...] += jnp.sum(dzf, axis=0, keepdims=True)
        dz_o[:, 5 * AW:] = dzf.astype(BF16)
        zu = zuv_ref[:, 0:AW]
        zv = zuv_ref[:, AW:]
        gu, tu, tv, xhat, rstd, vvb, mixed = _sg_forward(
            zu, zv, wm_r, bsg_r[...], lng_r[...], lnb_r[...], mixed_ref, tm)
        ysg = gu * mixed
        dysg_n = dys_ref[...]
        dys, dg = _rms_bwd(dysg_n, ysg, _rs(ysg, AW), gsg_r[...], AW)
        dgsg_a[...] += dg
        dgu = dys * mixed
        dmix = dys * gu
        dmb = dmix.astype(BF16)
        lane = lax.broadcasted_iota(jnp.int32, (CH, 128), 1)
        lo = lane < DH
        for c in range(tm // CH):
            rows = slice(c * CH, (c + 1) * CH)
            dbs_a[...] += dmix[rows, :]
            for j in range(4):
                cols = slice(j * 128, (j + 1) * 128)
                dmblk = dmb[rows, cols]
                vblk = vvb[rows, cols]
                d0 = _dot(wmt_r[2 * j], dmblk)
                d1 = _dot(wmt_r[2 * j + 1], dmblk)
                dvv_ref[rows, cols] = jnp.where(lo, d0, d1)
                dws_a[2 * j] += _dot_nt(jnp.where(lo, dmblk, jnp.zeros_like(dmblk)), vblk)
                dws_a[2 * j + 1] += _dot_nt(jnp.where(lo, jnp.zeros_like(dmblk), dmblk), vblk)
        dvv = dvv_ref[...]
        dlng_a[...] += jnp.sum(dvv * xhat, axis=0, keepdims=True)
        dlnb_a[...] += jnp.sum(dvv, axis=0, keepdims=True)
        dxh = dvv * lng_r[...]
        dvg = rstd * (dxh - jnp.sum(dxh, axis=-1, keepdims=True) * (1.0 / AW)
                      - xhat * (jnp.sum(dxh * xhat, axis=-1, keepdims=True) * (1.0 / AW)))
        dz_o[:, 3 * AW:4 * AW] = (dgu * _gelu_grad(zu, tu)).astype(BF16)
        dz_o[:, 4 * AW:5 * AW] = (dvg * _gelu_grad(zv, tv)).astype(BF16)
        dz_o[:, 0:AW] = (dq_ref[...] * (DH ** -0.5)).astype(BF16)
        dz_o[:, AW:2 * AW] = dk_ref[...].astype(BF16)
        dz_o[:, 2 * AW:3 * AW] = dv_ref[...].astype(BF16)
        da = _dot_nt(dz_o[...], win_r[...])
        xv = x_ref[...]
        dxa, dg = _rms_bwd(da, xv, _rs(xv, D), gpre_r[...], D)
        dgpre_a[...] += dg
        dx_o[...] = dh1_ref[...] + dxa

        @pl.when(pl.program_id(0) == T // tm - 1)
        def _():
            dsb_a[...] = _dot01_r(dbs_a[...], gsel_r[...])

    outs = _row_call(
        "pre_attn_bwd", body, T, tm, [x, dh1, dq, dk, dv, dccol, flog, zuv, dysg],
        [gpre, win, lng, lnb, wm, wmt, bsg, gsg, gsel],
        [(D, F32), (ZW, BF16)],
        [(1, D), (1, 128), (1, AW), (1, AW), (1, AW), (8, CH, CH), (CH, AW), (CH, 128)],
        scratch=[(1, 128), (tm, AW), (tm, AW)], reverse=True, vmem_mb=48)
    return outs


def _matmul_tn(name, a, b, tn=512, tt=512):
    T, K = a.shape
    N = b.shape[1]
    tk = min(K, 1024)
    tn = min(tn, N)
    tt = min(tt, T)

    def body(a_ref, b_ref, o_ref):
        @pl.when(pl.program_id(2) == 0)
        def _():
            o_ref[...] = jnp.zeros(o_ref.shape, F32)

        o_ref[...] += _dot_tn(a_ref[...].astype(BF16), b_ref[...].astype(BF16))

    return pl.pallas_call(
        body, name=name, grid=(K // tk, N // tn, T // tt),
        in_specs=[pl.BlockSpec((tt, tk), lambda i, j, t: (t, i)),
                  pl.BlockSpec((tt, tn), lambda i, j, t: (t, j))],
        out_specs=pl.BlockSpec((tk, tn), lambda i, j, t: (i, j)),
        out_shape=jax.ShapeDtypeStruct((K, N), F32),
        compiler_params=_params(32, ("arbitrary", "arbitrary", "arbitrary")),
    )(a, b)


def _me():
    return lax.axis_index("x"), lax.axis_index("y"), lax.axis_index("c")


HBM_SPEC = pl.BlockSpec(memory_space=pltpu.HBM)


def _gather_weights(mine):
    def body(mine_ref, out_ref, send_sems, recv_sems, local_sem):
        x, y, c = _me()
        k_me = 2 * x + y
        chips = [(1 - x, y), (x, 1 - y), (1 - x, 1 - y)]
        own = pltpu.make_async_copy(mine_ref, out_ref.at[k_me], local_sem)
        own.start()
        sends = [pltpu.make_async_remote_copy(
            src_ref=mine_ref, dst_ref=out_ref.at[k_me], send_sem=send_sems.at[j], recv_sem=recv_sems.at[j],
            device_id=(cx, cy, c), device_id_type=MESH) for j, (cx, cy) in enumerate(chips)]
        for s in sends:
            s.start()
        for j, (cx, cy) in enumerate(chips):
            pltpu.make_async_remote_copy(
                src_ref=mine_ref, dst_ref=out_ref.at[2 * cx + cy], send_sem=send_sems.at[j],
                recv_sem=recv_sems.at[j], device_id=(cx, cy, c), device_id_type=MESH).wait_recv()
        for s in sends:
            s.wait_send()
        own.wait()

    return pl.pallas_call(
        body, name="gather_weights", in_specs=[HBM_SPEC], out_specs=HBM_SPEC,
        out_shape=jax.ShapeDtypeStruct((4,) + mine.shape, mine.dtype),
        scratch_shapes=[pltpu.SemaphoreType.DMA((3,)), pltpu.SemaphoreType.DMA((3,)), pltpu.SemaphoreType.DMA],
    )(mine)


def _swap_halves(g):
    def body(g_ref, got_ref, send_sem, recv_sem):
        x, y, c = _me()
        theirs = pl.multiple_of((1 - c) * HALF_ROWS, 8)
        cp = pltpu.make_async_remote_copy(
            src_ref=g_ref.at[:, pl.ds(theirs, HALF_ROWS), :], dst_ref=got_ref, send_sem=send_sem,
            recv_sem=recv_sem, device_id=(x, y, 1 - c), device_id_type=MESH)
        cp.start()
        cp.wait()

    return pl.pallas_call(
        body, name="swap_halves", in_specs=[HBM_SPEC], out_specs=HBM_SPEC,
        out_shape=jax.ShapeDtypeStruct((4, HALF_ROWS, 1024), F32),
        scratch_shapes=[pltpu.SemaphoreType.DMA, pltpu.SemaphoreType.DMA],
    )(g)


def _pair_sum(mine_half, got):
    def body(a_ref, b_ref, o_ref):
        o_ref[...] = a_ref[...] + b_ref[...]

    spec = pl.BlockSpec((1, RED_ROWS, 1024), lambda k, i: (k, i, 0))
    return pl.pallas_call(
        body, name="pair_sum", grid=(4, HALF_ROWS // RED_ROWS), in_specs=[spec, spec], out_specs=spec,
        out_shape=jax.ShapeDtypeStruct(got.shape, F32),
        compiler_params=_params(32, ("arbitrary", "arbitrary")),
    )(mine_half, got)


def _exchange_chips(ps):
    def body(ps_ref, out_ref, send_sems, recv_sems, local_sem):
        x, y, c = _me()
        k_me = 2 * x + y
        chips = [(1 - x, y), (x, 1 - y), (1 - x, 1 - y)]
        own = pltpu.make_async_copy(ps_ref.at[k_me], out_ref.at[k_me], local_sem)
        own.start()
        sends = [pltpu.make_async_remote_copy(
            src_ref=ps_ref.at[2 * cx + cy], dst_ref=out_ref.at[k_me], send_sem=send_sems.at[j],
            recv_sem=recv_sems.at[j], device_id=(cx, cy, c), device_id_type=MESH)
            for j, (cx, cy) in enumerate(chips)]
        for s in sends:
            s.start()
        for j, (cx, cy) in enumerate(chips):
            pltpu.make_async_remote_copy(
                src_ref=ps_ref.at[k_me], dst_ref=out_ref.at[2 * cx + cy], send_sem=send_sems.at[j],
                recv_sem=recv_sems.at[j], device_id=(cx, cy, c), device_id_type=MESH).wait_recv()
        for s in sends:
            s.wait_send()
        own.wait()

    return pl.pallas_call(
        body, name="exchange_chips", in_specs=[HBM_SPEC], out_specs=HBM_SPEC,
        out_shape=jax.ShapeDtypeStruct(ps.shape, F32),
        scratch_shapes=[pltpu.SemaphoreType.DMA((3,)), pltpu.SemaphoreType.DMA((3,)), pltpu.SemaphoreType.DMA],
    )(ps)


def _adamw(w, g, m, v):
    m = B1 * m + (1.0 - B1) * g
    v = B2 * v + (1.0 - B2) * (g * g)
    delta = -LR * ((m / BC1) / (jnp.sqrt(v / BC2) + AEPS) + WD * w)
    return delta, m, v


def _reduce_chips(parts):
    def body(p_ref, o_ref):
        o_ref[...] = ((p_ref[0] + p_ref[1]) + p_ref[2]) + p_ref[3]

    return pl.pallas_call(
        body, name="reduce_chips", grid=(HALF_ROWS // RED_ROWS,),
        in_specs=[pl.BlockSpec((4, RED_ROWS, 1024), lambda i: (0, i, 0))],
        out_specs=pl.BlockSpec((RED_ROWS, 1024), lambda i: (i, 0)),
        out_shape=jax.ShapeDtypeStruct((HALF_ROWS, 1024), F32), compiler_params=_params(32),
    )(parts)


def _share_grad(gh):
    def body(g_ref, got_ref, send_sem, recv_sem):
        x, y, c = _me()
        cp = pltpu.make_async_remote_copy(
            src_ref=g_ref, dst_ref=got_ref, send_sem=send_sem, recv_sem=recv_sem,
            device_id=(x, y, 1 - c), device_id_type=MESH)
        cp.start()
        cp.wait()

    return pl.pallas_call(
        body, name="share_grad", in_specs=[HBM_SPEC], out_specs=HBM_SPEC,
        out_shape=jax.ShapeDtypeStruct((HALF_ROWS, 1024), F32),
        scratch_shapes=[pltpu.SemaphoreType.DMA, pltpu.SemaphoreType.DMA],
    )(gh)


def _update(g, w, m, v):
    def body(g_ref, w_ref, m_ref, v_ref, o_ref):
        delta, mn, vn = _adamw(w_ref[...], g_ref[...], m_ref[...], v_ref[...])
        o_ref[0] = delta
        o_ref[1] = mn
        o_ref[2] = vn

    s1 = pl.BlockSpec((RED_ROWS, 1024), lambda i: (i, 0))
    return pl.pallas_call(
        body, name="update", grid=(SHARD_ROWS // RED_ROWS,), in_specs=[s1, s1, s1, s1],
        out_specs=pl.BlockSpec((3, RED_ROWS, 1024), lambda i: (0, i, 0)),
        out_shape=jax.ShapeDtypeStruct((3, SHARD_ROWS, 1024), F32), compiler_params=_params(32),
    )(g, w, m, v)


def _small_allreduce_update(g, w, m, v):
    def body(g_ref, w_ref, m_ref, v_ref, o_ref, buf, send_sems, recv_sems):
        x, y, c = _me()
        me = 4 * x + 2 * y + c
        buf[me] = g_ref[...]
        rels = [(rx, ry, rc) for rx in (0, 1) for ry in (0, 1) for rc in (0, 1)][1:]

        def peer(r):
            return ((x + r[0]) % 2, (y + r[1]) % 2, (c + r[2]) % 2)

        sends = [pltpu.make_async_remote_copy(
            src_ref=g_ref, dst_ref=buf.at[me], send_sem=send_sems.at[j], recv_sem=recv_sems.at[j],
            device_id=peer(r), device_id_type=MESH) for j, r in enumerate(rels)]
        for s in sends:
            s.start()
        for j, r in enumerate(rels):
            px, py, pc = peer(r)
            pltpu.make_async_remote_copy(
                src_ref=g_ref, dst_ref=buf.at[4 * px + 2 * py + pc], send_sem=send_sems.at[j],
                recv_sem=recv_sems.at[j], device_id=peer(r), device_id_type=MESH).wait_recv()
        for s in sends:
            s.wait_send()
        tot = buf[0]
        for d in range(1, 8):
            tot = tot + buf[d]
        delta, mn, vn = _adamw(w_ref[...], tot, m_ref[...], v_ref[...])
        o_ref[0] = tot
        o_ref[1] = delta
        o_ref[2] = mn
        o_ref[3] = vn

    vm = pl.BlockSpec(memory_space=pltpu.VMEM)
    return pl.pallas_call(
        body, name="small_allreduce_update", in_specs=[vm, vm, vm, vm], out_specs=vm,
        out_shape=jax.ShapeDtypeStruct((4, SMALL_ROWS, 1024), F32),
        scratch_shapes=[pltpu.VMEM((8, SMALL_ROWS, 1024), F32), pltpu.SemaphoreType.DMA((7,)),
                        pltpu.SemaphoreType.DMA((7,))],
        compiler_params=pltpu.CompilerParams(vmem_limit_bytes=32 * 1024 * 1024),
    )(g, w, m, v)


def _pack_shard(w_in, w_out, w1, w2, plew, wg):
    return jnp.concatenate([
        jnp.pad(w_in, ((0, 0), (0, 768 - 642))).reshape(768, 1024), w_out, w1, w2,
        plew.reshape(64, 1024), wg], axis=0)


def _unpack_shard(pk):
    r = 0
    out = []
    for rows, shape in ((768, (1024, 768)), (256, (256, 1024)), (1024, (1024, 1024)), (1024, (1024, 1024)),
                        (64, (256, 256)), (256, (256, 1024))):
        out.append(pk[r:r + rows].reshape(shape))
        r += rows
    out[0] = out[0][:, :642]
    return out


def _full_weights(gathered):
    parts = [_unpack_shard(gathered[k]) for k in range(4)]
    w_in = jnp.concatenate([p[0] for p in parts], axis=1)
    w_in = jnp.concatenate([w_in[:, :3 * AW], w_in[:, 3 * AW + NH:], w_in[:, 3 * AW:3 * AW + NH],
                            jnp.zeros((D, 128 - NH), w_in.dtype)], axis=1)
    return (w_in, jnp.concatenate([p[1] for p in parts], axis=0), jnp.concatenate([p[2] for p in parts], axis=1),
            jnp.concatenate([p[3] for p in parts], axis=0), jnp.concatenate([p[4] for p in parts], axis=1),
            jnp.concatenate([p[5] for p in parts], axis=0))


def _pack_grads(dwin_k, dwout, dw1, dw2, dplew, dwg):
    dwin = jnp.concatenate([dwin_k[:, :3 * AW], dwin_k[:, 5 * AW:5 * AW + NH], dwin_k[:, 3 * AW:5 * AW]], axis=1)
    return jnp.stack([
        _pack_shard(dwin[:, 642 * k:642 * (k + 1)], dwout[256 * k:256 * (k + 1)], dw1[:, 1024 * k:1024 * (k + 1)],
                    dw2[1024 * k:1024 * (k + 1)], dplew[:, 256 * k:256 * (k + 1)], dwg[256 * k:256 * (k + 1)])
        for k in range(4)])


def _row1024(v):
    v = v.reshape(-1)
    return jnp.pad(v, (0, 1024 - v.shape[0])).reshape(1, 1024)


def _pack_small(sg_w, vecs, loss_row=None):
    rows = [sg_w.reshape(128, 1024)] + [_row1024(v) for v in vecs]
    rows.append(jnp.zeros((1, 1024), F32) if loss_row is None else loss_row)
    n = sum(r.shape[0] for r in rows)
    rows.append(jnp.zeros((SMALL_ROWS - n, 1024), F32))
    return jnp.concatenate(rows, axis=0)


def _local_step(x, p, tgt, win_k, wout, w1, w2, plew, wg, small):
    T = x.shape[0]
    row = lambda n: small[n].reshape(1, -1)
    fbias = jnp.pad(row("f_bias"), ((0, 0), (0, 128 - NH)))
    wm = _masked_sg_w(small["sg_w"].reshape(8, CH, CH))
    wmb = wm.astype(BF16)
    wmt = jnp.swapaxes(wm, 1, 2).astype(BF16)
    bsg = jnp.repeat(small["sg_b"].reshape(8, CH).T, DH, axis=1)
    ln_g, ln_b, gsg, gatt = row("sg_ln_g"), row("sg_ln_b"), row("sg_out_g"), row("att_out_g")
    gpre, gpm, gpf, gpff, bg = row("pre_mix_g"), row("post_mix_g"), row("pre_ffn_g"), row("post_ffn_g"), row("ple_gate_b")
    gsel = (jnp.arange(AW)[:, None] // DH == jnp.arange(128)[None, :]).astype(BF16)
    hsel = (jnp.arange(AW)[:, None] // DH == jnp.arange(AW)[None, :] // DH).astype(BF16)

    a, qkv, flog, ccol, zuv, ysgn = _pre_attn_fwd(x, gpre, win_k, fbias, ln_g, ln_b, wmb, bsg, gsg)

    nt = T // TQ
    hd = lambda t, i: t[:, i * AW:(i + 1) * AW].reshape(T, NH, DH)
    cc = ccol[:, :NH]
    c_hi = lax.reduce_precision(cc, 8, 7)
    c_mid = lax.reduce_precision(cc - c_hi, 8, 7)
    c_lo = lax.reduce_precision(cc - c_hi - c_mid, 8, 7)
    pc = lambda t: t[..., None].astype(BF16)
    zpad = lambda n: jnp.zeros((T, NH, n), BF16)
    one = jnp.ones((T, NH, 1), BF16)
    wide = lambda parts: jnp.concatenate(parts, axis=-1).reshape(T, NH * 128)
    slabs = lambda t: jnp.swapaxes(t.reshape(nt, TQ, NH * 128), 1, 2)
    q8 = wide([hd(qkv, 0), -one, -one, -one, zpad(61)])
    k8 = wide([hd(qkv, 1), pc(c_hi), pc(c_mid), pc(c_lo), zpad(61)])
    v8 = wide([hd(qkv, 2), zpad(64)])
    qt8 = slabs(q8)
    kt8 = slabs(wide([hd(qkv, 1), one, zpad(63)]))
    vt8 = slabs(wide([hd(qkv, 2), one, zpad(63)]))
    lanes = jnp.arange(128)
    sel = jnp.stack([((lanes[:, None] == lanes[None, :] - DH * j) & (lanes[:, None] < DH)).astype(BF16)
                     for j in (0, 1)])

    yatt, lse = _flash_fwd(qt8, k8, vt8, sel)
    y, ov, h1, c2, sact, rr = _tail_fwd1(x, yatt, ysgn, gatt, wout, gpm, gpf, w1)
    ff, h2b, de, dpre, dh2, loss_l, dbg = _tail_fwd2(sact, h1, p, tgt, w2, gpff, wg, bg, plew)
    dff, dr, do, dya, dlt, dysg, dh1, dgpff, dgpf, dgpm, dgatt = _tail_bwd(
        dh2, ff, rr, h1, ov, yatt, w2, w1, wout, gpff, gpf, gpm, gatt, hsel)
    do8 = wide([hd(dya, 0), zpad(64)])
    dlt4 = jnp.pad(dlt[:, ::DH].T.reshape(4, 2, T), ((0, 0), (0, 6), (0, 0)))
    dqt, dk8, dv8 = _flash_bwd(q8, qt8, k8, kt8, v8, do8, slabs(do8), lse, dlt4)
    dq8 = jnp.swapaxes(dqt, 1, 2).reshape(T, NH, 128)
    dk8 = dk8.reshape(T, NH, 128)
    packed = lambda t: t[:, :, :DH].reshape(T, AW)
    dccol = jnp.pad(dq8[:, :, DH] + dk8[:, :, DH], ((0, 0), (0, 128 - NH)))
    dx, dz, dgpre, dfb, dgsg, dlng, dlnb, dws, _, dsbt = _pre_attn_bwd(
        x, dh1, packed(dq8), packed(dk8), packed(dv8.reshape(T, NH, 128)), dccol, flog, zuv, dysg,
        gpre, win_k, ln_g, ln_b, wmb, wmt, bsg, gsg, gsel)

    dwin_k = _matmul_tn("grad_w_in", a, dz, tn=384)
    dwout = _matmul_tn("grad_w_out", y, do)
    dw1 = _matmul_tn("grad_w_ff1", c2, dr)
    dw2 = _matmul_tn("grad_w_ff2", sact, dff)
    dwg = _matmul_tn("grad_ple_gate_w", h2b, dpre)
    dplew = _matmul_tn("grad_ple_w", p, de)

    dsb = dsbt[:, :8].T
    gsmall = {"sg_w": _masked_sg_w(dws), "f_bias": dfb[:, :NH], "sg_ln_g": dlng, "sg_ln_b": dlnb, "sg_b": dsb,
              "att_out_g": dgatt, "sg_out_g": dgsg, "pre_mix_g": dgpre, "post_mix_g": dgpm, "pre_ffn_g": dgpf,
              "post_ffn_g": dgpff, "ple_gate_b": dbg}
    return loss_l, dx, (dwin_k, dwout, dw1, dw2, dplew, dwg), gsmall


def kernel(x, p, w_in, f_bias, sg_ln_g, sg_ln_b, sg_w, sg_b, att_out_g, sg_out_g, w_out, pre_mix_g, post_mix_g, pre_ffn_g, post_ffn_g, w_ff1, w_ff2, ple_w, ple_gate_w, ple_gate_b, loss_target, m_w_in, m_f_bias, m_sg_ln_g, m_sg_ln_b, m_sg_w, m_sg_b, m_att_out_g, m_sg_out_g, m_w_out, m_pre_mix_g, m_post_mix_g, m_pre_ffn_g, m_post_ffn_g, m_w_ff1, m_w_ff2, m_ple_w, m_ple_gate_w, m_ple_gate_b, v_w_in, v_f_bias, v_sg_ln_g, v_sg_ln_b, v_sg_w, v_sg_b, v_att_out_g, v_sg_out_g, v_w_out, v_pre_mix_g, v_post_mix_g, v_pre_ffn_g, v_post_ffn_g, v_w_ff1, v_w_ff2, v_ple_w, v_ple_gate_w, v_ple_gate_b):
    c = lax.axis_index("c")
    big = lambda t: (t[0][0], t[1][0], t[2][0], t[3][0], t[4][0], t[5][0])
    w_big = big((w_in, w_out, w_ff1, w_ff2, ple_w, ple_gate_w))
    m_big = big((m_w_in, m_w_out, m_w_ff1, m_w_ff2, m_ple_w, m_ple_gate_w))
    v_big = big((v_w_in, v_w_out, v_w_ff1, v_w_ff2, v_ple_w, v_ple_gate_w))
    small = {"sg_w": sg_w, "f_bias": f_bias, "sg_ln_g": sg_ln_g, "sg_ln_b": sg_ln_b, "sg_b": sg_b,
             "att_out_g": att_out_g, "sg_out_g": sg_out_g, "pre_mix_g": pre_mix_g, "post_mix_g": post_mix_g,
             "pre_ffn_g": pre_ffn_g, "post_ffn_g": post_ffn_g, "ple_gate_b": ple_gate_b}
    m_small = {"sg_w": m_sg_w, "f_bias": m_f_bias, "sg_ln_g": m_sg_ln_g, "sg_ln_b": m_sg_ln_b, "sg_b": m_sg_b,
               "att_out_g": m_att_out_g, "sg_out_g": m_sg_out_g, "pre_mix_g": m_pre_mix_g,
               "post_mix_g": m_post_mix_g, "pre_ffn_g": m_pre_ffn_g, "post_ffn_g": m_post_ffn_g,
               "ple_gate_b": m_ple_gate_b}
    v_small = {"sg_w": v_sg_w, "f_bias": v_f_bias, "sg_ln_g": v_sg_ln_g, "sg_ln_b": v_sg_ln_b, "sg_b": v_sg_b,
               "att_out_g": v_att_out_g, "sg_out_g": v_sg_out_g, "pre_mix_g": v_pre_mix_g,
               "post_mix_g": v_post_mix_g, "pre_ffn_g": v_pre_ffn_g, "post_ffn_g": v_post_ffn_g,
               "ple_gate_b": v_ple_gate_b}

    w_pk = _pack_shard(*w_big)
    gathered = _gather_weights(w_pk.astype(BF16))
    win_k, wout, w1, w2, plew, wg = _full_weights(gathered)

    loss_l, dx, gbig, gsmall = _local_step(x[0], p[0, 0], loss_target[0], win_k, wout, w1, w2, plew, wg, small)

    gp = _pack_grads(*gbig)
    got = _swap_halves(gp)
    mine_half = lax.dynamic_slice_in_dim(gp, c * HALF_ROWS, HALF_ROWS, axis=1)
    parts = _exchange_chips(_pair_sum(mine_half, got))
    gh = _reduce_chips(parts)
    got = _share_grad(gh)
    g_full = jnp.where(c == 0, jnp.concatenate([gh, got], axis=0), jnp.concatenate([got, gh], axis=0))
    upd = _update(g_full, w_pk, _pack_shard(*m_big), _pack_shard(*v_big))
    big_out = [_unpack_shard(t) for t in (g_full, upd[0], upd[1], upd[2])]

    vec = lambda d: [d[n] for n in VEC_NAMES]
    loss_row = loss_l * (0.5 / D)
    res_s = _small_allreduce_update(
        _pack_small(gsmall["sg_w"], vec(gsmall), loss_row), _pack_small(small["sg_w"], vec(small)),
        _pack_small(m_small["sg_w"], vec(m_small)), _pack_small(v_small["sg_w"], vec(v_small)))
    loss = jnp.sum(res_s[0, LOSS_ROW])

    def small_out(i, name):
        ref = small[name]
        if name == "sg_w":
            return res_s[i, 0:128].reshape(ref.shape)
        r = 128 + VEC_NAMES.index(name)
        return res_s[i, r, :ref.size].reshape(ref.shape)

    order = ["w_in", "f_bias", "sg_ln_g", "sg_ln_b", "sg_w", "sg_b", "att_out_g", "sg_out_g", "w_out",
             "pre_mix_g", "post_mix_g", "pre_ffn_g", "post_ffn_g", "w_ff1", "w_ff2", "ple_w", "ple_gate_w",
             "ple_gate_b"]
    big_idx = {"w_in": 0, "w_out": 1, "w_ff1": 2, "w_ff2": 3, "ple_w": 4, "ple_gate_w": 5}
    outs = [loss, dx[None]]
    for i in range(4):
        for name in order:
            if name in big_idx:
                outs.append(big_out[i][big_idx[name]][None])
            else:
                outs.append(small_out(i, name))
    return tuple(outs)
```

```python
import math

import jax
import jax.numpy as jnp
from jax import lax
from jax.experimental import pallas as pl
from jax.experimental.pallas import tpu as pltpu

F32 = jnp.float32
BF16 = jnp.bfloat16
MESH = pl.DeviceIdType.MESH

D = 1024
DH = 64
NH = 8
AW = 512
CH = 128
DFF = 4096
ZW = 5 * AW + 128
EPS = 1e-6
NEG = -1e30
MASKED = -2e30

TM = 256
TQ = 256

LR, B1, B2, AEPS, WD, STEP = 0.001, 0.9, 0.999, 1e-08, 0.01, 10
BC1 = 1.0 - B1 ** STEP
BC2 = 1.0 - B2 ** STEP

SHARD_ROWS = 768 + 256 + 1024 + 1024 + 64 + 256
HALF_ROWS = SHARD_ROWS // 2
RED_ROWS = HALF_ROWS // 4
SMALL_ROWS = 144
VEC_NAMES = ("f_bias", "sg_ln_g", "sg_ln_b", "sg_b", "att_out_g", "sg_out_g", "pre_mix_g",
             "post_mix_g", "pre_ffn_g", "post_ffn_g", "ple_gate_b")
LOSS_ROW = 128 + len(VEC_NAMES)


def _dot(a, b):
    return jnp.dot(a, b, preferred_element_type=F32)


def _dot_nt(a, b):
    return lax.dot_general(a, b, (((1,), (1,)), ((), ())), preferred_element_type=F32)


def _dot_tn(a, b):
    return lax.dot_general(a, b, (((0,), (0,)), ((), ())), preferred_element_type=F32)


def _split3(x):
    h = x.astype(BF16)
    r = x - h.astype(F32)
    m = r.astype(BF16)
    l = (r - m.astype(F32)).astype(BF16)
    return h, m, l


def _dot01(sel, x):
    h, m, l = _split3(x)
    return _dot(sel, h) + _dot(sel, m) + _dot(sel, l)


def _dot01_r(x, sel):
    h, m, l = _split3(x)
    return _dot(h, sel) + _dot(m, sel) + _dot(l, sel)


def _dot01_tn(x, sel):
    h, m, l = _split3(x)
    return _dot_tn(h, sel) + _dot_tn(m, sel) + _dot_tn(l, sel)


def _rs(x, n):
    return lax.rsqrt(jnp.sum(x * x, axis=-1, keepdims=True) * (1.0 / n) + EPS)


def _rms_bwd(dn, x, rs, g, n):
    w = dn * g
    dx = rs * w - x * ((rs * rs * rs) * (1.0 / n) * jnp.sum(w * x, axis=-1, keepdims=True))
    return dx, jnp.sum(dn * x * rs, axis=0, keepdims=True)


_GC = math.sqrt(2.0 / math.pi)


def _gelu(x):
    t = jnp.tanh(_GC * (x + 0.044715 * x * x * x))
    return 0.5 * x * (1.0 + t), t


def _gelu_grad(x, t):
    return 0.5 * (1.0 + t) + 0.5 * x * (1.0 - t * t) * (_GC * (1.0 + 3.0 * 0.044715 * x * x))


def _params(vmem_mb, sem=("arbitrary",)):
    return pltpu.CompilerParams(dimension_semantics=sem, vmem_limit_bytes=vmem_mb * 1024 * 1024)


def _row_call(name, body, T, tm, tiled, resident, outs, accs, scratch=(), reverse=False, vmem_mb=48):
    nt = T // tm
    n_t, n_r, n_o, n_a = len(tiled), len(resident), len(outs), len(accs)

    def kern(*refs):
        t_refs = refs[:n_t]
        r_hbm = refs[n_t:n_t + n_r]
        o_refs = refs[n_t + n_r:n_t + n_r + n_o]
        a_refs = refs[n_t + n_r + n_o:n_t + n_r + n_o + n_a]
        r_vmem = refs[n_t + n_r + n_o + n_a:n_t + 2 * n_r + n_o + n_a]
        s_refs = refs[n_t + 2 * n_r + n_o + n_a:]

        @pl.when(pl.program_id(0) == 0)
        def _():
            for h, v in zip(r_hbm, r_vmem):
                pltpu.sync_copy(h, v)
            for a in a_refs + s_refs:
                a[...] = jnp.zeros(a.shape, a.dtype)

        body(t_refs, r_vmem, o_refs, a_refs, s_refs)

    if reverse:
        idx = lambda i: (nt - 1 - i, 0)
        idx_t = lambda i: (nt - 1 - i, 0, 0)
    else:
        idx = lambda i: (i, 0)
        idx_t = lambda i: (i, 0, 0)
    arrays, in_specs = [], []
    for a in tiled:
        if isinstance(a, tuple):
            arrays.append(a[0])
            in_specs.append(pl.BlockSpec((None, a[0].shape[1], tm), idx_t))
        else:
            arrays.append(a)
            in_specs.append(pl.BlockSpec((tm, a.shape[1]), idx))
    in_specs += [pl.BlockSpec(memory_space=pl.ANY) for _ in resident]
    out_shape, out_specs = [], []
    for o in outs:
        if len(o) == 3:
            out_shape.append(jax.ShapeDtypeStruct((nt, o[0], tm), o[1]))
            out_specs.append(pl.BlockSpec((None, o[0], tm), idx_t))
        else:
            out_shape.append(jax.ShapeDtypeStruct((T, o[0]), o[1]))
            out_specs.append(pl.BlockSpec((tm, o[0]), idx))
    out_shape += [jax.ShapeDtypeStruct(s, F32) for s in accs]
    out_specs += [pl.BlockSpec(s, lambda i, n=len(s): (0,) * n) for s in accs]
    scratch_shapes = [pltpu.VMEM(r.shape, r.dtype) for r in resident]
    scratch_shapes += [pltpu.VMEM(s, F32) for s in scratch]
    return pl.pallas_call(
        kern, name=name, grid=(nt,), in_specs=in_specs, out_specs=out_specs, out_shape=out_shape,
        scratch_shapes=scratch_shapes, compiler_params=_params(vmem_mb),
    )(*arrays, *resident)


def _sg_forward(zu, zv, wm_ref, bsg, lng, lnb, mixed_ref, tm):
    gu, tu = _gelu(zu)
    vg, tv = _gelu(zv)
    mu = jnp.sum(vg, axis=-1, keepdims=True) * (1.0 / AW)
    xc = vg - mu
    rstd = lax.rsqrt(jnp.sum(xc * xc, axis=-1, keepdims=True) * (1.0 / AW) + EPS)
    xhat = xc * rstd
    vvb = (xhat * lng + lnb).astype(BF16)
    lane = lax.broadcasted_iota(jnp.int32, (CH, 128), 1)
    for c in range(tm // CH):
        for j in range(4):
            blk = vvb[c * CH:(c + 1) * CH, j * 128:(j + 1) * 128]
            m0 = _dot(wm_ref[2 * j], blk)
            m1 = _dot(wm_ref[2 * j + 1], blk)
            mixed_ref[c * CH:(c + 1) * CH, j * 128:(j + 1) * 128] = (
                jnp.where(lane < DH, m0, m1) + bsg[:, j * 128:(j + 1) * 128])
    return gu, tu, tv, xhat, rstd, vvb, mixed_ref[...]


def _head_consts():
    src = jnp.arange(AW)
    dst = (src // DH) * 128 + src % DH
    wide = jnp.arange(NH * 128)
    expand = (dst[:, None] == wide[None, :]).astype(BF16)
    heads = jnp.arange(128)
    pieces = jnp.stack([((heads[:, None] * 128 + DH + i == wide[None, :]) & (heads[:, None] < NH)).astype(BF16)
                        for i in range(3)])
    spare = wide % 128 - DH
    qconst = jnp.where((spare >= 0) & (spare < 3), -1.0, 0.0).astype(F32)[None, :]
    one64 = jnp.where(spare == 0, 1.0, 0.0).astype(F32)[None, :]
    pick64 = ((wide[:, None] == heads[None, :] * 128 + DH) & (heads[None, :] < NH)).astype(BF16)
    return expand, expand.T, pieces, qconst, one64, pick64


def _masked_sg_w(sg_w):
    r = lax.broadcasted_iota(jnp.int32, (CH, CH), 0)
    c = lax.broadcasted_iota(jnp.int32, (CH, CH), 1)
    return jnp.where((c <= r)[None], sg_w, 0.0)


def _pre_attn_fwd(x, gpre, win, fbias, lng, lnb, wm, bsg, gsg, expand, pieces, qconst):
    T = x.shape[0]
    tm = TM

    def body(t, r, o, a, s):
        (x_ref,) = t
        gpre_r, win_r, fb_r, lng_r, lnb_r, wm_r, bsg_r, gsg_r, ex_r, pc_r, qc_r = r
        a_o, qkv_o, flog_o, ccol_o, zuv_o, ysgn_o, q8_o, k8_o, v8_o = o
        carry_ref, mixed_ref = s
        xv = x_ref[...]
        av = (xv * _rs(xv, D) * gpre_r[...]).astype(BF16)
        a_o[...] = av
        z = _dot(av, win_r[...])
        zu = z[:, 3 * AW:4 * AW]
        zv = z[:, 4 * AW:5 * AW]
        zuv_o[:, 0:AW] = zu
        zuv_o[:, AW:2 * AW] = zv
        zf = z[:, 5 * AW:] + fb_r[...]
        flog_o[...] = zf
        lane = lax.broadcasted_iota(jnp.int32, (tm, 128), 1)
        logf = jnp.where(lane < NH, jnp.minimum(zf, 0.0) - jnp.log(1.0 + jnp.exp(-jnp.abs(zf))), 0.0)
        rr = lax.broadcasted_iota(jnp.int32, (tm, tm), 0)
        cc = lax.broadcasted_iota(jnp.int32, (tm, tm), 1)
        tri = (cc <= rr).astype(BF16)
        cum = _dot01(tri, logf) + carry_ref[...]
        carry_ref[...] = cum[tm - 1:tm, :]
        ccol_o[...] = cum
        ex = ex_r[...]
        q8_o[...] = (_dot((z[:, 0:AW] * (DH ** -0.5)).astype(BF16), ex) + qc_r[...]).astype(BF16)
        ch, cm, cl = _split3(cum)
        k8_o[...] = (_dot(z[:, AW:2 * AW].astype(BF16), ex) + _dot(ch, pc_r[0]) + _dot(cm, pc_r[1])
                     + _dot(cl, pc_r[2])).astype(BF16)
        v8_o[...] = _dot(z[:, 2 * AW:3 * AW].astype(BF16), ex).astype(BF16)
        qkv_o[:, 0:AW] = (z[:, 0:AW] * (DH ** -0.5)).astype(BF16)
        qkv_o[:, AW:3 * AW] = z[:, AW:3 * AW].astype(BF16)
        gu, _, _, _, _, _, mixed = _sg_forward(zu, zv, wm_r, bsg_r[...], lng_r[...], lnb_r[...], mixed_ref, tm)
        ysg = gu * mixed
        ysgn_o[...] = (ysg * _rs(ysg, AW) * gsg_r[...]).astype(BF16)

    return _row_call(
        "pre_attn_fwd", body, T, tm, [x], [gpre, win, fbias, lng, lnb, wm, bsg, gsg, expand, pieces, qconst],
        [(D, BF16), (3 * AW, BF16), (128, F32), (128, F32), (2 * AW, F32), (AW, BF16), (NH * 128, BF16),
         (NH * 128, BF16), (NH * 128, BF16)], [],
        scratch=[(1, 128), (tm, AW)], vmem_mb=48)


def _flash_fwd(qt8, k8, vt8, sel):
    T = k8.shape[0]
    nq = T // TQ

    def body(qt_ref, k_ref, vt_ref, sel_ref, o_ref, l_ref, u_scr, p_scr):
        qi = pl.program_id(1)
        qts = (qt_ref[0:128, :], qt_ref[128:256, :])
        dmat = (lax.broadcasted_iota(jnp.int32, (TQ, TQ), 0) - lax.broadcasted_iota(jnp.int32, (TQ, TQ), 1))
        u_scr[1] = jnp.full((2, TQ, TQ), MASKED, F32)
        p_scr[...] = jnp.zeros(p_scr.shape, BF16)

        def sub(t, carry, sc, sb):
            blk_c = jnp.clip(t - 2, 0, qi)
            off_a = pl.multiple_of(jnp.minimum(t, qi) * TQ, TQ)
            lim = (qi - t) * TQ
            new = []
            for j in (0, 1):
                m, al, acc = carry[j]
                acc = al * acc + _dot(vt_ref[blk_c, j * 128:(j + 1) * 128, :], p_scr[sc, j])
                m_new = jnp.maximum(m, jnp.max(u_scr[sb, j], axis=0, keepdims=True))
                p_scr[sb, j] = jnp.exp(u_scr[sb, j] - m_new).astype(BF16)
                u_scr[sc, j] = jnp.where(
                    dmat <= lim, _dot(k_ref[pl.ds(off_a, TQ), j * 128:(j + 1) * 128], qts[j]), MASKED)
                new.append((m_new, jnp.exp(m - m_new), acc))
            return tuple(new)

        def it(t2, carry):
            return sub(2 * t2 + 1, sub(2 * t2, carry, 0, 1), 1, 0)

        init = tuple((jnp.full((1, TQ), NEG, F32), jnp.ones((1, TQ), F32), jnp.zeros((128, TQ), F32))
                     for _ in (0, 1))
        (m0, _, a0), (m1, _, a1) = lax.fori_loop(0, (qi + 4) // 2, it, init)
        l0 = a0[DH:DH + 1, :]
        l1 = a1[DH:DH + 1, :]
        o_ref[...] = _dot01_tn(a0 * (1.0 / l0), sel_ref[0]) + _dot01_tn(a1 * (1.0 / l1), sel_ref[1])
        l_ref[0:1, :] = m0 + jnp.log(l0)
        l_ref[1:2, :] = m1 + jnp.log(l1)
        l_ref[2:8, :] = jnp.zeros((6, TQ), F32)

    return pl.pallas_call(
        body, name="flash_fwd", grid=(4, nq),
        in_specs=[pl.BlockSpec((None, 256, TQ), lambda h, i: (i, h, 0)),
                  pl.BlockSpec((T, 256), lambda h, i: (0, h)),
                  pl.BlockSpec((nq, 256, TQ), lambda h, i: (0, h, 0)),
                  pl.BlockSpec((2, 128, 128), lambda h, i: (0, 0, 0))],
        out_specs=[pl.BlockSpec((TQ, 128), lambda h, i: (i, h)),
                   pl.BlockSpec((None, 8, TQ), lambda h, i: (h, 0, i))],
        out_shape=[jax.ShapeDtypeStruct((T, AW), F32), jax.ShapeDtypeStruct((4, 8, T), F32)],
        scratch_shapes=[pltpu.VMEM((2, 2, TQ, TQ), F32), pltpu.VMEM((2, 2, TQ, TQ), BF16)],
        compiler_params=_params(40, ("arbitrary", "arbitrary")),
    )(qt8, k8, vt8, sel)


def _flash_bwd(q8, qt8, k8, kt8, v8, do8, dot8, lse, dlt):
    T = q8.shape[0]
    nk = T // TQ

    def body(q_ref, qt_ref, k_ref, kt_ref, v_ref, do_ref, dot_ref, l_ref, d_ref, dqt_ref, dk_ref, dv_ref,
             u_scr, dp_scr, p_scr, ds_scr):
        kb = pl.program_id(1)
        n = nk - kb

        @pl.when(kb == 0)
        def _():
            dqt_ref[...] = jnp.zeros(dqt_ref.shape, F32)

        dk_ref[...] = jnp.zeros(dk_ref.shape, F32)
        dv_ref[...] = jnp.zeros(dv_ref.shape, F32)
        u_scr[1] = jnp.full((2, TQ, TQ), MASKED, F32)
        dp_scr[1] = jnp.zeros((2, TQ, TQ), F32)
        p_scr[...] = jnp.zeros(p_scr.shape, BF16)
        ds_scr[...] = jnp.zeros(ds_scr.shape, BF16)
        dmat = (lax.broadcasted_iota(jnp.int32, (TQ, TQ), 0) - lax.broadcasted_iota(jnp.int32, (TQ, TQ), 1))
        ks = (k_ref[:, 0:128], k_ref[:, 128:256])
        vs = (v_ref[:, 0:128], v_ref[:, 128:256])
        kts = (kt_ref[0:128, :], kt_ref[128:256, :])

        def sub(t, sc, sb):
            blk_a = kb + jnp.minimum(t, n - 1)
            blk_c = kb + jnp.clip(t - 2, 0, n - 1)
            off_b = pl.multiple_of((kb + jnp.clip(t - 1, 0, n - 1)) * TQ, TQ)
            off_c = pl.multiple_of(blk_c * TQ, TQ)
            lim = jnp.where(t < n, t * TQ, -TQ)
            for j in (0, 1):
                hl = slice(j * 128, (j + 1) * 128)
                dqt_ref[blk_c, hl, :] += _dot(kts[j], ds_scr[sc, j])
                dk_ref[:, hl] += _dot(ds_scr[sc, j], q_ref[pl.ds(off_c, TQ), hl])
                dv_ref[:, hl] += _dot(p_scr[sc, j], do_ref[pl.ds(off_c, TQ), hl])
                p = jnp.exp(u_scr[sb, j] - l_ref[j:j + 1, pl.ds(off_b, TQ)])
                p_scr[sb, j] = p.astype(BF16)
                ds_scr[sb, j] = (p * (dp_scr[sb, j] - d_ref[j:j + 1, pl.ds(off_b, TQ)])).astype(BF16)
                u_scr[sc, j] = jnp.where(dmat <= lim, _dot(ks[j], qt_ref[blk_a, hl, :]), MASKED)
                dp_scr[sc, j] = _dot(vs[j], dot_ref[blk_a, hl, :])

        def it(t2, carry):
            sub(2 * t2, 0, 1)
            sub(2 * t2 + 1, 1, 0)
            return carry

        lax.fori_loop(0, (n + 3) // 2, it, 0)

    return pl.pallas_call(
        body, name="flash_bwd", grid=(4, nk),
        in_specs=[pl.BlockSpec((T, 256), lambda h, i: (0, h)),
                  pl.BlockSpec((nk, 256, TQ), lambda h, i: (0, h, 0)),
                  pl.BlockSpec((TQ, 256), lambda h, i: (i, h)),
                  pl.BlockSpec((None, 256, TQ), lambda h, i: (i, h, 0)),
                  pl.BlockSpec((TQ, 256), lambda h, i: (i, h)),
                  pl.BlockSpec((T, 256), lambda h, i: (0, h)),
                  pl.BlockSpec((nk, 256, TQ), lambda h, i: (0, h, 0)),
                  pl.BlockSpec((None, 8, T), lambda h, i: (h, 0, 0)),
                  pl.BlockSpec((None, 8, T), lambda h, i: (h, 0, 0))],
        out_specs=[pl.BlockSpec((nk, 256, TQ), lambda h, i: (0, h, 0)),
                   pl.BlockSpec((TQ, 256), lambda h, i: (i, h)),
                   pl.BlockSpec((TQ, 256), lambda h, i: (i, h))],
        out_shape=[jax.ShapeDtypeStruct((nk, NH * 128, TQ), F32), jax.ShapeDtypeStruct((T, NH * 128), F32),
                   jax.ShapeDtypeStruct((T, NH * 128), F32)],
        scratch_shapes=[pltpu.VMEM((2, 2, TQ, TQ), F32), pltpu.VMEM((2, 2, TQ, TQ), F32),
                        pltpu.VMEM((2, 2, TQ, TQ), BF16), pltpu.VMEM((2, 2, TQ, TQ), BF16)],
        compiler_params=_params(56, ("arbitrary", "arbitrary")),
    )(q8, qt8, k8, kt8, v8, do8, dot8, lse, dlt)


def _tail_fwd1(x, yatt, ysgn, gatt, wout, gpm, gpf, w1):
    T = x.shape[0]

    def body(t, r, o, a, s):
        x_ref, ya_ref, ys_ref = t
        gatt_r, wout_r, gpm_r, gpf_r, w1_r = r
        y_o, o_o, h1_o, c2_o, s_o, rr_o = o
        ya = ya_ref[...]
        yan = (ya * _rs(ya, AW) * gatt_r[...]).astype(BF16)
        y_o[:, 0:AW] = yan
        y_o[:, AW:] = ys_ref[...]
        ov = _dot(yan, wout_r[0:AW, :]) + _dot(ys_ref[...], wout_r[AW:, :])
        o_o[...] = ov
        h1 = x_ref[...] + ov * _rs(ov, D) * gpm_r[...]
        h1_o[...] = h1
        c2 = (h1 * _rs(h1, D) * gpf_r[...]).astype(BF16)
        c2_o[...] = c2
        rr = jnp.maximum(_dot(c2, w1_r[...]), 0.0)
        rr_o[...] = rr.astype(BF16)
        s_o[...] = (rr * rr).astype(BF16)

    return _row_call(
        "tail_fwd1", body, T, TM, [x, yatt, ysgn], [gatt, wout, gpm, gpf, w1],
        [(D, BF16), (D, F32), (D, F32), (D, BF16), (DFF, BF16), (DFF, BF16)], [], vmem_mb=48)


def _tail_fwd2(sact, h1, p, tgt, w2, gpff, wg, bg, wpe):
    T = h1.shape[0]

    def body(t, r, o, a, s):
        s_ref, h1_ref, p_ref, t_ref = t
        w2_r, gpff_r, wg_r, bg_r, wpe_r = r
        ff_o, h2b_o, de_o, dpre_o, dh2_o = o
        loss_a, dbg_a = a
        ff = _dot(s_ref[...], w2_r[...])
        ff_o[...] = ff
        h2 = h1_ref[...] + ff * _rs(ff, D) * gpff_r[...]
        h2b = h2.astype(BF16)
        h2b_o[...] = h2b
        gate = 1.0 / (1.0 + jnp.exp(-(_dot(h2b, wg_r[...]) + bg_r[...])))
        e = _dot(p_ref[...].astype(BF16), wpe_r[...])
        diff = h2 + gate * e - t_ref[...]
        loss_a[...] += jnp.sum(diff * diff, axis=0, keepdims=True)
        dh3 = diff * (1.0 / D)
        de_o[...] = (dh3 * gate).astype(BF16)
        dpre = dh3 * e * gate * (1.0 - gate)
        dbg_a[...] += jnp.sum(dpre, axis=0, keepdims=True)
        dpb = dpre.astype(BF16)
        dpre_o[...] = dpb
        dh2_o[...] = dh3 + _dot_nt(dpb, wg_r[...])

    return _row_call(
        "tail_fwd2", body, T, TM, [sact, h1, p, tgt], [w2, gpff, wg, bg, wpe],
        [(D, F32), (D, BF16), (D, BF16), (D, BF16), (D, F32)], [(1, D), (1, D)], vmem_mb=48)


def _tail_bwd(dh2, ff, rr, h1, ov, yatt, w2, w1, wout, gpff, gpf, gpm, gatt, hsel):
    T = dh2.shape[0]

    def body(t, r, o, a, s):
        dh2_ref, ff_ref, rr_ref, h1_ref, o_ref, ya_ref = t
        w2_r, w1_r, wout_r, gpff_r, gpf_r, gpm_r, gatt_r, hsel_r = r
        dff_o, dr_o, do_o, dya_o, dlt_o, dysg_o, dh1_o = o
        dgpff_a, dgpf_a, dgpm_a, dgatt_a = a
        dh2v = dh2_ref[...]
        ffv = ff_ref[...]
        dff, dg = _rms_bwd(dh2v, ffv, _rs(ffv, D), gpff_r[...], D)
        dgpff_a[...] += dg
        dffb = dff.astype(BF16)
        dff_o[...] = dffb
        drb = (_dot_nt(dffb, w2_r[...]) * (2.0 * rr_ref[...].astype(F32))).astype(BF16)
        dr_o[...] = drb
        dc2 = _dot_nt(drb, w1_r[...])
        h1v = h1_ref[...]
        d1, dg = _rms_bwd(dc2, h1v, _rs(h1v, D), gpf_r[...], D)
        dgpf_a[...] += dg
        dh1 = dh2v + d1
        dh1_o[...] = dh1
        ovv = o_ref[...]
        dov, dg = _rms_bwd(dh1, ovv, _rs(ovv, D), gpm_r[...], D)
        dgpm_a[...] += dg
        dob = dov.astype(BF16)
        do_o[...] = dob
        dysg_o[...] = _dot_nt(dob, wout_r[AW:, :])
        dyan = _dot_nt(dob, wout_r[0:AW, :])
        ya = ya_ref[...]
        dya, dg = _rms_bwd(dyan, ya, _rs(ya, AW), gatt_r[...], AW)
        dgatt_a[...] += dg
        dya_o[...] = dya.astype(BF16)
        dlt_o[...] = _dot01_r(dya * ya, hsel_r[...])

    return _row_call(
        "tail_bwd", body, T, TM, [dh2, ff, rr, h1, ov, yatt],
        [w2, w1, wout, gpff, gpf, gpm, gatt, hsel],
        [(D, BF16), (DFF, BF16), (D, BF16), (AW, BF16), (AW, F32), (AW, F32), (D, F32)],
        [(1, D), (1, D), (1, D), (1, AW)], vmem_mb=56)


def _pre_attn_bwd(x, dh1, dq, dk, dv, dccol, flog, zuv, dysg, gpre, win, lng, lnb, wm, wmt, bsg, gsg, gsel):
    T = x.shape[0]
    tm = TM

    def body(t, r, o, a, s):
        x_ref, dh1_ref, dq_ref, dk_ref, dv_ref, dc_ref, fl_ref, zuv_ref, dys_ref = t
        gpre_r, win_r, lng_r, lnb_r, wm_r, wmt_r, bsg_r, gsg_r, gsel_r = r
        dx_o, dz_o = o
        dgpre_a, dfb_a, dgsg_a, dlng_a, dlnb_a, dws_a, dbs_a, dsb_a = a
        carry_ref, mixed_ref, dvv_ref = s
        dcv = dc_ref[...]
        rr = lax.broadcasted_iota(jnp.int32, (tm, tm), 0)
        cc = lax.broadcasted_iota(jnp.int32, (tm, tm), 1)
        triu = (cc >= rr).astype(BF16)
        dlogf = _dot01(triu, dcv) + carry_ref[...]
        carry_ref[...] = dlogf[0:1, :]
        dzf = dlogf * (1.0 / (1.0 + jnp.exp(fl_ref[...])))
        dfb_a[...] += jnp.sum(dzf, axis=0, keepdims=True)
        dz_o[:, 5 * AW:] = dzf.astype(BF16)
        zu = zuv_ref[:, 0:AW]
        zv = zuv_ref[:, AW:]
        gu, tu, tv, xhat, rstd, vvb, mixed = _sg_forward(
            zu, zv, wm_r, bsg_r[...], lng_r[...], lnb_r[...], mixed_ref, tm)
        ysg = gu * mixed
        dysg_n = dys_ref[...]
        dys, dg = _rms_bwd(dysg_n, ysg, _rs(ysg, AW), gsg_r[...], AW)
        dgsg_a[...] += dg
        dgu = dys * mixed
        dmix = dys * gu
        dmb = dmix.astype(BF16)
        lane = lax.broadcasted_iota(jnp.int32, (CH, 128), 1)
        lo = lane < DH
        for c in range(tm // CH):
            rows = slice(c * CH, (c + 1) * CH)
            dbs_a[...] += dmix[rows, :]
            for j in range(4):
                cols = slice(j * 128, (j + 1) * 128)
                dmblk = dmb[rows, cols]
                vblk = vvb[rows, cols]
                d0 = _dot(wmt_r[2 * j], dmblk)
                d1 = _dot(wmt_r[2 * j + 1], dmblk)
                dvv_ref[rows, cols] = jnp.where(lo, d0, d1)
                dws_a[2 * j] += _dot_nt(jnp.where(lo, dmblk, jnp.zeros_like(dmblk)), vblk)
                dws_a[2 * j + 1] += _dot_nt(jnp.where(lo, jnp.zeros_like(dmblk), dmblk), vblk)
        dvv = dvv_ref[...]
        dlng_a[...] += jnp.sum(dvv * xhat, axis=0, keepdims=True)
        dlnb_a[...] += jnp.sum(dvv, axis=0, keepdims=True)
        dxh = dvv * lng_r[...]
        dvg = rstd * (dxh - jnp.sum(dxh, axis=-1, keepdims=True) * (1.0 / AW)
                      - xhat * (jnp.sum(dxh * xhat, axis=-1, keepdims=True) * (1.0 / AW)))
        dz_o[:, 3 * AW:4 * AW] = (dgu * _gelu_grad(zu, tu)).astype(BF16)
        dz_o[:, 4 * AW:5 * AW] = (dvg * _gelu_grad(zv, tv)).astype(BF16)
        dz_o[:, 0:AW] = (dq_ref[...] * (DH ** -0.5)).astype(BF16)
        dz_o[:, AW:2 * AW] = dk_ref[...].astype(BF16)
        dz_o[:, 2 * AW:3 * AW] = dv_ref[...].astype(BF16)
        da = _dot_nt(dz_o[...], win_r[...])
        xv = x_ref[...]
        dxa, dg = _rms_bwd(da, xv, _rs(xv, D), gpre_r[...], D)
        dgpre_a[...] += dg
        dx_o[...] = dh1_ref[...] + dxa

        @pl.when(pl.program_id(0) == T // tm - 1)
        def _():
            dsb_a[...] = _dot01_r(dbs_a[...], gsel_r[...])

    outs = _row_call(
        "pre_attn_bwd", body, T, tm, [x, dh1, dq, dk, dv, dccol, flog, zuv, dysg],
        [gpre, win, lng, lnb, wm, wmt, bsg, gsg, gsel],
        [(D, F32), (ZW, BF16)],
        [(1, D), (1, 128), (1, AW), (1, AW), (1, AW), (8, CH, CH), (CH, AW), (CH, 128)],
        scratch=[(1, 128), (tm, AW), (tm, AW)], reverse=True, vmem_mb=48)
    return outs


def _matmul_tn(name, a, b, tn=512, tt=512):
    T, K = a.shape
    N = b.shape[1]
    tk = min(K, 1024)
    tn = min(tn, N)
    tt = min(tt, T)

    def body(a_ref, b_ref, o_ref):
        @pl.when(pl.program_id(2) == 0)
        def _():
            o_ref[...] = jnp.zeros(o_ref.shape, F32)

        o_ref[...] += _dot_tn(a_ref[...].astype(BF16), b_ref[...].astype(BF16))

    return pl.pallas_call(
        body, name=name, grid=(K // tk, N // tn, T // tt),
        in_specs=[pl.BlockSpec((tt, tk), lambda i, j, t: (t, i)),
                  pl.BlockSpec((tt, tn), lambda i, j, t: (t, j))],
        out_specs=pl.BlockSpec((tk, tn), lambda i, j, t: (i, j)),
        out_shape=jax.ShapeDtypeStruct((K, N), F32),
        compiler_params=_params(32, ("arbitrary", "arbitrary", "arbitrary")),
    )(a, b)


def _me():
    return lax.axis_index("x"), lax.axis_index("y"), lax.axis_index("c")


HBM_SPEC = pl.BlockSpec(memory_space=pltpu.HBM)


def _gather_weights(mine):
    def body(mine_ref, out_ref, send_sems, recv_sems, local_sem):
        x, y, c = _me()
        k_me = 2 * x + y
        chips = [(1 - x, y), (x, 1 - y), (1 - x, 1 - y)]
        own = pltpu.make_async_copy(mine_ref, out_ref.at[k_me], local_sem)
        own.start()
        sends = [pltpu.make_async_remote_copy(
            src_ref=mine_ref, dst_ref=out_ref.at[k_me], send_sem=send_sems.at[j], recv_sem=recv_sems.at[j],
            device_id=(cx, cy, c), device_id_type=MESH) for j, (cx, cy) in enumerate(chips)]
        for s in sends:
            s.start()
        for j, (cx, cy) in enumerate(chips):
            pltpu.make_async_remote_copy(
                src_ref=mine_ref, dst_ref=out_ref.at[2 * cx + cy], send_sem=send_sems.at[j],
                recv_sem=recv_sems.at[j], device_id=(cx, cy, c), device_id_type=MESH).wait_recv()
        for s in sends:
            s.wait_send()
        own.wait()

    return pl.pallas_call(
        body, name="gather_weights", in_specs=[HBM_SPEC], out_specs=HBM_SPEC,
        out_shape=jax.ShapeDtypeStruct((4,) + mine.shape, mine.dtype),
        scratch_shapes=[pltpu.SemaphoreType.DMA((3,)), pltpu.SemaphoreType.DMA((3,)), pltpu.SemaphoreType.DMA],
    )(mine)


def _swap_halves(g):
    def body(g_ref, got_ref, send_sem, recv_sem):
        x, y, c = _me()
        theirs = pl.multiple_of((1 - c) * HALF_ROWS, 8)
        cp = pltpu.make_async_remote_copy(
            src_ref=g_ref.at[:, pl.ds(theirs, HALF_ROWS), :], dst_ref=got_ref, send_sem=send_sem,
            recv_sem=recv_sem, device_id=(x, y, 1 - c), device_id_type=MESH)
        cp.start()
        cp.wait()

    return pl.pallas_call(
        body, name="swap_halves", in_specs=[HBM_SPEC], out_specs=HBM_SPEC,
        out_shape=jax.ShapeDtypeStruct((4, HALF_ROWS, 1024), F32),
        scratch_shapes=[pltpu.SemaphoreType.DMA, pltpu.SemaphoreType.DMA],
    )(g)


def _pair_sum(mine_half, got):
    def body(a_ref, b_ref, o_ref):
        o_ref[...] = a_ref[...] + b_ref[...]

    spec = pl.BlockSpec((1, RED_ROWS, 1024), lambda k, i: (k, i, 0))
    return pl.pallas_call(
        body, name="pair_sum", grid=(4, HALF_ROWS // RED_ROWS), in_specs=[spec, spec], out_specs=spec,
        out_shape=jax.ShapeDtypeStruct(got.shape, F32),
        compiler_params=_params(32, ("arbitrary", "arbitrary")),
    )(mine_half, got)


def _exchange_chips(ps):
    def body(ps_ref, out_ref, send_sems, recv_sems, local_sem):
        x, y, c = _me()
        k_me = 2 * x + y
        chips = [(1 - x, y), (x, 1 - y), (1 - x, 1 - y)]
        own = pltpu.make_async_copy(ps_ref.at[k_me], out_ref.at[k_me], local_sem)
        own.start()
        sends = [pltpu.make_async_remote_copy(
            src_ref=ps_ref.at[2 * cx + cy], dst_ref=out_ref.at[k_me], send_sem=send_sems.at[j],
            recv_sem=recv_sems.at[j], device_id=(cx, cy, c), device_id_type=MESH)
            for j, (cx, cy) in enumerate(chips)]
        for s in sends:
            s.start()
        for j, (cx, cy) in enumerate(chips):
            pltpu.make_async_remote_copy(
                src_ref=ps_ref.at[k_me], dst_ref=out_ref.at[2 * cx + cy], send_sem=send_sems.at[j],
                recv_sem=recv_sems.at[j], device_id=(cx, cy, c), device_id_type=MESH).wait_recv()
        for s in sends:
            s.wait_send()
        own.wait()

    return pl.pallas_call(
        body, name="exchange_chips", in_specs=[HBM_SPEC], out_specs=HBM_SPEC,
        out_shape=jax.ShapeDtypeStruct(ps.shape, F32),
        scratch_shapes=[pltpu.SemaphoreType.DMA((3,)), pltpu.SemaphoreType.DMA((3,)), pltpu.SemaphoreType.DMA],
    )(ps)


def _adamw(w, g, m, v):
    m = B1 * m + (1.0 - B1) * g
    v = B2 * v + (1.0 - B2) * (g * g)
    delta = -LR * ((m / BC1) / (jnp.sqrt(v / BC2) + AEPS) + WD * w)
    return delta, m, v


def _reduce_chips(parts):
    def body(p_ref, o_ref):
        o_ref[...] = ((p_ref[0] + p_ref[1]) + p_ref[2]) + p_ref[3]

    return pl.pallas_call(
        body, name="reduce_chips", grid=(HALF_ROWS // RED_ROWS,),
        in_specs=[pl.BlockSpec((4, RED_ROWS, 1024), lambda i: (0, i, 0))],
        out_specs=pl.BlockSpec((RED_ROWS, 1024), lambda i: (i, 0)),
        out_shape=jax.ShapeDtypeStruct((HALF_ROWS, 1024), F32), compiler_params=_params(32),
    )(parts)


def _share_grad(gh):
    def body(g_ref, got_ref, send_sem, recv_sem):
        x, y, c = _me()
        cp = pltpu.make_async_remote_copy(
            src_ref=g_ref, dst_ref=got_ref, send_sem=send_sem, recv_sem=recv_sem,
            device_id=(x, y, 1 - c), device_id_type=MESH)
        cp.start()
        cp.wait()

    return pl.pallas_call(
        body, name="share_grad", in_specs=[HBM_SPEC], out_specs=HBM_SPEC,
        out_shape=jax.ShapeDtypeStruct((HALF_ROWS, 1024), F32),
        scratch_shapes=[pltpu.SemaphoreType.DMA, pltpu.SemaphoreType.DMA],
    )(gh)


def _update(g, w, m, v):
    def body(g_ref, w_ref, m_ref, v_ref, o_ref):
        delta, mn, vn = _adamw(w_ref[...], g_ref[...], m_ref[...], v_ref[...])
        o_ref[0] = delta
        o_ref[1] = mn
        o_ref[2] = vn

    s1 = pl.BlockSpec((RED_ROWS, 1024), lambda i: (i, 0))
    return pl.pallas_call(
        body, name="update", grid=(SHARD_ROWS // RED_ROWS,), in_specs=[s1, s1, s1, s1],
        out_specs=pl.BlockSpec((3, RED_ROWS, 1024), lambda i: (0, i, 0)),
        out_shape=jax.ShapeDtypeStruct((3, SHARD_ROWS, 1024), F32), compiler_params=_params(32),
    )(g, w, m, v)


def _small_allreduce_update(g, w, m, v):
    def body(g_ref, w_ref, m_ref, v_ref, o_ref, buf, send_sems, recv_sems):
        x, y, c = _me()
        me = 4 * x + 2 * y + c
        buf[me] = g_ref[...]
        rels = [(rx, ry, rc) for rx in (0, 1) for ry in (0, 1) for rc in (0, 1)][1:]

        def peer(r):
            return ((x + r[0]) % 2, (y + r[1]) % 2, (c + r[2]) % 2)

        sends = [pltpu.make_async_remote_copy(
            src_ref=g_ref, dst_ref=buf.at[me], send_sem=send_sems.at[j], recv_sem=recv_sems.at[j],
            device_id=peer(r), device_id_type=MESH) for j, r in enumerate(rels)]
        for s in sends:
            s.start()
        for j, r in enumerate(rels):
            px, py, pc = peer(r)
            pltpu.make_async_remote_copy(
                src_ref=g_ref, dst_ref=buf.at[4 * px + 2 * py + pc], send_sem=send_sems.at[j],
                recv_sem=recv_sems.at[j], device_id=peer(r), device_id_type=MESH).wait_recv()
        for s in sends:
            s.wait_send()
        tot = buf[0]
        for d in range(1, 8):
            tot = tot + buf[d]
        delta, mn, vn = _adamw(w_ref[...], tot, m_ref[...], v_ref[...])
        o_ref[0] = tot
        o_ref[1] = delta
        o_ref[2] = mn
        o_ref[3] = vn

    vm = pl.BlockSpec(memory_space=pltpu.VMEM)
    return pl.pallas_call(
        body, name="small_allreduce_update", in_specs=[vm, vm, vm, vm], out_specs=vm,
        out_shape=jax.ShapeDtypeStruct((4, SMALL_ROWS, 1024), F32),
        scratch_shapes=[pltpu.VMEM((8, SMALL_ROWS, 1024), F32), pltpu.SemaphoreType.DMA((7,)),
                        pltpu.SemaphoreType.DMA((7,))],
        compiler_params=pltpu.CompilerParams(vmem_limit_bytes=32 * 1024 * 1024),
    )(g, w, m, v)


def _pack_shard(w_in, w_out, w1, w2, plew, wg):
    return jnp.concatenate([
        jnp.pad(w_in, ((0, 0), (0, 768 - 642))).reshape(768, 1024), w_out, w1, w2,
        plew.reshape(64, 1024), wg], axis=0)


def _unpack_shard(pk):
    r = 0
    out = []
    for rows, shape in ((768, (1024, 768)), (256, (256, 1024)), (1024, (1024, 1024)), (1024, (1024, 1024)),
                        (64, (256, 256)), (256, (256, 1024))):
        out.append(pk[r:r + rows].reshape(shape))
        r += rows
    out[0] = out[0][:, :642]
    return out


def _full_weights(gathered):
    parts = [_unpack_shard(gathered[k]) for k in range(4)]
    w_in = jnp.concatenate([p[0] for p in parts], axis=1)
    w_in = jnp.concatenate([w_in[:, :3 * AW], w_in[:, 3 * AW + NH:], w_in[:, 3 * AW:3 * AW + NH],
                            jnp.zeros((D, 128 - NH), w_in.dtype)], axis=1)
    return (w_in, jnp.concatenate([p[1] for p in parts], axis=0), jnp.concatenate([p[2] for p in parts], axis=1),
            jnp.concatenate([p[3] for p in parts], axis=0), jnp.concatenate([p[4] for p in parts], axis=1),
            jnp.concatenate([p[5] for p in parts], axis=0))


def _pack_grads(dwin_k, dwout, dw1, dw2, dplew, dwg):
    dwin = jnp.concatenate([dwin_k[:, :3 * AW], dwin_k[:, 5 * AW:5 * AW + NH], dwin_k[:, 3 * AW:5 * AW]], axis=1)
    return jnp.stack([
        _pack_shard(dwin[:, 642 * k:642 * (k + 1)], dwout[256 * k:256 * (k + 1)], dw1[:, 1024 * k:1024 * (k + 1)],
                    dw2[1024 * k:1024 * (k + 1)], dplew[:, 256 * k:256 * (k + 1)], dwg[256 * k:256 * (k + 1)])
        for k in range(4)])


def _row1024(v):
    v = v.reshape(-1)
    return jnp.pad(v, (0, 1024 - v.shape[0])).reshape(1, 1024)


def _pack_small(sg_w, vecs, loss_row=None):
    rows = [sg_w.reshape(128, 1024)] + [_row1024(v) for v in vecs]
    rows.append(jnp.zeros((1, 1024), F32) if loss_row is None else loss_row)
    n = sum(r.shape[0] for r in rows)
    rows.append(jnp.zeros((SMALL_ROWS - n, 1024), F32))
    return jnp.concatenate(rows, axis=0)


def _local_step(x, p, tgt, win_k, wout, w1, w2, plew, wg, small):
    T = x.shape[0]
    row = lambda n: small[n].reshape(1, -1)
    fbias = jnp.pad(row("f_bias"), ((0, 0), (0, 128 - NH)))
    wm = _masked_sg_w(small["sg_w"].reshape(8, CH, CH))
    wmb = wm.astype(BF16)
    wmt = jnp.swapaxes(wm, 1, 2).astype(BF16)
    bsg = jnp.repeat(small["sg_b"].reshape(8, CH).T, DH, axis=1)
    ln_g, ln_b, gsg, gatt = row("sg_ln_g"), row("sg_ln_b"), row("sg_out_g"), row("att_out_g")
    gpre, gpm, gpf, gpff, bg = row("pre_mix_g"), row("post_mix_g"), row("pre_ffn_g"), row("post_ffn_g"), row("ple_gate_b")
    gsel = (jnp.arange(AW)[:, None] // DH == jnp.arange(128)[None, :]).astype(BF16)
    hsel = (jnp.arange(AW)[:, None] // DH == jnp.arange(AW)[None, :] // DH).astype(BF16)

    expand, _, pieces, qconst, _, _ = _head_consts()
    a, qkv, flog, ccol, zuv, ysgn, q8, k8, v8 = _pre_attn_fwd(
        x, gpre, win_k, fbias, ln_g, ln_b, wmb, bsg, gsg, expand, pieces, qconst)

    nt = T // TQ
    hd = lambda t, i: t[:, i * AW:(i + 1) * AW].reshape(T, NH, DH)
    zpad = lambda n: jnp.zeros((T, NH, n), BF16)
    one = jnp.ones((T, NH, 1), BF16)
    wide = lambda parts: jnp.concatenate(parts, axis=-1).reshape(T, NH * 128)
    slabs = lambda t: jnp.swapaxes(t.reshape(nt, TQ, NH * 128), 1, 2)
    qt8 = slabs(q8)
    kt8 = slabs(wide([hd(qkv, 1), one, zpad(63)]))
    vt8 = slabs(wide([hd(qkv, 2), one, zpad(63)]))
    lanes = jnp.arange(128)
    sel = jnp.stack([((lanes[:, None] == lanes[None, :] - DH * j) & (lanes[:, None] < DH)).astype(BF16)
                     for j in (0, 1)])

    yatt, lse = _flash_fwd(qt8, k8, vt8, sel)
    y, ov, h1, c2, sact, rr = _tail_fwd1(x, yatt, ysgn, gatt, wout, gpm, gpf, w1)
    ff, h2b, de, dpre, dh2, loss_l, dbg = _tail_fwd2(sact, h1, p, tgt, w2, gpff, wg, bg, plew)
    dff, dr, do, dya, dlt, dysg, dh1, dgpff, dgpf, dgpm, dgatt = _tail_bwd(
        dh2, ff, rr, h1, ov, yatt, w2, w1, wout, gpff, gpf, gpm, gatt, hsel)
    do8 = wide([hd(dya, 0), zpad(64)])
    dlt4 = jnp.pad(dlt[:, ::DH].T.reshape(4, 2, T), ((0, 0), (0, 6), (0, 0)))
    dqt, dk8, dv8 = _flash_bwd(q8, qt8, k8, kt8, v8, do8, slabs(do8), lse, dlt4)
    dq8 = jnp.swapaxes(dqt, 1, 2).reshape(T, NH, 128)
    dk8 = dk8.reshape(T, NH, 128)
    packed = lambda t: t[:, :, :DH].reshape(T, AW)
    dccol = jnp.pad(dq8[:, :, DH] + dk8[:, :, DH], ((0, 0), (0, 128 - NH)))
    dx, dz, dgpre, dfb, dgsg, dlng, dlnb, dws, _, dsbt = _pre_attn_bwd(
        x, dh1, packed(dq8), packed(dk8), packed(dv8.reshape(T, NH, 128)), dccol, flog, zuv, dysg,
        gpre, win_k, ln_g, ln_b, wmb, wmt, bsg, gsg, gsel)

    dwin_k = _matmul_tn("grad_w_in", a, dz, tn=384)
    dwout = _matmul_tn("grad_w_out", y, do)
    dw1 = _matmul_tn("grad_w_ff1", c2, dr)
    dw2 = _matmul_tn("grad_w_ff2", sact, dff)
    dwg = _matmul_tn("grad_ple_gate_w", h2b, dpre)
    dplew = _matmul_tn("grad_ple_w", p, de)

    dsb = dsbt[:, :8].T
    gsmall = {"sg_w": _masked_sg_w(dws), "f_bias": dfb[:, :NH], "sg_ln_g": dlng, "sg_ln_b": dlnb, "sg_b": dsb,
              "att_out_g": dgatt, "sg_out_g": dgsg, "pre_mix_g": dgpre, "post_mix_g": dgpm, "pre_ffn_g": dgpf,
              "post_ffn_g": dgpff, "ple_gate_b": dbg}
    return loss_l, dx, (dwin_k, dwout, dw1, dw2, dplew, dwg), gsmall


def kernel(x, p, w_in, f_bias, sg_ln_g, sg_ln_b, sg_w, sg_b, att_out_g, sg_out_g, w_out, pre_mix_g, post_mix_g, pre_ffn_g, post_ffn_g, w_ff1, w_ff2, ple_w, ple_gate_w, ple_gate_b, loss_target, m_w_in, m_f_bias, m_sg_ln_g, m_sg_ln_b, m_sg_w, m_sg_b, m_att_out_g, m_sg_out_g, m_w_out, m_pre_mix_g, m_post_mix_g, m_pre_ffn_g, m_post_ffn_g, m_w_ff1, m_w_ff2, m_ple_w, m_ple_gate_w, m_ple_gate_b, v_w_in, v_f_bias, v_sg_ln_g, v_sg_ln_b, v_sg_w, v_sg_b, v_att_out_g, v_sg_out_g, v_w_out, v_pre_mix_g, v_post_mix_g, v_pre_ffn_g, v_post_ffn_g, v_w_ff1, v_w_ff2, v_ple_w, v_ple_gate_w, v_ple_gate_b):
    c = lax.axis_index("c")
    big = lambda t: (t[0][0], t[1][0], t[2][0], t[3][0], t[4][0], t[5][0])
    w_big = big((w_in, w_out, w_ff1, w_ff2, ple_w, ple_gate_w))
    m_big = big((m_w_in, m_w_out, m_w_ff1, m_w_ff2, m_ple_w, m_ple_gate_w))
    v_big = big((v_w_in, v_w_out, v_w_ff1, v_w_ff2, v_ple_w, v_ple_gate_w))
    small = {"sg_w": sg_w, "f_bias": f_bias, "sg_ln_g": sg_ln_g, "sg_ln_b": sg_ln_b, "sg_b": sg_b,
             "att_out_g": att_out_g, "sg_out_g": sg_out_g, "pre_mix_g": pre_mix_g, "post_mix_g": post_mix_g,
             "pre_ffn_g": pre_ffn_g, "post_ffn_g": post_ffn_g, "ple_gate_b": ple_gate_b}
    m_small = {"sg_w": m_sg_w, "f_bias": m_f_bias, "sg_ln_g": m_sg_ln_g, "sg_ln_b": m_sg_ln_b, "sg_b": m_sg_b,
               "att_out_g": m_att_out_g, "sg_out_g": m_sg_out_g, "pre_mix_g": m_pre_mix_g,
               "post_mix_g": m_post_mix_g, "pre_ffn_g": m_pre_ffn_g, "post_ffn_g": m_post_ffn_g,
               "ple_gate_b": m_ple_gate_b}
    v_small = {"sg_w": v_sg_w, "f_bias": v_f_bias, "sg_ln_g": v_sg_ln_g, "sg_ln_b": v_sg_ln_b, "sg_b": v_sg_b,
               "att_out_g": v_att_out_g, "sg_out_g": v_sg_out_g, "pre_mix_g": v_pre_mix_g,
               "post_mix_g": v_post_mix_g, "pre_ffn_g": v_pre_ffn_g, "post_ffn_g": v_post_ffn_g,
               "ple_gate_b": v_ple_gate_b}

    w_pk = _pack_shard(*w_big)
    gathered = _gather_weights(w_pk.astype(BF16))
    win_k, wout, w1, w2, plew, wg = _full_weights(gathered)

    loss_l, dx, gbig, gsmall = _local_step(x[0], p[0, 0], loss_target[0], win_k, wout, w1, w2, plew, wg, small)

    gp = _pack_grads(*gbig)
    got = _swap_halves(gp)
    mine_half = lax.dynamic_slice_in_dim(gp, c * HALF_ROWS, HALF_ROWS, axis=1)
    parts = _exchange_chips(_pair_sum(mine_half, got))
    gh = _reduce_chips(parts)
    got = _share_grad(gh)
    g_full = jnp.where(c == 0, jnp.concatenate([gh, got], axis=0), jnp.concatenate([got, gh], axis=0))
    upd = _update(g_full, w_pk, _pack_shard(*m_big), _pack_shard(*v_big))
    big_out = [_unpack_shard(t) for t in (g_full, upd[0], upd[1], upd[2])]

    vec = lambda d: [d[n] for n in VEC_NAMES]
    loss_row = loss_l * (0.5 / D)
    res_s = _small_allreduce_update(
        _pack_small(gsmall["sg_w"], vec(gsmall), loss_row), _pack_small(small["sg_w"], vec(small)),
        _pack_small(m_small["sg_w"], vec(m_small)), _pack_small(v_small["sg_w"], vec(v_small)))
    loss = jnp.sum(res_s[0, LOSS_ROW])

    def small_out(i, name):
        ref = small[name]
        if name == "sg_w":
            return res_s[i, 0:128].reshape(ref.shape)
        r = 128 + VEC_NAMES.index(name)
        return res_s[i, r, :ref.size].reshape(ref.shape)

    order = ["w_in", "f_bias", "sg_ln_g", "sg_ln_b", "sg_w", "sg_b", "att_out_g", "sg_out_g", "w_out",
             "pre_mix_g", "post_mix_g", "pre_ffn_g", "post_ffn_g", "w_ff1", "w_ff2", "ple_w", "ple_gate_w",
             "ple_gate_b"]
    big_idx = {"w_in": 0, "w_out": 1, "w_ff1": 2, "w_ff2": 3, "ple_w": 4, "ple_gate_w": 5}
    outs = [loss, dx[None]]
    for i in range(4):
        for name in order:
            if name in big_idx:
                outs.append(big_out[i][big_idx[name]][None])
            else:
                outs.append(small_out(i, name))
    return tuple(outs)
```

```python
import math

import jax
import jax.numpy as jnp
from jax import lax
from jax.experimental import pallas as pl
from jax.experimental.pallas import tpu as pltpu

F32 = jnp.float32
BF16 = jnp.bfloat16
MESH = pl.DeviceIdType.MESH

D = 1024
DH = 64
NH = 8
AW = 512
CH = 128
DFF = 4096
ZW = 5 * AW + 128
EPS = 1e-6
NEG = -1e30
MASKED = -2e30

TM = 256
TQ = 256

LR, B1, B2, AEPS, WD, STEP = 0.001, 0.9, 0.999, 1e-08, 0.01, 10
BC1 = 1.0 - B1 ** STEP
BC2 = 1.0 - B2 ** STEP

SHARD_ROWS = 768 + 256 + 1024 + 1024 + 64 + 256
HALF_ROWS = SHARD_ROWS // 2
RED_ROWS = HALF_ROWS // 4
XCH_ROWS = HALF_ROWS // 2
VEC_NAMES = ("f_bias", "sg_ln_g", "sg_ln_b", "sg_b", "att_out_g", "sg_out_g", "pre_mix_g",
             "post_mix_g", "pre_ffn_g", "post_ffn_g", "ple_gate_b")


def _dot(a, b):
    return jnp.dot(a, b, preferred_element_type=F32)


def _dot_nt(a, b):
    return lax.dot_general(a, b, (((1,), (1,)), ((), ())), preferred_element_type=F32)


def _dot_tn(a, b):
    return lax.dot_general(a, b, (((0,), (0,)), ((), ())), preferred_element_type=F32)


def _split3(x):
    h = x.astype(BF16)
    r = x - h.astype(F32)
    m = r.astype(BF16)
    l = (r - m.astype(F32)).astype(BF16)
    return h, m, l


def _dot01(sel, x):
    h, m, l = _split3(x)
    return _dot(sel, h) + _dot(sel, m) + _dot(sel, l)


def _dot01_r(x, sel):
    h, m, l = _split3(x)
    return _dot(h, sel) + _dot(m, sel) + _dot(l, sel)


def _dot01_tn(x, sel):
    h, m, l = _split3(x)
    return _dot_tn(h, sel) + _dot_tn(m, sel) + _dot_tn(l, sel)


def _rs(x, n):
    return lax.rsqrt(jnp.sum(x * x, axis=-1, keepdims=True) * (1.0 / n) + EPS)


def _rms_bwd(dn, x, rs, g, n):
    w = dn * g
    dx = rs * w - x * ((rs * rs * rs) * (1.0 / n) * jnp.sum(w * x, axis=-1, keepdims=True))
    return dx, jnp.sum(dn * x * rs, axis=0, keepdims=True)


_GC = math.sqrt(2.0 / math.pi)


def _gelu(x):
    t = jnp.tanh(_GC * (x + 0.044715 * x * x * x))
    return 0.5 * x * (1.0 + t), t


def _gelu_grad(x, t):
    return 0.5 * (1.0 + t) + 0.5 * x * (1.0 - t * t) * (_GC * (1.0 + 3.0 * 0.044715 * x * x))


def _params(vmem_mb, sem=("arbitrary",)):
    return pltpu.CompilerParams(dimension_semantics=sem, vmem_limit_bytes=vmem_mb * 1024 * 1024)


def _row_call(name, body, T, tm, tiled, resident, outs, accs, scratch=(), reverse=False, vmem_mb=48):
    nt = T // tm
    n_t, n_r, n_o, n_a = len(tiled), len(resident), len(outs), len(accs)

    def kern(*refs):
        t_refs = refs[:n_t]
        r_hbm = refs[n_t:n_t + n_r]
        o_refs = refs[n_t + n_r:n_t + n_r + n_o]
        a_refs = refs[n_t + n_r + n_o:n_t + n_r + n_o + n_a]
        r_vmem = refs[n_t + n_r + n_o + n_a:n_t + 2 * n_r + n_o + n_a]
        s_refs = refs[n_t + 2 * n_r + n_o + n_a:]

        @pl.when(pl.program_id(0) == 0)
        def _():
            for h, v in zip(r_hbm, r_vmem):
                pltpu.sync_copy(h, v)
            for a in a_refs + s_refs:
                a[...] = jnp.zeros(a.shape, a.dtype)

        body(t_refs, r_vmem, o_refs, a_refs, s_refs)

    if reverse:
        idx = lambda i: (nt - 1 - i, 0)
        idx_t = lambda i: (nt - 1 - i, 0, 0)
    else:
        idx = lambda i: (i, 0)
        idx_t = lambda i: (i, 0, 0)
    arrays, in_specs = [], []
    for a in tiled:
        if isinstance(a, tuple):
            arrays.append(a[0])
            in_specs.append(pl.BlockSpec((None, a[0].shape[1], tm), idx_t))
        else:
            arrays.append(a)
            in_specs.append(pl.BlockSpec((tm, a.shape[1]), idx))
    in_specs += [pl.BlockSpec(memory_space=pl.ANY) for _ in resident]
    out_shape, out_specs = [], []
    for o in outs:
        if len(o) == 3:
            out_shape.append(jax.ShapeDtypeStruct((nt, o[0], tm), o[1]))
            out_specs.append(pl.BlockSpec((None, o[0], tm), idx_t))
        else:
            out_shape.append(jax.ShapeDtypeStruct((T, o[0]), o[1]))
            out_specs.append(pl.BlockSpec((tm, o[0]), idx))
    out_shape += [jax.ShapeDtypeStruct(s, F32) for s in accs]
    out_specs += [pl.BlockSpec(s, lambda i, n=len(s): (0,) * n) for s in accs]
    scratch_shapes = [pltpu.VMEM(r.shape, r.dtype) for r in resident]
    scratch_shapes += [pltpu.VMEM(s, F32) for s in scratch]
    return pl.pallas_call(
        kern, name=name, grid=(nt,), in_specs=in_specs, out_specs=out_specs, out_shape=out_shape,
        scratch_shapes=scratch_shapes, compiler_params=_params(vmem_mb),
    )(*arrays, *resident)


def _sg_forward(zu, zv, wm_ref, bsg, lng, lnb, mixed_ref, tm):
    gu, tu = _gelu(zu)
    vg, tv = _gelu(zv)
    mu = jnp.sum(vg, axis=-1, keepdims=True) * (1.0 / AW)
    xc = vg - mu
    rstd = lax.rsqrt(jnp.sum(xc * xc, axis=-1, keepdims=True) * (1.0 / AW) + EPS)
    xhat = xc * rstd
    vvb = (xhat * lng + lnb).astype(BF16)
    lane = lax.broadcasted_iota(jnp.int32, (CH, 128), 1)
    for c in range(tm // CH):
        for j in range(4):
            blk = vvb[c * CH:(c + 1) * CH, j * 128:(j + 1) * 128]
            m0 = _dot(wm_ref[2 * j], blk)
            m1 = _dot(wm_ref[2 * j + 1], blk)
            mixed_ref[c * CH:(c + 1) * CH, j * 128:(j + 1) * 128] = (
                jnp.where(lane < DH, m0, m1) + bsg[:, j * 128:(j + 1) * 128])
    return gu, tu, tv, xhat, rstd, vvb, mixed_ref[...]


def _head_consts():
    src = jnp.arange(AW)
    dst = (src // DH) * 128 + src % DH
    wide = jnp.arange(NH * 128)
    expand = (dst[:, None] == wide[None, :]).astype(BF16)
    heads = jnp.arange(128)
    pieces = jnp.stack([((heads[:, None] * 128 + DH + i == wide[None, :]) & (heads[:, None] < NH)).astype(BF16)
                        for i in range(3)])
    spare = wide % 128 - DH
    qconst = jnp.where((spare >= 0) & (spare < 3), -1.0, 0.0).astype(F32)[None, :]
    one64 = jnp.where(spare == 0, 1.0, 0.0).astype(F32)[None, :]
    pick64 = ((wide[:, None] == heads[None, :] * 128 + DH) & (heads[None, :] < NH)).astype(BF16)
    return expand, expand.T, pieces, qconst, one64, pick64


def _masked_sg_w(sg_w):
    r = lax.broadcasted_iota(jnp.int32, (CH, CH), 0)
    c = lax.broadcasted_iota(jnp.int32, (CH, CH), 1)
    return jnp.where((c <= r)[None], sg_w, 0.0)


def _pre_attn_fwd(x, gpre, win, fbias, lng, lnb, wm, bsg, gsg, expand, pieces, qconst):
    T = x.shape[0]
    tm = TM

    def body(t, r, o, a, s):
        (x_ref,) = t
        gpre_r, win_r, fb_r, lng_r, lnb_r, wm_r, bsg_r, gsg_r, ex_r, pc_r, qc_r = r
        a_o, qkv_o, flog_o, ccol_o, zuv_o, ysgn_o, q8_o, k8_o, v8_o = o
        carry_ref, mixed_ref = s
        xv = x_ref[...]
        av = (xv * _rs(xv, D) * gpre_r[...]).astype(BF16)
        a_o[...] = av
        z = _dot(av, win_r[...])
        zu = z[:, 3 * AW:4 * AW]
        zv = z[:, 4 * AW:5 * AW]
        zuv_o[:, 0:AW] = zu
        zuv_o[:, AW:2 * AW] = zv
        zf = z[:, 5 * AW:] + fb_r[...]
        flog_o[...] = zf
        lane = lax.broadcasted_iota(jnp.int32, (tm, 128), 1)
        logf = jnp.where(lane < NH, jnp.minimum(zf, 0.0) - jnp.log(1.0 + jnp.exp(-jnp.abs(zf))), 0.0)
        rr = lax.broadcasted_iota(jnp.int32, (tm, tm), 0)
        cc = lax.broadcasted_iota(jnp.int32, (tm, tm), 1)
        tri = (cc <= rr).astype(BF16)
        cum = _dot01(tri, logf) + carry_ref[...]
        carry_ref[...] = cum[tm - 1:tm, :]
        ccol_o[...] = cum
        ex = ex_r[...]
        q8_o[...] = (_dot((z[:, 0:AW] * (DH ** -0.5)).astype(BF16), ex) + qc_r[...]).astype(BF16)
        ch, cm, cl = _split3(cum)
        k8_o[...] = (_dot(z[:, AW:2 * AW].astype(BF16), ex) + _dot(ch, pc_r[0]) + _dot(cm, pc_r[1])
                     + _dot(cl, pc_r[2])).astype(BF16)
        v8_o[...] = _dot(z[:, 2 * AW:3 * AW].astype(BF16), ex).astype(BF16)
        qkv_o[:, 0:AW] = (z[:, 0:AW] * (DH ** -0.5)).astype(BF16)
        qkv_o[:, AW:3 * AW] = z[:, AW:3 * AW].astype(BF16)
        gu, _, _, _, _, _, mixed = _sg_forward(zu, zv, wm_r, bsg_r[...], lng_r[...], lnb_r[...], mixed_ref, tm)
        ysg = gu * mixed
        ysgn_o[...] = (ysg * _rs(ysg, AW) * gsg_r[...]).astype(BF16)

    return _row_call(
        "pre_attn_fwd", body, T, tm, [x], [gpre, win, fbias, lng, lnb, wm, bsg, gsg, expand, pieces, qconst],
        [(D, BF16), (3 * AW, BF16), (128, F32), (128, F32), (2 * AW, F32), (AW, BF16), (NH * 128, BF16),
         (NH * 128, BF16), (NH * 128, BF16)], [],
        scratch=[(1, 128), (tm, AW)], vmem_mb=48)


def _flash_fwd(qt8, k8, vt8, sel):
    T = k8.shape[0]
    nq = T // TQ

    def body(qt_ref, k_ref, vt_ref, sel_ref, o_ref, l_ref, u_scr, p_scr):
        qi = pl.program_id(1)
        qts = (qt_ref[0:128, :], qt_ref[128:256, :])
        dmat = (lax.broadcasted_iota(jnp.int32, (TQ, TQ), 0) - lax.broadcasted_iota(jnp.int32, (TQ, TQ), 1))
        u_scr[1] = jnp.full((2, TQ, TQ), MASKED, F32)
        p_scr[...] = jnp.zeros(p_scr.shape, BF16)

        def sub(t, carry, sc, sb):
            blk_c = jnp.clip(t - 2, 0, qi)
            off_a = pl.multiple_of(jnp.minimum(t, qi) * TQ, TQ)
            lim = (qi - t) * TQ
            new = []
            for j in (0, 1):
                m, al, acc = carry[j]
                acc = al * acc + _dot(vt_ref[blk_c, j * 128:(j + 1) * 128, :], p_scr[sc, j])
                m_new = jnp.maximum(m, jnp.max(u_scr[sb, j], axis=0, keepdims=True))
                p_scr[sb, j] = jnp.exp(u_scr[sb, j] - m_new).astype(BF16)
                u_scr[sc, j] = jnp.where(
                    dmat <= lim, _dot(k_ref[pl.ds(off_a, TQ), j * 128:(j + 1) * 128], qts[j]), MASKED)
                new.append((m_new, jnp.exp(m - m_new), acc))
            return tuple(new)

        def it(t2, carry):
            return sub(2 * t2 + 1, sub(2 * t2, carry, 0, 1), 1, 0)

        init = tuple((jnp.full((1, TQ), NEG, F32), jnp.ones((1, TQ), F32), jnp.zeros((128, TQ), F32))
                     for _ in (0, 1))
        (m0, _, a0), (m1, _, a1) = lax.fori_loop(0, (qi + 4) // 2, it, init)
        l0 = a0[DH:DH + 1, :]
        l1 = a1[DH:DH + 1, :]
        o_ref[...] = _dot01_tn(a0 * (1.0 / l0), sel_ref[0]) + _dot01_tn(a1 * (1.0 / l1), sel_ref[1])
        l_ref[0:1, :] = m0 + jnp.log(l0)
        l_ref[1:2, :] = m1 + jnp.log(l1)
        l_ref[2:8, :] = jnp.zeros((6, TQ), F32)

    return pl.pallas_call(
        body, name="flash_fwd", grid=(4, nq),
        in_specs=[pl.BlockSpec((None, 256, TQ), lambda h, i: (i, h, 0)),
                  pl.BlockSpec((T, 256), lambda h, i: (0, h)),
                  pl.BlockSpec((nq, 256, TQ), lambda h, i: (0, h, 0)),
                  pl.BlockSpec((2, 128, 128), lambda h, i: (0, 0, 0))],
        out_specs=[pl.BlockSpec((TQ, 128), lambda h, i: (i, h)),
                   pl.BlockSpec((None, 8, TQ), lambda h, i: (h, 0, i))],
        out_shape=[jax.ShapeDtypeStruct((T, AW), F32), jax.ShapeDtypeStruct((4, 8, T), F32)],
        scratch_shapes=[pltpu.VMEM((2, 2, TQ, TQ), F32), pltpu.VMEM((2, 2, TQ, TQ), BF16)],
        compiler_params=_params(40, ("arbitrary", "arbitrary")),
    )(qt8, k8, vt8, sel)


def _flash_bwd(q8, qt8, k8, kt8, v8, do8, dot8, lse, dlt):
    T = q8.shape[0]
    nk = T // TQ

    def body(q_ref, qt_ref, k_ref, kt_ref, v_ref, do_ref, dot_ref, l_ref, d_ref, dqt_ref, dk_ref, dv_ref,
             u_scr, dp_scr, p_scr, ds_scr):
        kb = pl.program_id(1)
        n = nk - kb

        @pl.when(kb == 0)
        def _():
            dqt_ref[...] = jnp.zeros(dqt_ref.shape, F32)

        dk_ref[...] = jnp.zeros(dk_ref.shape, F32)
        dv_ref[...] = jnp.zeros(dv_ref.shape, F32)
        u_scr[1] = jnp.full((2, TQ, TQ), MASKED, F32)
        dp_scr[1] = jnp.zeros((2, TQ, TQ), F32)
        p_scr[...] = jnp.zeros(p_scr.shape, BF16)
        ds_scr[...] = jnp.zeros(ds_scr.shape, BF16)
        dmat = (lax.broadcasted_iota(jnp.int32, (TQ, TQ), 0) - lax.broadcasted_iota(jnp.int32, (TQ, TQ), 1))
        ks = (k_ref[:, 0:128], k_ref[:, 128:256])
        vs = (v_ref[:, 0:128], v_ref[:, 128:256])
        kts = (kt_ref[0:128, :], kt_ref[128:256, :])

        def sub(t, sc, sb):
            blk_a = kb + jnp.minimum(t, n - 1)
            blk_c = kb + jnp.clip(t - 2, 0, n - 1)
            off_b = pl.multiple_of((kb + jnp.clip(t - 1, 0, n - 1)) * TQ, TQ)
            off_c = pl.multiple_of(blk_c * TQ, TQ)
            lim = jnp.where(t < n, t * TQ, -TQ)
            for j in (0, 1):
                hl = slice(j * 128, (j + 1) * 128)
                dqt_ref[blk_c, hl, :] += _dot(kts[j], ds_scr[sc, j])
                dk_ref[:, hl] += _dot(ds_scr[sc, j], q_ref[pl.ds(off_c, TQ), hl])
                dv_ref[:, hl] += _dot(p_scr[sc, j], do_ref[pl.ds(off_c, TQ), hl])
                p = jnp.exp(u_scr[sb, j] - l_ref[j:j + 1, pl.ds(off_b, TQ)])
                p_scr[sb, j] = p.astype(BF16)
                ds_scr[sb, j] = (p * (dp_scr[sb, j] - d_ref[j:j + 1, pl.ds(off_b, TQ)])).astype(BF16)
                u_scr[sc, j] = jnp.where(dmat <= lim, _dot(ks[j], qt_ref[blk_a, hl, :]), MASKED)
                dp_scr[sc, j] = _dot(vs[j], dot_ref[blk_a, hl, :])

        def it(t2, carry):
            sub(2 * t2, 0, 1)
            sub(2 * t2 + 1, 1, 0)
            return carry

        lax.fori_loop(0, (n + 3) // 2, it, 0)

    return pl.pallas_call(
        body, name="flash_bwd", grid=(4, nk),
        in_specs=[pl.BlockSpec((T, 256), lambda h, i: (0, h)),
                  pl.BlockSpec((nk, 256, TQ), lambda h, i: (0, h, 0)),
                  pl.BlockSpec((TQ, 256), lambda h, i: (i, h)),
                  pl.BlockSpec((None, 256, TQ), lambda h, i: (i, h, 0)),
                  pl.BlockSpec((TQ, 256), lambda h, i: (i, h)),
                  pl.BlockSpec((T, 256), lambda h, i: (0, h)),
                  pl.BlockSpec((nk, 256, TQ), lambda h, i: (0, h, 0)),
                  pl.BlockSpec((None, 8, T), lambda h, i: (h, 0, 0)),
                  pl.BlockSpec((None, 8, T), lambda h, i: (h, 0, 0))],
        out_specs=[pl.BlockSpec((nk, 256, TQ), lambda h, i: (0, h, 0)),
                   pl.BlockSpec((TQ, 256), lambda h, i: (i, h)),
                   pl.BlockSpec((TQ, 256), lambda h, i: (i, h))],
        out_shape=[jax.ShapeDtypeStruct((nk, NH * 128, TQ), F32), jax.ShapeDtypeStruct((T, NH * 128), F32),
                   jax.ShapeDtypeStruct((T, NH * 128), F32)],
        scratch_shapes=[pltpu.VMEM((2, 2, TQ, TQ), F32), pltpu.VMEM((2, 2, TQ, TQ), F32),
                        pltpu.VMEM((2, 2, TQ, TQ), BF16), pltpu.VMEM((2, 2, TQ, TQ), BF16)],
        compiler_params=_params(56, ("arbitrary", "arbitrary")),
    )(q8, qt8, k8, kt8, v8, do8, dot8, lse, dlt)


def _tail_fwd1(x, yatt, ysgn, gatt, wout, gpm, gpf, w1):
    T = x.shape[0]

    def body(t, r, o, a, s):
        x_ref, ya_ref, ys_ref = t
        gatt_r, wout_r, gpm_r, gpf_r, w1_r = r
        y_o, o_o, h1_o, c2_o, s_o, rr_o = o
        ya = ya_ref[...]
        yan = (ya * _rs(ya, AW) * gatt_r[...]).astype(BF16)
        y_o[:, 0:AW] = yan
        y_o[:, AW:] = ys_ref[...]
        ov = _dot(yan, wout_r[0:AW, :]) + _dot(ys_ref[...], wout_r[AW:, :])
        o_o[...] = ov
        h1 = x_ref[...] + ov * _rs(ov, D) * gpm_r[...]
        h1_o[...] = h1
        c2 = (h1 * _rs(h1, D) * gpf_r[...]).astype(BF16)
        c2_o[...] = c2
        rr = jnp.maximum(_dot(c2, w1_r[...]), 0.0)
        rr_o[...] = rr.astype(BF16)
        s_o[...] = (rr * rr).astype(BF16)

    return _row_call(
        "tail_fwd1", body, T, TM, [x, yatt, ysgn], [gatt, wout, gpm, gpf, w1],
        [(D, BF16), (D, F32), (D, F32), (D, BF16), (DFF, BF16), (DFF, BF16)], [], vmem_mb=48)


def _tail_fwd2(sact, h1, p, tgt, w2, gpff, wg, bg, wpe):
    T = h1.shape[0]

    def body(t, r, o, a, s):
        s_ref, h1_ref, p_ref, t_ref = t
        w2_r, gpff_r, wg_r, bg_r, wpe_r = r
        ff_o, h2b_o, de_o, dpre_o, dh2_o = o
        loss_a, dbg_a = a
        ff = _dot(s_ref[...], w2_r[...])
        ff_o[...] = ff
        h2 = h1_ref[...] + ff * _rs(ff, D) * gpff_r[...]
        h2b = h2.astype(BF16)
        h2b_o[...] = h2b
        gate = 1.0 / (1.0 + jnp.exp(-(_dot(h2b, wg_r[...]) + bg_r[...])))
        e = _dot(p_ref[...].astype(BF16), wpe_r[...])
        diff = h2 + gate * e - t_ref[...]
        loss_a[...] += jnp.sum(diff * diff, axis=0, keepdims=True)
        dh3 = diff * (1.0 / D)
        de_o[...] = (dh3 * gate).astype(BF16)
        dpre = dh3 * e * gate * (1.0 - gate)
        dbg_a[...] += jnp.sum(dpre, axis=0, keepdims=True)
        dpb = dpre.astype(BF16)
        dpre_o[...] = dpb
        dh2_o[...] = dh3 + _dot_nt(dpb, wg_r[...])

    return _row_call(
        "tail_fwd2", body, T, TM, [sact, h1, p, tgt], [w2, gpff, wg, bg, wpe],
        [(D, F32), (D, BF16), (D, BF16), (D, BF16), (D, F32)], [(1, D), (1, D)], vmem_mb=48)


def _tail_bwd(dh2, ff, rr, h1, ov, yatt, w2, w1, wout, gpff, gpf, gpm, gatt, hsel):
    T = dh2.shape[0]

    def body(t, r, o, a, s):
        dh2_ref, ff_ref, rr_ref, h1_ref, o_ref, ya_ref = t
        w2_r, w1_r, wout_r, gpff_r, gpf_r, gpm_r, gatt_r, hsel_r = r
        dff_o, dr_o, do_o, dya_o, dlt_o, dysg_o, dh1_o = o
        dgpff_a, dgpf_a, dgpm_a, dgatt_a = a
        dh2v = dh2_ref[...]
        ffv = ff_ref[...]
        dff, dg = _rms_bwd(dh2v, ffv, _rs(ffv, D), gpff_r[...], D)
        dgpff_a[...] += dg
        dffb = dff.astype(BF16)
        dff_o[...] = dffb
        drb = (_dot_nt(dffb, w2_r[...]) * (2.0 * rr_ref[...].astype(F32))).astype(BF16)
        dr_o[...] = drb
        dc2 = _dot_nt(drb, w1_r[...])
        h1v = h1_ref[...]
        d1, dg = _rms_bwd(dc2, h1v, _rs(h1v, D), gpf_r[...], D)
        dgpf_a[...] += dg
        dh1 = dh2v + d1
        dh1_o[...] = dh1
        ovv = o_ref[...]
        dov, dg = _rms_bwd(dh1, ovv, _rs(ovv, D), gpm_r[...], D)
        dgpm_a[...] += dg
        dob = dov.astype(BF16)
        do_o[...] = dob
        dysg_o[...] = _dot_nt(dob, wout_r[AW:, :])
        dyan = _dot_nt(dob, wout_r[0:AW, :])
        ya = ya_ref[...]
        dya, dg = _rms_bwd(dyan, ya, _rs(ya, AW), gatt_r[...], AW)
        dgatt_a[...] += dg
        dya_o[...] = dya.astype(BF16)
        dlt_o[...] = _dot01_r(dya * ya, hsel_r[...])

    return _row_call(
        "tail_bwd", body, T, TM, [dh2, ff, rr, h1, ov, yatt],
        [w2, w1, wout, gpff, gpf, gpm, gatt, hsel],
        [(D, BF16), (DFF, BF16), (D, BF16), (AW, BF16), (AW, F32), (AW, F32), (D, F32)],
        [(1, D), (1, D), (1, D), (1, AW)], vmem_mb=56)


def _pre_attn_bwd(x, dh1, dq, dk, dv, dccol, flog, zuv, dysg, gpre, win, lng, lnb, wm, wmt, bsg, gsg, gsel):
    T = x.shape[0]
    tm = TM

    def body(t, r, o, a, s):
        x_ref, dh1_ref, dq_ref, dk_ref, dv_ref, dc_ref, fl_ref, zuv_ref, dys_ref = t
        gpre_r, win_r, lng_r, lnb_r, wm_r, wmt_r, bsg_r, gsg_r, gsel_r = r
        dx_o, dz_o = o
        dgpre_a, dfb_a, dgsg_a, dlng_a, dlnb_a, dws_a, dbs_a, dsb_a = a
        carry_ref, mixed_ref, dvv_ref = s
        dcv = dc_ref[...]
        rr = lax.broadcasted_iota(jnp.int32, (tm, tm), 0)
        cc = lax.broadcasted_iota(jnp.int32, (tm, tm), 1)
        triu = (cc >= rr).astype(BF16)
        dlogf = _dot01(triu, dcv) + carry_ref[...]
        carry_ref[...] = dlogf[0:1, :]
        dzf = dlogf * (1.0 / (1.0 + jnp.exp(fl_ref[...])))
        dfb_a[...] += jnp.sum(dzf, axis=0, keepdims=True)
        dz_o[:, 5 * AW:] = dzf.astype(BF16)
        zu = zuv_ref[:, 0:AW]
        zv = zuv_ref[:, AW:]
        gu, tu, tv, xhat, rstd, vvb, mixed = _sg_forward(
            zu, zv, wm_r, bsg_r[...], lng_r[...], lnb_r[...], mixed_ref, tm)
        ysg = gu * mixed
        dysg_n = dys_ref[...]
        dys, dg = _rms_bwd(dysg_n, ysg, _rs(ysg, AW), gsg_r[...], AW)
        dgsg_a[...] += dg
        dgu = dys * mixed
        dmix = dys * gu
        dmb = dmix.astype(BF16)
        lane = lax.broadcasted_iota(jnp.int32, (CH, 128), 1)
        lo = lane < DH
        for c in range(tm // CH):
            rows = slice(c * CH, (c + 1) * CH)
            dbs_a[...] += dmix[rows, :]
            for j in range(4):
                cols = slice(j * 128, (j + 1) * 128)
                dmblk = dmb[rows, cols]
                vblk = vvb[rows, cols]
                d0 = _dot(wmt_r[2 * j], dmblk)
                d1 = _dot(wmt_r[2 * j + 1], dmblk)
                dvv_ref[rows, cols] = jnp.where(lo, d0, d1)
                dws_a[2 * j] += _dot_nt(jnp.where(lo, dmblk, jnp.zeros_like(dmblk)), vblk)
                dws_a[2 * j + 1] += _dot_nt(jnp.where(lo, jnp.zeros_like(dmblk), dmblk), vblk)
        dvv = dvv_ref[...]
        dlng_a[...] += jnp.sum(dvv * xhat, axis=0, keepdims=True)
        dlnb_a[...] += jnp.sum(dvv, axis=0, keepdims=True)
        dxh = dvv * lng_r[...]
        dvg = rstd * (dxh - jnp.sum(dxh, axis=-1, keepdims=True) * (1.0 / AW)
                      - xhat * (jnp.sum(dxh * xhat, axis=-1, keepdims=True) * (1.0 / AW)))
        dz_o[:, 3 * AW:4 * AW] = (dgu * _gelu_grad(zu, tu)).astype(BF16)
        dz_o[:, 4 * AW:5 * AW] = (dvg * _gelu_grad(zv, tv)).astype(BF16)
        dz_o[:, 0:AW] = (dq_ref[...] * (DH ** -0.5)).astype(BF16)
        dz_o[:, AW:2 * AW] = dk_ref[...].astype(BF16)
        dz_o[:, 2 * AW:3 * AW] = dv_ref[...].astype(BF16)
        da = _dot_nt(dz_o[...], win_r[...])
        xv = x_ref[...]
        dxa, dg = _rms_bwd(da, xv, _rs(xv, D), gpre_r[...], D)
        dgpre_a[...] += dg
        dx_o[...] = dh1_ref[...] + dxa

        @pl.when(pl.program_id(0) == T // tm - 1)
        def _():
            dsb_a[...] = _dot01_r(dbs_a[...], gsel_r[...])

    outs = _row_call(
        "pre_attn_bwd", body, T, tm, [x, dh1, dq, dk, dv, dccol, flog, zuv, dysg],
        [gpre, win, lng, lnb, wm, wmt, bsg, gsg, gsel],
        [(D, F32), (ZW, BF16)],
        [(1, D), (1, 128), (1, AW), (1, AW), (1, AW), (8, CH, CH), (CH, AW), (CH, 128)],
        scratch=[(1, 128), (tm, AW), (tm, AW)], reverse=True, vmem_mb=48)
    return outs


def _matmul_tn(name, a, b, tn=512, tt=512):
    T, K = a.shape
    N = b.shape[1]
    tk = min(K, 1024)
    tn = min(tn, N)
    tt = min(tt, T)

    def body(a_ref, b_ref, o_ref):
        @pl.when(pl.program_id(2) == 0)
        def _():
            o_ref[...] = jnp.zeros(o_ref.shape, F32)

        o_ref[...] += _dot_tn(a_ref[...].astype(BF16), b_ref[...].astype(BF16))

    return pl.pallas_call(
        body, name=name, grid=(K // tk, N // tn, T // tt),
        in_specs=[pl.BlockSpec((tt, tk), lambda i, j, t: (t, i)),
                  pl.BlockSpec((tt, tn), lambda i, j, t: (t, j))],
        out_specs=pl.BlockSpec((tk, tn), lambda i, j, t: (i, j)),
        out_shape=jax.ShapeDtypeStruct((K, N), F32),
        compiler_params=_params(32, ("arbitrary", "arbitrary", "arbitrary")),
    )(a, b)


def _me():
    return lax.axis_index("x"), lax.axis_index("y"), lax.axis_index("c")


HBM_SPEC = pl.BlockSpec(memory_space=pltpu.HBM)


def _gather_weights(mine):
    def body(mine_ref, out_ref, send_sems, recv_sems, local_sem):
        x, y, c = _me()
        k_me = 2 * x + y
        chips = [(1 - x, y), (x, 1 - y), (1 - x, 1 - y)]
        own = pltpu.make_async_copy(mine_ref, out_ref.at[k_me], local_sem)
        own.start()
        sends = [pltpu.make_async_remote_copy(
            src_ref=mine_ref, dst_ref=out_ref.at[k_me], send_sem=send_sems.at[j], recv_sem=recv_sems.at[j],
            device_id=(cx, cy, c), device_id_type=MESH) for j, (cx, cy) in enumerate(chips)]
        for s in sends:
            s.start()
        for j, (cx, cy) in enumerate(chips):
            pltpu.make_async_remote_copy(
                src_ref=mine_ref, dst_ref=out_ref.at[2 * cx + cy], send_sem=send_sems.at[j],
                recv_sem=recv_sems.at[j], device_id=(cx, cy, c), device_id_type=MESH).wait_recv()
        for s in sends:
            s.wait_send()
        own.wait()

    return pl.pallas_call(
        body, name="gather_weights", in_specs=[HBM_SPEC], out_specs=HBM_SPEC,
        out_shape=jax.ShapeDtypeStruct((4,) + mine.shape, mine.dtype),
        scratch_shapes=[pltpu.SemaphoreType.DMA((3,)), pltpu.SemaphoreType.DMA((3,)), pltpu.SemaphoreType.DMA],
    )(mine)


def _swap_halves(g):
    def body(g_ref, got_ref, send_sem, recv_sem):
        x, y, c = _me()
        theirs = pl.multiple_of((1 - c) * HALF_ROWS, 8)
        cp = pltpu.make_async_remote_copy(
            src_ref=g_ref.at[:, pl.ds(theirs, HALF_ROWS), :], dst_ref=got_ref, send_sem=send_sem,
            recv_sem=recv_sem, device_id=(x, y, 1 - c), device_id_type=MESH)
        cp.start()
        cp.wait()

    return pl.pallas_call(
        body, name="swap_halves", in_specs=[HBM_SPEC], out_specs=HBM_SPEC,
        out_shape=jax.ShapeDtypeStruct((4, HALF_ROWS, 1024), F32),
        scratch_shapes=[pltpu.SemaphoreType.DMA, pltpu.SemaphoreType.DMA],
    )(g)


def _pair_sum(mine_half, got):
    def body(a_ref, b_ref, o_ref):
        o_ref[...] = (a_ref[...] + b_ref[...]).astype(BF16)

    spec = pl.BlockSpec((1, XCH_ROWS, 1024), lambda k, i: (k, i, 0))
    return pl.pallas_call(
        body, name="pair_sum", grid=(4, HALF_ROWS // XCH_ROWS), in_specs=[spec, spec], out_specs=spec,
        out_shape=jax.ShapeDtypeStruct(got.shape, BF16),
        compiler_params=_params(32, ("arbitrary", "arbitrary")),
    )(mine_half, got)


def _exchange_chips(ps):
    def body(ps_ref, out_ref, send_sems, recv_sems, local_sem):
        x, y, c = _me()
        k_me = 2 * x + y
        chips = [(1 - x, y), (x, 1 - y), (1 - x, 1 - y)]
        own = pltpu.make_async_copy(ps_ref.at[k_me], out_ref.at[k_me], local_sem)
        own.start()
        sends = [pltpu.make_async_remote_copy(
            src_ref=ps_ref.at[2 * cx + cy], dst_ref=out_ref.at[k_me], send_sem=send_sems.at[j],
            recv_sem=recv_sems.at[j], device_id=(cx, cy, c), device_id_type=MESH)
            for j, (cx, cy) in enumerate(chips)]
        for s in sends:
            s.start()
        for j, (cx, cy) in enumerate(chips):
            pltpu.make_async_remote_copy(
                src_ref=ps_ref.at[k_me], dst_ref=out_ref.at[2 * cx + cy], send_sem=send_sems.at[j],
                recv_sem=recv_sems.at[j], device_id=(cx, cy, c), device_id_type=MESH).wait_recv()
        for s in sends:
            s.wait_send()
        own.wait()

    return pl.pallas_call(
        body, name="exchange_chips", in_specs=[HBM_SPEC], out_specs=HBM_SPEC,
        out_shape=jax.ShapeDtypeStruct(ps.shape, ps.dtype),
        scratch_shapes=[pltpu.SemaphoreType.DMA((3,)), pltpu.SemaphoreType.DMA((3,)), pltpu.SemaphoreType.DMA],
    )(ps)


def _adamw(w, g, m, v):
    m = B1 * m + (1.0 - B1) * g
    v = B2 * v + (1.0 - B2) * (g * g)
    delta = -LR * ((m / BC1) / (jnp.sqrt(v / BC2) + AEPS) + WD * w)
    return delta, m, v


def _reduce_chips(parts):
    def body(p_ref, o_ref):
        f = lambda k: p_ref[k].astype(F32)
        o_ref[...] = ((f(0) + f(1)) + f(2)) + f(3)

    return pl.pallas_call(
        body, name="reduce_chips", grid=(HALF_ROWS // XCH_ROWS,),
        in_specs=[pl.BlockSpec((4, XCH_ROWS, 1024), lambda i: (0, i, 0))],
        out_specs=pl.BlockSpec((XCH_ROWS, 1024), lambda i: (i, 0)),
        out_shape=jax.ShapeDtypeStruct((HALF_ROWS, 1024), F32), compiler_params=_params(32),
    )(parts)


def _share_grad(gh):
    def body(g_ref, got_ref, send_sem, recv_sem):
        x, y, c = _me()
        cp = pltpu.make_async_remote_copy(
            src_ref=g_ref, dst_ref=got_ref, send_sem=send_sem, recv_sem=recv_sem,
            device_id=(x, y, 1 - c), device_id_type=MESH)
        cp.start()
        cp.wait()

    return pl.pallas_call(
        body, name="share_grad", in_specs=[HBM_SPEC], out_specs=HBM_SPEC,
        out_shape=jax.ShapeDtypeStruct((HALF_ROWS, 1024), F32),
        scratch_shapes=[pltpu.SemaphoreType.DMA, pltpu.SemaphoreType.DMA],
    )(gh)


def _update(g, w, m, v):
    def body(g_ref, w_ref, m_ref, v_ref, o_ref):
        delta, mn, vn = _adamw(w_ref[...], g_ref[...], m_ref[...], v_ref[...])
        o_ref[0] = delta
        o_ref[1] = mn
        o_ref[2] = vn

    s1 = pl.BlockSpec((RED_ROWS, 1024), lambda i: (i, 0))
    return pl.pallas_call(
        body, name="update", grid=(SHARD_ROWS // RED_ROWS,), in_specs=[s1, s1, s1, s1],
        out_specs=pl.BlockSpec((3, RED_ROWS, 1024), lambda i: (0, i, 0)),
        out_shape=jax.ShapeDtypeStruct((3, SHARD_ROWS, 1024), F32), compiler_params=_params(32),
    )(g, w, m, v)


SMALL_NAMES = ("sg_w",) + VEC_NAMES
VEC_ROWS = 24
VEC_ROW = {"f_bias": 0, "sg_ln_g": 1, "sg_ln_b": 2, "att_out_g": 3, "sg_out_g": 4, "pre_mix_g": 5,
           "post_mix_g": 6, "pre_ffn_g": 7, "sg_b": 8, "post_ffn_g": 16, "ple_gate_b": 17}
LOSS_VEC_ROW = 18


def _small_allreduce_update(g, w, m, v, loss_l):
    n = len(SMALL_NAMES)

    def body(*refs):
        g_r = dict(zip(SMALL_NAMES, refs[0:n]))
        w_r = dict(zip(SMALL_NAMES, refs[n:2 * n]))
        m_r = dict(zip(SMALL_NAMES, refs[2 * n:3 * n]))
        v_r = dict(zip(SMALL_NAMES, refs[3 * n:4 * n]))
        loss_r = refs[4 * n]
        loss_o = refs[4 * n + 1]
        outs = refs[4 * n + 2:8 * n + 2]
        bufv, bufw, send_sems, recv_sems = refs[8 * n + 2:]
        x, y, c = _me()
        me = 4 * x + 2 * y + c
        bufv[me] = jnp.zeros((VEC_ROWS, 1024), F32)
        for name in VEC_NAMES:
            val = g_r[name][...]
            bufv[me, pl.ds(VEC_ROW[name], val.shape[0]), pl.ds(0, val.shape[1])] = val
        bufv[me, pl.ds(LOSS_VEC_ROW, 1), :] = loss_r[...] * (0.5 / D)
        rr = lax.broadcasted_iota(jnp.int32, (CH, CH), 0)
        cc = lax.broadcasted_iota(jnp.int32, (CH, CH), 1)
        bufw[me] = jnp.where((cc <= rr)[None], g_r["sg_w"][...], 0.0)

        rels = [(rx, ry, rc) for rx in (0, 1) for ry in (0, 1) for rc in (0, 1)][1:]

        def peer(r):
            return ((x + r[0]) % 2, (y + r[1]) % 2, (c + r[2]) % 2)

        def copies(j, slot, to):
            return [pltpu.make_async_remote_copy(
                src_ref=buf.at[slot], dst_ref=buf.at[slot], send_sem=send_sems.at[2 * j + i],
                recv_sem=recv_sems.at[2 * j + i], device_id=to, device_id_type=MESH)
                for i, buf in enumerate((bufv, bufw))]

        sends = [cp for j, r in enumerate(rels) for cp in copies(j, me, peer(r))]
        for cp in sends:
            cp.start()
        for j, r in enumerate(rels):
            px, py, pc = peer(r)
            for cp in copies(j, 4 * px + 2 * py + pc, peer(r)):
                cp.wait_recv()
        for cp in sends:
            cp.wait_send()

        tot_v = bufv[0]
        tot_w = bufw[0]
        for d in range(1, 8):
            tot_v = tot_v + bufv[d]
            tot_w = tot_w + bufw[d]
        loss_o[...] = jnp.sum(tot_v[LOSS_VEC_ROW:LOSS_VEC_ROW + 1, :], axis=-1, keepdims=True) + jnp.zeros((1, 128), F32)
        for i, name in enumerate(SMALL_NAMES):
            if name == "sg_w":
                gt = tot_w
            else:
                rows, width = w_r[name].shape
                gt = tot_v[VEC_ROW[name]:VEC_ROW[name] + rows, 0:width]
            delta, mn, vn = _adamw(w_r[name][...], gt, m_r[name][...], v_r[name][...])
            outs[4 * i][...] = gt
            outs[4 * i + 1][...] = delta
            outs[4 * i + 2][...] = mn
            outs[4 * i + 3][...] = vn

    vm = pl.BlockSpec(memory_space=pltpu.VMEM)
    args = [d[k] for d in (g, w, m, v) for k in SMALL_NAMES] + [loss_l]
    out_shape = [jax.ShapeDtypeStruct((1, 128), F32)]
    out_shape += [jax.ShapeDtypeStruct(w[k].shape, F32) for k in SMALL_NAMES for _ in range(4)]
    res = pl.pallas_call(
        body, name="small_allreduce_update", in_specs=[vm] * len(args), out_specs=[vm] * len(out_shape),
        out_shape=out_shape,
        scratch_shapes=[pltpu.VMEM((8, VEC_ROWS, 1024), F32), pltpu.VMEM((8, 8, CH, CH), F32),
                        pltpu.SemaphoreType.DMA((14,)), pltpu.SemaphoreType.DMA((14,))],
        compiler_params=pltpu.CompilerParams(vmem_limit_bytes=32 * 1024 * 1024),
    )(*args)
    return res[0], {k: res[1 + 4 * i:5 + 4 * i] for i, k in enumerate(SMALL_NAMES)}


def _pack_shard(w_in, w_out, w1, w2, plew, wg):
    return jnp.concatenate([
        jnp.pad(w_in, ((0, 0), (0, 768 - 642))).reshape(768, 1024), w_out, w1, w2,
        plew.reshape(64, 1024), wg], axis=0)


def _unpack_shard(pk):
    r = 0
    out = []
    for rows, shape in ((768, (1024, 768)), (256, (256, 1024)), (1024, (1024, 1024)), (1024, (1024, 1024)),
                        (64, (256, 256)), (256, (256, 1024))):
        out.append(pk[r:r + rows].reshape(shape))
        r += rows
    out[0] = out[0][:, :642]
    return out


def _full_weights(gathered):
    parts = [_unpack_shard(gathered[k]) for k in range(4)]
    w_in = jnp.concatenate([p[0] for p in parts], axis=1)
    w_in = jnp.concatenate([w_in[:, :3 * AW], w_in[:, 3 * AW + NH:], w_in[:, 3 * AW:3 * AW + NH],
                            jnp.zeros((D, 128 - NH), w_in.dtype)], axis=1)
    return (w_in, jnp.concatenate([p[1] for p in parts], axis=0), jnp.concatenate([p[2] for p in parts], axis=1),
            jnp.concatenate([p[3] for p in parts], axis=0), jnp.concatenate([p[4] for p in parts], axis=1),
            jnp.concatenate([p[5] for p in parts], axis=0))


def _pack_grads(dwin_k, dwout, dw1, dw2, dplew, dwg):
    dwin = jnp.concatenate([dwin_k[:, :3 * AW], dwin_k[:, 5 * AW:5 * AW + NH], dwin_k[:, 3 * AW:5 * AW]], axis=1)
    return jnp.stack([
        _pack_shard(dwin[:, 642 * k:642 * (k + 1)], dwout[256 * k:256 * (k + 1)], dw1[:, 1024 * k:1024 * (k + 1)],
                    dw2[1024 * k:1024 * (k + 1)], dplew[:, 256 * k:256 * (k + 1)], dwg[256 * k:256 * (k + 1)])
        for k in range(4)])


def _local_step(x, p, tgt, win_k, wout, w1, w2, plew, wg, small):
    T = x.shape[0]
    row = lambda n: small[n].reshape(1, -1)
    fbias = jnp.pad(row("f_bias"), ((0, 0), (0, 128 - NH)))
    wm = _masked_sg_w(small["sg_w"].reshape(8, CH, CH))
    wmb = wm.astype(BF16)
    wmt = jnp.swapaxes(wm, 1, 2).astype(BF16)
    bsg = jnp.repeat(small["sg_b"].reshape(8, CH).T, DH, axis=1)
    ln_g, ln_b, gsg, gatt = row("sg_ln_g"), row("sg_ln_b"), row("sg_out_g"), row("att_out_g")
    gpre, gpm, gpf, gpff, bg = row("pre_mix_g"), row("post_mix_g"), row("pre_ffn_g"), row("post_ffn_g"), row("ple_gate_b")
    gsel = (jnp.arange(AW)[:, None] // DH == jnp.arange(128)[None, :]).astype(BF16)
    hsel = (jnp.arange(AW)[:, None] // DH == jnp.arange(AW)[None, :] // DH).astype(BF16)

    expand, _, pieces, qconst, _, _ = _head_consts()
    a, qkv, flog, ccol, zuv, ysgn, q8, k8, v8 = _pre_attn_fwd(
        x, gpre, win_k, fbias, ln_g, ln_b, wmb, bsg, gsg, expand, pieces, qconst)

    nt = T // TQ
    hd = lambda t, i: t[:, i * AW:(i + 1) * AW].reshape(T, NH, DH)
    zpad = lambda n: jnp.zeros((T, NH, n), BF16)
    one = jnp.ones((T, NH, 1), BF16)
    wide = lambda parts: jnp.concatenate(parts, axis=-1).reshape(T, NH * 128)
    slabs = lambda t: jnp.swapaxes(t.reshape(nt, TQ, NH * 128), 1, 2)
    qt8 = slabs(q8)
    kt8 = slabs(wide([hd(qkv, 1), one, zpad(63)]))
    vt8 = slabs(wide([hd(qkv, 2), one, zpad(63)]))
    lanes = jnp.arange(128)
    sel = jnp.stack([((lanes[:, None] == lanes[None, :] - DH * j) & (lanes[:, None] < DH)).astype(BF16)
                     for j in (0, 1)])

    yatt, lse = _flash_fwd(qt8, k8, vt8, sel)
    y, ov, h1, c2, sact, rr = _tail_fwd1(x, yatt, ysgn, gatt, wout, gpm, gpf, w1)
    ff, h2b, de, dpre, dh2, loss_l, dbg = _tail_fwd2(sact, h1, p, tgt, w2, gpff, wg, bg, plew)
    dff, dr, do, dya, dlt, dysg, dh1, dgpff, dgpf, dgpm, dgatt = _tail_bwd(
        dh2, ff, rr, h1, ov, yatt, w2, w1, wout, gpff, gpf, gpm, gatt, hsel)
    do8 = wide([hd(dya, 0), zpad(64)])
    dlt4 = jnp.pad(dlt[:, ::DH].T.reshape(4, 2, T), ((0, 0), (0, 6), (0, 0)))
    dqt, dk8, dv8 = _flash_bwd(q8, qt8, k8, kt8, v8, do8, slabs(do8), lse, dlt4)
    dq8 = jnp.swapaxes(dqt, 1, 2).reshape(T, NH, 128)
    dk8 = dk8.reshape(T, NH, 128)
    packed = lambda t: t[:, :, :DH].reshape(T, AW)
    dccol = jnp.pad(dq8[:, :, DH] + dk8[:, :, DH], ((0, 0), (0, 128 - NH)))
    dx, dz, dgpre, dfb, dgsg, dlng, dlnb, dws, _, dsbt = _pre_attn_bwd(
        x, dh1, packed(dq8), packed(dk8), packed(dv8.reshape(T, NH, 128)), dccol, flog, zuv, dysg,
        gpre, win_k, ln_g, ln_b, wmb, wmt, bsg, gsg, gsel)

    dwin_k = _matmul_tn("grad_w_in", a, dz, tn=384)
    dwout = _matmul_tn("grad_w_out", y, do)
    dw1 = _matmul_tn("grad_w_ff1", c2, dr)
    dw2 = _matmul_tn("grad_w_ff2", sact, dff)
    dwg = _matmul_tn("grad_ple_gate_w", h2b, dpre)
    dplew = _matmul_tn("grad_ple_w", p, de)

    dsb = dsbt[:, :8].T
    gsmall = {"sg_w": dws, "f_bias": dfb, "sg_ln_g": dlng, "sg_ln_b": dlnb, "sg_b": dsb,
              "att_out_g": dgatt, "sg_out_g": dgsg, "pre_mix_g": dgpre, "post_mix_g": dgpm, "pre_ffn_g": dgpf,
              "post_ffn_g": dgpff, "ple_gate_b": dbg}
    return loss_l, dx, (dwin_k, dwout, dw1, dw2, dplew, dwg), gsmall


def kernel(x, p, w_in, f_bias, sg_ln_g, sg_ln_b, sg_w, sg_b, att_out_g, sg_out_g, w_out, pre_mix_g, post_mix_g, pre_ffn_g, post_ffn_g, w_ff1, w_ff2, ple_w, ple_gate_w, ple_gate_b, loss_target, m_w_in, m_f_bias, m_sg_ln_g, m_sg_ln_b, m_sg_w, m_sg_b, m_att_out_g, m_sg_out_g, m_w_out, m_pre_mix_g, m_post_mix_g, m_pre_ffn_g, m_post_ffn_g, m_w_ff1, m_w_ff2, m_ple_w, m_ple_gate_w, m_ple_gate_b, v_w_in, v_f_bias, v_sg_ln_g, v_sg_ln_b, v_sg_w, v_sg_b, v_att_out_g, v_sg_out_g, v_w_out, v_pre_mix_g, v_post_mix_g, v_pre_ffn_g, v_post_ffn_g, v_w_ff1, v_w_ff2, v_ple_w, v_ple_gate_w, v_ple_gate_b):
    c = lax.axis_index("c")
    big = lambda t: (t[0][0], t[1][0], t[2][0], t[3][0], t[4][0], t[5][0])
    w_big = big((w_in, w_out, w_ff1, w_ff2, ple_w, ple_gate_w))
    m_big = big((m_w_in, m_w_out, m_w_ff1, m_w_ff2, m_ple_w, m_ple_gate_w))
    v_big = big((v_w_in, v_w_out, v_w_ff1, v_w_ff2, v_ple_w, v_ple_gate_w))
    small = {"sg_w": sg_w, "f_bias": f_bias, "sg_ln_g": sg_ln_g, "sg_ln_b": sg_ln_b, "sg_b": sg_b,
             "att_out_g": att_out_g, "sg_out_g": sg_out_g, "pre_mix_g": pre_mix_g, "post_mix_g": post_mix_g,
             "pre_ffn_g": pre_ffn_g, "post_ffn_g": post_ffn_g, "ple_gate_b": ple_gate_b}
    m_small = {"sg_w": m_sg_w, "f_bias": m_f_bias, "sg_ln_g": m_sg_ln_g, "sg_ln_b": m_sg_ln_b, "sg_b": m_sg_b,
               "att_out_g": m_att_out_g, "sg_out_g": m_sg_out_g, "pre_mix_g": m_pre_mix_g,
               "post_mix_g": m_post_mix_g, "pre_ffn_g": m_pre_ffn_g, "post_ffn_g": m_post_ffn_g,
               "ple_gate_b": m_ple_gate_b}
    v_small = {"sg_w": v_sg_w, "f_bias": v_f_bias, "sg_ln_g": v_sg_ln_g, "sg_ln_b": v_sg_ln_b, "sg_b": v_sg_b,
               "att_out_g": v_att_out_g, "sg_out_g": v_sg_out_g, "pre_mix_g": v_pre_mix_g,
               "post_mix_g": v_post_mix_g, "pre_ffn_g": v_pre_ffn_g, "post_ffn_g": v_post_ffn_g,
               "ple_gate_b": v_ple_gate_b}

    w_pk = _pack_shard(*w_big)
    gathered = _gather_weights(w_pk.astype(BF16))
    win_k, wout, w1, w2, plew, wg = _full_weights(gathered)

    loss_l, dx, gbig, gsmall = _local_step(x[0], p[0, 0], loss_target[0], win_k, wout, w1, w2, plew, wg, small)

    gp = _pack_grads(*gbig)
    got = _swap_halves(gp)
    mine_half = lax.dynamic_slice_in_dim(gp, c * HALF_ROWS, HALF_ROWS, axis=1)
    parts = _exchange_chips(_pair_sum(mine_half, got))
    gh = _reduce_chips(parts)
    got = _share_grad(gh)
    g_full = jnp.where(c == 0, jnp.concatenate([gh, got], axis=0), jnp.concatenate([got, gh], axis=0))
    upd = _update(g_full, w_pk, _pack_shard(*m_big), _pack_shard(*v_big))
    big_out = [_unpack_shard(t) for t in (g_full, upd[0], upd[1], upd[2])]

    view = lambda t: t.reshape(t.shape[-3:]) if t.ndim == 4 else t.reshape(t.shape[-2:])
    views = lambda d: {k: view(d[k]) for k in SMALL_NAMES}
    loss11, res_s = _small_allreduce_update(gsmall, views(small), views(m_small), views(v_small), loss_l)
    loss = loss11[0, 0]

    def small_out(i, name):
        return res_s[name][i].reshape(small[name].shape)

    order = ["w_in", "f_bias", "sg_ln_g", "sg_ln_b", "sg_w", "sg_b", "att_out_g", "sg_out_g", "w_out",
             "pre_mix_g", "post_mix_g", "pre_ffn_g", "post_ffn_g", "w_ff1", "w_ff2", "ple_w", "ple_gate_w",
             "ple_gate_b"]
    big_idx = {"w_in": 0, "w_out": 1, "w_ff1": 2, "w_ff2": 3, "ple_w": 4, "ple_gate_w": 5}
    outs = [loss, dx[None]]
    for i in range(4):
        for name in order:
            if name in big_idx:
                outs.append(big_out[i][big_idx[name]][None])
            else:
                outs.append(small_out(i, name))
    return tuple(outs)
```

```python
import math

import jax
import jax.numpy as jnp
from jax import lax
from jax.experimental import pallas as pl
from jax.experimental.pallas import tpu as pltpu

F32 = jnp.float32
BF16 = jnp.bfloat16
MESH = pl.DeviceIdType.MESH

D = 1024
DH = 64
NH = 8
AW = 512
CH = 128
DFF = 4096
ZW = 5 * AW + 128
EPS = 1e-6
NEG = -1e30
MASKED = -2e30

TM = 256
TQ = 256

LR, B1, B2, AEPS, WD, STEP = 0.001, 0.9, 0.999, 1e-08, 0.01, 10
BC1 = 1.0 - B1 ** STEP
BC2 = 1.0 - B2 ** STEP

SHARD_ROWS = 768 + 256 + 1024 + 1024 + 64 + 256
HALF_ROWS = SHARD_ROWS // 2
RED_ROWS = HALF_ROWS // 4
XCH_ROWS = HALF_ROWS // 2
VEC_NAMES = ("f_bias", "sg_ln_g", "sg_ln_b", "sg_b", "att_out_g", "sg_out_g", "pre_mix_g",
             "post_mix_g", "pre_ffn_g", "post_ffn_g", "ple_gate_b")


def _dot(a, b):
    return jnp.dot(a, b, preferred_element_type=F32)


def _dot_nt(a, b):
    return lax.dot_general(a, b, (((1,), (1,)), ((), ())), preferred_element_type=F32)


def _dot_tn(a, b):
    return lax.dot_general(a, b, (((0,), (0,)), ((), ())), preferred_element_type=F32)


def _split3(x):
    h = x.astype(BF16)
    r = x - h.astype(F32)
    m = r.astype(BF16)
    l = (r - m.astype(F32)).astype(BF16)
    return h, m, l


def _dot01(sel, x):
    h, m, l = _split3(x)
    return _dot(sel, h) + _dot(sel, m) + _dot(sel, l)


def _dot01_r(x, sel):
    h, m, l = _split3(x)
    return _dot(h, sel) + _dot(m, sel) + _dot(l, sel)


def _dot01_tn(x, sel):
    h, m, l = _split3(x)
    return _dot_tn(h, sel) + _dot_tn(m, sel) + _dot_tn(l, sel)


def _rs(x, n):
    return lax.rsqrt(jnp.sum(x * x, axis=-1, keepdims=True) * (1.0 / n) + EPS)


def _rms_bwd(dn, x, rs, g, n):
    w = dn * g
    dx = rs * w - x * ((rs * rs * rs) * (1.0 / n) * jnp.sum(w * x, axis=-1, keepdims=True))
    return dx, jnp.sum(dn * x * rs, axis=0, keepdims=True)


_GC = math.sqrt(2.0 / math.pi)


def _gelu(x):
    t = jnp.tanh(_GC * (x + 0.044715 * x * x * x))
    return 0.5 * x * (1.0 + t), t


def _gelu_grad(x, t):
    return 0.5 * (1.0 + t) + 0.5 * x * (1.0 - t * t) * (_GC * (1.0 + 3.0 * 0.044715 * x * x))


def _params(vmem_mb, sem=("arbitrary",)):
    return pltpu.CompilerParams(dimension_semantics=sem, vmem_limit_bytes=vmem_mb * 1024 * 1024)


def _row_call(name, body, T, tm, tiled, resident, outs, accs, scratch=(), reverse=False, vmem_mb=48):
    nt = T // tm
    n_t, n_r, n_o, n_a = len(tiled), len(resident), len(outs), len(accs)

    def kern(*refs):
        t_refs = refs[:n_t]
        r_hbm = refs[n_t:n_t + n_r]
        o_refs = refs[n_t + n_r:n_t + n_r + n_o]
        a_refs = refs[n_t + n_r + n_o:n_t + n_r + n_o + n_a]
        r_vmem = refs[n_t + n_r + n_o + n_a:n_t + 2 * n_r + n_o + n_a]
        s_refs = refs[n_t + 2 * n_r + n_o + n_a:]

        @pl.when(pl.program_id(0) == 0)
        def _():
            for h, v in zip(r_hbm, r_vmem):
                pltpu.sync_copy(h, v)
            for a in a_refs + s_refs:
                a[...] = jnp.zeros(a.shape, a.dtype)

        body(t_refs, r_vmem, o_refs, a_refs, s_refs)

    if reverse:
        idx = lambda i: (nt - 1 - i, 0)
        idx_t = lambda i: (nt - 1 - i, 0, 0)
    else:
        idx = lambda i: (i, 0)
        idx_t = lambda i: (i, 0, 0)
    arrays, in_specs = [], []
    for a in tiled:
        if isinstance(a, tuple):
            arrays.append(a[0])
            in_specs.append(pl.BlockSpec((None, a[0].shape[1], tm), idx_t))
        else:
            arrays.append(a)
            in_specs.append(pl.BlockSpec((tm, a.shape[1]), idx))
    in_specs += [pl.BlockSpec(memory_space=pl.ANY) for _ in resident]
    out_shape, out_specs = [], []
    for o in outs:
        if len(o) == 3:
            out_shape.append(jax.ShapeDtypeStruct((nt, o[0], tm), o[1]))
            out_specs.append(pl.BlockSpec((None, o[0], tm), idx_t))
        else:
            out_shape.append(jax.ShapeDtypeStruct((T, o[0]), o[1]))
            out_specs.append(pl.BlockSpec((tm, o[0]), idx))
    out_shape += [jax.ShapeDtypeStruct(s, F32) for s in accs]
    out_specs += [pl.BlockSpec(s, lambda i, n=len(s): (0,) * n) for s in accs]
    scratch_shapes = [pltpu.VMEM(r.shape, r.dtype) for r in resident]
    scratch_shapes += [pltpu.VMEM(s, F32) for s in scratch]
    return pl.pallas_call(
        kern, name=name, grid=(nt,), in_specs=in_specs, out_specs=out_specs, out_shape=out_shape,
        scratch_shapes=scratch_shapes, compiler_params=_params(vmem_mb),
    )(*arrays, *resident)


def _sg_forward(zu, zv, wm_ref, bsg, lng, lnb, mixed_ref, tm):
    gu, tu = _gelu(zu)
    vg, tv = _gelu(zv)
    mu = jnp.sum(vg, axis=-1, keepdims=True) * (1.0 / AW)
    xc = vg - mu
    rstd = lax.rsqrt(jnp.sum(xc * xc, axis=-1, keepdims=True) * (1.0 / AW) + EPS)
    xhat = xc * rstd
    vvb = (xhat * lng + lnb).astype(BF16)
    lane = lax.broadcasted_iota(jnp.int32, (CH, 128), 1)
    for c in range(tm // CH):
        for j in range(4):
            blk = vvb[c * CH:(c + 1) * CH, j * 128:(j + 1) * 128]
            m0 = _dot(wm_ref[2 * j], blk)
            m1 = _dot(wm_ref[2 * j + 1], blk)
            mixed_ref[c * CH:(c + 1) * CH, j * 128:(j + 1) * 128] = (
                jnp.where(lane < DH, m0, m1) + bsg[:, j * 128:(j + 1) * 128])
    return gu, tu, tv, xhat, rstd, vvb, mixed_ref[...]


def _head_consts():
    src = jnp.arange(AW)
    dst = (src // DH) * 128 + src % DH
    wide = jnp.arange(NH * 128)
    expand = (dst[:, None] == wide[None, :]).astype(BF16)
    heads = jnp.arange(128)
    pieces = jnp.stack([((heads[:, None] * 128 + DH + i == wide[None, :]) & (heads[:, None] < NH)).astype(BF16)
                        for i in range(3)])
    spare = wide % 128 - DH
    qconst = jnp.where((spare >= 0) & (spare < 3), -1.0, 0.0).astype(F32)[None, :]
    one64 = jnp.where(spare == 0, 1.0, 0.0).astype(F32)[None, :]
    pick64 = ((wide[:, None] == heads[None, :] * 128 + DH) & (heads[None, :] < NH)).astype(BF16)
    return expand, expand.T, pieces, qconst, one64, pick64


def _masked_sg_w(sg_w):
    r = lax.broadcasted_iota(jnp.int32, (CH, CH), 0)
    c = lax.broadcasted_iota(jnp.int32, (CH, CH), 1)
    return jnp.where((c <= r)[None], sg_w, 0.0)


def _pre_attn_fwd(x, gpre, win, fbias, lng, lnb, wm, bsg, gsg, expand, pieces, qconst):
    T = x.shape[0]
    tm = TM

    def body(t, r, o, a, s):
        (x_ref,) = t
        gpre_r, win_r, fb_r, lng_r, lnb_r, wm_r, bsg_r, gsg_r, ex_r, pc_r, qc_r = r
        a_o, qkv_o, flog_o, ccol_o, zuv_o, ysgn_o, q8_o, k8_o, v8_o = o
        carry_ref, mixed_ref = s
        xv = x_ref[...]
        av = (xv * _rs(xv, D) * gpre_r[...]).astype(BF16)
        a_o[...] = av
        z = _dot(av, win_r[...])
        zu = z[:, 3 * AW:4 * AW]
        zv = z[:, 4 * AW:5 * AW]
        zuv_o[:, 0:AW] = zu
        zuv_o[:, AW:2 * AW] = zv
        zf = z[:, 5 * AW:] + fb_r[...]
        flog_o[...] = zf
        lane = lax.broadcasted_iota(jnp.int32, (tm, 128), 1)
        logf = jnp.where(lane < NH, jnp.minimum(zf, 0.0) - jnp.log(1.0 + jnp.exp(-jnp.abs(zf))), 0.0)
        rr = lax.broadcasted_iota(jnp.int32, (tm, tm), 0)
        cc = lax.broadcasted_iota(jnp.int32, (tm, tm), 1)
        tri = (cc <= rr).astype(BF16)
        cum = _dot01(tri, logf) + carry_ref[...]
        carry_ref[...] = cum[tm - 1:tm, :]
        ccol_o[...] = cum
        ex = ex_r[...]
        q8_o[...] = (_dot((z[:, 0:AW] * (DH ** -0.5)).astype(BF16), ex) + qc_r[...]).astype(BF16)
        ch, cm, cl = _split3(cum)
        k8_o[...] = (_dot(z[:, AW:2 * AW].astype(BF16), ex) + _dot(ch, pc_r[0]) + _dot(cm, pc_r[1])
                     + _dot(cl, pc_r[2])).astype(BF16)
        v8_o[...] = _dot(z[:, 2 * AW:3 * AW].astype(BF16), ex).astype(BF16)
        qkv_o[:, 0:AW] = (z[:, 0:AW] * (DH ** -0.5)).astype(BF16)
        qkv_o[:, AW:3 * AW] = z[:, AW:3 * AW].astype(BF16)
        gu, _, _, _, _, _, mixed = _sg_forward(zu, zv, wm_r, bsg_r[...], lng_r[...], lnb_r[...], mixed_ref, tm)
        ysg = gu * mixed
        ysgn_o[...] = (ysg * _rs(ysg, AW) * gsg_r[...]).astype(BF16)

    return _row_call(
        "pre_attn_fwd", body, T, tm, [x], [gpre, win, fbias, lng, lnb, wm, bsg, gsg, expand, pieces, qconst],
        [(D, BF16), (3 * AW, BF16), (128, F32), (128, F32), (2 * AW, F32), (AW, BF16), (NH * 128, BF16),
         (NH * 128, BF16), (NH * 128, BF16)], [],
        scratch=[(1, 128), (tm, AW)], vmem_mb=48)


def _flash_fwd(qt8, k8, vt8, sel):
    T = k8.shape[0]
    nq = T // TQ

    def body(qt_ref, k_ref, vt_ref, sel_ref, o_ref, l_ref, u_scr, p_scr):
        qi = pl.program_id(1)
        qts = (qt_ref[0:128, :], qt_ref[128:256, :])
        dmat = (lax.broadcasted_iota(jnp.int32, (TQ, TQ), 0) - lax.broadcasted_iota(jnp.int32, (TQ, TQ), 1))
        u_scr[1] = jnp.full((2, TQ, TQ), MASKED, F32)
        p_scr[...] = jnp.zeros(p_scr.shape, BF16)

        def sub(t, carry, sc, sb):
            blk_c = jnp.clip(t - 2, 0, qi)
            off_a = pl.multiple_of(jnp.minimum(t, qi) * TQ, TQ)
            lim = (qi - t) * TQ
            new = []
            for j in (0, 1):
                m, al, acc = carry[j]
                acc = al * acc + _dot(vt_ref[blk_c, j * 128:(j + 1) * 128, :], p_scr[sc, j])
                m_new = jnp.maximum(m, jnp.max(u_scr[sb, j], axis=0, keepdims=True))
                p_scr[sb, j] = jnp.exp(u_scr[sb, j] - m_new).astype(BF16)
                u_scr[sc, j] = jnp.where(
                    dmat <= lim, _dot(k_ref[pl.ds(off_a, TQ), j * 128:(j + 1) * 128], qts[j]), MASKED)
                new.append((m_new, jnp.exp(m - m_new), acc))
            return tuple(new)

        def it(t2, carry):
            return sub(2 * t2 + 1, sub(2 * t2, carry, 0, 1), 1, 0)

        init = tuple((jnp.full((1, TQ), NEG, F32), jnp.ones((1, TQ), F32), jnp.zeros((128, TQ), F32))
                     for _ in (0, 1))
        (m0, _, a0), (m1, _, a1) = lax.fori_loop(0, (qi + 4) // 2, it, init)
        l0 = a0[DH:DH + 1, :]
        l1 = a1[DH:DH + 1, :]
        o_ref[...] = _dot01_tn(a0 * (1.0 / l0), sel_ref[0]) + _dot01_tn(a1 * (1.0 / l1), sel_ref[1])
        l_ref[0:1, :] = m0 + jnp.log(l0)
        l_ref[1:2, :] = m1 + jnp.log(l1)
        l_ref[2:8, :] = jnp.zeros((6, TQ), F32)

    return pl.pallas_call(
        body, name="flash_fwd", grid=(4, nq),
        in_specs=[pl.BlockSpec((None, 256, TQ), lambda h, i: (i, h, 0)),
                  pl.BlockSpec((T, 256), lambda h, i: (0, h)),
                  pl.BlockSpec((nq, 256, TQ), lambda h, i: (0, h, 0)),
                  pl.BlockSpec((2, 128, 128), lambda h, i: (0, 0, 0))],
        out_specs=[pl.BlockSpec((TQ, 128), lambda h, i: (i, h)),
                   pl.BlockSpec((None, 8, TQ), lambda h, i: (h, 0, i))],
        out_shape=[jax.ShapeDtypeStruct((T, AW), F32), jax.ShapeDtypeStruct((4, 8, T), F32)],
        scratch_shapes=[pltpu.VMEM((2, 2, TQ, TQ), F32), pltpu.VMEM((2, 2, TQ, TQ), BF16)],
        compiler_params=_params(40, ("arbitrary", "arbitrary")),
    )(qt8, k8, vt8, sel)


def _flash_bwd(q8, qt8, k8, kt8, v8, do8, dot8, lse, dlt):
    T = q8.shape[0]
    nk = T // TQ

    def body(q_ref, qt_ref, k_ref, kt_ref, v_ref, do_ref, dot_ref, l_ref, d_ref, dqt_ref, dk_ref, dv_ref,
             u_scr, dp_scr, p_scr, ds_scr):
        kb = pl.program_id(1)
        n = nk - kb

        @pl.when(kb == 0)
        def _():
            dqt_ref[...] = jnp.zeros(dqt_ref.shape, F32)

        dk_ref[...] = jnp.zeros(dk_ref.shape, F32)
        dv_ref[...] = jnp.zeros(dv_ref.shape, F32)
        u_scr[1] = jnp.full((2, TQ, TQ), MASKED, F32)
        dp_scr[1] = jnp.zeros((2, TQ, TQ), F32)
        p_scr[...] = jnp.zeros(p_scr.shape, BF16)
        ds_scr[...] = jnp.zeros(ds_scr.shape, BF16)
        dmat = (lax.broadcasted_iota(jnp.int32, (TQ, TQ), 0) - lax.broadcasted_iota(jnp.int32, (TQ, TQ), 1))
        ks = (k_ref[:, 0:128], k_ref[:, 128:256])
        vs = (v_ref[:, 0:128], v_ref[:, 128:256])
        kts = (kt_ref[0:128, :], kt_ref[128:256, :])

        def sub(t, sc, sb):
            blk_a = kb + jnp.minimum(t, n - 1)
            blk_c = kb + jnp.clip(t - 2, 0, n - 1)
            off_b = pl.multiple_of((kb + jnp.clip(t - 1, 0, n - 1)) * TQ, TQ)
            off_c = pl.multiple_of(blk_c * TQ, TQ)
            lim = jnp.where(t < n, t * TQ, -TQ)
            for j in (0, 1):
                hl = slice(j * 128, (j + 1) * 128)
                dqt_ref[blk_c, hl, :] += _dot(kts[j], ds_scr[sc, j])
                dk_ref[:, hl] += _dot(ds_scr[sc, j], q_ref[pl.ds(off_c, TQ), hl])
                dv_ref[:, hl] += _dot(p_scr[sc, j], do_ref[pl.ds(off_c, TQ), hl])
                p = jnp.exp(u_scr[sb, j] - l_ref[j:j + 1, pl.ds(off_b, TQ)])
                p_scr[sb, j] = p.astype(BF16)
                ds_scr[sb, j] = (p * (dp_scr[sb, j] - d_ref[j:j + 1, pl.ds(off_b, TQ)])).astype(BF16)
                u_scr[sc, j] = jnp.where(dmat <= lim, _dot(ks[j], qt_ref[blk_a, hl, :]), MASKED)
                dp_scr[sc, j] = _dot(vs[j], dot_ref[blk_a, hl, :])

        def it(t2, carry):
            sub(2 * t2, 0, 1)
            sub(2 * t2 + 1, 1, 0)
            return carry

        lax.fori_loop(0, (n + 3) // 2, it, 0)

    return pl.pallas_call(
        body, name="flash_bwd", grid=(4, nk),
        in_specs=[pl.BlockSpec((T, 256), lambda h, i: (0, h)),
                  pl.BlockSpec((nk, 256, TQ), lambda h, i: (0, h, 0)),
                  pl.BlockSpec((TQ, 256), lambda h, i: (i, h)),
                  pl.BlockSpec((None, 256, TQ), lambda h, i: (i, h, 0)),
                  pl.BlockSpec((TQ, 256), lambda h, i: (i, h)),
                  pl.BlockSpec((T, 256), lambda h, i: (0, h)),
                  pl.BlockSpec((nk, 256, TQ), lambda h, i: (0, h, 0)),
                  pl.BlockSpec((None, 8, T), lambda h, i: (h, 0, 0)),
                  pl.BlockSpec((None, 8, T), lambda h, i: (h, 0, 0))],
        out_specs=[pl.BlockSpec((nk, 256, TQ), lambda h, i: (0, h, 0)),
                   pl.BlockSpec((TQ, 256), lambda h, i: (i, h)),
                   pl.BlockSpec((TQ, 256), lambda h, i: (i, h))],
        out_shape=[jax.ShapeDtypeStruct((nk, NH * 128, TQ), F32), jax.ShapeDtypeStruct((T, NH * 128), F32),
                   jax.ShapeDtypeStruct((T, NH * 128), F32)],
        scratch_shapes=[pltpu.VMEM((2, 2, TQ, TQ), F32), pltpu.VMEM((2, 2, TQ, TQ), F32),
                        pltpu.VMEM((2, 2, TQ, TQ), BF16), pltpu.VMEM((2, 2, TQ, TQ), BF16)],
        compiler_params=_params(56, ("arbitrary", "arbitrary")),
    )(q8, qt8, k8, kt8, v8, do8, dot8, lse, dlt)


def _tail_fwd1(x, yatt, ysgn, gatt, wout, gpm, gpf, w1):
    T = x.shape[0]

    def body(t, r, o, a, s):
        x_ref, ya_ref, ys_ref = t
        gatt_r, wout_r, gpm_r, gpf_r, w1_r = r
        y_o, o_o, h1_o, c2_o, s_o, rr_o = o
        ya = ya_ref[...]
        yan = (ya * _rs(ya, AW) * gatt_r[...]).astype(BF16)
        y_o[:, 0:AW] = yan
        y_o[:, AW:] = ys_ref[...]
        ov = _dot(yan, wout_r[0:AW, :]) + _dot(ys_ref[...], wout_r[AW:, :])
        o_o[...] = ov
        h1 = x_ref[...] + ov * _rs(ov, D) * gpm_r[...]
        h1_o[...] = h1
        c2 = (h1 * _rs(h1, D) * gpf_r[...]).astype(BF16)
        c2_o[...] = c2
        rr = jnp.maximum(_dot(c2, w1_r[...]), 0.0)
        rr_o[...] = rr.astype(BF16)
        s_o[...] = (rr * rr).astype(BF16)

    return _row_call(
        "tail_fwd1", body, T, TM, [x, yatt, ysgn], [gatt, wout, gpm, gpf, w1],
        [(D, BF16), (D, F32), (D, F32), (D, BF16), (DFF, BF16), (DFF, BF16)], [], vmem_mb=48)


def _tail_fwd2(sact, h1, p, tgt, w2, gpff, wg, bg, wpe):
    T = h1.shape[0]

    def body(t, r, o, a, s):
        s_ref, h1_ref, p_ref, t_ref = t
        w2_r, gpff_r, wg_r, bg_r, wpe_r = r
        ff_o, h2b_o, de_o, dpre_o, dh2_o = o
        loss_a, dbg_a = a
        ff = _dot(s_ref[...], w2_r[...])
        ff_o[...] = ff
        h2 = h1_ref[...] + ff * _rs(ff, D) * gpff_r[...]
        h2b = h2.astype(BF16)
        h2b_o[...] = h2b
        gate = 1.0 / (1.0 + jnp.exp(-(_dot(h2b, wg_r[...]) + bg_r[...])))
        e = _dot(p_ref[...].astype(BF16), wpe_r[...])
        diff = h2 + gate * e - t_ref[...]
        loss_a[...] += jnp.sum(diff * diff, axis=0, keepdims=True)
        dh3 = diff * (1.0 / D)
        de_o[...] = (dh3 * gate).astype(BF16)
        dpre = dh3 * e * gate * (1.0 - gate)
        dbg_a[...] += jnp.sum(dpre, axis=0, keepdims=True)
        dpb = dpre.astype(BF16)
        dpre_o[...] = dpb
        dh2_o[...] = dh3 + _dot_nt(dpb, wg_r[...])

    return _row_call(
        "tail_fwd2", body, T, TM, [sact, h1, p, tgt], [w2, gpff, wg, bg, wpe],
        [(D, F32), (D, BF16), (D, BF16), (D, BF16), (D, F32)], [(1, D), (1, D)], vmem_mb=48)


def _tail_bwd(dh2, ff, rr, h1, ov, yatt, w2, w1, wout, gpff, gpf, gpm, gatt, hsel, expand):
    T = dh2.shape[0]

    def body(t, r, o, a, s):
        dh2_ref, ff_ref, rr_ref, h1_ref, o_ref, ya_ref = t
        w2_r, w1_r, wout_r, gpff_r, gpf_r, gpm_r, gatt_r, hsel_r, ex_r = r
        dff_o, dr_o, do_o, do8_o, dlt_o, dysg_o, dh1_o = o
        dgpff_a, dgpf_a, dgpm_a, dgatt_a = a
        dh2v = dh2_ref[...]
        ffv = ff_ref[...]
        dff, dg = _rms_bwd(dh2v, ffv, _rs(ffv, D), gpff_r[...], D)
        dgpff_a[...] += dg
        dffb = dff.astype(BF16)
        dff_o[...] = dffb
        drb = (_dot_nt(dffb, w2_r[...]) * (2.0 * rr_ref[...].astype(F32))).astype(BF16)
        dr_o[...] = drb
        dc2 = _dot_nt(drb, w1_r[...])
        h1v = h1_ref[...]
        d1, dg = _rms_bwd(dc2, h1v, _rs(h1v, D), gpf_r[...], D)
        dgpf_a[...] += dg
        dh1 = dh2v + d1
        dh1_o[...] = dh1
        ovv = o_ref[...]
        dov, dg = _rms_bwd(dh1, ovv, _rs(ovv, D), gpm_r[...], D)
        dgpm_a[...] += dg
        dob = dov.astype(BF16)
        do_o[...] = dob
        dysg_o[...] = _dot_nt(dob, wout_r[AW:, :])
        dyan = _dot_nt(dob, wout_r[0:AW, :])
        ya = ya_ref[...]
        dya, dg = _rms_bwd(dyan, ya, _rs(ya, AW), gatt_r[...], AW)
        dgatt_a[...] += dg
        do8_o[...] = _dot(dya.astype(BF16), ex_r[...]).astype(BF16)
        dlt_o[...] = _dot01_r(dya * ya, hsel_r[...])

    return _row_call(
        "tail_bwd", body, T, TM, [dh2, ff, rr, h1, ov, yatt],
        [w2, w1, wout, gpff, gpf, gpm, gatt, hsel, expand],
        [(D, BF16), (DFF, BF16), (D, BF16), (NH * 128, BF16), (AW, F32), (AW, F32), (D, F32)],
        [(1, D), (1, D), (1, D), (1, AW)], vmem_mb=56)


def _pre_attn_bwd(x, dh1, dq8, dk8, dv8, flog, zuv, dysg, gpre, win, lng, lnb, wm, wmt, bsg, gsg, gsel, shrink, pick64):
    T = x.shape[0]
    tm = TM

    def body(t, r, o, a, s):
        x_ref, dh1_ref, dq_ref, dk_ref, dv_ref, fl_ref, zuv_ref, dys_ref = t
        gpre_r, win_r, lng_r, lnb_r, wm_r, wmt_r, bsg_r, gsg_r, gsel_r, sh_r, p64_r = r
        dx_o, dz_o = o
        dgpre_a, dfb_a, dgsg_a, dlng_a, dlnb_a, dws_a, dbs_a, dsb_a = a
        carry_ref, mixed_ref, dvv_ref = s
        dq8v = dq_ref[...]
        dk8v = dk_ref[...]
        dcv = _dot01_r(dq8v, p64_r[...]) + _dot01_r(dk8v, p64_r[...])
        rr = lax.broadcasted_iota(jnp.int32, (tm, tm), 0)
        cc = lax.broadcasted_iota(jnp.int32, (tm, tm), 1)
        triu = (cc >= rr).astype(BF16)
        dlogf = _dot01(triu, dcv) + carry_ref[...]
        carry_ref[...] = dlogf[0:1, :]
        dzf = dlogf * (1.0 / (1.0 + jnp.exp(fl_ref[...])))
        dfb_a[...] += jnp.sum(dzf, axis=0, keepdims=True)
        dz_o[:, 5 * AW:] = dzf.astype(BF16)
        zu = zuv_ref[:, 0:AW]
        zv = zuv_ref[:, AW:]
        gu, tu, tv, xhat, rstd, vvb, mixed = _sg_forward(
            zu, zv, wm_r, bsg_r[...], lng_r[...], lnb_r[...], mixed_ref, tm)
        ysg = gu * mixed
        dysg_n = dys_ref[...]
        dys, dg = _rms_bwd(dysg_n, ysg, _rs(ysg, AW), gsg_r[...], AW)
        dgsg_a[...] += dg
        dgu = dys * mixed
        dmix = dys * gu
        dmb = dmix.astype(BF16)
        lane = lax.broadcasted_iota(jnp.int32, (CH, 128), 1)
        lo = lane < DH
        for c in range(tm // CH):
            rows = slice(c * CH, (c + 1) * CH)
            dbs_a[...] += dmix[rows, :]
            for j in range(4):
                cols = slice(j * 128, (j + 1) * 128)
                dmblk = dmb[rows, cols]
                vblk = vvb[rows, cols]
                d0 = _dot(wmt_r[2 * j], dmblk)
                d1 = _dot(wmt_r[2 * j + 1], dmblk)
                dvv_ref[rows, cols] = jnp.where(lo, d0, d1)
                dws_a[2 * j] += _dot_nt(jnp.where(lo, dmblk, jnp.zeros_like(dmblk)), vblk)
                dws_a[2 * j + 1] += _dot_nt(jnp.where(lo, jnp.zeros_like(dmblk), dmblk), vblk)
        dvv = dvv_ref[...]
        dlng_a[...] += jnp.sum(dvv * xhat, axis=0, keepdims=True)
        dlnb_a[...] += jnp.sum(dvv, axis=0, keepdims=True)
        dxh = dvv * lng_r[...]
        dvg = rstd * (dxh - jnp.sum(dxh, axis=-1, keepdims=True) * (1.0 / AW)
                      - xhat * (jnp.sum(dxh * xhat, axis=-1, keepdims=True) * (1.0 / AW)))
        dz_o[:, 3 * AW:4 * AW] = (dgu * _gelu_grad(zu, tu)).astype(BF16)
        dz_o[:, 4 * AW:5 * AW] = (dvg * _gelu_grad(zv, tv)).astype(BF16)
        dz_o[:, 0:AW] = _dot((dq8v * (DH ** -0.5)).astype(BF16), sh_r[...]).astype(BF16)
        dz_o[:, AW:2 * AW] = _dot(dk8v.astype(BF16), sh_r[...]).astype(BF16)
        dz_o[:, 2 * AW:3 * AW] = _dot(dv_ref[...].astype(BF16), sh_r[...]).astype(BF16)
        da = _dot_nt(dz_o[...], win_r[...])
        xv = x_ref[...]
        dxa, dg = _rms_bwd(da, xv, _rs(xv, D), gpre_r[...], D)
        dgpre_a[...] += dg
        dx_o[...] = dh1_ref[...] + dxa

        @pl.when(pl.program_id(0) == T // tm - 1)
        def _():
            dsb_a[...] = _dot01_r(dbs_a[...], gsel_r[...])

    outs = _row_call(
        "pre_attn_bwd", body, T, tm, [x, dh1, dq8, dk8, dv8, flog, zuv, dysg],
        [gpre, win, lng, lnb, wm, wmt, bsg, gsg, gsel, shrink, pick64],
        [(D, F32), (ZW, BF16)],
        [(1, D), (1, 128), (1, AW), (1, AW), (1, AW), (8, CH, CH), (CH, AW), (CH, 128)],
        scratch=[(1, 128), (tm, AW), (tm, AW)], reverse=True, vmem_mb=48)
    return outs


def _matmul_tn(name, a, b, tn=512, tt=512):
    T, K = a.shape
    N = b.shape[1]
    tk = min(K, 1024)
    tn = min(tn, N)
    tt = min(tt, T)

    def body(a_ref, b_ref, o_ref):
        @pl.when(pl.program_id(2) == 0)
        def _():
            o_ref[...] = jnp.zeros(o_ref.shape, F32)

        o_ref[...] += _dot_tn(a_ref[...].astype(BF16), b_ref[...].astype(BF16))

    return pl.pallas_call(
        body, name=name, grid=(K // tk, N // tn, T // tt),
        in_specs=[pl.BlockSpec((tt, tk), lambda i, j, t: (t, i)),
                  pl.BlockSpec((tt, tn), lambda i, j, t: (t, j))],
        out_specs=pl.BlockSpec((tk, tn), lambda i, j, t: (i, j)),
        out_shape=jax.ShapeDtypeStruct((K, N), F32),
        compiler_params=_params(32, ("arbitrary", "arbitrary", "arbitrary")),
    )(a, b)


def _me():
    return lax.axis_index("x"), lax.axis_index("y"), lax.axis_index("c")


HBM_SPEC = pl.BlockSpec(memory_space=pltpu.HBM)


def _gather_weights(mine):
    def body(mine_ref, out_ref, ici_send, ici_recv, d2d_send, d2d_recv, local_sem):
        x, y, c = _me()
        k_me = 2 * x + y
        chips = [(1 - x, y), (x, 1 - y), (1 - x, 1 - y)]
        my_rows = pl.ds(pl.multiple_of(c * HALF_ROWS, 16), HALF_ROWS)
        sib_rows = pl.ds(pl.multiple_of((1 - c) * HALF_ROWS, 16), HALF_ROWS)

        def over_ici(j, k, to):
            src = mine_ref.at[my_rows] if k is None else out_ref.at[k, my_rows]
            return pltpu.make_async_remote_copy(
                src_ref=src, dst_ref=out_ref.at[k_me if k is None else k, my_rows], send_sem=ici_send.at[j],
                recv_sem=ici_recv.at[j], device_id=to, device_id_type=MESH)

        def over_d2d(j, k, rows):
            return pltpu.make_async_remote_copy(
                src_ref=out_ref.at[k, rows], dst_ref=out_ref.at[k, rows], send_sem=d2d_send.at[j],
                recv_sem=d2d_recv.at[j], device_id=(x, y, 1 - c), device_id_type=MESH)

        own = pltpu.make_async_copy(mine_ref, out_ref.at[k_me], local_sem)
        own.start()
        first = [over_ici(j, None, (cx, cy, c)) for j, (cx, cy) in enumerate(chips)]
        for cp in first:
            cp.start()
        passed = [over_d2d(j, 2 * cx + cy, my_rows) for j, (cx, cy) in enumerate(chips)]
        for j, (cx, cy) in enumerate(chips):
            over_ici(j, 2 * cx + cy, (cx, cy, c)).wait_recv()
            passed[j].start()
        for j, (cx, cy) in enumerate(chips):
            over_d2d(j, 2 * cx + cy, sib_rows).wait_recv()
        for cp in first + passed:
            cp.wait_send()
        own.wait()

    return pl.pallas_call(
        body, name="gather_weights", in_specs=[HBM_SPEC], out_specs=HBM_SPEC,
        out_shape=jax.ShapeDtypeStruct((4,) + mine.shape, mine.dtype),
        scratch_shapes=[pltpu.SemaphoreType.DMA((3,)), pltpu.SemaphoreType.DMA((3,)), pltpu.SemaphoreType.DMA((3,)),
                        pltpu.SemaphoreType.DMA((3,)), pltpu.SemaphoreType.DMA],
    )(mine)


def _swap_halves(g):
    def body(g_ref, got_ref, send_sem, recv_sem):
        x, y, c = _me()
        theirs = pl.multiple_of((1 - c) * HALF_ROWS, 8)
        cp = pltpu.make_async_remote_copy(
            src_ref=g_ref.at[:, pl.ds(theirs, HALF_ROWS), :], dst_ref=got_ref, send_sem=send_sem,
            recv_sem=recv_sem, device_id=(x, y, 1 - c), device_id_type=MESH)
        cp.start()
        cp.wait()

    return pl.pallas_call(
        body, name="swap_halves", in_specs=[HBM_SPEC], out_specs=HBM_SPEC,
        out_shape=jax.ShapeDtypeStruct((4, HALF_ROWS, 1024), F32),
        scratch_shapes=[pltpu.SemaphoreType.DMA, pltpu.SemaphoreType.DMA],
    )(g)


def _pair_sum(mine_half, got):
    def body(a_ref, b_ref, o_ref):
        o_ref[...] = (a_ref[...] + b_ref[...]).astype(BF16)

    spec = pl.BlockSpec((1, XCH_ROWS, 1024), lambda k, i: (k, i, 0))
    return pl.pallas_call(
        body, name="pair_sum", grid=(4, HALF_ROWS // XCH_ROWS), in_specs=[spec, spec], out_specs=spec,
        out_shape=jax.ShapeDtypeStruct(got.shape, BF16),
        compiler_params=_params(32, ("arbitrary", "arbitrary")),
    )(mine_half, got)


def _exchange_chips(ps):
    def body(ps_ref, out_ref, send_sems, recv_sems, local_sem):
        x, y, c = _me()
        k_me = 2 * x + y
        chips = [(1 - x, y), (x, 1 - y), (1 - x, 1 - y)]
        own = pltpu.make_async_copy(ps_ref.at[k_me], out_ref.at[k_me], local_sem)
        own.start()
        sends = [pltpu.make_async_remote_copy(
            src_ref=ps_ref.at[2 * cx + cy], dst_ref=out_ref.at[k_me], send_sem=send_sems.at[j],
            recv_sem=recv_sems.at[j], device_id=(cx, cy, c), device_id_type=MESH)
            for j, (cx, cy) in enumerate(chips)]
        for s in sends:
            s.start()
        for j, (cx, cy) in enumerate(chips):
            pltpu.make_async_remote_copy(
                src_ref=ps_ref.at[k_me], dst_ref=out_ref.at[2 * cx + cy], send_sem=send_sems.at[j],
                recv_sem=recv_sems.at[j], device_id=(cx, cy, c), device_id_type=MESH).wait_recv()
        for s in sends:
            s.wait_send()
        own.wait()

    return pl.pallas_call(
        body, name="exchange_chips", in_specs=[HBM_SPEC], out_specs=HBM_SPEC,
        out_shape=jax.ShapeDtypeStruct(ps.shape, ps.dtype),
        scratch_shapes=[pltpu.SemaphoreType.DMA((3,)), pltpu.SemaphoreType.DMA((3,)), pltpu.SemaphoreType.DMA],
    )(ps)


def _adamw(w, g, m, v):
    m = B1 * m + (1.0 - B1) * g
    v = B2 * v + (1.0 - B2) * (g * g)
    delta = -LR * ((m / BC1) / (jnp.sqrt(v / BC2) + AEPS) + WD * w)
    return delta, m, v


def _reduce_chips(parts):
    def body(p_ref, o_ref):
        f = lambda k: p_ref[k].astype(F32)
        o_ref[...] = ((f(0) + f(1)) + f(2)) + f(3)

    return pl.pallas_call(
        body, name="reduce_chips", grid=(HALF_ROWS // XCH_ROWS,),
        in_specs=[pl.BlockSpec((4, XCH_ROWS, 1024), lambda i: (0, i, 0))],
        out_specs=pl.BlockSpec((XCH_ROWS, 1024), lambda i: (i, 0)),
        out_shape=jax.ShapeDtypeStruct((HALF_ROWS, 1024), F32), compiler_params=_params(32),
    )(parts)


def _share_grad(gh):
    def body(g_ref, got_ref, send_sem, recv_sem):
        x, y, c = _me()
        cp = pltpu.make_async_remote_copy(
            src_ref=g_ref, dst_ref=got_ref, send_sem=send_sem, recv_sem=recv_sem,
            device_id=(x, y, 1 - c), device_id_type=MESH)
        cp.start()
        cp.wait()

    return pl.pallas_call(
        body, name="share_grad", in_specs=[HBM_SPEC], out_specs=HBM_SPEC,
        out_shape=jax.ShapeDtypeStruct((HALF_ROWS, 1024), F32),
        scratch_shapes=[pltpu.SemaphoreType.DMA, pltpu.SemaphoreType.DMA],
    )(gh)


def _update(g, w, m, v):
    def body(g_ref, w_ref, m_ref, v_ref, o_ref):
        delta, mn, vn = _adamw(w_ref[...], g_ref[...], m_ref[...], v_ref[...])
        o_ref[0] = delta
        o_ref[1] = mn
        o_ref[2] = vn

    s1 = pl.BlockSpec((RED_ROWS, 1024), lambda i: (i, 0))
    return pl.pallas_call(
        body, name="update", grid=(SHARD_ROWS // RED_ROWS,), in_specs=[s1, s1, s1, s1],
        out_specs=pl.BlockSpec((3, RED_ROWS, 1024), lambda i: (0, i, 0)),
        out_shape=jax.ShapeDtypeStruct((3, SHARD_ROWS, 1024), F32), compiler_params=_params(32),
    )(g, w, m, v)


SMALL_NAMES = ("sg_w",) + VEC_NAMES
VEC_ROWS = 24
VEC_ROW = {"f_bias": 0, "sg_ln_g": 1, "sg_ln_b": 2, "att_out_g": 3, "sg_out_g": 4, "pre_mix_g": 5,
           "post_mix_g": 6, "pre_ffn_g": 7, "sg_b": 8, "post_ffn_g": 16, "ple_gate_b": 17}
LOSS_VEC_ROW = 18


def _small_allreduce_update(g, w, m, v, loss_l):
    n = len(SMALL_NAMES)

    def body(*refs):
        g_r = dict(zip(SMALL_NAMES, refs[0:n]))
        w_r = dict(zip(SMALL_NAMES, refs[n:2 * n]))
        m_r = dict(zip(SMALL_NAMES, refs[2 * n:3 * n]))
        v_r = dict(zip(SMALL_NAMES, refs[3 * n:4 * n]))
        loss_r = refs[4 * n]
        loss_o = refs[4 * n + 1]
        outs = refs[4 * n + 2:8 * n + 2]
        bufv, bufw, send_sems, recv_sems = refs[8 * n + 2:]
        x, y, c = _me()
        me = 4 * x + 2 * y + c
        bufv[me] = jnp.zeros((VEC_ROWS, 1024), F32)
        for name in VEC_NAMES:
            val = g_r[name][...]
            bufv[me, pl.ds(VEC_ROW[name], val.shape[0]), pl.ds(0, val.shape[1])] = val
        bufv[me, pl.ds(LOSS_VEC_ROW, 1), :] = loss_r[...] * (0.5 / D)
        rr = lax.broadcasted_iota(jnp.int32, (CH, CH), 0)
        cc = lax.broadcasted_iota(jnp.int32, (CH, CH), 1)
        bufw[me] = jnp.where((cc <= rr)[None], g_r["sg_w"][...], 0.0)

        rels = [(rx, ry, rc) for rx in (0, 1) for ry in (0, 1) for rc in (0, 1)][1:]

        def peer(r):
            return ((x + r[0]) % 2, (y + r[1]) % 2, (c + r[2]) % 2)

        def copies(j, slot, to):
            return [pltpu.make_async_remote_copy(
                src_ref=buf.at[slot], dst_ref=buf.at[slot], send_sem=send_sems.at[2 * j + i],
                recv_sem=recv_sems.at[2 * j + i], device_id=to, device_id_type=MESH)
                for i, buf in enumerate((bufv, bufw))]

        sends = [cp for j, r in enumerate(rels) for cp in copies(j, me, peer(r))]
        for cp in sends:
            cp.start()
        for j, r in enumerate(rels):
            px, py, pc = peer(r)
            for cp in copies(j, 4 * px + 2 * py + pc, peer(r)):
                cp.wait_recv()
        for cp in sends:
            cp.wait_send()

        tot_v = bufv[0]
        tot_w = bufw[0]
        for d in range(1, 8):
            tot_v = tot_v + bufv[d]
            tot_w = tot_w + bufw[d]
        loss_o[...] = jnp.sum(tot_v[LOSS_VEC_ROW:LOSS_VEC_ROW + 1, :], axis=-1, keepdims=True) + jnp.zeros((1, 128), F32)
        for i, name in enumerate(SMALL_NAMES):
            if name == "sg_w":
                gt = tot_w
            else:
                rows, width = w_r[name].shape
                gt = tot_v[VEC_ROW[name]:VEC_ROW[name] + rows, 0:width]
            delta, mn, vn = _adamw(w_r[name][...], gt, m_r[name][...], v_r[name][...])
            outs[4 * i][...] = gt
            outs[4 * i + 1][...] = delta
            outs[4 * i + 2][...] = mn
            outs[4 * i + 3][...] = vn

    vm = pl.BlockSpec(memory_space=pltpu.VMEM)
    args = [d[k] for d in (g, w, m, v) for k in SMALL_NAMES] + [loss_l]
    out_shape = [jax.ShapeDtypeStruct((1, 128), F32)]
    out_shape += [jax.ShapeDtypeStruct(w[k].shape, F32) for k in SMALL_NAMES for _ in range(4)]
    res = pl.pallas_call(
        body, name="small_allreduce_update", in_specs=[vm] * len(args), out_specs=[vm] * len(out_shape),
        out_shape=out_shape,
        scratch_shapes=[pltpu.VMEM((8, VEC_ROWS, 1024), F32), pltpu.VMEM((8, 8, CH, CH), F32),
                        pltpu.SemaphoreType.DMA((14,)), pltpu.SemaphoreType.DMA((14,))],
        compiler_params=pltpu.CompilerParams(vmem_limit_bytes=32 * 1024 * 1024),
    )(*args)
    return res[0], {k: res[1 + 4 * i:5 + 4 * i] for i, k in enumerate(SMALL_NAMES)}


def _pack_shard(w_in, w_out, w1, w2, plew, wg):
    return jnp.concatenate([
        jnp.pad(w_in, ((0, 0), (0, 768 - 642))).reshape(768, 1024), w_out, w1, w2,
        plew.reshape(64, 1024), wg], axis=0)


def _unpack_shard(pk):
    r = 0
    out = []
    for rows, shape in ((768, (1024, 768)), (256, (256, 1024)), (1024, (1024, 1024)), (1024, (1024, 1024)),
                        (64, (256, 256)), (256, (256, 1024))):
        out.append(pk[r:r + rows].reshape(shape))
        r += rows
    out[0] = out[0][:, :642]
    return out


def _full_weights(gathered):
    parts = [_unpack_shard(gathered[k]) for k in range(4)]
    w_in = jnp.concatenate([p[0] for p in parts], axis=1)
    w_in = jnp.concatenate([w_in[:, :3 * AW], w_in[:, 3 * AW + NH:], w_in[:, 3 * AW:3 * AW + NH],
                            jnp.zeros((D, 128 - NH), w_in.dtype)], axis=1)
    return (w_in, jnp.concatenate([p[1] for p in parts], axis=0), jnp.concatenate([p[2] for p in parts], axis=1),
            jnp.concatenate([p[3] for p in parts], axis=0), jnp.concatenate([p[4] for p in parts], axis=1),
            jnp.concatenate([p[5] for p in parts], axis=0))


def _pack_grads(dwin_k, dwout, dw1, dw2, dplew, dwg):
    dwin = jnp.concatenate([dwin_k[:, :3 * AW], dwin_k[:, 5 * AW:5 * AW + NH], dwin_k[:, 3 * AW:5 * AW]], axis=1)
    return jnp.stack([
        _pack_shard(dwin[:, 642 * k:642 * (k + 1)], dwout[256 * k:256 * (k + 1)], dw1[:, 1024 * k:1024 * (k + 1)],
                    dw2[1024 * k:1024 * (k + 1)], dplew[:, 256 * k:256 * (k + 1)], dwg[256 * k:256 * (k + 1)])
        for k in range(4)])


def _local_step(x, p, tgt, win_k, wout, w1, w2, plew, wg, small):
    T = x.shape[0]
    row = lambda n: small[n].reshape(1, -1)
    fbias = jnp.pad(row("f_bias"), ((0, 0), (0, 128 - NH)))
    wm = _masked_sg_w(small["sg_w"].reshape(8, CH, CH))
    wmb = wm.astype(BF16)
    wmt = jnp.swapaxes(wm, 1, 2).astype(BF16)
    bsg = jnp.repeat(small["sg_b"].reshape(8, CH).T, DH, axis=1)
    ln_g, ln_b, gsg, gatt = row("sg_ln_g"), row("sg_ln_b"), row("sg_out_g"), row("att_out_g")
    gpre, gpm, gpf, gpff, bg = row("pre_mix_g"), row("post_mix_g"), row("pre_ffn_g"), row("post_ffn_g"), row("ple_gate_b")
    gsel = (jnp.arange(AW)[:, None] // DH == jnp.arange(128)[None, :]).astype(BF16)
    hsel = (jnp.arange(AW)[:, None] // DH == jnp.arange(AW)[None, :] // DH).astype(BF16)

    expand, shrink, pieces, qconst, _, pick64 = _head_consts()
    a, qkv, flog, ccol, zuv, ysgn, q8, k8, v8 = _pre_attn_fwd(
        x, gpre, win_k, fbias, ln_g, ln_b, wmb, bsg, gsg, expand, pieces, qconst)

    nt = T // TQ
    hd = lambda t, i: t[:, i * AW:(i + 1) * AW].reshape(T, NH, DH)
    zpad = lambda n: jnp.zeros((T, NH, n), BF16)
    one = jnp.ones((T, NH, 1), BF16)
    wide = lambda parts: jnp.concatenate(parts, axis=-1).reshape(T, NH * 128)
    slabs = lambda t: jnp.swapaxes(t.reshape(nt, TQ, NH * 128), 1, 2)
    qt8 = slabs(q8)
    kt8 = slabs(wide([hd(qkv, 1), one, zpad(63)]))
    vt8 = slabs(wide([hd(qkv, 2), one, zpad(63)]))
    lanes = jnp.arange(128)
    sel = jnp.stack([((lanes[:, None] == lanes[None, :] - DH * j) & (lanes[:, None] < DH)).astype(BF16)
                     for j in (0, 1)])

    yatt, lse = _flash_fwd(qt8, k8, vt8, sel)
    y, ov, h1, c2, sact, rr = _tail_fwd1(x, yatt, ysgn, gatt, wout, gpm, gpf, w1)
    ff, h2b, de, dpre, dh2, loss_l, dbg = _tail_fwd2(sact, h1, p, tgt, w2, gpff, wg, bg, plew)
    dff, dr, do, do8, dlt, dysg, dh1, dgpff, dgpf, dgpm, dgatt = _tail_bwd(
        dh2, ff, rr, h1, ov, yatt, w2, w1, wout, gpff, gpf, gpm, gatt, hsel, expand)
    dlt4 = jnp.pad(dlt[:, ::DH].T.reshape(4, 2, T), ((0, 0), (0, 6), (0, 0)))
    dqt, dk8, dv8 = _flash_bwd(q8, qt8, k8, kt8, v8, do8, slabs(do8), lse, dlt4)
    dx, dz, dgpre, dfb, dgsg, dlng, dlnb, dws, _, dsbt = _pre_attn_bwd(
        x, dh1, jnp.swapaxes(dqt, 1, 2).reshape(T, NH * 128), dk8, dv8, flog, zuv, dysg,
        gpre, win_k, ln_g, ln_b, wmb, wmt, bsg, gsg, gsel, shrink, pick64)

    dwin_k = _matmul_tn("grad_w_in", a, dz, tn=384)
    dwout = _matmul_tn("grad_w_out", y, do)
    dw1 = _matmul_tn("grad_w_ff1", c2, dr)
    dw2 = _matmul_tn("grad_w_ff2", sact, dff)
    dwg = _matmul_tn("grad_ple_gate_w", h2b, dpre)
    dplew = _matmul_tn("grad_ple_w", p, de)

    dsb = dsbt[:, :8].T
    gsmall = {"sg_w": dws, "f_bias": dfb, "sg_ln_g": dlng, "sg_ln_b": dlnb, "sg_b": dsb,
              "att_out_g": dgatt, "sg_out_g": dgsg, "pre_mix_g": dgpre, "post_mix_g": dgpm, "pre_ffn_g": dgpf,
              "post_ffn_g": dgpff, "ple_gate_b": dbg}
    return loss_l, dx, (dwin_k, dwout, dw1, dw2, dplew, dwg), gsmall


def kernel(x, p, w_in, f_bias, sg_ln_g, sg_ln_b, sg_w, sg_b, att_out_g, sg_out_g, w_out, pre_mix_g, post_mix_g, pre_ffn_g, post_ffn_g, w_ff1, w_ff2, ple_w, ple_gate_w, ple_gate_b, loss_target, m_w_in, m_f_bias, m_sg_ln_g, m_sg_ln_b, m_sg_w, m_sg_b, m_att_out_g, m_sg_out_g, m_w_out, m_pre_mix_g, m_post_mix_g, m_pre_ffn_g, m_post_ffn_g, m_w_ff1, m_w_ff2, m_ple_w, m_ple_gate_w, m_ple_gate_b, v_w_in, v_f_bias, v_sg_ln_g, v_sg_ln_b, v_sg_w, v_sg_b, v_att_out_g, v_sg_out_g, v_w_out, v_pre_mix_g, v_post_mix_g, v_pre_ffn_g, v_post_ffn_g, v_w_ff1, v_w_ff2, v_ple_w, v_ple_gate_w, v_ple_gate_b):
    c = lax.axis_index("c")
    big = lambda t: (t[0][0], t[1][0], t[2][0], t[3][0], t[4][0], t[5][0])
    w_big = big((w_in, w_out, w_ff1, w_ff2, ple_w, ple_gate_w))
    m_big = big((m_w_in, m_w_out, m_w_ff1, m_w_ff2, m_ple_w, m_ple_gate_w))
    v_big = big((v_w_in, v_w_out, v_w_ff1, v_w_ff2, v_ple_w, v_ple_gate_w))
    small = {"sg_w": sg_w, "f_bias": f_bias, "sg_ln_g": sg_ln_g, "sg_ln_b": sg_ln_b, "sg_b": sg_b,
             "att_out_g": att_out_g, "sg_out_g": sg_out_g, "pre_mix_g": pre_mix_g, "post_mix_g": post_mix_g,
             "pre_ffn_g": pre_ffn_g, "post_ffn_g": post_ffn_g, "ple_gate_b": ple_gate_b}
    m_small = {"sg_w": m_sg_w, "f_bias": m_f_bias, "sg_ln_g": m_sg_ln_g, "sg_ln_b": m_sg_ln_b, "sg_b": m_sg_b,
               "att_out_g": m_att_out_g, "sg_out_g": m_sg_out_g, "pre_mix_g": m_pre_mix_g,
               "post_mix_g": m_post_mix_g, "pre_ffn_g": m_pre_ffn_g, "post_ffn_g": m_post_ffn_g,
               "ple_gate_b": m_ple_gate_b}
    v_small = {"sg_w": v_sg_w, "f_bias": v_f_bias, "sg_ln_g": v_sg_ln_g, "sg_ln_b": v_sg_ln_b, "sg_b": v_sg_b,
               "att_out_g": v_att_out_g, "sg_out_g": v_sg_out_g, "pre_mix_g": v_pre_mix_g,
               "post_mix_g": v_post_mix_g, "pre_ffn_g": v_pre_ffn_g, "post_ffn_g": v_post_ffn_g,
               "ple_gate_b": v_ple_gate_b}

    w_pk = _pack_shard(*w_big)
    gathered = _gather_weights(w_pk.astype(BF16))
    win_k, wout, w1, w2, plew, wg = _full_weights(gathered)

    loss_l, dx, gbig, gsmall = _local_step(x[0], p[0, 0], loss_target[0], win_k, wout, w1, w2, plew, wg, small)

    gp = _pack_grads(*gbig)
    got = _swap_halves(gp)
    mine_half = lax.dynamic_slice_in_dim(gp, c * HALF_ROWS, HALF_ROWS, axis=1)
    parts = _exchange_chips(_pair_sum(mine_half, got))
    gh = _reduce_chips(parts)
    got = _share_grad(gh)
    g_full = jnp.where(c == 0, jnp.concatenate([gh, got], axis=0), jnp.concatenate([got, gh], axis=0))
    upd = _update(g_full, w_pk, _pack_shard(*m_big), _pack_shard(*v_big))
    big_out = [_unpack_shard(t) for t in (g_full, upd[0], upd[1], upd[2])]

    view = lambda t: t.reshape(t.shape[-3:]) if t.ndim == 4 else t.reshape(t.shape[-2:])
    views = lambda d: {k: view(d[k]) for k in SMALL_NAMES}
    loss11, res_s = _small_allreduce_update(gsmall, views(small), views(m_small), views(v_small), loss_l)
    loss = loss11[0, 0]

    def small_out(i, name):
        return res_s[name][i].reshape(small[name].shape)

    order = ["w_in", "f_bias", "sg_ln_g", "sg_ln_b", "sg_w", "sg_b", "att_out_g", "sg_out_g", "w_out",
             "pre_mix_g", "post_mix_g", "pre_ffn_g", "post_ffn_g", "w_ff1", "w_ff2", "ple_w", "ple_gate_w",
             "ple_gate_b"]
    big_idx = {"w_in": 0, "w_out": 1, "w_ff1": 2, "w_ff2": 3, "ple_w": 4, "ple_gate_w": 5}
    outs = [loss, dx[None]]
    for i in range(4):
        for name in order:
            if name in big_idx:
                outs.append(big_out[i][big_idx[name]][None])
            else:
                outs.append(small_out(i, name))
    return tuple(outs)
```

```python
import math

import jax
import jax.numpy as jnp
from jax import lax
from jax.experimental import pallas as pl
from jax.experimental.pallas import tpu as pltpu

F32 = jnp.float32
BF16 = jnp.bfloat16
MESH = pl.DeviceIdType.MESH

D = 1024
DH = 64
NH = 8
AW = 512
CH = 128
DFF = 4096
ZW = 5 * AW + 128
EPS = 1e-6
NEG = -1e30
MASKED = -2e30

TM = 256
TQ = 256

LR, B1, B2, AEPS, WD, STEP = 0.001, 0.9, 0.999, 1e-08, 0.01, 10
BC1 = 1.0 - B1 ** STEP
BC2 = 1.0 - B2 ** STEP

SHARD_ROWS = 768 + 256 + 1024 + 1024 + 64 + 256
HALF_ROWS = SHARD_ROWS // 2
RED_ROWS = HALF_ROWS // 4
XCH_ROWS = HALF_ROWS // 2
VEC_NAMES = ("f_bias", "sg_ln_g", "sg_ln_b", "sg_b", "att_out_g", "sg_out_g", "pre_mix_g",
             "post_mix_g", "pre_ffn_g", "post_ffn_g", "ple_gate_b")


def _dot(a, b):
    return jnp.dot(a, b, preferred_element_type=F32)


def _dot_nt(a, b):
    return lax.dot_general(a, b, (((1,), (1,)), ((), ())), preferred_element_type=F32)


def _dot_tn(a, b):
    return lax.dot_general(a, b, (((0,), (0,)), ((), ())), preferred_element_type=F32)


def _split3(x):
    h = x.astype(BF16)
    r = x - h.astype(F32)
    m = r.astype(BF16)
    l = (r - m.astype(F32)).astype(BF16)
    return h, m, l


def _dot01(sel, x):
    h, m, l = _split3(x)
    return _dot(sel, h) + _dot(sel, m) + _dot(sel, l)


def _dot01_r(x, sel):
    h, m, l = _split3(x)
    return _dot(h, sel) + _dot(m, sel) + _dot(l, sel)


def _dot01_tn(x, sel):
    h, m, l = _split3(x)
    return _dot_tn(h, sel) + _dot_tn(m, sel) + _dot_tn(l, sel)


def _rs(x, n):
    return lax.rsqrt(jnp.sum(x * x, axis=-1, keepdims=True) * (1.0 / n) + EPS)


def _rms_bwd(dn, x, rs, g, n):
    w = dn * g
    dx = rs * w - x * ((rs * rs * rs) * (1.0 / n) * jnp.sum(w * x, axis=-1, keepdims=True))
    return dx, jnp.sum(dn * x * rs, axis=0, keepdims=True)


_GC = math.sqrt(2.0 / math.pi)


def _gelu(x):
    t = jnp.tanh(_GC * (x + 0.044715 * x * x * x))
    return 0.5 * x * (1.0 + t), t


def _gelu_grad(x, t):
    return 0.5 * (1.0 + t) + 0.5 * x * (1.0 - t * t) * (_GC * (1.0 + 3.0 * 0.044715 * x * x))


def _params(vmem_mb, sem=("arbitrary",)):
    return pltpu.CompilerParams(dimension_semantics=sem, vmem_limit_bytes=vmem_mb * 1024 * 1024)


def _row_call(name, body, T, tm, tiled, resident, outs, accs, scratch=(), reverse=False, vmem_mb=48):
    nt = T // tm
    n_t, n_r, n_o, n_a = len(tiled), len(resident), len(outs), len(accs)

    def kern(*refs):
        t_refs = refs[:n_t]
        r_hbm = refs[n_t:n_t + n_r]
        o_refs = refs[n_t + n_r:n_t + n_r + n_o]
        a_refs = refs[n_t + n_r + n_o:n_t + n_r + n_o + n_a]
        r_vmem = refs[n_t + n_r + n_o + n_a:n_t + 2 * n_r + n_o + n_a]
        s_refs = refs[n_t + 2 * n_r + n_o + n_a:]

        @pl.when(pl.program_id(0) == 0)
        def _():
            for h, v in zip(r_hbm, r_vmem):
                pltpu.sync_copy(h, v)
            for a in a_refs + s_refs:
                a[...] = jnp.zeros(a.shape, a.dtype)

        body(t_refs, r_vmem, o_refs, a_refs, s_refs)

    if reverse:
        idx = lambda i: (nt - 1 - i, 0)
        idx_t = lambda i: (nt - 1 - i, 0, 0)
    else:
        idx = lambda i: (i, 0)
        idx_t = lambda i: (i, 0, 0)
    arrays, in_specs = [], []
    for a in tiled:
        if isinstance(a, tuple):
            arrays.append(a[0])
            in_specs.append(pl.BlockSpec((None, a[0].shape[1], tm), idx_t))
        else:
            arrays.append(a)
            in_specs.append(pl.BlockSpec((tm, a.shape[1]), idx))
    in_specs += [pl.BlockSpec(memory_space=pl.ANY) for _ in resident]
    out_shape, out_specs = [], []
    for o in outs:
        if len(o) == 3:
            out_shape.append(jax.ShapeDtypeStruct((nt, o[0], tm), o[1]))
            out_specs.append(pl.BlockSpec((None, o[0], tm), idx_t))
        else:
            out_shape.append(jax.ShapeDtypeStruct((T, o[0]), o[1]))
            out_specs.append(pl.BlockSpec((tm, o[0]), idx))
    out_shape += [jax.ShapeDtypeStruct(s, F32) for s in accs]
    out_specs += [pl.BlockSpec(s, lambda i, n=len(s): (0,) * n) for s in accs]
    scratch_shapes = [pltpu.VMEM(r.shape, r.dtype) for r in resident]
    scratch_shapes += [pltpu.VMEM(s, F32) for s in scratch]
    return pl.pallas_call(
        kern, name=name, grid=(nt,), in_specs=in_specs, out_specs=out_specs, out_shape=out_shape,
        scratch_shapes=scratch_shapes, compiler_params=_params(vmem_mb),
    )(*arrays, *resident)


def _sg_forward(zu, zv, wm_ref, bsg, lng, lnb, mixed_ref, tm):
    gu, tu = _gelu(zu)
    vg, tv = _gelu(zv)
    mu = jnp.sum(vg, axis=-1, keepdims=True) * (1.0 / AW)
    xc = vg - mu
    rstd = lax.rsqrt(jnp.sum(xc * xc, axis=-1, keepdims=True) * (1.0 / AW) + EPS)
    xhat = xc * rstd
    vvb = (xhat * lng + lnb).astype(BF16)
    lane = lax.broadcasted_iota(jnp.int32, (CH, 128), 1)
    for c in range(tm // CH):
        for j in range(4):
            blk = vvb[c * CH:(c + 1) * CH, j * 128:(j + 1) * 128]
            m0 = _dot(wm_ref[2 * j], blk)
            m1 = _dot(wm_ref[2 * j + 1], blk)
            mixed_ref[c * CH:(c + 1) * CH, j * 128:(j + 1) * 128] = (
                jnp.where(lane < DH, m0, m1) + bsg[:, j * 128:(j + 1) * 128])
    return gu, tu, tv, xhat, rstd, vvb, mixed_ref[...]


def _head_consts():
    src = jnp.arange(AW)
    dst = (src // DH) * 128 + src % DH
    wide = jnp.arange(NH * 128)
    expand = (dst[:, None] == wide[None, :]).astype(BF16)
    heads = jnp.arange(128)
    pieces = jnp.stack([((heads[:, None] * 128 + DH + i == wide[None, :]) & (heads[:, None] < NH)).astype(BF16)
                        for i in range(3)])
    spare = wide % 128 - DH
    qconst = jnp.where((spare >= 0) & (spare < 3), -1.0, 0.0).astype(F32)[None, :]
    one64 = jnp.where(spare == 0, 1.0, 0.0).astype(F32)[None, :]
    pick64 = ((wide[:, None] == heads[None, :] * 128 + DH) & (heads[None, :] < NH)).astype(BF16)
    return expand, expand.T, pieces, qconst, one64, pick64


def _masked_sg_w(sg_w):
    r = lax.broadcasted_iota(jnp.int32, (CH, CH), 0)
    c = lax.broadcasted_iota(jnp.int32, (CH, CH), 1)
    return jnp.where((c <= r)[None], sg_w, 0.0)


def _pre_attn_fwd(x, gpre, win, fbias, lng, lnb, wm, bsg, gsg, expand, pieces, qconst):
    T = x.shape[0]
    tm = TM

    def body(t, r, o, a, s):
        (x_ref,) = t
        gpre_r, win_r, fb_r, lng_r, lnb_r, wm_r, bsg_r, gsg_r, ex_r, pc_r, qc_r = r
        a_o, qkv_o, flog_o, ccol_o, zuv_o, ysgn_o, q8_o, k8_o, v8_o = o
        carry_ref, mixed_ref = s
        xv = x_ref[...]
        av = (xv * _rs(xv, D) * gpre_r[...]).astype(BF16)
        a_o[...] = av
        z = _dot(av, win_r[...])
        zu = z[:, 3 * AW:4 * AW]
        zv = z[:, 4 * AW:5 * AW]
        zuv_o[:, 0:AW] = zu
        zuv_o[:, AW:2 * AW] = zv
        zf = z[:, 5 * AW:] + fb_r[...]
        flog_o[...] = zf
        lane = lax.broadcasted_iota(jnp.int32, (tm, 128), 1)
        logf = jnp.where(lane < NH, jnp.minimum(zf, 0.0) - jnp.log(1.0 + jnp.exp(-jnp.abs(zf))), 0.0)
        rr = lax.broadcasted_iota(jnp.int32, (tm, tm), 0)
        cc = lax.broadcasted_iota(jnp.int32, (tm, tm), 1)
        tri = (cc <= rr).astype(BF16)
        cum = _dot01(tri, logf) + carry_ref[...]
        carry_ref[...] = cum[tm - 1:tm, :]
        ccol_o[...] = cum
        ex = ex_r[...]
        q8_o[...] = (_dot((z[:, 0:AW] * (DH ** -0.5)).astype(BF16), ex) + qc_r[...]).astype(BF16)
        ch, cm, cl = _split3(cum)
        k8_o[...] = (_dot(z[:, AW:2 * AW].astype(BF16), ex) + _dot(ch, pc_r[0]) + _dot(cm, pc_r[1])
                     + _dot(cl, pc_r[2])).astype(BF16)
        v8_o[...] = _dot(z[:, 2 * AW:3 * AW].astype(BF16), ex).astype(BF16)
        qkv_o[:, 0:AW] = (z[:, 0:AW] * (DH ** -0.5)).astype(BF16)
        qkv_o[:, AW:3 * AW] = z[:, AW:3 * AW].astype(BF16)
        gu, _, _, _, _, _, mixed = _sg_forward(zu, zv, wm_r, bsg_r[...], lng_r[...], lnb_r[...], mixed_ref, tm)
        ysg = gu * mixed
        ysgn_o[...] = (ysg * _rs(ysg, AW) * gsg_r[...]).astype(BF16)

    return _row_call(
        "pre_attn_fwd", body, T, tm, [x], [gpre, win, fbias, lng, lnb, wm, bsg, gsg, expand, pieces, qconst],
        [(D, BF16), (3 * AW, BF16), (128, F32), (128, F32), (2 * AW, F32), (AW, BF16), (NH * 128, BF16),
         (NH * 128, BF16), (NH * 128, BF16)], [],
        scratch=[(1, 128), (tm, AW)], vmem_mb=48)


def _flash_fwd(qt8, k8, vt8, sel):
    T = k8.shape[0]
    nq = T // TQ

    def body(qt_ref, k_ref, vt_ref, sel_ref, o_ref, l_ref, u_scr, p_scr):
        qi = pl.program_id(1)
        qts = (qt_ref[0:128, :], qt_ref[128:256, :])
        dmat = (lax.broadcasted_iota(jnp.int32, (TQ, TQ), 0) - lax.broadcasted_iota(jnp.int32, (TQ, TQ), 1))
        u_scr[1] = jnp.full((2, TQ, TQ), MASKED, F32)
        p_scr[...] = jnp.zeros(p_scr.shape, BF16)

        def sub(t, carry, sc, sb):
            blk_c = jnp.clip(t - 2, 0, qi)
            off_a = pl.multiple_of(jnp.minimum(t, qi) * TQ, TQ)
            lim = (qi - t) * TQ
            new = []
            for j in (0, 1):
                m, al, acc = carry[j]
                acc = al * acc + _dot(vt_ref[blk_c, j * 128:(j + 1) * 128, :], p_scr[sc, j])
                m_new = jnp.maximum(m, jnp.max(u_scr[sb, j], axis=0, keepdims=True))
                p_scr[sb, j] = jnp.exp(u_scr[sb, j] - m_new).astype(BF16)
                u_scr[sc, j] = jnp.where(
                    dmat <= lim, _dot(k_ref[pl.ds(off_a, TQ), j * 128:(j + 1) * 128], qts[j]), MASKED)
                new.append((m_new, jnp.exp(m - m_new), acc))
            return tuple(new)

        def it(t2, carry):
            return sub(2 * t2 + 1, sub(2 * t2, carry, 0, 1), 1, 0)

        init = tuple((jnp.full((1, TQ), NEG, F32), jnp.ones((1, TQ), F32), jnp.zeros((128, TQ), F32))
                     for _ in (0, 1))
        (m0, _, a0), (m1, _, a1) = lax.fori_loop(0, (qi + 4) // 2, it, init)
        l0 = a0[DH:DH + 1, :]
        l1 = a1[DH:DH + 1, :]
        o_ref[...] = _dot01_tn(a0 * (1.0 / l0), sel_ref[0]) + _dot01_tn(a1 * (1.0 / l1), sel_ref[1])
        l_ref[0:1, :] = m0 + jnp.log(l0)
        l_ref[1:2, :] = m1 + jnp.log(l1)
        l_ref[2:8, :] = jnp.zeros((6, TQ), F32)

    return pl.pallas_call(
        body, name="flash_fwd", grid=(4, nq),
        in_specs=[pl.BlockSpec((None, 256, TQ), lambda h, i: (i, h, 0)),
                  pl.BlockSpec((T, 256), lambda h, i: (0, h)),
                  pl.BlockSpec((nq, 256, TQ), lambda h, i: (0, h, 0)),
                  pl.BlockSpec((2, 128, 128), lambda h, i: (0, 0, 0))],
        out_specs=[pl.BlockSpec((TQ, 128), lambda h, i: (i, h)),
                   pl.BlockSpec((None, 8, TQ), lambda h, i: (h, 0, i))],
        out_shape=[jax.ShapeDtypeStruct((T, AW), F32), jax.ShapeDtypeStruct((4, 8, T), F32)],
        scratch_shapes=[pltpu.VMEM((2, 2, TQ, TQ), F32), pltpu.VMEM((2, 2, TQ, TQ), BF16)],
        compiler_params=_params(40, ("arbitrary", "arbitrary")),
    )(qt8, k8, vt8, sel)


def _flash_bwd(q8, qt8, k8, kt8, v8, do8, dot8, lse, dlt):
    T = q8.shape[0]
    nk = T // TQ

    def body(q_ref, qt_ref, k_ref, kt_ref, v_ref, do_ref, dot_ref, l_ref, d_ref, dqt_ref, dk_ref, dv_ref,
             u_scr, dp_scr, p_scr, ds_scr):
        kb = pl.program_id(1)
        n = nk - kb

        @pl.when(kb == 0)
        def _():
            dqt_ref[...] = jnp.zeros(dqt_ref.shape, F32)

        dk_ref[...] = jnp.zeros(dk_ref.shape, F32)
        dv_ref[...] = jnp.zeros(dv_ref.shape, F32)
        u_scr[1] = jnp.full((2, TQ, TQ), MASKED, F32)
        dp_scr[1] = jnp.zeros((2, TQ, TQ), F32)
        p_scr[...] = jnp.zeros(p_scr.shape, BF16)
        ds_scr[...] = jnp.zeros(ds_scr.shape, BF16)
        dmat = (lax.broadcasted_iota(jnp.int32, (TQ, TQ), 0) - lax.broadcasted_iota(jnp.int32, (TQ, TQ), 1))
        ks = (k_ref[:, 0:128], k_ref[:, 128:256])
        vs = (v_ref[:, 0:128], v_ref[:, 128:256])
        kts = (kt_ref[0:128, :], kt_ref[128:256, :])

        def sub(t, sc, sb):
            blk_a = kb + jnp.minimum(t, n - 1)
            blk_c = kb + jnp.clip(t - 2, 0, n - 1)
            off_b = pl.multiple_of((kb + jnp.clip(t - 1, 0, n - 1)) * TQ, TQ)
            off_c = pl.multiple_of(blk_c * TQ, TQ)
            lim = jnp.where(t < n, t * TQ, -TQ)
            for j in (0, 1):
                hl = slice(j * 128, (j + 1) * 128)
                dqt_ref[blk_c, hl, :] += _dot(kts[j], ds_scr[sc, j])
                dk_ref[:, hl] += _dot(ds_scr[sc, j], q_ref[pl.ds(off_c, TQ), hl])
                dv_ref[:, hl] += _dot(p_scr[sc, j], do_ref[pl.ds(off_c, TQ), hl])
                p = jnp.exp(u_scr[sb, j] - l_ref[j:j + 1, pl.ds(off_b, TQ)])
                p_scr[sb, j] = p.astype(BF16)
                ds_scr[sb, j] = (p * (dp_scr[sb, j] - d_ref[j:j + 1, pl.ds(off_b, TQ)])).astype(BF16)
                u_scr[sc, j] = jnp.where(dmat <= lim, _dot(ks[j], qt_ref[blk_a, hl, :]), MASKED)
                dp_scr[sc, j] = _dot(vs[j], dot_ref[blk_a, hl, :])

        def it(t2, carry):
            sub(2 * t2, 0, 1)
            sub(2 * t2 + 1, 1, 0)
            return carry

        lax.fori_loop(0, (n + 3) // 2, it, 0)

    return pl.pallas_call(
        body, name="flash_bwd", grid=(4, nk),
        in_specs=[pl.BlockSpec((T, 256), lambda h, i: (0, h)),
                  pl.BlockSpec((nk, 256, TQ), lambda h, i: (0, h, 0)),
                  pl.BlockSpec((TQ, 256), lambda h, i: (i, h)),
                  pl.BlockSpec((None, 256, TQ), lambda h, i: (i, h, 0)),
                  pl.BlockSpec((TQ, 256), lambda h, i: (i, h)),
                  pl.BlockSpec((T, 256), lambda h, i: (0, h)),
                  pl.BlockSpec((nk, 256, TQ), lambda h, i: (0, h, 0)),
                  pl.BlockSpec((None, 8, T), lambda h, i: (h, 0, 0)),
                  pl.BlockSpec((None, 8, T), lambda h, i: (h, 0, 0))],
        out_specs=[pl.BlockSpec((nk, 256, TQ), lambda h, i: (0, h, 0)),
                   pl.BlockSpec((TQ, 256), lambda h, i: (i, h)),
                   pl.BlockSpec((TQ, 256), lambda h, i: (i, h))],
        out_shape=[jax.ShapeDtypeStruct((nk, NH * 128, TQ), F32), jax.ShapeDtypeStruct((T, NH * 128), F32),
                   jax.ShapeDtypeStruct((T, NH * 128), F32)],
        scratch_shapes=[pltpu.VMEM((2, 2, TQ, TQ), F32), pltpu.VMEM((2, 2, TQ, TQ), F32),
                        pltpu.VMEM((2, 2, TQ, TQ), BF16), pltpu.VMEM((2, 2, TQ, TQ), BF16)],
        compiler_params=_params(56, ("arbitrary", "arbitrary")),
    )(q8, qt8, k8, kt8, v8, do8, dot8, lse, dlt)


def _tail_fwd1(x, yatt, ysgn, gatt, wout, gpm, gpf, w1):
    T = x.shape[0]

    def body(t, r, o, a, s):
        x_ref, ya_ref, ys_ref = t
        gatt_r, wout_r, gpm_r, gpf_r, w1_r = r
        y_o, o_o, h1_o, c2_o, s_o, rr_o = o
        ya = ya_ref[...]
        yan = (ya * _rs(ya, AW) * gatt_r[...]).astype(BF16)
        y_o[:, 0:AW] = yan
        y_o[:, AW:] = ys_ref[...]
        ov = _dot(yan, wout_r[0:AW, :]) + _dot(ys_ref[...], wout_r[AW:, :])
        o_o[...] = ov
        h1 = x_ref[...] + ov * _rs(ov, D) * gpm_r[...]
        h1_o[...] = h1
        c2 = (h1 * _rs(h1, D) * gpf_r[...]).astype(BF16)
        c2_o[...] = c2
        rr = jnp.maximum(_dot(c2, w1_r[...]), 0.0)
        rr_o[...] = rr.astype(BF16)
        s_o[...] = (rr * rr).astype(BF16)

    return _row_call(
        "tail_fwd1", body, T, TM, [x, yatt, ysgn], [gatt, wout, gpm, gpf, w1],
        [(D, BF16), (D, F32), (D, F32), (D, BF16), (DFF, BF16), (DFF, BF16)], [], vmem_mb=48)


def _tail_fwd2(sact, h1, p, tgt, w2, gpff, wg, bg, wpe):
    T = h1.shape[0]

    def body(t, r, o, a, s):
        s_ref, h1_ref, p_ref, t_ref = t
        w2_r, gpff_r, wg_r, bg_r, wpe_r = r
        ff_o, h2b_o, de_o, dpre_o, dh2_o = o
        loss_a, dbg_a = a
        ff = _dot(s_ref[...], w2_r[...])
        ff_o[...] = ff
        h2 = h1_ref[...] + ff * _rs(ff, D) * gpff_r[...]
        h2b = h2.astype(BF16)
        h2b_o[...] = h2b
        gate = 1.0 / (1.0 + jnp.exp(-(_dot(h2b, wg_r[...]) + bg_r[...])))
        e = _dot(p_ref[...].astype(BF16), wpe_r[...])
        diff = h2 + gate * e - t_ref[...]
        loss_a[...] += jnp.sum(diff * diff, axis=0, keepdims=True)
        dh3 = diff * (1.0 / D)
        de_o[...] = (dh3 * gate).astype(BF16)
        dpre = dh3 * e * gate * (1.0 - gate)
        dbg_a[...] += jnp.sum(dpre, axis=0, keepdims=True)
        dpb = dpre.astype(BF16)
        dpre_o[...] = dpb
        dh2_o[...] = dh3 + _dot_nt(dpb, wg_r[...])

    return _row_call(
        "tail_fwd2", body, T, TM, [sact, h1, p, tgt], [w2, gpff, wg, bg, wpe],
        [(D, F32), (D, BF16), (D, BF16), (D, BF16), (D, F32)], [(1, D), (1, D)], vmem_mb=48)


def _tail_bwd(dh2, ff, rr, h1, ov, yatt, w2, w1, wout, gpff, gpf, gpm, gatt, hsel, expand):
    T = dh2.shape[0]

    def body(t, r, o, a, s):
        dh2_ref, ff_ref, rr_ref, h1_ref, o_ref, ya_ref = t
        w2_r, w1_r, wout_r, gpff_r, gpf_r, gpm_r, gatt_r, hsel_r, ex_r = r
        dff_o, dr_o, do_o, do8_o, dlt_o, dysg_o, dh1_o = o
        dgpff_a, dgpf_a, dgpm_a, dgatt_a = a
        dh2v = dh2_ref[...]
        ffv = ff_ref[...]
        dff, dg = _rms_bwd(dh2v, ffv, _rs(ffv, D), gpff_r[...], D)
        dgpff_a[...] += dg
        dffb = dff.astype(BF16)
        dff_o[...] = dffb
        drb = (_dot_nt(dffb, w2_r[...]) * (2.0 * rr_ref[...].astype(F32))).astype(BF16)
        dr_o[...] = drb
        dc2 = _dot_nt(drb, w1_r[...])
        h1v = h1_ref[...]
        d1, dg = _rms_bwd(dc2, h1v, _rs(h1v, D), gpf_r[...], D)
        dgpf_a[...] += dg
        dh1 = dh2v + d1
        dh1_o[...] = dh1
        ovv = o_ref[...]
        dov, dg = _rms_bwd(dh1, ovv, _rs(ovv, D), gpm_r[...], D)
        dgpm_a[...] += dg
        dob = dov.astype(BF16)
        do_o[...] = dob
        dysg_o[...] = _dot_nt(dob, wout_r[AW:, :])
        dyan = _dot_nt(dob, wout_r[0:AW, :])
        ya = ya_ref[...]
        dya, dg = _rms_bwd(dyan, ya, _rs(ya, AW), gatt_r[...], AW)
        dgatt_a[...] += dg
        do8_o[...] = _dot(dya.astype(BF16), ex_r[...]).astype(BF16)
        dlt_o[...] = _dot01_r(dya * ya, hsel_r[...])

    return _row_call(
        "tail_bwd", body, T, TM, [dh2, ff, rr, h1, ov, yatt],
        [w2, w1, wout, gpff, gpf, gpm, gatt, hsel, expand],
        [(D, BF16), (DFF, BF16), (D, BF16), (NH * 128, BF16), (AW, F32), (AW, F32), (D, F32)],
        [(1, D), (1, D), (1, D), (1, AW)], vmem_mb=56)


def _pre_attn_bwd(x, dh1, dq8, dk8, dv8, flog, zuv, dysg, gpre, win, lng, lnb, wm, wmt, bsg, gsg, gsel, shrink, pick64):
    T = x.shape[0]
    tm = TM

    def body(t, r, o, a, s):
        x_ref, dh1_ref, dq_ref, dk_ref, dv_ref, fl_ref, zuv_ref, dys_ref = t
        gpre_r, win_r, lng_r, lnb_r, wm_r, wmt_r, bsg_r, gsg_r, gsel_r, sh_r, p64_r = r
        dx_o, dz_o = o
        dgpre_a, dfb_a, dgsg_a, dlng_a, dlnb_a, dws_a, dbs_a, dsb_a = a
        carry_ref, mixed_ref, dvv_ref = s
        dq8v = dq_ref[...]
        dk8v = dk_ref[...]
        dcv = _dot01_r(dq8v, p64_r[...]) + _dot01_r(dk8v, p64_r[...])
        rr = lax.broadcasted_iota(jnp.int32, (tm, tm), 0)
        cc = lax.broadcasted_iota(jnp.int32, (tm, tm), 1)
        triu = (cc >= rr).astype(BF16)
        dlogf = _dot01(triu, dcv) + carry_ref[...]
        carry_ref[...] = dlogf[0:1, :]
        dzf = dlogf * (1.0 / (1.0 + jnp.exp(fl_ref[...])))
        dfb_a[...] += jnp.sum(dzf, axis=0, keepdims=True)
        dz_o[:, 5 * AW:] = dzf.astype(BF16)
        zu = zuv_ref[:, 0:AW]
        zv = zuv_ref[:, AW:]
        gu, tu, tv, xhat, rstd, vvb, mixed = _sg_forward(
            zu, zv, wm_r, bsg_r[...], lng_r[...], lnb_r[...], mixed_ref, tm)
        ysg = gu * mixed
        dysg_n = dys_ref[...]
        dys, dg = _rms_bwd(dysg_n, ysg, _rs(ysg, AW), gsg_r[...], AW)
        dgsg_a[...] += dg
        dgu = dys * mixed
        dmix = dys * gu
        dmb = dmix.astype(BF16)
        lane = lax.broadcasted_iota(jnp.int32, (CH, 128), 1)
        lo = lane < DH
        for c in range(tm // CH):
            rows = slice(c * CH, (c + 1) * CH)
            dbs_a[...] += dmix[rows, :]
            for j in range(4):
                cols = slice(j * 128, (j + 1) * 128)
                dmblk = dmb[rows, cols]
                vblk = vvb[rows, cols]
                d0 = _dot(wmt_r[2 * j], dmblk)
                d1 = _dot(wmt_r[2 * j + 1], dmblk)
                dvv_ref[rows, cols] = jnp.where(lo, d0, d1)
                dws_a[2 * j] += _dot_nt(jnp.where(lo, dmblk, jnp.zeros_like(dmblk)), vblk)
                dws_a[2 * j + 1] += _dot_nt(jnp.where(lo, jnp.zeros_like(dmblk), dmblk), vblk)
        dvv = dvv_ref[...]
        dlng_a[...] += jnp.sum(dvv * xhat, axis=0, keepdims=True)
        dlnb_a[...] += jnp.sum(dvv, axis=0, keepdims=True)
        dxh = dvv * lng_r[...]
        dvg = rstd * (dxh - jnp.sum(dxh, axis=-1, keepdims=True) * (1.0 / AW)
                      - xhat * (jnp.sum(dxh * xhat, axis=-1, keepdims=True) * (1.0 / AW)))
        dz_o[:, 3 * AW:4 * AW] = (dgu * _gelu_grad(zu, tu)).astype(BF16)
        dz_o[:, 4 * AW:5 * AW] = (dvg * _gelu_grad(zv, tv)).astype(BF16)
        dz_o[:, 0:AW] = _dot((dq8v * (DH ** -0.5)).astype(BF16), sh_r[...]).astype(BF16)
        dz_o[:, AW:2 * AW] = _dot(dk8v.astype(BF16), sh_r[...]).astype(BF16)
        dz_o[:, 2 * AW:3 * AW] = _dot(dv_ref[...].astype(BF16), sh_r[...]).astype(BF16)
        da = _dot_nt(dz_o[...], win_r[...])
        xv = x_ref[...]
        dxa, dg = _rms_bwd(da, xv, _rs(xv, D), gpre_r[...], D)
        dgpre_a[...] += dg
        dx_o[...] = dh1_ref[...] + dxa

        @pl.when(pl.program_id(0) == T // tm - 1)
        def _():
            dsb_a[...] = _dot01_r(dbs_a[...], gsel_r[...])

    outs = _row_call(
        "pre_attn_bwd", body, T, tm, [x, dh1, dq8, dk8, dv8, flog, zuv, dysg],
        [gpre, win, lng, lnb, wm, wmt, bsg, gsg, gsel, shrink, pick64],
        [(D, F32), (ZW, BF16)],
        [(1, D), (1, 128), (1, AW), (1, AW), (1, AW), (8, CH, CH), (CH, AW), (CH, 128)],
        scratch=[(1, 128), (tm, AW), (tm, AW)], reverse=True, vmem_mb=48)
    return outs


def _matmul_tn(name, a, b, tn=512, tt=2048, shards=1):
    T, K = a.shape
    N = b.shape[1]
    tk = min(K, 1024)
    tn = min(tn, N // shards)
    tt = min(tt, T)
    nj = N // shards // tn

    def body(a_ref, b_ref, o_ref):
        @pl.when(pl.program_id(2) == 0)
        def _():
            o_ref[...] = jnp.zeros(o_ref.shape, F32)

        o_ref[...] += _dot_tn(a_ref[...].astype(BF16), b_ref[...].astype(BF16))

    if shards == 1:
        out_shape = jax.ShapeDtypeStruct((K, N), F32)
        out_spec = pl.BlockSpec((tk, tn), lambda i, j, t: (i, j))
    else:
        out_shape = jax.ShapeDtypeStruct((shards, K, N // shards), F32)
        out_spec = pl.BlockSpec((None, tk, tn), lambda i, j, t: (j // nj, i, j % nj))
    return pl.pallas_call(
        body, name=name, grid=(K // tk, N // tn, T // tt),
        in_specs=[pl.BlockSpec((tt, tk), lambda i, j, t: (t, i)),
                  pl.BlockSpec((tt, tn), lambda i, j, t: (t, j))],
        out_specs=out_spec, out_shape=out_shape,
        compiler_params=_params(40, ("arbitrary", "arbitrary", "arbitrary")),
    )(a, b)


def _me():
    return lax.axis_index("x"), lax.axis_index("y"), lax.axis_index("c")


HBM_SPEC = pl.BlockSpec(memory_space=pltpu.HBM)


def _gather_weights(mine):
    def body(mine_ref, out_ref, ici_send, ici_recv, d2d_send, d2d_recv, local_sem):
        x, y, c = _me()
        k_me = 2 * x + y
        chips = [(1 - x, y), (x, 1 - y), (1 - x, 1 - y)]
        my_rows = pl.ds(pl.multiple_of(c * HALF_ROWS, 16), HALF_ROWS)
        sib_rows = pl.ds(pl.multiple_of((1 - c) * HALF_ROWS, 16), HALF_ROWS)

        def over_ici(j, k, to):
            src = mine_ref.at[my_rows] if k is None else out_ref.at[k, my_rows]
            return pltpu.make_async_remote_copy(
                src_ref=src, dst_ref=out_ref.at[k_me if k is None else k, my_rows], send_sem=ici_send.at[j],
                recv_sem=ici_recv.at[j], device_id=to, device_id_type=MESH)

        def over_d2d(j, k, rows):
            return pltpu.make_async_remote_copy(
                src_ref=out_ref.at[k, rows], dst_ref=out_ref.at[k, rows], send_sem=d2d_send.at[j],
                recv_sem=d2d_recv.at[j], device_id=(x, y, 1 - c), device_id_type=MESH)

        own = pltpu.make_async_copy(mine_ref, out_ref.at[k_me], local_sem)
        own.start()
        first = [over_ici(j, None, (cx, cy, c)) for j, (cx, cy) in enumerate(chips)]
        for cp in first:
            cp.start()
        passed = [over_d2d(j, 2 * cx + cy, my_rows) for j, (cx, cy) in enumerate(chips)]
        for j, (cx, cy) in enumerate(chips):
            over_ici(j, 2 * cx + cy, (cx, cy, c)).wait_recv()
            passed[j].start()
        for j, (cx, cy) in enumerate(chips):
            over_d2d(j, 2 * cx + cy, sib_rows).wait_recv()
        for cp in first + passed:
            cp.wait_send()
        own.wait()

    return pl.pallas_call(
        body, name="gather_weights", in_specs=[HBM_SPEC], out_specs=HBM_SPEC,
        out_shape=jax.ShapeDtypeStruct((4,) + mine.shape, mine.dtype),
        scratch_shapes=[pltpu.SemaphoreType.DMA((3,)), pltpu.SemaphoreType.DMA((3,)), pltpu.SemaphoreType.DMA((3,)),
                        pltpu.SemaphoreType.DMA((3,)), pltpu.SemaphoreType.DMA],
    )(mine)


def _swap_halves(gs):
    n = len(gs)

    def body(*refs):
        g_refs, got_refs, send_sems, recv_sems = refs[:n], refs[n:2 * n], refs[2 * n], refs[2 * n + 1]
        x, y, c = _me()
        cps = []
        for i, (g_ref, got_ref) in enumerate(zip(g_refs, got_refs)):
            half = g_ref.shape[1] // 2
            theirs = pl.multiple_of((1 - c) * half, 16)
            cps.append(pltpu.make_async_remote_copy(
                src_ref=g_ref.at[:, pl.ds(theirs, half), :], dst_ref=got_ref, send_sem=send_sems.at[i],
                recv_sem=recv_sems.at[i], device_id=(x, y, 1 - c), device_id_type=MESH))
        for cp in cps:
            cp.start()
        for cp in cps:
            cp.wait()

    return pl.pallas_call(
        body, name="swap_halves", in_specs=[HBM_SPEC] * n, out_specs=[HBM_SPEC] * n,
        out_shape=[jax.ShapeDtypeStruct((4, g.shape[1] // 2, g.shape[2]), F32) for g in gs],
        scratch_shapes=[pltpu.SemaphoreType.DMA((n,)), pltpu.SemaphoreType.DMA((n,))],
    )(*gs)


def _pair_sum(name, c1, g, got):
    half, cols = got.shape[1], got.shape[2]

    def body(c_ref, a_ref, b_ref, o_ref):
        o_ref[...] = (a_ref[...] + b_ref[...]).astype(BF16)

    return pl.pallas_call(
        body, name="pair_sum_" + name,
        grid_spec=pltpu.PrefetchScalarGridSpec(
            num_scalar_prefetch=1, grid=(4,),
            in_specs=[pl.BlockSpec((1, half, cols), lambda k, c_ref: (k, c_ref[0], 0)),
                      pl.BlockSpec((1, half, cols), lambda k, c_ref: (k, 0, 0))],
            out_specs=pl.BlockSpec((1, half, cols), lambda k, c_ref: (k, 0, 0))),
        out_shape=jax.ShapeDtypeStruct(got.shape, BF16), compiler_params=_params(32),
    )(c1, g, got)


def _exchange_chips(pss):
    n = len(pss)

    def body(*refs):
        ps_refs, out_refs = refs[:n], refs[n:2 * n]
        send_sems, recv_sems, local_sems = refs[2 * n:]
        x, y, c = _me()
        k_me = 2 * x + y
        chips = [(1 - x, y), (x, 1 - y), (1 - x, 1 - y)]
        owns, sends = [], []
        for i, (ps_ref, out_ref) in enumerate(zip(ps_refs, out_refs)):
            owns.append(pltpu.make_async_copy(ps_ref.at[k_me], out_ref.at[k_me], local_sems.at[i]))
            for j, (cx, cy) in enumerate(chips):
                sends.append(pltpu.make_async_remote_copy(
                    src_ref=ps_ref.at[2 * cx + cy], dst_ref=out_ref.at[k_me], send_sem=send_sems.at[3 * i + j],
                    recv_sem=recv_sems.at[3 * i + j], device_id=(cx, cy, c), device_id_type=MESH))
        for cp in owns + sends:
            cp.start()
        for i, (ps_ref, out_ref) in enumerate(zip(ps_refs, out_refs)):
            for j, (cx, cy) in enumerate(chips):
                pltpu.make_async_remote_copy(
                    src_ref=ps_ref.at[k_me], dst_ref=out_ref.at[2 * cx + cy], send_sem=send_sems.at[3 * i + j],
                    recv_sem=recv_sems.at[3 * i + j], device_id=(cx, cy, c), device_id_type=MESH).wait_recv()
        for cp in sends:
            cp.wait_send()
        for cp in owns:
            cp.wait()

    return pl.pallas_call(
        body, name="exchange_chips", in_specs=[HBM_SPEC] * n, out_specs=[HBM_SPEC] * n,
        out_shape=[jax.ShapeDtypeStruct(ps.shape, ps.dtype) for ps in pss],
        scratch_shapes=[pltpu.SemaphoreType.DMA((3 * n,)), pltpu.SemaphoreType.DMA((3 * n,)),
                        pltpu.SemaphoreType.DMA((n,))],
    )(*pss)


def _adamw(w, g, m, v):
    m = B1 * m + (1.0 - B1) * g
    v = B2 * v + (1.0 - B2) * (g * g)
    delta = -LR * ((m / BC1) / (jnp.sqrt(v / BC2) + AEPS) + WD * w)
    return delta, m, v


def _reduce_chips(name, parts):
    half, cols = parts.shape[1], parts.shape[2]

    def body(p_ref, o_ref):
        f = lambda k: p_ref[k].astype(F32)
        o_ref[...] = ((f(0) + f(1)) + f(2)) + f(3)

    return pl.pallas_call(
        body, name="reduce_chips_" + name, grid=(1,),
        in_specs=[pl.BlockSpec((4, half, cols), lambda i: (0, 0, 0))],
        out_specs=pl.BlockSpec((half, cols), lambda i: (0, 0)),
        out_shape=jax.ShapeDtypeStruct((half, cols), F32), compiler_params=_params(32),
    )(parts)


def _share_grad(ghs):
    n = len(ghs)

    def body(*refs):
        g_refs, got_refs, send_sems, recv_sems = refs[:n], refs[n:2 * n], refs[2 * n], refs[2 * n + 1]
        x, y, c = _me()
        cps = [pltpu.make_async_remote_copy(
            src_ref=g_ref, dst_ref=got_ref, send_sem=send_sems.at[i], recv_sem=recv_sems.at[i],
            device_id=(x, y, 1 - c), device_id_type=MESH) for i, (g_ref, got_ref) in enumerate(zip(g_refs, got_refs))]
        for cp in cps:
            cp.start()
        for cp in cps:
            cp.wait()

    return pl.pallas_call(
        body, name="share_grad", in_specs=[HBM_SPEC] * n, out_specs=[HBM_SPEC] * n,
        out_shape=[jax.ShapeDtypeStruct(g.shape, F32) for g in ghs],
        scratch_shapes=[pltpu.SemaphoreType.DMA((n,)), pltpu.SemaphoreType.DMA((n,))],
    )(*ghs)


def _update(name, c1, gh, got, w, m, v):
    half, cols = gh.shape

    def body(c_ref, gh_ref, got_ref, w_ref, m_ref, v_ref, g_o, d_o, m_o, v_o):
        g = jnp.where(pl.program_id(0) == c_ref[0], gh_ref[...], got_ref[...])
        delta, mn, vn = _adamw(w_ref[...], g, m_ref[...], v_ref[...])
        g_o[...] = g
        d_o[...] = delta
        m_o[...] = mn
        v_o[...] = vn

    same = pl.BlockSpec((half, cols), lambda h, c_ref: (0, 0))
    rows = pl.BlockSpec((half, cols), lambda h, c_ref: (h, 0))
    return pl.pallas_call(
        body, name="update_" + name,
        grid_spec=pltpu.PrefetchScalarGridSpec(
            num_scalar_prefetch=1, grid=(2,), in_specs=[same, same, rows, rows, rows],
            out_specs=[rows, rows, rows, rows]),
        out_shape=[jax.ShapeDtypeStruct(w.shape, F32)] * 4, compiler_params=_params(40),
    )(c1, gh, got, w, m, v)


SMALL_NAMES = ("sg_w",) + VEC_NAMES
VEC_ROWS = 24
VEC_ROW = {"f_bias": 0, "sg_ln_g": 1, "sg_ln_b": 2, "att_out_g": 3, "sg_out_g": 4, "pre_mix_g": 5,
           "post_mix_g": 6, "pre_ffn_g": 7, "sg_b": 8, "post_ffn_g": 16, "ple_gate_b": 17}
LOSS_VEC_ROW = 18


def _small_allreduce_update(g, w, m, v, loss_l):
    n = len(SMALL_NAMES)

    def body(*refs):
        g_r = dict(zip(SMALL_NAMES, refs[0:n]))
        w_r = dict(zip(SMALL_NAMES, refs[n:2 * n]))
        m_r = dict(zip(SMALL_NAMES, refs[2 * n:3 * n]))
        v_r = dict(zip(SMALL_NAMES, refs[3 * n:4 * n]))
        loss_r = refs[4 * n]
        loss_o = refs[4 * n + 1]
        outs = refs[4 * n + 2:8 * n + 2]
        bufv, bufw, send_sems, recv_sems = refs[8 * n + 2:]
        x, y, c = _me()
        me = 4 * x + 2 * y + c
        bufv[me] = jnp.zeros((VEC_ROWS, 1024), F32)
        for name in VEC_NAMES:
            val = g_r[name][...]
            bufv[me, pl.ds(VEC_ROW[name], val.shape[0]), pl.ds(0, val.shape[1])] = val
        bufv[me, pl.ds(LOSS_VEC_ROW, 1), :] = loss_r[...] * (0.5 / D)
        rr = lax.broadcasted_iota(jnp.int32, (CH, CH), 0)
        cc = lax.broadcasted_iota(jnp.int32, (CH, CH), 1)
        bufw[me] = jnp.where((cc <= rr)[None], g_r["sg_w"][...], 0.0)

        rels = [(rx, ry, rc) for rx in (0, 1) for ry in (0, 1) for rc in (0, 1)][1:]

        def peer(r):
            return ((x + r[0]) % 2, (y + r[1]) % 2, (c + r[2]) % 2)

        def copies(j, slot, to):
            return [pltpu.make_async_remote_copy(
                src_ref=buf.at[slot], dst_ref=buf.at[slot], send_sem=send_sems.at[2 * j + i],
                recv_sem=recv_sems.at[2 * j + i], device_id=to, device_id_type=MESH)
                for i, buf in enumerate((bufv, bufw))]

        sends = [cp for j, r in enumerate(rels) for cp in copies(j, me, peer(r))]
        for cp in sends:
            cp.start()
        for j, r in enumerate(rels):
            px, py, pc = peer(r)
            for cp in copies(j, 4 * px + 2 * py + pc, peer(r)):
                cp.wait_recv()
        for cp in sends:
            cp.wait_send()

        tot_v = bufv[0]
        tot_w = bufw[0]
        for d in range(1, 8):
            tot_v = tot_v + bufv[d]
            tot_w = tot_w + bufw[d]
        loss_o[...] = jnp.sum(tot_v[LOSS_VEC_ROW:LOSS_VEC_ROW + 1, :], axis=-1, keepdims=True) + jnp.zeros((1, 128), F32)
        for i, name in enumerate(SMALL_NAMES):
            if name == "sg_w":
                gt = tot_w
            else:
                rows, width = w_r[name].shape
                gt = tot_v[VEC_ROW[name]:VEC_ROW[name] + rows, 0:width]
            delta, mn, vn = _adamw(w_r[name][...], gt, m_r[name][...], v_r[name][...])
            outs[4 * i][...] = gt
            outs[4 * i + 1][...] = delta
            outs[4 * i + 2][...] = mn
            outs[4 * i + 3][...] = vn

    vm = pl.BlockSpec(memory_space=pltpu.VMEM)
    args = [d[k] for d in (g, w, m, v) for k in SMALL_NAMES] + [loss_l]
    out_shape = [jax.ShapeDtypeStruct((1, 128), F32)]
    out_shape += [jax.ShapeDtypeStruct(w[k].shape, F32) for k in SMALL_NAMES for _ in range(4)]
    res = pl.pallas_call(
        body, name="small_allreduce_update", in_specs=[vm] * len(args), out_specs=[vm] * len(out_shape),
        out_shape=out_shape,
        scratch_shapes=[pltpu.VMEM((8, VEC_ROWS, 1024), F32), pltpu.VMEM((8, 8, CH, CH), F32),
                        pltpu.SemaphoreType.DMA((14,)), pltpu.SemaphoreType.DMA((14,))],
        compiler_params=pltpu.CompilerParams(vmem_limit_bytes=32 * 1024 * 1024),
    )(*args)
    return res[0], {k: res[1 + 4 * i:5 + 4 * i] for i, k in enumerate(SMALL_NAMES)}


def _pack_shard(w_in, w_out, w1, w2, plew, wg):
    return jnp.concatenate([
        jnp.pad(w_in, ((0, 0), (0, 768 - 642))).reshape(768, 1024), w_out, w1, w2,
        plew.reshape(64, 1024), wg], axis=0)


def _unpack_shard(pk):
    r = 0
    out = []
    for rows, shape in ((768, (1024, 768)), (256, (256, 1024)), (1024, (1024, 1024)), (1024, (1024, 1024)),
                        (64, (256, 256)), (256, (256, 1024))):
        out.append(pk[r:r + rows].reshape(shape))
        r += rows
    out[0] = out[0][:, :642]
    return out


def _full_weights(gathered):
    parts = [_unpack_shard(gathered[k]) for k in range(4)]
    w_in = jnp.concatenate([p[0] for p in parts], axis=1)
    w_in = jnp.concatenate([w_in[:, :3 * AW], w_in[:, 3 * AW + NH:], w_in[:, 3 * AW:3 * AW + NH],
                            jnp.zeros((D, 128 - NH), w_in.dtype)], axis=1)
    return (w_in, jnp.concatenate([p[1] for p in parts], axis=0), jnp.concatenate([p[2] for p in parts], axis=1),
            jnp.concatenate([p[3] for p in parts], axis=0), jnp.concatenate([p[4] for p in parts], axis=1),
            jnp.concatenate([p[5] for p in parts], axis=0))


def _local_step(x, p, tgt, win_k, wout, w1, w2, plew, wg, small):
    T = x.shape[0]
    row = lambda n: small[n].reshape(1, -1)
    fbias = jnp.pad(row("f_bias"), ((0, 0), (0, 128 - NH)))
    wm = _masked_sg_w(small["sg_w"].reshape(8, CH, CH))
    wmb = wm.astype(BF16)
    wmt = jnp.swapaxes(wm, 1, 2).astype(BF16)
    bsg = jnp.repeat(small["sg_b"].reshape(8, CH).T, DH, axis=1)
    ln_g, ln_b, gsg, gatt = row("sg_ln_g"), row("sg_ln_b"), row("sg_out_g"), row("att_out_g")
    gpre, gpm, gpf, gpff, bg = row("pre_mix_g"), row("post_mix_g"), row("pre_ffn_g"), row("post_ffn_g"), row("ple_gate_b")
    gsel = (jnp.arange(AW)[:, None] // DH == jnp.arange(128)[None, :]).astype(BF16)
    hsel = (jnp.arange(AW)[:, None] // DH == jnp.arange(AW)[None, :] // DH).astype(BF16)

    expand, shrink, pieces, qconst, _, pick64 = _head_consts()
    a, qkv, flog, ccol, zuv, ysgn, q8, k8, v8 = _pre_attn_fwd(
        x, gpre, win_k, fbias, ln_g, ln_b, wmb, bsg, gsg, expand, pieces, qconst)

    nt = T // TQ
    hd = lambda t, i: t[:, i * AW:(i + 1) * AW].reshape(T, NH, DH)
    zpad = lambda n: jnp.zeros((T, NH, n), BF16)
    one = jnp.ones((T, NH, 1), BF16)
    wide = lambda parts: jnp.concatenate(parts, axis=-1).reshape(T, NH * 128)
    slabs = lambda t: jnp.swapaxes(t.reshape(nt, TQ, NH * 128), 1, 2)
    qt8 = slabs(q8)
    kt8 = slabs(wide([hd(qkv, 1), one, zpad(63)]))
    vt8 = slabs(wide([hd(qkv, 2), one, zpad(63)]))
    lanes = jnp.arange(128)
    sel = jnp.stack([((lanes[:, None] == lanes[None, :] - DH * j) & (lanes[:, None] < DH)).astype(BF16)
                     for j in (0, 1)])

    yatt, lse = _flash_fwd(qt8, k8, vt8, sel)
    y, ov, h1, c2, sact, rr = _tail_fwd1(x, yatt, ysgn, gatt, wout, gpm, gpf, w1)
    ff, h2b, de, dpre, dh2, loss_l, dbg = _tail_fwd2(sact, h1, p, tgt, w2, gpff, wg, bg, plew)
    dff, dr, do, do8, dlt, dysg, dh1, dgpff, dgpf, dgpm, dgatt = _tail_bwd(
        dh2, ff, rr, h1, ov, yatt, w2, w1, wout, gpff, gpf, gpm, gatt, hsel, expand)
    dlt4 = jnp.pad(dlt[:, ::DH].T.reshape(4, 2, T), ((0, 0), (0, 6), (0, 0)))
    dqt, dk8, dv8 = _flash_bwd(q8, qt8, k8, kt8, v8, do8, slabs(do8), lse, dlt4)
    dx, dz, dgpre, dfb, dgsg, dlng, dlnb, dws, _, dsbt = _pre_attn_bwd(
        x, dh1, jnp.swapaxes(dqt, 1, 2).reshape(T, NH * 128), dk8, dv8, flog, zuv, dysg,
        gpre, win_k, ln_g, ln_b, wmb, wmt, bsg, gsg, gsel, shrink, pick64)

    dwin_k = _matmul_tn("grad_w_in", a, dz, tn=384)
    dwout = _matmul_tn("grad_w_out", y, do)
    dw1 = _matmul_tn("grad_w_ff1", c2, dr, shards=4)
    dw2 = _matmul_tn("grad_w_ff2", sact, dff)
    dwg = _matmul_tn("grad_ple_gate_w", h2b, dpre)
    dplew = _matmul_tn("grad_ple_w", p, de, tn=256, shards=4)

    dsb = dsbt[:, :8].T
    gsmall = {"sg_w": dws, "f_bias": dfb, "sg_ln_g": dlng, "sg_ln_b": dlnb, "sg_b": dsb,
              "att_out_g": dgatt, "sg_out_g": dgsg, "pre_mix_g": dgpre, "post_mix_g": dgpm, "pre_ffn_g": dgpf,
              "post_ffn_g": dgpff, "ple_gate_b": dbg}
    return loss_l, dx, (dwin_k, dwout, dw1, dw2, dplew, dwg), gsmall


def kernel(x, p, w_in, f_bias, sg_ln_g, sg_ln_b, sg_w, sg_b, att_out_g, sg_out_g, w_out, pre_mix_g, post_mix_g, pre_ffn_g, post_ffn_g, w_ff1, w_ff2, ple_w, ple_gate_w, ple_gate_b, loss_target, m_w_in, m_f_bias, m_sg_ln_g, m_sg_ln_b, m_sg_w, m_sg_b, m_att_out_g, m_sg_out_g, m_w_out, m_pre_mix_g, m_post_mix_g, m_pre_ffn_g, m_post_ffn_g, m_w_ff1, m_w_ff2, m_ple_w, m_ple_gate_w, m_ple_gate_b, v_w_in, v_f_bias, v_sg_ln_g, v_sg_ln_b, v_sg_w, v_sg_b, v_att_out_g, v_sg_out_g, v_w_out, v_pre_mix_g, v_post_mix_g, v_pre_ffn_g, v_post_ffn_g, v_w_ff1, v_w_ff2, v_ple_w, v_ple_gate_w, v_ple_gate_b):
    c = lax.axis_index("c")
    big = lambda t: (t[0][0], t[1][0], t[2][0], t[3][0], t[4][0], t[5][0])
    w_big = big((w_in, w_out, w_ff1, w_ff2, ple_w, ple_gate_w))
    m_big = big((m_w_in, m_w_out, m_w_ff1, m_w_ff2, m_ple_w, m_ple_gate_w))
    v_big = big((v_w_in, v_w_out, v_w_ff1, v_w_ff2, v_ple_w, v_ple_gate_w))
    small = {"sg_w": sg_w, "f_bias": f_bias, "sg_ln_g": sg_ln_g, "sg_ln_b": sg_ln_b, "sg_b": sg_b,
             "att_out_g": att_out_g, "sg_out_g": sg_out_g, "pre_mix_g": pre_mix_g, "post_mix_g": post_mix_g,
             "pre_ffn_g": pre_ffn_g, "post_ffn_g": post_ffn_g, "ple_gate_b": ple_gate_b}
    m_small = {"sg_w": m_sg_w, "f_bias": m_f_bias, "sg_ln_g": m_sg_ln_g, "sg_ln_b": m_sg_ln_b, "sg_b": m_sg_b,
               "att_out_g": m_att_out_g, "sg_out_g": m_sg_out_g, "pre_mix_g": m_pre_mix_g,
               "post_mix_g": m_post_mix_g, "pre_ffn_g": m_pre_ffn_g, "post_ffn_g": m_post_ffn_g,
               "ple_gate_b": m_ple_gate_b}
    v_small = {"sg_w": v_sg_w, "f_bias": v_f_bias, "sg_ln_g": v_sg_ln_g, "sg_ln_b": v_sg_ln_b, "sg_b": v_sg_b,
               "att_out_g": v_att_out_g, "sg_out_g": v_sg_out_g, "pre_mix_g": v_pre_mix_g,
               "post_mix_g": v_post_mix_g, "pre_ffn_g": v_pre_ffn_g, "post_ffn_g": v_post_ffn_g,
               "ple_gate_b": v_ple_gate_b}

    w_pk = _pack_shard(*w_big)
    gathered = _gather_weights(w_pk.astype(BF16))
    win_k, wout, w1, w2, plew, wg = _full_weights(gathered)

    loss_l, dx, gbig, gsmall = _local_step(x[0], p[0, 0], loss_target[0], win_k, wout, w1, w2, plew, wg, small)

    dwin_k, dwout, dw1, dw2, dplew, dwg = gbig
    dwin = jnp.concatenate([dwin_k[:, :3 * AW], dwin_k[:, 5 * AW:5 * AW + NH], dwin_k[:, 3 * AW:5 * AW]], axis=1)
    dwin = jnp.pad(jnp.swapaxes(dwin.reshape(D, 4, 642), 0, 1), ((0, 0), (0, 0), (0, 768 - 642)))
    names = ("w_in", "w_out", "w_ff1", "w_ff2", "ple_w", "ple_gate_w")
    gs = [dwin, dwout.reshape(4, 256, D), dw1, dw2.reshape(4, D, D), dplew, dwg.reshape(4, 256, D)]
    padded = lambda t: (jnp.pad(t[0], ((0, 0), (0, 768 - 642))),) + tuple(t[1:])
    c1 = jnp.reshape(c, (1,)).astype(jnp.int32)
    gots = _swap_halves(gs)
    parts = _exchange_chips([_pair_sum(nm, c1, g, got) for nm, g, got in zip(names, gs, gots)])
    ghs = [_reduce_chips(nm, pt) for nm, pt in zip(names, parts)]
    got2 = _share_grad(ghs)
    big_out = [_update(nm, c1, gh, g2, w, m, v) for nm, gh, g2, w, m, v in
               zip(names, ghs, got2, padded(w_big), padded(m_big), padded(v_big))]
    big_out = [[big_out[j][i][:, :642] if j == 0 else big_out[j][i] for j in range(6)] for i in range(4)]

    view = lambda t: t.reshape(t.shape[-3:]) if t.ndim == 4 else t.reshape(t.shape[-2:])
    views = lambda d: {k: view(d[k]) for k in SMALL_NAMES}
    loss11, res_s = _small_allreduce_update(gsmall, views(small), views(m_small), views(v_small), loss_l)
    loss = loss11[0, 0]

    def small_out(i, name):
        return res_s[name][i].reshape(small[name].shape)

    order = ["w_in", "f_bias", "sg_ln_g", "sg_ln_b", "sg_w", "sg_b", "att_out_g", "sg_out_g", "w_out",
             "pre_mix_g", "post_mix_g", "pre_ffn_g", "post_ffn_g", "w_ff1", "w_ff2", "ple_w", "ple_gate_w",
             "ple_gate_b"]
    big_idx = {"w_in": 0, "w_out": 1, "w_ff1": 2, "w_ff2": 3, "ple_w": 4, "ple_gate_w": 5}
    outs = [loss, dx[None]]
    for i in range(4):
        for name in order:
            if name in big_idx:
                outs.append(big_out[i][big_idx[name]][None])
            else:
                outs.append(small_out(i, name))
    return tuple(outs)
```

```python
import math

import jax
import jax.numpy as jnp
from jax import lax
from jax.experimental import pallas as pl
from jax.experimental.pallas import tpu as pltpu

F32 = jnp.float32
BF16 = jnp.bfloat16
MESH = pl.DeviceIdType.MESH

D = 1024
DH = 64
NH = 8
AW = 512
CH = 128
DFF = 4096
ZW = 5 * AW + 128
EPS = 1e-6
NEG = -1e30
MASKED = -2e30

TM = 256
TQ = 256

LR, B1, B2, AEPS, WD, STEP = 0.001, 0.9, 0.999, 1e-08, 0.01, 10
BC1 = 1.0 - B1 ** STEP
BC2 = 1.0 - B2 ** STEP

VEC_NAMES = ("f_bias", "sg_ln_g", "sg_ln_b", "sg_b", "att_out_g", "sg_out_g", "pre_mix_g",
             "post_mix_g", "pre_ffn_g", "post_ffn_g", "ple_gate_b")


def _dot(a, b):
    return jnp.dot(a, b, preferred_element_type=F32)


def _dot_nt(a, b):
    return lax.dot_general(a, b, (((1,), (1,)), ((), ())), preferred_element_type=F32)


def _dot_tn(a, b):
    return lax.dot_general(a, b, (((0,), (0,)), ((), ())), preferred_element_type=F32)


def _split3(x):
    h = x.astype(BF16)
    r = x - h.astype(F32)
    m = r.astype(BF16)
    l = (r - m.astype(F32)).astype(BF16)
    return h, m, l


def _dot01(sel, x):
    h, m, l = _split3(x)
    return _dot(sel, h) + _dot(sel, m) + _dot(sel, l)


def _dot01_r(x, sel):
    h, m, l = _split3(x)
    return _dot(h, sel) + _dot(m, sel) + _dot(l, sel)


def _dot01_tn(x, sel):
    h, m, l = _split3(x)
    return _dot_tn(h, sel) + _dot_tn(m, sel) + _dot_tn(l, sel)


def _rs(x, n):
    return lax.rsqrt(jnp.sum(x * x, axis=-1, keepdims=True) * (1.0 / n) + EPS)


def _rms_bwd(dn, x, rs, g, n):
    w = dn * g
    dx = rs * w - x * ((rs * rs * rs) * (1.0 / n) * jnp.sum(w * x, axis=-1, keepdims=True))
    return dx, jnp.sum(dn * x * rs, axis=0, keepdims=True)


_GC = math.sqrt(2.0 / math.pi)


def _gelu(x):
    t = jnp.tanh(_GC * (x + 0.044715 * x * x * x))
    return 0.5 * x * (1.0 + t), t


def _gelu_grad(x, t):
    return 0.5 * (1.0 + t) + 0.5 * x * (1.0 - t * t) * (_GC * (1.0 + 3.0 * 0.044715 * x * x))


def _params(vmem_mb, sem=("arbitrary",)):
    return pltpu.CompilerParams(dimension_semantics=sem, vmem_limit_bytes=vmem_mb * 1024 * 1024)


def _row_call(name, body, T, tm, tiled, resident, outs, accs, scratch=(), reverse=False, vmem_mb=48):
    nt = T // tm
    n_t, n_r, n_o, n_a = len(tiled), len(resident), len(outs), len(accs)

    def kern(*refs):
        t_refs = refs[:n_t]
        r_hbm = refs[n_t:n_t + n_r]
        o_refs = refs[n_t + n_r:n_t + n_r + n_o]
        a_refs = refs[n_t + n_r + n_o:n_t + n_r + n_o + n_a]
        r_vmem = refs[n_t + n_r + n_o + n_a:n_t + 2 * n_r + n_o + n_a]
        s_refs = refs[n_t + 2 * n_r + n_o + n_a:]

        @pl.when(pl.program_id(0) == 0)
        def _():
            for h, v in zip(r_hbm, r_vmem):
                pltpu.sync_copy(h, v)
            for a in a_refs + s_refs:
                a[...] = jnp.zeros(a.shape, a.dtype)

        body(t_refs, r_vmem, o_refs, a_refs, s_refs)

    if reverse:
        idx = lambda i: (nt - 1 - i, 0)
        idx_t = lambda i: (nt - 1 - i, 0, 0)
    else:
        idx = lambda i: (i, 0)
        idx_t = lambda i: (i, 0, 0)
    arrays, in_specs = [], []
    for a in tiled:
        if isinstance(a, tuple):
            arrays.append(a[0])
            in_specs.append(pl.BlockSpec((None, a[0].shape[1], tm), idx_t))
        else:
            arrays.append(a)
            in_specs.append(pl.BlockSpec((tm, a.shape[1]), idx))
    in_specs += [pl.BlockSpec(memory_space=pl.ANY) for _ in resident]
    out_shape, out_specs = [], []
    for o in outs:
        if len(o) == 3:
            out_shape.append(jax.ShapeDtypeStruct((nt, o[0], tm), o[1]))
            out_specs.append(pl.BlockSpec((None, o[0], tm), idx_t))
        else:
            out_shape.append(jax.ShapeDtypeStruct((T, o[0]), o[1]))
            out_specs.append(pl.BlockSpec((tm, o[0]), idx))
    out_shape += [jax.ShapeDtypeStruct(s, F32) for s in accs]
    out_specs += [pl.BlockSpec(s, lambda i, n=len(s): (0,) * n) for s in accs]
    scratch_shapes = [pltpu.VMEM(r.shape, r.dtype) for r in resident]
    scratch_shapes += [pltpu.VMEM(s, F32) for s in scratch]
    return pl.pallas_call(
        kern, name=name, grid=(nt,), in_specs=in_specs, out_specs=out_specs, out_shape=out_shape,
        scratch_shapes=scratch_shapes, compiler_params=_params(vmem_mb),
    )(*arrays, *resident)


def _sg_forward(zu, zv, wm_ref, bsg, lng, lnb, mixed_ref, tm):
    gu, tu = _gelu(zu)
    vg, tv = _gelu(zv)
    mu = jnp.sum(vg, axis=-1, keepdims=True) * (1.0 / AW)
    xc = vg - mu
    rstd = lax.rsqrt(jnp.sum(xc * xc, axis=-1, keepdims=True) * (1.0 / AW) + EPS)
    xhat = xc * rstd
    vvb = (xhat * lng + lnb).astype(BF16)
    lane = lax.broadcasted_iota(jnp.int32, (CH, 128), 1)
    for c in range(tm // CH):
        for j in range(4):
            blk = vvb[c * CH:(c + 1) * CH, j * 128:(j + 1) * 128]
            m0 = _dot(wm_ref[2 * j], blk)
            m1 = _dot(wm_ref[2 * j + 1], blk)
            mixed_ref[c * CH:(c + 1) * CH, j * 128:(j + 1) * 128] = (
                jnp.where(lane < DH, m0, m1) + bsg[:, j * 128:(j + 1) * 128])
    return gu, tu, tv, xhat, rstd, vvb, mixed_ref[...]


def _head_consts():
    src = jnp.arange(AW)
    dst = (src // DH) * 128 + src % DH
    wide = jnp.arange(NH * 128)
    expand = (dst[:, None] == wide[None, :]).astype(BF16)
    heads = jnp.arange(128)
    pieces = jnp.stack([((heads[:, None] * 128 + DH + i == wide[None, :]) & (heads[:, None] < NH)).astype(BF16)
                        for i in range(3)])
    spare = wide % 128 - DH
    qconst = jnp.where((spare >= 0) & (spare < 3), -1.0, 0.0).astype(F32)[None, :]
    one64 = jnp.where(spare == 0, 1.0, 0.0).astype(F32)[None, :]
    pick64 = ((wide[:, None] == heads[None, :] * 128 + DH) & (heads[None, :] < NH)).astype(BF16)
    return expand, expand.T, pieces, qconst, one64, pick64


def _masked_sg_w(sg_w):
    r = lax.broadcasted_iota(jnp.int32, (CH, CH), 0)
    c = lax.broadcasted_iota(jnp.int32, (CH, CH), 1)
    return jnp.where((c <= r)[None], sg_w, 0.0)


def _pre_attn_fwd(x, gpre, win, fbias, lng, lnb, wm, bsg, gsg, expand, pieces, qconst):
    T = x.shape[0]
    tm = TM

    def body(t, r, o, a, s):
        (x_ref,) = t
        gpre_r, win_r, fb_r, lng_r, lnb_r, wm_r, bsg_r, gsg_r, ex_r, pc_r, qc_r = r
        a_o, qkv_o, flog_o, ccol_o, zuv_o, ysgn_o, q8_o, k8_o, v8_o = o
        carry_ref, mixed_ref = s
        xv = x_ref[...]
        av = (xv * _rs(xv, D) * gpre_r[...]).astype(BF16)
        a_o[...] = av
        z = _dot(av, win_r[...])
        zu = z[:, 3 * AW:4 * AW]
        zv = z[:, 4 * AW:5 * AW]
        zuv_o[:, 0:AW] = zu
        zuv_o[:, AW:2 * AW] = zv
        zf = z[:, 5 * AW:] + fb_r[...]
        flog_o[...] = zf
        lane = lax.broadcasted_iota(jnp.int32, (tm, 128), 1)
        logf = jnp.where(lane < NH, jnp.minimum(zf, 0.0) - jnp.log(1.0 + jnp.exp(-jnp.abs(zf))), 0.0)
        rr = lax.broadcasted_iota(jnp.int32, (tm, tm), 0)
        cc = lax.broadcasted_iota(jnp.int32, (tm, tm), 1)
        tri = (cc <= rr).astype(BF16)
        cum = _dot01(tri, logf) + carry_ref[...]
        carry_ref[...] = cum[tm - 1:tm, :]
        ccol_o[...] = cum
        ex = ex_r[...]
        q8_o[...] = (_dot((z[:, 0:AW] * (DH ** -0.5)).astype(BF16), ex) + qc_r[...]).astype(BF16)
        ch, cm, cl = _split3(cum)
        k8_o[...] = (_dot(z[:, AW:2 * AW].astype(BF16), ex) + _dot(ch, pc_r[0]) + _dot(cm, pc_r[1])
                     + _dot(cl, pc_r[2])).astype(BF16)
        v8_o[...] = _dot(z[:, 2 * AW:3 * AW].astype(BF16), ex).astype(BF16)
        qkv_o[:, 0:AW] = (z[:, 0:AW] * (DH ** -0.5)).astype(BF16)
        qkv_o[:, AW:3 * AW] = z[:, AW:3 * AW].astype(BF16)
        gu, _, _, _, _, _, mixed = _sg_forward(zu, zv, wm_r, bsg_r[...], lng_r[...], lnb_r[...], mixed_ref, tm)
        ysg = gu * mixed
        ysgn_o[...] = (ysg * _rs(ysg, AW) * gsg_r[...]).astype(BF16)

    return _row_call(
        "pre_attn_fwd", body, T, tm, [x], [gpre, win, fbias, lng, lnb, wm, bsg, gsg, expand, pieces, qconst],
        [(D, BF16), (3 * AW, BF16), (128, F32), (128, F32), (2 * AW, F32), (AW, BF16), (NH * 128, BF16),
         (NH * 128, BF16), (NH * 128, BF16)], [],
        scratch=[(1, 128), (tm, AW)], vmem_mb=48)


def _flash_fwd(qt8, k8, vt8, sel):
    T = k8.shape[0]
    nq = T // TQ

    def body(qt_ref, k_ref, vt_ref, sel_ref, o_ref, l_ref, u_scr, p_scr):
        qi = pl.program_id(1)
        qts = (qt_ref[0:128, :], qt_ref[128:256, :])
        dmat = (lax.broadcasted_iota(jnp.int32, (TQ, TQ), 0) - lax.broadcasted_iota(jnp.int32, (TQ, TQ), 1))
        u_scr[1] = jnp.full((2, TQ, TQ), MASKED, F32)
        p_scr[...] = jnp.zeros(p_scr.shape, BF16)

        def sub(t, carry, sc, sb, masked):
            blk_c = jnp.clip(t - 2, 0, qi)
            off_a = pl.multiple_of(jnp.minimum(t, qi) * TQ, TQ)
            new = []
            for j in (0, 1):
                m, al, acc = carry[j]
                acc = al * acc + _dot(vt_ref[blk_c, j * 128:(j + 1) * 128, :], p_scr[sc, j])
                m_new = jnp.maximum(m, jnp.max(u_scr[sb, j], axis=0, keepdims=True))
                p_scr[sb, j] = jnp.exp(u_scr[sb, j] - m_new).astype(BF16)
                u = _dot(k_ref[pl.ds(off_a, TQ), j * 128:(j + 1) * 128], qts[j])
                u_scr[sc, j] = jnp.where(dmat <= (qi - t) * TQ, u, MASKED) if masked else u
                new.append((m_new, jnp.exp(m - m_new), acc))
            return tuple(new)

        def pair(t2, carry, masked):
            return sub(2 * t2 + 1, sub(2 * t2, carry, 0, 1, masked), 1, 0, masked)

        init = tuple((jnp.full((1, TQ), NEG, F32), jnp.ones((1, TQ), F32), jnp.zeros((128, TQ), F32))
                     for _ in (0, 1))
        carry = lax.fori_loop(0, qi // 2, lambda t2, cr: pair(t2, cr, False), init)
        (m0, _, a0), (m1, _, a1) = pair(qi // 2 + 1, pair(qi // 2, carry, True), True)
        l0 = a0[DH:DH + 1, :]
        l1 = a1[DH:DH + 1, :]
        o_ref[...] = _dot01_tn(a0 * (1.0 / l0), sel_ref[0]) + _dot01_tn(a1 * (1.0 / l1), sel_ref[1])
        l_ref[0:1, :] = m0 + jnp.log(l0)
        l_ref[1:2, :] = m1 + jnp.log(l1)
        l_ref[2:8, :] = jnp.zeros((6, TQ), F32)

    return pl.pallas_call(
        body, name="flash_fwd", grid=(4, nq),
        in_specs=[pl.BlockSpec((None, 256, TQ), lambda h, i: (i, h, 0)),
                  pl.BlockSpec((T, 256), lambda h, i: (0, h)),
                  pl.BlockSpec((nq, 256, TQ), lambda h, i: (0, h, 0)),
                  pl.BlockSpec((2, 128, 128), lambda h, i: (0, 0, 0))],
        out_specs=[pl.BlockSpec((TQ, 128), lambda h, i: (i, h)),
                   pl.BlockSpec((None, 8, TQ), lambda h, i: (h, 0, i))],
        out_shape=[jax.ShapeDtypeStruct((T, AW), F32), jax.ShapeDtypeStruct((4, 8, T), F32)],
        scratch_shapes=[pltpu.VMEM((2, 2, TQ, TQ), F32), pltpu.VMEM((2, 2, TQ, TQ), BF16)],
        compiler_params=_params(40, ("arbitrary", "arbitrary")),
    )(qt8, k8, vt8, sel)


def _flash_bwd(q8, qt8, k8, kt8, v8, do8, dot8, lse, dlt):
    T = q8.shape[0]
    nk = T // TQ

    def body(q_ref, qt_ref, k_ref, kt_ref, v_ref, do_ref, dot_ref, l_ref, d_ref, dqt_ref, dk_ref, dv_ref,
             u_scr, dp_scr, p_scr, ds_scr):
        kb = pl.program_id(1)
        n = nk - kb

        @pl.when(kb == 0)
        def _():
            dqt_ref[...] = jnp.zeros(dqt_ref.shape, F32)

        dk_ref[...] = jnp.zeros(dk_ref.shape, F32)
        dv_ref[...] = jnp.zeros(dv_ref.shape, F32)
        u_scr[1] = jnp.full((2, TQ, TQ), MASKED, F32)
        dp_scr[1] = jnp.zeros((2, TQ, TQ), F32)
        p_scr[...] = jnp.zeros(p_scr.shape, BF16)
        ds_scr[...] = jnp.zeros(ds_scr.shape, BF16)
        dmat = (lax.broadcasted_iota(jnp.int32, (TQ, TQ), 0) - lax.broadcasted_iota(jnp.int32, (TQ, TQ), 1))
        ks = (k_ref[:, 0:128], k_ref[:, 128:256])
        vs = (v_ref[:, 0:128], v_ref[:, 128:256])
        kts = (kt_ref[0:128, :], kt_ref[128:256, :])

        def sub(t, sc, sb):
            blk_a = kb + jnp.minimum(t, n - 1)
            blk_c = kb + jnp.clip(t - 2, 0, n - 1)
            off_b = pl.multiple_of((kb + jnp.clip(t - 1, 0, n - 1)) * TQ, TQ)
            off_c = pl.multiple_of(blk_c * TQ, TQ)
            lim = jnp.where(t < n, t * TQ, -TQ)
            for j in (0, 1):
                hl = slice(j * 128, (j + 1) * 128)
                dqt_ref[blk_c, hl, :] += _dot(kts[j], ds_scr[sc, j])
                dk_ref[:, hl] += _dot(ds_scr[sc, j], q_ref[pl.ds(off_c, TQ), hl])
                dv_ref[:, hl] += _dot(p_scr[sc, j], do_ref[pl.ds(off_c, TQ), hl])
                p = jnp.exp(u_scr[sb, j] - l_ref[j:j + 1, pl.ds(off_b, TQ)])
                p_scr[sb, j] = p.astype(BF16)
                ds_scr[sb, j] = (p * (dp_scr[sb, j] - d_ref[j:j + 1, pl.ds(off_b, TQ)])).astype(BF16)
                u_scr[sc, j] = jnp.where(dmat <= lim, _dot(ks[j], qt_ref[blk_a, hl, :]), MASKED)
                dp_scr[sc, j] = _dot(vs[j], dot_ref[blk_a, hl, :])

        def it(t2, carry):
            sub(2 * t2, 0, 1)
            sub(2 * t2 + 1, 1, 0)
            return carry

        lax.fori_loop(0, (n + 3) // 2, it, 0)

    return pl.pallas_call(
        body, name="flash_bwd", grid=(4, nk),
        in_specs=[pl.BlockSpec((T, 256), lambda h, i: (0, h)),
                  pl.BlockSpec((nk, 256, TQ), lambda h, i: (0, h, 0)),
                  pl.BlockSpec((TQ, 256), lambda h, i: (i, h)),
                  pl.BlockSpec((None, 256, TQ), lambda h, i: (i, h, 0)),
                  pl.BlockSpec((TQ, 256), lambda h, i: (i, h)),
                  pl.BlockSpec((T, 256), lambda h, i: (0, h)),
                  pl.BlockSpec((nk, 256, TQ), lambda h, i: (0, h, 0)),
                  pl.BlockSpec((None, 8, T), lambda h, i: (h, 0, 0)),
                  pl.BlockSpec((None, 8, T), lambda h, i: (h, 0, 0))],
        out_specs=[pl.BlockSpec((nk, 256, TQ), lambda h, i: (0, h, 0)),
                   pl.BlockSpec((TQ, 256), lambda h, i: (i, h)),
                   pl.BlockSpec((TQ, 256), lambda h, i: (i, h))],
        out_shape=[jax.ShapeDtypeStruct((nk, NH * 128, TQ), F32), jax.ShapeDtypeStruct((T, NH * 128), F32),
                   jax.ShapeDtypeStruct((T, NH * 128), F32)],
        scratch_shapes=[pltpu.VMEM((2, 2, TQ, TQ), F32), pltpu.VMEM((2, 2, TQ, TQ), F32),
                        pltpu.VMEM((2, 2, TQ, TQ), BF16), pltpu.VMEM((2, 2, TQ, TQ), BF16)],
        compiler_params=_params(56, ("arbitrary", "arbitrary")),
    )(q8, qt8, k8, kt8, v8, do8, dot8, lse, dlt)


def _tail_fwd1(x, yatt, ysgn, gatt, wout, gpm, gpf, w1):
    T = x.shape[0]

    def body(t, r, o, a, s):
        x_ref, ya_ref, ys_ref = t
        gatt_r, wout_r, gpm_r, gpf_r, w1_r = r
        y_o, o_o, h1_o, c2_o, s_o, rr_o = o
        ya = ya_ref[...]
        yan = (ya * _rs(ya, AW) * gatt_r[...]).astype(BF16)
        y_o[:, 0:AW] = yan
        y_o[:, AW:] = ys_ref[...]
        ov = _dot(yan, wout_r[0:AW, :]) + _dot(ys_ref[...], wout_r[AW:, :])
        o_o[...] = ov
        h1 = x_ref[...] + ov * _rs(ov, D) * gpm_r[...]
        h1_o[...] = h1
        c2 = (h1 * _rs(h1, D) * gpf_r[...]).astype(BF16)
        c2_o[...] = c2
        rr = jnp.maximum(_dot(c2, w1_r[...]), 0.0)
        rr_o[...] = rr.astype(BF16)
        s_o[...] = (rr * rr).astype(BF16)

    return _row_call(
        "tail_fwd1", body, T, TM, [x, yatt, ysgn], [gatt, wout, gpm, gpf, w1],
        [(D, BF16), (D, F32), (D, F32), (D, BF16), (DFF, BF16), (DFF, BF16)], [], vmem_mb=48)


def _tail_fwd2(sact, h1, p, tgt, w2, gpff, wg, bg, wpe):
    T = h1.shape[0]

    def body(t, r, o, a, s):
        s_ref, h1_ref, p_ref, t_ref = t
        w2_r, gpff_r, wg_r, bg_r, wpe_r = r
        ff_o, h2b_o, de_o, dpre_o, dh2_o = o
        loss_a, dbg_a = a
        ff = _dot(s_ref[...], w2_r[...])
        ff_o[...] = ff
        h2 = h1_ref[...] + ff * _rs(ff, D) * gpff_r[...]
        h2b = h2.astype(BF16)
        h2b_o[...] = h2b
        gate = 1.0 / (1.0 + jnp.exp(-(_dot(h2b, wg_r[...]) + bg_r[...])))
        e = _dot(p_ref[...].astype(BF16), wpe_r[...])
        diff = h2 + gate * e - t_ref[...]
        loss_a[...] += jnp.sum(diff * diff, axis=0, keepdims=True)
        dh3 = diff * (1.0 / D)
        de_o[...] = (dh3 * gate).astype(BF16)
        dpre = dh3 * e * gate * (1.0 - gate)
        dbg_a[...] += jnp.sum(dpre, axis=0, keepdims=True)
        dpb = dpre.astype(BF16)
        dpre_o[...] = dpb
        dh2_o[...] = dh3 + _dot_nt(dpb, wg_r[...])

    return _row_call(
        "tail_fwd2", body, T, TM, [sact, h1, p, tgt], [w2, gpff, wg, bg, wpe],
        [(D, F32), (D, BF16), (D, BF16), (D, BF16), (D, F32)], [(1, D), (1, D)], vmem_mb=48)


def _tail_bwd(dh2, ff, rr, h1, ov, yatt, w2, w1, wout, gpff, gpf, gpm, gatt, hsel, expand):
    T = dh2.shape[0]

    def body(t, r, o, a, s):
        dh2_ref, ff_ref, rr_ref, h1_ref, o_ref, ya_ref = t
        w2_r, w1_r, wout_r, gpff_r, gpf_r, gpm_r, gatt_r, hsel_r, ex_r = r
        dff_o, dr_o, do_o, do8_o, dlt_o, dysg_o, dh1_o = o
        dgpff_a, dgpf_a, dgpm_a, dgatt_a = a
        dh2v = dh2_ref[...]
        ffv = ff_ref[...]
        dff, dg = _rms_bwd(dh2v, ffv, _rs(ffv, D), gpff_r[...], D)
        dgpff_a[...] += dg
        dffb = dff.astype(BF16)
        dff_o[...] = dffb
        drb = (_dot_nt(dffb, w2_r[...]) * (2.0 * rr_ref[...].astype(F32))).astype(BF16)
        dr_o[...] = drb
        dc2 = _dot_nt(drb, w1_r[...])
        h1v = h1_ref[...]
        d1, dg = _rms_bwd(dc2, h1v, _rs(h1v, D), gpf_r[...], D)
        dgpf_a[...] += dg
        dh1 = dh2v + d1
        dh1_o[...] = dh1
        ovv = o_ref[...]
        dov, dg = _rms_bwd(dh1, ovv, _rs(ovv, D), gpm_r[...], D)
        dgpm_a[...] += dg
        dob = dov.astype(BF16)
        do_o[...] = dob
        dysg_o[...] = _dot_nt(dob, wout_r[AW:, :])
        dyan = _dot_nt(dob, wout_r[0:AW, :])
        ya = ya_ref[...]
        dya, dg = _rms_bwd(dyan, ya, _rs(ya, AW), gatt_r[...], AW)
        dgatt_a[...] += dg
        do8_o[...] = _dot(dya.astype(BF16), ex_r[...]).astype(BF16)
        dlt_o[...] = _dot01_r(dya * ya, hsel_r[...])

    return _row_call(
        "tail_bwd", body, T, TM, [dh2, ff, rr, h1, ov, yatt],
        [w2, w1, wout, gpff, gpf, gpm, gatt, hsel, expand],
        [(D, BF16), (DFF, BF16), (D, BF16), (NH * 128, BF16), (AW, F32), (AW, F32), (D, F32)],
        [(1, D), (1, D), (1, D), (1, AW)], vmem_mb=56)


def _pre_attn_bwd(x, dh1, dq8, dk8, dv8, flog, zuv, dysg, gpre, win, lng, lnb, wm, wmt, bsg, gsg, gsel, shrink, pick64):
    T = x.shape[0]
    tm = TM

    def body(t, r, o, a, s):
        x_ref, dh1_ref, dq_ref, dk_ref, dv_ref, fl_ref, zuv_ref, dys_ref = t
        gpre_r, win_r, lng_r, lnb_r, wm_r, wmt_r, bsg_r, gsg_r, gsel_r, sh_r, p64_r = r
        dx_o, dz_o = o
        dgpre_a, dfb_a, dgsg_a, dlng_a, dlnb_a, dws_a, dbs_a, dsb_a = a
        carry_ref, mixed_ref, dvv_ref = s
        dq8v = dq_ref[...]
        dk8v = dk_ref[...]
        dcv = _dot01_r(dq8v, p64_r[...]) + _dot01_r(dk8v, p64_r[...])
        rr = lax.broadcasted_iota(jnp.int32, (tm, tm), 0)
        cc = lax.broadcasted_iota(jnp.int32, (tm, tm), 1)
        triu = (cc >= rr).astype(BF16)
        dlogf = _dot01(triu, dcv) + carry_ref[...]
        carry_ref[...] = dlogf[0:1, :]
        dzf = dlogf * (1.0 / (1.0 + jnp.exp(fl_ref[...])))
        dfb_a[...] += jnp.sum(dzf, axis=0, keepdims=True)
        dz_o[:, 5 * AW:] = dzf.astype(BF16)
        zu = zuv_ref[:, 0:AW]
        zv = zuv_ref[:, AW:]
        gu, tu, tv, xhat, rstd, vvb, mixed = _sg_forward(
            zu, zv, wm_r, bsg_r[...], lng_r[...], lnb_r[...], mixed_ref, tm)
        ysg = gu * mixed
        dysg_n = dys_ref[...]
        dys, dg = _rms_bwd(dysg_n, ysg, _rs(ysg, AW), gsg_r[...], AW)
        dgsg_a[...] += dg
        dgu = dys * mixed
        dmix = dys * gu
        dmb = dmix.astype(BF16)
        lane = lax.broadcasted_iota(jnp.int32, (CH, 128), 1)
        lo = lane < DH
        for c in range(tm // CH):
            rows = slice(c * CH, (c + 1) * CH)
            dbs_a[...] += dmix[rows, :]
            for j in range(4):
                cols = slice(j * 128, (j + 1) * 128)
                dmblk = dmb[rows, cols]
                vblk = vvb[rows, cols]
                d0 = _dot(wmt_r[2 * j], dmblk)
                d1 = _dot(wmt_r[2 * j + 1], dmblk)
                dvv_ref[rows, cols] = jnp.where(lo, d0, d1)
                dws_a[2 * j] += _dot_nt(jnp.where(lo, dmblk, jnp.zeros_like(dmblk)), vblk)
                dws_a[2 * j + 1] += _dot_nt(jnp.where(lo, jnp.zeros_like(dmblk), dmblk), vblk)
        dvv = dvv_ref[...]
        dlng_a[...] += jnp.sum(dvv * xhat, axis=0, keepdims=True)
        dlnb_a[...] += jnp.sum(dvv, axis=0, keepdims=True)
        dxh = dvv * lng_r[...]
        dvg = rstd * (dxh - jnp.sum(dxh, axis=-1, keepdims=True) * (1.0 / AW)
                      - xhat * (jnp.sum(dxh * xhat, axis=-1, keepdims=True) * (1.0 / AW)))
        dz_o[:, 3 * AW:4 * AW] = (dgu * _gelu_grad(zu, tu)).astype(BF16)
        dz_o[:, 4 * AW:5 * AW] = (dvg * _gelu_grad(zv, tv)).astype(BF16)
        dz_o[:, 0:AW] = _dot((dq8v * (DH ** -0.5)).astype(BF16), sh_r[...]).astype(BF16)
        dz_o[:, AW:2 * AW] = _dot(dk8v.astype(BF16), sh_r[...]).astype(BF16)
        dz_o[:, 2 * AW:3 * AW] = _dot(dv_ref[...].astype(BF16), sh_r[...]).astype(BF16)
        da = _dot_nt(dz_o[...], win_r[...])
        xv = x_ref[...]
        dxa, dg = _rms_bwd(da, xv, _rs(xv, D), gpre_r[...], D)
        dgpre_a[...] += dg
        dx_o[...] = dh1_ref[...] + dxa

        @pl.when(pl.program_id(0) == T // tm - 1)
        def _():
            dsb_a[...] = _dot01_r(dbs_a[...], gsel_r[...])

    outs = _row_call(
        "pre_attn_bwd", body, T, tm, [x, dh1, dq8, dk8, dv8, flog, zuv, dysg],
        [gpre, win, lng, lnb, wm, wmt, bsg, gsg, gsel, shrink, pick64],
        [(D, F32), (ZW, BF16)],
        [(1, D), (1, 128), (1, AW), (1, AW), (1, AW), (8, CH, CH), (CH, AW), (CH, 128)],
        scratch=[(1, 128), (tm, AW), (tm, AW)], reverse=True, vmem_mb=48)
    return outs


def _matmul_tn(name, a, b, tn=512, tt=2048, shards=1):
    T, K = a.shape
    N = b.shape[1]
    tk = min(K, 1024)
    tn = min(tn, N // shards)
    tt = min(tt, T)
    nj = N // shards // tn

    def body(a_ref, b_ref, o_ref):
        @pl.when(pl.program_id(2) == 0)
        def _():
            o_ref[...] = jnp.zeros(o_ref.shape, F32)

        o_ref[...] += _dot_tn(a_ref[...].astype(BF16), b_ref[...].astype(BF16))

    if shards == 1:
        out_shape = jax.ShapeDtypeStruct((K, N), F32)
        out_spec = pl.BlockSpec((tk, tn), lambda i, j, t: (i, j))
    else:
        out_shape = jax.ShapeDtypeStruct((shards, K, N // shards), F32)
        out_spec = pl.BlockSpec((None, tk, tn), lambda i, j, t: (j // nj, i, j % nj))
    return pl.pallas_call(
        body, name=name, grid=(K // tk, N // tn, T // tt),
        in_specs=[pl.BlockSpec((tt, tk), lambda i, j, t: (t, i)),
                  pl.BlockSpec((tt, tn), lambda i, j, t: (t, j))],
        out_specs=out_spec, out_shape=out_shape,
        compiler_params=_params(40, ("arbitrary", "arbitrary", "arbitrary")),
    )(a, b)


def _me():
    return lax.axis_index("x"), lax.axis_index("y"), lax.axis_index("c")


HBM_SPEC = pl.BlockSpec(memory_space=pltpu.HBM)


def _gather_weights(mine):
    half = mine.shape[0] // 2

    def body(mine_ref, out_ref, ici_send, ici_recv, d2d_send, d2d_recv, local_sem):
        x, y, c = _me()
        k_me = 2 * x + y
        chips = [(1 - x, y), (x, 1 - y), (1 - x, 1 - y)]
        my_rows = pl.ds(pl.multiple_of(c * half, 16), half)
        sib_rows = pl.ds(pl.multiple_of((1 - c) * half, 16), half)

        def over_ici(j, k, to):
            src = mine_ref.at[my_rows] if k is None else out_ref.at[k, my_rows]
            return pltpu.make_async_remote_copy(
                src_ref=src, dst_ref=out_ref.at[k_me if k is None else k, my_rows], send_sem=ici_send.at[j],
                recv_sem=ici_recv.at[j], device_id=to, device_id_type=MESH)

        def over_d2d(j, k, rows):
            return pltpu.make_async_remote_copy(
                src_ref=out_ref.at[k, rows], dst_ref=out_ref.at[k, rows], send_sem=d2d_send.at[j],
                recv_sem=d2d_recv.at[j], device_id=(x, y, 1 - c), device_id_type=MESH)

        own = pltpu.make_async_copy(mine_ref, out_ref.at[k_me], local_sem)
        own.start()
        first = [over_ici(j, None, (cx, cy, c)) for j, (cx, cy) in enumerate(chips)]
        for cp in first:
            cp.start()
        passed = [over_d2d(j, 2 * cx + cy, my_rows) for j, (cx, cy) in enumerate(chips)]
        for j, (cx, cy) in enumerate(chips):
            over_ici(j, 2 * cx + cy, (cx, cy, c)).wait_recv()
            passed[j].start()
        for j, (cx, cy) in enumerate(chips):
            over_d2d(j, 2 * cx + cy, sib_rows).wait_recv()
        for cp in first + passed:
            cp.wait_send()
        own.wait()

    return pl.pallas_call(
        body, name="gather_weights", in_specs=[HBM_SPEC], out_specs=HBM_SPEC,
        out_shape=jax.ShapeDtypeStruct((4,) + mine.shape, mine.dtype),
        scratch_shapes=[pltpu.SemaphoreType.DMA((3,)), pltpu.SemaphoreType.DMA((3,)), pltpu.SemaphoreType.DMA((3,)),
                        pltpu.SemaphoreType.DMA((3,)), pltpu.SemaphoreType.DMA],
    )(mine)


SEM_SPEC = pl.BlockSpec(memory_space=pltpu.SEMAPHORE)
EFFECT = pltpu.SideEffectType.DATAFLOW_SIDE_EFFECTING


def _gather_late_start(mine):
    def body(mine_ref, land_ref, send_sems, recv_sems, mine_thru, land_thru, token, local_sem):
        x, y, c = _me()
        k_me = 2 * x + y
        own = pltpu.make_async_copy(mine_ref, land_ref.at[k_me], local_sem)
        own.start()
        for j, (cx, cy) in enumerate([(1 - x, y), (x, 1 - y), (1 - x, 1 - y)]):
            pltpu.make_async_remote_copy(
                src_ref=mine_ref, dst_ref=land_ref.at[k_me], send_sem=send_sems.at[j], recv_sem=recv_sems.at[j],
                device_id=(cx, cy, c), device_id_type=MESH).start()
        own.wait()
        token[...] = jnp.zeros(token.shape, F32)

    land = lax.empty((4,) + mine.shape, mine.dtype)
    return pl.pallas_call(
        body, name="gather_late_start",
        out_shape=(pltpu.SemaphoreType.DMA((3,)), pltpu.SemaphoreType.DMA((3,)), pltpu.HBM(mine.shape, mine.dtype),
                   pltpu.HBM(land.shape, land.dtype), jax.ShapeDtypeStruct((8, 128), F32)),
        in_specs=(HBM_SPEC, HBM_SPEC),
        out_specs=(SEM_SPEC, SEM_SPEC, HBM_SPEC, HBM_SPEC, pl.BlockSpec(memory_space=pltpu.VMEM)),
        input_output_aliases={0: 2, 1: 3}, scratch_shapes=[pltpu.SemaphoreType.DMA],
        compiler_params=pltpu.CompilerParams(has_side_effects=EFFECT),
    )(pltpu.with_memory_space_constraint(mine, pltpu.HBM), pltpu.with_memory_space_constraint(land, pltpu.HBM))


def _gather_late_wait(send_sems, recv_sems, mine_thru, land_thru, after):
    def body(mine_ref, land_ref, send_sems, recv_sems, after_ref, mine_dead, got_ref):
        x, y, c = _me()
        for j, (cx, cy) in enumerate([(1 - x, y), (x, 1 - y), (1 - x, 1 - y)]):
            cp = pltpu.make_async_remote_copy(
                src_ref=mine_ref, dst_ref=land_ref.at[2 * cx + cy], send_sem=send_sems.at[j],
                recv_sem=recv_sems.at[j], device_id=(cx, cy, c), device_id_type=MESH)
            cp.wait_send()
            cp.wait_recv()

    return pl.pallas_call(
        body, name="gather_late_wait",
        out_shape=(pltpu.HBM(mine_thru.shape, mine_thru.dtype), pltpu.HBM(land_thru.shape, land_thru.dtype)),
        in_specs=(HBM_SPEC, HBM_SPEC, SEM_SPEC, SEM_SPEC, pl.BlockSpec(memory_space=pl.ANY)),
        out_specs=(HBM_SPEC, HBM_SPEC), input_output_aliases={0: 0, 1: 1},
        compiler_params=pltpu.CompilerParams(has_side_effects=EFFECT),
    )(mine_thru, land_thru, send_sems, recv_sems, after)[1]


def _swap_halves(gs):
    n = len(gs)

    def body(*refs):
        g_refs, got_refs, send_sems, recv_sems = refs[:n], refs[n:2 * n], refs[2 * n], refs[2 * n + 1]
        x, y, c = _me()
        cps = []
        for i, (g_ref, got_ref) in enumerate(zip(g_refs, got_refs)):
            half = g_ref.shape[1] // 2
            theirs = pl.multiple_of((1 - c) * half, 16)
            cps.append(pltpu.make_async_remote_copy(
                src_ref=g_ref.at[:, pl.ds(theirs, half), :], dst_ref=got_ref, send_sem=send_sems.at[i],
                recv_sem=recv_sems.at[i], device_id=(x, y, 1 - c), device_id_type=MESH))
        for cp in cps:
            cp.start()
        for cp in cps:
            cp.wait()

    return pl.pallas_call(
        body, name="swap_halves", in_specs=[HBM_SPEC] * n, out_specs=[HBM_SPEC] * n,
        out_shape=[jax.ShapeDtypeStruct((4, g.shape[1] // 2, g.shape[2]), F32) for g in gs],
        scratch_shapes=[pltpu.SemaphoreType.DMA((n,)), pltpu.SemaphoreType.DMA((n,))],
    )(*gs)


def _pair_sum(name, c1, g, got):
    half, cols = got.shape[1], got.shape[2]

    def body(c_ref, a_ref, b_ref, o_ref):
        o_ref[...] = (a_ref[...] + b_ref[...]).astype(BF16)

    return pl.pallas_call(
        body, name="pair_sum_" + name,
        grid_spec=pltpu.PrefetchScalarGridSpec(
            num_scalar_prefetch=1, grid=(4,),
            in_specs=[pl.BlockSpec((1, half, cols), lambda k, c_ref: (k, c_ref[0], 0)),
                      pl.BlockSpec((1, half, cols), lambda k, c_ref: (k, 0, 0))],
            out_specs=pl.BlockSpec((1, half, cols), lambda k, c_ref: (k, 0, 0))),
        out_shape=jax.ShapeDtypeStruct(got.shape, BF16), compiler_params=_params(32),
    )(c1, g, got)


def _exchange_chips(pss):
    n = len(pss)

    def body(*refs):
        ps_refs, out_refs = refs[:n], refs[n:2 * n]
        send_sems, recv_sems, local_sems = refs[2 * n:]
        x, y, c = _me()
        k_me = 2 * x + y
        chips = [(1 - x, y), (x, 1 - y), (1 - x, 1 - y)]
        owns, sends = [], []
        for i, (ps_ref, out_ref) in enumerate(zip(ps_refs, out_refs)):
            owns.append(pltpu.make_async_copy(ps_ref.at[k_me], out_ref.at[k_me], local_sems.at[i]))
            for j, (cx, cy) in enumerate(chips):
                sends.append(pltpu.make_async_remote_copy(
                    src_ref=ps_ref.at[2 * cx + cy], dst_ref=out_ref.at[k_me], send_sem=send_sems.at[3 * i + j],
                    recv_sem=recv_sems.at[3 * i + j], device_id=(cx, cy, c), device_id_type=MESH))
        for cp in owns + sends:
            cp.start()
        for i, (ps_ref, out_ref) in enumerate(zip(ps_refs, out_refs)):
            for j, (cx, cy) in enumerate(chips):
                pltpu.make_async_remote_copy(
                    src_ref=ps_ref.at[k_me], dst_ref=out_ref.at[2 * cx + cy], send_sem=send_sems.at[3 * i + j],
                    recv_sem=recv_sems.at[3 * i + j], device_id=(cx, cy, c), device_id_type=MESH).wait_recv()
        for cp in sends:
            cp.wait_send()
        for cp in owns:
            cp.wait()

    return pl.pallas_call(
        body, name="exchange_chips", in_specs=[HBM_SPEC] * n, out_specs=[HBM_SPEC] * n,
        out_shape=[jax.ShapeDtypeStruct(ps.shape, ps.dtype) for ps in pss],
        scratch_shapes=[pltpu.SemaphoreType.DMA((3 * n,)), pltpu.SemaphoreType.DMA((3 * n,)),
                        pltpu.SemaphoreType.DMA((n,))],
    )(*pss)


def _adamw(w, g, m, v):
    m = B1 * m + (1.0 - B1) * g
    v = B2 * v + (1.0 - B2) * (g * g)
    delta = -LR * ((m / BC1) / (jnp.sqrt(v / BC2) + AEPS) + WD * w)
    return delta, m, v


def _reduce_chips(name, parts):
    half, cols = parts.shape[1], parts.shape[2]

    def body(p_ref, o_ref):
        f = lambda k: p_ref[k].astype(F32)
        o_ref[...] = ((f(0) + f(1)) + f(2)) + f(3)

    return pl.pallas_call(
        body, name="reduce_chips_" + name, grid=(1,),
        in_specs=[pl.BlockSpec((4, half, cols), lambda i: (0, 0, 0))],
        out_specs=pl.BlockSpec((half, cols), lambda i: (0, 0)),
        out_shape=jax.ShapeDtypeStruct((half, cols), F32), compiler_params=_params(32),
    )(parts)


def _share_grad(ghs):
    n = len(ghs)

    def body(*refs):
        g_refs, got_refs, send_sems, recv_sems = refs[:n], refs[n:2 * n], refs[2 * n], refs[2 * n + 1]
        x, y, c = _me()
        cps = [pltpu.make_async_remote_copy(
            src_ref=g_ref, dst_ref=got_ref, send_sem=send_sems.at[i], recv_sem=recv_sems.at[i],
            device_id=(x, y, 1 - c), device_id_type=MESH) for i, (g_ref, got_ref) in enumerate(zip(g_refs, got_refs))]
        for cp in cps:
            cp.start()
        for cp in cps:
            cp.wait()

    return pl.pallas_call(
        body, name="share_grad", in_specs=[HBM_SPEC] * n, out_specs=[HBM_SPEC] * n,
        out_shape=[jax.ShapeDtypeStruct(g.shape, F32) for g in ghs],
        scratch_shapes=[pltpu.SemaphoreType.DMA((n,)), pltpu.SemaphoreType.DMA((n,))],
    )(*ghs)


def _update(name, c1, gh, got, w, m, v):
    half, cols = gh.shape

    def body(c_ref, gh_ref, got_ref, w_ref, m_ref, v_ref, g_o, d_o, m_o, v_o):
        g = jnp.where(pl.program_id(0) == c_ref[0], gh_ref[...], got_ref[...])
        delta, mn, vn = _adamw(w_ref[...], g, m_ref[...], v_ref[...])
        g_o[...] = g
        d_o[...] = delta
        m_o[...] = mn
        v_o[...] = vn

    same = pl.BlockSpec((half, cols), lambda h, c_ref: (0, 0))
    rows = pl.BlockSpec((half, cols), lambda h, c_ref: (h, 0))
    return pl.pallas_call(
        body, name="update_" + name,
        grid_spec=pltpu.PrefetchScalarGridSpec(
            num_scalar_prefetch=1, grid=(2,), in_specs=[same, same, rows, rows, rows],
            out_specs=[rows, rows, rows, rows]),
        out_shape=[jax.ShapeDtypeStruct(w.shape, F32)] * 4, compiler_params=_params(40),
    )(c1, gh, got, w, m, v)


SMALL_NAMES = ("sg_w",) + VEC_NAMES
VEC_ROWS = 24
VEC_ROW = {"f_bias": 0, "sg_ln_g": 1, "sg_ln_b": 2, "att_out_g": 3, "sg_out_g": 4, "pre_mix_g": 5,
           "post_mix_g": 6, "pre_ffn_g": 7, "sg_b": 8, "post_ffn_g": 16, "ple_gate_b": 17}
LOSS_VEC_ROW = 18


def _small_allreduce_update(g, w, m, v, loss_l):
    n = len(SMALL_NAMES)

    def body(*refs):
        g_r = dict(zip(SMALL_NAMES, refs[0:n]))
        w_r = dict(zip(SMALL_NAMES, refs[n:2 * n]))
        m_r = dict(zip(SMALL_NAMES, refs[2 * n:3 * n]))
        v_r = dict(zip(SMALL_NAMES, refs[3 * n:4 * n]))
        loss_r = refs[4 * n]
        loss_o = refs[4 * n + 1]
        outs = refs[4 * n + 2:8 * n + 2]
        bufv, bufw, send_sems, recv_sems = refs[8 * n + 2:]
        x, y, c = _me()
        me = 4 * x + 2 * y + c
        bufv[me] = jnp.zeros((VEC_ROWS, 1024), F32)
        for name in VEC_NAMES:
            val = g_r[name][...]
            bufv[me, pl.ds(VEC_ROW[name], val.shape[0]), pl.ds(0, val.shape[1])] = val
        bufv[me, pl.ds(LOSS_VEC_ROW, 1), :] = loss_r[...] * (0.5 / D)
        rr = lax.broadcasted_iota(jnp.int32, (CH, CH), 0)
        cc = lax.broadcasted_iota(jnp.int32, (CH, CH), 1)
        bufw[me] = jnp.where((cc <= rr)[None], g_r["sg_w"][...], 0.0)

        rels = [(rx, ry, rc) for rx in (0, 1) for ry in (0, 1) for rc in (0, 1)][1:]

        def peer(r):
            return ((x + r[0]) % 2, (y + r[1]) % 2, (c + r[2]) % 2)

        def copies(j, slot, to):
            return [pltpu.make_async_remote_copy(
                src_ref=buf.at[slot], dst_ref=buf.at[slot], send_sem=send_sems.at[2 * j + i],
                recv_sem=recv_sems.at[2 * j + i], device_id=to, device_id_type=MESH)
                for i, buf in enumerate((bufv, bufw))]

        sends = [cp for j, r in enumerate(rels) for cp in copies(j, me, peer(r))]
        for cp in sends:
            cp.start()
        for j, r in enumerate(rels):
            px, py, pc = peer(r)
            for cp in copies(j, 4 * px + 2 * py + pc, peer(r)):
                cp.wait_recv()
        for cp in sends:
            cp.wait_send()

        tot_v = bufv[0]
        tot_w = bufw[0]
        for d in range(1, 8):
            tot_v = tot_v + bufv[d]
            tot_w = tot_w + bufw[d]
        loss_o[...] = jnp.sum(tot_v[LOSS_VEC_ROW:LOSS_VEC_ROW + 1, :], axis=-1, keepdims=True) + jnp.zeros((1, 128), F32)
        for i, name in enumerate(SMALL_NAMES):
            if name == "sg_w":
                gt = tot_w
            else:
                rows, width = w_r[name].shape
                gt = tot_v[VEC_ROW[name]:VEC_ROW[name] + rows, 0:width]
            delta, mn, vn = _adamw(w_r[name][...], gt, m_r[name][...], v_r[name][...])
            outs[4 * i][...] = gt
            outs[4 * i + 1][...] = delta
            outs[4 * i + 2][...] = mn
            outs[4 * i + 3][...] = vn

    vm = pl.BlockSpec(memory_space=pltpu.VMEM)
    args = [d[k] for d in (g, w, m, v) for k in SMALL_NAMES] + [loss_l]
    out_shape = [jax.ShapeDtypeStruct((1, 128), F32)]
    out_shape += [jax.ShapeDtypeStruct(w[k].shape, F32) for k in SMALL_NAMES for _ in range(4)]
    res = pl.pallas_call(
        body, name="small_allreduce_update", in_specs=[vm] * len(args), out_specs=[vm] * len(out_shape),
        out_shape=out_shape,
        scratch_shapes=[pltpu.VMEM((8, VEC_ROWS, 1024), F32), pltpu.VMEM((8, 8, CH, CH), F32),
                        pltpu.SemaphoreType.DMA((14,)), pltpu.SemaphoreType.DMA((14,))],
        compiler_params=pltpu.CompilerParams(vmem_limit_bytes=32 * 1024 * 1024),
    )(*args)
    return res[0], {k: res[1 + 4 * i:5 + 4 * i] for i, k in enumerate(SMALL_NAMES)}


def _win_kernel_order(gathered):
    w_in = jnp.concatenate([gathered[k].reshape(D, 768)[:, :642] for k in range(4)], axis=1)
    return jnp.concatenate([w_in[:, :3 * AW], w_in[:, 3 * AW + NH:], w_in[:, 3 * AW:3 * AW + NH],
                            jnp.zeros((D, 128 - NH), w_in.dtype)], axis=1)


LATE_ROWS = 256 + 1024 + 1024 + 64 + 256


def _pack_late(w_out, w1, w2, plew, wg):
    return jnp.concatenate([w_out, w1, w2, plew.reshape(64, 1024), wg], axis=0)


def _unpack_late(gathered):
    cols = lambda t: jnp.swapaxes(t, 0, 1).reshape(t.shape[1], 4 * t.shape[2])
    return (gathered[:, 0:256].reshape(D, D), cols(gathered[:, 256:1280]), gathered[:, 1280:2304].reshape(DFF, D),
            cols(gathered[:, 2304:2368].reshape(4, 256, 256)), gathered[:, 2368:2624].reshape(D, D))


def _local_step(x, p, tgt, win_k, late_weights, token, small):
    T = x.shape[0]
    row = lambda n: small[n].reshape(1, -1)
    fbias = jnp.pad(row("f_bias"), ((0, 0), (0, 128 - NH))) + token[0:1, :]
    wm = _masked_sg_w(small["sg_w"].reshape(8, CH, CH))
    wmb = wm.astype(BF16)
    wmt = jnp.swapaxes(wm, 1, 2).astype(BF16)
    bsg = jnp.repeat(small["sg_b"].reshape(8, CH).T, DH, axis=1)
    ln_g, ln_b, gsg, gatt = row("sg_ln_g"), row("sg_ln_b"), row("sg_out_g"), row("att_out_g")
    gpre, gpm, gpf, gpff, bg = row("pre_mix_g"), row("post_mix_g"), row("pre_ffn_g"), row("post_ffn_g"), row("ple_gate_b")
    gsel = (jnp.arange(AW)[:, None] // DH == jnp.arange(128)[None, :]).astype(BF16)
    hsel = (jnp.arange(AW)[:, None] // DH == jnp.arange(AW)[None, :] // DH).astype(BF16)

    expand, shrink, pieces, qconst, _, pick64 = _head_consts()
    a, qkv, flog, ccol, zuv, ysgn, q8, k8, v8 = _pre_attn_fwd(
        x, gpre, win_k, fbias, ln_g, ln_b, wmb, bsg, gsg, expand, pieces, qconst)

    nt = T // TQ
    hd = lambda t, i: t[:, i * AW:(i + 1) * AW].reshape(T, NH, DH)
    zpad = lambda n: jnp.zeros((T, NH, n), BF16)
    one = jnp.ones((T, NH, 1), BF16)
    wide = lambda parts: jnp.concatenate(parts, axis=-1).reshape(T, NH * 128)
    slabs = lambda t: jnp.swapaxes(t.reshape(nt, TQ, NH * 128), 1, 2)
    qt8 = slabs(q8)
    kt8 = slabs(wide([hd(qkv, 1), one, zpad(63)]))
    vt8 = slabs(wide([hd(qkv, 2), one, zpad(63)]))
    lanes = jnp.arange(128)
    sel = jnp.stack([((lanes[:, None] == lanes[None, :] - DH * j) & (lanes[:, None] < DH)).astype(BF16)
                     for j in (0, 1)])

    yatt, lse = _flash_fwd(qt8, k8, vt8, sel)
    wout, w1, w2, plew, wg = late_weights(lse)
    y, ov, h1, c2, sact, rr = _tail_fwd1(x, yatt, ysgn, gatt, wout, gpm, gpf, w1)
    ff, h2b, de, dpre, dh2, loss_l, dbg = _tail_fwd2(sact, h1, p, tgt, w2, gpff, wg, bg, plew)
    dff, dr, do, do8, dlt, dysg, dh1, dgpff, dgpf, dgpm, dgatt = _tail_bwd(
        dh2, ff, rr, h1, ov, yatt, w2, w1, wout, gpff, gpf, gpm, gatt, hsel, expand)
    dlt4 = jnp.pad(dlt[:, ::DH].T.reshape(4, 2, T), ((0, 0), (0, 6), (0, 0)))
    dqt, dk8, dv8 = _flash_bwd(q8, qt8, k8, kt8, v8, do8, slabs(do8), lse, dlt4)
    dx, dz, dgpre, dfb, dgsg, dlng, dlnb, dws, _, dsbt = _pre_attn_bwd(
        x, dh1, jnp.swapaxes(dqt, 1, 2).reshape(T, NH * 128), dk8, dv8, flog, zuv, dysg,
        gpre, win_k, ln_g, ln_b, wmb, wmt, bsg, gsg, gsel, shrink, pick64)

    dwin_k = _matmul_tn("grad_w_in", a, dz, tn=384)
    dwout = _matmul_tn("grad_w_out", y, do)
    dw1 = _matmul_tn("grad_w_ff1", c2, dr, shards=4)
    dw2 = _matmul_tn("grad_w_ff2", sact, dff)
    dwg = _matmul_tn("grad_ple_gate_w", h2b, dpre)
    dplew = _matmul_tn("grad_ple_w", p, de, tn=256, shards=4)

    dsb = dsbt[:, :8].T
    gsmall = {"sg_w": dws, "f_bias": dfb, "sg_ln_g": dlng, "sg_ln_b": dlnb, "sg_b": dsb,
              "att_out_g": dgatt, "sg_out_g": dgsg, "pre_mix_g": dgpre, "post_mix_g": dgpm, "pre_ffn_g": dgpf,
              "post_ffn_g": dgpff, "ple_gate_b": dbg}
    return loss_l, dx, (dwin_k, dwout, dw1, dw2, dplew, dwg), gsmall


def kernel(x, p, w_in, f_bias, sg_ln_g, sg_ln_b, sg_w, sg_b, att_out_g, sg_out_g, w_out, pre_mix_g, post_mix_g, pre_ffn_g, post_ffn_g, w_ff1, w_ff2, ple_w, ple_gate_w, ple_gate_b, loss_target, m_w_in, m_f_bias, m_sg_ln_g, m_sg_ln_b, m_sg_w, m_sg_b, m_att_out_g, m_sg_out_g, m_w_out, m_pre_mix_g, m_post_mix_g, m_pre_ffn_g, m_post_ffn_g, m_w_ff1, m_w_ff2, m_ple_w, m_ple_gate_w, m_ple_gate_b, v_w_in, v_f_bias, v_sg_ln_g, v_sg_ln_b, v_sg_w, v_sg_b, v_att_out_g, v_sg_out_g, v_w_out, v_pre_mix_g, v_post_mix_g, v_pre_ffn_g, v_post_ffn_g, v_w_ff1, v_w_ff2, v_ple_w, v_ple_gate_w, v_ple_gate_b):
    c = lax.axis_index("c")
    big = lambda t: (t[0][0], t[1][0], t[2][0], t[3][0], t[4][0], t[5][0])
    w_big = big((w_in, w_out, w_ff1, w_ff2, ple_w, ple_gate_w))
    m_big = big((m_w_in, m_w_out, m_w_ff1, m_w_ff2, m_ple_w, m_ple_gate_w))
    v_big = big((v_w_in, v_w_out, v_w_ff1, v_w_ff2, v_ple_w, v_ple_gate_w))
    small = {"sg_w": sg_w, "f_bias": f_bias, "sg_ln_g": sg_ln_g, "sg_ln_b": sg_ln_b, "sg_b": sg_b,
             "att_out_g": att_out_g, "sg_out_g": sg_out_g, "pre_mix_g": pre_mix_g, "post_mix_g": post_mix_g,
             "pre_ffn_g": pre_ffn_g, "post_ffn_g": post_ffn_g, "ple_gate_b": ple_gate_b}
    m_small = {"sg_w": m_sg_w, "f_bias": m_f_bias, "sg_ln_g": m_sg_ln_g, "sg_ln_b": m_sg_ln_b, "sg_b": m_sg_b,
               "att_out_g": m_att_out_g, "sg_out_g": m_sg_out_g, "pre_mix_g": m_pre_mix_g,
               "post_mix_g": m_post_mix_g, "pre_ffn_g": m_pre_ffn_g, "post_ffn_g": m_post_ffn_g,
               "ple_gate_b": m_ple_gate_b}
    v_small = {"sg_w": v_sg_w, "f_bias": v_f_bias, "sg_ln_g": v_sg_ln_g, "sg_ln_b": v_sg_ln_b, "sg_b": v_sg_b,
               "att_out_g": v_att_out_g, "sg_out_g": v_sg_out_g, "pre_mix_g": v_pre_mix_g,
               "post_mix_g": v_post_mix_g, "pre_ffn_g": v_pre_ffn_g, "post_ffn_g": v_post_ffn_g,
               "ple_gate_b": v_ple_gate_b}

    late = _gather_late_start(_pack_late(*w_big[1:]).astype(BF16))
    win_k = _win_kernel_order(_gather_weights(
        jnp.pad(w_big[0], ((0, 0), (0, 768 - 642))).reshape(768, 1024).astype(BF16)))
    late_weights = lambda after: _unpack_late(_gather_late_wait(late[0], late[1], late[2], late[3], after))

    loss_l, dx, gbig, gsmall = _local_step(x[0], p[0, 0], loss_target[0], win_k, late_weights, late[4], small)

    dwin_k, dwout, dw1, dw2, dplew, dwg = gbig
    dwin = jnp.concatenate([dwin_k[:, :3 * AW], dwin_k[:, 5 * AW:5 * AW + NH], dwin_k[:, 3 * AW:5 * AW]], axis=1)
    dwin = jnp.pad(jnp.swapaxes(dwin.reshape(D, 4, 642), 0, 1), ((0, 0), (0, 0), (0, 768 - 642)))
    names = ("w_in", "w_out", "w_ff1", "w_ff2", "ple_w", "ple_gate_w")
    gs = [dwin, dwout.reshape(4, 256, D), dw1, dw2.reshape(4, D, D), dplew, dwg.reshape(4, 256, D)]
    padded = lambda t: (jnp.pad(t[0], ((0, 0), (0, 768 - 642))),) + tuple(t[1:])
    c1 = jnp.reshape(c, (1,)).astype(jnp.int32)
    gots = _swap_halves(gs)
    parts = _exchange_chips([_pair_sum(nm, c1, g, got) for nm, g, got in zip(names, gs, gots)])
    ghs = [_reduce_chips(nm, pt) for nm, pt in zip(names, parts)]
    got2 = _share_grad(ghs)
    big_out = [_update(nm, c1, gh, g2, w, m, v) for nm, gh, g2, w, m, v in
               zip(names, ghs, got2, padded(w_big), padded(m_big), padded(v_big))]
    big_out = [[big_out[j][i][:, :642] if j == 0 else big_out[j][i] for j in range(6)] for i in range(4)]

    view = lambda t: t.reshape(t.shape[-3:]) if t.ndim == 4 else t.reshape(t.shape[-2:])
    views = lambda d: {k: view(d[k]) for k in SMALL_NAMES}
    loss11, res_s = _small_allreduce_update(gsmall, views(small), views(m_small), views(v_small), loss_l)
    loss = loss11[0, 0]

    def small_out(i, name):
        return res_s[name][i].reshape(small[name].shape)

    order = ["w_in", "f_bias", "sg_ln_g", "sg_ln_b", "sg_w", "sg_b", "att_out_g", "sg_out_g", "w_out",
             "pre_mix_g", "post_mix_g", "pre_ffn_g", "post_ffn_g", "w_ff1", "w_ff2", "ple_w", "ple_gate_w",
             "ple_gate_b"]
    big_idx = {"w_in": 0, "w_out": 1, "w_ff1": 2, "w_ff2": 3, "ple_w": 4, "ple_gate_w": 5}
    outs = [loss, dx[None]]
    for i in range(4):
        for name in order:
            if name in big_idx:
                outs.append(big_out[i][big_idx[name]][None])
            else:
                outs.append(small_out(i, name))
    return tuple(outs)
```

```python
import math

import jax
import jax.numpy as jnp
from jax import lax
from jax.experimental import pallas as pl
from jax.experimental.pallas import tpu as pltpu

F32 = jnp.float32
BF16 = jnp.bfloat16
MESH = pl.DeviceIdType.MESH

D = 1024
DH = 64
NH = 8
AW = 512
CH = 128
DFF = 4096
ZW = 5 * AW + 128
EPS = 1e-6
NEG = -1e30
MASKED = -2e30

TM = 256
TQ = 256

LR, B1, B2, AEPS, WD, STEP = 0.001, 0.9, 0.999, 1e-08, 0.01, 10
BC1 = 1.0 - B1 ** STEP
BC2 = 1.0 - B2 ** STEP

VEC_NAMES = ("f_bias", "sg_ln_g", "sg_ln_b", "sg_b", "att_out_g", "sg_out_g", "pre_mix_g",
             "post_mix_g", "pre_ffn_g", "post_ffn_g", "ple_gate_b")


def _dot(a, b):
    return jnp.dot(a, b, preferred_element_type=F32)


def _dot_nt(a, b):
    return lax.dot_general(a, b, (((1,), (1,)), ((), ())), preferred_element_type=F32)


def _dot_tn(a, b):
    return lax.dot_general(a, b, (((0,), (0,)), ((), ())), preferred_element_type=F32)


def _split3(x):
    h = x.astype(BF16)
    r = x - h.astype(F32)
    m = r.astype(BF16)
    l = (r - m.astype(F32)).astype(BF16)
    return h, m, l


def _dot01(sel, x):
    h, m, l = _split3(x)
    return _dot(sel, h) + _dot(sel, m) + _dot(sel, l)


def _dot01_r(x, sel):
    h, m, l = _split3(x)
    return _dot(h, sel) + _dot(m, sel) + _dot(l, sel)


def _dot01_tn(x, sel):
    h, m, l = _split3(x)
    return _dot_tn(h, sel) + _dot_tn(m, sel) + _dot_tn(l, sel)


def _rs(x, n):
    return lax.rsqrt(jnp.sum(x * x, axis=-1, keepdims=True) * (1.0 / n) + EPS)


def _rms_bwd(dn, x, rs, g, n):
    w = dn * g
    dx = rs * w - x * ((rs * rs * rs) * (1.0 / n) * jnp.sum(w * x, axis=-1, keepdims=True))
    return dx, jnp.sum(dn * x * rs, axis=0, keepdims=True)


_GC = math.sqrt(2.0 / math.pi)


def _gelu(x):
    t = jnp.tanh(_GC * (x + 0.044715 * x * x * x))
    return 0.5 * x * (1.0 + t), t


def _gelu_grad(x, t):
    return 0.5 * (1.0 + t) + 0.5 * x * (1.0 - t * t) * (_GC * (1.0 + 3.0 * 0.044715 * x * x))


def _params(vmem_mb, sem=("arbitrary",)):
    return pltpu.CompilerParams(dimension_semantics=sem, vmem_limit_bytes=vmem_mb * 1024 * 1024)


def _row_call(name, body, T, tm, tiled, resident, outs, accs, scratch=(), reverse=False, vmem_mb=48):
    nt = T // tm
    n_t, n_r, n_o, n_a = len(tiled), len(resident), len(outs), len(accs)

    def kern(*refs):
        t_refs = refs[:n_t]
        r_hbm = refs[n_t:n_t + n_r]
        o_refs = refs[n_t + n_r:n_t + n_r + n_o]
        a_refs = refs[n_t + n_r + n_o:n_t + n_r + n_o + n_a]
        r_vmem = refs[n_t + n_r + n_o + n_a:n_t + 2 * n_r + n_o + n_a]
        s_refs = refs[n_t + 2 * n_r + n_o + n_a:]

        @pl.when(pl.program_id(0) == 0)
        def _():
            for h, v in zip(r_hbm, r_vmem):
                pltpu.sync_copy(h, v)
            for a in a_refs + s_refs:
                a[...] = jnp.zeros(a.shape, a.dtype)

        body(t_refs, r_vmem, o_refs, a_refs, s_refs)

    if reverse:
        idx = lambda i: (nt - 1 - i, 0)
        idx_t = lambda i: (nt - 1 - i, 0, 0)
    else:
        idx = lambda i: (i, 0)
        idx_t = lambda i: (i, 0, 0)
    arrays, in_specs = [], []
    for a in tiled:
        if isinstance(a, tuple):
            arrays.append(a[0])
            in_specs.append(pl.BlockSpec((None, a[0].shape[1], tm), idx_t))
        else:
            arrays.append(a)
            in_specs.append(pl.BlockSpec((tm, a.shape[1]), idx))
    in_specs += [pl.BlockSpec(memory_space=pl.ANY) for _ in resident]
    out_shape, out_specs = [], []
    for o in outs:
        if len(o) == 3:
            out_shape.append(jax.ShapeDtypeStruct((nt, o[0], tm), o[1]))
            out_specs.append(pl.BlockSpec((None, o[0], tm), idx_t))
        else:
            out_shape.append(jax.ShapeDtypeStruct((T, o[0]), o[1]))
            out_specs.append(pl.BlockSpec((tm, o[0]), idx))
    out_shape += [jax.ShapeDtypeStruct(s, F32) for s in accs]
    out_specs += [pl.BlockSpec(s, lambda i, n=len(s): (0,) * n) for s in accs]
    scratch_shapes = [pltpu.VMEM(r.shape, r.dtype) for r in resident]
    scratch_shapes += [pltpu.VMEM(s, F32) for s in scratch]
    return pl.pallas_call(
        kern, name=name, grid=(nt,), in_specs=in_specs, out_specs=out_specs, out_shape=out_shape,
        scratch_shapes=scratch_shapes, compiler_params=_params(vmem_mb),
    )(*arrays, *resident)


def _sg_forward(zu, zv, wm_ref, bsg, lng, lnb, mixed_ref, tm):
    gu, tu = _gelu(zu)
    vg, tv = _gelu(zv)
    mu = jnp.sum(vg, axis=-1, keepdims=True) * (1.0 / AW)
    xc = vg - mu
    rstd = lax.rsqrt(jnp.sum(xc * xc, axis=-1, keepdims=True) * (1.0 / AW) + EPS)
    xhat = xc * rstd
    vvb = (xhat * lng + lnb).astype(BF16)
    lane = lax.broadcasted_iota(jnp.int32, (CH, 128), 1)
    for c in range(tm // CH):
        for j in range(4):
            blk = vvb[c * CH:(c + 1) * CH, j * 128:(j + 1) * 128]
            m0 = _dot(wm_ref[2 * j], blk)
            m1 = _dot(wm_ref[2 * j + 1], blk)
            mixed_ref[c * CH:(c + 1) * CH, j * 128:(j + 1) * 128] = (
                jnp.where(lane < DH, m0, m1) + bsg[:, j * 128:(j + 1) * 128])
    return gu, tu, tv, xhat, rstd, vvb, mixed_ref[...]


def _head_consts():
    src = jnp.arange(AW)
    dst = (src // DH) * 128 + src % DH
    wide = jnp.arange(NH * 128)
    expand = (dst[:, None] == wide[None, :]).astype(BF16)
    heads = jnp.arange(128)
    pieces = jnp.stack([((heads[:, None] * 128 + DH + i == wide[None, :]) & (heads[:, None] < NH)).astype(BF16)
                        for i in range(3)])
    spare = wide % 128 - DH
    qconst = jnp.where((spare >= 0) & (spare < 3), -1.0, 0.0).astype(F32)[None, :]
    one64 = jnp.where(spare == 0, 1.0, 0.0).astype(F32)[None, :]
    pick64 = ((wide[:, None] == heads[None, :] * 128 + DH) & (heads[None, :] < NH)).astype(BF16)
    return expand, expand.T, pieces, qconst, one64, pick64


def _masked_sg_w(sg_w):
    r = lax.broadcasted_iota(jnp.int32, (CH, CH), 0)
    c = lax.broadcasted_iota(jnp.int32, (CH, CH), 1)
    return jnp.where((c <= r)[None], sg_w, 0.0)


def _pre_attn_fwd(x, gpre, win, fbias, lng, lnb, wm, bsg, gsg, expand, pieces, qconst):
    T = x.shape[0]
    tm = TM

    def body(t, r, o, a, s):
        (x_ref,) = t
        gpre_r, win_r, fb_r, lng_r, lnb_r, wm_r, bsg_r, gsg_r, ex_r, pc_r, qc_r = r
        a_o, qkv_o, flog_o, ccol_o, zuv_o, ysgn_o, q8_o, k8_o, v8_o = o
        carry_ref, mixed_ref = s
        xv = x_ref[...]
        av = (xv * _rs(xv, D) * gpre_r[...]).astype(BF16)
        a_o[...] = av
        z = _dot(av, win_r[...])
        zu = z[:, 3 * AW:4 * AW]
        zv = z[:, 4 * AW:5 * AW]
        zuv_o[:, 0:AW] = zu
        zuv_o[:, AW:2 * AW] = zv
        zf = z[:, 5 * AW:] + fb_r[...]
        flog_o[...] = zf
        lane = lax.broadcasted_iota(jnp.int32, (tm, 128), 1)
        logf = jnp.where(lane < NH, jnp.minimum(zf, 0.0) - jnp.log(1.0 + jnp.exp(-jnp.abs(zf))), 0.0)
        rr = lax.broadcasted_iota(jnp.int32, (tm, tm), 0)
        cc = lax.broadcasted_iota(jnp.int32, (tm, tm), 1)
        tri = (cc <= rr).astype(BF16)
        cum = _dot01(tri, logf) + carry_ref[...]
        carry_ref[...] = cum[tm - 1:tm, :]
        ccol_o[...] = cum
        ex = ex_r[...]
        q8_o[...] = (_dot((z[:, 0:AW] * (DH ** -0.5)).astype(BF16), ex) + qc_r[...]).astype(BF16)
        ch, cm, cl = _split3(cum)
        k8_o[...] = (_dot(z[:, AW:2 * AW].astype(BF16), ex) + _dot(ch, pc_r[0]) + _dot(cm, pc_r[1])
                     + _dot(cl, pc_r[2])).astype(BF16)
        v8_o[...] = _dot(z[:, 2 * AW:3 * AW].astype(BF16), ex).astype(BF16)
        qkv_o[:, 0:AW] = (z[:, 0:AW] * (DH ** -0.5)).astype(BF16)
        qkv_o[:, AW:3 * AW] = z[:, AW:3 * AW].astype(BF16)
        gu, _, _, _, _, _, mixed = _sg_forward(zu, zv, wm_r, bsg_r[...], lng_r[...], lnb_r[...], mixed_ref, tm)
        ysg = gu * mixed
        ysgn_o[...] = (ysg * _rs(ysg, AW) * gsg_r[...]).astype(BF16)

    return _row_call(
        "pre_attn_fwd", body, T, tm, [x], [gpre, win, fbias, lng, lnb, wm, bsg, gsg, expand, pieces, qconst],
        [(D, BF16), (3 * AW, BF16), (128, F32), (128, F32), (2 * AW, F32), (AW, BF16), (NH * 128, BF16),
         (NH * 128, BF16), (NH * 128, BF16)], [],
        scratch=[(1, 128), (tm, AW)], vmem_mb=48)


def _flash_fwd(qt8, k8, vt8, sel):
    T = k8.shape[0]
    nq = T // TQ

    def body(qt_ref, k_ref, vt_ref, sel_ref, o_ref, l_ref, u_scr, p_scr):
        qi = pl.program_id(1)
        qts = (qt_ref[0:128, :], qt_ref[128:256, :])
        dmat = (lax.broadcasted_iota(jnp.int32, (TQ, TQ), 0) - lax.broadcasted_iota(jnp.int32, (TQ, TQ), 1))
        u_scr[1] = jnp.full((2, TQ, TQ), MASKED, F32)
        p_scr[...] = jnp.zeros(p_scr.shape, BF16)

        def sub(t, carry, sc, sb, masked):
            blk_c = jnp.clip(t - 2, 0, qi)
            off_a = pl.multiple_of(jnp.minimum(t, qi) * TQ, TQ)
            new = []
            for j in (0, 1):
                m, al, acc = carry[j]
                acc = al * acc + _dot(vt_ref[blk_c, j * 128:(j + 1) * 128, :], p_scr[sc, j])
                m_new = jnp.maximum(m, jnp.max(u_scr[sb, j], axis=0, keepdims=True))
                p_scr[sb, j] = jnp.exp(u_scr[sb, j] - m_new).astype(BF16)
                u = _dot(k_ref[pl.ds(off_a, TQ), j * 128:(j + 1) * 128], qts[j])
                u_scr[sc, j] = jnp.where(dmat <= (qi - t) * TQ, u, MASKED) if masked else u
                new.append((m_new, jnp.exp(m - m_new), acc))
            return tuple(new)

        def pair(t2, carry, masked):
            return sub(2 * t2 + 1, sub(2 * t2, carry, 0, 1, masked), 1, 0, masked)

        init = tuple((jnp.full((1, TQ), NEG, F32), jnp.ones((1, TQ), F32), jnp.zeros((128, TQ), F32))
                     for _ in (0, 1))
        carry = lax.fori_loop(0, qi // 2, lambda t2, cr: pair(t2, cr, False), init)
        (m0, _, a0), (m1, _, a1) = pair(qi // 2 + 1, pair(qi // 2, carry, True), True)
        l0 = a0[DH:DH + 1, :]
        l1 = a1[DH:DH + 1, :]
        o_ref[...] = _dot01_tn(a0 * (1.0 / l0), sel_ref[0]) + _dot01_tn(a1 * (1.0 / l1), sel_ref[1])
        l_ref[0:1, :] = m0 + jnp.log(l0)
        l_ref[1:2, :] = m1 + jnp.log(l1)
        l_ref[2:8, :] = jnp.zeros((6, TQ), F32)

    return pl.pallas_call(
        body, name="flash_fwd", grid=(4, nq),
        in_specs=[pl.BlockSpec((None, 256, TQ), lambda h, i: (i, h, 0)),
                  pl.BlockSpec((T, 256), lambda h, i: (0, h)),
                  pl.BlockSpec((nq, 256, TQ), lambda h, i: (0, h, 0)),
                  pl.BlockSpec((2, 128, 128), lambda h, i: (0, 0, 0))],
        out_specs=[pl.BlockSpec((TQ, 128), lambda h, i: (i, h)),
                   pl.BlockSpec((None, 8, TQ), lambda h, i: (h, 0, i))],
        out_shape=[jax.ShapeDtypeStruct((T, AW), F32), jax.ShapeDtypeStruct((4, 8, T), F32)],
        scratch_shapes=[pltpu.VMEM((2, 2, TQ, TQ), F32), pltpu.VMEM((2, 2, TQ, TQ), BF16)],
        compiler_params=_params(40, ("arbitrary", "arbitrary")),
    )(qt8, k8, vt8, sel)


def _flash_bwd(q8, qt8, k8, kt8, v8, do8, dot8, lse, dlt):
    T = q8.shape[0]
    nk = T // TQ

    def body(q_ref, qt_ref, k_ref, kt_ref, v_ref, do_ref, dot_ref, l_ref, d_ref, dqt_ref, dk_ref, dv_ref,
             u_scr, dp_scr, p_scr, ds_scr):
        kb = pl.program_id(1)
        n = nk - kb

        @pl.when(kb == 0)
        def _():
            dqt_ref[...] = jnp.zeros(dqt_ref.shape, F32)

        dk_ref[...] = jnp.zeros(dk_ref.shape, F32)
        dv_ref[...] = jnp.zeros(dv_ref.shape, F32)
        u_scr[1] = jnp.full((2, TQ, TQ), MASKED, F32)
        dp_scr[1] = jnp.zeros((2, TQ, TQ), F32)
        p_scr[...] = jnp.zeros(p_scr.shape, BF16)
        ds_scr[...] = jnp.zeros(ds_scr.shape, BF16)
        dmat = (lax.broadcasted_iota(jnp.int32, (TQ, TQ), 0) - lax.broadcasted_iota(jnp.int32, (TQ, TQ), 1))
        ks = (k_ref[:, 0:128], k_ref[:, 128:256])
        vs = (v_ref[:, 0:128], v_ref[:, 128:256])
        kts = (kt_ref[0:128, :], kt_ref[128:256, :])

        def sub(t, sc, sb):
            blk_a = kb + jnp.minimum(t, n - 1)
            blk_c = kb + jnp.clip(t - 2, 0, n - 1)
            off_b = pl.multiple_of((kb + jnp.clip(t - 1, 0, n - 1)) * TQ, TQ)
            off_c = pl.multiple_of(blk_c * TQ, TQ)
            lim = jnp.where(t < n, t * TQ, -TQ)
            for j in (0, 1):
                hl = slice(j * 128, (j + 1) * 128)
                dqt_ref[blk_c, hl, :] += _dot(kts[j], ds_scr[sc, j])
                dk_ref[:, hl] += _dot(ds_scr[sc, j], q_ref[pl.ds(off_c, TQ), hl])
                dv_ref[:, hl] += _dot(p_scr[sc, j], do_ref[pl.ds(off_c, TQ), hl])
                p = jnp.exp(u_scr[sb, j] - l_ref[j:j + 1, pl.ds(off_b, TQ)])
                p_scr[sb, j] = p.astype(BF16)
                ds_scr[sb, j] = (p * (dp_scr[sb, j] - d_ref[j:j + 1, pl.ds(off_b, TQ)])).astype(BF16)
                u_scr[sc, j] = jnp.where(dmat <= lim, _dot(ks[j], qt_ref[blk_a, hl, :]), MASKED)
                dp_scr[sc, j] = _dot(vs[j], dot_ref[blk_a, hl, :])

        def it(t2, carry):
            sub(2 * t2, 0, 1)
            sub(2 * t2 + 1, 1, 0)
            return carry

        lax.fori_loop(0, (n + 3) // 2, it, 0)

    return pl.pallas_call(
        body, name="flash_bwd", grid=(4, nk),
        in_specs=[pl.BlockSpec((T, 256), lambda h, i: (0, h)),
                  pl.BlockSpec((nk, 256, TQ), lambda h, i: (0, h, 0)),
                  pl.BlockSpec((TQ, 256), lambda h, i: (i, h)),
                  pl.BlockSpec((None, 256, TQ), lambda h, i: (i, h, 0)),
                  pl.BlockSpec((TQ, 256), lambda h, i: (i, h)),
                  pl.BlockSpec((T, 256), lambda h, i: (0, h)),
                  pl.BlockSpec((nk, 256, TQ), lambda h, i: (0, h, 0)),
                  pl.BlockSpec((None, 8, T), lambda h, i: (h, 0, 0)),
                  pl.BlockSpec((None, 8, T), lambda h, i: (h, 0, 0))],
        out_specs=[pl.BlockSpec((nk, 256, TQ), lambda h, i: (0, h, 0)),
                   pl.BlockSpec((TQ, 256), lambda h, i: (i, h)),
                   pl.BlockSpec((TQ, 256), lambda h, i: (i, h))],
        out_shape=[jax.ShapeDtypeStruct((nk, NH * 128, TQ), F32), jax.ShapeDtypeStruct((T, NH * 128), F32),
                   jax.ShapeDtypeStruct((T, NH * 128), F32)],
        scratch_shapes=[pltpu.VMEM((2, 2, TQ, TQ), F32), pltpu.VMEM((2, 2, TQ, TQ), F32),
                        pltpu.VMEM((2, 2, TQ, TQ), BF16), pltpu.VMEM((2, 2, TQ, TQ), BF16)],
        compiler_params=_params(56, ("arbitrary", "arbitrary")),
    )(q8, qt8, k8, kt8, v8, do8, dot8, lse, dlt)


def _tail_fwd1(x, yatt, ysgn, gatt, wout, gpm, gpf, w1):
    T = x.shape[0]

    def body(t, r, o, a, s):
        x_ref, ya_ref, ys_ref = t
        gatt_r, wout_r, gpm_r, gpf_r, w1_r = r
        y_o, o_o, h1_o, c2_o, s_o, rr_o = o
        ya = ya_ref[...]
        yan = (ya * _rs(ya, AW) * gatt_r[...]).astype(BF16)
        y_o[:, 0:AW] = yan
        y_o[:, AW:] = ys_ref[...]
        ov = _dot(yan, wout_r[0:AW, :]) + _dot(ys_ref[...], wout_r[AW:, :])
        o_o[...] = ov
        h1 = x_ref[...] + ov * _rs(ov, D) * gpm_r[...]
        h1_o[...] = h1
        c2 = (h1 * _rs(h1, D) * gpf_r[...]).astype(BF16)
        c2_o[...] = c2
        rr = jnp.maximum(_dot(c2, w1_r[...]), 0.0)
        rr_o[...] = rr.astype(BF16)
        s_o[...] = (rr * rr).astype(BF16)

    return _row_call(
        "tail_fwd1", body, T, TM, [x, yatt, ysgn], [gatt, wout, gpm, gpf, w1],
        [(D, BF16), (D, F32), (D, F32), (D, BF16), (DFF, BF16), (DFF, BF16)], [], vmem_mb=48)


def _tail_fwd2(sact, h1, p, tgt, w2, gpff, wg, bg, wpe):
    T = h1.shape[0]

    def body(t, r, o, a, s):
        s_ref, h1_ref, p_ref, t_ref = t
        w2_r, gpff_r, wg_r, bg_r, wpe_r = r
        ff_o, h2b_o, de_o, dpre_o, dh2_o = o
        loss_a, dbg_a = a
        ff = _dot(s_ref[...], w2_r[...])
        ff_o[...] = ff
        h2 = h1_ref[...] + ff * _rs(ff, D) * gpff_r[...]
        h2b = h2.astype(BF16)
        h2b_o[...] = h2b
        gate = 1.0 / (1.0 + jnp.exp(-(_dot(h2b, wg_r[...]) + bg_r[...])))
        e = _dot(p_ref[...].astype(BF16), wpe_r[...])
        diff = h2 + gate * e - t_ref[...]
        loss_a[...] += jnp.sum(diff * diff, axis=0, keepdims=True)
        dh3 = diff * (1.0 / D)
        de_o[...] = (dh3 * gate).astype(BF16)
        dpre = dh3 * e * gate * (1.0 - gate)
        dbg_a[...] += jnp.sum(dpre, axis=0, keepdims=True)
        dpb = dpre.astype(BF16)
        dpre_o[...] = dpb
        dh2_o[...] = dh3 + _dot_nt(dpb, wg_r[...])

    return _row_call(
        "tail_fwd2", body, T, TM, [sact, h1, p, tgt], [w2, gpff, wg, bg, wpe],
        [(D, F32), (D, BF16), (D, BF16), (D, BF16), (D, F32)], [(1, D), (1, D)], vmem_mb=48)


def _tail_bwd(dh2, ff, rr, h1, ov, yatt, w2, w1, wout, gpff, gpf, gpm, gatt, hsel, expand):
    T = dh2.shape[0]

    def body(t, r, o, a, s):
        dh2_ref, ff_ref, rr_ref, h1_ref, o_ref, ya_ref = t
        w2_r, w1_r, wout_r, gpff_r, gpf_r, gpm_r, gatt_r, hsel_r, ex_r = r
        dff_o, dr_o, do_o, do8_o, dlt_o, dysg_o, dh1_o = o
        dgpff_a, dgpf_a, dgpm_a, dgatt_a = a
        dh2v = dh2_ref[...]
        ffv = ff_ref[...]
        dff, dg = _rms_bwd(dh2v, ffv, _rs(ffv, D), gpff_r[...], D)
        dgpff_a[...] += dg
        dffb = dff.astype(BF16)
        dff_o[...] = dffb
        drb = (_dot_nt(dffb, w2_r[...]) * (2.0 * rr_ref[...].astype(F32))).astype(BF16)
        dr_o[...] = drb
        dc2 = _dot_nt(drb, w1_r[...])
        h1v = h1_ref[...]
        d1, dg = _rms_bwd(dc2, h1v, _rs(h1v, D), gpf_r[...], D)
        dgpf_a[...] += dg
        dh1 = dh2v + d1
        dh1_o[...] = dh1
        ovv = o_ref[...]
        dov, dg = _rms_bwd(dh1, ovv, _rs(ovv, D), gpm_r[...], D)
        dgpm_a[...] += dg
        dob = dov.astype(BF16)
        do_o[...] = dob
        dysg_o[...] = _dot_nt(dob, wout_r[AW:, :])
        dyan = _dot_nt(dob, wout_r[0:AW, :])
        ya = ya_ref[...]
        dya, dg = _rms_bwd(dyan, ya, _rs(ya, AW), gatt_r[...], AW)
        dgatt_a[...] += dg
        do8_o[...] = _dot(dya.astype(BF16), ex_r[...]).astype(BF16)
        dlt_o[...] = _dot01_r(dya * ya, hsel_r[...])

    return _row_call(
        "tail_bwd", body, T, TM, [dh2, ff, rr, h1, ov, yatt],
        [w2, w1, wout, gpff, gpf, gpm, gatt, hsel, expand],
        [(D, BF16), (DFF, BF16), (D, BF16), (NH * 128, BF16), (AW, F32), (AW, F32), (D, F32)],
        [(1, D), (1, D), (1, D), (1, AW)], vmem_mb=56)


def _pre_attn_bwd(x, dh1, dq8, dk8, dv8, flog, zuv, dysg, gpre, win, lng, lnb, wm, wmt, bsg, gsg, gsel, shrink, pick64):
    T = x.shape[0]
    tm = TM

    def body(t, r, o, a, s):
        x_ref, dh1_ref, dq_ref, dk_ref, dv_ref, fl_ref, zuv_ref, dys_ref = t
        gpre_r, win_r, lng_r, lnb_r, wm_r, wmt_r, bsg_r, gsg_r, gsel_r, sh_r, p64_r = r
        dx_o, dz_o = o
        dgpre_a, dfb_a, dgsg_a, dlng_a, dlnb_a, dws_a, dbs_a, dsb_a = a
        carry_ref, mixed_ref, dvv_ref = s
        dq8v = dq_ref[...]
        dk8v = dk_ref[...]
        dcv = _dot01_r(dq8v, p64_r[...]) + _dot01_r(dk8v, p64_r[...])
        rr = lax.broadcasted_iota(jnp.int32, (tm, tm), 0)
        cc = lax.broadcasted_iota(jnp.int32, (tm, tm), 1)
        triu = (cc >= rr).astype(BF16)
        dlogf = _dot01(triu, dcv) + carry_ref[...]
        carry_ref[...] = dlogf[0:1, :]
        dzf = dlogf * (1.0 / (1.0 + jnp.exp(fl_ref[...])))
        dfb_a[...] += jnp.sum(dzf, axis=0, keepdims=True)
        dz_o[:, 5 * AW:] = dzf.astype(BF16)
        zu = zuv_ref[:, 0:AW]
        zv = zuv_ref[:, AW:]
        gu, tu, tv, xhat, rstd, vvb, mixed = _sg_forward(
            zu, zv, wm_r, bsg_r[...], lng_r[...], lnb_r[...], mixed_ref, tm)
        ysg = gu * mixed
        dysg_n = dys_ref[...]
        dys, dg = _rms_bwd(dysg_n, ysg, _rs(ysg, AW), gsg_r[...], AW)
        dgsg_a[...] += dg
        dgu = dys * mixed
        dmix = dys * gu
        dmb = dmix.astype(BF16)
        lane = lax.broadcasted_iota(jnp.int32, (CH, 128), 1)
        lo = lane < DH
        for c in range(tm // CH):
            rows = slice(c * CH, (c + 1) * CH)
            dbs_a[...] += dmix[rows, :]
            for j in range(4):
                cols = slice(j * 128, (j + 1) * 128)
                dmblk = dmb[rows, cols]
                vblk = vvb[rows, cols]
                d0 = _dot(wmt_r[2 * j], dmblk)
                d1 = _dot(wmt_r[2 * j + 1], dmblk)
                dvv_ref[rows, cols] = jnp.where(lo, d0, d1)
                dws_a[2 * j] += _dot_nt(jnp.where(lo, dmblk, jnp.zeros_like(dmblk)), vblk)
                dws_a[2 * j + 1] += _dot_nt(jnp.where(lo, jnp.zeros_like(dmblk), dmblk), vblk)
        dvv = dvv_ref[...]
        dlng_a[...] += jnp.sum(dvv * xhat, axis=0, keepdims=True)
        dlnb_a[...] += jnp.sum(dvv, axis=0, keepdims=True)
        dxh = dvv * lng_r[...]
        dvg = rstd * (dxh - jnp.sum(dxh, axis=-1, keepdims=True) * (1.0 / AW)
                      - xhat * (jnp.sum(dxh * xhat, axis=-1, keepdims=True) * (1.0 / AW)))
        dz_o[:, 3 * AW:4 * AW] = (dgu * _gelu_grad(zu, tu)).astype(BF16)
        dz_o[:, 4 * AW:5 * AW] = (dvg * _gelu_grad(zv, tv)).astype(BF16)
        dz_o[:, 0:AW] = _dot((dq8v * (DH ** -0.5)).astype(BF16), sh_r[...]).astype(BF16)
        dz_o[:, AW:2 * AW] = _dot(dk8v.astype(BF16), sh_r[...]).astype(BF16)
        dz_o[:, 2 * AW:3 * AW] = _dot(dv_ref[...].astype(BF16), sh_r[...]).astype(BF16)
        da = _dot_nt(dz_o[...], win_r[...])
        xv = x_ref[...]
        dxa, dg = _rms_bwd(da, xv, _rs(xv, D), gpre_r[...], D)
        dgpre_a[...] += dg
        dx_o[...] = dh1_ref[...] + dxa

        @pl.when(pl.program_id(0) == T // tm - 1)
        def _():
            dsb_a[...] = _dot01_r(dbs_a[...], gsel_r[...])

    outs = _row_call(
        "pre_attn_bwd", body, T, tm, [x, dh1, dq8, dk8, dv8, flog, zuv, dysg],
        [gpre, win, lng, lnb, wm, wmt, bsg, gsg, gsel, shrink, pick64],
        [(D, F32), (ZW, BF16)],
        [(1, D), (1, 128), (1, AW), (1, AW), (1, AW), (8, CH, CH), (CH, AW), (CH, 128)],
        scratch=[(1, 128), (tm, AW), (tm, AW)], reverse=True, vmem_mb=48)
    return outs


def _matmul_tn(name, a, b, tn=512, tt=2048, shards=1):
    T, K = a.shape
    N = b.shape[1]
    tk = min(K, 1024)
    tn = min(tn, N // shards)
    tt = min(tt, T)
    nj = N // shards // tn

    def body(a_ref, b_ref, o_ref):
        @pl.when(pl.program_id(2) == 0)
        def _():
            o_ref[...] = jnp.zeros(o_ref.shape, F32)

        o_ref[...] += _dot_tn(a_ref[...].astype(BF16), b_ref[...].astype(BF16))

    if shards == 1:
        out_shape = jax.ShapeDtypeStruct((K, N), F32)
        out_spec = pl.BlockSpec((tk, tn), lambda i, j, t: (i, j))
    else:
        out_shape = jax.ShapeDtypeStruct((shards, K, N // shards), F32)
        out_spec = pl.BlockSpec((None, tk, tn), lambda i, j, t: (j // nj, i, j % nj))
    return pl.pallas_call(
        body, name=name, grid=(K // tk, N // tn, T // tt),
        in_specs=[pl.BlockSpec((tt, tk), lambda i, j, t: (t, i)),
                  pl.BlockSpec((tt, tn), lambda i, j, t: (t, j))],
        out_specs=out_spec, out_shape=out_shape,
        compiler_params=_params(40, ("arbitrary", "arbitrary", "arbitrary")),
    )(a, b)


def _me():
    return lax.axis_index("x"), lax.axis_index("y"), lax.axis_index("c")


HBM_SPEC = pl.BlockSpec(memory_space=pltpu.HBM)


def _gather_weights(mine):
    half = mine.shape[0] // 2

    def body(mine_ref, out_ref, ici_send, ici_recv, d2d_send, d2d_recv):
        x, y, c = _me()
        k_me = 2 * x + y
        chips = [(1 - x, y), (x, 1 - y), (1 - x, 1 - y)]
        my_rows = pl.ds(pl.multiple_of(c * half, 16), half)
        sib_rows = pl.ds(pl.multiple_of((1 - c) * half, 16), half)

        def over_ici(j, k, to):
            src = mine_ref.at[my_rows] if k is None else out_ref.at[k, my_rows]
            return pltpu.make_async_remote_copy(
                src_ref=src, dst_ref=out_ref.at[k_me if k is None else k, my_rows], send_sem=ici_send.at[j],
                recv_sem=ici_recv.at[j], device_id=to, device_id_type=MESH)

        def over_d2d(j, k, rows):
            return pltpu.make_async_remote_copy(
                src_ref=out_ref.at[k, rows], dst_ref=out_ref.at[k, rows], send_sem=d2d_send.at[j],
                recv_sem=d2d_recv.at[j], device_id=(x, y, 1 - c), device_id_type=MESH)

        first = [over_ici(j, None, (cx, cy, c)) for j, (cx, cy) in enumerate(chips)]
        for cp in first:
            cp.start()
        passed = [over_d2d(j, 2 * cx + cy, my_rows) for j, (cx, cy) in enumerate(chips)]
        for j, (cx, cy) in enumerate(chips):
            over_ici(j, 2 * cx + cy, (cx, cy, c)).wait_recv()
            passed[j].start()
        for j, (cx, cy) in enumerate(chips):
            over_d2d(j, 2 * cx + cy, sib_rows).wait_recv()
        for cp in first + passed:
            cp.wait_send()

    return pl.pallas_call(
        body, name="gather_weights", in_specs=[HBM_SPEC], out_specs=HBM_SPEC,
        out_shape=jax.ShapeDtypeStruct((4,) + mine.shape, mine.dtype),
        scratch_shapes=[pltpu.SemaphoreType.DMA((3,)), pltpu.SemaphoreType.DMA((3,)), pltpu.SemaphoreType.DMA((3,)),
                        pltpu.SemaphoreType.DMA((3,))],
    )(mine)


SEM_SPEC = pl.BlockSpec(memory_space=pltpu.SEMAPHORE)
EFFECT = pltpu.SideEffectType.DATAFLOW_SIDE_EFFECTING


def _gather_late_start(mine):
    def body(mine_ref, land_ref, send_sems, recv_sems, mine_thru, land_thru, token):
        x, y, c = _me()
        k_me = 2 * x + y
        for j, (cx, cy) in enumerate([(1 - x, y), (x, 1 - y), (1 - x, 1 - y)]):
            pltpu.make_async_remote_copy(
                src_ref=mine_ref, dst_ref=land_ref.at[k_me], send_sem=send_sems.at[j], recv_sem=recv_sems.at[j],
                device_id=(cx, cy, c), device_id_type=MESH).start()
        token[...] = jnp.zeros(token.shape, F32)

    land = lax.empty((4,) + mine.shape, mine.dtype)
    return pl.pallas_call(
        body, name="gather_late_start",
        out_shape=(pltpu.SemaphoreType.DMA((3,)), pltpu.SemaphoreType.DMA((3,)), pltpu.HBM(mine.shape, mine.dtype),
                   pltpu.HBM(land.shape, land.dtype), jax.ShapeDtypeStruct((8, 128), F32)),
        in_specs=(HBM_SPEC, HBM_SPEC),
        out_specs=(SEM_SPEC, SEM_SPEC, HBM_SPEC, HBM_SPEC, pl.BlockSpec(memory_space=pltpu.VMEM)),
        input_output_aliases={0: 2, 1: 3},
        compiler_params=pltpu.CompilerParams(has_side_effects=EFFECT),
    )(pltpu.with_memory_space_constraint(mine, pltpu.HBM), pltpu.with_memory_space_constraint(land, pltpu.HBM))


def _gather_late_wait(send_sems, recv_sems, mine_thru, land_thru, after):
    def body(mine_ref, land_ref, send_sems, recv_sems, after_ref, mine_dead, got_ref):
        x, y, c = _me()
        for j, (cx, cy) in enumerate([(1 - x, y), (x, 1 - y), (1 - x, 1 - y)]):
            cp = pltpu.make_async_remote_copy(
                src_ref=mine_ref, dst_ref=land_ref.at[2 * cx + cy], send_sem=send_sems.at[j],
                recv_sem=recv_sems.at[j], device_id=(cx, cy, c), device_id_type=MESH)
            cp.wait_send()
            cp.wait_recv()

    return pl.pallas_call(
        body, name="gather_late_wait",
        out_shape=(pltpu.HBM(mine_thru.shape, mine_thru.dtype), pltpu.HBM(land_thru.shape, land_thru.dtype)),
        in_specs=(HBM_SPEC, HBM_SPEC, SEM_SPEC, SEM_SPEC, pl.BlockSpec(memory_space=pl.ANY)),
        out_specs=(HBM_SPEC, HBM_SPEC), input_output_aliases={0: 0, 1: 1},
        compiler_params=pltpu.CompilerParams(has_side_effects=EFFECT),
    )(mine_thru, land_thru, send_sems, recv_sems, after)[1]


def _swap_halves(gs):
    n = len(gs)

    def body(*refs):
        g_refs, got_refs, send_sems, recv_sems = refs[:n], refs[n:2 * n], refs[2 * n], refs[2 * n + 1]
        x, y, c = _me()
        cps = []
        for i, (g_ref, got_ref) in enumerate(zip(g_refs, got_refs)):
            half = g_ref.shape[1] // 2
            theirs = pl.multiple_of((1 - c) * half, 16)
            cps.append(pltpu.make_async_remote_copy(
                src_ref=g_ref.at[:, pl.ds(theirs, half), :], dst_ref=got_ref, send_sem=send_sems.at[i],
                recv_sem=recv_sems.at[i], device_id=(x, y, 1 - c), device_id_type=MESH))
        for cp in cps:
            cp.start()
        for cp in cps:
            cp.wait()

    return pl.pallas_call(
        body, name="swap_halves", in_specs=[HBM_SPEC] * n, out_specs=[HBM_SPEC] * n,
        out_shape=[jax.ShapeDtypeStruct((4, g.shape[1] // 2, g.shape[2]), F32) for g in gs],
        scratch_shapes=[pltpu.SemaphoreType.DMA((n,)), pltpu.SemaphoreType.DMA((n,))],
    )(*gs)


def _pair_sum(name, c1, g, got):
    half, cols = got.shape[1], got.shape[2]

    def body(c_ref, a_ref, b_ref, o_ref):
        o_ref[...] = (a_ref[...] + b_ref[...]).astype(BF16)

    return pl.pallas_call(
        body, name="pair_sum_" + name,
        grid_spec=pltpu.PrefetchScalarGridSpec(
            num_scalar_prefetch=1, grid=(4,),
            in_specs=[pl.BlockSpec((1, half, cols), lambda k, c_ref: (k, c_ref[0], 0)),
                      pl.BlockSpec((1, half, cols), lambda k, c_ref: (k, 0, 0))],
            out_specs=pl.BlockSpec((1, half, cols), lambda k, c_ref: (k, 0, 0))),
        out_shape=jax.ShapeDtypeStruct(got.shape, BF16), compiler_params=_params(32),
    )(c1, g, got)


def _exchange_chips(pss):
    n = len(pss)

    def body(*refs):
        ps_refs, out_refs = refs[:n], refs[n:2 * n]
        send_sems, recv_sems, local_sems = refs[2 * n:]
        x, y, c = _me()
        k_me = 2 * x + y
        chips = [(1 - x, y), (x, 1 - y), (1 - x, 1 - y)]
        owns, sends = [], []
        for i, (ps_ref, out_ref) in enumerate(zip(ps_refs, out_refs)):
            owns.append(pltpu.make_async_copy(ps_ref.at[k_me], out_ref.at[k_me], local_sems.at[i]))
            for j, (cx, cy) in enumerate(chips):
                sends.append(pltpu.make_async_remote_copy(
                    src_ref=ps_ref.at[2 * cx + cy], dst_ref=out_ref.at[k_me], send_sem=send_sems.at[3 * i + j],
                    recv_sem=recv_sems.at[3 * i + j], device_id=(cx, cy, c), device_id_type=MESH))
        for cp in owns + sends:
            cp.start()
        for i, (ps_ref, out_ref) in enumerate(zip(ps_refs, out_refs)):
            for j, (cx, cy) in enumerate(chips):
                pltpu.make_async_remote_copy(
                    src_ref=ps_ref.at[k_me], dst_ref=out_ref.at[2 * cx + cy], send_sem=send_sems.at[3 * i + j],
                    recv_sem=recv_sems.at[3 * i + j], device_id=(cx, cy, c), device_id_type=MESH).wait_recv()
        for cp in sends:
            cp.wait_send()
        for cp in owns:
            cp.wait()

    return pl.pallas_call(
        body, name="exchange_chips", in_specs=[HBM_SPEC] * n, out_specs=[HBM_SPEC] * n,
        out_shape=[jax.ShapeDtypeStruct(ps.shape, ps.dtype) for ps in pss],
        scratch_shapes=[pltpu.SemaphoreType.DMA((3 * n,)), pltpu.SemaphoreType.DMA((3 * n,)),
                        pltpu.SemaphoreType.DMA((n,))],
    )(*pss)


def _adamw(w, g, m, v):
    m = B1 * m + (1.0 - B1) * g
    v = B2 * v + (1.0 - B2) * (g * g)
    delta = -LR * ((m / BC1) / (jnp.sqrt(v / BC2) + AEPS) + WD * w)
    return delta, m, v


def _reduce_chips(name, parts):
    half, cols = parts.shape[1], parts.shape[2]

    def body(p_ref, o_ref):
        f = lambda k: p_ref[k].astype(F32)
        o_ref[...] = ((f(0) + f(1)) + f(2)) + f(3)

    return pl.pallas_call(
        body, name="reduce_chips_" + name, grid=(1,),
        in_specs=[pl.BlockSpec((4, half, cols), lambda i: (0, 0, 0))],
        out_specs=pl.BlockSpec((half, cols), lambda i: (0, 0)),
        out_shape=jax.ShapeDtypeStruct((half, cols), F32), compiler_params=_params(32),
    )(parts)


def _share_grad(ghs):
    n = len(ghs)

    def body(*refs):
        g_refs, got_refs, send_sems, recv_sems = refs[:n], refs[n:2 * n], refs[2 * n], refs[2 * n + 1]
        x, y, c = _me()
        cps = [pltpu.make_async_remote_copy(
            src_ref=g_ref, dst_ref=got_ref, send_sem=send_sems.at[i], recv_sem=recv_sems.at[i],
            device_id=(x, y, 1 - c), device_id_type=MESH) for i, (g_ref, got_ref) in enumerate(zip(g_refs, got_refs))]
        for cp in cps:
            cp.start()
        for cp in cps:
            cp.wait()

    return pl.pallas_call(
        body, name="share_grad", in_specs=[HBM_SPEC] * n, out_specs=[HBM_SPEC] * n,
        out_shape=[jax.ShapeDtypeStruct(g.shape, F32) for g in ghs],
        scratch_shapes=[pltpu.SemaphoreType.DMA((n,)), pltpu.SemaphoreType.DMA((n,))],
    )(*ghs)


def _update(name, c1, gh, got, w, m, v):
    half, cols = gh.shape

    def body(c_ref, gh_ref, got_ref, w_ref, m_ref, v_ref, g_o, d_o, m_o, v_o):
        g = jnp.where(pl.program_id(0) == c_ref[0], gh_ref[...], got_ref[...])
        delta, mn, vn = _adamw(w_ref[...], g, m_ref[...], v_ref[...])
        g_o[...] = g
        d_o[...] = delta
        m_o[...] = mn
        v_o[...] = vn

    same = pl.BlockSpec((half, cols), lambda h, c_ref: (0, 0))
    rows = pl.BlockSpec((half, cols), lambda h, c_ref: (h, 0))
    return pl.pallas_call(
        body, name="update_" + name,
        grid_spec=pltpu.PrefetchScalarGridSpec(
            num_scalar_prefetch=1, grid=(2,), in_specs=[same, same, rows, rows, rows],
            out_specs=[rows, rows, rows, rows]),
        out_shape=[jax.ShapeDtypeStruct(w.shape, F32)] * 4, compiler_params=_params(40),
    )(c1, gh, got, w, m, v)


SMALL_NAMES = ("sg_w",) + VEC_NAMES
VEC_ROWS = 24
VEC_ROW = {"f_bias": 0, "sg_ln_g": 1, "sg_ln_b": 2, "att_out_g": 3, "sg_out_g": 4, "pre_mix_g": 5,
           "post_mix_g": 6, "pre_ffn_g": 7, "sg_b": 8, "post_ffn_g": 16, "ple_gate_b": 17}
LOSS_VEC_ROW = 18


def _small_allreduce_update(g, w, m, v, loss_l):
    n = len(SMALL_NAMES)

    def body(*refs):
        g_r = dict(zip(SMALL_NAMES, refs[0:n]))
        w_r = dict(zip(SMALL_NAMES, refs[n:2 * n]))
        m_r = dict(zip(SMALL_NAMES, refs[2 * n:3 * n]))
        v_r = dict(zip(SMALL_NAMES, refs[3 * n:4 * n]))
        loss_r = refs[4 * n]
        loss_o = refs[4 * n + 1]
        outs = refs[4 * n + 2:8 * n + 2]
        bufv, bufw, send_sems, recv_sems = refs[8 * n + 2:]
        x, y, c = _me()
        me = 4 * x + 2 * y + c
        bufv[me] = jnp.zeros((VEC_ROWS, 1024), F32)
        for name in VEC_NAMES:
            val = g_r[name][...]
            bufv[me, pl.ds(VEC_ROW[name], val.shape[0]), pl.ds(0, val.shape[1])] = val
        bufv[me, pl.ds(LOSS_VEC_ROW, 1), :] = loss_r[...] * (0.5 / D)
        rr = lax.broadcasted_iota(jnp.int32, (CH, CH), 0)
        cc = lax.broadcasted_iota(jnp.int32, (CH, CH), 1)
        bufw[me] = jnp.where((cc <= rr)[None], g_r["sg_w"][...], 0.0)

        rels = [(rx, ry, rc) for rx in (0, 1) for ry in (0, 1) for rc in (0, 1)][1:]

        def peer(r):
            return ((x + r[0]) % 2, (y + r[1]) % 2, (c + r[2]) % 2)

        def copies(j, slot, to):
            return [pltpu.make_async_remote_copy(
                src_ref=buf.at[slot], dst_ref=buf.at[slot], send_sem=send_sems.at[2 * j + i],
                recv_sem=recv_sems.at[2 * j + i], device_id=to, device_id_type=MESH)
                for i, buf in enumerate((bufv, bufw))]

        sends = [cp for j, r in enumerate(rels) for cp in copies(j, me, peer(r))]
        for cp in sends:
            cp.start()
        for j, r in enumerate(rels):
            px, py, pc = peer(r)
            for cp in copies(j, 4 * px + 2 * py + pc, peer(r)):
                cp.wait_recv()
        for cp in sends:
            cp.wait_send()

        tot_v = bufv[0]
        tot_w = bufw[0]
        for d in range(1, 8):
            tot_v = tot_v + bufv[d]
            tot_w = tot_w + bufw[d]
        loss_o[...] = jnp.sum(tot_v[LOSS_VEC_ROW:LOSS_VEC_ROW + 1, :], axis=-1, keepdims=True) + jnp.zeros((1, 128), F32)
        for i, name in enumerate(SMALL_NAMES):
            if name == "sg_w":
                gt = tot_w
            else:
                rows, width = w_r[name].shape
                gt = tot_v[VEC_ROW[name]:VEC_ROW[name] + rows, 0:width]
            delta, mn, vn = _adamw(w_r[name][...], gt, m_r[name][...], v_r[name][...])
            outs[4 * i][...] = gt
            outs[4 * i + 1][...] = delta
            outs[4 * i + 2][...] = mn
            outs[4 * i + 3][...] = vn

    vm = pl.BlockSpec(memory_space=pltpu.VMEM)
    args = [d[k] for d in (g, w, m, v) for k in SMALL_NAMES] + [loss_l]
    out_shape = [jax.ShapeDtypeStruct((1, 128), F32)]
    out_shape += [jax.ShapeDtypeStruct(w[k].shape, F32) for k in SMALL_NAMES for _ in range(4)]
    res = pl.pallas_call(
        body, name="small_allreduce_update", in_specs=[vm] * len(args), out_specs=[vm] * len(out_shape),
        out_shape=out_shape,
        scratch_shapes=[pltpu.VMEM((8, VEC_ROWS, 1024), F32), pltpu.VMEM((8, 8, CH, CH), F32),
                        pltpu.SemaphoreType.DMA((14,)), pltpu.SemaphoreType.DMA((14,))],
        compiler_params=pltpu.CompilerParams(vmem_limit_bytes=32 * 1024 * 1024),
    )(*args)
    return res[0], {k: res[1 + 4 * i:5 + 4 * i] for i, k in enumerate(SMALL_NAMES)}


def _win_kernel_order(gathered):
    w_in = jnp.concatenate([gathered[k].reshape(D, 768)[:, :642] for k in range(4)], axis=1)
    return jnp.concatenate([w_in[:, :3 * AW], w_in[:, 3 * AW + NH:], w_in[:, 3 * AW:3 * AW + NH],
                            jnp.zeros((D, 128 - NH), w_in.dtype)], axis=1)


LATE_ROWS = 256 + 1024 + 1024 + 64 + 256


def _pack_late(w_out, w1, w2, plew, wg):
    return jnp.concatenate([w_out, w1, w2, plew.reshape(64, 1024), wg], axis=0)


def _unpack_late(gathered):
    cols = lambda t: jnp.swapaxes(t, 0, 1).reshape(t.shape[1], 4 * t.shape[2])
    return (gathered[:, 0:256].reshape(D, D), cols(gathered[:, 256:1280]), gathered[:, 1280:2304].reshape(DFF, D),
            cols(gathered[:, 2304:2368].reshape(4, 256, 256)), gathered[:, 2368:2624].reshape(D, D))


def _local_step(x, p, tgt, win_k, late_weights, token, small):
    T = x.shape[0]
    row = lambda n: small[n].reshape(1, -1)
    fbias = jnp.pad(row("f_bias"), ((0, 0), (0, 128 - NH))) + token[0:1, :]
    wm = _masked_sg_w(small["sg_w"].reshape(8, CH, CH))
    wmb = wm.astype(BF16)
    wmt = jnp.swapaxes(wm, 1, 2).astype(BF16)
    bsg = jnp.repeat(small["sg_b"].reshape(8, CH).T, DH, axis=1)
    ln_g, ln_b, gsg, gatt = row("sg_ln_g"), row("sg_ln_b"), row("sg_out_g"), row("att_out_g")
    gpre, gpm, gpf, gpff, bg = row("pre_mix_g"), row("post_mix_g"), row("pre_ffn_g"), row("post_ffn_g"), row("ple_gate_b")
    gsel = (jnp.arange(AW)[:, None] // DH == jnp.arange(128)[None, :]).astype(BF16)
    hsel = (jnp.arange(AW)[:, None] // DH == jnp.arange(AW)[None, :] // DH).astype(BF16)

    expand, shrink, pieces, qconst, _, pick64 = _head_consts()
    a, qkv, flog, ccol, zuv, ysgn, q8, k8, v8 = _pre_attn_fwd(
        x, gpre, win_k, fbias, ln_g, ln_b, wmb, bsg, gsg, expand, pieces, qconst)

    nt = T // TQ
    hd = lambda t, i: t[:, i * AW:(i + 1) * AW].reshape(T, NH, DH)
    zpad = lambda n: jnp.zeros((T, NH, n), BF16)
    one = jnp.ones((T, NH, 1), BF16)
    wide = lambda parts: jnp.concatenate(parts, axis=-1).reshape(T, NH * 128)
    slabs = lambda t: jnp.swapaxes(t.reshape(nt, TQ, NH * 128), 1, 2)
    qt8 = slabs(q8)
    kt8 = slabs(wide([hd(qkv, 1), one, zpad(63)]))
    vt8 = slabs(wide([hd(qkv, 2), one, zpad(63)]))
    lanes = jnp.arange(128)
    sel = jnp.stack([((lanes[:, None] == lanes[None, :] - DH * j) & (lanes[:, None] < DH)).astype(BF16)
                     for j in (0, 1)])

    yatt, lse = _flash_fwd(qt8, k8, vt8, sel)
    wout, w1, w2, plew, wg = late_weights(lse)
    y, ov, h1, c2, sact, rr = _tail_fwd1(x, yatt, ysgn, gatt, wout, gpm, gpf, w1)
    ff, h2b, de, dpre, dh2, loss_l, dbg = _tail_fwd2(sact, h1, p, tgt, w2, gpff, wg, bg, plew)
    dff, dr, do, do8, dlt, dysg, dh1, dgpff, dgpf, dgpm, dgatt = _tail_bwd(
        dh2, ff, rr, h1, ov, yatt, w2, w1, wout, gpff, gpf, gpm, gatt, hsel, expand)
    dlt4 = jnp.pad(dlt[:, ::DH].T.reshape(4, 2, T), ((0, 0), (0, 6), (0, 0)))
    dqt, dk8, dv8 = _flash_bwd(q8, qt8, k8, kt8, v8, do8, slabs(do8), lse, dlt4)
    dx, dz, dgpre, dfb, dgsg, dlng, dlnb, dws, _, dsbt = _pre_attn_bwd(
        x, dh1, jnp.swapaxes(dqt, 1, 2).reshape(T, NH * 128), dk8, dv8, flog, zuv, dysg,
        gpre, win_k, ln_g, ln_b, wmb, wmt, bsg, gsg, gsel, shrink, pick64)

    dwin_k = _matmul_tn("grad_w_in", a, dz, tn=384)
    dwout = _matmul_tn("grad_w_out", y, do)
    dw1 = _matmul_tn("grad_w_ff1", c2, dr, shards=4)
    dw2 = _matmul_tn("grad_w_ff2", sact, dff)
    dwg = _matmul_tn("grad_ple_gate_w", h2b, dpre)
    dplew = _matmul_tn("grad_ple_w", p, de, tn=256, shards=4)

    dsb = dsbt[:, :8].T
    gsmall = {"sg_w": dws, "f_bias": dfb, "sg_ln_g": dlng, "sg_ln_b": dlnb, "sg_b": dsb,
              "att_out_g": dgatt, "sg_out_g": dgsg, "pre_mix_g": dgpre, "post_mix_g": dgpm, "pre_ffn_g": dgpf,
              "post_ffn_g": dgpff, "ple_gate_b": dbg}
    return loss_l, dx, (dwin_k, dwout, dw1, dw2, dplew, dwg), gsmall


def kernel(x, p, w_in, f_bias, sg_ln_g, sg_ln_b, sg_w, sg_b, att_out_g, sg_out_g, w_out, pre_mix_g, post_mix_g, pre_ffn_g, post_ffn_g, w_ff1, w_ff2, ple_w, ple_gate_w, ple_gate_b, loss_target, m_w_in, m_f_bias, m_sg_ln_g, m_sg_ln_b, m_sg_w, m_sg_b, m_att_out_g, m_sg_out_g, m_w_out, m_pre_mix_g, m_post_mix_g, m_pre_ffn_g, m_post_ffn_g, m_w_ff1, m_w_ff2, m_ple_w, m_ple_gate_w, m_ple_gate_b, v_w_in, v_f_bias, v_sg_ln_g, v_sg_ln_b, v_sg_w, v_sg_b, v_att_out_g, v_sg_out_g, v_w_out, v_pre_mix_g, v_post_mix_g, v_pre_ffn_g, v_post_ffn_g, v_w_ff1, v_w_ff2, v_ple_w, v_ple_gate_w, v_ple_gate_b):
    c = lax.axis_index("c")
    big = lambda t: (t[0][0], t[1][0], t[2][0], t[3][0], t[4][0], t[5][0])
    w_big = big((w_in, w_out, w_ff1, w_ff2, ple_w, ple_gate_w))
    m_big = big((m_w_in, m_w_out, m_w_ff1, m_w_ff2, m_ple_w, m_ple_gate_w))
    v_big = big((v_w_in, v_w_out, v_w_ff1, v_w_ff2, v_ple_w, v_ple_gate_w))
    small = {"sg_w": sg_w, "f_bias": f_bias, "sg_ln_g": sg_ln_g, "sg_ln_b": sg_ln_b, "sg_b": sg_b,
             "att_out_g": att_out_g, "sg_out_g": sg_out_g, "pre_mix_g": pre_mix_g, "post_mix_g": post_mix_g,
             "pre_ffn_g": pre_ffn_g, "post_ffn_g": post_ffn_g, "ple_gate_b": ple_gate_b}
    m_small = {"sg_w": m_sg_w, "f_bias": m_f_bias, "sg_ln_g": m_sg_ln_g, "sg_ln_b": m_sg_ln_b, "sg_b": m_sg_b,
               "att_out_g": m_att_out_g, "sg_out_g": m_sg_out_g, "pre_mix_g": m_pre_mix_g,
               "post_mix_g": m_post_mix_g, "pre_ffn_g": m_pre_ffn_g, "post_ffn_g": m_post_ffn_g,
               "ple_gate_b": m_ple_gate_b}
    v_small = {"sg_w": v_sg_w, "f_bias": v_f_bias, "sg_ln_g": v_sg_ln_g, "sg_ln_b": v_sg_ln_b, "sg_b": v_sg_b,
               "att_out_g": v_att_out_g, "sg_out_g": v_sg_out_g, "pre_mix_g": v_pre_mix_g,
               "post_mix_g": v_post_mix_g, "pre_ffn_g": v_pre_ffn_g, "post_ffn_g": v_post_ffn_g,
               "ple_gate_b": v_ple_gate_b}

    k_me = 2 * lax.axis_index("x") + lax.axis_index("y")
    own_slot = lambda got, mine: lax.dynamic_update_slice(got, mine[None], (k_me, 0, 0))
    late_mine = _pack_late(*w_big[1:]).astype(BF16)
    late = _gather_late_start(late_mine)
    win_mine = jnp.pad(w_big[0], ((0, 0), (0, 768 - 642))).reshape(768, 1024).astype(BF16)
    win_k = _win_kernel_order(own_slot(_gather_weights(win_mine), win_mine))
    late_weights = lambda after: _unpack_late(
        own_slot(_gather_late_wait(late[0], late[1], late[2], late[3], after), late_mine))

    loss_l, dx, gbig, gsmall = _local_step(x[0], p[0, 0], loss_target[0], win_k, late_weights, late[4], small)

    dwin_k, dwout, dw1, dw2, dplew, dwg = gbig
    dwin = jnp.concatenate([dwin_k[:, :3 * AW], dwin_k[:, 5 * AW:5 * AW + NH], dwin_k[:, 3 * AW:5 * AW]], axis=1)
    dwin = jnp.pad(jnp.swapaxes(dwin.reshape(D, 4, 642), 0, 1), ((0, 0), (0, 0), (0, 768 - 642)))
    names = ("w_in", "w_out", "w_ff1", "w_ff2", "ple_w", "ple_gate_w")
    gs = [dwin, dwout.reshape(4, 256, D), dw1, dw2.reshape(4, D, D), dplew, dwg.reshape(4, 256, D)]
    padded = lambda t: (jnp.pad(t[0], ((0, 0), (0, 768 - 642))),) + tuple(t[1:])
    c1 = jnp.reshape(c, (1,)).astype(jnp.int32)
    gots = _swap_halves(gs)
    parts = _exchange_chips([_pair_sum(nm, c1, g, got) for nm, g, got in zip(names, gs, gots)])
    ghs = [_reduce_chips(nm, pt) for nm, pt in zip(names, parts)]
    got2 = _share_grad(ghs)
    big_out = [_update(nm, c1, gh, g2, w, m, v) for nm, gh, g2, w, m, v in
               zip(names, ghs, got2, padded(w_big), padded(m_big), padded(v_big))]
    big_out = [[big_out[j][i][:, :642] if j == 0 else big_out[j][i] for j in range(6)] for i in range(4)]

    view = lambda t: t.reshape(t.shape[-3:]) if t.ndim == 4 else t.reshape(t.shape[-2:])
    views = lambda d: {k: view(d[k]) for k in SMALL_NAMES}
    loss11, res_s = _small_allreduce_update(gsmall, views(small), views(m_small), views(v_small), loss_l)
    loss = loss11[0, 0]

    def small_out(i, name):
        return res_s[name][i].reshape(small[name].shape)

    order = ["w_in", "f_bias", "sg_ln_g", "sg_ln_b", "sg_w", "sg_b", "att_out_g", "sg_out_g", "w_out",
             "pre_mix_g", "post_mix_g", "pre_ffn_g", "post_ffn_g", "w_ff1", "w_ff2", "ple_w", "ple_gate_w",
             "ple_gate_b"]
    big_idx = {"w_in": 0, "w_out": 1, "w_ff1": 2, "w_ff2": 3, "ple_w": 4, "ple_gate_w": 5}
    outs = [loss, dx[None]]
    for i in range(4):
        for name in order:
            if name in big_idx:
                outs.append(big_out[i][big_idx[name]][None])
            else:
                outs.append(small_out(i, name))
    return tuple(outs)
```

```python
import math

import jax
import jax.numpy as jnp
from jax import lax
from jax.experimental import pallas as pl
from jax.experimental.pallas import tpu as pltpu

F32 = jnp.float32
BF16 = jnp.bfloat16
MESH = pl.DeviceIdType.MESH

D = 1024
DH = 64
NH = 8
AW = 512
CH = 128
DFF = 4096
ZW = 5 * AW + 128
EPS = 1e-6
NEG = -1e30
MASKED = -2e30

TM = 256
TQ = 256

LR, B1, B2, AEPS, WD, STEP = 0.001, 0.9, 0.999, 1e-08, 0.01, 10
BC1 = 1.0 - B1 ** STEP
BC2 = 1.0 - B2 ** STEP

VEC_NAMES = ("f_bias", "sg_ln_g", "sg_ln_b", "sg_b", "att_out_g", "sg_out_g", "pre_mix_g",
             "post_mix_g", "pre_ffn_g", "post_ffn_g", "ple_gate_b")


def _dot(a, b):
    return jnp.dot(a, b, preferred_element_type=F32)


def _dot_nt(a, b):
    return lax.dot_general(a, b, (((1,), (1,)), ((), ())), preferred_element_type=F32)


def _dot_tn(a, b):
    return lax.dot_general(a, b, (((0,), (0,)), ((), ())), preferred_element_type=F32)


def _split3(x):
    h = x.astype(BF16)
    r = x - h.astype(F32)
    m = r.astype(BF16)
    l = (r - m.astype(F32)).astype(BF16)
    return h, m, l


def _dot01(sel, x):
    h, m, l = _split3(x)
    return _dot(sel, h) + _dot(sel, m) + _dot(sel, l)


def _dot01_r(x, sel):
    h, m, l = _split3(x)
    return _dot(h, sel) + _dot(m, sel) + _dot(l, sel)


def _dot01_tn(x, sel):
    h, m, l = _split3(x)
    return _dot_tn(h, sel) + _dot_tn(m, sel) + _dot_tn(l, sel)


def _rs(x, n):
    return lax.rsqrt(jnp.sum(x * x, axis=-1, keepdims=True) * (1.0 / n) + EPS)


def _rms_bwd(dn, x, rs, g, n):
    w = dn * g
    dx = rs * w - x * ((rs * rs * rs) * (1.0 / n) * jnp.sum(w * x, axis=-1, keepdims=True))
    return dx, jnp.sum(dn * x * rs, axis=0, keepdims=True)


_GC = math.sqrt(2.0 / math.pi)


def _gelu(x):
    t = jnp.tanh(_GC * (x + 0.044715 * x * x * x))
    return 0.5 * x * (1.0 + t), t


def _gelu_grad(x, t):
    return 0.5 * (1.0 + t) + 0.5 * x * (1.0 - t * t) * (_GC * (1.0 + 3.0 * 0.044715 * x * x))


def _params(vmem_mb, sem=("arbitrary",)):
    return pltpu.CompilerParams(dimension_semantics=sem, vmem_limit_bytes=vmem_mb * 1024 * 1024)


def _row_call(name, body, T, tm, tiled, resident, outs, accs, scratch=(), reverse=False, vmem_mb=48):
    nt = T // tm
    n_t, n_r, n_o, n_a = len(tiled), len(resident), len(outs), len(accs)

    def kern(*refs):
        t_refs = refs[:n_t]
        r_hbm = refs[n_t:n_t + n_r]
        o_refs = refs[n_t + n_r:n_t + n_r + n_o]
        a_refs = refs[n_t + n_r + n_o:n_t + n_r + n_o + n_a]
        r_vmem = refs[n_t + n_r + n_o + n_a:n_t + 2 * n_r + n_o + n_a]
        s_refs = refs[n_t + 2 * n_r + n_o + n_a:]

        @pl.when(pl.program_id(0) == 0)
        def _():
            for h, v in zip(r_hbm, r_vmem):
                pltpu.sync_copy(h, v)
            for a in a_refs + s_refs:
                a[...] = jnp.zeros(a.shape, a.dtype)

        body(t_refs, r_vmem, o_refs, a_refs, s_refs)

    if reverse:
        idx = lambda i: (nt - 1 - i, 0)
        idx_t = lambda i: (nt - 1 - i, 0, 0)
    else:
        idx = lambda i: (i, 0)
        idx_t = lambda i: (i, 0, 0)
    arrays, in_specs = [], []
    for a in tiled:
        if isinstance(a, tuple):
            arrays.append(a[0])
            in_specs.append(pl.BlockSpec((None, a[0].shape[1], tm), idx_t))
        else:
            arrays.append(a)
            in_specs.append(pl.BlockSpec((tm, a.shape[1]), idx))
    in_specs += [pl.BlockSpec(memory_space=pl.ANY) for _ in resident]
    out_shape, out_specs = [], []
    for o in outs:
        if len(o) == 3:
            out_shape.append(jax.ShapeDtypeStruct((nt, o[0], tm), o[1]))
            out_specs.append(pl.BlockSpec((None, o[0], tm), idx_t))
        else:
            out_shape.append(jax.ShapeDtypeStruct((T, o[0]), o[1]))
            out_specs.append(pl.BlockSpec((tm, o[0]), idx))
    out_shape += [jax.ShapeDtypeStruct(s, F32) for s in accs]
    out_specs += [pl.BlockSpec(s, lambda i, n=len(s): (0,) * n) for s in accs]
    scratch_shapes = [pltpu.VMEM(r.shape, r.dtype) for r in resident]
    scratch_shapes += [pltpu.VMEM(s, F32) for s in scratch]
    return pl.pallas_call(
        kern, name=name, grid=(nt,), in_specs=in_specs, out_specs=out_specs, out_shape=out_shape,
        scratch_shapes=scratch_shapes, compiler_params=_params(vmem_mb),
    )(*arrays, *resident)


def _sg_forward(zu, zv, wm_ref, bsg, lng, lnb, mixed_ref, tm):
    gu, tu = _gelu(zu)
    vg, tv = _gelu(zv)
    mu = jnp.sum(vg, axis=-1, keepdims=True) * (1.0 / AW)
    xc = vg - mu
    rstd = lax.rsqrt(jnp.sum(xc * xc, axis=-1, keepdims=True) * (1.0 / AW) + EPS)
    xhat = xc * rstd
    vvb = (xhat * lng + lnb).astype(BF16)
    lane = lax.broadcasted_iota(jnp.int32, (CH, 128), 1)
    for c in range(tm // CH):
        for j in range(4):
            blk = vvb[c * CH:(c + 1) * CH, j * 128:(j + 1) * 128]
            m0 = _dot(wm_ref[2 * j], blk)
            m1 = _dot(wm_ref[2 * j + 1], blk)
            mixed_ref[c * CH:(c + 1) * CH, j * 128:(j + 1) * 128] = (
                jnp.where(lane < DH, m0, m1) + bsg[:, j * 128:(j + 1) * 128])
    return gu, tu, tv, xhat, rstd, vvb, mixed_ref[...]


def _head_consts():
    src = jnp.arange(AW)
    dst = (src // DH) * 128 + src % DH
    wide = jnp.arange(NH * 128)
    expand = (dst[:, None] == wide[None, :]).astype(BF16)
    heads = jnp.arange(128)
    pieces = jnp.stack([((heads[:, None] * 128 + DH + i == wide[None, :]) & (heads[:, None] < NH)).astype(BF16)
                        for i in range(3)])
    spare = wide % 128 - DH
    qconst = jnp.where((spare >= 0) & (spare < 3), -1.0, 0.0).astype(F32)[None, :]
    one64 = jnp.where(spare == 0, 1.0, 0.0).astype(F32)[None, :]
    pick64 = ((wide[:, None] == heads[None, :] * 128 + DH) & (heads[None, :] < NH)).astype(BF16)
    return expand, expand.T, pieces, qconst, one64, pick64


def _masked_sg_w(sg_w):
    r = lax.broadcasted_iota(jnp.int32, (CH, CH), 0)
    c = lax.broadcasted_iota(jnp.int32, (CH, CH), 1)
    return jnp.where((c <= r)[None], sg_w, 0.0)


def _pre_attn_fwd(x, gpre, win, fbias, lng, lnb, wm, bsg, gsg, expand, pieces, qconst):
    T = x.shape[0]
    tm = TM

    def body(t, r, o, a, s):
        (x_ref,) = t
        gpre_r, win_r, fb_r, lng_r, lnb_r, wm_r, bsg_r, gsg_r, ex_r, pc_r, qc_r = r
        a_o, qkv_o, flog_o, ccol_o, zuv_o, ysgn_o, q8_o, k8_o, v8_o = o
        carry_ref, mixed_ref = s
        xv = x_ref[...]
        av = (xv * _rs(xv, D) * gpre_r[...]).astype(BF16)
        a_o[...] = av
        z = _dot(av, win_r[...])
        zu = z[:, 3 * AW:4 * AW]
        zv = z[:, 4 * AW:5 * AW]
        zuv_o[:, 0:AW] = zu
        zuv_o[:, AW:2 * AW] = zv
        zf = z[:, 5 * AW:] + fb_r[...]
        flog_o[...] = zf
        lane = lax.broadcasted_iota(jnp.int32, (tm, 128), 1)
        logf = jnp.where(lane < NH, jnp.minimum(zf, 0.0) - jnp.log(1.0 + jnp.exp(-jnp.abs(zf))), 0.0)
        rr = lax.broadcasted_iota(jnp.int32, (tm, tm), 0)
        cc = lax.broadcasted_iota(jnp.int32, (tm, tm), 1)
        tri = (cc <= rr).astype(BF16)
        cum = _dot01(tri, logf) + carry_ref[...]
        carry_ref[...] = cum[tm - 1:tm, :]
        ccol_o[...] = cum
        ex = ex_r[...]
        q8_o[...] = (_dot((z[:, 0:AW] * (DH ** -0.5)).astype(BF16), ex) + qc_r[...]).astype(BF16)
        ch, cm, cl = _split3(cum)
        k8_o[...] = (_dot(z[:, AW:2 * AW].astype(BF16), ex) + _dot(ch, pc_r[0]) + _dot(cm, pc_r[1])
                     + _dot(cl, pc_r[2])).astype(BF16)
        v8_o[...] = _dot(z[:, 2 * AW:3 * AW].astype(BF16), ex).astype(BF16)
        qkv_o[:, 0:AW] = (z[:, 0:AW] * (DH ** -0.5)).astype(BF16)
        qkv_o[:, AW:3 * AW] = z[:, AW:3 * AW].astype(BF16)
        gu, _, _, _, _, _, mixed = _sg_forward(zu, zv, wm_r, bsg_r[...], lng_r[...], lnb_r[...], mixed_ref, tm)
        ysg = gu * mixed
        ysgn_o[...] = (ysg * _rs(ysg, AW) * gsg_r[...]).astype(BF16)

    return _row_call(
        "pre_attn_fwd", body, T, tm, [x], [gpre, win, fbias, lng, lnb, wm, bsg, gsg, expand, pieces, qconst],
        [(D, BF16), (3 * AW, BF16), (128, F32), (128, F32), (2 * AW, F32), (AW, BF16), (NH * 128, BF16),
         (NH * 128, BF16), (NH * 128, BF16)], [],
        scratch=[(1, 128), (tm, AW)], vmem_mb=48)


def _flash_fwd(qt8, k8, vt8, sel):
    T = k8.shape[0]
    nq = T // TQ

    def body(qt_ref, k_ref, vt_ref, sel_ref, o_ref, l_ref, u_scr, p_scr):
        qi = pl.program_id(1)
        qts = (qt_ref[0:128, :], qt_ref[128:256, :])
        dmat = (lax.broadcasted_iota(jnp.int32, (TQ, TQ), 0) - lax.broadcasted_iota(jnp.int32, (TQ, TQ), 1))
        u_scr[1] = jnp.full((2, TQ, TQ), MASKED, F32)
        p_scr[...] = jnp.zeros(p_scr.shape, BF16)

        def sub(t, carry, sc, sb, masked):
            blk_c = jnp.clip(t - 2, 0, qi)
            off_a = pl.multiple_of(jnp.minimum(t, qi) * TQ, TQ)
            new = []
            for j in (0, 1):
                m, al, acc = carry[j]
                acc = al * acc + _dot(vt_ref[blk_c, j * 128:(j + 1) * 128, :], p_scr[sc, j])
                m_new = jnp.maximum(m, jnp.max(u_scr[sb, j], axis=0, keepdims=True))
                p_scr[sb, j] = jnp.exp(u_scr[sb, j] - m_new).astype(BF16)
                u = _dot(k_ref[pl.ds(off_a, TQ), j * 128:(j + 1) * 128], qts[j])
                u_scr[sc, j] = jnp.where(dmat <= (qi - t) * TQ, u, MASKED) if masked else u
                new.append((m_new, jnp.exp(m - m_new), acc))
            return tuple(new)

        def pair(t2, carry, masked):
            return sub(2 * t2 + 1, sub(2 * t2, carry, 0, 1, masked), 1, 0, masked)

        init = tuple((jnp.full((1, TQ), NEG, F32), jnp.ones((1, TQ), F32), jnp.zeros((128, TQ), F32))
                     for _ in (0, 1))
        carry = lax.fori_loop(0, qi // 2, lambda t2, cr: pair(t2, cr, False), init)
        (m0, _, a0), (m1, _, a1) = pair(qi // 2 + 1, pair(qi // 2, carry, True), True)
        l0 = a0[DH:DH + 1, :]
        l1 = a1[DH:DH + 1, :]
        o_ref[...] = _dot01_tn(a0 * (1.0 / l0), sel_ref[0]) + _dot01_tn(a1 * (1.0 / l1), sel_ref[1])
        l_ref[0:1, :] = m0 + jnp.log(l0)
        l_ref[1:2, :] = m1 + jnp.log(l1)
        l_ref[2:8, :] = jnp.zeros((6, TQ), F32)

    return pl.pallas_call(
        body, name="flash_fwd", grid=(4, nq),
        in_specs=[pl.BlockSpec((None, 256, TQ), lambda h, i: (i, h, 0)),
                  pl.BlockSpec((T, 256), lambda h, i: (0, h)),
                  pl.BlockSpec((nq, 256, TQ), lambda h, i: (0, h, 0)),
                  pl.BlockSpec((2, 128, 128), lambda h, i: (0, 0, 0))],
        out_specs=[pl.BlockSpec((TQ, 128), lambda h, i: (i, h)),
                   pl.BlockSpec((None, 8, TQ), lambda h, i: (h, 0, i))],
        out_shape=[jax.ShapeDtypeStruct((T, AW), F32), jax.ShapeDtypeStruct((4, 8, T), F32)],
        scratch_shapes=[pltpu.VMEM((2, 2, TQ, TQ), F32), pltpu.VMEM((2, 2, TQ, TQ), BF16)],
        compiler_params=_params(40, ("arbitrary", "arbitrary")),
    )(qt8, k8, vt8, sel)


def _flash_bwd(q8, qt8, k8, kt8, v8, do8, dot8, lse, dlt):
    T = q8.shape[0]
    nk = T // TQ

    def body(q_ref, qt_ref, k_ref, kt_ref, v_ref, do_ref, dot_ref, l_ref, d_ref, dqt_ref, dk_ref, dv_ref,
             u_scr, dp_scr, p_scr, ds_scr):
        kb = pl.program_id(1)
        n = nk - kb

        @pl.when(kb == 0)
        def _():
            dqt_ref[...] = jnp.zeros(dqt_ref.shape, F32)

        dk_ref[...] = jnp.zeros(dk_ref.shape, F32)
        dv_ref[...] = jnp.zeros(dv_ref.shape, F32)
        u_scr[1] = jnp.full((2, TQ, TQ), MASKED, F32)
        dp_scr[1] = jnp.zeros((2, TQ, TQ), F32)
        p_scr[...] = jnp.zeros(p_scr.shape, BF16)
        ds_scr[...] = jnp.zeros(ds_scr.shape, BF16)
        dmat = (lax.broadcasted_iota(jnp.int32, (TQ, TQ), 0) - lax.broadcasted_iota(jnp.int32, (TQ, TQ), 1))
        ks = (k_ref[:, 0:128], k_ref[:, 128:256])
        vs = (v_ref[:, 0:128], v_ref[:, 128:256])
        kts = (kt_ref[0:128, :], kt_ref[128:256, :])

        def sub(t, sc, sb):
            blk_a = kb + jnp.minimum(t, n - 1)
            blk_c = kb + jnp.clip(t - 2, 0, n - 1)
            off_b = pl.multiple_of((kb + jnp.clip(t - 1, 0, n - 1)) * TQ, TQ)
            off_c = pl.multiple_of(blk_c * TQ, TQ)
            lim = jnp.where(t < n, t * TQ, -TQ)
            for j in (0, 1):
                hl = slice(j * 128, (j + 1) * 128)
                dqt_ref[blk_c, hl, :] += _dot(kts[j], ds_scr[sc, j])
                dk_ref[:, hl] += _dot(ds_scr[sc, j], q_ref[pl.ds(off_c, TQ), hl])
                dv_ref[:, hl] += _dot(p_scr[sc, j], do_ref[pl.ds(off_c, TQ), hl])
                p = jnp.exp(u_scr[sb, j] - l_ref[j:j + 1, pl.ds(off_b, TQ)])
                p_scr[sb, j] = p.astype(BF16)
                ds_scr[sb, j] = (p * (dp_scr[sb, j] - d_ref[j:j + 1, pl.ds(off_b, TQ)])).astype(BF16)
                u_scr[sc, j] = jnp.where(dmat <= lim, _dot(ks[j], qt_ref[blk_a, hl, :]), MASKED)
                dp_scr[sc, j] = _dot(vs[j], dot_ref[blk_a, hl, :])

        def it(t2, carry):
            sub(2 * t2, 0, 1)
            sub(2 * t2 + 1, 1, 0)
            return carry

        lax.fori_loop(0, (n + 3) // 2, it, 0)

    return pl.pallas_call(
        body, name="flash_bwd", grid=(4, nk),
        in_specs=[pl.BlockSpec((T, 256), lambda h, i: (0, h)),
                  pl.BlockSpec((nk, 256, TQ), lambda h, i: (0, h, 0)),
                  pl.BlockSpec((TQ, 256), lambda h, i: (i, h)),
                  pl.BlockSpec((None, 256, TQ), lambda h, i: (i, h, 0)),
                  pl.BlockSpec((TQ, 256), lambda h, i: (i, h)),
                  pl.BlockSpec((T, 256), lambda h, i: (0, h)),
                  pl.BlockSpec((nk, 256, TQ), lambda h, i: (0, h, 0)),
                  pl.BlockSpec((None, 8, T), lambda h, i: (h, 0, 0)),
                  pl.BlockSpec((None, 8, T), lambda h, i: (h, 0, 0))],
        out_specs=[pl.BlockSpec((nk, 256, TQ), lambda h, i: (0, h, 0)),
                   pl.BlockSpec((TQ, 256), lambda h, i: (i, h)),
                   pl.BlockSpec((TQ, 256), lambda h, i: (i, h))],
        out_shape=[jax.ShapeDtypeStruct((nk, NH * 128, TQ), F32), jax.ShapeDtypeStruct((T, NH * 128), F32),
                   jax.ShapeDtypeStruct((T, NH * 128), F32)],
        scratch_shapes=[pltpu.VMEM((2, 2, TQ, TQ), F32), pltpu.VMEM((2, 2, TQ, TQ), F32),
                        pltpu.VMEM((2, 2, TQ, TQ), BF16), pltpu.VMEM((2, 2, TQ, TQ), BF16)],
        compiler_params=_params(56, ("arbitrary", "arbitrary")),
    )(q8, qt8, k8, kt8, v8, do8, dot8, lse, dlt)


def _tail_fwd1(x, yatt, ysgn, gatt, wout, gpm, gpf, w1):
    T = x.shape[0]

    def body(t, r, o, a, s):
        x_ref, ya_ref, ys_ref = t
        gatt_r, wout_r, gpm_r, gpf_r, w1_r = r
        y_o, o_o, h1_o, c2_o, s_o, rr_o = o
        ya = ya_ref[...]
        yan = (ya * _rs(ya, AW) * gatt_r[...]).astype(BF16)
        y_o[:, 0:AW] = yan
        y_o[:, AW:] = ys_ref[...]
        ov = _dot(yan, wout_r[0:AW, :]) + _dot(ys_ref[...], wout_r[AW:, :])
        o_o[...] = ov
        h1 = x_ref[...] + ov * _rs(ov, D) * gpm_r[...]
        h1_o[...] = h1
        c2 = (h1 * _rs(h1, D) * gpf_r[...]).astype(BF16)
        c2_o[...] = c2
        rr = jnp.maximum(_dot(c2, w1_r[...]), 0.0)
        rr_o[...] = rr.astype(BF16)
        s_o[...] = (rr * rr).astype(BF16)

    return _row_call(
        "tail_fwd1", body, T, TM, [x, yatt, ysgn], [gatt, wout, gpm, gpf, w1],
        [(D, BF16), (D, F32), (D, F32), (D, BF16), (DFF, BF16), (DFF, BF16)], [], vmem_mb=48)


def _tail_fwd2(sact, h1, p, tgt, w2, gpff, wg, bg, wpe):
    T = h1.shape[0]

    def body(t, r, o, a, s):
        s_ref, h1_ref, p_ref, t_ref = t
        w2_r, gpff_r, wg_r, bg_r, wpe_r = r
        ff_o, h2b_o, de_o, dpre_o, dh2_o = o
        loss_a, dbg_a = a
        ff = _dot(s_ref[...], w2_r[...])
        ff_o[...] = ff
        h2 = h1_ref[...] + ff * _rs(ff, D) * gpff_r[...]
        h2b = h2.astype(BF16)
        h2b_o[...] = h2b
        gate = 1.0 / (1.0 + jnp.exp(-(_dot(h2b, wg_r[...]) + bg_r[...])))
        e = _dot(p_ref[...].astype(BF16), wpe_r[...])
        diff = h2 + gate * e - t_ref[...]
        loss_a[...] += jnp.sum(diff * diff, axis=0, keepdims=True)
        dh3 = diff * (1.0 / D)
        de_o[...] = (dh3 * gate).astype(BF16)
        dpre = dh3 * e * gate * (1.0 - gate)
        dbg_a[...] += jnp.sum(dpre, axis=0, keepdims=True)
        dpb = dpre.astype(BF16)
        dpre_o[...] = dpb
        dh2_o[...] = dh3 + _dot_nt(dpb, wg_r[...])

    return _row_call(
        "tail_fwd2", body, T, TM, [sact, h1, p, tgt], [w2, gpff, wg, bg, wpe],
        [(D, F32), (D, BF16), (D, BF16), (D, BF16), (D, F32)], [(1, D), (1, D)], vmem_mb=48)


def _tail_bwd(dh2, ff, rr, h1, ov, yatt, w2, w1, wout, gpff, gpf, gpm, gatt, hsel, expand):
    T = dh2.shape[0]

    def body(t, r, o, a, s):
        dh2_ref, ff_ref, rr_ref, h1_ref, o_ref, ya_ref = t
        w2_r, w1_r, wout_r, gpff_r, gpf_r, gpm_r, gatt_r, hsel_r, ex_r = r
        dff_o, dr_o, do_o, do8_o, dlt_o, dysg_o, dh1_o = o
        dgpff_a, dgpf_a, dgpm_a, dgatt_a = a
        dh2v = dh2_ref[...]
        ffv = ff_ref[...]
        dff, dg = _rms_bwd(dh2v, ffv, _rs(ffv, D), gpff_r[...], D)
        dgpff_a[...] += dg
        dffb = dff.astype(BF16)
        dff_o[...] = dffb
        drb = (_dot_nt(dffb, w2_r[...]) * (2.0 * rr_ref[...].astype(F32))).astype(BF16)
        dr_o[...] = drb
        dc2 = _dot_nt(drb, w1_r[...])
        h1v = h1_ref[...]
        d1, dg = _rms_bwd(dc2, h1v, _rs(h1v, D), gpf_r[...], D)
        dgpf_a[...] += dg
        dh1 = dh2v + d1
        dh1_o[...] = dh1
        ovv = o_ref[...]
        dov, dg = _rms_bwd(dh1, ovv, _rs(ovv, D), gpm_r[...], D)
        dgpm_a[...] += dg
        dob = dov.astype(BF16)
        do_o[...] = dob
        dysg_o[...] = _dot_nt(dob, wout_r[AW:, :])
        dyan = _dot_nt(dob, wout_r[0:AW, :])
        ya = ya_ref[...]
        dya, dg = _rms_bwd(dyan, ya, _rs(ya, AW), gatt_r[...], AW)
        dgatt_a[...] += dg
        do8_o[...] = _dot(dya.astype(BF16), ex_r[...]).astype(BF16)
        dlt_o[...] = _dot01_r(dya * ya, hsel_r[...])

    return _row_call(
        "tail_bwd", body, T, TM, [dh2, ff, rr, h1, ov, yatt],
        [w2, w1, wout, gpff, gpf, gpm, gatt, hsel, expand],
        [(D, BF16), (DFF, BF16), (D, BF16), (NH * 128, BF16), (AW, F32), (AW, F32), (D, F32)],
        [(1, D), (1, D), (1, D), (1, AW)], vmem_mb=56)


def _pre_attn_bwd(x, dh1, dq8, dk8, dv8, flog, zuv, dysg, gpre, win, lng, lnb, wm, wmt, bsg, gsg, gsel, shrink, pick64):
    T = x.shape[0]
    tm = TM

    def body(t, r, o, a, s):
        x_ref, dh1_ref, dq_ref, dk_ref, dv_ref, fl_ref, zuv_ref, dys_ref = t
        gpre_r, win_r, lng_r, lnb_r, wm_r, wmt_r, bsg_r, gsg_r, gsel_r, sh_r, p64_r = r
        dx_o, dz_o = o
        dgpre_a, dfb_a, dgsg_a, dlng_a, dlnb_a, dws_a, dbs_a, dsb_a = a
        carry_ref, mixed_ref, dvv_ref = s
        dq8v = dq_ref[...]
        dk8v = dk_ref[...]
        dcv = _dot01_r(dq8v, p64_r[...]) + _dot01_r(dk8v, p64_r[...])
        rr = lax.broadcasted_iota(jnp.int32, (tm, tm), 0)
        cc = lax.broadcasted_iota(jnp.int32, (tm, tm), 1)
        triu = (cc >= rr).astype(BF16)
        dlogf = _dot01(triu, dcv) + carry_ref[...]
        carry_ref[...] = dlogf[0:1, :]
        dzf = dlogf * (1.0 / (1.0 + jnp.exp(fl_ref[...])))
        dfb_a[...] += jnp.sum(dzf, axis=0, keepdims=True)
        dz_o[:, 5 * AW:] = dzf.astype(BF16)
        zu = zuv_ref[:, 0:AW]
        zv = zuv_ref[:, AW:]
        gu, tu, tv, xhat, rstd, vvb, mixed = _sg_forward(
            zu, zv, wm_r, bsg_r[...], lng_r[...], lnb_r[...], mixed_ref, tm)
        ysg = gu * mixed
        dysg_n = dys_ref[...]
        dys, dg = _rms_bwd(dysg_n, ysg, _rs(ysg, AW), gsg_r[...], AW)
        dgsg_a[...] += dg
        dgu = dys * mixed
        dmix = dys * gu
        dmb = dmix.astype(BF16)
        lane = lax.broadcasted_iota(jnp.int32, (CH, 128), 1)
        lo = lane < DH
        for c in range(tm // CH):
            rows = slice(c * CH, (c + 1) * CH)
            dbs_a[...] += dmix[rows, :]
            for j in range(4):
                cols = slice(j * 128, (j + 1) * 128)
                dmblk = dmb[rows, cols]
                vblk = vvb[rows, cols]
                d0 = _dot(wmt_r[2 * j], dmblk)
                d1 = _dot(wmt_r[2 * j + 1], dmblk)
                dvv_ref[rows, cols] = jnp.where(lo, d0, d1)
                dws_a[2 * j] += _dot_nt(jnp.where(lo, dmblk, jnp.zeros_like(dmblk)), vblk)
                dws_a[2 * j + 1] += _dot_nt(jnp.where(lo, jnp.zeros_like(dmblk), dmblk), vblk)
        dvv = dvv_ref[...]
        dlng_a[...] += jnp.sum(dvv * xhat, axis=0, keepdims=True)
        dlnb_a[...] += jnp.sum(dvv, axis=0, keepdims=True)
        dxh = dvv * lng_r[...]
        dvg = rstd * (dxh - jnp.sum(dxh, axis=-1, keepdims=True) * (1.0 / AW)
                      - xhat * (jnp.sum(dxh * xhat, axis=-1, keepdims=True) * (1.0 / AW)))
        dz_o[:, 3 * AW:4 * AW] = (dgu * _gelu_grad(zu, tu)).astype(BF16)
        dz_o[:, 4 * AW:5 * AW] = (dvg * _gelu_grad(zv, tv)).astype(BF16)
        dz_o[:, 0:AW] = _dot((dq8v * (DH ** -0.5)).astype(BF16), sh_r[...]).astype(BF16)
        dz_o[:, AW:2 * AW] = _dot(dk8v.astype(BF16), sh_r[...]).astype(BF16)
        dz_o[:, 2 * AW:3 * AW] = _dot(dv_ref[...].astype(BF16), sh_r[...]).astype(BF16)
        da = _dot_nt(dz_o[...], win_r[...])
        xv = x_ref[...]
        dxa, dg = _rms_bwd(da, xv, _rs(xv, D), gpre_r[...], D)
        dgpre_a[...] += dg
        dx_o[...] = dh1_ref[...] + dxa

        @pl.when(pl.program_id(0) == T // tm - 1)
        def _():
            dsb_a[...] = _dot01_r(dbs_a[...], gsel_r[...])

    outs = _row_call(
        "pre_attn_bwd", body, T, tm, [x, dh1, dq8, dk8, dv8, flog, zuv, dysg],
        [gpre, win, lng, lnb, wm, wmt, bsg, gsg, gsel, shrink, pick64],
        [(D, F32), (ZW, BF16)],
        [(1, D), (1, 128), (1, AW), (1, AW), (1, AW), (8, CH, CH), (CH, AW), (CH, 128)],
        scratch=[(1, 128), (tm, AW), (tm, AW)], reverse=True, vmem_mb=48)
    return outs


def _matmul_tn(name, a, b, tn=512, tt=2048, shards=1):
    T, K = a.shape
    N = b.shape[1]
    tk = min(K, 1024)
    tn = min(tn, N // shards)
    tt = min(tt, T)
    nj = N // shards // tn

    def body(a_ref, b_ref, o_ref):
        @pl.when(pl.program_id(2) == 0)
        def _():
            o_ref[...] = jnp.zeros(o_ref.shape, F32)

        o_ref[...] += _dot_tn(a_ref[...].astype(BF16), b_ref[...].astype(BF16))

    if shards == 1:
        out_shape = jax.ShapeDtypeStruct((K, N), F32)
        out_spec = pl.BlockSpec((tk, tn), lambda i, j, t: (i, j))
    else:
        out_shape = jax.ShapeDtypeStruct((shards, K, N // shards), F32)
        out_spec = pl.BlockSpec((None, tk, tn), lambda i, j, t: (j // nj, i, j % nj))
    return pl.pallas_call(
        body, name=name, grid=(K // tk, N // tn, T // tt),
        in_specs=[pl.BlockSpec((tt, tk), lambda i, j, t: (t, i)),
                  pl.BlockSpec((tt, tn), lambda i, j, t: (t, j))],
        out_specs=out_spec, out_shape=out_shape,
        compiler_params=_params(40, ("arbitrary", "arbitrary", "arbitrary")),
    )(a, b)


def _me():
    return lax.axis_index("x"), lax.axis_index("y"), lax.axis_index("c")


HBM_SPEC = pl.BlockSpec(memory_space=pltpu.HBM)


def _gather_weights(mine):
    half = mine.shape[0] // 2

    def body(mine_ref, out_ref, ici_send, ici_recv, d2d_send, d2d_recv):
        x, y, c = _me()
        k_me = 2 * x + y
        chips = [(1 - x, y), (x, 1 - y), (1 - x, 1 - y)]
        my_rows = pl.ds(pl.multiple_of(c * half, 16), half)
        sib_rows = pl.ds(pl.multiple_of((1 - c) * half, 16), half)

        def over_ici(j, k, to):
            src = mine_ref.at[my_rows] if k is None else out_ref.at[k, my_rows]
            return pltpu.make_async_remote_copy(
                src_ref=src, dst_ref=out_ref.at[k_me if k is None else k, my_rows], send_sem=ici_send.at[j],
                recv_sem=ici_recv.at[j], device_id=to, device_id_type=MESH)

        def over_d2d(j, k, rows):
            return pltpu.make_async_remote_copy(
                src_ref=out_ref.at[k, rows], dst_ref=out_ref.at[k, rows], send_sem=d2d_send.at[j],
                recv_sem=d2d_recv.at[j], device_id=(x, y, 1 - c), device_id_type=MESH)

        first = [over_ici(j, None, (cx, cy, c)) for j, (cx, cy) in enumerate(chips)]
        for cp in first:
            cp.start()
        passed = [over_d2d(j, 2 * cx + cy, my_rows) for j, (cx, cy) in enumerate(chips)]
        for j, (cx, cy) in enumerate(chips):
            over_ici(j, 2 * cx + cy, (cx, cy, c)).wait_recv()
            passed[j].start()
        for j, (cx, cy) in enumerate(chips):
            over_d2d(j, 2 * cx + cy, sib_rows).wait_recv()
        for cp in first + passed:
            cp.wait_send()

    return pl.pallas_call(
        body, name="gather_weights", in_specs=[HBM_SPEC], out_specs=HBM_SPEC,
        out_shape=jax.ShapeDtypeStruct((4,) + mine.shape, mine.dtype),
        scratch_shapes=[pltpu.SemaphoreType.DMA((3,)), pltpu.SemaphoreType.DMA((3,)), pltpu.SemaphoreType.DMA((3,)),
                        pltpu.SemaphoreType.DMA((3,))],
    )(mine)


SEM_SPEC = pl.BlockSpec(memory_space=pltpu.SEMAPHORE)
EFFECT = pltpu.SideEffectType.DATAFLOW_SIDE_EFFECTING


def _gather_late_start(mine):
    def body(mine_ref, land_ref, send_sems, recv_sems, mine_thru, land_thru, token):
        x, y, c = _me()
        k_me = 2 * x + y
        for j, (cx, cy) in enumerate([(1 - x, y), (x, 1 - y), (1 - x, 1 - y)]):
            pltpu.make_async_remote_copy(
                src_ref=mine_ref, dst_ref=land_ref.at[k_me], send_sem=send_sems.at[j], recv_sem=recv_sems.at[j],
                device_id=(cx, cy, c), device_id_type=MESH).start()
        token[...] = jnp.zeros(token.shape, F32)

    land = lax.empty((4,) + mine.shape, mine.dtype)
    return pl.pallas_call(
        body, name="gather_late_start",
        out_shape=(pltpu.SemaphoreType.DMA((3,)), pltpu.SemaphoreType.DMA((3,)), pltpu.HBM(mine.shape, mine.dtype),
                   pltpu.HBM(land.shape, land.dtype), jax.ShapeDtypeStruct((8, 128), F32)),
        in_specs=(HBM_SPEC, HBM_SPEC),
        out_specs=(SEM_SPEC, SEM_SPEC, HBM_SPEC, HBM_SPEC, pl.BlockSpec(memory_space=pltpu.VMEM)),
        input_output_aliases={0: 2, 1: 3},
        compiler_params=pltpu.CompilerParams(has_side_effects=EFFECT),
    )(pltpu.with_memory_space_constraint(mine, pltpu.HBM), pltpu.with_memory_space_constraint(land, pltpu.HBM))


def _gather_late_wait(send_sems, recv_sems, mine_thru, land_thru, after):
    def body(mine_ref, land_ref, send_sems, recv_sems, after_ref, mine_dead, got_ref):
        x, y, c = _me()
        for j, (cx, cy) in enumerate([(1 - x, y), (x, 1 - y), (1 - x, 1 - y)]):
            cp = pltpu.make_async_remote_copy(
                src_ref=mine_ref, dst_ref=land_ref.at[2 * cx + cy], send_sem=send_sems.at[j],
                recv_sem=recv_sems.at[j], device_id=(cx, cy, c), device_id_type=MESH)
            cp.wait_send()
            cp.wait_recv()

    return pl.pallas_call(
        body, name="gather_late_wait",
        out_shape=(pltpu.HBM(mine_thru.shape, mine_thru.dtype), pltpu.HBM(land_thru.shape, land_thru.dtype)),
        in_specs=(HBM_SPEC, HBM_SPEC, SEM_SPEC, SEM_SPEC, pl.BlockSpec(memory_space=pl.ANY)),
        out_specs=(HBM_SPEC, HBM_SPEC), input_output_aliases={0: 0, 1: 1},
        compiler_params=pltpu.CompilerParams(has_side_effects=EFFECT),
    )(mine_thru, land_thru, send_sems, recv_sems, after)[1]


def _swap_halves(gs, tag):
    n = len(gs)

    def body(*refs):
        g_refs, got_refs, send_sems, recv_sems = refs[:n], refs[n:2 * n], refs[2 * n], refs[2 * n + 1]
        x, y, c = _me()
        cps = []
        for i, (g_ref, got_ref) in enumerate(zip(g_refs, got_refs)):
            half = g_ref.shape[1] // 2
            theirs = pl.multiple_of((1 - c) * half, 16)
            cps.append(pltpu.make_async_remote_copy(
                src_ref=g_ref.at[:, pl.ds(theirs, half), :], dst_ref=got_ref, send_sem=send_sems.at[i],
                recv_sem=recv_sems.at[i], device_id=(x, y, 1 - c), device_id_type=MESH))
        for cp in cps:
            cp.start()
        for cp in cps:
            cp.wait()

    return pl.pallas_call(
        body, name="swap_halves_" + tag, in_specs=[HBM_SPEC] * n, out_specs=[HBM_SPEC] * n,
        out_shape=[jax.ShapeDtypeStruct((4, g.shape[1] // 2, g.shape[2]), F32) for g in gs],
        scratch_shapes=[pltpu.SemaphoreType.DMA((n,)), pltpu.SemaphoreType.DMA((n,))],
    )(*gs)


def _pair_sum(name, c1, g, got):
    half, cols = got.shape[1], got.shape[2]

    def body(c_ref, a_ref, b_ref, o_ref):
        o_ref[...] = (a_ref[...] + b_ref[...]).astype(BF16)

    return pl.pallas_call(
        body, name="pair_sum_" + name,
        grid_spec=pltpu.PrefetchScalarGridSpec(
            num_scalar_prefetch=1, grid=(4,),
            in_specs=[pl.BlockSpec((1, half, cols), lambda k, c_ref: (k, c_ref[0], 0)),
                      pl.BlockSpec((1, half, cols), lambda k, c_ref: (k, 0, 0))],
            out_specs=pl.BlockSpec((1, half, cols), lambda k, c_ref: (k, 0, 0))),
        out_shape=jax.ShapeDtypeStruct(got.shape, BF16), compiler_params=_params(32),
    )(c1, g, got)


def _exchange_chips(pss):
    n = len(pss)

    def body(*refs):
        ps_refs, out_refs = refs[:n], refs[n:2 * n]
        send_sems, recv_sems = refs[2 * n:]
        x, y, c = _me()
        k_me = 2 * x + y
        chips = [(1 - x, y), (x, 1 - y), (1 - x, 1 - y)]
        sends = []
        for i, (ps_ref, out_ref) in enumerate(zip(ps_refs, out_refs)):
            for j, (cx, cy) in enumerate(chips):
                sends.append(pltpu.make_async_remote_copy(
                    src_ref=ps_ref.at[2 * cx + cy], dst_ref=out_ref.at[k_me], send_sem=send_sems.at[3 * i + j],
                    recv_sem=recv_sems.at[3 * i + j], device_id=(cx, cy, c), device_id_type=MESH))
        for cp in sends:
            cp.start()
        for i, (ps_ref, out_ref) in enumerate(zip(ps_refs, out_refs)):
            for j, (cx, cy) in enumerate(chips):
                pltpu.make_async_remote_copy(
                    src_ref=ps_ref.at[k_me], dst_ref=out_ref.at[2 * cx + cy], send_sem=send_sems.at[3 * i + j],
                    recv_sem=recv_sems.at[3 * i + j], device_id=(cx, cy, c), device_id_type=MESH).wait_recv()
        for cp in sends:
            cp.wait_send()

    return pl.pallas_call(
        body, name="exchange_chips", in_specs=[HBM_SPEC] * n, out_specs=[HBM_SPEC] * n,
        out_shape=[jax.ShapeDtypeStruct(ps.shape, ps.dtype) for ps in pss],
        scratch_shapes=[pltpu.SemaphoreType.DMA((3 * n,)), pltpu.SemaphoreType.DMA((3 * n,))],
    )(*pss)


def _exchange_start(pss):
    n = len(pss)

    def body(*refs):
        ps_refs, land_refs = refs[:n], refs[n:2 * n]
        send_sems, recv_sems = refs[2 * n], refs[2 * n + 1]
        token = refs[4 * n + 2]
        x, y, c = _me()
        k_me = 2 * x + y
        for i, (ps_ref, land_ref) in enumerate(zip(ps_refs, land_refs)):
            for j, (cx, cy) in enumerate([(1 - x, y), (x, 1 - y), (1 - x, 1 - y)]):
                pltpu.make_async_remote_copy(
                    src_ref=ps_ref.at[2 * cx + cy], dst_ref=land_ref.at[k_me], send_sem=send_sems.at[3 * i + j],
                    recv_sem=recv_sems.at[3 * i + j], device_id=(cx, cy, c), device_id_type=MESH).start()
        token[...] = jnp.zeros(token.shape, F32)

    lands = [lax.empty(ps.shape, ps.dtype) for ps in pss]
    hbm = lambda t: pltpu.HBM(t.shape, t.dtype)
    res = pl.pallas_call(
        body, name="exchange_start",
        out_shape=(pltpu.SemaphoreType.DMA((3 * n,)), pltpu.SemaphoreType.DMA((3 * n,)), *[hbm(t) for t in pss],
                   *[hbm(t) for t in lands], jax.ShapeDtypeStruct((8, 128), F32)),
        in_specs=(HBM_SPEC,) * (2 * n),
        out_specs=(SEM_SPEC, SEM_SPEC) + (HBM_SPEC,) * (2 * n) + (pl.BlockSpec(memory_space=pltpu.VMEM),),
        input_output_aliases={i: 2 + i for i in range(2 * n)},
        compiler_params=pltpu.CompilerParams(has_side_effects=EFFECT),
    )(*[pltpu.with_memory_space_constraint(t, pltpu.HBM) for t in list(pss) + lands])
    return res[0], res[1], res[2:2 + n], res[2 + n:2 + 2 * n], res[2 + 2 * n]


def _exchange_wait(send_sems, recv_sems, ps_thru, land_thru, after):
    n = len(ps_thru)

    def body(*refs):
        ps_refs, land_refs = refs[:n], refs[n:2 * n]
        send_sems, recv_sems = refs[2 * n], refs[2 * n + 1]
        x, y, c = _me()
        k_me = 2 * x + y
        for i, (ps_ref, land_ref) in enumerate(zip(ps_refs, land_refs)):
            for j, (cx, cy) in enumerate([(1 - x, y), (x, 1 - y), (1 - x, 1 - y)]):
                cp = pltpu.make_async_remote_copy(
                    src_ref=ps_ref.at[k_me], dst_ref=land_ref.at[2 * cx + cy], send_sem=send_sems.at[3 * i + j],
                    recv_sem=recv_sems.at[3 * i + j], device_id=(cx, cy, c), device_id_type=MESH)
                cp.wait_send()
                cp.wait_recv()

    hbm = lambda t: pltpu.HBM(t.shape, t.dtype)
    res = pl.pallas_call(
        body, name="exchange_wait",
        out_shape=tuple(hbm(t) for t in list(ps_thru) + list(land_thru)),
        in_specs=(HBM_SPEC,) * (2 * n) + (SEM_SPEC, SEM_SPEC, pl.BlockSpec(memory_space=pl.ANY)),
        out_specs=(HBM_SPEC,) * (2 * n), input_output_aliases={i: i for i in range(2 * n)},
        compiler_params=pltpu.CompilerParams(has_side_effects=EFFECT),
    )(*ps_thru, *land_thru, send_sems, recv_sems, after)
    return res[:n], res[n:]


def _adamw(w, g, m, v):
    m = B1 * m + (1.0 - B1) * g
    v = B2 * v + (1.0 - B2) * (g * g)
    delta = -LR * ((m / BC1) / (jnp.sqrt(v / BC2) + AEPS) + WD * w)
    return delta, m, v


def _reduce_chips(name, parts):
    half, cols = parts.shape[1], parts.shape[2]

    def body(p_ref, o_ref):
        f = lambda k: p_ref[k].astype(F32)
        o_ref[...] = ((f(0) + f(1)) + f(2)) + f(3)

    return pl.pallas_call(
        body, name="reduce_chips_" + name, grid=(1,),
        in_specs=[pl.BlockSpec((4, half, cols), lambda i: (0, 0, 0))],
        out_specs=pl.BlockSpec((half, cols), lambda i: (0, 0)),
        out_shape=jax.ShapeDtypeStruct((half, cols), F32), compiler_params=_params(32),
    )(parts)


def _share_grad(ghs, tag):
    n = len(ghs)

    def body(*refs):
        g_refs, got_refs, send_sems, recv_sems = refs[:n], refs[n:2 * n], refs[2 * n], refs[2 * n + 1]
        x, y, c = _me()
        cps = [pltpu.make_async_remote_copy(
            src_ref=g_ref, dst_ref=got_ref, send_sem=send_sems.at[i], recv_sem=recv_sems.at[i],
            device_id=(x, y, 1 - c), device_id_type=MESH) for i, (g_ref, got_ref) in enumerate(zip(g_refs, got_refs))]
        for cp in cps:
            cp.start()
        for cp in cps:
            cp.wait()

    return pl.pallas_call(
        body, name="share_grad_" + tag, in_specs=[HBM_SPEC] * n, out_specs=[HBM_SPEC] * n,
        out_shape=[jax.ShapeDtypeStruct(g.shape, F32) for g in ghs],
        scratch_shapes=[pltpu.SemaphoreType.DMA((n,)), pltpu.SemaphoreType.DMA((n,))],
    )(*ghs)


def _update(name, c1, gh, got, w, m, v):
    half, cols = gh.shape

    def body(c_ref, gh_ref, got_ref, w_ref, m_ref, v_ref, g_o, d_o, m_o, v_o):
        g = jnp.where(pl.program_id(0) == c_ref[0], gh_ref[...], got_ref[...])
        delta, mn, vn = _adamw(w_ref[...], g, m_ref[...], v_ref[...])
        g_o[...] = g
        d_o[...] = delta
        m_o[...] = mn
        v_o[...] = vn

    same = pl.BlockSpec((half, cols), lambda h, c_ref: (0, 0))
    rows = pl.BlockSpec((half, cols), lambda h, c_ref: (h, 0))
    return pl.pallas_call(
        body, name="update_" + name,
        grid_spec=pltpu.PrefetchScalarGridSpec(
            num_scalar_prefetch=1, grid=(2,), in_specs=[same, same, rows, rows, rows],
            out_specs=[rows, rows, rows, rows]),
        out_shape=[jax.ShapeDtypeStruct(w.shape, F32)] * 4, compiler_params=_params(40),
    )(c1, gh, got, w, m, v)


SMALL_NAMES = ("sg_w",) + VEC_NAMES
VEC_ROWS = 24
VEC_ROW = {"f_bias": 0, "sg_ln_g": 1, "sg_ln_b": 2, "att_out_g": 3, "sg_out_g": 4, "pre_mix_g": 5,
           "post_mix_g": 6, "pre_ffn_g": 7, "sg_b": 8, "post_ffn_g": 16, "ple_gate_b": 17}
LOSS_VEC_ROW = 18


def _small_allreduce(g, loss_l):
    n = len(SMALL_NAMES)

    def body(*refs):
        g_r = dict(zip(SMALL_NAMES, refs[0:n]))
        loss_r, totv_o, totw_o, bufv, bufw, send_sems, recv_sems = refs[n:]
        x, y, c = _me()
        me = 4 * x + 2 * y + c
        bufv[me] = jnp.zeros((VEC_ROWS, 1024), F32)
        for name in VEC_NAMES:
            val = g_r[name][...]
            bufv[me, pl.ds(VEC_ROW[name], val.shape[0]), pl.ds(0, val.shape[1])] = val
        bufv[me, pl.ds(LOSS_VEC_ROW, 1), :] = loss_r[...] * (0.5 / D)
        rr = lax.broadcasted_iota(jnp.int32, (CH, CH), 0)
        cc = lax.broadcasted_iota(jnp.int32, (CH, CH), 1)
        bufw[me] = jnp.where((cc <= rr)[None], g_r["sg_w"][...], 0.0)

        rels = [(rx, ry, rc) for rx in (0, 1) for ry in (0, 1) for rc in (0, 1)][1:]

        def peer(r):
            return ((x + r[0]) % 2, (y + r[1]) % 2, (c + r[2]) % 2)

        def copies(j, slot, to):
            return [pltpu.make_async_remote_copy(
                src_ref=buf.at[slot], dst_ref=buf.at[slot], send_sem=send_sems.at[2 * j + i],
                recv_sem=recv_sems.at[2 * j + i], device_id=to, device_id_type=MESH)
                for i, buf in enumerate((bufv, bufw))]

        sends = [cp for j, r in enumerate(rels) for cp in copies(j, me, peer(r))]
        for cp in sends:
            cp.start()
        for j, r in enumerate(rels):
            px, py, pc = peer(r)
            for cp in copies(j, 4 * px + 2 * py + pc, peer(r)):
                cp.wait_recv()
        for cp in sends:
            cp.wait_send()

        tot_v = bufv[0]
        tot_w = bufw[0]
        for d in range(1, 8):
            tot_v = tot_v + bufv[d]
            tot_w = tot_w + bufw[d]
        totv_o[...] = tot_v
        totw_o[...] = tot_w

    vm = pl.BlockSpec(memory_space=pltpu.VMEM)
    args = [g[k] for k in SMALL_NAMES] + [loss_l]
    return pl.pallas_call(
        body, name="small_allreduce", in_specs=[vm] * len(args), out_specs=[vm, vm],
        out_shape=[jax.ShapeDtypeStruct((VEC_ROWS, 1024), F32), jax.ShapeDtypeStruct((8, CH, CH), F32)],
        scratch_shapes=[pltpu.VMEM((8, VEC_ROWS, 1024), F32), pltpu.VMEM((8, 8, CH, CH), F32),
                        pltpu.SemaphoreType.DMA((14,)), pltpu.SemaphoreType.DMA((14,))],
        compiler_params=pltpu.CompilerParams(vmem_limit_bytes=32 * 1024 * 1024),
    )(*args)


def _small_update(tot_v, tot_w, w, m, v):
    n = len(SMALL_NAMES)

    def body(*refs):
        totv_r, totw_r = refs[0], refs[1]
        w_r = dict(zip(SMALL_NAMES, refs[2:2 + n]))
        m_r = dict(zip(SMALL_NAMES, refs[2 + n:2 + 2 * n]))
        v_r = dict(zip(SMALL_NAMES, refs[2 + 2 * n:2 + 3 * n]))
        loss_o = refs[2 + 3 * n]
        outs = refs[3 + 3 * n:]
        loss_o[...] = jnp.sum(totv_r[LOSS_VEC_ROW:LOSS_VEC_ROW + 1, :], axis=-1, keepdims=True) + jnp.zeros((1, 128), F32)
        for i, name in enumerate(SMALL_NAMES):
            if name == "sg_w":
                gt = totw_r[...]
            else:
                rows, width = w_r[name].shape
                gt = totv_r[VEC_ROW[name]:VEC_ROW[name] + rows, 0:width]
            delta, mn, vn = _adamw(w_r[name][...], gt, m_r[name][...], v_r[name][...])
            outs[4 * i][...] = gt
            outs[4 * i + 1][...] = delta
            outs[4 * i + 2][...] = mn
            outs[4 * i + 3][...] = vn

    vm = pl.BlockSpec(memory_space=pltpu.VMEM)
    args = [tot_v, tot_w] + [d[k] for d in (w, m, v) for k in SMALL_NAMES]
    out_shape = [jax.ShapeDtypeStruct((1, 128), F32)]
    out_shape += [jax.ShapeDtypeStruct(w[k].shape, F32) for k in SMALL_NAMES for _ in range(4)]
    res = pl.pallas_call(
        body, name="small_update", in_specs=[vm] * len(args), out_specs=[vm] * len(out_shape), out_shape=out_shape,
        compiler_params=pltpu.CompilerParams(vmem_limit_bytes=32 * 1024 * 1024),
    )(*args)
    return res[0], {k: res[1 + 4 * i:5 + 4 * i] for i, k in enumerate(SMALL_NAMES)}


def _win_kernel_order(gathered):
    w_in = jnp.concatenate([gathered[k].reshape(D, 768)[:, :642] for k in range(4)], axis=1)
    return jnp.concatenate([w_in[:, :3 * AW], w_in[:, 3 * AW + NH:], w_in[:, 3 * AW:3 * AW + NH],
                            jnp.zeros((D, 128 - NH), w_in.dtype)], axis=1)


LATE_ROWS = 256 + 1024 + 1024 + 64 + 256


def _pack_late(w_out, w1, w2, plew, wg):
    return jnp.concatenate([w_out, w1, w2, plew.reshape(64, 1024), wg], axis=0)


def _unpack_late(gathered):
    cols = lambda t: jnp.swapaxes(t, 0, 1).reshape(t.shape[1], 4 * t.shape[2])
    return (gathered[:, 0:256].reshape(D, D), cols(gathered[:, 256:1280]), gathered[:, 1280:2304].reshape(DFF, D),
            cols(gathered[:, 2304:2368].reshape(4, 256, 256)), gathered[:, 2368:2624].reshape(D, D))


def _local_step(x, p, tgt, win_k, late_weights, token, on_tail_grads, small):
    T = x.shape[0]
    row = lambda n: small[n].reshape(1, -1)
    fbias = jnp.pad(row("f_bias"), ((0, 0), (0, 128 - NH))) + token[0:1, :]
    wm = _masked_sg_w(small["sg_w"].reshape(8, CH, CH))
    wmb = wm.astype(BF16)
    wmt = jnp.swapaxes(wm, 1, 2).astype(BF16)
    bsg = jnp.repeat(small["sg_b"].reshape(8, CH).T, DH, axis=1)
    ln_g, ln_b, gsg, gatt = row("sg_ln_g"), row("sg_ln_b"), row("sg_out_g"), row("att_out_g")
    gpre, gpm, gpf, gpff, bg = row("pre_mix_g"), row("post_mix_g"), row("pre_ffn_g"), row("post_ffn_g"), row("ple_gate_b")
    gsel = (jnp.arange(AW)[:, None] // DH == jnp.arange(128)[None, :]).astype(BF16)
    hsel = (jnp.arange(AW)[:, None] // DH == jnp.arange(AW)[None, :] // DH).astype(BF16)

    expand, shrink, pieces, qconst, _, pick64 = _head_consts()
    a, qkv, flog, ccol, zuv, ysgn, q8, k8, v8 = _pre_attn_fwd(
        x, gpre, win_k, fbias, ln_g, ln_b, wmb, bsg, gsg, expand, pieces, qconst)

    nt = T // TQ
    hd = lambda t, i: t[:, i * AW:(i + 1) * AW].reshape(T, NH, DH)
    zpad = lambda n: jnp.zeros((T, NH, n), BF16)
    one = jnp.ones((T, NH, 1), BF16)
    wide = lambda parts: jnp.concatenate(parts, axis=-1).reshape(T, NH * 128)
    slabs = lambda t: jnp.swapaxes(t.reshape(nt, TQ, NH * 128), 1, 2)
    qt8 = slabs(q8)
    kt8 = slabs(wide([hd(qkv, 1), one, zpad(63)]))
    vt8 = slabs(wide([hd(qkv, 2), one, zpad(63)]))
    lanes = jnp.arange(128)
    sel = jnp.stack([((lanes[:, None] == lanes[None, :] - DH * j) & (lanes[:, None] < DH)).astype(BF16)
                     for j in (0, 1)])

    yatt, lse = _flash_fwd(qt8, k8, vt8, sel)
    wout, w1, w2, plew, wg = late_weights(lse)
    y, ov, h1, c2, sact, rr = _tail_fwd1(x, yatt, ysgn, gatt, wout, gpm, gpf, w1)
    ff, h2b, de, dpre, dh2, loss_l, dbg = _tail_fwd2(sact, h1, p, tgt, w2, gpff, wg, bg, plew)
    dff, dr, do, do8, dlt, dysg, dh1, dgpff, dgpf, dgpm, dgatt = _tail_bwd(
        dh2, ff, rr, h1, ov, yatt, w2, w1, wout, gpff, gpf, gpm, gatt, hsel, expand)
    dwout = _matmul_tn("grad_w_out", y, do)
    dw1 = _matmul_tn("grad_w_ff1", c2, dr, shards=4)
    dw2 = _matmul_tn("grad_w_ff2", sact, dff)
    dwg = _matmul_tn("grad_ple_gate_w", h2b, dpre)
    dplew = _matmul_tn("grad_ple_w", p, de, tn=256, shards=4)
    tail_token = on_tail_grads((dwout, dw1, dw2, dplew, dwg))
    dlt4 = jnp.pad(dlt[:, ::DH].T.reshape(4, 2, T), ((0, 0), (0, 6), (0, 0))) + tail_token[0, 0]
    dqt, dk8, dv8 = _flash_bwd(q8, qt8, k8, kt8, v8, do8, slabs(do8), lse, dlt4)
    dx, dz, dgpre, dfb, dgsg, dlng, dlnb, dws, _, dsbt = _pre_attn_bwd(
        x, dh1, jnp.swapaxes(dqt, 1, 2).reshape(T, NH * 128), dk8, dv8, flog, zuv, dysg,
        gpre, win_k, ln_g, ln_b, wmb, wmt, bsg, gsg, gsel, shrink, pick64)

    dwin_k = _matmul_tn("grad_w_in", a, dz, tn=384)

    dsb = dsbt[:, :8].T
    gsmall = {"sg_w": dws, "f_bias": dfb, "sg_ln_g": dlng, "sg_ln_b": dlnb, "sg_b": dsb,
              "att_out_g": dgatt, "sg_out_g": dgsg, "pre_mix_g": dgpre, "post_mix_g": dgpm, "pre_ffn_g": dgpf,
              "post_ffn_g": dgpff, "ple_gate_b": dbg}
    return loss_l, dx, dwin_k, gsmall


def kernel(x, p, w_in, f_bias, sg_ln_g, sg_ln_b, sg_w, sg_b, att_out_g, sg_out_g, w_out, pre_mix_g, post_mix_g, pre_ffn_g, post_ffn_g, w_ff1, w_ff2, ple_w, ple_gate_w, ple_gate_b, loss_target, m_w_in, m_f_bias, m_sg_ln_g, m_sg_ln_b, m_sg_w, m_sg_b, m_att_out_g, m_sg_out_g, m_w_out, m_pre_mix_g, m_post_mix_g, m_pre_ffn_g, m_post_ffn_g, m_w_ff1, m_w_ff2, m_ple_w, m_ple_gate_w, m_ple_gate_b, v_w_in, v_f_bias, v_sg_ln_g, v_sg_ln_b, v_sg_w, v_sg_b, v_att_out_g, v_sg_out_g, v_w_out, v_pre_mix_g, v_post_mix_g, v_pre_ffn_g, v_post_ffn_g, v_w_ff1, v_w_ff2, v_ple_w, v_ple_gate_w, v_ple_gate_b):
    c = lax.axis_index("c")
    big = lambda t: (t[0][0], t[1][0], t[2][0], t[3][0], t[4][0], t[5][0])
    w_big = big((w_in, w_out, w_ff1, w_ff2, ple_w, ple_gate_w))
    m_big = big((m_w_in, m_w_out, m_w_ff1, m_w_ff2, m_ple_w, m_ple_gate_w))
    v_big = big((v_w_in, v_w_out, v_w_ff1, v_w_ff2, v_ple_w, v_ple_gate_w))
    small = {"sg_w": sg_w, "f_bias": f_bias, "sg_ln_g": sg_ln_g, "sg_ln_b": sg_ln_b, "sg_b": sg_b,
             "att_out_g": att_out_g, "sg_out_g": sg_out_g, "pre_mix_g": pre_mix_g, "post_mix_g": post_mix_g,
             "pre_ffn_g": pre_ffn_g, "post_ffn_g": post_ffn_g, "ple_gate_b": ple_gate_b}
    m_small = {"sg_w": m_sg_w, "f_bias": m_f_bias, "sg_ln_g": m_sg_ln_g, "sg_ln_b": m_sg_ln_b, "sg_b": m_sg_b,
               "att_out_g": m_att_out_g, "sg_out_g": m_sg_out_g, "pre_mix_g": m_pre_mix_g,
               "post_mix_g": m_post_mix_g, "pre_ffn_g": m_pre_ffn_g, "post_ffn_g": m_post_ffn_g,
               "ple_gate_b": m_ple_gate_b}
    v_small = {"sg_w": v_sg_w, "f_bias": v_f_bias, "sg_ln_g": v_sg_ln_g, "sg_ln_b": v_sg_ln_b, "sg_b": v_sg_b,
               "att_out_g": v_att_out_g, "sg_out_g": v_sg_out_g, "pre_mix_g": v_pre_mix_g,
               "post_mix_g": v_post_mix_g, "pre_ffn_g": v_pre_ffn_g, "post_ffn_g": v_post_ffn_g,
               "ple_gate_b": v_ple_gate_b}

    k_me = 2 * lax.axis_index("x") + lax.axis_index("y")
    own_slot = lambda got, mine: lax.dynamic_update_slice(got, mine[None], (k_me, 0, 0))
    late_mine = _pack_late(*w_big[1:]).astype(BF16)
    late = _gather_late_start(late_mine)
    win_mine = jnp.pad(w_big[0], ((0, 0), (0, 768 - 642))).reshape(768, 1024).astype(BF16)
    win_k = _win_kernel_order(own_slot(_gather_weights(win_mine), win_mine))
    late_weights = lambda after: _unpack_late(
        own_slot(_gather_late_wait(late[0], late[1], late[2], late[3], after), late_mine))

    names = ("w_in", "w_out", "w_ff1", "w_ff2", "ple_w", "ple_gate_w")
    c1 = jnp.reshape(c, (1,)).astype(jnp.int32)
    own_part = lambda parts, pss: [lax.dynamic_update_slice(pt, lax.dynamic_slice_in_dim(ps, k_me, 1, 0), (k_me, 0, 0))
                                   for pt, ps in zip(parts, pss)]
    tail = {}

    def on_tail_grads(grads):
        dwout, dw1, dw2, dplew, dwg = grads
        gs = [dwout.reshape(4, 256, D), dw1, dw2.reshape(4, D, D), dplew, dwg.reshape(4, 256, D)]
        gots = _swap_halves(gs, "late")
        pss = [_pair_sum(nm, c1, g, got) for nm, g, got in zip(names[1:], gs, gots)]
        tail["xch"] = _exchange_start(pss)
        return tail["xch"][4]

    loss_l, dx, dwin_k, gsmall = _local_step(
        x[0], p[0, 0], loss_target[0], win_k, late_weights, late[4], on_tail_grads, small)

    dwin = jnp.concatenate([dwin_k[:, :3 * AW], dwin_k[:, 5 * AW:5 * AW + NH], dwin_k[:, 3 * AW:5 * AW]], axis=1)
    dwin = jnp.pad(jnp.swapaxes(dwin.reshape(D, 4, 642), 0, 1), ((0, 0), (0, 0), (0, 768 - 642)))
    ps_in = [_pair_sum(names[0], c1, dwin, _swap_halves([dwin], "in")[0])]
    parts = own_part(_exchange_chips(ps_in), ps_in)
    xs, xr, ps_thru, land_thru, _ = tail["xch"]
    ps_late, landed = _exchange_wait(xs, xr, ps_thru, land_thru, dx)
    parts += own_part(landed, ps_late)
    ghs = [_reduce_chips(nm, pt) for nm, pt in zip(names, parts)]
    got2 = _share_grad(ghs, "all")
    padded = lambda t: (jnp.pad(t[0], ((0, 0), (0, 768 - 642))),) + tuple(t[1:])
    big_out = [_update(nm, c1, gh, g2, w, m, v) for nm, gh, g2, w, m, v in
               zip(names, ghs, got2, padded(w_big), padded(m_big), padded(v_big))]
    big_out = [[big_out[j][i][:, :642] if j == 0 else big_out[j][i] for j in range(6)] for i in range(4)]

    view = lambda t: t.reshape(t.shape[-3:]) if t.ndim == 4 else t.reshape(t.shape[-2:])
    views = lambda d: {k: view(d[k]) for k in SMALL_NAMES}
    tot_v, tot_w = _small_allreduce(gsmall, loss_l)
    loss11, res_s = _small_update(tot_v, tot_w, views(small), views(m_small), views(v_small))
    loss = loss11[0, 0]

    def small_out(i, name):
        return res_s[name][i].reshape(small[name].shape)

    order = ["w_in", "f_bias", "sg_ln_g", "sg_ln_b", "sg_w", "sg_b", "att_out_g", "sg_out_g", "w_out",
             "pre_mix_g", "post_mix_g", "pre_ffn_g", "post_ffn_g", "w_ff1", "w_ff2", "ple_w", "ple_gate_w",
             "ple_gate_b"]
    big_idx = {"w_in": 0, "w_out": 1, "w_ff1": 2, "w_ff2": 3, "ple_w": 4, "ple_gate_w": 5}
    outs = [loss, dx[None]]
    for i in range(4):
        for name in order:
            if name in big_idx:
                outs.append(big_out[i][big_idx[name]][None])
            else:
                outs.append(small_out(i, name))
    return tuple(outs)
```

```python
import math

import jax
import jax.numpy as jnp
from jax import lax
from jax.experimental import pallas as pl
from jax.experimental.pallas import tpu as pltpu

F32 = jnp.float32
BF16 = jnp.bfloat16
MESH = pl.DeviceIdType.MESH

D = 1024
DH = 64
NH = 8
AW = 512
CH = 128
DFF = 4096
ZW = 5 * AW + 128
EPS = 1e-6
NEG = -1e30
MASKED = -2e30

TM = 256
TQ = 256

LR, B1, B2, AEPS, WD, STEP = 0.001, 0.9, 0.999, 1e-08, 0.01, 10
BC1 = 1.0 - B1 ** STEP
BC2 = 1.0 - B2 ** STEP

VEC_NAMES = ("f_bias", "sg_ln_g", "sg_ln_b", "sg_b", "att_out_g", "sg_out_g", "pre_mix_g",
             "post_mix_g", "pre_ffn_g", "post_ffn_g", "ple_gate_b")


def _dot(a, b):
    return jnp.dot(a, b, preferred_element_type=F32)


def _dot_nt(a, b):
    return lax.dot_general(a, b, (((1,), (1,)), ((), ())), preferred_element_type=F32)


def _dot_tn(a, b):
    return lax.dot_general(a, b, (((0,), (0,)), ((), ())), preferred_element_type=F32)


def _split3(x):
    h = x.astype(BF16)
    r = x - h.astype(F32)
    m = r.astype(BF16)
    l = (r - m.astype(F32)).astype(BF16)
    return h, m, l


def _dot01(sel, x):
    h, m, l = _split3(x)
    return _dot(sel, h) + _dot(sel, m) + _dot(sel, l)


def _dot01_r(x, sel):
    h, m, l = _split3(x)
    return _dot(h, sel) + _dot(m, sel) + _dot(l, sel)


def _dot01_tn(x, sel):
    h, m, l = _split3(x)
    return _dot_tn(h, sel) + _dot_tn(m, sel) + _dot_tn(l, sel)


def _rs(x, n):
    return lax.rsqrt(jnp.sum(x * x, axis=-1, keepdims=True) * (1.0 / n) + EPS)


def _rms_bwd(dn, x, rs, g, n):
    w = dn * g
    dx = rs * w - x * ((rs * rs * rs) * (1.0 / n) * jnp.sum(w * x, axis=-1, keepdims=True))
    return dx, jnp.sum(dn * x * rs, axis=0, keepdims=True)


_GC = math.sqrt(2.0 / math.pi)


def _gelu(x):
    t = jnp.tanh(_GC * (x + 0.044715 * x * x * x))
    return 0.5 * x * (1.0 + t), t


def _gelu_grad(x, t):
    return 0.5 * (1.0 + t) + 0.5 * x * (1.0 - t * t) * (_GC * (1.0 + 3.0 * 0.044715 * x * x))


def _params(vmem_mb, sem=("arbitrary",)):
    return pltpu.CompilerParams(dimension_semantics=sem, vmem_limit_bytes=vmem_mb * 1024 * 1024)


def _row_call(name, body, T, tm, tiled, resident, outs, accs, scratch=(), reverse=False, vmem_mb=48):
    nt = T // tm
    n_t, n_r, n_o, n_a = len(tiled), len(resident), len(outs), len(accs)

    def kern(*refs):
        t_refs = refs[:n_t]
        r_hbm = refs[n_t:n_t + n_r]
        o_refs = refs[n_t + n_r:n_t + n_r + n_o]
        a_refs = refs[n_t + n_r + n_o:n_t + n_r + n_o + n_a]
        r_vmem = refs[n_t + n_r + n_o + n_a:n_t + 2 * n_r + n_o + n_a]
        s_refs = refs[n_t + 2 * n_r + n_o + n_a:]

        @pl.when(pl.program_id(0) == 0)
        def _():
            for h, v in zip(r_hbm, r_vmem):
                pltpu.sync_copy(h, v)
            for a in a_refs + s_refs:
                a[...] = jnp.zeros(a.shape, a.dtype)

        body(t_refs, r_vmem, o_refs, a_refs, s_refs)

    if reverse:
        idx = lambda i: (nt - 1 - i, 0)
        idx_t = lambda i: (nt - 1 - i, 0, 0)
    else:
        idx = lambda i: (i, 0)
        idx_t = lambda i: (i, 0, 0)
    arrays, in_specs = [], []
    for a in tiled:
        if isinstance(a, tuple):
            arrays.append(a[0])
            in_specs.append(pl.BlockSpec((None, a[0].shape[1], tm), idx_t))
        else:
            arrays.append(a)
            in_specs.append(pl.BlockSpec((tm, a.shape[1]), idx))
    in_specs += [pl.BlockSpec(memory_space=pl.ANY) for _ in resident]
    out_shape, out_specs = [], []
    for o in outs:
        if len(o) == 3:
            out_shape.append(jax.ShapeDtypeStruct((nt, o[0], tm), o[1]))
            out_specs.append(pl.BlockSpec((None, o[0], tm), idx_t))
        else:
            out_shape.append(jax.ShapeDtypeStruct((T, o[0]), o[1]))
            out_specs.append(pl.BlockSpec((tm, o[0]), idx))
    out_shape += [jax.ShapeDtypeStruct(s, F32) for s in accs]
    out_specs += [pl.BlockSpec(s, lambda i, n=len(s): (0,) * n) for s in accs]
    scratch_shapes = [pltpu.VMEM(r.shape, r.dtype) for r in resident]
    scratch_shapes += [pltpu.VMEM(s, F32) for s in scratch]
    return pl.pallas_call(
        kern, name=name, grid=(nt,), in_specs=in_specs, out_specs=out_specs, out_shape=out_shape,
        scratch_shapes=scratch_shapes, compiler_params=_params(vmem_mb),
    )(*arrays, *resident)


def _sg_forward(zu, zv, wm_ref, bsg, lng, lnb, mixed_ref, tm):
    gu, tu = _gelu(zu)
    vg, tv = _gelu(zv)
    mu = jnp.sum(vg, axis=-1, keepdims=True) * (1.0 / AW)
    xc = vg - mu
    rstd = lax.rsqrt(jnp.sum(xc * xc, axis=-1, keepdims=True) * (1.0 / AW) + EPS)
    xhat = xc * rstd
    vvb = (xhat * lng + lnb).astype(BF16)
    lane = lax.broadcasted_iota(jnp.int32, (CH, 128), 1)
    for c in range(tm // CH):
        for j in range(4):
            blk = vvb[c * CH:(c + 1) * CH, j * 128:(j + 1) * 128]
            m0 = _dot(wm_ref[2 * j], blk)
            m1 = _dot(wm_ref[2 * j + 1], blk)
            mixed_ref[c * CH:(c + 1) * CH, j * 128:(j + 1) * 128] = (
                jnp.where(lane < DH, m0, m1) + bsg[:, j * 128:(j + 1) * 128])
    return gu, tu, tv, xhat, rstd, vvb, mixed_ref[...]


def _head_consts():
    src = jnp.arange(AW)
    dst = (src // DH) * 128 + src % DH
    wide = jnp.arange(NH * 128)
    expand = (dst[:, None] == wide[None, :]).astype(BF16)
    heads = jnp.arange(128)
    pieces = jnp.stack([((heads[:, None] * 128 + DH + i == wide[None, :]) & (heads[:, None] < NH)).astype(BF16)
                        for i in range(3)])
    spare = wide % 128 - DH
    qconst = jnp.where((spare >= 0) & (spare < 3), -1.0, 0.0).astype(F32)[None, :]
    one64 = jnp.where(spare == 0, 1.0, 0.0).astype(F32)[None, :]
    one67 = jnp.where(spare == 3, 1.0, 0.0).astype(F32)[None, :]
    pick64 = ((wide[:, None] == heads[None, :] * 128 + DH) & (heads[None, :] < NH)).astype(BF16)
    pick67 = ((wide[:, None] == heads[None, :] * 128 + DH + 3) & (heads[None, :] < NH)).astype(BF16)
    return expand, expand.T, pieces, qconst, one64, one67, pick64, pick67


def _masked_sg_w(sg_w):
    r = lax.broadcasted_iota(jnp.int32, (CH, CH), 0)
    c = lax.broadcasted_iota(jnp.int32, (CH, CH), 1)
    return jnp.where((c <= r)[None], sg_w, 0.0)


def _pre_attn_fwd(x, gpre, win, fbias, lng, lnb, wm, bsg, gsg, expand, pieces, qconst, kconst, vconst):
    T = x.shape[0]
    tm = TM

    def body(t, r, o, a, s):
        (x_ref,) = t
        gpre_r, win_r, fb_r, lng_r, lnb_r, wm_r, bsg_r, gsg_r, ex_r, pc_r, qc_r, kc_r, vc_r = r
        a_o, qkv_o, flog_o, ccol_o, zuv_o, ysgn_o, q8_o, k8_o, v8_o = o
        carry_ref, mixed_ref = s
        xv = x_ref[...]
        av = (xv * _rs(xv, D) * gpre_r[...]).astype(BF16)
        a_o[...] = av
        z = _dot(av, win_r[...])
        zu = z[:, 3 * AW:4 * AW]
        zv = z[:, 4 * AW:5 * AW]
        zuv_o[:, 0:AW] = zu
        zuv_o[:, AW:2 * AW] = zv
        zf = z[:, 5 * AW:] + fb_r[...]
        flog_o[...] = zf
        lane = lax.broadcasted_iota(jnp.int32, (tm, 128), 1)
        logf = jnp.where(lane < NH, jnp.minimum(zf, 0.0) - jnp.log(1.0 + jnp.exp(-jnp.abs(zf))), 0.0)
        rr = lax.broadcasted_iota(jnp.int32, (tm, tm), 0)
        cc = lax.broadcasted_iota(jnp.int32, (tm, tm), 1)
        tri = (cc <= rr).astype(BF16)
        cum = _dot01(tri, logf) + carry_ref[...]
        carry_ref[...] = cum[tm - 1:tm, :]
        ccol_o[...] = cum
        ex = ex_r[...]
        q8_o[...] = (_dot((z[:, 0:AW] * (DH ** -0.5)).astype(BF16), ex) + qc_r[...]).astype(BF16)
        ch, cm, cl = _split3(cum)
        k8_o[...] = (_dot(z[:, AW:2 * AW].astype(BF16), ex) + _dot(ch, pc_r[0]) + _dot(cm, pc_r[1])
                     + _dot(cl, pc_r[2]) + kc_r[...]).astype(BF16)
        v8_o[...] = (_dot(z[:, 2 * AW:3 * AW].astype(BF16), ex) + vc_r[...]).astype(BF16)
        qkv_o[:, 0:AW] = (z[:, 0:AW] * (DH ** -0.5)).astype(BF16)
        qkv_o[:, AW:3 * AW] = z[:, AW:3 * AW].astype(BF16)
        gu, _, _, _, _, _, mixed = _sg_forward(zu, zv, wm_r, bsg_r[...], lng_r[...], lnb_r[...], mixed_ref, tm)
        ysg = gu * mixed
        ysgn_o[...] = (ysg * _rs(ysg, AW) * gsg_r[...]).astype(BF16)

    return _row_call(
        "pre_attn_fwd", body, T, tm, [x],
        [gpre, win, fbias, lng, lnb, wm, bsg, gsg, expand, pieces, qconst, kconst, vconst],
        [(D, BF16), (3 * AW, BF16), (128, F32), (128, F32), (2 * AW, F32), (AW, BF16), (NH * 128, BF16),
         (NH * 128, BF16), (NH * 128, BF16)], [],
        scratch=[(1, 128), (tm, AW)], vmem_mb=48)


def _flash_fwd(qt8, k8, vt8, sel):
    T = k8.shape[0]
    nq = T // TQ

    def body(qt_ref, k_ref, vt_ref, sel_ref, o_ref, l_ref, u_scr, p_scr):
        qi = pl.program_id(1)
        qts = (qt_ref[0:128, :], qt_ref[128:256, :])
        dmat = (lax.broadcasted_iota(jnp.int32, (TQ, TQ), 0) - lax.broadcasted_iota(jnp.int32, (TQ, TQ), 1))
        u_scr[1] = jnp.full((2, TQ, TQ), MASKED, F32)
        p_scr[...] = jnp.zeros(p_scr.shape, BF16)

        def sub(t, carry, sc, sb, masked):
            blk_c = jnp.clip(t - 2, 0, qi)
            off_a = pl.multiple_of(jnp.minimum(t, qi) * TQ, TQ)
            new = []
            for j in (0, 1):
                m, al, acc = carry[j]
                acc = al * acc + _dot(vt_ref[blk_c, j * 128:(j + 1) * 128, :], p_scr[sc, j])
                m_new = jnp.maximum(m, jnp.max(u_scr[sb, j], axis=0, keepdims=True))
                p_scr[sb, j] = jnp.exp(u_scr[sb, j] - m_new).astype(BF16)
                u = _dot(k_ref[pl.ds(off_a, TQ), j * 128:(j + 1) * 128], qts[j])
                u_scr[sc, j] = jnp.where(dmat <= (qi - t) * TQ, u, MASKED) if masked else u
                new.append((m_new, jnp.exp(m - m_new), acc))
            return tuple(new)

        def pair(t2, carry, masked):
            return sub(2 * t2 + 1, sub(2 * t2, carry, 0, 1, masked), 1, 0, masked)

        init = tuple((jnp.full((1, TQ), NEG, F32), jnp.ones((1, TQ), F32), jnp.zeros((128, TQ), F32))
                     for _ in (0, 1))
        carry = lax.fori_loop(0, qi // 2, lambda t2, cr: pair(t2, cr, False), init)
        (m0, _, a0), (m1, _, a1) = pair(qi // 2 + 1, pair(qi // 2, carry, True), True)
        l0 = a0[DH:DH + 1, :]
        l1 = a1[DH:DH + 1, :]
        o_ref[...] = _dot01_tn(a0 * (1.0 / l0), sel_ref[0]) + _dot01_tn(a1 * (1.0 / l1), sel_ref[1])
        l_ref[0:1, :] = m0 + jnp.log(l0)
        l_ref[1:2, :] = m1 + jnp.log(l1)
        l_ref[2:8, :] = jnp.zeros((6, TQ), F32)

    return pl.pallas_call(
        body, name="flash_fwd", grid=(4, nq),
        in_specs=[pl.BlockSpec((None, 256, TQ), lambda h, i: (i, h, 0)),
                  pl.BlockSpec((T, 256), lambda h, i: (0, h)),
                  pl.BlockSpec((nq, 256, TQ), lambda h, i: (0, h, 0)),
                  pl.BlockSpec((2, 128, 128), lambda h, i: (0, 0, 0))],
        out_specs=[pl.BlockSpec((TQ, 128), lambda h, i: (i, h)),
                   pl.BlockSpec((None, 8, TQ), lambda h, i: (h, 0, i))],
        out_shape=[jax.ShapeDtypeStruct((T, AW), F32), jax.ShapeDtypeStruct((4, 8, T), F32)],
        scratch_shapes=[pltpu.VMEM((2, 2, TQ, TQ), F32), pltpu.VMEM((2, 2, TQ, TQ), BF16)],
        compiler_params=_params(40, ("arbitrary", "arbitrary")),
    )(qt8, k8, vt8, sel)


def _flash_bwd(q8, qt8, k8, v8, do8, dot8, lse, dlt):
    T = q8.shape[0]
    nk = T // TQ

    def body(q_ref, qt_ref, k_ref, v_ref, do_ref, dot_ref, l_ref, d_ref, dqt_ref, dk_ref, dv_ref,
             u_scr, dp_scr, p_scr, ds_scr):
        kb = pl.program_id(1)
        n = nk - kb

        @pl.when(kb == 0)
        def _():
            dqt_ref[...] = jnp.zeros(dqt_ref.shape, F32)

        dk_ref[...] = jnp.zeros(dk_ref.shape, F32)
        dv_ref[...] = jnp.zeros(dv_ref.shape, F32)
        u_scr[1] = jnp.full((2, TQ, TQ), MASKED, F32)
        dp_scr[1] = jnp.zeros((2, TQ, TQ), F32)
        p_scr[...] = jnp.zeros(p_scr.shape, BF16)
        ds_scr[...] = jnp.zeros(ds_scr.shape, BF16)
        dmat = (lax.broadcasted_iota(jnp.int32, (TQ, TQ), 0) - lax.broadcasted_iota(jnp.int32, (TQ, TQ), 1))
        ks = (k_ref[:, 0:128], k_ref[:, 128:256])
        vs = (v_ref[:, 0:128], v_ref[:, 128:256])

        def sub(t, sc, sb):
            blk_a = kb + jnp.minimum(t, n - 1)
            blk_c = kb + jnp.clip(t - 2, 0, n - 1)
            off_b = pl.multiple_of((kb + jnp.clip(t - 1, 0, n - 1)) * TQ, TQ)
            off_c = pl.multiple_of(blk_c * TQ, TQ)
            lim = jnp.where(t < n, t * TQ, -TQ)
            for j in (0, 1):
                hl = slice(j * 128, (j + 1) * 128)
                dqt_ref[blk_c, hl, :] += _dot_tn(ks[j], ds_scr[sc, j])
                dk_ref[:, hl] += _dot(ds_scr[sc, j], q_ref[pl.ds(off_c, TQ), hl])
                dv_ref[:, hl] += _dot(p_scr[sc, j], do_ref[pl.ds(off_c, TQ), hl])
                p = jnp.exp(u_scr[sb, j] - l_ref[j:j + 1, pl.ds(off_b, TQ)])
                p_scr[sb, j] = p.astype(BF16)
                ds_scr[sb, j] = (p * (dp_scr[sb, j] - d_ref[j:j + 1, pl.ds(off_b, TQ)])).astype(BF16)
                u_scr[sc, j] = jnp.where(dmat <= lim, _dot(ks[j], qt_ref[blk_a, hl, :]), MASKED)
                dp_scr[sc, j] = _dot(vs[j], dot_ref[blk_a, hl, :])

        def it(t2, carry):
            sub(2 * t2, 0, 1)
            sub(2 * t2 + 1, 1, 0)
            return carry

        lax.fori_loop(0, (n + 3) // 2, it, 0)

    return pl.pallas_call(
        body, name="flash_bwd", grid=(4, nk),
        in_specs=[pl.BlockSpec((T, 256), lambda h, i: (0, h)),
                  pl.BlockSpec((nk, 256, TQ), lambda h, i: (0, h, 0)),
                  pl.BlockSpec((TQ, 256), lambda h, i: (i, h)),
                  pl.BlockSpec((TQ, 256), lambda h, i: (i, h)),
                  pl.BlockSpec((T, 256), lambda h, i: (0, h)),
                  pl.BlockSpec((nk, 256, TQ), lambda h, i: (0, h, 0)),
                  pl.BlockSpec((None, 8, T), lambda h, i: (h, 0, 0)),
                  pl.BlockSpec((None, 8, T), lambda h, i: (h, 0, 0))],
        out_specs=[pl.BlockSpec((nk, 256, TQ), lambda h, i: (0, h, 0)),
                   pl.BlockSpec((TQ, 256), lambda h, i: (i, h)),
                   pl.BlockSpec((TQ, 256), lambda h, i: (i, h))],
        out_shape=[jax.ShapeDtypeStruct((nk, NH * 128, TQ), F32), jax.ShapeDtypeStruct((T, NH * 128), F32),
                   jax.ShapeDtypeStruct((T, NH * 128), F32)],
        scratch_shapes=[pltpu.VMEM((2, 2, TQ, TQ), F32), pltpu.VMEM((2, 2, TQ, TQ), F32),
                        pltpu.VMEM((2, 2, TQ, TQ), BF16), pltpu.VMEM((2, 2, TQ, TQ), BF16)],
        compiler_params=_params(56, ("arbitrary", "arbitrary")),
    )(q8, qt8, k8, v8, do8, dot8, lse, dlt)


def _tail_fwd1(x, yatt, ysgn, gatt, wout, gpm, gpf, w1):
    T = x.shape[0]

    def body(t, r, o, a, s):
        x_ref, ya_ref, ys_ref = t
        gatt_r, wout_r, gpm_r, gpf_r, w1_r = r
        y_o, o_o, h1_o, c2_o, s_o, rr_o = o
        ya = ya_ref[...]
        yan = (ya * _rs(ya, AW) * gatt_r[...]).astype(BF16)
        y_o[:, 0:AW] = yan
        y_o[:, AW:] = ys_ref[...]
        ov = _dot(yan, wout_r[0:AW, :]) + _dot(ys_ref[...], wout_r[AW:, :])
        o_o[...] = ov
        h1 = x_ref[...] + ov * _rs(ov, D) * gpm_r[...]
        h1_o[...] = h1
        c2 = (h1 * _rs(h1, D) * gpf_r[...]).astype(BF16)
        c2_o[...] = c2
        for k in range(4):
            rr = jnp.maximum(_dot(c2, w1_r[k]), 0.0)
            rr_o[:, k * D:(k + 1) * D] = rr.astype(BF16)
            s_o[:, k * D:(k + 1) * D] = (rr * rr).astype(BF16)

    return _row_call(
        "tail_fwd1", body, T, TM, [x, yatt, ysgn], [gatt, wout, gpm, gpf, w1],
        [(D, BF16), (D, F32), (D, F32), (D, BF16), (DFF, BF16), (DFF, BF16)], [], vmem_mb=48)


def _tail_fwd2(sact, h1, p, tgt, w2, gpff, wg, bg, wpe):
    T = h1.shape[0]

    def body(t, r, o, a, s):
        s_ref, h1_ref, p_ref, t_ref = t
        w2_r, gpff_r, wg_r, bg_r, wpe_r = r
        ff_o, h2b_o, de_o, dpre_o, dh2_o = o
        loss_a, dbg_a = a
        ff = _dot(s_ref[...], w2_r[...])
        ff_o[...] = ff
        h2 = h1_ref[...] + ff * _rs(ff, D) * gpff_r[...]
        h2b = h2.astype(BF16)
        h2b_o[...] = h2b
        gate = 1.0 / (1.0 + jnp.exp(-(_dot(h2b, wg_r[...]) + bg_r[...])))
        pb = p_ref[...].astype(BF16)
        e = jnp.concatenate([_dot(pb, wpe_r[k]) for k in range(4)], axis=1)
        diff = h2 + gate * e - t_ref[...]
        loss_a[...] += jnp.sum(diff * diff, axis=0, keepdims=True)
        dh3 = diff * (1.0 / D)
        de_o[...] = (dh3 * gate).astype(BF16)
        dpre = dh3 * e * gate * (1.0 - gate)
        dbg_a[...] += jnp.sum(dpre, axis=0, keepdims=True)
        dpb = dpre.astype(BF16)
        dpre_o[...] = dpb
        dh2_o[...] = dh3 + _dot_nt(dpb, wg_r[...])

    return _row_call(
        "tail_fwd2", body, T, TM, [sact, h1, p, tgt], [w2, gpff, wg, bg, wpe],
        [(D, F32), (D, BF16), (D, BF16), (D, BF16), (D, F32)], [(1, D), (1, D)], vmem_mb=48)


def _tail_bwd(dh2, ff, rr, h1, ov, yatt, w2, w1, wout, gpff, gpf, gpm, gatt, gsel, expand):
    T = dh2.shape[0]

    def body(t, r, o, a, s):
        dh2_ref, ff_ref, rr_ref, h1_ref, o_ref, ya_ref = t
        w2_r, w1_r, wout_r, gpff_r, gpf_r, gpm_r, gatt_r, gsel_r, ex_r = r
        dff_o, dr_o, do_o, do8_o, dlt_o, dysg_o, dh1_o = o
        dgpff_a, dgpf_a, dgpm_a, dgatt_a = a
        dh2v = dh2_ref[...]
        ffv = ff_ref[...]
        dff, dg = _rms_bwd(dh2v, ffv, _rs(ffv, D), gpff_r[...], D)
        dgpff_a[...] += dg
        dffb = dff.astype(BF16)
        dff_o[...] = dffb
        drb = (_dot_nt(dffb, w2_r[...]) * (2.0 * rr_ref[...].astype(F32))).astype(BF16)
        dr_o[...] = drb
        dc2 = _dot_nt(drb[:, 0:D], w1_r[0])
        for k in range(1, 4):
            dc2 = dc2 + _dot_nt(drb[:, k * D:(k + 1) * D], w1_r[k])
        h1v = h1_ref[...]
        d1, dg = _rms_bwd(dc2, h1v, _rs(h1v, D), gpf_r[...], D)
        dgpf_a[...] += dg
        dh1 = dh2v + d1
        dh1_o[...] = dh1
        ovv = o_ref[...]
        dov, dg = _rms_bwd(dh1, ovv, _rs(ovv, D), gpm_r[...], D)
        dgpm_a[...] += dg
        dob = dov.astype(BF16)
        do_o[...] = dob
        dysg_o[...] = _dot_nt(dob, wout_r[AW:, :])
        dyan = _dot_nt(dob, wout_r[0:AW, :])
        ya = ya_ref[...]
        dya, dg = _rms_bwd(dyan, ya, _rs(ya, AW), gatt_r[...], AW)
        dgatt_a[...] += dg
        do8_o[...] = _dot(dya.astype(BF16), ex_r[...]).astype(BF16)
        dlt_o[...] = _dot01_r(dya * ya, gsel_r[...])

    return _row_call(
        "tail_bwd", body, T, TM, [dh2, ff, rr, h1, ov, yatt],
        [w2, w1, wout, gpff, gpf, gpm, gatt, gsel, expand],
        [(D, BF16), (DFF, BF16), (D, BF16), (NH * 128, BF16), (128, F32), (AW, F32), (D, F32)],
        [(1, D), (1, D), (1, D), (1, AW)], vmem_mb=56)


def _pre_attn_bwd(x, dh1, dq8, dk8, dv8, flog, zuv, dysg, gpre, win, lng, lnb, wm, wmt, bsg, gsg, gsel, shrink, pick64, pick67):
    T = x.shape[0]
    tm = TM

    def body(t, r, o, a, s):
        x_ref, dh1_ref, dq_ref, dk_ref, dv_ref, fl_ref, zuv_ref, dys_ref = t
        gpre_r, win_r, lng_r, lnb_r, wm_r, wmt_r, bsg_r, gsg_r, gsel_r, sh_r, p64_r, p67_r = r
        dx_o, dz_o = o
        dgpre_a, dfb_a, dgsg_a, dlng_a, dlnb_a, dws_a, dbs_a, dsb_a = a
        carry_ref, mixed_ref, dvv_ref = s
        dq8v = dq_ref[...]
        dk8v = dk_ref[...]
        dcv = _dot01_r(dq8v, p67_r[...]) + _dot01_r(dk8v, p64_r[...])
        rr = lax.broadcasted_iota(jnp.int32, (tm, tm), 0)
        cc = lax.broadcasted_iota(jnp.int32, (tm, tm), 1)
        triu = (cc >= rr).astype(BF16)
        dlogf = _dot01(triu, dcv) + carry_ref[...]
        carry_ref[...] = dlogf[0:1, :]
        dzf = dlogf * (1.0 / (1.0 + jnp.exp(fl_ref[...])))
        dfb_a[...] += jnp.sum(dzf, axis=0, keepdims=True)
        dz_o[:, 5 * AW:] = dzf.astype(BF16)
        zu = zuv_ref[:, 0:AW]
        zv = zuv_ref[:, AW:]
        gu, tu, tv, xhat, rstd, vvb, mixed = _sg_forward(
            zu, zv, wm_r, bsg_r[...], lng_r[...], lnb_r[...], mixed_ref, tm)
        ysg = gu * mixed
        dysg_n = dys_ref[...]
        dys, dg = _rms_bwd(dysg_n, ysg, _rs(ysg, AW), gsg_r[...], AW)
        dgsg_a[...] += dg
        dgu = dys * mixed
        dmix = dys * gu
        dmb = dmix.astype(BF16)
        lane = lax.broadcasted_iota(jnp.int32, (CH, 128), 1)
        lo = lane < DH
        for c in range(tm // CH):
            rows = slice(c * CH, (c + 1) * CH)
            dbs_a[...] += dmix[rows, :]
            for j in range(4):
                cols = slice(j * 128, (j + 1) * 128)
                dmblk = dmb[rows, cols]
                vblk = vvb[rows, cols]
                d0 = _dot(wmt_r[2 * j], dmblk)
                d1 = _dot(wmt_r[2 * j + 1], dmblk)
                dvv_ref[rows, cols] = jnp.where(lo, d0, d1)
                dws_a[2 * j] += _dot_nt(jnp.where(lo, dmblk, jnp.zeros_like(dmblk)), vblk)
                dws_a[2 * j + 1] += _dot_nt(jnp.where(lo, jnp.zeros_like(dmblk), dmblk), vblk)
        dvv = dvv_ref[...]
        dlng_a[...] += jnp.sum(dvv * xhat, axis=0, keepdims=True)
        dlnb_a[...] += jnp.sum(dvv, axis=0, keepdims=True)
        dxh = dvv * lng_r[...]
        dvg = rstd * (dxh - jnp.sum(dxh, axis=-1, keepdims=True) * (1.0 / AW)
                      - xhat * (jnp.sum(dxh * xhat, axis=-1, keepdims=True) * (1.0 / AW)))
        dz_o[:, 3 * AW:4 * AW] = (dgu * _gelu_grad(zu, tu)).astype(BF16)
        dz_o[:, 4 * AW:5 * AW] = (dvg * _gelu_grad(zv, tv)).astype(BF16)
        dz_o[:, 0:AW] = _dot((dq8v * (DH ** -0.5)).astype(BF16), sh_r[...]).astype(BF16)
        dz_o[:, AW:2 * AW] = _dot(dk8v.astype(BF16), sh_r[...]).astype(BF16)
        dz_o[:, 2 * AW:3 * AW] = _dot(dv_ref[...].astype(BF16), sh_r[...]).astype(BF16)
        da = _dot_nt(dz_o[...], win_r[...])
        xv = x_ref[...]
        dxa, dg = _rms_bwd(da, xv, _rs(xv, D), gpre_r[...], D)
        dgpre_a[...] += dg
        dx_o[...] = dh1_ref[...] + dxa

        @pl.when(pl.program_id(0) == T // tm - 1)
        def _():
            dsb_a[...] = _dot01_r(dbs_a[...], gsel_r[...])

    outs = _row_call(
        "pre_attn_bwd", body, T, tm, [x, dh1, dq8, dk8, dv8, flog, zuv, dysg],
        [gpre, win, lng, lnb, wm, wmt, bsg, gsg, gsel, shrink, pick64, pick67],
        [(D, F32), (ZW, BF16)],
        [(1, D), (1, 128), (1, AW), (1, AW), (1, AW), (8, CH, CH), (CH, AW), (CH, 128)],
        scratch=[(1, 128), (tm, AW), (tm, AW)], reverse=True, vmem_mb=48)
    return outs


def _matmul_tn(name, a, b, tn=512, tt=2048, shards=1):
    T, K = a.shape
    N = b.shape[1]
    tk = min(K, 1024)
    tn = min(tn, N // shards)
    tt = min(tt, T)
    nj = N // shards // tn

    def body(a_ref, b_ref, o_ref):
        @pl.when(pl.program_id(2) == 0)
        def _():
            o_ref[...] = jnp.zeros(o_ref.shape, F32)

        o_ref[...] += _dot_tn(a_ref[...].astype(BF16), b_ref[...].astype(BF16))

    if shards == 1:
        out_shape = jax.ShapeDtypeStruct((K, N), F32)
        out_spec = pl.BlockSpec((tk, tn), lambda i, j, t: (i, j))
    else:
        out_shape = jax.ShapeDtypeStruct((shards, K, N // shards), F32)
        out_spec = pl.BlockSpec((None, tk, tn), lambda i, j, t: (j // nj, i, j % nj))
    return pl.pallas_call(
        body, name=name, grid=(K // tk, N // tn, T // tt),
        in_specs=[pl.BlockSpec((tt, tk), lambda i, j, t: (t, i)),
                  pl.BlockSpec((tt, tn), lambda i, j, t: (t, j))],
        out_specs=out_spec, out_shape=out_shape,
        compiler_params=_params(40, ("arbitrary", "arbitrary", "arbitrary")),
    )(a, b)


def _me():
    return lax.axis_index("x"), lax.axis_index("y"), lax.axis_index("c")


HBM_SPEC = pl.BlockSpec(memory_space=pltpu.HBM)


def _gather_weights(mine):
    half = mine.shape[0] // 2

    def body(mine_ref, out_ref, ici_send, ici_recv, d2d_send, d2d_recv):
        x, y, c = _me()
        k_me = 2 * x + y
        chips = [(1 - x, y), (x, 1 - y), (1 - x, 1 - y)]
        my_rows = pl.ds(pl.multiple_of(c * half, 16), half)
        sib_rows = pl.ds(pl.multiple_of((1 - c) * half, 16), half)

        def over_ici(j, k, to):
            src = mine_ref.at[my_rows] if k is None else out_ref.at[k, my_rows]
            return pltpu.make_async_remote_copy(
                src_ref=src, dst_ref=out_ref.at[k_me if k is None else k, my_rows], send_sem=ici_send.at[j],
                recv_sem=ici_recv.at[j], device_id=to, device_id_type=MESH)

        def over_d2d(j, k, rows):
            return pltpu.make_async_remote_copy(
                src_ref=out_ref.at[k, rows], dst_ref=out_ref.at[k, rows], send_sem=d2d_send.at[j],
                recv_sem=d2d_recv.at[j], device_id=(x, y, 1 - c), device_id_type=MESH)

        first = [over_ici(j, None, (cx, cy, c)) for j, (cx, cy) in enumerate(chips)]
        for cp in first:
            cp.start()
        passed = [over_d2d(j, 2 * cx + cy, my_rows) for j, (cx, cy) in enumerate(chips)]
        for j, (cx, cy) in enumerate(chips):
            over_ici(j, 2 * cx + cy, (cx, cy, c)).wait_recv()
            passed[j].start()
        for j, (cx, cy) in enumerate(chips):
            over_d2d(j, 2 * cx + cy, sib_rows).wait_recv()
        for cp in first + passed:
            cp.wait_send()

    return pl.pallas_call(
        body, name="gather_weights", in_specs=[HBM_SPEC], out_specs=HBM_SPEC,
        out_shape=jax.ShapeDtypeStruct((4,) + mine.shape, mine.dtype),
        scratch_shapes=[pltpu.SemaphoreType.DMA((3,)), pltpu.SemaphoreType.DMA((3,)), pltpu.SemaphoreType.DMA((3,)),
                        pltpu.SemaphoreType.DMA((3,))],
    )(mine)


SEM_SPEC = pl.BlockSpec(memory_space=pltpu.SEMAPHORE)
EFFECT = pltpu.SideEffectType.DATAFLOW_SIDE_EFFECTING


def _gather_late_start(mine):
    def body(mine_ref, land_ref, send_sems, recv_sems, mine_thru, land_thru, token):
        x, y, c = _me()
        k_me = 2 * x + y
        for j, (cx, cy) in enumerate([(1 - x, y), (x, 1 - y), (1 - x, 1 - y)]):
            pltpu.make_async_remote_copy(
                src_ref=mine_ref, dst_ref=land_ref.at[k_me], send_sem=send_sems.at[j], recv_sem=recv_sems.at[j],
                device_id=(cx, cy, c), device_id_type=MESH).start()
        token[...] = jnp.zeros(token.shape, F32)

    land = lax.empty((4,) + mine.shape, mine.dtype)
    return pl.pallas_call(
        body, name="gather_late_start",
        out_shape=(pltpu.SemaphoreType.DMA((3,)), pltpu.SemaphoreType.DMA((3,)), pltpu.HBM(mine.shape, mine.dtype),
                   pltpu.HBM(land.shape, land.dtype), jax.ShapeDtypeStruct((8, 128), F32)),
        in_specs=(HBM_SPEC, HBM_SPEC),
        out_specs=(SEM_SPEC, SEM_SPEC, HBM_SPEC, HBM_SPEC, pl.BlockSpec(memory_space=pltpu.VMEM)),
        input_output_aliases={0: 2, 1: 3},
        compiler_params=pltpu.CompilerParams(has_side_effects=EFFECT),
    )(pltpu.with_memory_space_constraint(mine, pltpu.HBM), pltpu.with_memory_space_constraint(land, pltpu.HBM))


def _gather_late_wait(send_sems, recv_sems, mine_thru, land_thru, after):
    def body(mine_ref, land_ref, send_sems, recv_sems, after_ref, mine_dead, got_ref):
        x, y, c = _me()
        for j, (cx, cy) in enumerate([(1 - x, y), (x, 1 - y), (1 - x, 1 - y)]):
            cp = pltpu.make_async_remote_copy(
                src_ref=mine_ref, dst_ref=land_ref.at[2 * cx + cy], send_sem=send_sems.at[j],
                recv_sem=recv_sems.at[j], device_id=(cx, cy, c), device_id_type=MESH)
            cp.wait_send()
            cp.wait_recv()

    return pl.pallas_call(
        body, name="gather_late_wait",
        out_shape=(pltpu.HBM(mine_thru.shape, mine_thru.dtype), pltpu.HBM(land_thru.shape, land_thru.dtype)),
        in_specs=(HBM_SPEC, HBM_SPEC, SEM_SPEC, SEM_SPEC, pl.BlockSpec(memory_space=pl.ANY)),
        out_specs=(HBM_SPEC, HBM_SPEC), input_output_aliases={0: 0, 1: 1},
        compiler_params=pltpu.CompilerParams(has_side_effects=EFFECT),
    )(mine_thru, land_thru, send_sems, recv_sems, after)[1]


def _swap_halves(gs, tag):
    n = len(gs)

    def body(*refs):
        g_refs, got_refs, send_sems, recv_sems = refs[:n], refs[n:2 * n], refs[2 * n], refs[2 * n + 1]
        x, y, c = _me()
        cps = []
        for i, (g_ref, got_ref) in enumerate(zip(g_refs, got_refs)):
            half = g_ref.shape[1] // 2
            theirs = pl.multiple_of((1 - c) * half, 16)
            cps.append(pltpu.make_async_remote_copy(
                src_ref=g_ref.at[:, pl.ds(theirs, half), :], dst_ref=got_ref, send_sem=send_sems.at[i],
                recv_sem=recv_sems.at[i], device_id=(x, y, 1 - c), device_id_type=MESH))
        for cp in cps:
            cp.start()
        for cp in cps:
            cp.wait()

    return pl.pallas_call(
        body, name="swap_halves_" + tag, in_specs=[HBM_SPEC] * n, out_specs=[HBM_SPEC] * n,
        out_shape=[jax.ShapeDtypeStruct((4, g.shape[1] // 2, g.shape[2]), F32) for g in gs],
        scratch_shapes=[pltpu.SemaphoreType.DMA((n,)), pltpu.SemaphoreType.DMA((n,))],
    )(*gs)


def _pair_sum(name, c1, g, got):
    half, cols = got.shape[1], got.shape[2]

    def body(c_ref, a_ref, b_ref, o_ref):
        o_ref[...] = (a_ref[...] + b_ref[...]).astype(BF16)

    return pl.pallas_call(
        body, name="pair_sum_" + name,
        grid_spec=pltpu.PrefetchScalarGridSpec(
            num_scalar_prefetch=1, grid=(4,),
            in_specs=[pl.BlockSpec((1, half, cols), lambda k, c_ref: (k, c_ref[0], 0)),
                      pl.BlockSpec((1, half, cols), lambda k, c_ref: (k, 0, 0))],
            out_specs=pl.BlockSpec((1, half, cols), lambda k, c_ref: (k, 0, 0))),
        out_shape=jax.ShapeDtypeStruct(got.shape, BF16), compiler_params=_params(32),
    )(c1, g, got)


def _exchange_chips(pss):
    n = len(pss)

    def body(*refs):
        ps_refs, out_refs = refs[:n], refs[n:2 * n]
        send_sems, recv_sems = refs[2 * n:]
        x, y, c = _me()
        k_me = 2 * x + y
        chips = [(1 - x, y), (x, 1 - y), (1 - x, 1 - y)]
        sends = []
        for i, (ps_ref, out_ref) in enumerate(zip(ps_refs, out_refs)):
            for j, (cx, cy) in enumerate(chips):
                sends.append(pltpu.make_async_remote_copy(
                    src_ref=ps_ref.at[2 * cx + cy], dst_ref=out_ref.at[k_me], send_sem=send_sems.at[3 * i + j],
                    recv_sem=recv_sems.at[3 * i + j], device_id=(cx, cy, c), device_id_type=MESH))
        for cp in sends:
            cp.start()
        for i, (ps_ref, out_ref) in enumerate(zip(ps_refs, out_refs)):
            for j, (cx, cy) in enumerate(chips):
                pltpu.make_async_remote_copy(
                    src_ref=ps_ref.at[k_me], dst_ref=out_ref.at[2 * cx + cy], send_sem=send_sems.at[3 * i + j],
                    recv_sem=recv_sems.at[3 * i + j], device_id=(cx, cy, c), device_id_type=MESH).wait_recv()
        for cp in sends:
            cp.wait_send()

    return pl.pallas_call(
        body, name="exchange_chips", in_specs=[HBM_SPEC] * n, out_specs=[HBM_SPEC] * n,
        out_shape=[jax.ShapeDtypeStruct(ps.shape, ps.dtype) for ps in pss],
        scratch_shapes=[pltpu.SemaphoreType.DMA((3 * n,)), pltpu.SemaphoreType.DMA((3 * n,))],
    )(*pss)


def _exchange_start(pss):
    n = len(pss)

    def body(*refs):
        ps_refs, land_refs = refs[:n], refs[n:2 * n]
        send_sems, recv_sems = refs[2 * n], refs[2 * n + 1]
        token = refs[4 * n + 2]
        x, y, c = _me()
        k_me = 2 * x + y
        for i, (ps_ref, land_ref) in enumerate(zip(ps_refs, land_refs)):
            for j, (cx, cy) in enumerate([(1 - x, y), (x, 1 - y), (1 - x, 1 - y)]):
                pltpu.make_async_remote_copy(
                    src_ref=ps_ref.at[2 * cx + cy], dst_ref=land_ref.at[k_me], send_sem=send_sems.at[3 * i + j],
                    recv_sem=recv_sems.at[3 * i + j], device_id=(cx, cy, c), device_id_type=MESH).start()
        token[...] = jnp.zeros(token.shape, F32)

    lands = [lax.empty(ps.shape, ps.dtype) for ps in pss]
    hbm = lambda t: pltpu.HBM(t.shape, t.dtype)
    res = pl.pallas_call(
        body, name="exchange_start",
        out_shape=(pltpu.SemaphoreType.DMA((3 * n,)), pltpu.SemaphoreType.DMA((3 * n,)), *[hbm(t) for t in pss],
                   *[hbm(t) for t in lands], jax.ShapeDtypeStruct((8, 128), F32)),
        in_specs=(HBM_SPEC,) * (2 * n),
        out_specs=(SEM_SPEC, SEM_SPEC) + (HBM_SPEC,) * (2 * n) + (pl.BlockSpec(memory_space=pltpu.VMEM),),
        input_output_aliases={i: 2 + i for i in range(2 * n)},
        compiler_params=pltpu.CompilerParams(has_side_effects=EFFECT),
    )(*[pltpu.with_memory_space_constraint(t, pltpu.HBM) for t in list(pss) + lands])
    return res[0], res[1], res[2:2 + n], res[2 + n:2 + 2 * n], res[2 + 2 * n]


def _exchange_wait(send_sems, recv_sems, ps_thru, land_thru, after):
    n = len(ps_thru)

    def body(*refs):
        ps_refs, land_refs = refs[:n], refs[n:2 * n]
        send_sems, recv_sems = refs[2 * n], refs[2 * n + 1]
        x, y, c = _me()
        k_me = 2 * x + y
        for i, (ps_ref, land_ref) in enumerate(zip(ps_refs, land_refs)):
            for j, (cx, cy) in enumerate([(1 - x, y), (x, 1 - y), (1 - x, 1 - y)]):
                cp = pltpu.make_async_remote_copy(
                    src_ref=ps_ref.at[k_me], dst_ref=land_ref.at[2 * cx + cy], send_sem=send_sems.at[3 * i + j],
                    recv_sem=recv_sems.at[3 * i + j], device_id=(cx, cy, c), device_id_type=MESH)
                cp.wait_send()
                cp.wait_recv()

    hbm = lambda t: pltpu.HBM(t.shape, t.dtype)
    res = pl.pallas_call(
        body, name="exchange_wait",
        out_shape=tuple(hbm(t) for t in list(ps_thru) + list(land_thru)),
        in_specs=(HBM_SPEC,) * (2 * n) + (SEM_SPEC, SEM_SPEC, pl.BlockSpec(memory_space=pl.ANY)),
        out_specs=(HBM_SPEC,) * (2 * n), input_output_aliases={i: i for i in range(2 * n)},
        compiler_params=pltpu.CompilerParams(has_side_effects=EFFECT),
    )(*ps_thru, *land_thru, send_sems, recv_sems, after)
    return res[:n], res[n:]


def _adamw(w, g, m, v):
    m = B1 * m + (1.0 - B1) * g
    v = B2 * v + (1.0 - B2) * (g * g)
    delta = -LR * ((m / BC1) / (jnp.sqrt(v / BC2) + AEPS) + WD * w)
    return delta, m, v


def _reduce_chips(name, parts):
    half, cols = parts.shape[1], parts.shape[2]

    def body(p_ref, o_ref):
        f = lambda k: p_ref[k].astype(F32)
        o_ref[...] = ((f(0) + f(1)) + f(2)) + f(3)

    return pl.pallas_call(
        body, name="reduce_chips_" + name, grid=(1,),
        in_specs=[pl.BlockSpec((4, half, cols), lambda i: (0, 0, 0))],
        out_specs=pl.BlockSpec((half, cols), lambda i: (0, 0)),
        out_shape=jax.ShapeDtypeStruct((half, cols), F32), compiler_params=_params(32),
    )(parts)


def _share_grad(ghs, tag):
    n = len(ghs)

    def body(*refs):
        g_refs, got_refs, send_sems, recv_sems = refs[:n], refs[n:2 * n], refs[2 * n], refs[2 * n + 1]
        x, y, c = _me()
        cps = [pltpu.make_async_remote_copy(
            src_ref=g_ref, dst_ref=got_ref, send_sem=send_sems.at[i], recv_sem=recv_sems.at[i],
            device_id=(x, y, 1 - c), device_id_type=MESH) for i, (g_ref, got_ref) in enumerate(zip(g_refs, got_refs))]
        for cp in cps:
            cp.start()
        for cp in cps:
            cp.wait()

    return pl.pallas_call(
        body, name="share_grad_" + tag, in_specs=[HBM_SPEC] * n, out_specs=[HBM_SPEC] * n,
        out_shape=[jax.ShapeDtypeStruct(g.shape, F32) for g in ghs],
        scratch_shapes=[pltpu.SemaphoreType.DMA((n,)), pltpu.SemaphoreType.DMA((n,))],
    )(*ghs)


def _update(name, c1, gh, got, w, m, v):
    half, cols = gh.shape

    def body(c_ref, gh_ref, got_ref, w_ref, m_ref, v_ref, g_o, d_o, m_o, v_o):
        g = jnp.where(pl.program_id(0) == c_ref[0], gh_ref[...], got_ref[...])
        delta, mn, vn = _adamw(w_ref[...], g, m_ref[...], v_ref[...])
        g_o[...] = g
        d_o[...] = delta
        m_o[...] = mn
        v_o[...] = vn

    same = pl.BlockSpec((half, cols), lambda h, c_ref: (0, 0))
    rows = pl.BlockSpec((half, cols), lambda h, c_ref: (h, 0))
    return pl.pallas_call(
        body, name="update_" + name,
        grid_spec=pltpu.PrefetchScalarGridSpec(
            num_scalar_prefetch=1, grid=(2,), in_specs=[same, same, rows, rows, rows],
            out_specs=[rows, rows, rows, rows]),
        out_shape=[jax.ShapeDtypeStruct(w.shape, F32)] * 4, compiler_params=_params(40),
    )(c1, gh, got, w, m, v)


SMALL_NAMES = ("sg_w",) + VEC_NAMES
VEC_ROWS = 24
VEC_ROW = {"f_bias": 0, "sg_ln_g": 1, "sg_ln_b": 2, "att_out_g": 3, "sg_out_g": 4, "pre_mix_g": 5,
           "post_mix_g": 6, "pre_ffn_g": 7, "sg_b": 8, "post_ffn_g": 16, "ple_gate_b": 17}
LOSS_VEC_ROW = 18


def _small_allreduce(g, loss_l):
    n = len(SMALL_NAMES)

    def body(*refs):
        g_r = dict(zip(SMALL_NAMES, refs[0:n]))
        loss_r, totv_o, totw_o, bufv, bufw, send_sems, recv_sems = refs[n:]
        x, y, c = _me()
        me = 4 * x + 2 * y + c
        bufv[me] = jnp.zeros((VEC_ROWS, 1024), F32)
        for name in VEC_NAMES:
            val = g_r[name][...]
            bufv[me, pl.ds(VEC_ROW[name], val.shape[0]), pl.ds(0, val.shape[1])] = val
        bufv[me, pl.ds(LOSS_VEC_ROW, 1), :] = loss_r[...] * (0.5 / D)
        rr = lax.broadcasted_iota(jnp.int32, (CH, CH), 0)
        cc = lax.broadcasted_iota(jnp.int32, (CH, CH), 1)
        bufw[me] = jnp.where((cc <= rr)[None], g_r["sg_w"][...], 0.0)

        rels = [(rx, ry, rc) for rx in (0, 1) for ry in (0, 1) for rc in (0, 1)][1:]

        def peer(r):
            return ((x + r[0]) % 2, (y + r[1]) % 2, (c + r[2]) % 2)

        def copies(j, slot, to):
            return [pltpu.make_async_remote_copy(
                src_ref=buf.at[slot], dst_ref=buf.at[slot], send_sem=send_sems.at[2 * j + i],
                recv_sem=recv_sems.at[2 * j + i], device_id=to, device_id_type=MESH)
                for i, buf in enumerate((bufv, bufw))]

        sends = [cp for j, r in enumerate(rels) for cp in copies(j, me, peer(r))]
        for cp in sends:
            cp.start()
        for j, r in enumerate(rels):
            px, py, pc = peer(r)
            for cp in copies(j, 4 * px + 2 * py + pc, peer(r)):
                cp.wait_recv()
        for cp in sends:
            cp.wait_send()

        tot_v = bufv[0]
        tot_w = bufw[0]
        for d in range(1, 8):
            tot_v = tot_v + bufv[d]
            tot_w = tot_w + bufw[d]
        totv_o[...] = tot_v
        totw_o[...] = tot_w

    vm = pl.BlockSpec(memory_space=pltpu.VMEM)
    args = [g[k] for k in SMALL_NAMES] + [loss_l]
    return pl.pallas_call(
        body, name="small_allreduce", in_specs=[vm] * len(args), out_specs=[vm, vm],
        out_shape=[jax.ShapeDtypeStruct((VEC_ROWS, 1024), F32), jax.ShapeDtypeStruct((8, CH, CH), F32)],
        scratch_shapes=[pltpu.VMEM((8, VEC_ROWS, 1024), F32), pltpu.VMEM((8, 8, CH, CH), F32),
                        pltpu.SemaphoreType.DMA((14,)), pltpu.SemaphoreType.DMA((14,))],
        compiler_params=pltpu.CompilerParams(vmem_limit_bytes=32 * 1024 * 1024),
    )(*args)


def _small_update(tot_v, tot_w, w, m, v):
    n = len(SMALL_NAMES)

    def body(*refs):
        totv_r, totw_r = refs[0], refs[1]
        w_r = dict(zip(SMALL_NAMES, refs[2:2 + n]))
        m_r = dict(zip(SMALL_NAMES, refs[2 + n:2 + 2 * n]))
        v_r = dict(zip(SMALL_NAMES, refs[2 + 2 * n:2 + 3 * n]))
        loss_o = refs[2 + 3 * n]
        outs = refs[3 + 3 * n:]
        loss_o[...] = jnp.sum(totv_r[LOSS_VEC_ROW:LOSS_VEC_ROW + 1, :], axis=-1, keepdims=True) + jnp.zeros((1, 128), F32)
        for i, name in enumerate(SMALL_NAMES):
            if name == "sg_w":
                gt = totw_r[...]
            else:
                rows, width = w_r[name].shape
                gt = totv_r[VEC_ROW[name]:VEC_ROW[name] + rows, 0:width]
            delta, mn, vn = _adamw(w_r[name][...], gt, m_r[name][...], v_r[name][...])
            outs[4 * i][...] = gt
            outs[4 * i + 1][...] = delta
            outs[4 * i + 2][...] = mn
            outs[4 * i + 3][...] = vn

    vm = pl.BlockSpec(memory_space=pltpu.VMEM)
    args = [tot_v, tot_w] + [d[k] for d in (w, m, v) for k in SMALL_NAMES]
    out_shape = [jax.ShapeDtypeStruct((1, 128), F32)]
    out_shape += [jax.ShapeDtypeStruct(w[k].shape, F32) for k in SMALL_NAMES for _ in range(4)]
    res = pl.pallas_call(
        body, name="small_update", in_specs=[vm] * len(args), out_specs=[vm] * len(out_shape), out_shape=out_shape,
        compiler_params=pltpu.CompilerParams(vmem_limit_bytes=32 * 1024 * 1024),
    )(*args)
    return res[0], {k: res[1 + 4 * i:5 + 4 * i] for i, k in enumerate(SMALL_NAMES)}


def _win_kernel_order(gathered):
    w_in = jnp.concatenate([gathered[k].reshape(D, 768)[:, :642] for k in range(4)], axis=1)
    return jnp.concatenate([w_in[:, :3 * AW], w_in[:, 3 * AW + NH:], w_in[:, 3 * AW:3 * AW + NH],
                            jnp.zeros((D, 128 - NH), w_in.dtype)], axis=1)


LATE_ROWS = 256 + 1024 + 1024 + 64 + 256


def _pack_late(w_out, w1, w2, plew, wg):
    return jnp.concatenate([w_out, w1, w2, plew.reshape(64, 1024), wg], axis=0)


def _unpack_late(gathered):
    return (gathered[:, 0:256].reshape(D, D), gathered[:, 256:1280], gathered[:, 1280:2304].reshape(DFF, D),
            gathered[:, 2304:2368].reshape(4, 256, 256), gathered[:, 2368:2624].reshape(D, D))


def _local_step(x, p, tgt, win_k, late_weights, token, on_tail_grads, small):
    T = x.shape[0]
    row = lambda n: small[n].reshape(1, -1)
    fbias = jnp.pad(row("f_bias"), ((0, 0), (0, 128 - NH))) + token[0:1, :]
    wm = _masked_sg_w(small["sg_w"].reshape(8, CH, CH))
    wmb = wm.astype(BF16)
    wmt = jnp.swapaxes(wm, 1, 2).astype(BF16)
    bsg = jnp.repeat(small["sg_b"].reshape(8, CH).T, DH, axis=1)
    ln_g, ln_b, gsg, gatt = row("sg_ln_g"), row("sg_ln_b"), row("sg_out_g"), row("att_out_g")
    gpre, gpm, gpf, gpff, bg = row("pre_mix_g"), row("post_mix_g"), row("pre_ffn_g"), row("post_ffn_g"), row("ple_gate_b")
    gsel = (jnp.arange(AW)[:, None] // DH == jnp.arange(128)[None, :]).astype(BF16)

    expand, shrink, pieces, qconst, one64, one67, pick64, pick67 = _head_consts()
    a, qkv, flog, ccol, zuv, ysgn, q8, k8, v8 = _pre_attn_fwd(
        x, gpre, win_k, fbias, ln_g, ln_b, wmb, bsg, gsg, expand, pieces, qconst, one67, one64)

    slabs = lambda t: jnp.swapaxes(t.reshape(T // TQ, TQ, NH * 128), 1, 2)
    qt8 = slabs(q8)
    lanes = jnp.arange(128)
    sel = jnp.stack([((lanes[:, None] == lanes[None, :] - DH * j) & (lanes[:, None] < DH)).astype(BF16)
                     for j in (0, 1)])

    yatt, lse = _flash_fwd(qt8, k8, slabs(v8), sel)
    wout, w1, w2, plew, wg = late_weights(lse)
    y, ov, h1, c2, sact, rr = _tail_fwd1(x, yatt, ysgn, gatt, wout, gpm, gpf, w1)
    ff, h2b, de, dpre, dh2, loss_l, dbg = _tail_fwd2(sact, h1, p, tgt, w2, gpff, wg, bg, plew)
    dff, dr, do, do8, dlt, dysg, dh1, dgpff, dgpf, dgpm, dgatt = _tail_bwd(
        dh2, ff, rr, h1, ov, yatt, w2, w1, wout, gpff, gpf, gpm, gatt, gsel, expand)
    dwout = _matmul_tn("grad_w_out", y, do)
    dw1 = _matmul_tn("grad_w_ff1", c2, dr, shards=4)
    dw2 = _matmul_tn("grad_w_ff2", sact, dff)
    dwg = _matmul_tn("grad_ple_gate_w", h2b, dpre)
    dplew = _matmul_tn("grad_ple_w", p, de, tn=256, shards=4)
    tail_token = on_tail_grads((dwout, dw1, dw2, dplew, dwg))
    dlt4 = jnp.pad(dlt[:, :NH].T.reshape(4, 2, T), ((0, 0), (0, 6), (0, 0))) + tail_token[0, 0]
    dqt, dk8, dv8 = _flash_bwd(q8, qt8, k8, v8, do8, slabs(do8), lse, dlt4)
    dx, dz, dgpre, dfb, dgsg, dlng, dlnb, dws, _, dsbt = _pre_attn_bwd(
        x, dh1, jnp.swapaxes(dqt, 1, 2).reshape(T, NH * 128), dk8, dv8, flog, zuv, dysg,
        gpre, win_k, ln_g, ln_b, wmb, wmt, bsg, gsg, gsel, shrink, pick64, pick67)

    dwin_k = _matmul_tn("grad_w_in", a, dz, tn=384)

    dsb = dsbt[:, :8].T
    gsmall = {"sg_w": dws, "f_bias": dfb, "sg_ln_g": dlng, "sg_ln_b": dlnb, "sg_b": dsb,
              "att_out_g": dgatt, "sg_out_g": dgsg, "pre_mix_g": dgpre, "post_mix_g": dgpm, "pre_ffn_g": dgpf,
              "post_ffn_g": dgpff, "ple_gate_b": dbg}
    return loss_l, dx, dwin_k, gsmall


def kernel(x, p, w_in, f_bias, sg_ln_g, sg_ln_b, sg_w, sg_b, att_out_g, sg_out_g, w_out, pre_mix_g, post_mix_g, pre_ffn_g, post_ffn_g, w_ff1, w_ff2, ple_w, ple_gate_w, ple_gate_b, loss_target, m_w_in, m_f_bias, m_sg_ln_g, m_sg_ln_b, m_sg_w, m_sg_b, m_att_out_g, m_sg_out_g, m_w_out, m_pre_mix_g, m_post_mix_g, m_pre_ffn_g, m_post_ffn_g, m_w_ff1, m_w_ff2, m_ple_w, m_ple_gate_w, m_ple_gate_b, v_w_in, v_f_bias, v_sg_ln_g, v_sg_ln_b, v_sg_w, v_sg_b, v_att_out_g, v_sg_out_g, v_w_out, v_pre_mix_g, v_post_mix_g, v_pre_ffn_g, v_post_ffn_g, v_w_ff1, v_w_ff2, v_ple_w, v_ple_gate_w, v_ple_gate_b):
    c = lax.axis_index("c")
    big = lambda t: (t[0][0], t[1][0], t[2][0], t[3][0], t[4][0], t[5][0])
    w_big = big((w_in, w_out, w_ff1, w_ff2, ple_w, ple_gate_w))
    m_big = big((m_w_in, m_w_out, m_w_ff1, m_w_ff2, m_ple_w, m_ple_gate_w))
    v_big = big((v_w_in, v_w_out, v_w_ff1, v_w_ff2, v_ple_w, v_ple_gate_w))
    small = {"sg_w": sg_w, "f_bias": f_bias, "sg_ln_g": sg_ln_g, "sg_ln_b": sg_ln_b, "sg_b": sg_b,
             "att_out_g": att_out_g, "sg_out_g": sg_out_g, "pre_mix_g": pre_mix_g, "post_mix_g": post_mix_g,
             "pre_ffn_g": pre_ffn_g, "post_ffn_g": post_ffn_g, "ple_gate_b": ple_gate_b}
    m_small = {"sg_w": m_sg_w, "f_bias": m_f_bias, "sg_ln_g": m_sg_ln_g, "sg_ln_b": m_sg_ln_b, "sg_b": m_sg_b,
               "att_out_g": m_att_out_g, "sg_out_g": m_sg_out_g, "pre_mix_g": m_pre_mix_g,
               "post_mix_g": m_post_mix_g, "pre_ffn_g": m_pre_ffn_g, "post_ffn_g": m_post_ffn_g,
               "ple_gate_b": m_ple_gate_b}
    v_small = {"sg_w": v_sg_w, "f_bias": v_f_bias, "sg_ln_g": v_sg_ln_g, "sg_ln_b": v_sg_ln_b, "sg_b": v_sg_b,
               "att_out_g": v_att_out_g, "sg_out_g": v_sg_out_g, "pre_mix_g": v_pre_mix_g,
               "post_mix_g": v_post_mix_g, "pre_ffn_g": v_pre_ffn_g, "post_ffn_g": v_post_ffn_g,
               "ple_gate_b": v_ple_gate_b}

    k_me = 2 * lax.axis_index("x") + lax.axis_index("y")
    own_slot = lambda got, mine: lax.dynamic_update_slice(got, mine[None], (k_me, 0, 0))
    late_mine = _pack_late(*w_big[1:]).astype(BF16)
    late = _gather_late_start(late_mine)
    win_mine = jnp.pad(w_big[0], ((0, 0), (0, 768 - 642))).reshape(768, 1024).astype(BF16)
    win_k = _win_kernel_order(own_slot(_gather_weights(win_mine), win_mine))
    late_weights = lambda after: _unpack_late(
        own_slot(_gather_late_wait(late[0], late[1], late[2], late[3], after), late_mine))

    names = ("w_in", "w_out", "w_ff1", "w_ff2", "ple_w", "ple_gate_w")
    c1 = jnp.reshape(c, (1,)).astype(jnp.int32)
    own_part = lambda parts, pss: [lax.dynamic_update_slice(pt, lax.dynamic_slice_in_dim(ps, k_me, 1, 0), (k_me, 0, 0))
                                   for pt, ps in zip(parts, pss)]
    tail = {}

    def on_tail_grads(grads):
        dwout, dw1, dw2, dplew, dwg = grads
        gs = [dwout.reshape(4, 256, D), dw1, dw2.reshape(4, D, D), dplew, dwg.reshape(4, 256, D)]
        gots = _swap_halves(gs, "late")
        pss = [_pair_sum(nm, c1, g, got) for nm, g, got in zip(names[1:], gs, gots)]
        tail["xch"] = _exchange_start(pss)
        return tail["xch"][4]

    loss_l, dx, dwin_k, gsmall = _local_step(
        x[0], p[0, 0], loss_target[0], win_k, late_weights, late[4], on_tail_grads, small)

    dwin = jnp.concatenate([dwin_k[:, :3 * AW], dwin_k[:, 5 * AW:5 * AW + NH], dwin_k[:, 3 * AW:5 * AW]], axis=1)
    dwin = jnp.pad(jnp.swapaxes(dwin.reshape(D, 4, 642), 0, 1), ((0, 0), (0, 0), (0, 768 - 642)))
    ps_in = [_pair_sum(names[0], c1, dwin, _swap_halves([dwin], "in")[0])]
    parts = own_part(_exchange_chips(ps_in), ps_in)
    xs, xr, ps_thru, land_thru, _ = tail["xch"]
    ps_late, landed = _exchange_wait(xs, xr, ps_thru, land_thru, dx)
    parts += own_part(landed, ps_late)
    ghs = [_reduce_chips(nm, pt) for nm, pt in zip(names, parts)]
    got2 = _share_grad(ghs, "all")
    padded = lambda t: (jnp.pad(t[0], ((0, 0), (0, 768 - 642))),) + tuple(t[1:])
    big_out = [_update(nm, c1, gh, g2, w, m, v) for nm, gh, g2, w, m, v in
               zip(names, ghs, got2, padded(w_big), padded(m_big), padded(v_big))]
    big_out = [[big_out[j][i][:, :642] if j == 0 else big_out[j][i] for j in range(6)] for i in range(4)]

    view = lambda t: t.reshape(t.shape[-3:]) if t.ndim == 4 else t.reshape(t.shape[-2:])
    views = lambda d: {k: view(d[k]) for k in SMALL_NAMES}
    tot_v, tot_w = _small_allreduce(gsmall, loss_l)
    loss11, res_s = _small_update(tot_v, tot_w, views(small), views(m_small), views(v_small))
    loss = loss11[0, 0]

    def small_out(i, name):
        return res_s[name][i].reshape(small[name].shape)

    order = ["w_in", "f_bias", "sg_ln_g", "sg_ln_b", "sg_w", "sg_b", "att_out_g", "sg_out_g", "w_out",
             "pre_mix_g", "post_mix_g", "pre_ffn_g", "post_ffn_g", "w_ff1", "w_ff2", "ple_w", "ple_gate_w",
             "ple_gate_b"]
    big_idx = {"w_in": 0, "w_out": 1, "w_ff1": 2, "w_ff2": 3, "ple_w": 4, "ple_gate_w": 5}
    outs = [loss, dx[None]]
    for i in range(4):
        for name in order:
            if name in big_idx:
                outs.append(big_out[i][big_idx[name]][None])
            else:
                outs.append(small_out(i, name))
    return tuple(outs)
```

```python
import math

import jax
import jax.numpy as jnp
from jax import lax
from jax.experimental import pallas as pl
from jax.experimental.pallas import tpu as pltpu

F32 = jnp.float32
BF16 = jnp.bfloat16
MESH = pl.DeviceIdType.MESH

D = 1024
DH = 64
NH = 8
AW = 512
CH = 128
DFF = 4096
ZW = 5 * AW + 128
EPS = 1e-6
NEG = -1e30
MASKED = -2e30

TM = 256
TQ = 256

LR, B1, B2, AEPS, WD, STEP = 0.001, 0.9, 0.999, 1e-08, 0.01, 10
BC1 = 1.0 - B1 ** STEP
BC2 = 1.0 - B2 ** STEP

VEC_NAMES = ("f_bias", "sg_ln_g", "sg_ln_b", "sg_b", "att_out_g", "sg_out_g", "pre_mix_g",
             "post_mix_g", "pre_ffn_g", "post_ffn_g", "ple_gate_b")


def _dot(a, b):
    return jnp.dot(a, b, preferred_element_type=F32)


def _dot_nt(a, b):
    return lax.dot_general(a, b, (((1,), (1,)), ((), ())), preferred_element_type=F32)


def _dot_tn(a, b):
    return lax.dot_general(a, b, (((0,), (0,)), ((), ())), preferred_element_type=F32)


def _split3(x):
    h = x.astype(BF16)
    r = x - h.astype(F32)
    m = r.astype(BF16)
    l = (r - m.astype(F32)).astype(BF16)
    return h, m, l


def _dot01(sel, x):
    h, m, l = _split3(x)
    return _dot(sel, h) + _dot(sel, m) + _dot(sel, l)


def _dot01_r(x, sel):
    h, m, l = _split3(x)
    return _dot(h, sel) + _dot(m, sel) + _dot(l, sel)


def _dot01_tn(x, sel):
    h, m, l = _split3(x)
    return _dot_tn(h, sel) + _dot_tn(m, sel) + _dot_tn(l, sel)


def _rs(x, n):
    return lax.rsqrt(jnp.sum(x * x, axis=-1, keepdims=True) * (1.0 / n) + EPS)


def _rms_bwd(dn, x, rs, g, n):
    w = dn * g
    dx = rs * w - x * ((rs * rs * rs) * (1.0 / n) * jnp.sum(w * x, axis=-1, keepdims=True))
    return dx, jnp.sum(dn * x * rs, axis=0, keepdims=True)


_GC = math.sqrt(2.0 / math.pi)


def _gelu(x):
    t = jnp.tanh(_GC * (x + 0.044715 * x * x * x))
    return 0.5 * x * (1.0 + t), t


def _gelu_grad(x, t):
    return 0.5 * (1.0 + t) + 0.5 * x * (1.0 - t * t) * (_GC * (1.0 + 3.0 * 0.044715 * x * x))


def _params(vmem_mb, sem=("arbitrary",)):
    return pltpu.CompilerParams(dimension_semantics=sem, vmem_limit_bytes=vmem_mb * 1024 * 1024)


def _row_call(name, body, T, tm, tiled, resident, outs, accs, scratch=(), reverse=False, vmem_mb=48):
    nt = T // tm
    n_t, n_r, n_o, n_a = len(tiled), len(resident), len(outs), len(accs)

    def kern(*refs):
        t_refs = refs[:n_t]
        r_hbm = refs[n_t:n_t + n_r]
        o_refs = refs[n_t + n_r:n_t + n_r + n_o]
        a_refs = refs[n_t + n_r + n_o:n_t + n_r + n_o + n_a]
        r_vmem = refs[n_t + n_r + n_o + n_a:n_t + 2 * n_r + n_o + n_a]
        s_refs = refs[n_t + 2 * n_r + n_o + n_a:]

        @pl.when(pl.program_id(0) == 0)
        def _():
            for h, v in zip(r_hbm, r_vmem):
                pltpu.sync_copy(h, v)
            for a in a_refs + s_refs:
                a[...] = jnp.zeros(a.shape, a.dtype)

        body(t_refs, r_vmem, o_refs, a_refs, s_refs)

    if reverse:
        idx = lambda i: (nt - 1 - i, 0)
        idx_t = lambda i: (nt - 1 - i, 0, 0)
    else:
        idx = lambda i: (i, 0)
        idx_t = lambda i: (i, 0, 0)
    arrays, in_specs = [], []
    for a in tiled:
        if isinstance(a, tuple):
            arrays.append(a[0])
            in_specs.append(pl.BlockSpec((None, a[0].shape[1], tm), idx_t))
        else:
            arrays.append(a)
            in_specs.append(pl.BlockSpec((tm, a.shape[1]), idx))
    in_specs += [pl.BlockSpec(memory_space=pl.ANY) for _ in resident]
    out_shape, out_specs = [], []
    for o in outs:
        if len(o) == 3:
            out_shape.append(jax.ShapeDtypeStruct((nt, o[0], tm), o[1]))
            out_specs.append(pl.BlockSpec((None, o[0], tm), idx_t))
        else:
            out_shape.append(jax.ShapeDtypeStruct((T, o[0]), o[1]))
            out_specs.append(pl.BlockSpec((tm, o[0]), idx))
    out_shape += [jax.ShapeDtypeStruct(s, F32) for s in accs]
    out_specs += [pl.BlockSpec(s, lambda i, n=len(s): (0,) * n) for s in accs]
    scratch_shapes = [pltpu.VMEM(r.shape, r.dtype) for r in resident]
    scratch_shapes += [pltpu.VMEM(s, F32) for s in scratch]
    return pl.pallas_call(
        kern, name=name, grid=(nt,), in_specs=in_specs, out_specs=out_specs, out_shape=out_shape,
        scratch_shapes=scratch_shapes, compiler_params=_params(vmem_mb),
    )(*arrays, *resident)


def _sg_forward(zu, zv, wm_ref, bsg, lng, lnb, mixed_ref, tm):
    gu, tu = _gelu(zu)
    vg, tv = _gelu(zv)
    mu = jnp.sum(vg, axis=-1, keepdims=True) * (1.0 / AW)
    xc = vg - mu
    rstd = lax.rsqrt(jnp.sum(xc * xc, axis=-1, keepdims=True) * (1.0 / AW) + EPS)
    xhat = xc * rstd
    vvb = (xhat * lng + lnb).astype(BF16)
    lane = lax.broadcasted_iota(jnp.int32, (CH, 128), 1)
    for c in range(tm // CH):
        for j in range(4):
            blk = vvb[c * CH:(c + 1) * CH, j * 128:(j + 1) * 128]
            m0 = _dot(wm_ref[2 * j], blk)
            m1 = _dot(wm_ref[2 * j + 1], blk)
            mixed_ref[c * CH:(c + 1) * CH, j * 128:(j + 1) * 128] = (
                jnp.where(lane < DH, m0, m1) + bsg[:, j * 128:(j + 1) * 128])
    return gu, tu, tv, xhat, rstd, vvb, mixed_ref[...]


def _head_consts():
    src = jnp.arange(AW)
    dst = (src // DH) * 128 + src % DH
    wide = jnp.arange(NH * 128)
    expand = (dst[:, None] == wide[None, :]).astype(BF16)
    heads = jnp.arange(128)
    pieces = jnp.stack([((heads[:, None] * 128 + DH + i == wide[None, :]) & (heads[:, None] < NH)).astype(BF16)
                        for i in range(3)])
    spare = wide % 128 - DH
    qconst = jnp.where((spare >= 0) & (spare < 3), -1.0, 0.0).astype(F32)[None, :]
    one64 = jnp.where(spare == 0, 1.0, 0.0).astype(F32)[None, :]
    one67 = jnp.where(spare == 3, 1.0, 0.0).astype(F32)[None, :]
    pick64 = ((wide[:, None] == heads[None, :] * 128 + DH) & (heads[None, :] < NH)).astype(BF16)
    pick67 = ((wide[:, None] == heads[None, :] * 128 + DH + 3) & (heads[None, :] < NH)).astype(BF16)
    return expand, expand.T, pieces, qconst, one64, one67, pick64, pick67


def _masked_sg_w(sg_w):
    r = lax.broadcasted_iota(jnp.int32, (CH, CH), 0)
    c = lax.broadcasted_iota(jnp.int32, (CH, CH), 1)
    return jnp.where((c <= r)[None], sg_w, 0.0)


def _pre_attn_fwd(x, gpre, win, fbias, lng, lnb, wm, bsg, gsg, expand, pieces, qconst, kconst, vconst):
    T = x.shape[0]
    tm = TM

    def body(t, r, o, a, s):
        (x_ref,) = t
        gpre_r, win_r, fb_r, lng_r, lnb_r, wm_r, bsg_r, gsg_r, ex_r, pc_r, qc_r, kc_r, vc_r = r
        a_o, flog_o, zuv_o, ysgn_o, q8_o, k8_o, v8_o = o
        carry_ref, mixed_ref = s
        xv = x_ref[...]
        av = (xv * _rs(xv, D) * gpre_r[...]).astype(BF16)
        a_o[...] = av
        z = _dot(av, win_r[...])
        zu = z[:, 3 * AW:4 * AW]
        zv = z[:, 4 * AW:5 * AW]
        zuv_o[:, 0:AW] = zu
        zuv_o[:, AW:2 * AW] = zv
        zf = z[:, 5 * AW:] + fb_r[...]
        flog_o[...] = zf
        lane = lax.broadcasted_iota(jnp.int32, (tm, 128), 1)
        logf = jnp.where(lane < NH, jnp.minimum(zf, 0.0) - jnp.log(1.0 + jnp.exp(-jnp.abs(zf))), 0.0)
        rr = lax.broadcasted_iota(jnp.int32, (tm, tm), 0)
        cc = lax.broadcasted_iota(jnp.int32, (tm, tm), 1)
        tri = (cc <= rr).astype(BF16)
        cum = _dot01(tri, logf) + carry_ref[...]
        carry_ref[...] = cum[tm - 1:tm, :]
        ex = ex_r[...]
        q8_o[...] = (_dot((z[:, 0:AW] * (DH ** -0.5)).astype(BF16), ex) + qc_r[...]).astype(BF16)
        ch, cm, cl = _split3(cum)
        k8_o[...] = (_dot(z[:, AW:2 * AW].astype(BF16), ex) + _dot(ch, pc_r[0]) + _dot(cm, pc_r[1])
                     + _dot(cl, pc_r[2]) + kc_r[...]).astype(BF16)
        v8_o[...] = (_dot(z[:, 2 * AW:3 * AW].astype(BF16), ex) + vc_r[...]).astype(BF16)
        gu, _, _, _, _, _, mixed = _sg_forward(zu, zv, wm_r, bsg_r[...], lng_r[...], lnb_r[...], mixed_ref, tm)
        ysg = gu * mixed
        ysgn_o[...] = (ysg * _rs(ysg, AW) * gsg_r[...]).astype(BF16)

    return _row_call(
        "pre_attn_fwd", body, T, tm, [x],
        [gpre, win, fbias, lng, lnb, wm, bsg, gsg, expand, pieces, qconst, kconst, vconst],
        [(D, BF16), (128, F32), (2 * AW, F32), (AW, BF16), (NH * 128, BF16), (NH * 128, BF16), (NH * 128, BF16)], [],
        scratch=[(1, 128), (tm, AW)], vmem_mb=48)


def _flash_fwd(qt8, k8, vt8, sel):
    T = k8.shape[0]
    nq = T // TQ

    def body(qt_ref, k_ref, vt_ref, sel_ref, o_ref, l_ref, u_scr, p_scr):
        qi = pl.program_id(1)
        qts = (qt_ref[0:128, :], qt_ref[128:256, :])
        dmat = (lax.broadcasted_iota(jnp.int32, (TQ, TQ), 0) - lax.broadcasted_iota(jnp.int32, (TQ, TQ), 1))
        u_scr[1] = jnp.full((2, TQ, TQ), MASKED, F32)
        p_scr[...] = jnp.zeros(p_scr.shape, BF16)

        def sub(t, carry, sc, sb, masked):
            blk_c = jnp.clip(t - 2, 0, qi)
            off_a = pl.multiple_of(jnp.minimum(t, qi) * TQ, TQ)
            new = []
            for j in (0, 1):
                m, al, acc = carry[j]
                acc = al * acc + _dot(vt_ref[blk_c, j * 128:(j + 1) * 128, :], p_scr[sc, j])
                m_new = jnp.maximum(m, jnp.max(u_scr[sb, j], axis=0, keepdims=True))
                p_scr[sb, j] = jnp.exp(u_scr[sb, j] - m_new).astype(BF16)
                u = _dot(k_ref[pl.ds(off_a, TQ), j * 128:(j + 1) * 128], qts[j])
                u_scr[sc, j] = jnp.where(dmat <= (qi - t) * TQ, u, MASKED) if masked else u
                new.append((m_new, jnp.exp(m - m_new), acc))
            return tuple(new)

        def pair(t2, carry, masked):
            return sub(2 * t2 + 1, sub(2 * t2, carry, 0, 1, masked), 1, 0, masked)

        init = tuple((jnp.full((1, TQ), NEG, F32), jnp.ones((1, TQ), F32), jnp.zeros((128, TQ), F32))
                     for _ in (0, 1))
        carry = lax.fori_loop(0, qi // 2, lambda t2, cr: pair(t2, cr, False), init)
        (m0, _, a0), (m1, _, a1) = pair(qi // 2 + 1, pair(qi // 2, carry, True), True)
        l0 = a0[DH:DH + 1, :]
        l1 = a1[DH:DH + 1, :]
        o_ref[...] = _dot01_tn(a0 * (1.0 / l0), sel_ref[0]) + _dot01_tn(a1 * (1.0 / l1), sel_ref[1])
        l_ref[0:1, :] = m0 + jnp.log(l0)
        l_ref[1:2, :] = m1 + jnp.log(l1)
        l_ref[2:8, :] = jnp.zeros((6, TQ), F32)

    return pl.pallas_call(
        body, name="flash_fwd", grid=(4, nq),
        in_specs=[pl.BlockSpec((None, 256, TQ), lambda h, i: (i, h, 0)),
                  pl.BlockSpec((T, 256), lambda h, i: (0, h)),
                  pl.BlockSpec((nq, 256, TQ), lambda h, i: (0, h, 0)),
                  pl.BlockSpec((2, 128, 128), lambda h, i: (0, 0, 0))],
        out_specs=[pl.BlockSpec((TQ, 128), lambda h, i: (i, h)),
                   pl.BlockSpec((None, 8, TQ), lambda h, i: (h, 0, i))],
        out_shape=[jax.ShapeDtypeStruct((T, AW), F32), jax.ShapeDtypeStruct((4, 8, T), F32)],
        scratch_shapes=[pltpu.VMEM((2, 2, TQ, TQ), F32), pltpu.VMEM((2, 2, TQ, TQ), BF16)],
        compiler_params=_params(40, ("arbitrary", "arbitrary")),
    )(qt8, k8, vt8, sel)


def _flash_bwd(q8, qt8, k8, v8, do8, dot8, lse, dlt):
    T = q8.shape[0]
    nk = T // TQ

    def body(q_ref, qt_ref, k_ref, v_ref, do_ref, dot_ref, l_ref, d_ref, dqt_ref, dk_ref, dv_ref,
             u_scr, dp_scr, p_scr, ds_scr):
        kb = pl.program_id(1)
        n = nk - kb

        @pl.when(kb == 0)
        def _():
            dqt_ref[...] = jnp.zeros(dqt_ref.shape, F32)

        dk_ref[...] = jnp.zeros(dk_ref.shape, F32)
        dv_ref[...] = jnp.zeros(dv_ref.shape, F32)
        u_scr[1] = jnp.full((2, TQ, TQ), MASKED, F32)
        dp_scr[1] = jnp.zeros((2, TQ, TQ), F32)
        p_scr[...] = jnp.zeros(p_scr.shape, BF16)
        ds_scr[...] = jnp.zeros(ds_scr.shape, BF16)
        dmat = (lax.broadcasted_iota(jnp.int32, (TQ, TQ), 0) - lax.broadcasted_iota(jnp.int32, (TQ, TQ), 1))
        ks = (k_ref[:, 0:128], k_ref[:, 128:256])
        vs = (v_ref[:, 0:128], v_ref[:, 128:256])

        def sub(t, sc, sb):
            blk_a = kb + jnp.minimum(t, n - 1)
            blk_c = kb + jnp.clip(t - 2, 0, n - 1)
            off_b = pl.multiple_of((kb + jnp.clip(t - 1, 0, n - 1)) * TQ, TQ)
            off_c = pl.multiple_of(blk_c * TQ, TQ)
            lim = jnp.where(t < n, t * TQ, -TQ)
            for j in (0, 1):
                hl = slice(j * 128, (j + 1) * 128)
                dqt_ref[blk_c, hl, :] += _dot_tn(ks[j], ds_scr[sc, j])
                dk_ref[:, hl] += _dot(ds_scr[sc, j], q_ref[pl.ds(off_c, TQ), hl])
                dv_ref[:, hl] += _dot(p_scr[sc, j], do_ref[pl.ds(off_c, TQ), hl])
                p = jnp.exp(u_scr[sb, j] - l_ref[j:j + 1, pl.ds(off_b, TQ)])
                p_scr[sb, j] = p.astype(BF16)
                ds_scr[sb, j] = (p * (dp_scr[sb, j] - d_ref[j:j + 1, pl.ds(off_b, TQ)])).astype(BF16)
                u_scr[sc, j] = jnp.where(dmat <= lim, _dot(ks[j], qt_ref[blk_a, hl, :]), MASKED)
                dp_scr[sc, j] = _dot(vs[j], dot_ref[blk_a, hl, :])

        def it(t2, carry):
            sub(2 * t2, 0, 1)
            sub(2 * t2 + 1, 1, 0)
            return carry

        lax.fori_loop(0, n // 2 + 1, it, 0)

        @pl.when(n % 2 == 1)
        def _():
            sub(n + 1, 0, 1)

    return pl.pallas_call(
        body, name="flash_bwd", grid=(4, nk),
        in_specs=[pl.BlockSpec((T, 256), lambda h, i: (0, h)),
                  pl.BlockSpec((nk, 256, TQ), lambda h, i: (0, h, 0)),
                  pl.BlockSpec((TQ, 256), lambda h, i: (i, h)),
                  pl.BlockSpec((TQ, 256), lambda h, i: (i, h)),
                  pl.BlockSpec((T, 256), lambda h, i: (0, h)),
                  pl.BlockSpec((nk, 256, TQ), lambda h, i: (0, h, 0)),
                  pl.BlockSpec((None, 8, T), lambda h, i: (h, 0, 0)),
                  pl.BlockSpec((None, 8, T), lambda h, i: (h, 0, 0))],
        out_specs=[pl.BlockSpec((nk, 256, TQ), lambda h, i: (0, h, 0)),
                   pl.BlockSpec((TQ, 256), lambda h, i: (i, h)),
                   pl.BlockSpec((TQ, 256), lambda h, i: (i, h))],
        out_shape=[jax.ShapeDtypeStruct((nk, NH * 128, TQ), F32), jax.ShapeDtypeStruct((T, NH * 128), F32),
                   jax.ShapeDtypeStruct((T, NH * 128), F32)],
        scratch_shapes=[pltpu.VMEM((2, 2, TQ, TQ), F32), pltpu.VMEM((2, 2, TQ, TQ), F32),
                        pltpu.VMEM((2, 2, TQ, TQ), BF16), pltpu.VMEM((2, 2, TQ, TQ), BF16)],
        compiler_params=_params(56, ("arbitrary", "arbitrary")),
    )(q8, qt8, k8, v8, do8, dot8, lse, dlt)


def _tail_fwd1(x, yatt, ysgn, gatt, wout, gpm, gpf, w1):
    T = x.shape[0]

    def body(t, r, o, a, s):
        x_ref, ya_ref, ys_ref = t
        gatt_r, wout_r, gpm_r, gpf_r, w1_r = r
        y_o, o_o, h1_o, c2_o, s_o, rr_o = o
        ya = ya_ref[...]
        yan = (ya * _rs(ya, AW) * gatt_r[...]).astype(BF16)
        y_o[:, 0:AW] = yan
        y_o[:, AW:] = ys_ref[...]
        ov = _dot(yan, wout_r[0:AW, :]) + _dot(ys_ref[...], wout_r[AW:, :])
        o_o[...] = ov
        h1 = x_ref[...] + ov * _rs(ov, D) * gpm_r[...]
        h1_o[...] = h1
        c2 = (h1 * _rs(h1, D) * gpf_r[...]).astype(BF16)
        c2_o[...] = c2
        for k in range(4):
            rr = jnp.maximum(_dot(c2, w1_r[k]), 0.0)
            rr_o[:, k * D:(k + 1) * D] = rr.astype(BF16)
            s_o[:, k * D:(k + 1) * D] = (rr * rr).astype(BF16)

    return _row_call(
        "tail_fwd1", body, T, TM, [x, yatt, ysgn], [gatt, wout, gpm, gpf, w1],
        [(D, BF16), (D, F32), (D, F32), (D, BF16), (DFF, BF16), (DFF, BF16)], [], vmem_mb=48)


def _tail_fwd2(sact, h1, p, tgt, w2, gpff, wg, bg, wpe):
    T = h1.shape[0]

    def body(t, r, o, a, s):
        s_ref, h1_ref, p_ref, t_ref = t
        w2_r, gpff_r, wg_r, bg_r, wpe_r = r
        ff_o, h2b_o, de_o, dpre_o, dh2_o = o
        loss_a, dbg_a = a
        ff = _dot(s_ref[...], w2_r[...])
        ff_o[...] = ff
        h2 = h1_ref[...] + ff * _rs(ff, D) * gpff_r[...]
        h2b = h2.astype(BF16)
        h2b_o[...] = h2b
        gate = 1.0 / (1.0 + jnp.exp(-(_dot(h2b, wg_r[...]) + bg_r[...])))
        pb = p_ref[...].astype(BF16)
        e = jnp.concatenate([_dot(pb, wpe_r[k]) for k in range(4)], axis=1)
        diff = h2 + gate * e - t_ref[...]
        loss_a[...] += jnp.sum(diff * diff, axis=0, keepdims=True)
        dh3 = diff * (1.0 / D)
        de_o[...] = (dh3 * gate).astype(BF16)
        dpre = dh3 * e * gate * (1.0 - gate)
        dbg_a[...] += jnp.sum(dpre, axis=0, keepdims=True)
        dpb = dpre.astype(BF16)
        dpre_o[...] = dpb
        dh2_o[...] = dh3 + _dot_nt(dpb, wg_r[...])

    return _row_call(
        "tail_fwd2", body, T, TM, [sact, h1, p, tgt], [w2, gpff, wg, bg, wpe],
        [(D, F32), (D, BF16), (D, BF16), (D, BF16), (D, F32)], [(1, D), (1, D)], vmem_mb=48)


def _tail_bwd(dh2, ff, rr, h1, ov, yatt, w2, w1, wout, gpff, gpf, gpm, gatt, gsel, expand):
    T = dh2.shape[0]

    def body(t, r, o, a, s):
        dh2_ref, ff_ref, rr_ref, h1_ref, o_ref, ya_ref = t
        w2_r, w1_r, wout_r, gpff_r, gpf_r, gpm_r, gatt_r, gsel_r, ex_r = r
        dff_o, dr_o, do_o, do8_o, dlt_o, dysg_o, dh1_o = o
        dgpff_a, dgpf_a, dgpm_a, dgatt_a = a
        dh2v = dh2_ref[...]
        ffv = ff_ref[...]
        dff, dg = _rms_bwd(dh2v, ffv, _rs(ffv, D), gpff_r[...], D)
        dgpff_a[...] += dg
        dffb = dff.astype(BF16)
        dff_o[...] = dffb
        drb = (_dot_nt(dffb, w2_r[...]) * (2.0 * rr_ref[...].astype(F32))).astype(BF16)
        dr_o[...] = drb
        dc2 = _dot_nt(drb[:, 0:D], w1_r[0])
        for k in range(1, 4):
            dc2 = dc2 + _dot_nt(drb[:, k * D:(k + 1) * D], w1_r[k])
        h1v = h1_ref[...]
        d1, dg = _rms_bwd(dc2, h1v, _rs(h1v, D), gpf_r[...], D)
        dgpf_a[...] += dg
        dh1 = dh2v + d1
        dh1_o[...] = dh1
        ovv = o_ref[...]
        dov, dg = _rms_bwd(dh1, ovv, _rs(ovv, D), gpm_r[...], D)
        dgpm_a[...] += dg
        dob = dov.astype(BF16)
        do_o[...] = dob
        dysg_o[...] = _dot_nt(dob, wout_r[AW:, :])
        dyan = _dot_nt(dob, wout_r[0:AW, :])
        ya = ya_ref[...]
        dya, dg = _rms_bwd(dyan, ya, _rs(ya, AW), gatt_r[...], AW)
        dgatt_a[...] += dg
        do8_o[...] = _dot(dya.astype(BF16), ex_r[...]).astype(BF16)
        dlt_o[...] = _dot01_r(dya * ya, gsel_r[...])

    return _row_call(
        "tail_bwd", body, T, TM, [dh2, ff, rr, h1, ov, yatt],
        [w2, w1, wout, gpff, gpf, gpm, gatt, gsel, expand],
        [(D, BF16), (DFF, BF16), (D, BF16), (NH * 128, BF16), (128, F32), (AW, F32), (D, F32)],
        [(1, D), (1, D), (1, D), (1, AW)], vmem_mb=56)


def _pre_attn_bwd(x, dh1, dq8, dk8, dv8, flog, zuv, dysg, gpre, win, lng, lnb, wm, wmt, bsg, gsg, gsel, shrink, pick64, pick67):
    T = x.shape[0]
    tm = TM

    def body(t, r, o, a, s):
        x_ref, dh1_ref, dq_ref, dk_ref, dv_ref, fl_ref, zuv_ref, dys_ref = t
        gpre_r, win_r, lng_r, lnb_r, wm_r, wmt_r, bsg_r, gsg_r, gsel_r, sh_r, p64_r, p67_r = r
        dx_o, dz_o = o
        dgpre_a, dfb_a, dgsg_a, dlng_a, dlnb_a, dws_a, dbs_a, dsb_a = a
        carry_ref, mixed_ref, dvv_ref = s
        dq8v = dq_ref[...]
        dk8v = dk_ref[...]
        dcv = _dot01_r(dq8v, p67_r[...]) + _dot01_r(dk8v, p64_r[...])
        rr = lax.broadcasted_iota(jnp.int32, (tm, tm), 0)
        cc = lax.broadcasted_iota(jnp.int32, (tm, tm), 1)
        triu = (cc >= rr).astype(BF16)
        dlogf = _dot01(triu, dcv) + carry_ref[...]
        carry_ref[...] = dlogf[0:1, :]
        dzf = dlogf * (1.0 / (1.0 + jnp.exp(fl_ref[...])))
        dfb_a[...] += jnp.sum(dzf, axis=0, keepdims=True)
        dz_o[:, 5 * AW:] = dzf.astype(BF16)
        zu = zuv_ref[:, 0:AW]
        zv = zuv_ref[:, AW:]
        gu, tu, tv, xhat, rstd, vvb, mixed = _sg_forward(
            zu, zv, wm_r, bsg_r[...], lng_r[...], lnb_r[...], mixed_ref, tm)
        ysg = gu * mixed
        dysg_n = dys_ref[...]
        dys, dg = _rms_bwd(dysg_n, ysg, _rs(ysg, AW), gsg_r[...], AW)
        dgsg_a[...] += dg
        dgu = dys * mixed
        dmix = dys * gu
        dmb = dmix.astype(BF16)
        lane = lax.broadcasted_iota(jnp.int32, (CH, 128), 1)
        lo = lane < DH
        for c in range(tm // CH):
            rows = slice(c * CH, (c + 1) * CH)
            dbs_a[...] += dmix[rows, :]
            for j in range(4):
                cols = slice(j * 128, (j + 1) * 128)
                dmblk = dmb[rows, cols]
                vblk = vvb[rows, cols]
                d0 = _dot(wmt_r[2 * j], dmblk)
                d1 = _dot(wmt_r[2 * j + 1], dmblk)
                dvv_ref[rows, cols] = jnp.where(lo, d0, d1)
                dws_a[2 * j] += _dot_nt(jnp.where(lo, dmblk, jnp.zeros_like(dmblk)), vblk)
                dws_a[2 * j + 1] += _dot_nt(jnp.where(lo, jnp.zeros_like(dmblk), dmblk), vblk)
        dvv = dvv_ref[...]
        dlng_a[...] += jnp.sum(dvv * xhat, axis=0, keepdims=True)
        dlnb_a[...] += jnp.sum(dvv, axis=0, keepdims=True)
        dxh = dvv * lng_r[...]
        dvg = rstd * (dxh - jnp.sum(dxh, axis=-1, keepdims=True) * (1.0 / AW)
                      - xhat * (jnp.sum(dxh * xhat, axis=-1, keepdims=True) * (1.0 / AW)))
        dz_o[:, 3 * AW:4 * AW] = (dgu * _gelu_grad(zu, tu)).astype(BF16)
        dz_o[:, 4 * AW:5 * AW] = (dvg * _gelu_grad(zv, tv)).astype(BF16)
        dz_o[:, 0:AW] = _dot((dq8v * (DH ** -0.5)).astype(BF16), sh_r[...]).astype(BF16)
        dz_o[:, AW:2 * AW] = _dot(dk8v.astype(BF16), sh_r[...]).astype(BF16)
        dz_o[:, 2 * AW:3 * AW] = _dot(dv_ref[...].astype(BF16), sh_r[...]).astype(BF16)
        da = _dot_nt(dz_o[...], win_r[...])
        xv = x_ref[...]
        dxa, dg = _rms_bwd(da, xv, _rs(xv, D), gpre_r[...], D)
        dgpre_a[...] += dg
        dx_o[...] = dh1_ref[...] + dxa

        @pl.when(pl.program_id(0) == T // tm - 1)
        def _():
            dsb_a[...] = _dot01_r(dbs_a[...], gsel_r[...])

    outs = _row_call(
        "pre_attn_bwd", body, T, tm, [x, dh1, dq8, dk8, dv8, flog, zuv, dysg],
        [gpre, win, lng, lnb, wm, wmt, bsg, gsg, gsel, shrink, pick64, pick67],
        [(D, F32), (ZW, BF16)],
        [(1, D), (1, 128), (1, AW), (1, AW), (1, AW), (8, CH, CH), (CH, AW), (CH, 128)],
        scratch=[(1, 128), (tm, AW), (tm, AW)], reverse=True, vmem_mb=48)
    return outs


def _matmul_tn(name, a, b, tn=512, tt=2048, shards=1):
    T, K = a.shape
    N = b.shape[1]
    tk = min(K, 1024)
    tn = min(tn, N // shards)
    tt = min(tt, T)
    nj = N // shards // tn

    def body(a_ref, b_ref, o_ref):
        @pl.when(pl.program_id(2) == 0)
        def _():
            o_ref[...] = jnp.zeros(o_ref.shape, F32)

        o_ref[...] += _dot_tn(a_ref[...].astype(BF16), b_ref[...].astype(BF16))

    if shards == 1:
        out_shape = jax.ShapeDtypeStruct((K, N), F32)
        out_spec = pl.BlockSpec((tk, tn), lambda i, j, t: (i, j))
    else:
        out_shape = jax.ShapeDtypeStruct((shards, K, N // shards), F32)
        out_spec = pl.BlockSpec((None, tk, tn), lambda i, j, t: (j // nj, i, j % nj))
    return pl.pallas_call(
        body, name=name, grid=(K // tk, N // tn, T // tt),
        in_specs=[pl.BlockSpec((tt, tk), lambda i, j, t: (t, i)),
                  pl.BlockSpec((tt, tn), lambda i, j, t: (t, j))],
        out_specs=out_spec, out_shape=out_shape,
        compiler_params=_params(40, ("arbitrary", "arbitrary", "arbitrary")),
    )(a, b)


def _me():
    return lax.axis_index("x"), lax.axis_index("y"), lax.axis_index("c")


HBM_SPEC = pl.BlockSpec(memory_space=pltpu.HBM)


def _gather_weights(mine):
    half = mine.shape[0] // 2

    def body(mine_ref, out_ref, ici_send, ici_recv, d2d_send, d2d_recv):
        x, y, c = _me()
        k_me = 2 * x + y
        chips = [(1 - x, y), (x, 1 - y), (1 - x, 1 - y)]
        my_rows = pl.ds(pl.multiple_of(c * half, 16), half)
        sib_rows = pl.ds(pl.multiple_of((1 - c) * half, 16), half)

        def over_ici(j, k, to):
            src = mine_ref.at[my_rows] if k is None else out_ref.at[k, my_rows]
            return pltpu.make_async_remote_copy(
                src_ref=src, dst_ref=out_ref.at[k_me if k is None else k, my_rows], send_sem=ici_send.at[j],
                recv_sem=ici_recv.at[j], device_id=to, device_id_type=MESH)

        def over_d2d(j, k, rows):
            return pltpu.make_async_remote_copy(
                src_ref=out_ref.at[k, rows], dst_ref=out_ref.at[k, rows], send_sem=d2d_send.at[j],
                recv_sem=d2d_recv.at[j], device_id=(x, y, 1 - c), device_id_type=MESH)

        first = [over_ici(j, None, (cx, cy, c)) for j, (cx, cy) in enumerate(chips)]
        for cp in first:
            cp.start()
        passed = [over_d2d(j, 2 * cx + cy, my_rows) for j, (cx, cy) in enumerate(chips)]
        for j, (cx, cy) in enumerate(chips):
            over_ici(j, 2 * cx + cy, (cx, cy, c)).wait_recv()
            passed[j].start()
        for j, (cx, cy) in enumerate(chips):
            over_d2d(j, 2 * cx + cy, sib_rows).wait_recv()
        for cp in first + passed:
            cp.wait_send()

    return pl.pallas_call(
        body, name="gather_weights", in_specs=[HBM_SPEC], out_specs=HBM_SPEC,
        out_shape=jax.ShapeDtypeStruct((4,) + mine.shape, mine.dtype),
        scratch_shapes=[pltpu.SemaphoreType.DMA((3,)), pltpu.SemaphoreType.DMA((3,)), pltpu.SemaphoreType.DMA((3,)),
                        pltpu.SemaphoreType.DMA((3,))],
    )(mine)


SEM_SPEC = pl.BlockSpec(memory_space=pltpu.SEMAPHORE)
EFFECT = pltpu.SideEffectType.DATAFLOW_SIDE_EFFECTING


def _gather_late_start(mine):
    def body(mine_ref, land_ref, send_sems, recv_sems, mine_thru, land_thru, token):
        x, y, c = _me()
        k_me = 2 * x + y
        for j, (cx, cy) in enumerate([(1 - x, y), (x, 1 - y), (1 - x, 1 - y)]):
            pltpu.make_async_remote_copy(
                src_ref=mine_ref, dst_ref=land_ref.at[k_me], send_sem=send_sems.at[j], recv_sem=recv_sems.at[j],
                device_id=(cx, cy, c), device_id_type=MESH).start()
        token[...] = jnp.zeros(token.shape, F32)

    land = lax.empty((4,) + mine.shape, mine.dtype)
    return pl.pallas_call(
        body, name="gather_late_start",
        out_shape=(pltpu.SemaphoreType.DMA((3,)), pltpu.SemaphoreType.DMA((3,)), pltpu.HBM(mine.shape, mine.dtype),
                   pltpu.HBM(land.shape, land.dtype), jax.ShapeDtypeStruct((8, 128), F32)),
        in_specs=(HBM_SPEC, HBM_SPEC),
        out_specs=(SEM_SPEC, SEM_SPEC, HBM_SPEC, HBM_SPEC, pl.BlockSpec(memory_space=pltpu.VMEM)),
        input_output_aliases={0: 2, 1: 3},
        compiler_params=pltpu.CompilerParams(has_side_effects=EFFECT),
    )(pltpu.with_memory_space_constraint(mine, pltpu.HBM), pltpu.with_memory_space_constraint(land, pltpu.HBM))


def _gather_late_wait(send_sems, recv_sems, mine_thru, land_thru, after):
    def body(mine_ref, land_ref, send_sems, recv_sems, after_ref, mine_dead, got_ref):
        x, y, c = _me()
        for j, (cx, cy) in enumerate([(1 - x, y), (x, 1 - y), (1 - x, 1 - y)]):
            cp = pltpu.make_async_remote_copy(
                src_ref=mine_ref, dst_ref=land_ref.at[2 * cx + cy], send_sem=send_sems.at[j],
                recv_sem=recv_sems.at[j], device_id=(cx, cy, c), device_id_type=MESH)
            cp.wait_send()
            cp.wait_recv()

    return pl.pallas_call(
        body, name="gather_late_wait",
        out_shape=(pltpu.HBM(mine_thru.shape, mine_thru.dtype), pltpu.HBM(land_thru.shape, land_thru.dtype)),
        in_specs=(HBM_SPEC, HBM_SPEC, SEM_SPEC, SEM_SPEC, pl.BlockSpec(memory_space=pl.ANY)),
        out_specs=(HBM_SPEC, HBM_SPEC), input_output_aliases={0: 0, 1: 1},
        compiler_params=pltpu.CompilerParams(has_side_effects=EFFECT),
    )(mine_thru, land_thru, send_sems, recv_sems, after)[1]


def _swap_halves(gs, tag):
    n = len(gs)

    def body(*refs):
        g_refs, got_refs, send_sems, recv_sems = refs[:n], refs[n:2 * n], refs[2 * n], refs[2 * n + 1]
        x, y, c = _me()
        cps = []
        for i, (g_ref, got_ref) in enumerate(zip(g_refs, got_refs)):
            half = g_ref.shape[1] // 2
            theirs = pl.multiple_of((1 - c) * half, 16)
            cps.append(pltpu.make_async_remote_copy(
                src_ref=g_ref.at[:, pl.ds(theirs, half), :], dst_ref=got_ref, send_sem=send_sems.at[i],
                recv_sem=recv_sems.at[i], device_id=(x, y, 1 - c), device_id_type=MESH))
        for cp in cps:
            cp.start()
        for cp in cps:
            cp.wait()

    return pl.pallas_call(
        body, name="swap_halves_" + tag, in_specs=[HBM_SPEC] * n, out_specs=[HBM_SPEC] * n,
        out_shape=[jax.ShapeDtypeStruct((4, g.shape[1] // 2, g.shape[2]), F32) for g in gs],
        scratch_shapes=[pltpu.SemaphoreType.DMA((n,)), pltpu.SemaphoreType.DMA((n,))],
    )(*gs)


def _pair_sum(name, c1, g, got):
    half, cols = got.shape[1], got.shape[2]

    def body(c_ref, a_ref, b_ref, o_ref):
        o_ref[...] = (a_ref[...] + b_ref[...]).astype(BF16)

    return pl.pallas_call(
        body, name="pair_sum_" + name,
        grid_spec=pltpu.PrefetchScalarGridSpec(
            num_scalar_prefetch=1, grid=(4,),
            in_specs=[pl.BlockSpec((1, half, cols), lambda k, c_ref: (k, c_ref[0], 0)),
                      pl.BlockSpec((1, half, cols), lambda k, c_ref: (k, 0, 0))],
            out_specs=pl.BlockSpec((1, half, cols), lambda k, c_ref: (k, 0, 0))),
        out_shape=jax.ShapeDtypeStruct(got.shape, BF16), compiler_params=_params(32),
    )(c1, g, got)


def _exchange_chips(pss):
    n = len(pss)

    def body(*refs):
        ps_refs, out_refs = refs[:n], refs[n:2 * n]
        send_sems, recv_sems = refs[2 * n:]
        x, y, c = _me()
        k_me = 2 * x + y
        chips = [(1 - x, y), (x, 1 - y), (1 - x, 1 - y)]
        sends = []
        for i, (ps_ref, out_ref) in enumerate(zip(ps_refs, out_refs)):
            for j, (cx, cy) in enumerate(chips):
                sends.append(pltpu.make_async_remote_copy(
                    src_ref=ps_ref.at[2 * cx + cy], dst_ref=out_ref.at[k_me], send_sem=send_sems.at[3 * i + j],
                    recv_sem=recv_sems.at[3 * i + j], device_id=(cx, cy, c), device_id_type=MESH))
        for cp in sends:
            cp.start()
        for i, (ps_ref, out_ref) in enumerate(zip(ps_refs, out_refs)):
            for j, (cx, cy) in enumerate(chips):
                pltpu.make_async_remote_copy(
                    src_ref=ps_ref.at[k_me], dst_ref=out_ref.at[2 * cx + cy], send_sem=send_sems.at[3 * i + j],
                    recv_sem=recv_sems.at[3 * i + j], device_id=(cx, cy, c), device_id_type=MESH).wait_recv()
        for cp in sends:
            cp.wait_send()

    return pl.pallas_call(
        body, name="exchange_chips", in_specs=[HBM_SPEC] * n, out_specs=[HBM_SPEC] * n,
        out_shape=[jax.ShapeDtypeStruct(ps.shape, ps.dtype) for ps in pss],
        scratch_shapes=[pltpu.SemaphoreType.DMA((3 * n,)), pltpu.SemaphoreType.DMA((3 * n,))],
    )(*pss)


def _exchange_start(pss):
    n = len(pss)

    def body(*refs):
        ps_refs, land_refs = refs[:n], refs[n:2 * n]
        send_sems, recv_sems = refs[2 * n], refs[2 * n + 1]
        token = refs[4 * n + 2]
        x, y, c = _me()
        k_me = 2 * x + y
        for i, (ps_ref, land_ref) in enumerate(zip(ps_refs, land_refs)):
            for j, (cx, cy) in enumerate([(1 - x, y), (x, 1 - y), (1 - x, 1 - y)]):
                pltpu.make_async_remote_copy(
                    src_ref=ps_ref.at[2 * cx + cy], dst_ref=land_ref.at[k_me], send_sem=send_sems.at[3 * i + j],
                    recv_sem=recv_sems.at[3 * i + j], device_id=(cx, cy, c), device_id_type=MESH).start()
        token[...] = jnp.zeros(token.shape, F32)

    lands = [lax.empty(ps.shape, ps.dtype) for ps in pss]
    hbm = lambda t: pltpu.HBM(t.shape, t.dtype)
    res = pl.pallas_call(
        body, name="exchange_start",
        out_shape=(pltpu.SemaphoreType.DMA((3 * n,)), pltpu.SemaphoreType.DMA((3 * n,)), *[hbm(t) for t in pss],
                   *[hbm(t) for t in lands], jax.ShapeDtypeStruct((8, 128), F32)),
        in_specs=(HBM_SPEC,) * (2 * n),
        out_specs=(SEM_SPEC, SEM_SPEC) + (HBM_SPEC,) * (2 * n) + (pl.BlockSpec(memory_space=pltpu.VMEM),),
        input_output_aliases={i: 2 + i for i in range(2 * n)},
        compiler_params=pltpu.CompilerParams(has_side_effects=EFFECT),
    )(*[pltpu.with_memory_space_constraint(t, pltpu.HBM) for t in list(pss) + lands])
    return res[0], res[1], res[2:2 + n], res[2 + n:2 + 2 * n], res[2 + 2 * n]


def _exchange_wait(send_sems, recv_sems, ps_thru, land_thru, after):
    n = len(ps_thru)

    def body(*refs):
        ps_refs, land_refs = refs[:n], refs[n:2 * n]
        send_sems, recv_sems = refs[2 * n], refs[2 * n + 1]
        x, y, c = _me()
        k_me = 2 * x + y
        for i, (ps_ref, land_ref) in enumerate(zip(ps_refs, land_refs)):
            for j, (cx, cy) in enumerate([(1 - x, y), (x, 1 - y), (1 - x, 1 - y)]):
                cp = pltpu.make_async_remote_copy(
                    src_ref=ps_ref.at[k_me], dst_ref=land_ref.at[2 * cx + cy], send_sem=send_sems.at[3 * i + j],
                    recv_sem=recv_sems.at[3 * i + j], device_id=(cx, cy, c), device_id_type=MESH)
                cp.wait_send()
                cp.wait_recv()

    hbm = lambda t: pltpu.HBM(t.shape, t.dtype)
    res = pl.pallas_call(
        body, name="exchange_wait",
        out_shape=tuple(hbm(t) for t in list(ps_thru) + list(land_thru)),
        in_specs=(HBM_SPEC,) * (2 * n) + (SEM_SPEC, SEM_SPEC, pl.BlockSpec(memory_space=pl.ANY)),
        out_specs=(HBM_SPEC,) * (2 * n), input_output_aliases={i: i for i in range(2 * n)},
        compiler_params=pltpu.CompilerParams(has_side_effects=EFFECT),
    )(*ps_thru, *land_thru, send_sems, recv_sems, after)
    return res[:n], res[n:]


def _adamw(w, g, m, v):
    m = B1 * m + (1.0 - B1) * g
    v = B2 * v + (1.0 - B2) * (g * g)
    delta = -LR * ((m / BC1) / (jnp.sqrt(v / BC2) + AEPS) + WD * w)
    return delta, m, v


def _reduce_chips(name, parts):
    half, cols = parts.shape[1], parts.shape[2]

    def body(p_ref, o_ref):
        f = lambda k: p_ref[k].astype(F32)
        o_ref[...] = ((f(0) + f(1)) + f(2)) + f(3)

    return pl.pallas_call(
        body, name="reduce_chips_" + name, grid=(1,),
        in_specs=[pl.BlockSpec((4, half, cols), lambda i: (0, 0, 0))],
        out_specs=pl.BlockSpec((half, cols), lambda i: (0, 0)),
        out_shape=jax.ShapeDtypeStruct((half, cols), F32), compiler_params=_params(32),
    )(parts)


def _share_grad(ghs, tag):
    n = len(ghs)

    def body(*refs):
        g_refs, got_refs, send_sems, recv_sems = refs[:n], refs[n:2 * n], refs[2 * n], refs[2 * n + 1]
        x, y, c = _me()
        cps = [pltpu.make_async_remote_copy(
            src_ref=g_ref, dst_ref=got_ref, send_sem=send_sems.at[i], recv_sem=recv_sems.at[i],
            device_id=(x, y, 1 - c), device_id_type=MESH) for i, (g_ref, got_ref) in enumerate(zip(g_refs, got_refs))]
        for cp in cps:
            cp.start()
        for cp in cps:
            cp.wait()

    return pl.pallas_call(
        body, name="share_grad_" + tag, in_specs=[HBM_SPEC] * n, out_specs=[HBM_SPEC] * n,
        out_shape=[jax.ShapeDtypeStruct(g.shape, F32) for g in ghs],
        scratch_shapes=[pltpu.SemaphoreType.DMA((n,)), pltpu.SemaphoreType.DMA((n,))],
    )(*ghs)


def _update(name, c1, gh, got, w, m, v):
    half, cols = gh.shape

    def body(c_ref, gh_ref, got_ref, w_ref, m_ref, v_ref, g_o, d_o, m_o, v_o):
        g = jnp.where(pl.program_id(0) == c_ref[0], gh_ref[...], got_ref[...])
        delta, mn, vn = _adamw(w_ref[...], g, m_ref[...], v_ref[...])
        g_o[...] = g
        d_o[...] = delta
        m_o[...] = mn
        v_o[...] = vn

    same = pl.BlockSpec((half, cols), lambda h, c_ref: (0, 0))
    rows = pl.BlockSpec((half, cols), lambda h, c_ref: (h, 0))
    return pl.pallas_call(
        body, name="update_" + name,
        grid_spec=pltpu.PrefetchScalarGridSpec(
            num_scalar_prefetch=1, grid=(2,), in_specs=[same, same, rows, rows, rows],
            out_specs=[rows, rows, rows, rows]),
        out_shape=[jax.ShapeDtypeStruct(w.shape, F32)] * 4, compiler_params=_params(40),
    )(c1, gh, got, w, m, v)


SMALL_NAMES = ("sg_w",) + VEC_NAMES
VEC_ROWS = 24
VEC_ROW = {"f_bias": 0, "sg_ln_g": 1, "sg_ln_b": 2, "att_out_g": 3, "sg_out_g": 4, "pre_mix_g": 5,
           "post_mix_g": 6, "pre_ffn_g": 7, "sg_b": 8, "post_ffn_g": 16, "ple_gate_b": 17}
LOSS_VEC_ROW = 18


def _small_allreduce(g, loss_l):
    n = len(SMALL_NAMES)

    def body(*refs):
        g_r = dict(zip(SMALL_NAMES, refs[0:n]))
        loss_r, totv_o, totw_o, bufv, bufw, send_sems, recv_sems = refs[n:]
        x, y, c = _me()
        me = 4 * x + 2 * y + c
        bufv[me] = jnp.zeros((VEC_ROWS, 1024), F32)
        for name in VEC_NAMES:
            val = g_r[name][...]
            bufv[me, pl.ds(VEC_ROW[name], val.shape[0]), pl.ds(0, val.shape[1])] = val
        bufv[me, pl.ds(LOSS_VEC_ROW, 1), :] = loss_r[...] * (0.5 / D)
        rr = lax.broadcasted_iota(jnp.int32, (CH, CH), 0)
        cc = lax.broadcasted_iota(jnp.int32, (CH, CH), 1)
        bufw[me] = jnp.where((cc <= rr)[None], g_r["sg_w"][...], 0.0)

        rels = [(rx, ry, rc) for rx in (0, 1) for ry in (0, 1) for rc in (0, 1)][1:]

        def peer(r):
            return ((x + r[0]) % 2, (y + r[1]) % 2, (c + r[2]) % 2)

        def copies(j, slot, to):
            return [pltpu.make_async_remote_copy(
                src_ref=buf.at[slot], dst_ref=buf.at[slot], send_sem=send_sems.at[2 * j + i],
                recv_sem=recv_sems.at[2 * j + i], device_id=to, device_id_type=MESH)
                for i, buf in enumerate((bufv, bufw))]

        sends = [cp for j, r in enumerate(rels) for cp in copies(j, me, peer(r))]
        for cp in sends:
            cp.start()
        for j, r in enumerate(rels):
            px, py, pc = peer(r)
            for cp in copies(j, 4 * px + 2 * py + pc, peer(r)):
                cp.wait_recv()
        for cp in sends:
            cp.wait_send()

        tot_v = bufv[0]
        tot_w = bufw[0]
        for d in range(1, 8):
            tot_v = tot_v + bufv[d]
            tot_w = tot_w + bufw[d]
        totv_o[...] = tot_v
        totw_o[...] = tot_w

    vm = pl.BlockSpec(memory_space=pltpu.VMEM)
    args = [g[k] for k in SMALL_NAMES] + [loss_l]
    return pl.pallas_call(
        body, name="small_allreduce", in_specs=[vm] * len(args), out_specs=[vm, vm],
        out_shape=[jax.ShapeDtypeStruct((VEC_ROWS, 1024), F32), jax.ShapeDtypeStruct((8, CH, CH), F32)],
        scratch_shapes=[pltpu.VMEM((8, VEC_ROWS, 1024), F32), pltpu.VMEM((8, 8, CH, CH), F32),
                        pltpu.SemaphoreType.DMA((14,)), pltpu.SemaphoreType.DMA((14,))],
        compiler_params=pltpu.CompilerParams(vmem_limit_bytes=32 * 1024 * 1024),
    )(*args)


def _small_update(tot_v, tot_w, w, m, v):
    n = len(SMALL_NAMES)

    def body(*refs):
        totv_r, totw_r = refs[0], refs[1]
        w_r = dict(zip(SMALL_NAMES, refs[2:2 + n]))
        m_r = dict(zip(SMALL_NAMES, refs[2 + n:2 + 2 * n]))
        v_r = dict(zip(SMALL_NAMES, refs[2 + 2 * n:2 + 3 * n]))
        loss_o = refs[2 + 3 * n]
        outs = refs[3 + 3 * n:]
        loss_o[...] = jnp.sum(totv_r[LOSS_VEC_ROW:LOSS_VEC_ROW + 1, :], axis=-1, keepdims=True) + jnp.zeros((1, 128), F32)
        for i, name in enumerate(SMALL_NAMES):
            if name == "sg_w":
                gt = totw_r[...]
            else:
                rows, width = w_r[name].shape
                gt = totv_r[VEC_ROW[name]:VEC_ROW[name] + rows, 0:width]
            delta, mn, vn = _adamw(w_r[name][...], gt, m_r[name][...], v_r[name][...])
            outs[4 * i][...] = gt
            outs[4 * i + 1][...] = delta
            outs[4 * i + 2][...] = mn
            outs[4 * i + 3][...] = vn

    vm = pl.BlockSpec(memory_space=pltpu.VMEM)
    args = [tot_v, tot_w] + [d[k] for d in (w, m, v) for k in SMALL_NAMES]
    out_shape = [jax.ShapeDtypeStruct((1, 128), F32)]
    out_shape += [jax.ShapeDtypeStruct(w[k].shape, F32) for k in SMALL_NAMES for _ in range(4)]
    res = pl.pallas_call(
        body, name="small_update", in_specs=[vm] * len(args), out_specs=[vm] * len(out_shape), out_shape=out_shape,
        compiler_params=pltpu.CompilerParams(vmem_limit_bytes=32 * 1024 * 1024),
    )(*args)
    return res[0], {k: res[1 + 4 * i:5 + 4 * i] for i, k in enumerate(SMALL_NAMES)}


def _win_kernel_order(gathered):
    w_in = jnp.concatenate([gathered[k].reshape(D, 768)[:, :642] for k in range(4)], axis=1)
    return jnp.concatenate([w_in[:, :3 * AW], w_in[:, 3 * AW + NH:], w_in[:, 3 * AW:3 * AW + NH],
                            jnp.zeros((D, 128 - NH), w_in.dtype)], axis=1)


LATE_ROWS = 256 + 1024 + 1024 + 64 + 256


def _pack_late(w_out, w1, w2, plew, wg):
    return jnp.concatenate([w_out, w1, w2, plew.reshape(64, 1024), wg], axis=0)


def _unpack_late(gathered):
    return (gathered[:, 0:256].reshape(D, D), gathered[:, 256:1280], gathered[:, 1280:2304].reshape(DFF, D),
            gathered[:, 2304:2368].reshape(4, 256, 256), gathered[:, 2368:2624].reshape(D, D))


def _local_step(x, p, tgt, win_k, late_weights, token, on_tail_grads, small):
    T = x.shape[0]
    row = lambda n: small[n].reshape(1, -1)
    fbias = jnp.pad(row("f_bias"), ((0, 0), (0, 128 - NH))) + token[0:1, :]
    wm = _masked_sg_w(small["sg_w"].reshape(8, CH, CH))
    wmb = wm.astype(BF16)
    wmt = jnp.swapaxes(wm, 1, 2).astype(BF16)
    bsg = jnp.repeat(small["sg_b"].reshape(8, CH).T, DH, axis=1)
    ln_g, ln_b, gsg, gatt = row("sg_ln_g"), row("sg_ln_b"), row("sg_out_g"), row("att_out_g")
    gpre, gpm, gpf, gpff, bg = row("pre_mix_g"), row("post_mix_g"), row("pre_ffn_g"), row("post_ffn_g"), row("ple_gate_b")
    gsel = (jnp.arange(AW)[:, None] // DH == jnp.arange(128)[None, :]).astype(BF16)

    expand, shrink, pieces, qconst, one64, one67, pick64, pick67 = _head_consts()
    a, flog, zuv, ysgn, q8, k8, v8 = _pre_attn_fwd(
        x, gpre, win_k, fbias, ln_g, ln_b, wmb, bsg, gsg, expand, pieces, qconst, one67, one64)

    slabs = lambda t: jnp.swapaxes(t.reshape(T // TQ, TQ, NH * 128), 1, 2)
    qt8 = slabs(q8)
    lanes = jnp.arange(128)
    sel = jnp.stack([((lanes[:, None] == lanes[None, :] - DH * j) & (lanes[:, None] < DH)).astype(BF16)
                     for j in (0, 1)])

    yatt, lse = _flash_fwd(qt8, k8, slabs(v8), sel)
    wout, w1, w2, plew, wg = late_weights(lse)
    y, ov, h1, c2, sact, rr = _tail_fwd1(x, yatt, ysgn, gatt, wout, gpm, gpf, w1)
    ff, h2b, de, dpre, dh2, loss_l, dbg = _tail_fwd2(sact, h1, p, tgt, w2, gpff, wg, bg, plew)
    dff, dr, do, do8, dlt, dysg, dh1, dgpff, dgpf, dgpm, dgatt = _tail_bwd(
        dh2, ff, rr, h1, ov, yatt, w2, w1, wout, gpff, gpf, gpm, gatt, gsel, expand)
    dwout = _matmul_tn("grad_w_out", y, do)
    dw1 = _matmul_tn("grad_w_ff1", c2, dr, shards=4)
    dw2 = _matmul_tn("grad_w_ff2", sact, dff)
    dwg = _matmul_tn("grad_ple_gate_w", h2b, dpre)
    dplew = _matmul_tn("grad_ple_w", p, de, tn=256, shards=4)
    tail_token = on_tail_grads((dwout, dw1, dw2, dplew, dwg))
    dlt4 = jnp.pad(dlt[:, :NH].T.reshape(4, 2, T), ((0, 0), (0, 6), (0, 0))) + tail_token[0, 0]
    dqt, dk8, dv8 = _flash_bwd(q8, qt8, k8, v8, do8, slabs(do8), lse, dlt4)
    dx, dz, dgpre, dfb, dgsg, dlng, dlnb, dws, _, dsbt = _pre_attn_bwd(
        x, dh1, jnp.swapaxes(dqt, 1, 2).reshape(T, NH * 128), dk8, dv8, flog, zuv, dysg,
        gpre, win_k, ln_g, ln_b, wmb, wmt, bsg, gsg, gsel, shrink, pick64, pick67)

    dwin_k = _matmul_tn("grad_w_in", a, dz, tn=384)

    dsb = dsbt[:, :8].T
    gsmall = {"sg_w": dws, "f_bias": dfb, "sg_ln_g": dlng, "sg_ln_b": dlnb, "sg_b": dsb,
              "att_out_g": dgatt, "sg_out_g": dgsg, "pre_mix_g": dgpre, "post_mix_g": dgpm, "pre_ffn_g": dgpf,
              "post_ffn_g": dgpff, "ple_gate_b": dbg}
    return loss_l, dx, dwin_k, gsmall


def kernel(x, p, w_in, f_bias, sg_ln_g, sg_ln_b, sg_w, sg_b, att_out_g, sg_out_g, w_out, pre_mix_g, post_mix_g, pre_ffn_g, post_ffn_g, w_ff1, w_ff2, ple_w, ple_gate_w, ple_gate_b, loss_target, m_w_in, m_f_bias, m_sg_ln_g, m_sg_ln_b, m_sg_w, m_sg_b, m_att_out_g, m_sg_out_g, m_w_out, m_pre_mix_g, m_post_mix_g, m_pre_ffn_g, m_post_ffn_g, m_w_ff1, m_w_ff2, m_ple_w, m_ple_gate_w, m_ple_gate_b, v_w_in, v_f_bias, v_sg_ln_g, v_sg_ln_b, v_sg_w, v_sg_b, v_att_out_g, v_sg_out_g, v_w_out, v_pre_mix_g, v_post_mix_g, v_pre_ffn_g, v_post_ffn_g, v_w_ff1, v_w_ff2, v_ple_w, v_ple_gate_w, v_ple_gate_b):
    c = lax.axis_index("c")
    big = lambda t: (t[0][0], t[1][0], t[2][0], t[3][0], t[4][0], t[5][0])
    w_big = big((w_in, w_out, w_ff1, w_ff2, ple_w, ple_gate_w))
    m_big = big((m_w_in, m_w_out, m_w_ff1, m_w_ff2, m_ple_w, m_ple_gate_w))
    v_big = big((v_w_in, v_w_out, v_w_ff1, v_w_ff2, v_ple_w, v_ple_gate_w))
    small = {"sg_w": sg_w, "f_bias": f_bias, "sg_ln_g": sg_ln_g, "sg_ln_b": sg_ln_b, "sg_b": sg_b,
             "att_out_g": att_out_g, "sg_out_g": sg_out_g, "pre_mix_g": pre_mix_g, "post_mix_g": post_mix_g,
             "pre_ffn_g": pre_ffn_g, "post_ffn_g": post_ffn_g, "ple_gate_b": ple_gate_b}
    m_small = {"sg_w": m_sg_w, "f_bias": m_f_bias, "sg_ln_g": m_sg_ln_g, "sg_ln_b": m_sg_ln_b, "sg_b": m_sg_b,
               "att_out_g": m_att_out_g, "sg_out_g": m_sg_out_g, "pre_mix_g": m_pre_mix_g,
               "post_mix_g": m_post_mix_g, "pre_ffn_g": m_pre_ffn_g, "post_ffn_g": m_post_ffn_g,
               "ple_gate_b": m_ple_gate_b}
    v_small = {"sg_w": v_sg_w, "f_bias": v_f_bias, "sg_ln_g": v_sg_ln_g, "sg_ln_b": v_sg_ln_b, "sg_b": v_sg_b,
               "att_out_g": v_att_out_g, "sg_out_g": v_sg_out_g, "pre_mix_g": v_pre_mix_g,
               "post_mix_g": v_post_mix_g, "pre_ffn_g": v_pre_ffn_g, "post_ffn_g": v_post_ffn_g,
               "ple_gate_b": v_ple_gate_b}

    k_me = 2 * lax.axis_index("x") + lax.axis_index("y")
    own_slot = lambda got, mine: lax.dynamic_update_slice(got, mine[None], (k_me, 0, 0))
    late_mine = _pack_late(*w_big[1:]).astype(BF16)
    late = _gather_late_start(late_mine)
    win_mine = jnp.pad(w_big[0], ((0, 0), (0, 768 - 642))).reshape(768, 1024).astype(BF16)
    win_k = _win_kernel_order(own_slot(_gather_weights(win_mine), win_mine))
    late_weights = lambda after: _unpack_late(
        own_slot(_gather_late_wait(late[0], late[1], late[2], late[3], after), late_mine))

    names = ("w_in", "w_out", "w_ff1", "w_ff2", "ple_w", "ple_gate_w")
    c1 = jnp.reshape(c, (1,)).astype(jnp.int32)
    own_part = lambda parts, pss: [lax.dynamic_update_slice(pt, lax.dynamic_slice_in_dim(ps, k_me, 1, 0), (k_me, 0, 0))
                                   for pt, ps in zip(parts, pss)]
    tail = {}

    def on_tail_grads(grads):
        dwout, dw1, dw2, dplew, dwg = grads
        gs = [dwout.reshape(4, 256, D), dw1, dw2.reshape(4, D, D), dplew, dwg.reshape(4, 256, D)]
        gots = _swap_halves(gs, "late")
        pss = [_pair_sum(nm, c1, g, got) for nm, g, got in zip(names[1:], gs, gots)]
        tail["xch"] = _exchange_start(pss)
        return tail["xch"][4]

    loss_l, dx, dwin_k, gsmall = _local_step(
        x[0], p[0, 0], loss_target[0], win_k, late_weights, late[4], on_tail_grads, small)

    dwin = jnp.concatenate([dwin_k[:, :3 * AW], dwin_k[:, 5 * AW:5 * AW + NH], dwin_k[:, 3 * AW:5 * AW]], axis=1)
    dwin = jnp.pad(jnp.swapaxes(dwin.reshape(D, 4, 642), 0, 1), ((0, 0), (0, 0), (0, 768 - 642)))
    ps_in = [_pair_sum(names[0], c1, dwin, _swap_halves([dwin], "in")[0])]
    parts = own_part(_exchange_chips(ps_in), ps_in)
    xs, xr, ps_thru, land_thru, _ = tail["xch"]
    ps_late, landed = _exchange_wait(xs, xr, ps_thru, land_thru, dx)
    parts += own_part(landed, ps_late)
    ghs = [_reduce_chips(nm, pt) for nm, pt in zip(names, parts)]
    got2 = _share_grad(ghs, "all")
    padded = lambda t: (jnp.pad(t[0], ((0, 0), (0, 768 - 642))),) + tuple(t[1:])
    big_out = [_update(nm, c1, gh, g2, w, m, v) for nm, gh, g2, w, m, v in
               zip(names, ghs, got2, padded(w_big), padded(m_big), padded(v_big))]
    big_out = [[big_out[j][i][:, :642] if j == 0 else big_out[j][i] for j in range(6)] for i in range(4)]

    view = lambda t: t.reshape(t.shape[-3:]) if t.ndim == 4 else t.reshape(t.shape[-2:])
    views = lambda d: {k: view(d[k]) for k in SMALL_NAMES}
    tot_v, tot_w = _small_allreduce(gsmall, loss_l)
    loss11, res_s = _small_update(tot_v, tot_w, views(small), views(m_small), views(v_small))
    loss = loss11[0, 0]

    def small_out(i, name):
        return res_s[name][i].reshape(small[name].shape)

    order = ["w_in", "f_bias", "sg_ln_g", "sg_ln_b", "sg_w", "sg_b", "att_out_g", "sg_out_g", "w_out",
             "pre_mix_g", "post_mix_g", "pre_ffn_g", "post_ffn_g", "w_ff1", "w_ff2", "ple_w", "ple_gate_w",
             "ple_gate_b"]
    big_idx = {"w_in": 0, "w_out": 1, "w_ff1": 2, "w_ff2": 3, "ple_w": 4, "ple_gate_w": 5}
    outs = [loss, dx[None]]
    for i in range(4):
        for name in order:
            if name in big_idx:
                outs.append(big_out[i][big_idx[name]][None])
            else:
                outs.append(small_out(i, name))
    return tuple(outs)
```

```python
import math

import jax
import jax.numpy as jnp
from jax import lax
from jax.experimental import pallas as pl
from jax.experimental.pallas import tpu as pltpu

F32 = jnp.float32
BF16 = jnp.bfloat16
MESH = pl.DeviceIdType.MESH

D = 1024
DH = 64
NH = 8
AW = 512
CH = 128
DFF = 4096
ZW = 5 * AW + 128
EPS = 1e-6
NEG = -1e30
MASKED = -2e30

TM = 256
TQ = 256

LR, B1, B2, AEPS, WD, STEP = 0.001, 0.9, 0.999, 1e-08, 0.01, 10
BC1 = 1.0 - B1 ** STEP
BC2 = 1.0 - B2 ** STEP

VEC_NAMES = ("f_bias", "sg_ln_g", "sg_ln_b", "sg_b", "att_out_g", "sg_out_g", "pre_mix_g",
             "post_mix_g", "pre_ffn_g", "post_ffn_g", "ple_gate_b")


def _dot(a, b):
    return jnp.dot(a, b, preferred_element_type=F32)


def _dot_nt(a, b):
    return lax.dot_general(a, b, (((1,), (1,)), ((), ())), preferred_element_type=F32)


def _dot_tn(a, b):
    return lax.dot_general(a, b, (((0,), (0,)), ((), ())), preferred_element_type=F32)


def _split3(x):
    h = x.astype(BF16)
    r = x - h.astype(F32)
    m = r.astype(BF16)
    l = (r - m.astype(F32)).astype(BF16)
    return h, m, l


def _dot01(sel, x):
    h, m, l = _split3(x)
    return _dot(sel, h) + _dot(sel, m) + _dot(sel, l)


def _dot01_r(x, sel):
    h, m, l = _split3(x)
    return _dot(h, sel) + _dot(m, sel) + _dot(l, sel)


def _dot01_tn(x, sel):
    h, m, l = _split3(x)
    return _dot_tn(h, sel) + _dot_tn(m, sel) + _dot_tn(l, sel)


def _rs(x, n):
    return lax.rsqrt(jnp.sum(x * x, axis=-1, keepdims=True) * (1.0 / n) + EPS)


def _rms_bwd(dn, x, rs, g, n):
    w = dn * g
    dx = rs * w - x * ((rs * rs * rs) * (1.0 / n) * jnp.sum(w * x, axis=-1, keepdims=True))
    return dx, jnp.sum(dn * x * rs, axis=0, keepdims=True)


_GC = math.sqrt(2.0 / math.pi)


def _gelu(x):
    t = jnp.tanh(_GC * (x + 0.044715 * x * x * x))
    return 0.5 * x * (1.0 + t), t


def _gelu_grad(x, t):
    return 0.5 * (1.0 + t) + 0.5 * x * (1.0 - t * t) * (_GC * (1.0 + 3.0 * 0.044715 * x * x))


def _params(vmem_mb, sem=("arbitrary",)):
    return pltpu.CompilerParams(dimension_semantics=sem, vmem_limit_bytes=vmem_mb * 1024 * 1024)


def _row_call(name, body, T, tm, tiled, resident, outs, accs, scratch=(), reverse=False, vmem_mb=48):
    nt = T // tm
    n_t, n_r, n_o, n_a = len(tiled), len(resident), len(outs), len(accs)

    def kern(*refs):
        t_refs = refs[:n_t]
        r_hbm = refs[n_t:n_t + n_r]
        o_refs = refs[n_t + n_r:n_t + n_r + n_o]
        a_refs = refs[n_t + n_r + n_o:n_t + n_r + n_o + n_a]
        r_vmem = refs[n_t + n_r + n_o + n_a:n_t + 2 * n_r + n_o + n_a]
        s_refs = refs[n_t + 2 * n_r + n_o + n_a:]

        @pl.when(pl.program_id(0) == 0)
        def _():
            for h, v in zip(r_hbm, r_vmem):
                pltpu.sync_copy(h, v)
            for a in a_refs + s_refs:
                a[...] = jnp.zeros(a.shape, a.dtype)

        body(t_refs, r_vmem, o_refs, a_refs, s_refs)

    if reverse:
        idx = lambda i: (nt - 1 - i, 0)
        idx_t = lambda i: (nt - 1 - i, 0, 0)
    else:
        idx = lambda i: (i, 0)
        idx_t = lambda i: (i, 0, 0)
    arrays, in_specs = [], []
    for a in tiled:
        if isinstance(a, tuple):
            arrays.append(a[0])
            in_specs.append(pl.BlockSpec((None, a[0].shape[1], tm), idx_t))
        else:
            arrays.append(a)
            in_specs.append(pl.BlockSpec((tm, a.shape[1]), idx))
    in_specs += [pl.BlockSpec(memory_space=pl.ANY) for _ in resident]
    out_shape, out_specs = [], []
    for o in outs:
        if len(o) == 3:
            out_shape.append(jax.ShapeDtypeStruct((nt, o[0], tm), o[1]))
            out_specs.append(pl.BlockSpec((None, o[0], tm), idx_t))
        else:
            out_shape.append(jax.ShapeDtypeStruct((T, o[0]), o[1]))
            out_specs.append(pl.BlockSpec((tm, o[0]), idx))
    out_shape += [jax.ShapeDtypeStruct(s, F32) for s in accs]
    out_specs += [pl.BlockSpec(s, lambda i, n=len(s): (0,) * n) for s in accs]
    scratch_shapes = [pltpu.VMEM(r.shape, r.dtype) for r in resident]
    scratch_shapes += [pltpu.VMEM(s, F32) for s in scratch]
    return pl.pallas_call(
        kern, name=name, grid=(nt,), in_specs=in_specs, out_specs=out_specs, out_shape=out_shape,
        scratch_shapes=scratch_shapes, compiler_params=_params(vmem_mb),
    )(*arrays, *resident)


def _sg_forward(zu, zv, wm_ref, bsg, lng, lnb, mixed_ref, tm):
    gu, tu = _gelu(zu)
    vg, tv = _gelu(zv)
    mu = jnp.sum(vg, axis=-1, keepdims=True) * (1.0 / AW)
    xc = vg - mu
    rstd = lax.rsqrt(jnp.sum(xc * xc, axis=-1, keepdims=True) * (1.0 / AW) + EPS)
    xhat = xc * rstd
    vvb = (xhat * lng + lnb).astype(BF16)
    lane = lax.broadcasted_iota(jnp.int32, (CH, 128), 1)
    for c in range(tm // CH):
        for j in range(4):
            blk = vvb[c * CH:(c + 1) * CH, j * 128:(j + 1) * 128]
            m0 = _dot(wm_ref[2 * j], blk)
            m1 = _dot(wm_ref[2 * j + 1], blk)
            mixed_ref[c * CH:(c + 1) * CH, j * 128:(j + 1) * 128] = (
                jnp.where(lane < DH, m0, m1) + bsg[:, j * 128:(j + 1) * 128])
    return gu, tu, tv, xhat, rstd, vvb, mixed_ref[...]


def _head_consts():
    src = jnp.arange(AW)
    dst = (src // DH) * 128 + src % DH
    wide = jnp.arange(NH * 128)
    expand = (dst[:, None] == wide[None, :]).astype(BF16)
    heads = jnp.arange(128)
    pieces = jnp.stack([((heads[:, None] * 128 + DH + i == wide[None, :]) & (heads[:, None] < NH)).astype(BF16)
                        for i in range(3)])
    spare = wide % 128 - DH
    qconst = jnp.where((spare >= 0) & (spare < 3), -1.0, 0.0).astype(F32)[None, :]
    one64 = jnp.where(spare == 0, 1.0, 0.0).astype(F32)[None, :]
    one67 = jnp.where(spare == 3, 1.0, 0.0).astype(F32)[None, :]
    pick64 = ((wide[:, None] == heads[None, :] * 128 + DH) & (heads[None, :] < NH)).astype(BF16)
    pick67 = ((wide[:, None] == heads[None, :] * 128 + DH + 3) & (heads[None, :] < NH)).astype(BF16)
    return expand, expand.T, pieces, qconst, one64, one67, pick64, pick67


def _masked_sg_w(sg_w):
    r = lax.broadcasted_iota(jnp.int32, (CH, CH), 0)
    c = lax.broadcasted_iota(jnp.int32, (CH, CH), 1)
    return jnp.where((c <= r)[None], sg_w, 0.0)


def _pre_attn_fwd(x, gpre, win, fbias, lng, lnb, wm, bsg, gsg, expand, pieces, qconst, kconst, vconst):
    T = x.shape[0]
    tm = TM

    def body(t, r, o, a, s):
        (x_ref,) = t
        gpre_r, win_r, fb_r, lng_r, lnb_r, wm_r, bsg_r, gsg_r, ex_r, pc_r, qc_r, kc_r, vc_r = r
        a_o, flog_o, zuv_o, ysgn_o, q8_o, k8_o, v8_o = o
        carry_ref, mixed_ref = s
        xv = x_ref[...]
        av = (xv * _rs(xv, D) * gpre_r[...]).astype(BF16)
        a_o[...] = av
        z = _dot(av, win_r[...])
        zu = z[:, 3 * AW:4 * AW]
        zv = z[:, 4 * AW:5 * AW]
        zuv_o[:, 0:AW] = zu
        zuv_o[:, AW:2 * AW] = zv
        zf = z[:, 5 * AW:] + fb_r[...]
        flog_o[...] = zf
        lane = lax.broadcasted_iota(jnp.int32, (tm, 128), 1)
        logf = jnp.where(lane < NH, jnp.minimum(zf, 0.0) - jnp.log(1.0 + jnp.exp(-jnp.abs(zf))), 0.0)
        rr = lax.broadcasted_iota(jnp.int32, (tm, tm), 0)
        cc = lax.broadcasted_iota(jnp.int32, (tm, tm), 1)
        tri = (cc <= rr).astype(BF16)
        cum = _dot01(tri, logf) + carry_ref[...]
        carry_ref[...] = cum[tm - 1:tm, :]
        ex = ex_r[...]
        q8_o[...] = (_dot((z[:, 0:AW] * (DH ** -0.5)).astype(BF16), ex) + qc_r[...]).astype(BF16)
        ch, cm, cl = _split3(cum)
        k8_o[...] = (_dot(z[:, AW:2 * AW].astype(BF16), ex) + _dot(ch, pc_r[0]) + _dot(cm, pc_r[1])
                     + _dot(cl, pc_r[2]) + kc_r[...]).astype(BF16)
        v8_o[...] = (_dot(z[:, 2 * AW:3 * AW].astype(BF16), ex) + vc_r[...]).astype(BF16)
        gu, _, _, _, _, _, mixed = _sg_forward(zu, zv, wm_r, bsg_r[...], lng_r[...], lnb_r[...], mixed_ref, tm)
        ysg = gu * mixed
        ysgn_o[...] = (ysg * _rs(ysg, AW) * gsg_r[...]).astype(BF16)

    return _row_call(
        "pre_attn_fwd", body, T, tm, [x],
        [gpre, win, fbias, lng, lnb, wm, bsg, gsg, expand, pieces, qconst, kconst, vconst],
        [(D, BF16), (128, F32), (2 * AW, F32), (AW, BF16), (NH * 128, BF16), (NH * 128, BF16), (NH * 128, BF16)], [],
        scratch=[(1, 128), (tm, AW)], vmem_mb=48)


def _flash_fwd(qt8, k8, vt8, sel):
    T = k8.shape[0]
    nq = T // TQ

    def body(qt_ref, k_ref, vt_ref, sel_ref, o_ref, l_ref, u_scr, p_scr):
        qi = pl.program_id(1)
        qts = (qt_ref[0:128, :], qt_ref[128:256, :])
        dmat = (lax.broadcasted_iota(jnp.int32, (TQ, TQ), 0) - lax.broadcasted_iota(jnp.int32, (TQ, TQ), 1))
        u_scr[1] = jnp.full((2, TQ, TQ), MASKED, F32)
        p_scr[...] = jnp.zeros(p_scr.shape, BF16)

        def sub(t, carry, sc, sb, masked):
            blk_c = jnp.clip(t - 2, 0, qi)
            off_a = pl.multiple_of(jnp.minimum(t, qi) * TQ, TQ)
            new = []
            for j in (0, 1):
                m, al, acc = carry[j]
                acc = al * acc + _dot(vt_ref[blk_c, j * 128:(j + 1) * 128, :], p_scr[sc, j])
                m_new = jnp.maximum(m, jnp.max(u_scr[sb, j], axis=0, keepdims=True))
                p_scr[sb, j] = jnp.exp(u_scr[sb, j] - m_new).astype(BF16)
                u = _dot(k_ref[pl.ds(off_a, TQ), j * 128:(j + 1) * 128], qts[j])
                u_scr[sc, j] = jnp.where(dmat <= (qi - t) * TQ, u, MASKED) if masked else u
                new.append((m_new, jnp.exp(m - m_new), acc))
            return tuple(new)

        def pair(t2, carry, masked):
            return sub(2 * t2 + 1, sub(2 * t2, carry, 0, 1, masked), 1, 0, masked)

        init = tuple((jnp.full((1, TQ), NEG, F32), jnp.ones((1, TQ), F32), jnp.zeros((128, TQ), F32))
                     for _ in (0, 1))
        carry = lax.fori_loop(0, qi // 2, lambda t2, cr: pair(t2, cr, False), init)
        (m0, _, a0), (m1, _, a1) = pair(qi // 2 + 1, pair(qi // 2, carry, True), True)
        l0 = a0[DH:DH + 1, :]
        l1 = a1[DH:DH + 1, :]
        o_ref[...] = _dot01_tn(a0 * (1.0 / l0), sel_ref[0]) + _dot01_tn(a1 * (1.0 / l1), sel_ref[1])
        l_ref[0:1, :] = m0 + jnp.log(l0)
        l_ref[1:2, :] = m1 + jnp.log(l1)
        l_ref[2:8, :] = jnp.zeros((6, TQ), F32)

    return pl.pallas_call(
        body, name="flash_fwd", grid=(4, nq),
        in_specs=[pl.BlockSpec((None, 256, TQ), lambda h, i: (i, h, 0)),
                  pl.BlockSpec((T, 256), lambda h, i: (0, h)),
                  pl.BlockSpec((nq, 256, TQ), lambda h, i: (0, h, 0)),
                  pl.BlockSpec((2, 128, 128), lambda h, i: (0, 0, 0))],
        out_specs=[pl.BlockSpec((TQ, 128), lambda h, i: (i, h)),
                   pl.BlockSpec((None, 8, TQ), lambda h, i: (h, 0, i))],
        out_shape=[jax.ShapeDtypeStruct((T, AW), F32), jax.ShapeDtypeStruct((4, 8, T), F32)],
        scratch_shapes=[pltpu.VMEM((2, 2, TQ, TQ), F32), pltpu.VMEM((2, 2, TQ, TQ), BF16)],
        compiler_params=_params(40, ("arbitrary", "arbitrary")),
    )(qt8, k8, vt8, sel)


def _flash_bwd(q8, qt8, k8, v8, do8, dot8, lse, dlt):
    T = q8.shape[0]
    nk = T // TQ

    def body(q_ref, qt_ref, k_ref, v_ref, do_ref, dot_ref, l_ref, d_ref, dqt_ref, dk_ref, dv_ref,
             u_scr, dp_scr, p_scr, ds_scr):
        kb = pl.program_id(1)
        n = nk - kb

        @pl.when(kb == 0)
        def _():
            dqt_ref[...] = jnp.zeros(dqt_ref.shape, F32)

        dk_ref[...] = jnp.zeros(dk_ref.shape, F32)
        dv_ref[...] = jnp.zeros(dv_ref.shape, F32)
        u_scr[1] = jnp.full((2, TQ, TQ), MASKED, F32)
        dp_scr[1] = jnp.zeros((2, TQ, TQ), F32)
        p_scr[...] = jnp.zeros(p_scr.shape, BF16)
        ds_scr[...] = jnp.zeros(ds_scr.shape, BF16)
        dmat = (lax.broadcasted_iota(jnp.int32, (TQ, TQ), 0) - lax.broadcasted_iota(jnp.int32, (TQ, TQ), 1))
        ks = (k_ref[:, 0:128], k_ref[:, 128:256])
        vs = (v_ref[:, 0:128], v_ref[:, 128:256])

        def sub(t, sc, sb):
            blk_a = kb + jnp.minimum(t, n - 1)
            blk_c = kb + jnp.clip(t - 2, 0, n - 1)
            off_b = pl.multiple_of((kb + jnp.clip(t - 1, 0, n - 1)) * TQ, TQ)
            off_c = pl.multiple_of(blk_c * TQ, TQ)
            lim = jnp.where(t < n, t * TQ, -TQ)
            for j in (0, 1):
                hl = slice(j * 128, (j + 1) * 128)
                dqt_ref[blk_c, hl, :] += _dot_tn(ks[j], ds_scr[sc, j])
                dk_ref[:, hl] += _dot(ds_scr[sc, j], q_ref[pl.ds(off_c, TQ), hl])
                dv_ref[:, hl] += _dot(p_scr[sc, j], do_ref[pl.ds(off_c, TQ), hl])
                p = jnp.exp(u_scr[sb, j] - l_ref[j:j + 1, pl.ds(off_b, TQ)])
                p_scr[sb, j] = p.astype(BF16)
                ds_scr[sb, j] = (p * (dp_scr[sb, j] - d_ref[j:j + 1, pl.ds(off_b, TQ)])).astype(BF16)
                u_scr[sc, j] = jnp.where(dmat <= lim, _dot(ks[j], qt_ref[blk_a, hl, :]), MASKED)
                dp_scr[sc, j] = _dot(vs[j], dot_ref[blk_a, hl, :])

        def it(t2, carry):
            sub(2 * t2, 0, 1)
            sub(2 * t2 + 1, 1, 0)
            return carry

        lax.fori_loop(0, n // 2 + 1, it, 0)

        @pl.when(n % 2 == 1)
        def _():
            sub(n + 1, 0, 1)

    return pl.pallas_call(
        body, name="flash_bwd", grid=(4, nk),
        in_specs=[pl.BlockSpec((T, 256), lambda h, i: (0, h)),
                  pl.BlockSpec((nk, 256, TQ), lambda h, i: (0, h, 0)),
                  pl.BlockSpec((TQ, 256), lambda h, i: (i, h)),
                  pl.BlockSpec((TQ, 256), lambda h, i: (i, h)),
                  pl.BlockSpec((T, 256), lambda h, i: (0, h)),
                  pl.BlockSpec((nk, 256, TQ), lambda h, i: (0, h, 0)),
                  pl.BlockSpec((None, 8, T), lambda h, i: (h, 0, 0)),
                  pl.BlockSpec((None, 8, T), lambda h, i: (h, 0, 0))],
        out_specs=[pl.BlockSpec((nk, 256, TQ), lambda h, i: (0, h, 0)),
                   pl.BlockSpec((TQ, 256), lambda h, i: (i, h)),
                   pl.BlockSpec((TQ, 256), lambda h, i: (i, h))],
        out_shape=[jax.ShapeDtypeStruct((nk, NH * 128, TQ), F32), jax.ShapeDtypeStruct((T, NH * 128), F32),
                   jax.ShapeDtypeStruct((T, NH * 128), F32)],
        scratch_shapes=[pltpu.VMEM((2, 2, TQ, TQ), F32), pltpu.VMEM((2, 2, TQ, TQ), F32),
                        pltpu.VMEM((2, 2, TQ, TQ), BF16), pltpu.VMEM((2, 2, TQ, TQ), BF16)],
        compiler_params=_params(56, ("arbitrary", "arbitrary")),
    )(q8, qt8, k8, v8, do8, dot8, lse, dlt)


def _tail_fwd1(x, yatt, ysgn, gatt, wout, gpm, gpf, w1):
    T = x.shape[0]

    def body(t, r, o, a, s):
        x_ref, ya_ref, ys_ref = t
        gatt_r, wout_r, gpm_r, gpf_r, w1_r = r
        y_o, o_o, h1_o, c2_o, s_o, rr_o = o
        ya = ya_ref[...]
        yan = (ya * _rs(ya, AW) * gatt_r[...]).astype(BF16)
        y_o[:, 0:AW] = yan
        y_o[:, AW:] = ys_ref[...]
        ov = _dot(yan, wout_r[0:AW, :]) + _dot(ys_ref[...], wout_r[AW:, :])
        o_o[...] = ov
        h1 = x_ref[...] + ov * _rs(ov, D) * gpm_r[...]
        h1_o[...] = h1
        c2 = (h1 * _rs(h1, D) * gpf_r[...]).astype(BF16)
        c2_o[...] = c2
        for k in range(4):
            rr = jnp.maximum(_dot(c2, w1_r[k]), 0.0)
            rr_o[:, k * D:(k + 1) * D] = rr.astype(BF16)
            s_o[:, k * D:(k + 1) * D] = (rr * rr).astype(BF16)

    return _row_call(
        "tail_fwd1", body, T, TM, [x, yatt, ysgn], [gatt, wout, gpm, gpf, w1],
        [(D, BF16), (D, F32), (D, F32), (D, BF16), (DFF, BF16), (DFF, BF16)], [], vmem_mb=48)


def _tail_fwd2(sact, h1, p, tgt, w2, gpff, wg, bg, wpe):
    T = h1.shape[0]

    def body(t, r, o, a, s):
        s_ref, h1_ref, p_ref, t_ref = t
        w2_r, gpff_r, wg_r, bg_r, wpe_r = r
        ff_o, h2b_o, de_o, dpre_o, dh2_o = o
        loss_a, dbg_a = a
        ff = _dot(s_ref[...], w2_r[...])
        ff_o[...] = ff
        h2 = h1_ref[...] + ff * _rs(ff, D) * gpff_r[...]
        h2b = h2.astype(BF16)
        h2b_o[...] = h2b
        gate = 1.0 / (1.0 + jnp.exp(-(_dot(h2b, wg_r[...]) + bg_r[...])))
        pb = p_ref[...].astype(BF16)
        e = jnp.concatenate([_dot(pb, wpe_r[k]) for k in range(4)], axis=1)
        diff = h2 + gate * e - t_ref[...]
        loss_a[...] += jnp.sum(diff * diff, axis=0, keepdims=True)
        dh3 = diff * (1.0 / D)
        de_o[...] = (dh3 * gate).astype(BF16)
        dpre = dh3 * e * gate * (1.0 - gate)
        dbg_a[...] += jnp.sum(dpre, axis=0, keepdims=True)
        dpb = dpre.astype(BF16)
        dpre_o[...] = dpb
        dh2_o[...] = dh3 + _dot_nt(dpb, wg_r[...])

    return _row_call(
        "tail_fwd2", body, T, TM, [sact, h1, p, tgt], [w2, gpff, wg, bg, wpe],
        [(D, F32), (D, BF16), (D, BF16), (D, BF16), (D, F32)], [(1, D), (1, D)], vmem_mb=48)


def _tail_bwd(dh2, ff, rr, h1, ov, yatt, w2, w1, wout, gpff, gpf, gpm, gatt, gsel, expand):
    T = dh2.shape[0]

    def body(t, r, o, a, s):
        dh2_ref, ff_ref, rr_ref, h1_ref, o_ref, ya_ref = t
        w2_r, w1_r, wout_r, gpff_r, gpf_r, gpm_r, gatt_r, gsel_r, ex_r = r
        dff_o, dr_o, do_o, do8_o, dlt_o, dysg_o, dh1_o = o
        dgpff_a, dgpf_a, dgpm_a, dgatt_a = a
        dh2v = dh2_ref[...]
        ffv = ff_ref[...]
        dff, dg = _rms_bwd(dh2v, ffv, _rs(ffv, D), gpff_r[...], D)
        dgpff_a[...] += dg
        dffb = dff.astype(BF16)
        dff_o[...] = dffb
        drb = (_dot_nt(dffb, w2_r[...]) * (2.0 * rr_ref[...].astype(F32))).astype(BF16)
        dr_o[...] = drb
        dc2 = _dot_nt(drb[:, 0:D], w1_r[0])
        for k in range(1, 4):
            dc2 = dc2 + _dot_nt(drb[:, k * D:(k + 1) * D], w1_r[k])
        h1v = h1_ref[...]
        d1, dg = _rms_bwd(dc2, h1v, _rs(h1v, D), gpf_r[...], D)
        dgpf_a[...] += dg
        dh1 = dh2v + d1
        dh1_o[...] = dh1
        ovv = o_ref[...]
        dov, dg = _rms_bwd(dh1, ovv, _rs(ovv, D), gpm_r[...], D)
        dgpm_a[...] += dg
        dob = dov.astype(BF16)
        do_o[...] = dob
        dysg_o[...] = _dot_nt(dob, wout_r[AW:, :])
        dyan = _dot_nt(dob, wout_r[0:AW, :])
        ya = ya_ref[...]
        dya, dg = _rms_bwd(dyan, ya, _rs(ya, AW), gatt_r[...], AW)
        dgatt_a[...] += dg
        do8_o[...] = _dot(dya.astype(BF16), ex_r[...]).astype(BF16)
        dlt_o[...] = _dot01_r(dya * ya, gsel_r[...])

    return _row_call(
        "tail_bwd", body, T, TM, [dh2, ff, rr, h1, ov, yatt],
        [w2, w1, wout, gpff, gpf, gpm, gatt, gsel, expand],
        [(D, BF16), (DFF, BF16), (D, BF16), (NH * 128, BF16), (128, F32), (AW, F32), (D, F32)],
        [(1, D), (1, D), (1, D), (1, AW)], vmem_mb=56)


def _pre_attn_bwd(x, dh1, dq8, dk8, dv8, flog, zuv, dysg, gpre, win, lng, lnb, wm, wmt, bsg, gsg, gsel, shrink, pick64, pick67):
    T = x.shape[0]
    tm = TM

    def body(t, r, o, a, s):
        x_ref, dh1_ref, dq_ref, dk_ref, dv_ref, fl_ref, zuv_ref, dys_ref = t
        gpre_r, win_r, lng_r, lnb_r, wm_r, wmt_r, bsg_r, gsg_r, gsel_r, sh_r, p64_r, p67_r = r
        dx_o, dz_o = o
        dgpre_a, dfb_a, dgsg_a, dlng_a, dlnb_a, dws_a, dbs_a, dsb_a = a
        carry_ref, mixed_ref, dvv_ref = s
        dq8v = dq_ref[...]
        dk8v = dk_ref[...]
        dcv = _dot01_r(dq8v, p67_r[...]) + _dot01_r(dk8v, p64_r[...])
        rr = lax.broadcasted_iota(jnp.int32, (tm, tm), 0)
        cc = lax.broadcasted_iota(jnp.int32, (tm, tm), 1)
        triu = (cc >= rr).astype(BF16)
        dlogf = _dot01(triu, dcv) + carry_ref[...]
        carry_ref[...] = dlogf[0:1, :]
        dzf = dlogf * (1.0 / (1.0 + jnp.exp(fl_ref[...])))
        dfb_a[...] += jnp.sum(dzf, axis=0, keepdims=True)
        dz_o[:, 5 * AW:] = dzf.astype(BF16)
        zu = zuv_ref[:, 0:AW]
        zv = zuv_ref[:, AW:]
        gu, tu, tv, xhat, rstd, vvb, mixed = _sg_forward(
            zu, zv, wm_r, bsg_r[...], lng_r[...], lnb_r[...], mixed_ref, tm)
        ysg = gu * mixed
        dysg_n = dys_ref[...]
        dys, dg = _rms_bwd(dysg_n, ysg, _rs(ysg, AW), gsg_r[...], AW)
        dgsg_a[...] += dg
        dgu = dys * mixed
        dmix = dys * gu
        dmb = dmix.astype(BF16)
        lane = lax.broadcasted_iota(jnp.int32, (CH, 128), 1)
        lo = lane < DH
        for c in range(tm // CH):
            rows = slice(c * CH, (c + 1) * CH)
            dbs_a[...] += dmix[rows, :]
            for j in range(4):
                cols = slice(j * 128, (j + 1) * 128)
                dmblk = dmb[rows, cols]
                vblk = vvb[rows, cols]
                d0 = _dot(wmt_r[2 * j], dmblk)
                d1 = _dot(wmt_r[2 * j + 1], dmblk)
                dvv_ref[rows, cols] = jnp.where(lo, d0, d1)
                dws_a[2 * j] += _dot_nt(jnp.where(lo, dmblk, jnp.zeros_like(dmblk)), vblk)
                dws_a[2 * j + 1] += _dot_nt(jnp.where(lo, jnp.zeros_like(dmblk), dmblk), vblk)
        dvv = dvv_ref[...]
        dlng_a[...] += jnp.sum(dvv * xhat, axis=0, keepdims=True)
        dlnb_a[...] += jnp.sum(dvv, axis=0, keepdims=True)
        dxh = dvv * lng_r[...]
        dvg = rstd * (dxh - jnp.sum(dxh, axis=-1, keepdims=True) * (1.0 / AW)
                      - xhat * (jnp.sum(dxh * xhat, axis=-1, keepdims=True) * (1.0 / AW)))
        dz_o[:, 3 * AW:4 * AW] = (dgu * _gelu_grad(zu, tu)).astype(BF16)
        dz_o[:, 4 * AW:5 * AW] = (dvg * _gelu_grad(zv, tv)).astype(BF16)
        dz_o[:, 0:AW] = _dot((dq8v * (DH ** -0.5)).astype(BF16), sh_r[...]).astype(BF16)
        dz_o[:, AW:2 * AW] = _dot(dk8v.astype(BF16), sh_r[...]).astype(BF16)
        dz_o[:, 2 * AW:3 * AW] = _dot(dv_ref[...].astype(BF16), sh_r[...]).astype(BF16)
        da = _dot_nt(dz_o[...], win_r[...])
        xv = x_ref[...]
        dxa, dg = _rms_bwd(da, xv, _rs(xv, D), gpre_r[...], D)
        dgpre_a[...] += dg
        dx_o[...] = dh1_ref[...] + dxa

        @pl.when(pl.program_id(0) == T // tm - 1)
        def _():
            dsb_a[...] = _dot01_r(dbs_a[...], gsel_r[...])

    outs = _row_call(
        "pre_attn_bwd", body, T, tm, [x, dh1, dq8, dk8, dv8, flog, zuv, dysg],
        [gpre, win, lng, lnb, wm, wmt, bsg, gsg, gsel, shrink, pick64, pick67],
        [(D, F32), (ZW, BF16)],
        [(1, D), (1, 128), (1, AW), (1, AW), (1, AW), (8, CH, CH), (CH, AW), (CH, 128)],
        scratch=[(1, 128), (tm, AW), (tm, AW)], reverse=True, vmem_mb=48)
    return outs


def _matmul_tn(name, a, b, tn=512, tt=2048, shards=1):
    T, K = a.shape
    N = b.shape[1]
    tk = min(K, 1024)
    tn = min(tn, N // shards)
    tt = min(tt, T)
    nj = N // shards // tn

    def body(a_ref, b_ref, o_ref):
        @pl.when(pl.program_id(2) == 0)
        def _():
            o_ref[...] = jnp.zeros(o_ref.shape, F32)

        o_ref[...] += _dot_tn(a_ref[...].astype(BF16), b_ref[...].astype(BF16))

    if shards == 1:
        out_shape = jax.ShapeDtypeStruct((K, N), F32)
        out_spec = pl.BlockSpec((tk, tn), lambda i, j, t: (i, j))
    else:
        out_shape = jax.ShapeDtypeStruct((shards, K, N // shards), F32)
        out_spec = pl.BlockSpec((None, tk, tn), lambda i, j, t: (j // nj, i, j % nj))
    return pl.pallas_call(
        body, name=name, grid=(K // tk, N // tn, T // tt),
        in_specs=[pl.BlockSpec((tt, tk), lambda i, j, t: (t, i)),
                  pl.BlockSpec((tt, tn), lambda i, j, t: (t, j))],
        out_specs=out_spec, out_shape=out_shape,
        compiler_params=_params(40, ("arbitrary", "arbitrary", "arbitrary")),
    )(a, b)


def _me():
    return lax.axis_index("x"), lax.axis_index("y"), lax.axis_index("c")


HBM_SPEC = pl.BlockSpec(memory_space=pltpu.HBM)


def _gather_weights(mine):
    half = mine.shape[0] // 2

    def body(mine_ref, out_ref, ici_send, ici_recv, d2d_send, d2d_recv):
        x, y, c = _me()
        k_me = 2 * x + y
        chips = [(1 - x, y), (x, 1 - y), (1 - x, 1 - y)]
        my_rows = pl.ds(pl.multiple_of(c * half, 16), half)
        sib_rows = pl.ds(pl.multiple_of((1 - c) * half, 16), half)

        def over_ici(j, k, to):
            src = mine_ref.at[my_rows] if k is None else out_ref.at[k, my_rows]
            return pltpu.make_async_remote_copy(
                src_ref=src, dst_ref=out_ref.at[k_me if k is None else k, my_rows], send_sem=ici_send.at[j],
                recv_sem=ici_recv.at[j], device_id=to, device_id_type=MESH)

        def over_d2d(j, k, rows):
            return pltpu.make_async_remote_copy(
                src_ref=out_ref.at[k, rows], dst_ref=out_ref.at[k, rows], send_sem=d2d_send.at[j],
                recv_sem=d2d_recv.at[j], device_id=(x, y, 1 - c), device_id_type=MESH)

        first = [over_ici(j, None, (cx, cy, c)) for j, (cx, cy) in enumerate(chips)]
        for cp in first:
            cp.start()
        passed = [over_d2d(j, 2 * cx + cy, my_rows) for j, (cx, cy) in enumerate(chips)]
        for j, (cx, cy) in enumerate(chips):
            over_ici(j, 2 * cx + cy, (cx, cy, c)).wait_recv()
            passed[j].start()
        for j, (cx, cy) in enumerate(chips):
            over_d2d(j, 2 * cx + cy, sib_rows).wait_recv()
        for cp in first + passed:
            cp.wait_send()

    return pl.pallas_call(
        body, name="gather_weights", in_specs=[HBM_SPEC], out_specs=HBM_SPEC,
        out_shape=jax.ShapeDtypeStruct((4,) + mine.shape, mine.dtype),
        scratch_shapes=[pltpu.SemaphoreType.DMA((3,)), pltpu.SemaphoreType.DMA((3,)), pltpu.SemaphoreType.DMA((3,)),
                        pltpu.SemaphoreType.DMA((3,))],
    )(mine)


SEM_SPEC = pl.BlockSpec(memory_space=pltpu.SEMAPHORE)
EFFECT = pltpu.SideEffectType.DATAFLOW_SIDE_EFFECTING


def _gather_late_start(mine):
    def body(mine_ref, land_ref, send_sems, recv_sems, mine_thru, land_thru, token):
        x, y, c = _me()
        k_me = 2 * x + y
        for j, (cx, cy) in enumerate([(1 - x, y), (x, 1 - y), (1 - x, 1 - y)]):
            pltpu.make_async_remote_copy(
                src_ref=mine_ref, dst_ref=land_ref.at[k_me], send_sem=send_sems.at[j], recv_sem=recv_sems.at[j],
                device_id=(cx, cy, c), device_id_type=MESH).start()
        token[...] = jnp.zeros(token.shape, F32)

    land = lax.empty((4,) + mine.shape, mine.dtype)
    return pl.pallas_call(
        body, name="gather_late_start",
        out_shape=(pltpu.SemaphoreType.DMA((3,)), pltpu.SemaphoreType.DMA((3,)), pltpu.HBM(mine.shape, mine.dtype),
                   pltpu.HBM(land.shape, land.dtype), jax.ShapeDtypeStruct((8, 128), F32)),
        in_specs=(HBM_SPEC, HBM_SPEC),
        out_specs=(SEM_SPEC, SEM_SPEC, HBM_SPEC, HBM_SPEC, pl.BlockSpec(memory_space=pltpu.VMEM)),
        input_output_aliases={0: 2, 1: 3},
        compiler_params=pltpu.CompilerParams(has_side_effects=EFFECT),
    )(pltpu.with_memory_space_constraint(mine, pltpu.HBM), pltpu.with_memory_space_constraint(land, pltpu.HBM))


def _gather_late_wait(send_sems, recv_sems, mine_thru, land_thru, after):
    def body(mine_ref, land_ref, send_sems, recv_sems, after_ref, mine_dead, got_ref):
        x, y, c = _me()
        for j, (cx, cy) in enumerate([(1 - x, y), (x, 1 - y), (1 - x, 1 - y)]):
            cp = pltpu.make_async_remote_copy(
                src_ref=mine_ref, dst_ref=land_ref.at[2 * cx + cy], send_sem=send_sems.at[j],
                recv_sem=recv_sems.at[j], device_id=(cx, cy, c), device_id_type=MESH)
            cp.wait_send()
            cp.wait_recv()

    return pl.pallas_call(
        body, name="gather_late_wait",
        out_shape=(pltpu.HBM(mine_thru.shape, mine_thru.dtype), pltpu.HBM(land_thru.shape, land_thru.dtype)),
        in_specs=(HBM_SPEC, HBM_SPEC, SEM_SPEC, SEM_SPEC, pl.BlockSpec(memory_space=pl.ANY)),
        out_specs=(HBM_SPEC, HBM_SPEC), input_output_aliases={0: 0, 1: 1},
        compiler_params=pltpu.CompilerParams(has_side_effects=EFFECT),
    )(mine_thru, land_thru, send_sems, recv_sems, after)[1]


def _swap_halves(gs, tag):
    n = len(gs)

    def body(*refs):
        g_refs, got_refs, send_sems, recv_sems = refs[:n], refs[n:2 * n], refs[2 * n], refs[2 * n + 1]
        x, y, c = _me()
        cps = []
        for i, (g_ref, got_ref) in enumerate(zip(g_refs, got_refs)):
            half = g_ref.shape[1] // 2
            theirs = pl.multiple_of((1 - c) * half, 16)
            cps.append(pltpu.make_async_remote_copy(
                src_ref=g_ref.at[:, pl.ds(theirs, half), :], dst_ref=got_ref, send_sem=send_sems.at[i],
                recv_sem=recv_sems.at[i], device_id=(x, y, 1 - c), device_id_type=MESH))
        for cp in cps:
            cp.start()
        for cp in cps:
            cp.wait()

    return pl.pallas_call(
        body, name="swap_halves_" + tag, in_specs=[HBM_SPEC] * n, out_specs=[HBM_SPEC] * n,
        out_shape=[jax.ShapeDtypeStruct((4, g.shape[1] // 2, g.shape[2]), F32) for g in gs],
        scratch_shapes=[pltpu.SemaphoreType.DMA((n,)), pltpu.SemaphoreType.DMA((n,))],
    )(*gs)


def _pair_sum(name, c1, g, got):
    half, cols = got.shape[1], got.shape[2]

    def body(c_ref, a_ref, b_ref, o_ref):
        o_ref[...] = (a_ref[...] + b_ref[...]).astype(BF16)

    return pl.pallas_call(
        body, name="pair_sum_" + name,
        grid_spec=pltpu.PrefetchScalarGridSpec(
            num_scalar_prefetch=1, grid=(4,),
            in_specs=[pl.BlockSpec((1, half, cols), lambda k, c_ref: (k, c_ref[0], 0)),
                      pl.BlockSpec((1, half, cols), lambda k, c_ref: (k, 0, 0))],
            out_specs=pl.BlockSpec((1, half, cols), lambda k, c_ref: (k, 0, 0))),
        out_shape=jax.ShapeDtypeStruct(got.shape, BF16), compiler_params=_params(32),
    )(c1, g, got)


def _exchange_chips(pss):
    n = len(pss)

    def body(*refs):
        ps_refs, out_refs = refs[:n], refs[n:2 * n]
        send_sems, recv_sems = refs[2 * n:]
        x, y, c = _me()
        k_me = 2 * x + y
        chips = [(1 - x, y), (x, 1 - y), (1 - x, 1 - y)]
        sends = []
        for i, (ps_ref, out_ref) in enumerate(zip(ps_refs, out_refs)):
            for j, (cx, cy) in enumerate(chips):
                sends.append(pltpu.make_async_remote_copy(
                    src_ref=ps_ref.at[2 * cx + cy], dst_ref=out_ref.at[k_me], send_sem=send_sems.at[3 * i + j],
                    recv_sem=recv_sems.at[3 * i + j], device_id=(cx, cy, c), device_id_type=MESH))
        for cp in sends:
            cp.start()
        for i, (ps_ref, out_ref) in enumerate(zip(ps_refs, out_refs)):
            for j, (cx, cy) in enumerate(chips):
                pltpu.make_async_remote_copy(
                    src_ref=ps_ref.at[k_me], dst_ref=out_ref.at[2 * cx + cy], send_sem=send_sems.at[3 * i + j],
                    recv_sem=recv_sems.at[3 * i + j], device_id=(cx, cy, c), device_id_type=MESH).wait_recv()
        for cp in sends:
            cp.wait_send()

    return pl.pallas_call(
        body, name="exchange_chips", in_specs=[HBM_SPEC] * n, out_specs=[HBM_SPEC] * n,
        out_shape=[jax.ShapeDtypeStruct(ps.shape, ps.dtype) for ps in pss],
        scratch_shapes=[pltpu.SemaphoreType.DMA((3 * n,)), pltpu.SemaphoreType.DMA((3 * n,))],
    )(*pss)


def _exchange_start(pss):
    n = len(pss)

    def body(*refs):
        ps_refs, land_refs = refs[:n], refs[n:2 * n]
        send_sems, recv_sems = refs[2 * n], refs[2 * n + 1]
        token = refs[4 * n + 2]
        x, y, c = _me()
        k_me = 2 * x + y
        for i, (ps_ref, land_ref) in enumerate(zip(ps_refs, land_refs)):
            for j, (cx, cy) in enumerate([(1 - x, y), (x, 1 - y), (1 - x, 1 - y)]):
                pltpu.make_async_remote_copy(
                    src_ref=ps_ref.at[2 * cx + cy], dst_ref=land_ref.at[k_me], send_sem=send_sems.at[3 * i + j],
                    recv_sem=recv_sems.at[3 * i + j], device_id=(cx, cy, c), device_id_type=MESH).start()
        token[...] = jnp.zeros(token.shape, F32)

    lands = [lax.empty(ps.shape, ps.dtype) for ps in pss]
    hbm = lambda t: pltpu.HBM(t.shape, t.dtype)
    res = pl.pallas_call(
        body, name="exchange_start",
        out_shape=(pltpu.SemaphoreType.DMA((3 * n,)), pltpu.SemaphoreType.DMA((3 * n,)), *[hbm(t) for t in pss],
                   *[hbm(t) for t in lands], jax.ShapeDtypeStruct((8, 128), F32)),
        in_specs=(HBM_SPEC,) * (2 * n),
        out_specs=(SEM_SPEC, SEM_SPEC) + (HBM_SPEC,) * (2 * n) + (pl.BlockSpec(memory_space=pltpu.VMEM),),
        input_output_aliases={i: 2 + i for i in range(2 * n)},
        compiler_params=pltpu.CompilerParams(has_side_effects=EFFECT),
    )(*[pltpu.with_memory_space_constraint(t, pltpu.HBM) for t in list(pss) + lands])
    return res[0], res[1], res[2:2 + n], res[2 + n:2 + 2 * n], res[2 + 2 * n]


def _exchange_wait(send_sems, recv_sems, ps_thru, land_thru, after):
    n = len(ps_thru)

    def body(*refs):
        ps_refs, land_refs = refs[:n], refs[n:2 * n]
        send_sems, recv_sems = refs[2 * n], refs[2 * n + 1]
        x, y, c = _me()
        k_me = 2 * x + y
        for i, (ps_ref, land_ref) in enumerate(zip(ps_refs, land_refs)):
            for j, (cx, cy) in enumerate([(1 - x, y), (x, 1 - y), (1 - x, 1 - y)]):
                cp = pltpu.make_async_remote_copy(
                    src_ref=ps_ref.at[k_me], dst_ref=land_ref.at[2 * cx + cy], send_sem=send_sems.at[3 * i + j],
                    recv_sem=recv_sems.at[3 * i + j], device_id=(cx, cy, c), device_id_type=MESH)
                cp.wait_send()
                cp.wait_recv()

    hbm = lambda t: pltpu.HBM(t.shape, t.dtype)
    res = pl.pallas_call(
        body, name="exchange_wait",
        out_shape=tuple(hbm(t) for t in list(ps_thru) + list(land_thru)),
        in_specs=(HBM_SPEC,) * (2 * n) + (SEM_SPEC, SEM_SPEC, pl.BlockSpec(memory_space=pl.ANY)),
        out_specs=(HBM_SPEC,) * (2 * n), input_output_aliases={i: i for i in range(2 * n)},
        compiler_params=pltpu.CompilerParams(has_side_effects=EFFECT),
    )(*ps_thru, *land_thru, send_sems, recv_sems, after)
    return res[:n], res[n:]


def _adamw(w, g, m, v):
    m = B1 * m + (1.0 - B1) * g
    v = B2 * v + (1.0 - B2) * (g * g)
    delta = -LR * ((m / BC1) / (jnp.sqrt(v / BC2) + AEPS) + WD * w)
    return delta, m, v


def _reduce_chips(name, parts):
    half, cols = parts.shape[1], parts.shape[2]

    def body(p_ref, o_ref):
        f = lambda k: p_ref[k].astype(F32)
        o_ref[...] = ((f(0) + f(1)) + f(2)) + f(3)

    return pl.pallas_call(
        body, name="reduce_chips_" + name, grid=(1,),
        in_specs=[pl.BlockSpec((4, half, cols), lambda i: (0, 0, 0))],
        out_specs=pl.BlockSpec((half, cols), lambda i: (0, 0)),
        out_shape=jax.ShapeDtypeStruct((half, cols), F32), compiler_params=_params(32),
    )(parts)


def _share_grad(ghs, tag):
    n = len(ghs)

    def body(*refs):
        g_refs, got_refs, send_sems, recv_sems = refs[:n], refs[n:2 * n], refs[2 * n], refs[2 * n + 1]
        x, y, c = _me()
        cps = [pltpu.make_async_remote_copy(
            src_ref=g_ref, dst_ref=got_ref, send_sem=send_sems.at[i], recv_sem=recv_sems.at[i],
            device_id=(x, y, 1 - c), device_id_type=MESH) for i, (g_ref, got_ref) in enumerate(zip(g_refs, got_refs))]
        for cp in cps:
            cp.start()
        for cp in cps:
            cp.wait()

    return pl.pallas_call(
        body, name="share_grad_" + tag, in_specs=[HBM_SPEC] * n, out_specs=[HBM_SPEC] * n,
        out_shape=[jax.ShapeDtypeStruct(g.shape, F32) for g in ghs],
        scratch_shapes=[pltpu.SemaphoreType.DMA((n,)), pltpu.SemaphoreType.DMA((n,))],
    )(*ghs)


def _update(name, c1, gh, got, w, m, v):
    half, cols = gh.shape

    def body(c_ref, gh_ref, got_ref, w_ref, m_ref, v_ref, g_o, d_o, m_o, v_o):
        g = jnp.where(pl.program_id(0) == c_ref[0], gh_ref[...], got_ref[...])
        delta, mn, vn = _adamw(w_ref[...], g, m_ref[...], v_ref[...])
        g_o[...] = g
        d_o[...] = delta
        m_o[...] = mn
        v_o[...] = vn

    same = pl.BlockSpec((half, cols), lambda h, c_ref: (0, 0))
    rows = pl.BlockSpec((half, cols), lambda h, c_ref: (h, 0))
    return pl.pallas_call(
        body, name="update_" + name,
        grid_spec=pltpu.PrefetchScalarGridSpec(
            num_scalar_prefetch=1, grid=(2,), in_specs=[same, same, rows, rows, rows],
            out_specs=[rows, rows, rows, rows]),
        out_shape=[jax.ShapeDtypeStruct(w.shape, F32)] * 4, compiler_params=_params(40),
    )(c1, gh, got, w, m, v)


SMALL_NAMES = ("sg_w",) + VEC_NAMES
VEC_ROWS = 24
VEC_ROW = {"f_bias": 0, "sg_ln_g": 1, "sg_ln_b": 2, "att_out_g": 3, "sg_out_g": 4, "pre_mix_g": 5,
           "post_mix_g": 6, "pre_ffn_g": 7, "sg_b": 8, "post_ffn_g": 16, "ple_gate_b": 17}
LOSS_VEC_ROW = 18


def _small_pack(g, loss_l):
    n = len(SMALL_NAMES)

    def body(*refs):
        g_r = dict(zip(SMALL_NAMES, refs[0:n]))
        loss_r, vec_o, w_o = refs[n:]
        vec_o[...] = jnp.zeros((VEC_ROWS, 1024), F32)
        for name in VEC_NAMES:
            val = g_r[name][...]
            vec_o[pl.ds(VEC_ROW[name], val.shape[0]), pl.ds(0, val.shape[1])] = val
        vec_o[pl.ds(LOSS_VEC_ROW, 1), :] = loss_r[...] * (0.5 / D)
        rr = lax.broadcasted_iota(jnp.int32, (CH, CH), 0)
        cc = lax.broadcasted_iota(jnp.int32, (CH, CH), 1)
        w_o[...] = jnp.where((cc <= rr)[None], g_r["sg_w"][...], 0.0)

    vm = pl.BlockSpec(memory_space=pltpu.VMEM)
    args = [g[k] for k in SMALL_NAMES] + [loss_l]
    return pl.pallas_call(
        body, name="small_pack", in_specs=[vm] * len(args), out_specs=[vm, vm],
        out_shape=[jax.ShapeDtypeStruct((VEC_ROWS, 1024), F32), jax.ShapeDtypeStruct((8, CH, CH), F32)],
    )(*args)


def _small_peers(x, y, c):
    rels = [(rx, ry, rc) for rx in (0, 1) for ry in (0, 1) for rc in (0, 1)][1:]
    return [((x + rx) % 2, (y + ry) % 2, (c + rc) % 2) for rx, ry, rc in rels]


def _small_start(vec, w8):
    def body(vec_ref, w_ref, lv_ref, lw_ref, send_sems, recv_sems, vec_thru, w_thru, lv_thru, lw_thru, token):
        x, y, c = _me()
        me = 4 * x + 2 * y + c
        for j, to in enumerate(_small_peers(x, y, c)):
            for i, (src, land) in enumerate(((vec_ref, lv_ref), (w_ref, lw_ref))):
                pltpu.make_async_remote_copy(
                    src_ref=src, dst_ref=land.at[me], send_sem=send_sems.at[2 * j + i],
                    recv_sem=recv_sems.at[2 * j + i], device_id=to, device_id_type=MESH).start()
        token[...] = jnp.zeros(token.shape, F32)

    ops = [vec, w8, lax.empty((8,) + vec.shape, F32), lax.empty((8,) + w8.shape, F32)]
    hbm = lambda t: pltpu.HBM(t.shape, t.dtype)
    res = pl.pallas_call(
        body, name="small_start",
        out_shape=(pltpu.SemaphoreType.DMA((14,)), pltpu.SemaphoreType.DMA((14,)), *[hbm(t) for t in ops],
                   jax.ShapeDtypeStruct((8, 128), F32)),
        in_specs=(HBM_SPEC,) * 4,
        out_specs=(SEM_SPEC, SEM_SPEC) + (HBM_SPEC,) * 4 + (pl.BlockSpec(memory_space=pltpu.VMEM),),
        input_output_aliases={i: 2 + i for i in range(4)},
        compiler_params=pltpu.CompilerParams(has_side_effects=EFFECT),
    )(*[pltpu.with_memory_space_constraint(t, pltpu.HBM) for t in ops])
    return res[0], res[1], res[2:6], res[6]


def _small_wait(send_sems, recv_sems, thru, after):
    def body(vec_ref, w_ref, lv_ref, lw_ref, send_sems, recv_sems, after_ref, vec_o, w_o, lv_o, lw_o):
        x, y, c = _me()
        for j, (px, py, pc) in enumerate(_small_peers(x, y, c)):
            for i, (src, land) in enumerate(((vec_ref, lv_ref), (w_ref, lw_ref))):
                cp = pltpu.make_async_remote_copy(
                    src_ref=src, dst_ref=land.at[4 * px + 2 * py + pc], send_sem=send_sems.at[2 * j + i],
                    recv_sem=recv_sems.at[2 * j + i], device_id=(px, py, pc), device_id_type=MESH)
                cp.wait_send()
                cp.wait_recv()

    hbm = lambda t: pltpu.HBM(t.shape, t.dtype)
    return pl.pallas_call(
        body, name="small_wait", out_shape=tuple(hbm(t) for t in thru),
        in_specs=(HBM_SPEC,) * 4 + (SEM_SPEC, SEM_SPEC, pl.BlockSpec(memory_space=pl.ANY)),
        out_specs=(HBM_SPEC,) * 4, input_output_aliases={i: i for i in range(4)},
        compiler_params=pltpu.CompilerParams(has_side_effects=EFFECT),
    )(*thru, send_sems, recv_sems, after)


def _small_update(all_v, all_w, w, m, v):
    n = len(SMALL_NAMES)

    def body(*refs):
        allv_r, allw_r = refs[0], refs[1]
        tot_v = allv_r[0]
        tot_w = allw_r[0]
        for d in range(1, 8):
            tot_v = tot_v + allv_r[d]
            tot_w = tot_w + allw_r[d]
        w_r = dict(zip(SMALL_NAMES, refs[2:2 + n]))
        m_r = dict(zip(SMALL_NAMES, refs[2 + n:2 + 2 * n]))
        v_r = dict(zip(SMALL_NAMES, refs[2 + 2 * n:2 + 3 * n]))
        loss_o = refs[2 + 3 * n]
        outs = refs[3 + 3 * n:]
        loss_o[...] = jnp.sum(tot_v[LOSS_VEC_ROW:LOSS_VEC_ROW + 1, :], axis=-1, keepdims=True) + jnp.zeros((1, 128), F32)
        for i, name in enumerate(SMALL_NAMES):
            if name == "sg_w":
                gt = tot_w
            else:
                rows, width = w_r[name].shape
                gt = tot_v[VEC_ROW[name]:VEC_ROW[name] + rows, 0:width]
            delta, mn, vn = _adamw(w_r[name][...], gt, m_r[name][...], v_r[name][...])
            outs[4 * i][...] = gt
            outs[4 * i + 1][...] = delta
            outs[4 * i + 2][...] = mn
            outs[4 * i + 3][...] = vn

    vm = pl.BlockSpec(memory_space=pltpu.VMEM)
    args = [all_v, all_w] + [d[k] for d in (w, m, v) for k in SMALL_NAMES]
    out_shape = [jax.ShapeDtypeStruct((1, 128), F32)]
    out_shape += [jax.ShapeDtypeStruct(w[k].shape, F32) for k in SMALL_NAMES for _ in range(4)]
    res = pl.pallas_call(
        body, name="small_update", in_specs=[vm] * len(args), out_specs=[vm] * len(out_shape), out_shape=out_shape,
        compiler_params=pltpu.CompilerParams(vmem_limit_bytes=32 * 1024 * 1024),
    )(*args)
    return res[0], {k: res[1 + 4 * i:5 + 4 * i] for i, k in enumerate(SMALL_NAMES)}


def _win_kernel_order(gathered):
    w_in = jnp.concatenate([gathered[k].reshape(D, 768)[:, :642] for k in range(4)], axis=1)
    return jnp.concatenate([w_in[:, :3 * AW], w_in[:, 3 * AW + NH:], w_in[:, 3 * AW:3 * AW + NH],
                            jnp.zeros((D, 128 - NH), w_in.dtype)], axis=1)


LATE_ROWS = 256 + 1024 + 1024 + 64 + 256


def _pack_late(w_out, w1, w2, plew, wg):
    return jnp.concatenate([w_out, w1, w2, plew.reshape(64, 1024), wg], axis=0)


def _unpack_late(gathered):
    return (gathered[:, 0:256].reshape(D, D), gathered[:, 256:1280], gathered[:, 1280:2304].reshape(DFF, D),
            gathered[:, 2304:2368].reshape(4, 256, 256), gathered[:, 2368:2624].reshape(D, D))


def _local_step(x, p, tgt, win_k, late_weights, token, on_tail_grads, on_small_grads, small):
    T = x.shape[0]
    row = lambda n: small[n].reshape(1, -1)
    fbias = jnp.pad(row("f_bias"), ((0, 0), (0, 128 - NH))) + token[0:1, :]
    wm = _masked_sg_w(small["sg_w"].reshape(8, CH, CH))
    wmb = wm.astype(BF16)
    wmt = jnp.swapaxes(wm, 1, 2).astype(BF16)
    bsg = jnp.repeat(small["sg_b"].reshape(8, CH).T, DH, axis=1)
    ln_g, ln_b, gsg, gatt = row("sg_ln_g"), row("sg_ln_b"), row("sg_out_g"), row("att_out_g")
    gpre, gpm, gpf, gpff, bg = row("pre_mix_g"), row("post_mix_g"), row("pre_ffn_g"), row("post_ffn_g"), row("ple_gate_b")
    gsel = (jnp.arange(AW)[:, None] // DH == jnp.arange(128)[None, :]).astype(BF16)

    expand, shrink, pieces, qconst, one64, one67, pick64, pick67 = _head_consts()
    a, flog, zuv, ysgn, q8, k8, v8 = _pre_attn_fwd(
        x, gpre, win_k, fbias, ln_g, ln_b, wmb, bsg, gsg, expand, pieces, qconst, one67, one64)

    slabs = lambda t: jnp.swapaxes(t.reshape(T // TQ, TQ, NH * 128), 1, 2)
    qt8 = slabs(q8)
    lanes = jnp.arange(128)
    sel = jnp.stack([((lanes[:, None] == lanes[None, :] - DH * j) & (lanes[:, None] < DH)).astype(BF16)
                     for j in (0, 1)])

    yatt, lse = _flash_fwd(qt8, k8, slabs(v8), sel)
    wout, w1, w2, plew, wg = late_weights(lse)
    y, ov, h1, c2, sact, rr = _tail_fwd1(x, yatt, ysgn, gatt, wout, gpm, gpf, w1)
    ff, h2b, de, dpre, dh2, loss_l, dbg = _tail_fwd2(sact, h1, p, tgt, w2, gpff, wg, bg, plew)
    dff, dr, do, do8, dlt, dysg, dh1, dgpff, dgpf, dgpm, dgatt = _tail_bwd(
        dh2, ff, rr, h1, ov, yatt, w2, w1, wout, gpff, gpf, gpm, gatt, gsel, expand)
    dwout = _matmul_tn("grad_w_out", y, do)
    dw1 = _matmul_tn("grad_w_ff1", c2, dr, shards=4)
    dw2 = _matmul_tn("grad_w_ff2", sact, dff)
    dwg = _matmul_tn("grad_ple_gate_w", h2b, dpre)
    dplew = _matmul_tn("grad_ple_w", p, de, tn=256, shards=4)
    tail_token = on_tail_grads((dwout, dw1, dw2, dplew, dwg))
    dlt4 = jnp.pad(dlt[:, :NH].T.reshape(4, 2, T), ((0, 0), (0, 6), (0, 0))) + tail_token[0, 0]
    dqt, dk8, dv8 = _flash_bwd(q8, qt8, k8, v8, do8, slabs(do8), lse, dlt4)
    dx, dz, dgpre, dfb, dgsg, dlng, dlnb, dws, _, dsbt = _pre_attn_bwd(
        x, dh1, jnp.swapaxes(dqt, 1, 2).reshape(T, NH * 128), dk8, dv8, flog, zuv, dysg,
        gpre, win_k, ln_g, ln_b, wmb, wmt, bsg, gsg, gsel, shrink, pick64, pick67)


    dsb = dsbt[:, :8].T
    gsmall = {"sg_w": dws, "f_bias": dfb, "sg_ln_g": dlng, "sg_ln_b": dlnb, "sg_b": dsb,
              "att_out_g": dgatt, "sg_out_g": dgsg, "pre_mix_g": dgpre, "post_mix_g": dgpm, "pre_ffn_g": dgpf,
              "post_ffn_g": dgpff, "ple_gate_b": dbg}
    on_small_grads(gsmall, loss_l)
    dwin_k = _matmul_tn("grad_w_in", a, dz, tn=384)
    return loss_l, dx, dwin_k, gsmall


def kernel(x, p, w_in, f_bias, sg_ln_g, sg_ln_b, sg_w, sg_b, att_out_g, sg_out_g, w_out, pre_mix_g, post_mix_g, pre_ffn_g, post_ffn_g, w_ff1, w_ff2, ple_w, ple_gate_w, ple_gate_b, loss_target, m_w_in, m_f_bias, m_sg_ln_g, m_sg_ln_b, m_sg_w, m_sg_b, m_att_out_g, m_sg_out_g, m_w_out, m_pre_mix_g, m_post_mix_g, m_pre_ffn_g, m_post_ffn_g, m_w_ff1, m_w_ff2, m_ple_w, m_ple_gate_w, m_ple_gate_b, v_w_in, v_f_bias, v_sg_ln_g, v_sg_ln_b, v_sg_w, v_sg_b, v_att_out_g, v_sg_out_g, v_w_out, v_pre_mix_g, v_post_mix_g, v_pre_ffn_g, v_post_ffn_g, v_w_ff1, v_w_ff2, v_ple_w, v_ple_gate_w, v_ple_gate_b):
    c = lax.axis_index("c")
    big = lambda t: (t[0][0], t[1][0], t[2][0], t[3][0], t[4][0], t[5][0])
    w_big = big((w_in, w_out, w_ff1, w_ff2, ple_w, ple_gate_w))
    m_big = big((m_w_in, m_w_out, m_w_ff1, m_w_ff2, m_ple_w, m_ple_gate_w))
    v_big = big((v_w_in, v_w_out, v_w_ff1, v_w_ff2, v_ple_w, v_ple_gate_w))
    small = {"sg_w": sg_w, "f_bias": f_bias, "sg_ln_g": sg_ln_g, "sg_ln_b": sg_ln_b, "sg_b": sg_b,
             "att_out_g": att_out_g, "sg_out_g": sg_out_g, "pre_mix_g": pre_mix_g, "post_mix_g": post_mix_g,
             "pre_ffn_g": pre_ffn_g, "post_ffn_g": post_ffn_g, "ple_gate_b": ple_gate_b}
    m_small = {"sg_w": m_sg_w, "f_bias": m_f_bias, "sg_ln_g": m_sg_ln_g, "sg_ln_b": m_sg_ln_b, "sg_b": m_sg_b,
               "att_out_g": m_att_out_g, "sg_out_g": m_sg_out_g, "pre_mix_g": m_pre_mix_g,
               "post_mix_g": m_post_mix_g, "pre_ffn_g": m_pre_ffn_g, "post_ffn_g": m_post_ffn_g,
               "ple_gate_b": m_ple_gate_b}
    v_small = {"sg_w": v_sg_w, "f_bias": v_f_bias, "sg_ln_g": v_sg_ln_g, "sg_ln_b": v_sg_ln_b, "sg_b": v_sg_b,
               "att_out_g": v_att_out_g, "sg_out_g": v_sg_out_g, "pre_mix_g": v_pre_mix_g,
               "post_mix_g": v_post_mix_g, "pre_ffn_g": v_pre_ffn_g, "post_ffn_g": v_post_ffn_g,
               "ple_gate_b": v_ple_gate_b}

    k_me = 2 * lax.axis_index("x") + lax.axis_index("y")
    own_slot = lambda got, mine: lax.dynamic_update_slice(got, mine[None], (k_me, 0, 0))
    late_mine = _pack_late(*w_big[1:]).astype(BF16)
    late = _gather_late_start(late_mine)
    win_mine = jnp.pad(w_big[0], ((0, 0), (0, 768 - 642))).reshape(768, 1024).astype(BF16)
    win_k = _win_kernel_order(own_slot(_gather_weights(win_mine), win_mine))
    late_weights = lambda after: _unpack_late(
        own_slot(_gather_late_wait(late[0], late[1], late[2], late[3], after), late_mine))

    names = ("w_in", "w_out", "w_ff1", "w_ff2", "ple_w", "ple_gate_w")
    c1 = jnp.reshape(c, (1,)).astype(jnp.int32)
    own_part = lambda parts, pss: [lax.dynamic_update_slice(pt, lax.dynamic_slice_in_dim(ps, k_me, 1, 0), (k_me, 0, 0))
                                   for pt, ps in zip(parts, pss)]
    tail = {}

    def on_tail_grads(grads):
        dwout, dw1, dw2, dplew, dwg = grads
        gs = [dwout.reshape(4, 256, D), dw1, dw2.reshape(4, D, D), dplew, dwg.reshape(4, 256, D)]
        gots = _swap_halves(gs, "late")
        pss = [_pair_sum(nm, c1, g, got) for nm, g, got in zip(names[1:], gs, gots)]
        tail["xch"] = _exchange_start(pss)
        return tail["xch"][4]

    def on_small_grads(gsmall, loss_l):
        tail["small"] = _small_start(*_small_pack(gsmall, loss_l))

    loss_l, dx, dwin_k, gsmall = _local_step(
        x[0], p[0, 0], loss_target[0], win_k, late_weights, late[4], on_tail_grads, on_small_grads, small)

    dwin = jnp.concatenate([dwin_k[:, :3 * AW], dwin_k[:, 5 * AW:5 * AW + NH], dwin_k[:, 3 * AW:5 * AW]], axis=1)
    dwin = jnp.pad(jnp.swapaxes(dwin.reshape(D, 4, 642), 0, 1), ((0, 0), (0, 0), (0, 768 - 642)))
    ps_in = [_pair_sum(names[0], c1, dwin, _swap_halves([dwin], "in")[0])]
    parts = own_part(_exchange_chips(ps_in), ps_in)
    xs, xr, ps_thru, land_thru, _ = tail["xch"]
    ps_late, landed = _exchange_wait(xs, xr, ps_thru, land_thru, dx)
    parts += own_part(landed, ps_late)
    ghs = [_reduce_chips(nm, pt) for nm, pt in zip(names, parts)]
    got2 = _share_grad(ghs, "all")
    padded = lambda t: (jnp.pad(t[0], ((0, 0), (0, 768 - 642))),) + tuple(t[1:])
    big_out = [_update(nm, c1, gh, g2, w, m, v) for nm, gh, g2, w, m, v in
               zip(names, ghs, got2, padded(w_big), padded(m_big), padded(v_big))]
    big_out = [[big_out[j][i][:, :642] if j == 0 else big_out[j][i] for j in range(6)] for i in range(4)]

    view = lambda t: t.reshape(t.shape[-3:]) if t.ndim == 4 else t.reshape(t.shape[-2:])
    views = lambda d: {k: view(d[k]) for k in SMALL_NAMES}
    me = 4 * lax.axis_index("x") + 2 * lax.axis_index("y") + c
    ss, sr, sthru, _ = tail["small"]
    vec, w8, lv, lw = _small_wait(ss, sr, sthru, big_out[1][3])
    all_v = lax.dynamic_update_slice(lv, vec[None], (me, 0, 0))
    all_w = lax.dynamic_update_slice(lw, w8[None], (me, 0, 0, 0))
    loss11, res_s = _small_update(all_v, all_w, views(small), views(m_small), views(v_small))
    loss = loss11[0, 0]

    def small_out(i, name):
        return res_s[name][i].reshape(small[name].shape)

    order = ["w_in", "f_bias", "sg_ln_g", "sg_ln_b", "sg_w", "sg_b", "att_out_g", "sg_out_g", "w_out",
             "pre_mix_g", "post_mix_g", "pre_ffn_g", "post_ffn_g", "w_ff1", "w_ff2", "ple_w", "ple_gate_w",
             "ple_gate_b"]
    big_idx = {"w_in": 0, "w_out": 1, "w_ff1": 2, "w_ff2": 3, "ple_w": 4, "ple_gate_w": 5}
    outs = [loss, dx[None]]
    for i in range(4):
        for name in order:
            if name in big_idx:
                outs.append(big_out[i][big_idx[name]][None])
            else:
                outs.append(small_out(i, name))
    return tuple(outs)
```

```python
import math

import jax
import jax.numpy as jnp
from jax import lax
from jax.experimental import pallas as pl
from jax.experimental.pallas import tpu as pltpu

F32 = jnp.float32
BF16 = jnp.bfloat16
MESH = pl.DeviceIdType.MESH

D = 1024
DH = 64
NH = 8
AW = 512
CH = 128
DFF = 4096
ZW = 5 * AW + 128
EPS = 1e-6
NEG = -1e30
MASKED = -2e30

TM = 256
TQ = 256

LR, B1, B2, AEPS, WD, STEP = 0.001, 0.9, 0.999, 1e-08, 0.01, 10
BC1 = 1.0 - B1 ** STEP
BC2 = 1.0 - B2 ** STEP

VEC_NAMES = ("f_bias", "sg_ln_g", "sg_ln_b", "sg_b", "att_out_g", "sg_out_g", "pre_mix_g",
             "post_mix_g", "pre_ffn_g", "post_ffn_g", "ple_gate_b")


def _dot(a, b):
    return jnp.dot(a, b, preferred_element_type=F32)


def _dot_nt(a, b):
    return lax.dot_general(a, b, (((1,), (1,)), ((), ())), preferred_element_type=F32)


def _dot_tn(a, b):
    return lax.dot_general(a, b, (((0,), (0,)), ((), ())), preferred_element_type=F32)


def _split3(x):
    h = x.astype(BF16)
    r = x - h.astype(F32)
    m = r.astype(BF16)
    l = (r - m.astype(F32)).astype(BF16)
    return h, m, l


def _dot01(sel, x):
    h, m, l = _split3(x)
    return _dot(sel, h) + _dot(sel, m) + _dot(sel, l)


def _dot01_r(x, sel):
    h, m, l = _split3(x)
    return _dot(h, sel) + _dot(m, sel) + _dot(l, sel)


def _dot01_tn(x, sel):
    h, m, l = _split3(x)
    return _dot_tn(h, sel) + _dot_tn(m, sel) + _dot_tn(l, sel)


def _rs(x, n):
    return lax.rsqrt(jnp.sum(x * x, axis=-1, keepdims=True) * (1.0 / n) + EPS)


def _rms_bwd(dn, x, rs, g, n):
    w = dn * g
    dx = rs * w - x * ((rs * rs * rs) * (1.0 / n) * jnp.sum(w * x, axis=-1, keepdims=True))
    return dx, jnp.sum(dn * x * rs, axis=0, keepdims=True)


_GC = math.sqrt(2.0 / math.pi)


def _gelu(x):
    t = jnp.tanh(_GC * (x + 0.044715 * x * x * x))
    return 0.5 * x * (1.0 + t), t


def _gelu_grad(x, t):
    return 0.5 * (1.0 + t) + 0.5 * x * (1.0 - t * t) * (_GC * (1.0 + 3.0 * 0.044715 * x * x))


def _params(vmem_mb, sem=("arbitrary",)):
    return pltpu.CompilerParams(dimension_semantics=sem, vmem_limit_bytes=vmem_mb * 1024 * 1024)


def _row_call(name, body, T, tm, tiled, resident, outs, accs, scratch=(), reverse=False, vmem_mb=48):
    nt = T // tm
    n_t, n_r, n_o, n_a = len(tiled), len(resident), len(outs), len(accs)

    def kern(*refs):
        t_refs = refs[:n_t]
        r_hbm = refs[n_t:n_t + n_r]
        o_refs = refs[n_t + n_r:n_t + n_r + n_o]
        a_refs = refs[n_t + n_r + n_o:n_t + n_r + n_o + n_a]
        r_vmem = refs[n_t + n_r + n_o + n_a:n_t + 2 * n_r + n_o + n_a]
        s_refs = refs[n_t + 2 * n_r + n_o + n_a:]

        @pl.when(pl.program_id(0) == 0)
        def _():
            for h, v in zip(r_hbm, r_vmem):
                pltpu.sync_copy(h, v)
            for a in a_refs + s_refs:
                a[...] = jnp.zeros(a.shape, a.dtype)

        body(t_refs, r_vmem, o_refs, a_refs, s_refs)

    if reverse:
        idx = lambda i: (nt - 1 - i, 0)
        idx_t = lambda i: (nt - 1 - i, 0, 0)
    else:
        idx = lambda i: (i, 0)
        idx_t = lambda i: (i, 0, 0)
    arrays, in_specs = [], []
    for a in tiled:
        if isinstance(a, tuple):
            arrays.append(a[0])
            in_specs.append(pl.BlockSpec((None, a[0].shape[1], tm), idx_t))
        else:
            arrays.append(a)
            in_specs.append(pl.BlockSpec((tm, a.shape[1]), idx))
    in_specs += [pl.BlockSpec(memory_space=pl.ANY) for _ in resident]
    out_shape, out_specs = [], []
    for o in outs:
        if len(o) == 3:
            out_shape.append(jax.ShapeDtypeStruct((nt, o[0], tm), o[1]))
            out_specs.append(pl.BlockSpec((None, o[0], tm), idx_t))
        else:
            out_shape.append(jax.ShapeDtypeStruct((T, o[0]), o[1]))
            out_specs.append(pl.BlockSpec((tm, o[0]), idx))
    out_shape += [jax.ShapeDtypeStruct(s, F32) for s in accs]
    out_specs += [pl.BlockSpec(s, lambda i, n=len(s): (0,) * n) for s in accs]
    scratch_shapes = [pltpu.VMEM(r.shape, r.dtype) for r in resident]
    scratch_shapes += [pltpu.VMEM(s, F32) for s in scratch]
    return pl.pallas_call(
        kern, name=name, grid=(nt,), in_specs=in_specs, out_specs=out_specs, out_shape=out_shape,
        scratch_shapes=scratch_shapes, compiler_params=_params(vmem_mb),
    )(*arrays, *resident)


def _sg_forward(zu, zv, wm_ref, bsg, lng, lnb, mixed_ref, tm):
    gu, tu = _gelu(zu)
    vg, tv = _gelu(zv)
    mu = jnp.sum(vg, axis=-1, keepdims=True) * (1.0 / AW)
    xc = vg - mu
    rstd = lax.rsqrt(jnp.sum(xc * xc, axis=-1, keepdims=True) * (1.0 / AW) + EPS)
    xhat = xc * rstd
    vvb = (xhat * lng + lnb).astype(BF16)
    lane = lax.broadcasted_iota(jnp.int32, (CH, 128), 1)
    for c in range(tm // CH):
        for j in range(4):
            blk = vvb[c * CH:(c + 1) * CH, j * 128:(j + 1) * 128]
            m0 = _dot(wm_ref[2 * j], blk)
            m1 = _dot(wm_ref[2 * j + 1], blk)
            mixed_ref[c * CH:(c + 1) * CH, j * 128:(j + 1) * 128] = (
                jnp.where(lane < DH, m0, m1) + bsg[:, j * 128:(j + 1) * 128])
    return gu, tu, tv, xhat, rstd, vvb, mixed_ref[...]


def _head_consts():
    src = jnp.arange(AW)
    dst = (src // DH) * 128 + src % DH
    wide = jnp.arange(NH * 128)
    expand = (dst[:, None] == wide[None, :]).astype(BF16)
    heads = jnp.arange(128)
    pieces = jnp.stack([((heads[:, None] * 128 + DH + i == wide[None, :]) & (heads[:, None] < NH)).astype(BF16)
                        for i in range(3)])
    spare = wide % 128 - DH
    qconst = jnp.where((spare >= 0) & (spare < 3), -1.0, 0.0).astype(F32)[None, :]
    one64 = jnp.where(spare == 0, 1.0, 0.0).astype(F32)[None, :]
    one67 = jnp.where(spare == 3, 1.0, 0.0).astype(F32)[None, :]
    pick64 = ((wide[:, None] == heads[None, :] * 128 + DH) & (heads[None, :] < NH)).astype(BF16)
    pick67 = ((wide[:, None] == heads[None, :] * 128 + DH + 3) & (heads[None, :] < NH)).astype(BF16)
    return expand, expand.T, pieces, qconst, one64, one67, pick64, pick67


def _masked_sg_w(sg_w):
    r = lax.broadcasted_iota(jnp.int32, (CH, CH), 0)
    c = lax.broadcasted_iota(jnp.int32, (CH, CH), 1)
    return jnp.where((c <= r)[None], sg_w, 0.0)


def _pre_attn_fwd(x, gpre, win, fbias, lng, lnb, wm, bsg, gsg, expand, pieces, qconst, kconst, vconst):
    T = x.shape[0]
    tm = TM

    def body(t, r, o, a, s):
        (x_ref,) = t
        gpre_r, win_r, fb_r, lng_r, lnb_r, wm_r, bsg_r, gsg_r, ex_r, pc_r, qc_r, kc_r, vc_r = r
        a_o, flog_o, zuv_o, ysgn_o, q8_o, k8_o, v8_o = o
        carry_ref, mixed_ref = s
        xv = x_ref[...]
        av = (xv * _rs(xv, D) * gpre_r[...]).astype(BF16)
        a_o[...] = av
        z = _dot(av, win_r[...])
        zu = z[:, 3 * AW:4 * AW]
        zv = z[:, 4 * AW:5 * AW]
        zuv_o[:, 0:AW] = zu
        zuv_o[:, AW:2 * AW] = zv
        zf = z[:, 5 * AW:] + fb_r[...]
        flog_o[...] = zf
        lane = lax.broadcasted_iota(jnp.int32, (tm, 128), 1)
        logf = jnp.where(lane < NH, jnp.minimum(zf, 0.0) - jnp.log(1.0 + jnp.exp(-jnp.abs(zf))), 0.0)
        rr = lax.broadcasted_iota(jnp.int32, (tm, tm), 0)
        cc = lax.broadcasted_iota(jnp.int32, (tm, tm), 1)
        tri = (cc <= rr).astype(BF16)
        cum = _dot01(tri, logf) + carry_ref[...]
        carry_ref[...] = cum[tm - 1:tm, :]
        ex = ex_r[...]
        q8_o[...] = (_dot((z[:, 0:AW] * (DH ** -0.5)).astype(BF16), ex) + qc_r[...]).astype(BF16)
        ch, cm, cl = _split3(cum)
        k8_o[...] = (_dot(z[:, AW:2 * AW].astype(BF16), ex) + _dot(ch, pc_r[0]) + _dot(cm, pc_r[1])
                     + _dot(cl, pc_r[2]) + kc_r[...]).astype(BF16)
        v8_o[...] = (_dot(z[:, 2 * AW:3 * AW].astype(BF16), ex) + vc_r[...]).astype(BF16)
        gu, _, _, _, _, _, mixed = _sg_forward(zu, zv, wm_r, bsg_r[...], lng_r[...], lnb_r[...], mixed_ref, tm)
        ysg = gu * mixed
        ysgn_o[...] = (ysg * _rs(ysg, AW) * gsg_r[...]).astype(BF16)

    return _row_call(
        "pre_attn_fwd", body, T, tm, [x],
        [gpre, win, fbias, lng, lnb, wm, bsg, gsg, expand, pieces, qconst, kconst, vconst],
        [(D, BF16), (128, F32), (2 * AW, F32), (AW, BF16), (NH * 128, BF16), (NH * 128, BF16), (NH * 128, BF16)], [],
        scratch=[(1, 128), (tm, AW)], vmem_mb=48)


def _flash_fwd(qt8, k8, vt8, sel):
    T = k8.shape[0]
    nq = T // TQ

    def body(qt_ref, k_ref, vt_ref, sel_ref, o_ref, l_ref, u_scr, p_scr):
        qi = pl.program_id(1)
        qts = (qt_ref[0:128, :], qt_ref[128:256, :])
        dmat = (lax.broadcasted_iota(jnp.int32, (TQ, TQ), 0) - lax.broadcasted_iota(jnp.int32, (TQ, TQ), 1))
        u_scr[1] = jnp.full((2, TQ, TQ), MASKED, F32)
        p_scr[...] = jnp.zeros(p_scr.shape, BF16)

        def sub(t, carry, sc, sb, masked):
            blk_c = jnp.clip(t - 2, 0, qi)
            off_a = pl.multiple_of(jnp.minimum(t, qi) * TQ, TQ)
            new = []
            for j in (0, 1):
                m, al, acc = carry[j]
                acc = al * acc + _dot(vt_ref[blk_c, j * 128:(j + 1) * 128, :], p_scr[sc, j])
                m_new = jnp.maximum(m, jnp.max(u_scr[sb, j], axis=0, keepdims=True))
                p_scr[sb, j] = jnp.exp(u_scr[sb, j] - m_new).astype(BF16)
                u = _dot(k_ref[pl.ds(off_a, TQ), j * 128:(j + 1) * 128], qts[j])
                u_scr[sc, j] = jnp.where(dmat <= (qi - t) * TQ, u, MASKED) if masked else u
                new.append((m_new, jnp.exp(m - m_new), acc))
            return tuple(new)

        def pair(t2, carry, masked):
            return sub(2 * t2 + 1, sub(2 * t2, carry, 0, 1, masked), 1, 0, masked)

        init = tuple((jnp.full((1, TQ), NEG, F32), jnp.ones((1, TQ), F32), jnp.zeros((128, TQ), F32))
                     for _ in (0, 1))
        carry = lax.fori_loop(0, qi // 2, lambda t2, cr: pair(t2, cr, False), init)
        (m0, _, a0), (m1, _, a1) = pair(qi // 2 + 1, pair(qi // 2, carry, True), True)
        l0 = a0[DH:DH + 1, :]
        l1 = a1[DH:DH + 1, :]
        o_ref[...] = _dot01_tn(a0 * (1.0 / l0), sel_ref[0]) + _dot01_tn(a1 * (1.0 / l1), sel_ref[1])
        l_ref[0:1, :] = m0 + jnp.log(l0)
        l_ref[1:2, :] = m1 + jnp.log(l1)
        l_ref[2:8, :] = jnp.zeros((6, TQ), F32)

    return pl.pallas_call(
        body, name="flash_fwd", grid=(4, nq),
        in_specs=[pl.BlockSpec((None, 256, TQ), lambda h, i: (i, h, 0)),
                  pl.BlockSpec((T, 256), lambda h, i: (0, h)),
                  pl.BlockSpec((nq, 256, TQ), lambda h, i: (0, h, 0)),
                  pl.BlockSpec((2, 128, 128), lambda h, i: (0, 0, 0))],
        out_specs=[pl.BlockSpec((TQ, 128), lambda h, i: (i, h)),
                   pl.BlockSpec((None, 8, TQ), lambda h, i: (h, 0, i))],
        out_shape=[jax.ShapeDtypeStruct((T, AW), F32), jax.ShapeDtypeStruct((4, 8, T), F32)],
        scratch_shapes=[pltpu.VMEM((2, 2, TQ, TQ), F32), pltpu.VMEM((2, 2, TQ, TQ), BF16)],
        compiler_params=_params(40, ("arbitrary", "arbitrary")),
    )(qt8, k8, vt8, sel)


def _flash_bwd(q8, qt8, k8, v8, do8, dot8, lse, dlt):
    T = q8.shape[0]
    nk = T // TQ

    def body(q_ref, qt_ref, k_ref, v_ref, do_ref, dot_ref, l_ref, d_ref, dqt_ref, dk_ref, dv_ref,
             u_scr, dp_scr, p_scr, ds_scr):
        kb = pl.program_id(1)
        n = nk - kb

        @pl.when(kb == 0)
        def _():
            dqt_ref[...] = jnp.zeros(dqt_ref.shape, F32)

        dk_ref[...] = jnp.zeros(dk_ref.shape, F32)
        dv_ref[...] = jnp.zeros(dv_ref.shape, F32)
        u_scr[1] = jnp.full((2, TQ, TQ), MASKED, F32)
        dp_scr[1] = jnp.zeros((2, TQ, TQ), F32)
        p_scr[...] = jnp.zeros(p_scr.shape, BF16)
        ds_scr[...] = jnp.zeros(ds_scr.shape, BF16)
        dmat = (lax.broadcasted_iota(jnp.int32, (TQ, TQ), 0) - lax.broadcasted_iota(jnp.int32, (TQ, TQ), 1))
        ks = (k_ref[:, 0:128], k_ref[:, 128:256])
        vs = (v_ref[:, 0:128], v_ref[:, 128:256])

        def sub(t, sc, sb):
            blk_a = kb + jnp.minimum(t, n - 1)
            blk_c = kb + jnp.clip(t - 2, 0, n - 1)
            off_b = pl.multiple_of((kb + jnp.clip(t - 1, 0, n - 1)) * TQ, TQ)
            off_c = pl.multiple_of(blk_c * TQ, TQ)
            lim = jnp.where(t < n, t * TQ, -TQ)
            for j in (0, 1):
                hl = slice(j * 128, (j + 1) * 128)
                dqt_ref[blk_c, hl, :] += _dot_tn(ks[j], ds_scr[sc, j])
                dk_ref[:, hl] += _dot(ds_scr[sc, j], q_ref[pl.ds(off_c, TQ), hl])
                dv_ref[:, hl] += _dot(p_scr[sc, j], do_ref[pl.ds(off_c, TQ), hl])
                p = jnp.exp(u_scr[sb, j] - l_ref[j:j + 1, pl.ds(off_b, TQ)])
                p_scr[sb, j] = p.astype(BF16)
                ds_scr[sb, j] = (p * (dp_scr[sb, j] - d_ref[j:j + 1, pl.ds(off_b, TQ)])).astype(BF16)
                u_scr[sc, j] = jnp.where(dmat <= lim, _dot(ks[j], qt_ref[blk_a, hl, :]), MASKED)
                dp_scr[sc, j] = _dot(vs[j], dot_ref[blk_a, hl, :])

        def it(t2, carry):
            sub(2 * t2, 0, 1)
            sub(2 * t2 + 1, 1, 0)
            return carry

        lax.fori_loop(0, n // 2 + 1, it, 0)

        @pl.when(n % 2 == 1)
        def _():
            sub(n + 1, 0, 1)

    return pl.pallas_call(
        body, name="flash_bwd", grid=(4, nk),
        in_specs=[pl.BlockSpec((T, 256), lambda h, i: (0, h)),
                  pl.BlockSpec((nk, 256, TQ), lambda h, i: (0, h, 0)),
                  pl.BlockSpec((TQ, 256), lambda h, i: (i, h)),
                  pl.BlockSpec((TQ, 256), lambda h, i: (i, h)),
                  pl.BlockSpec((T, 256), lambda h, i: (0, h)),
                  pl.BlockSpec((nk, 256, TQ), lambda h, i: (0, h, 0)),
                  pl.BlockSpec((None, 8, T), lambda h, i: (h, 0, 0)),
                  pl.BlockSpec((None, 8, T), lambda h, i: (h, 0, 0))],
        out_specs=[pl.BlockSpec((nk, 256, TQ), lambda h, i: (0, h, 0)),
                   pl.BlockSpec((TQ, 256), lambda h, i: (i, h)),
                   pl.BlockSpec((TQ, 256), lambda h, i: (i, h))],
        out_shape=[jax.ShapeDtypeStruct((nk, NH * 128, TQ), F32), jax.ShapeDtypeStruct((T, NH * 128), F32),
                   jax.ShapeDtypeStruct((T, NH * 128), F32)],
        scratch_shapes=[pltpu.VMEM((2, 2, TQ, TQ), F32), pltpu.VMEM((2, 2, TQ, TQ), F32),
                        pltpu.VMEM((2, 2, TQ, TQ), BF16), pltpu.VMEM((2, 2, TQ, TQ), BF16)],
        compiler_params=_params(56, ("arbitrary", "arbitrary")),
    )(q8, qt8, k8, v8, do8, dot8, lse, dlt)


def _tail_fwd1(x, yatt, ysgn, gatt, wout, gpm, gpf, w1):
    T = x.shape[0]

    def body(t, r, o, a, s):
        x_ref, ya_ref, ys_ref = t
        gatt_r, wout_r, gpm_r, gpf_r, w1_r = r
        y_o, o_o, h1_o, c2_o, s_o, rr_o = o
        ya = ya_ref[...]
        yan = (ya * _rs(ya, AW) * gatt_r[...]).astype(BF16)
        y_o[:, 0:AW] = yan
        y_o[:, AW:] = ys_ref[...]
        ov = _dot(yan, wout_r[0:AW, :]) + _dot(ys_ref[...], wout_r[AW:, :])
        o_o[...] = ov
        h1 = x_ref[...] + ov * _rs(ov, D) * gpm_r[...]
        h1_o[...] = h1
        c2 = (h1 * _rs(h1, D) * gpf_r[...]).astype(BF16)
        c2_o[...] = c2
        for k in range(4):
            rr = jnp.maximum(_dot(c2, w1_r[k]), 0.0)
            rr_o[:, k * D:(k + 1) * D] = rr.astype(BF16)
            s_o[:, k * D:(k + 1) * D] = (rr * rr).astype(BF16)

    return _row_call(
        "tail_fwd1", body, T, TM, [x, yatt, ysgn], [gatt, wout, gpm, gpf, w1],
        [(D, BF16), (D, F32), (D, F32), (D, BF16), (DFF, BF16), (DFF, BF16)], [], vmem_mb=48)


def _tail_fwd2(sact, h1, p, tgt, w2, gpff, wg, bg, wpe):
    T = h1.shape[0]

    def body(t, r, o, a, s):
        s_ref, h1_ref, p_ref, t_ref = t
        w2_r, gpff_r, wg_r, bg_r, wpe_r = r
        ff_o, h2b_o, de_o, dpre_o, dh2_o = o
        loss_a, dbg_a = a
        ff = _dot(s_ref[...], w2_r[...])
        ff_o[...] = ff
        h2 = h1_ref[...] + ff * _rs(ff, D) * gpff_r[...]
        h2b = h2.astype(BF16)
        h2b_o[...] = h2b
        gate = 1.0 / (1.0 + jnp.exp(-(_dot(h2b, wg_r[...]) + bg_r[...])))
        pb = p_ref[...].astype(BF16)
        e = jnp.concatenate([_dot(pb, wpe_r[k]) for k in range(4)], axis=1)
        diff = h2 + gate * e - t_ref[...]
        loss_a[...] += jnp.sum(diff * diff, axis=0, keepdims=True)
        dh3 = diff * (1.0 / D)
        de_o[...] = (dh3 * gate).astype(BF16)
        dpre = dh3 * e * gate * (1.0 - gate)
        dbg_a[...] += jnp.sum(dpre, axis=0, keepdims=True)
        dpb = dpre.astype(BF16)
        dpre_o[...] = dpb
        dh2_o[...] = dh3 + _dot_nt(dpb, wg_r[...])

    return _row_call(
        "tail_fwd2", body, T, TM, [sact, h1, p, tgt], [w2, gpff, wg, bg, wpe],
        [(D, F32), (D, BF16), (D, BF16), (D, BF16), (D, F32)], [(1, D), (1, D)], vmem_mb=48)


def _tail_bwd(dh2, ff, rr, h1, ov, yatt, w2, w1, wout, gpff, gpf, gpm, gatt, gsel, expand):
    T = dh2.shape[0]

    def body(t, r, o, a, s):
        dh2_ref, ff_ref, rr_ref, h1_ref, o_ref, ya_ref = t
        w2_r, w1_r, wout_r, gpff_r, gpf_r, gpm_r, gatt_r, gsel_r, ex_r = r
        dff_o, dr_o, do_o, do8_o, dlt_o, dysg_o, dh1_o = o
        dgpff_a, dgpf_a, dgpm_a, dgatt_a = a
        dh2v = dh2_ref[...]
        ffv = ff_ref[...]
        dff, dg = _rms_bwd(dh2v, ffv, _rs(ffv, D), gpff_r[...], D)
        dgpff_a[...] += dg
        dffb = dff.astype(BF16)
        dff_o[...] = dffb
        drb = (_dot_nt(dffb, w2_r[...]) * (2.0 * rr_ref[...].astype(F32))).astype(BF16)
        dr_o[...] = drb
        dc2 = _dot_nt(drb[:, 0:D], w1_r[0])
        for k in range(1, 4):
            dc2 = dc2 + _dot_nt(drb[:, k * D:(k + 1) * D], w1_r[k])
        h1v = h1_ref[...]
        d1, dg = _rms_bwd(dc2, h1v, _rs(h1v, D), gpf_r[...], D)
        dgpf_a[...] += dg
        dh1 = dh2v + d1
        dh1_o[...] = dh1
        ovv = o_ref[...]
        dov, dg = _rms_bwd(dh1, ovv, _rs(ovv, D), gpm_r[...], D)
        dgpm_a[...] += dg
        dob = dov.astype(BF16)
        do_o[...] = dob
        dysg_o[...] = _dot_nt(dob, wout_r[AW:, :])
        dyan = _dot_nt(dob, wout_r[0:AW, :])
        ya = ya_ref[...]
        dya, dg = _rms_bwd(dyan, ya, _rs(ya, AW), gatt_r[...], AW)
        dgatt_a[...] += dg
        do8_o[...] = _dot(dya.astype(BF16), ex_r[...]).astype(BF16)
        dlt_o[...] = _dot01_r(dya * ya, gsel_r[...])

    return _row_call(
        "tail_bwd", body, T, TM, [dh2, ff, rr, h1, ov, yatt],
        [w2, w1, wout, gpff, gpf, gpm, gatt, gsel, expand],
        [(D, BF16), (DFF, BF16), (D, BF16), (NH * 128, BF16), (128, F32), (AW, F32), (D, F32)],
        [(1, D), (1, D), (1, D), (1, AW)], vmem_mb=56)


def _pre_attn_bwd(x, dh1, dq8, dk8, dv8, flog, zuv, dysg, gpre, win, lng, lnb, wm, wmt, bsg, gsg, gsel, shrink, pick64, pick67):
    T = x.shape[0]
    tm = TM

    def body(t, r, o, a, s):
        x_ref, dh1_ref, dq_ref, dk_ref, dv_ref, fl_ref, zuv_ref, dys_ref = t
        gpre_r, win_r, lng_r, lnb_r, wm_r, wmt_r, bsg_r, gsg_r, gsel_r, sh_r, p64_r, p67_r = r
        dx_o, dz_o = o
        dgpre_a, dfb_a, dgsg_a, dlng_a, dlnb_a, dws_a, dbs_a, dsb_a = a
        carry_ref, mixed_ref, dvv_ref = s
        dq8v = dq_ref[...]
        dk8v = dk_ref[...]
        dcv = _dot01_r(dq8v, p67_r[...]) + _dot01_r(dk8v, p64_r[...])
        rr = lax.broadcasted_iota(jnp.int32, (tm, tm), 0)
        cc = lax.broadcasted_iota(jnp.int32, (tm, tm), 1)
        triu = (cc >= rr).astype(BF16)
        dlogf = _dot01(triu, dcv) + carry_ref[...]
        carry_ref[...] = dlogf[0:1, :]
        dzf = dlogf * (1.0 / (1.0 + jnp.exp(fl_ref[...])))
        dfb_a[...] += jnp.sum(dzf, axis=0, keepdims=True)
        dz_o[:, 5 * AW:] = dzf.astype(BF16)
        zu = zuv_ref[:, 0:AW]
        zv = zuv_ref[:, AW:]
        gu, tu, tv, xhat, rstd, vvb, mixed = _sg_forward(
            zu, zv, wm_r, bsg_r[...], lng_r[...], lnb_r[...], mixed_ref, tm)
        ysg = gu * mixed
        dysg_n = dys_ref[...]
        dys, dg = _rms_bwd(dysg_n, ysg, _rs(ysg, AW), gsg_r[...], AW)
        dgsg_a[...] += dg
        dgu = dys * mixed
        dmix = dys * gu
        dmb = dmix.astype(BF16)
        lane = lax.broadcasted_iota(jnp.int32, (CH, 128), 1)
        lo = lane < DH
        for c in range(tm // CH):
            rows = slice(c * CH, (c + 1) * CH)
            dbs_a[...] += dmix[rows, :]
            for j in range(4):
                cols = slice(j * 128, (j + 1) * 128)
                dmblk = dmb[rows, cols]
                vblk = vvb[rows, cols]
                d0 = _dot(wmt_r[2 * j], dmblk)
                d1 = _dot(wmt_r[2 * j + 1], dmblk)
                dvv_ref[rows, cols] = jnp.where(lo, d0, d1)
                dws_a[2 * j] += _dot_nt(jnp.where(lo, dmblk, jnp.zeros_like(dmblk)), vblk)
                dws_a[2 * j + 1] += _dot_nt(jnp.where(lo, jnp.zeros_like(dmblk), dmblk), vblk)
        dvv = dvv_ref[...]
        dlng_a[...] += jnp.sum(dvv * xhat, axis=0, keepdims=True)
        dlnb_a[...] += jnp.sum(dvv, axis=0, keepdims=True)
        dxh = dvv * lng_r[...]
        dvg = rstd * (dxh - jnp.sum(dxh, axis=-1, keepdims=True) * (1.0 / AW)
                      - xhat * (jnp.sum(dxh * xhat, axis=-1, keepdims=True) * (1.0 / AW)))
        dz_o[:, 3 * AW:4 * AW] = (dgu * _gelu_grad(zu, tu)).astype(BF16)
        dz_o[:, 4 * AW:5 * AW] = (dvg * _gelu_grad(zv, tv)).astype(BF16)
        dz_o[:, 0:AW] = _dot((dq8v * (DH ** -0.5)).astype(BF16), sh_r[...]).astype(BF16)
        dz_o[:, AW:2 * AW] = _dot(dk8v.astype(BF16), sh_r[...]).astype(BF16)
        dz_o[:, 2 * AW:3 * AW] = _dot(dv_ref[...].astype(BF16), sh_r[...]).astype(BF16)
        da = _dot_nt(dz_o[...], win_r[...])
        xv = x_ref[...]
        dxa, dg = _rms_bwd(da, xv, _rs(xv, D), gpre_r[...], D)
        dgpre_a[...] += dg
        dx_o[...] = dh1_ref[...] + dxa

        @pl.when(pl.program_id(0) == T // tm - 1)
        def _():
            dsb_a[...] = _dot01_r(dbs_a[...], gsel_r[...])

    outs = _row_call(
        "pre_attn_bwd", body, T, tm, [x, dh1, dq8, dk8, dv8, flog, zuv, dysg],
        [gpre, win, lng, lnb, wm, wmt, bsg, gsg, gsel, shrink, pick64, pick67],
        [(D, F32), (ZW, BF16)],
        [(1, D), (1, 128), (1, AW), (1, AW), (1, AW), (8, CH, CH), (CH, AW), (CH, 128)],
        scratch=[(1, 128), (tm, AW), (tm, AW)], reverse=True, vmem_mb=48)
    return outs


def _matmul_tn(name, a, b, tn=512, tt=2048, shards=1):
    T, K = a.shape
    N = b.shape[1]
    tk = min(K, 1024)
    tn = min(tn, N // shards)
    tt = min(tt, T)
    nj = N // shards // tn

    def body(a_ref, b_ref, o_ref):
        @pl.when(pl.program_id(2) == 0)
        def _():
            o_ref[...] = jnp.zeros(o_ref.shape, F32)

        o_ref[...] += _dot_tn(a_ref[...].astype(BF16), b_ref[...].astype(BF16))

    if shards == 1:
        out_shape = jax.ShapeDtypeStruct((K, N), F32)
        out_spec = pl.BlockSpec((tk, tn), lambda i, j, t: (i, j))
    else:
        out_shape = jax.ShapeDtypeStruct((shards, K, N // shards), F32)
        out_spec = pl.BlockSpec((None, tk, tn), lambda i, j, t: (j // nj, i, j % nj))
    return pl.pallas_call(
        body, name=name, grid=(K // tk, N // tn, T // tt),
        in_specs=[pl.BlockSpec((tt, tk), lambda i, j, t: (t, i)),
                  pl.BlockSpec((tt, tn), lambda i, j, t: (t, j))],
        out_specs=out_spec, out_shape=out_shape,
        compiler_params=_params(40, ("arbitrary", "arbitrary", "arbitrary")),
    )(a, b)


def _me():
    return lax.axis_index("x"), lax.axis_index("y"), lax.axis_index("c")


HBM_SPEC = pl.BlockSpec(memory_space=pltpu.HBM)


def _gather_weights(mine):
    half = mine.shape[0] // 2

    def body(mine_ref, out_ref, ici_send, ici_recv, d2d_send, d2d_recv):
        x, y, c = _me()
        k_me = 2 * x + y
        chips = [(1 - x, y), (x, 1 - y), (1 - x, 1 - y)]
        my_rows = pl.ds(pl.multiple_of(c * half, 16), half)
        sib_rows = pl.ds(pl.multiple_of((1 - c) * half, 16), half)

        def over_ici(j, k, to):
            src = mine_ref.at[my_rows] if k is None else out_ref.at[k, my_rows]
            return pltpu.make_async_remote_copy(
                src_ref=src, dst_ref=out_ref.at[k_me if k is None else k, my_rows], send_sem=ici_send.at[j],
                recv_sem=ici_recv.at[j], device_id=to, device_id_type=MESH)

        def over_d2d(j, k, rows):
            return pltpu.make_async_remote_copy(
                src_ref=out_ref.at[k, rows], dst_ref=out_ref.at[k, rows], send_sem=d2d_send.at[j],
                recv_sem=d2d_recv.at[j], device_id=(x, y, 1 - c), device_id_type=MESH)

        first = [over_ici(j, None, (cx, cy, c)) for j, (cx, cy) in enumerate(chips)]
        for cp in first:
            cp.start()
        passed = [over_d2d(j, 2 * cx + cy, my_rows) for j, (cx, cy) in enumerate(chips)]
        for j, (cx, cy) in enumerate(chips):
            over_ici(j, 2 * cx + cy, (cx, cy, c)).wait_recv()
            passed[j].start()
        for j, (cx, cy) in enumerate(chips):
            over_d2d(j, 2 * cx + cy, sib_rows).wait_recv()
        for cp in first + passed:
            cp.wait_send()

    return pl.pallas_call(
        body, name="gather_weights", in_specs=[HBM_SPEC], out_specs=HBM_SPEC,
        out_shape=jax.ShapeDtypeStruct((4,) + mine.shape, mine.dtype),
        scratch_shapes=[pltpu.SemaphoreType.DMA((3,)), pltpu.SemaphoreType.DMA((3,)), pltpu.SemaphoreType.DMA((3,)),
                        pltpu.SemaphoreType.DMA((3,))],
    )(mine)


SEM_SPEC = pl.BlockSpec(memory_space=pltpu.SEMAPHORE)
EFFECT = pltpu.SideEffectType.DATAFLOW_SIDE_EFFECTING


def _gather_late_start(mine):
    def body(mine_ref, land_ref, send_sems, recv_sems, mine_thru, land_thru, token):
        x, y, c = _me()
        k_me = 2 * x + y
        for j, (cx, cy) in enumerate([(1 - x, y), (x, 1 - y), (1 - x, 1 - y)]):
            pltpu.make_async_remote_copy(
                src_ref=mine_ref, dst_ref=land_ref.at[k_me], send_sem=send_sems.at[j], recv_sem=recv_sems.at[j],
                device_id=(cx, cy, c), device_id_type=MESH).start()
        token[...] = jnp.zeros(token.shape, F32)

    land = lax.empty((4,) + mine.shape, mine.dtype)
    return pl.pallas_call(
        body, name="gather_late_start",
        out_shape=(pltpu.SemaphoreType.DMA((3,)), pltpu.SemaphoreType.DMA((3,)), pltpu.HBM(mine.shape, mine.dtype),
                   pltpu.HBM(land.shape, land.dtype), jax.ShapeDtypeStruct((8, 128), F32)),
        in_specs=(HBM_SPEC, HBM_SPEC),
        out_specs=(SEM_SPEC, SEM_SPEC, HBM_SPEC, HBM_SPEC, pl.BlockSpec(memory_space=pltpu.VMEM)),
        input_output_aliases={0: 2, 1: 3},
        compiler_params=pltpu.CompilerParams(has_side_effects=EFFECT),
    )(pltpu.with_memory_space_constraint(mine, pltpu.HBM), pltpu.with_memory_space_constraint(land, pltpu.HBM))


def _gather_late_wait(send_sems, recv_sems, mine_thru, land_thru, after):
    def body(mine_ref, land_ref, send_sems, recv_sems, after_ref, mine_dead, got_ref):
        x, y, c = _me()
        for j, (cx, cy) in enumerate([(1 - x, y), (x, 1 - y), (1 - x, 1 - y)]):
            cp = pltpu.make_async_remote_copy(
                src_ref=mine_ref, dst_ref=land_ref.at[2 * cx + cy], send_sem=send_sems.at[j],
                recv_sem=recv_sems.at[j], device_id=(cx, cy, c), device_id_type=MESH)
            cp.wait_send()
            cp.wait_recv()

    return pl.pallas_call(
        body, name="gather_late_wait",
        out_shape=(pltpu.HBM(mine_thru.shape, mine_thru.dtype), pltpu.HBM(land_thru.shape, land_thru.dtype)),
        in_specs=(HBM_SPEC, HBM_SPEC, SEM_SPEC, SEM_SPEC, pl.BlockSpec(memory_space=pl.ANY)),
        out_specs=(HBM_SPEC, HBM_SPEC), input_output_aliases={0: 0, 1: 1},
        compiler_params=pltpu.CompilerParams(has_side_effects=EFFECT),
    )(mine_thru, land_thru, send_sems, recv_sems, after)[1]


def _swap_halves(gs, tag):
    n = len(gs)

    def body(*refs):
        g_refs, got_refs, send_sems, recv_sems = refs[:n], refs[n:2 * n], refs[2 * n], refs[2 * n + 1]
        x, y, c = _me()
        cps = []
        for i, (g_ref, got_ref) in enumerate(zip(g_refs, got_refs)):
            half = g_ref.shape[1] // 2
            theirs = pl.multiple_of((1 - c) * half, 16)
            cps.append(pltpu.make_async_remote_copy(
                src_ref=g_ref.at[:, pl.ds(theirs, half), :], dst_ref=got_ref, send_sem=send_sems.at[i],
                recv_sem=recv_sems.at[i], device_id=(x, y, 1 - c), device_id_type=MESH))
        for cp in cps:
            cp.start()
        for cp in cps:
            cp.wait()

    return pl.pallas_call(
        body, name="swap_halves_" + tag, in_specs=[HBM_SPEC] * n, out_specs=[HBM_SPEC] * n,
        out_shape=[jax.ShapeDtypeStruct((4, g.shape[1] // 2, g.shape[2]), F32) for g in gs],
        scratch_shapes=[pltpu.SemaphoreType.DMA((n,)), pltpu.SemaphoreType.DMA((n,))],
    )(*gs)


def _pair_sum(name, c1, g, got):
    half, cols = got.shape[1], got.shape[2]

    def body(c_ref, a_ref, b_ref, o_ref):
        o_ref[...] = (a_ref[...] + b_ref[...]).astype(BF16)

    return pl.pallas_call(
        body, name="pair_sum_" + name,
        grid_spec=pltpu.PrefetchScalarGridSpec(
            num_scalar_prefetch=1, grid=(4,),
            in_specs=[pl.BlockSpec((1, half, cols), lambda k, c_ref: (k, c_ref[0], 0)),
                      pl.BlockSpec((1, half, cols), lambda k, c_ref: (k, 0, 0))],
            out_specs=pl.BlockSpec((1, half, cols), lambda k, c_ref: (k, 0, 0))),
        out_shape=jax.ShapeDtypeStruct(got.shape, BF16), compiler_params=_params(32),
    )(c1, g, got)


def _exchange_start(pss, tag):
    n = len(pss)

    def body(*refs):
        ps_refs, land_refs = refs[:n], refs[n:2 * n]
        send_sems, recv_sems = refs[2 * n], refs[2 * n + 1]
        token = refs[4 * n + 2]
        x, y, c = _me()
        k_me = 2 * x + y
        for i, (ps_ref, land_ref) in enumerate(zip(ps_refs, land_refs)):
            for j, (cx, cy) in enumerate([(1 - x, y), (x, 1 - y), (1 - x, 1 - y)]):
                pltpu.make_async_remote_copy(
                    src_ref=ps_ref.at[2 * cx + cy], dst_ref=land_ref.at[k_me], send_sem=send_sems.at[3 * i + j],
                    recv_sem=recv_sems.at[3 * i + j], device_id=(cx, cy, c), device_id_type=MESH).start()
        token[...] = jnp.zeros(token.shape, F32)

    lands = [lax.empty(ps.shape, ps.dtype) for ps in pss]
    hbm = lambda t: pltpu.HBM(t.shape, t.dtype)
    res = pl.pallas_call(
        body, name="exchange_start_" + tag,
        out_shape=(pltpu.SemaphoreType.DMA((3 * n,)), pltpu.SemaphoreType.DMA((3 * n,)), *[hbm(t) for t in pss],
                   *[hbm(t) for t in lands], jax.ShapeDtypeStruct((8, 128), F32)),
        in_specs=(HBM_SPEC,) * (2 * n),
        out_specs=(SEM_SPEC, SEM_SPEC) + (HBM_SPEC,) * (2 * n) + (pl.BlockSpec(memory_space=pltpu.VMEM),),
        input_output_aliases={i: 2 + i for i in range(2 * n)},
        compiler_params=pltpu.CompilerParams(has_side_effects=EFFECT),
    )(*[pltpu.with_memory_space_constraint(t, pltpu.HBM) for t in list(pss) + lands])
    return res[0], res[1], res[2:2 + n], res[2 + n:2 + 2 * n], res[2 + 2 * n]


def _exchange_wait(send_sems, recv_sems, ps_thru, land_thru, after, tag):
    n = len(ps_thru)

    def body(*refs):
        ps_refs, land_refs = refs[:n], refs[n:2 * n]
        send_sems, recv_sems = refs[2 * n], refs[2 * n + 1]
        x, y, c = _me()
        k_me = 2 * x + y
        for i, (ps_ref, land_ref) in enumerate(zip(ps_refs, land_refs)):
            for j, (cx, cy) in enumerate([(1 - x, y), (x, 1 - y), (1 - x, 1 - y)]):
                cp = pltpu.make_async_remote_copy(
                    src_ref=ps_ref.at[k_me], dst_ref=land_ref.at[2 * cx + cy], send_sem=send_sems.at[3 * i + j],
                    recv_sem=recv_sems.at[3 * i + j], device_id=(cx, cy, c), device_id_type=MESH)
                cp.wait_send()
                cp.wait_recv()

    hbm = lambda t: pltpu.HBM(t.shape, t.dtype)
    res = pl.pallas_call(
        body, name="exchange_wait_" + tag,
        out_shape=tuple(hbm(t) for t in list(ps_thru) + list(land_thru)),
        in_specs=(HBM_SPEC,) * (2 * n) + (SEM_SPEC, SEM_SPEC, pl.BlockSpec(memory_space=pl.ANY)),
        out_specs=(HBM_SPEC,) * (2 * n), input_output_aliases={i: i for i in range(2 * n)},
        compiler_params=pltpu.CompilerParams(has_side_effects=EFFECT),
    )(*ps_thru, *land_thru, send_sems, recv_sems, after)
    return res[:n], res[n:]


def _adamw(w, g, m, v):
    m = B1 * m + (1.0 - B1) * g
    v = B2 * v + (1.0 - B2) * (g * g)
    delta = -LR * ((m / BC1) / (jnp.sqrt(v / BC2) + AEPS) + WD * w)
    return delta, m, v


def _reduce_chips(name, parts):
    half, cols = parts.shape[1], parts.shape[2]

    def body(p_ref, o_ref):
        f = lambda k: p_ref[k].astype(F32)
        o_ref[...] = ((f(0) + f(1)) + f(2)) + f(3)

    return pl.pallas_call(
        body, name="reduce_chips_" + name, grid=(1,),
        in_specs=[pl.BlockSpec((4, half, cols), lambda i: (0, 0, 0))],
        out_specs=pl.BlockSpec((half, cols), lambda i: (0, 0)),
        out_shape=jax.ShapeDtypeStruct((half, cols), F32), compiler_params=_params(32),
    )(parts)


def _share_grad(ghs, tag):
    n = len(ghs)

    def body(*refs):
        g_refs, got_refs, send_sems, recv_sems = refs[:n], refs[n:2 * n], refs[2 * n], refs[2 * n + 1]
        x, y, c = _me()
        cps = [pltpu.make_async_remote_copy(
            src_ref=g_ref, dst_ref=got_ref, send_sem=send_sems.at[i], recv_sem=recv_sems.at[i],
            device_id=(x, y, 1 - c), device_id_type=MESH) for i, (g_ref, got_ref) in enumerate(zip(g_refs, got_refs))]
        for cp in cps:
            cp.start()
        for cp in cps:
            cp.wait()

    return pl.pallas_call(
        body, name="share_grad_" + tag, in_specs=[HBM_SPEC] * n, out_specs=[HBM_SPEC] * n,
        out_shape=[jax.ShapeDtypeStruct(g.shape, F32) for g in ghs],
        scratch_shapes=[pltpu.SemaphoreType.DMA((n,)), pltpu.SemaphoreType.DMA((n,))],
    )(*ghs)


def _update(name, c1, gh, got, w, m, v):
    half, cols = gh.shape

    def body(c_ref, gh_ref, got_ref, w_ref, m_ref, v_ref, g_o, d_o, m_o, v_o):
        g = jnp.where(pl.program_id(0) == c_ref[0], gh_ref[...], got_ref[...])
        delta, mn, vn = _adamw(w_ref[...], g, m_ref[...], v_ref[...])
        g_o[...] = g
        d_o[...] = delta
        m_o[...] = mn
        v_o[...] = vn

    same = pl.BlockSpec((half, cols), lambda h, c_ref: (0, 0))
    rows = pl.BlockSpec((half, cols), lambda h, c_ref: (h, 0))
    return pl.pallas_call(
        body, name="update_" + name,
        grid_spec=pltpu.PrefetchScalarGridSpec(
            num_scalar_prefetch=1, grid=(2,), in_specs=[same, same, rows, rows, rows],
            out_specs=[rows, rows, rows, rows]),
        out_shape=[jax.ShapeDtypeStruct(w.shape, F32)] * 4, compiler_params=_params(40),
    )(c1, gh, got, w, m, v)


SMALL_NAMES = ("sg_w",) + VEC_NAMES
VEC_ROWS = 24
VEC_ROW = {"f_bias": 0, "sg_ln_g": 1, "sg_ln_b": 2, "att_out_g": 3, "sg_out_g": 4, "pre_mix_g": 5,
           "post_mix_g": 6, "pre_ffn_g": 7, "sg_b": 8, "post_ffn_g": 16, "ple_gate_b": 17}
LOSS_VEC_ROW = 18


def _small_pack(g, loss_l):
    n = len(SMALL_NAMES)

    def body(*refs):
        g_r = dict(zip(SMALL_NAMES, refs[0:n]))
        loss_r, vec_o, w_o = refs[n:]
        vec_o[...] = jnp.zeros((VEC_ROWS, 1024), F32)
        for name in VEC_NAMES:
            val = g_r[name][...]
            vec_o[pl.ds(VEC_ROW[name], val.shape[0]), pl.ds(0, val.shape[1])] = val
        vec_o[pl.ds(LOSS_VEC_ROW, 1), :] = loss_r[...] * (0.5 / D)
        rr = lax.broadcasted_iota(jnp.int32, (CH, CH), 0)
        cc = lax.broadcasted_iota(jnp.int32, (CH, CH), 1)
        w_o[...] = jnp.where((cc <= rr)[None], g_r["sg_w"][...], 0.0)

    vm = pl.BlockSpec(memory_space=pltpu.VMEM)
    args = [g[k] for k in SMALL_NAMES] + [loss_l]
    return pl.pallas_call(
        body, name="small_pack", in_specs=[vm] * len(args), out_specs=[vm, vm],
        out_shape=[jax.ShapeDtypeStruct((VEC_ROWS, 1024), F32), jax.ShapeDtypeStruct((8, CH, CH), F32)],
    )(*args)


def _small_peers(x, y, c):
    rels = [(rx, ry, rc) for rx in (0, 1) for ry in (0, 1) for rc in (0, 1)][1:]
    return [((x + rx) % 2, (y + ry) % 2, (c + rc) % 2) for rx, ry, rc in rels]


def _small_start(vec, w8):
    def body(vec_ref, w_ref, lv_ref, lw_ref, send_sems, recv_sems, vec_thru, w_thru, lv_thru, lw_thru, token):
        x, y, c = _me()
        me = 4 * x + 2 * y + c
        for j, to in enumerate(_small_peers(x, y, c)):
            for i, (src, land) in enumerate(((vec_ref, lv_ref), (w_ref, lw_ref))):
                pltpu.make_async_remote_copy(
                    src_ref=src, dst_ref=land.at[me], send_sem=send_sems.at[2 * j + i],
                    recv_sem=recv_sems.at[2 * j + i], device_id=to, device_id_type=MESH).start()
        token[...] = jnp.zeros(token.shape, F32)

    ops = [vec, w8, lax.empty((8,) + vec.shape, F32), lax.empty((8,) + w8.shape, F32)]
    hbm = lambda t: pltpu.HBM(t.shape, t.dtype)
    res = pl.pallas_call(
        body, name="small_start",
        out_shape=(pltpu.SemaphoreType.DMA((14,)), pltpu.SemaphoreType.DMA((14,)), *[hbm(t) for t in ops],
                   jax.ShapeDtypeStruct((8, 128), F32)),
        in_specs=(HBM_SPEC,) * 4,
        out_specs=(SEM_SPEC, SEM_SPEC) + (HBM_SPEC,) * 4 + (pl.BlockSpec(memory_space=pltpu.VMEM),),
        input_output_aliases={i: 2 + i for i in range(4)},
        compiler_params=pltpu.CompilerParams(has_side_effects=EFFECT),
    )(*[pltpu.with_memory_space_constraint(t, pltpu.HBM) for t in ops])
    return res[0], res[1], res[2:6], res[6]


def _small_wait(send_sems, recv_sems, thru, after):
    def body(vec_ref, w_ref, lv_ref, lw_ref, send_sems, recv_sems, after_ref, vec_o, w_o, lv_o, lw_o):
        x, y, c = _me()
        for j, (px, py, pc) in enumerate(_small_peers(x, y, c)):
            for i, (src, land) in enumerate(((vec_ref, lv_ref), (w_ref, lw_ref))):
                cp = pltpu.make_async_remote_copy(
                    src_ref=src, dst_ref=land.at[4 * px + 2 * py + pc], send_sem=send_sems.at[2 * j + i],
                    recv_sem=recv_sems.at[2 * j + i], device_id=(px, py, pc), device_id_type=MESH)
                cp.wait_send()
                cp.wait_recv()

    hbm = lambda t: pltpu.HBM(t.shape, t.dtype)
    return pl.pallas_call(
        body, name="small_wait", out_shape=tuple(hbm(t) for t in thru),
        in_specs=(HBM_SPEC,) * 4 + (SEM_SPEC, SEM_SPEC, pl.BlockSpec(memory_space=pl.ANY)),
        out_specs=(HBM_SPEC,) * 4, input_output_aliases={i: i for i in range(4)},
        compiler_params=pltpu.CompilerParams(has_side_effects=EFFECT),
    )(*thru, send_sems, recv_sems, after)


def _small_update(all_v, all_w, w, m, v):
    n = len(SMALL_NAMES)

    def body(*refs):
        allv_r, allw_r = refs[0], refs[1]
        tot_v = allv_r[0]
        tot_w = allw_r[0]
        for d in range(1, 8):
            tot_v = tot_v + allv_r[d]
            tot_w = tot_w + allw_r[d]
        w_r = dict(zip(SMALL_NAMES, refs[2:2 + n]))
        m_r = dict(zip(SMALL_NAMES, refs[2 + n:2 + 2 * n]))
        v_r = dict(zip(SMALL_NAMES, refs[2 + 2 * n:2 + 3 * n]))
        loss_o = refs[2 + 3 * n]
        outs = refs[3 + 3 * n:]
        loss_o[...] = jnp.sum(tot_v[LOSS_VEC_ROW:LOSS_VEC_ROW + 1, :], axis=-1, keepdims=True) + jnp.zeros((1, 128), F32)
        for i, name in enumerate(SMALL_NAMES):
            if name == "sg_w":
                gt = tot_w
            else:
                rows, width = w_r[name].shape
                gt = tot_v[VEC_ROW[name]:VEC_ROW[name] + rows, 0:width]
            delta, mn, vn = _adamw(w_r[name][...], gt, m_r[name][...], v_r[name][...])
            outs[4 * i][...] = gt
            outs[4 * i + 1][...] = delta
            outs[4 * i + 2][...] = mn
            outs[4 * i + 3][...] = vn

    vm = pl.BlockSpec(memory_space=pltpu.VMEM)
    args = [all_v, all_w] + [d[k] for d in (w, m, v) for k in SMALL_NAMES]
    out_shape = [jax.ShapeDtypeStruct((1, 128), F32)]
    out_shape += [jax.ShapeDtypeStruct(w[k].shape, F32) for k in SMALL_NAMES for _ in range(4)]
    res = pl.pallas_call(
        body, name="small_update", in_specs=[vm] * len(args), out_specs=[vm] * len(out_shape), out_shape=out_shape,
        compiler_params=pltpu.CompilerParams(vmem_limit_bytes=32 * 1024 * 1024),
    )(*args)
    return res[0], {k: res[1 + 4 * i:5 + 4 * i] for i, k in enumerate(SMALL_NAMES)}


def _win_kernel_order(gathered):
    w_in = jnp.concatenate([gathered[k].reshape(D, 768)[:, :642] for k in range(4)], axis=1)
    return jnp.concatenate([w_in[:, :3 * AW], w_in[:, 3 * AW + NH:], w_in[:, 3 * AW:3 * AW + NH],
                            jnp.zeros((D, 128 - NH), w_in.dtype)], axis=1)


LATE_ROWS = 256 + 1024 + 1024 + 64 + 256


def _pack_late(w_out, w1, w2, plew, wg):
    return jnp.concatenate([w_out, w1, w2, plew.reshape(64, 1024), wg], axis=0)


def _unpack_late(gathered):
    return (gathered[:, 0:256].reshape(D, D), gathered[:, 256:1280], gathered[:, 1280:2304].reshape(DFF, D),
            gathered[:, 2304:2368].reshape(4, 256, 256), gathered[:, 2368:2624].reshape(D, D))


def _local_step(x, p, tgt, win_k, late_weights, token, on_tail_grads, on_small_grads, small):
    T = x.shape[0]
    row = lambda n: small[n].reshape(1, -1)
    fbias = jnp.pad(row("f_bias"), ((0, 0), (0, 128 - NH))) + token[0:1, :]
    wm = _masked_sg_w(small["sg_w"].reshape(8, CH, CH))
    wmb = wm.astype(BF16)
    wmt = jnp.swapaxes(wm, 1, 2).astype(BF16)
    bsg = jnp.repeat(small["sg_b"].reshape(8, CH).T, DH, axis=1)
    ln_g, ln_b, gsg, gatt = row("sg_ln_g"), row("sg_ln_b"), row("sg_out_g"), row("att_out_g")
    gpre, gpm, gpf, gpff, bg = row("pre_mix_g"), row("post_mix_g"), row("pre_ffn_g"), row("post_ffn_g"), row("ple_gate_b")
    gsel = (jnp.arange(AW)[:, None] // DH == jnp.arange(128)[None, :]).astype(BF16)

    expand, shrink, pieces, qconst, one64, one67, pick64, pick67 = _head_consts()
    a, flog, zuv, ysgn, q8, k8, v8 = _pre_attn_fwd(
        x, gpre, win_k, fbias, ln_g, ln_b, wmb, bsg, gsg, expand, pieces, qconst, one67, one64)

    slabs = lambda t: jnp.swapaxes(t.reshape(T // TQ, TQ, NH * 128), 1, 2)
    qt8 = slabs(q8)
    lanes = jnp.arange(128)
    sel = jnp.stack([((lanes[:, None] == lanes[None, :] - DH * j) & (lanes[:, None] < DH)).astype(BF16)
                     for j in (0, 1)])

    yatt, lse = _flash_fwd(qt8, k8, slabs(v8), sel)
    wout, w1, w2, plew, wg = late_weights(lse)
    y, ov, h1, c2, sact, rr = _tail_fwd1(x, yatt, ysgn, gatt, wout, gpm, gpf, w1)
    ff, h2b, de, dpre, dh2, loss_l, dbg = _tail_fwd2(sact, h1, p, tgt, w2, gpff, wg, bg, plew)
    dff, dr, do, do8, dlt, dysg, dh1, dgpff, dgpf, dgpm, dgatt = _tail_bwd(
        dh2, ff, rr, h1, ov, yatt, w2, w1, wout, gpff, gpf, gpm, gatt, gsel, expand)
    dwout = _matmul_tn("grad_w_out", y, do)
    dw1 = _matmul_tn("grad_w_ff1", c2, dr, shards=4)
    dw2 = _matmul_tn("grad_w_ff2", sact, dff)
    dwg = _matmul_tn("grad_ple_gate_w", h2b, dpre)
    dplew = _matmul_tn("grad_ple_w", p, de, tn=256, shards=4)
    tail_token = on_tail_grads((dwout, dw1, dw2, dplew, dwg))
    dlt4 = jnp.pad(dlt[:, :NH].T.reshape(4, 2, T), ((0, 0), (0, 6), (0, 0))) + tail_token[0, 0]
    dqt, dk8, dv8 = _flash_bwd(q8, qt8, k8, v8, do8, slabs(do8), lse, dlt4)
    dx, dz, dgpre, dfb, dgsg, dlng, dlnb, dws, _, dsbt = _pre_attn_bwd(
        x, dh1, jnp.swapaxes(dqt, 1, 2).reshape(T, NH * 128), dk8, dv8, flog, zuv, dysg,
        gpre, win_k, ln_g, ln_b, wmb, wmt, bsg, gsg, gsel, shrink, pick64, pick67)


    dsb = dsbt[:, :8].T
    gsmall = {"sg_w": dws, "f_bias": dfb, "sg_ln_g": dlng, "sg_ln_b": dlnb, "sg_b": dsb,
              "att_out_g": dgatt, "sg_out_g": dgsg, "pre_mix_g": dgpre, "post_mix_g": dgpm, "pre_ffn_g": dgpf,
              "post_ffn_g": dgpff, "ple_gate_b": dbg}
    on_small_grads(gsmall, loss_l)
    dwin_k = _matmul_tn("grad_w_in", a, dz, tn=384)
    return loss_l, dx, dwin_k, gsmall


def kernel(x, p, w_in, f_bias, sg_ln_g, sg_ln_b, sg_w, sg_b, att_out_g, sg_out_g, w_out, pre_mix_g, post_mix_g, pre_ffn_g, post_ffn_g, w_ff1, w_ff2, ple_w, ple_gate_w, ple_gate_b, loss_target, m_w_in, m_f_bias, m_sg_ln_g, m_sg_ln_b, m_sg_w, m_sg_b, m_att_out_g, m_sg_out_g, m_w_out, m_pre_mix_g, m_post_mix_g, m_pre_ffn_g, m_post_ffn_g, m_w_ff1, m_w_ff2, m_ple_w, m_ple_gate_w, m_ple_gate_b, v_w_in, v_f_bias, v_sg_ln_g, v_sg_ln_b, v_sg_w, v_sg_b, v_att_out_g, v_sg_out_g, v_w_out, v_pre_mix_g, v_post_mix_g, v_pre_ffn_g, v_post_ffn_g, v_w_ff1, v_w_ff2, v_ple_w, v_ple_gate_w, v_ple_gate_b):
    c = lax.axis_index("c")
    big = lambda t: (t[0][0], t[1][0], t[2][0], t[3][0], t[4][0], t[5][0])
    w_big = big((w_in, w_out, w_ff1, w_ff2, ple_w, ple_gate_w))
    m_big = big((m_w_in, m_w_out, m_w_ff1, m_w_ff2, m_ple_w, m_ple_gate_w))
    v_big = big((v_w_in, v_w_out, v_w_ff1, v_w_ff2, v_ple_w, v_ple_gate_w))
    small = {"sg_w": sg_w, "f_bias": f_bias, "sg_ln_g": sg_ln_g, "sg_ln_b": sg_ln_b, "sg_b": sg_b,
             "att_out_g": att_out_g, "sg_out_g": sg_out_g, "pre_mix_g": pre_mix_g, "post_mix_g": post_mix_g,
             "pre_ffn_g": pre_ffn_g, "post_ffn_g": post_ffn_g, "ple_gate_b": ple_gate_b}
    m_small = {"sg_w": m_sg_w, "f_bias": m_f_bias, "sg_ln_g": m_sg_ln_g, "sg_ln_b": m_sg_ln_b, "sg_b": m_sg_b,
               "att_out_g": m_att_out_g, "sg_out_g": m_sg_out_g, "pre_mix_g": m_pre_mix_g,
               "post_mix_g": m_post_mix_g, "pre_ffn_g": m_pre_ffn_g, "post_ffn_g": m_post_ffn_g,
               "ple_gate_b": m_ple_gate_b}
    v_small = {"sg_w": v_sg_w, "f_bias": v_f_bias, "sg_ln_g": v_sg_ln_g, "sg_ln_b": v_sg_ln_b, "sg_b": v_sg_b,
               "att_out_g": v_att_out_g, "sg_out_g": v_sg_out_g, "pre_mix_g": v_pre_mix_g,
               "post_mix_g": v_post_mix_g, "pre_ffn_g": v_pre_ffn_g, "post_ffn_g": v_post_ffn_g,
               "ple_gate_b": v_ple_gate_b}

    k_me = 2 * lax.axis_index("x") + lax.axis_index("y")
    own_slot = lambda got, mine: lax.dynamic_update_slice(got, mine[None], (k_me, 0, 0))
    late_mine = _pack_late(*w_big[1:]).astype(BF16)
    late = _gather_late_start(late_mine)
    win_mine = jnp.pad(w_big[0], ((0, 0), (0, 768 - 642))).reshape(768, 1024).astype(BF16)
    win_k = _win_kernel_order(own_slot(_gather_weights(win_mine), win_mine))
    late_weights = lambda after: _unpack_late(
        own_slot(_gather_late_wait(late[0], late[1], late[2], late[3], after), late_mine))

    names = ("w_in", "w_out", "w_ff1", "w_ff2", "ple_w", "ple_gate_w")
    c1 = jnp.reshape(c, (1,)).astype(jnp.int32)
    own_part = lambda parts, pss: [lax.dynamic_update_slice(pt, lax.dynamic_slice_in_dim(ps, k_me, 1, 0), (k_me, 0, 0))
                                   for pt, ps in zip(parts, pss)]
    tail = {}

    def on_tail_grads(grads):
        dwout, dw1, dw2, dplew, dwg = grads
        gs = [dwout.reshape(4, 256, D), dw1, dw2.reshape(4, D, D), dplew, dwg.reshape(4, 256, D)]
        gots = _swap_halves(gs, "late")
        pss = [_pair_sum(nm, c1, g, got) for nm, g, got in zip(names[1:], gs, gots)]
        tail["xch"] = _exchange_start(pss, "late")
        return tail["xch"][4]

    def on_small_grads(gsmall, loss_l):
        tail["small"] = _small_start(*_small_pack(gsmall, loss_l))

    loss_l, dx, dwin_k, gsmall = _local_step(
        x[0], p[0, 0], loss_target[0], win_k, late_weights, late[4], on_tail_grads, on_small_grads, small)

    dwin = jnp.concatenate([dwin_k[:, :3 * AW], dwin_k[:, 5 * AW:5 * AW + NH], dwin_k[:, 3 * AW:5 * AW]], axis=1)
    dwin = jnp.pad(jnp.swapaxes(dwin.reshape(D, 4, 642), 0, 1), ((0, 0), (0, 0), (0, 768 - 642)))
    ps_in = [_pair_sum(names[0], c1, dwin, _swap_halves([dwin], "in")[0])]
    ins, inr, in_thru, inland_thru, in_token = _exchange_start(ps_in, "in")
    xs, xr, ps_thru, land_thru, _ = tail["xch"]
    ps_late, landed = _exchange_wait(xs, xr, ps_thru, land_thru, in_token, "late")
    padded = lambda t: (jnp.pad(t[0], ((0, 0), (0, 768 - 642))),) + tuple(t[1:])

    def finish(nms, parts, tag, w, m, v):
        ghs = [_reduce_chips(nm, pt) for nm, pt in zip(nms, parts)]
        got2 = _share_grad(ghs, tag)
        return [_update(nm, c1, gh, g2, wi, mi, vi) for nm, gh, g2, wi, mi, vi in zip(nms, ghs, got2, w, m, v)]

    late_out = finish(names[1:], own_part(landed, ps_late), "late", w_big[1:], m_big[1:], v_big[1:])
    ps_in, landed_in = _exchange_wait(ins, inr, in_thru, inland_thru, late_out[-1][3], "in")
    big_out = finish(names[:1], own_part(landed_in, ps_in), "in", *[padded(t)[:1] for t in (w_big, m_big, v_big)])
    big_out += late_out
    big_out = [[big_out[j][i][:, :642] if j == 0 else big_out[j][i] for j in range(6)] for i in range(4)]

    view = lambda t: t.reshape(t.shape[-3:]) if t.ndim == 4 else t.reshape(t.shape[-2:])
    views = lambda d: {k: view(d[k]) for k in SMALL_NAMES}
    me = 4 * lax.axis_index("x") + 2 * lax.axis_index("y") + c
    ss, sr, sthru, _ = tail["small"]
    vec, w8, lv, lw = _small_wait(ss, sr, sthru, big_out[0][3])
    all_v = lax.dynamic_update_slice(lv, vec[None], (me, 0, 0))
    all_w = lax.dynamic_update_slice(lw, w8[None], (me, 0, 0, 0))
    loss11, res_s = _small_update(all_v, all_w, views(small), views(m_small), views(v_small))
    loss = loss11[0, 0]

    def small_out(i, name):
        return res_s[name][i].reshape(small[name].shape)

    order = ["w_in", "f_bias", "sg_ln_g", "sg_ln_b", "sg_w", "sg_b", "att_out_g", "sg_out_g", "w_out",
             "pre_mix_g", "post_mix_g", "pre_ffn_g", "post_ffn_g", "w_ff1", "w_ff2", "ple_w", "ple_gate_w",
             "ple_gate_b"]
    big_idx = {"w_in": 0, "w_out": 1, "w_ff1": 2, "w_ff2": 3, "ple_w": 4, "ple_gate_w": 5}
    outs = [loss, dx[None]]
    for i in range(4):
        for name in order:
            if name in big_idx:
                outs.append(big_out[i][big_idx[name]][None])
            else:
                outs.append(small_out(i, name))
    return tuple(outs)
```

```python
import math

import jax
import jax.numpy as jnp
from jax import lax
from jax.experimental import pallas as pl
from jax.experimental.pallas import tpu as pltpu

F32 = jnp.float32
BF16 = jnp.bfloat16
MESH = pl.DeviceIdType.MESH

D = 1024
DH = 64
NH = 8
AW = 512
CH = 128
DFF = 4096
ZW = 5 * AW + 128
EPS = 1e-6
NEG = -1e30
MASKED = -2e30

TM = 256
TQ = 256

LR, B1, B2, AEPS, WD, STEP = 0.001, 0.9, 0.999, 1e-08, 0.01, 10
BC1 = 1.0 - B1 ** STEP
BC2 = 1.0 - B2 ** STEP

VEC_NAMES = ("f_bias", "sg_ln_g", "sg_ln_b", "sg_b", "att_out_g", "sg_out_g", "pre_mix_g",
             "post_mix_g", "pre_ffn_g", "post_ffn_g", "ple_gate_b")


def _dot(a, b):
    return jnp.dot(a, b, preferred_element_type=F32)


def _dot_nt(a, b):
    return lax.dot_general(a, b, (((1,), (1,)), ((), ())), preferred_element_type=F32)


def _dot_tn(a, b):
    return lax.dot_general(a, b, (((0,), (0,)), ((), ())), preferred_element_type=F32)


def _split3(x):
    h = x.astype(BF16)
    r = x - h.astype(F32)
    m = r.astype(BF16)
    l = (r - m.astype(F32)).astype(BF16)
    return h, m, l


def _dot01(sel, x):
    h, m, l = _split3(x)
    return _dot(sel, h) + _dot(sel, m) + _dot(sel, l)


def _dot01_r(x, sel):
    h, m, l = _split3(x)
    return _dot(h, sel) + _dot(m, sel) + _dot(l, sel)


def _dot01_tn(x, sel):
    h, m, l = _split3(x)
    return _dot_tn(h, sel) + _dot_tn(m, sel) + _dot_tn(l, sel)


def _rs(x, n):
    return lax.rsqrt(jnp.sum(x * x, axis=-1, keepdims=True) * (1.0 / n) + EPS)


def _rms_bwd(dn, x, rs, g, n):
    w = dn * g
    dx = rs * w - x * ((rs * rs * rs) * (1.0 / n) * jnp.sum(w * x, axis=-1, keepdims=True))
    return dx, jnp.sum(dn * x * rs, axis=0, keepdims=True)


_GC = math.sqrt(2.0 / math.pi)


def _gelu(x):
    t = jnp.tanh(_GC * (x + 0.044715 * x * x * x))
    return 0.5 * x * (1.0 + t), t


def _gelu_grad(x, t):
    return 0.5 * (1.0 + t) + 0.5 * x * (1.0 - t * t) * (_GC * (1.0 + 3.0 * 0.044715 * x * x))


def _params(vmem_mb, sem=("arbitrary",)):
    return pltpu.CompilerParams(dimension_semantics=sem, vmem_limit_bytes=vmem_mb * 1024 * 1024)


def _row_call(name, body, T, tm, tiled, resident, outs, accs, scratch=(), reverse=False, vmem_mb=48):
    nt = T // tm
    n_t, n_r, n_o, n_a = len(tiled), len(resident), len(outs), len(accs)

    def kern(*refs):
        t_refs = refs[:n_t]
        r_hbm = refs[n_t:n_t + n_r]
        o_refs = refs[n_t + n_r:n_t + n_r + n_o]
        a_refs = refs[n_t + n_r + n_o:n_t + n_r + n_o + n_a]
        r_vmem = refs[n_t + n_r + n_o + n_a:n_t + 2 * n_r + n_o + n_a]
        s_refs = refs[n_t + 2 * n_r + n_o + n_a:]

        @pl.when(pl.program_id(0) == 0)
        def _():
            for h, v in zip(r_hbm, r_vmem):
                pltpu.sync_copy(h, v)
            for a in a_refs + s_refs:
                a[...] = jnp.zeros(a.shape, a.dtype)

        body(t_refs, r_vmem, o_refs, a_refs, s_refs)

    if reverse:
        idx = lambda i: (nt - 1 - i, 0)
        idx_t = lambda i: (nt - 1 - i, 0, 0)
    else:
        idx = lambda i: (i, 0)
        idx_t = lambda i: (i, 0, 0)
    arrays, in_specs = [], []
    for a in tiled:
        if isinstance(a, tuple):
            arrays.append(a[0])
            in_specs.append(pl.BlockSpec((None, a[0].shape[1], tm), idx_t))
        else:
            arrays.append(a)
            in_specs.append(pl.BlockSpec((tm, a.shape[1]), idx))
    in_specs += [pl.BlockSpec(memory_space=pl.ANY) for _ in resident]
    out_shape, out_specs = [], []
    for o in outs:
        if len(o) == 3:
            out_shape.append(jax.ShapeDtypeStruct((nt, o[0], tm), o[1]))
            out_specs.append(pl.BlockSpec((None, o[0], tm), idx_t))
        else:
            out_shape.append(jax.ShapeDtypeStruct((T, o[0]), o[1]))
            out_specs.append(pl.BlockSpec((tm, o[0]), idx))
    out_shape += [jax.ShapeDtypeStruct(s, F32) for s in accs]
    out_specs += [pl.BlockSpec(s, lambda i, n=len(s): (0,) * n) for s in accs]
    scratch_shapes = [pltpu.VMEM(r.shape, r.dtype) for r in resident]
    scratch_shapes += [pltpu.VMEM(s, F32) for s in scratch]
    return pl.pallas_call(
        kern, name=name, grid=(nt,), in_specs=in_specs, out_specs=out_specs, out_shape=out_shape,
        scratch_shapes=scratch_shapes, compiler_params=_params(vmem_mb),
    )(*arrays, *resident)


def _sg_forward(zu, zv, wm_ref, bsg, lng, lnb, mixed_ref, tm):
    gu, tu = _gelu(zu)
    vg, tv = _gelu(zv)
    mu = jnp.sum(vg, axis=-1, keepdims=True) * (1.0 / AW)
    xc = vg - mu
    rstd = lax.rsqrt(jnp.sum(xc * xc, axis=-1, keepdims=True) * (1.0 / AW) + EPS)
    xhat = xc * rstd
    vvb = (xhat * lng + lnb).astype(BF16)
    lane = lax.broadcasted_iota(jnp.int32, (CH, 128), 1)
    for c in range(tm // CH):
        for j in range(4):
            blk = vvb[c * CH:(c + 1) * CH, j * 128:(j + 1) * 128]
            m0 = _dot(wm_ref[2 * j], blk)
            m1 = _dot(wm_ref[2 * j + 1], blk)
            mixed_ref[c * CH:(c + 1) * CH, j * 128:(j + 1) * 128] = (
                jnp.where(lane < DH, m0, m1) + bsg[:, j * 128:(j + 1) * 128])
    return gu, tu, tv, xhat, rstd, vvb, mixed_ref[...]


def _head_consts():
    src = jnp.arange(AW)
    dst = (src // DH) * 128 + src % DH
    wide = jnp.arange(NH * 128)
    expand = (dst[:, None] == wide[None, :]).astype(BF16)
    heads = jnp.arange(128)
    pieces = jnp.stack([((heads[:, None] * 128 + DH + i == wide[None, :]) & (heads[:, None] < NH)).astype(BF16)
                        for i in range(3)])
    spare = wide % 128 - DH
    qconst = jnp.where((spare >= 0) & (spare < 3), -1.0, 0.0).astype(F32)[None, :]
    one64 = jnp.where(spare == 0, 1.0, 0.0).astype(F32)[None, :]
    one67 = jnp.where(spare == 3, 1.0, 0.0).astype(F32)[None, :]
    pick64 = ((wide[:, None] == heads[None, :] * 128 + DH) & (heads[None, :] < NH)).astype(BF16)
    pick67 = ((wide[:, None] == heads[None, :] * 128 + DH + 3) & (heads[None, :] < NH)).astype(BF16)
    return expand, expand.T, pieces, qconst, one64, one67, pick64, pick67


def _masked_sg_w(sg_w):
    r = lax.broadcasted_iota(jnp.int32, (CH, CH), 0)
    c = lax.broadcasted_iota(jnp.int32, (CH, CH), 1)
    return jnp.where((c <= r)[None], sg_w, 0.0)


def _pre_attn_fwd(x, gpre, win, fbias, lng, lnb, wm, bsg, gsg, expand, pieces, qconst, kconst, vconst):
    T = x.shape[0]
    tm = TM

    def body(t, r, o, a, s):
        (x_ref,) = t
        gpre_r, win_r, fb_r, lng_r, lnb_r, wm_r, bsg_r, gsg_r, ex_r, pc_r, qc_r, kc_r, vc_r = r
        a_o, flog_o, zuv_o, ysgn_o, q8_o, k8_o, v8_o = o
        carry_ref, mixed_ref = s
        xv = x_ref[...]
        av = (xv * _rs(xv, D) * gpre_r[...]).astype(BF16)
        a_o[...] = av
        z = _dot(av, win_r[...])
        zu = z[:, 3 * AW:4 * AW]
        zv = z[:, 4 * AW:5 * AW]
        zuv_o[:, 0:AW] = zu
        zuv_o[:, AW:2 * AW] = zv
        zf = z[:, 5 * AW:] + fb_r[...]
        flog_o[...] = zf
        lane = lax.broadcasted_iota(jnp.int32, (tm, 128), 1)
        logf = jnp.where(lane < NH, jnp.minimum(zf, 0.0) - jnp.log(1.0 + jnp.exp(-jnp.abs(zf))), 0.0)
        rr = lax.broadcasted_iota(jnp.int32, (tm, tm), 0)
        cc = lax.broadcasted_iota(jnp.int32, (tm, tm), 1)
        tri = (cc <= rr).astype(BF16)
        cum = _dot01(tri, logf) + carry_ref[...]
        carry_ref[...] = cum[tm - 1:tm, :]
        ex = ex_r[...]
        q8_o[...] = (_dot((z[:, 0:AW] * (DH ** -0.5)).astype(BF16), ex) + qc_r[...]).astype(BF16)
        ch, cm, cl = _split3(cum)
        k8_o[...] = (_dot(z[:, AW:2 * AW].astype(BF16), ex) + _dot(ch, pc_r[0]) + _dot(cm, pc_r[1])
                     + _dot(cl, pc_r[2]) + kc_r[...]).astype(BF16)
        v8_o[...] = (_dot(z[:, 2 * AW:3 * AW].astype(BF16), ex) + vc_r[...]).astype(BF16)
        gu, _, _, _, _, _, mixed = _sg_forward(zu, zv, wm_r, bsg_r[...], lng_r[...], lnb_r[...], mixed_ref, tm)
        ysg = gu * mixed
        ysgn_o[...] = (ysg * _rs(ysg, AW) * gsg_r[...]).astype(BF16)

    return _row_call(
        "pre_attn_fwd", body, T, tm, [x],
        [gpre, win, fbias, lng, lnb, wm, bsg, gsg, expand, pieces, qconst, kconst, vconst],
        [(D, BF16), (128, F32), (2 * AW, F32), (AW, BF16), (NH * 128, BF16), (NH * 128, BF16), (NH * 128, BF16)], [],
        scratch=[(1, 128), (tm, AW)], vmem_mb=48)


def _flash_fwd(qt8, k8, vt8, sel):
    T = k8.shape[0]
    nq = T // TQ

    def body(qt_ref, k_ref, vt_ref, sel_ref, o_ref, l_ref, u_scr, p_scr):
        qi = pl.program_id(1)
        qts = (qt_ref[0:128, :], qt_ref[128:256, :])
        dmat = (lax.broadcasted_iota(jnp.int32, (TQ, TQ), 0) - lax.broadcasted_iota(jnp.int32, (TQ, TQ), 1))
        u_scr[1] = jnp.full((2, TQ, TQ), MASKED, F32)
        p_scr[...] = jnp.zeros(p_scr.shape, BF16)

        def sub(t, carry, sc, sb, masked):
            blk_c = jnp.clip(t - 2, 0, qi)
            off_a = pl.multiple_of(jnp.minimum(t, qi) * TQ, TQ)
            new = []
            for j in (0, 1):
                m, al, acc = carry[j]
                acc = al * acc + _dot(vt_ref[blk_c, j * 128:(j + 1) * 128, :], p_scr[sc, j])
                m_new = jnp.maximum(m, jnp.max(u_scr[sb, j], axis=0, keepdims=True))
                p_scr[sb, j] = jnp.exp(u_scr[sb, j] - m_new).astype(BF16)
                u = _dot(k_ref[pl.ds(off_a, TQ), j * 128:(j + 1) * 128], qts[j])
                u_scr[sc, j] = jnp.where(dmat <= (qi - t) * TQ, u, MASKED) if masked else u
                new.append((m_new, jnp.exp(m - m_new), acc))
            return tuple(new)

        def pair(t2, carry, masked):
            return sub(2 * t2 + 1, sub(2 * t2, carry, 0, 1, masked), 1, 0, masked)

        init = tuple((jnp.full((1, TQ), NEG, F32), jnp.ones((1, TQ), F32), jnp.zeros((128, TQ), F32))
                     for _ in (0, 1))
        carry = lax.fori_loop(0, qi // 2, lambda t2, cr: pair(t2, cr, False), init)
        (m0, _, a0), (m1, _, a1) = pair(qi // 2 + 1, pair(qi // 2, carry, True), True)
        l0 = a0[DH:DH + 1, :]
        l1 = a1[DH:DH + 1, :]
        o_ref[...] = _dot01_tn(a0 * (1.0 / l0), sel_ref[0]) + _dot01_tn(a1 * (1.0 / l1), sel_ref[1])
        l_ref[0:1, :] = m0 + jnp.log(l0)
        l_ref[1:2, :] = m1 + jnp.log(l1)
        l_ref[2:8, :] = jnp.zeros((6, TQ), F32)

    return pl.pallas_call(
        body, name="flash_fwd", grid=(4, nq),
        in_specs=[pl.BlockSpec((None, 256, TQ), lambda h, i: (i, h, 0)),
                  pl.BlockSpec((T, 256), lambda h, i: (0, h)),
                  pl.BlockSpec((nq, 256, TQ), lambda h, i: (0, h, 0)),
                  pl.BlockSpec((2, 128, 128), lambda h, i: (0, 0, 0))],
        out_specs=[pl.BlockSpec((TQ, 128), lambda h, i: (i, h)),
                   pl.BlockSpec((None, 8, TQ), lambda h, i: (h, 0, i))],
        out_shape=[jax.ShapeDtypeStruct((T, AW), F32), jax.ShapeDtypeStruct((4, 8, T), F32)],
        scratch_shapes=[pltpu.VMEM((2, 2, TQ, TQ), F32), pltpu.VMEM((2, 2, TQ, TQ), BF16)],
        compiler_params=_params(40, ("arbitrary", "arbitrary")),
    )(qt8, k8, vt8, sel)


def _flash_bwd(q8, qt8, k8, v8, do8, dot8, lse, dlt):
    T = q8.shape[0]
    nk = T // TQ

    def body(q_ref, qt_ref, k_ref, v_ref, do_ref, dot_ref, l_ref, d_ref, dqt_ref, dk_ref, dv_ref,
             u_scr, dp_scr, p_scr, ds_scr):
        kb = pl.program_id(1)
        n = nk - kb

        @pl.when(kb == 0)
        def _():
            dqt_ref[...] = jnp.zeros(dqt_ref.shape, F32)

        dk_ref[...] = jnp.zeros(dk_ref.shape, F32)
        dv_ref[...] = jnp.zeros(dv_ref.shape, F32)
        u_scr[1] = jnp.full((2, TQ, TQ), MASKED, F32)
        dp_scr[1] = jnp.zeros((2, TQ, TQ), F32)
        p_scr[...] = jnp.zeros(p_scr.shape, BF16)
        ds_scr[...] = jnp.zeros(ds_scr.shape, BF16)
        dmat = (lax.broadcasted_iota(jnp.int32, (TQ, TQ), 0) - lax.broadcasted_iota(jnp.int32, (TQ, TQ), 1))
        ks = (k_ref[:, 0:128], k_ref[:, 128:256])
        vs = (v_ref[:, 0:128], v_ref[:, 128:256])

        def sub(t, sc, sb):
            blk_a = kb + jnp.minimum(t, n - 1)
            blk_c = kb + jnp.clip(t - 2, 0, n - 1)
            off_b = pl.multiple_of((kb + jnp.clip(t - 1, 0, n - 1)) * TQ, TQ)
            off_c = pl.multiple_of(blk_c * TQ, TQ)
            lim = jnp.where(t < n, t * TQ, -TQ)
            for j in (0, 1):
                hl = slice(j * 128, (j + 1) * 128)
                dqt_ref[blk_c, hl, :] += _dot_tn(ks[j], ds_scr[sc, j])
                dk_ref[:, hl] += _dot(ds_scr[sc, j], q_ref[pl.ds(off_c, TQ), hl])
                dv_ref[:, hl] += _dot(p_scr[sc, j], do_ref[pl.ds(off_c, TQ), hl])
                p = jnp.exp(u_scr[sb, j] - l_ref[j:j + 1, pl.ds(off_b, TQ)])
                p_scr[sb, j] = p.astype(BF16)
                ds_scr[sb, j] = (p * (dp_scr[sb, j] - d_ref[j:j + 1, pl.ds(off_b, TQ)])).astype(BF16)
                u_scr[sc, j] = jnp.where(dmat <= lim, _dot(ks[j], qt_ref[blk_a, hl, :]), MASKED)
                dp_scr[sc, j] = _dot(vs[j], dot_ref[blk_a, hl, :])

        def it(t2, carry):
            sub(2 * t2, 0, 1)
            sub(2 * t2 + 1, 1, 0)
            return carry

        lax.fori_loop(0, n // 2 + 1, it, 0)

        @pl.when(n % 2 == 1)
        def _():
            sub(n + 1, 0, 1)

    return pl.pallas_call(
        body, name="flash_bwd", grid=(4, nk),
        in_specs=[pl.BlockSpec((T, 256), lambda h, i: (0, h)),
                  pl.BlockSpec((nk, 256, TQ), lambda h, i: (0, h, 0)),
                  pl.BlockSpec((TQ, 256), lambda h, i: (i, h)),
                  pl.BlockSpec((TQ, 256), lambda h, i: (i, h)),
                  pl.BlockSpec((T, 256), lambda h, i: (0, h)),
                  pl.BlockSpec((nk, 256, TQ), lambda h, i: (0, h, 0)),
                  pl.BlockSpec((None, 8, T), lambda h, i: (h, 0, 0)),
                  pl.BlockSpec((None, 8, T), lambda h, i: (h, 0, 0))],
        out_specs=[pl.BlockSpec((nk, 256, TQ), lambda h, i: (0, h, 0)),
                   pl.BlockSpec((TQ, 256), lambda h, i: (i, h)),
                   pl.BlockSpec((TQ, 256), lambda h, i: (i, h))],
        out_shape=[jax.ShapeDtypeStruct((nk, NH * 128, TQ), F32), jax.ShapeDtypeStruct((T, NH * 128), F32),
                   jax.ShapeDtypeStruct((T, NH * 128), F32)],
        scratch_shapes=[pltpu.VMEM((2, 2, TQ, TQ), F32), pltpu.VMEM((2, 2, TQ, TQ), F32),
                        pltpu.VMEM((2, 2, TQ, TQ), BF16), pltpu.VMEM((2, 2, TQ, TQ), BF16)],
        compiler_params=_params(56, ("arbitrary", "arbitrary")),
    )(q8, qt8, k8, v8, do8, dot8, lse, dlt)


def _tail_fwd1(x, yatt, ysgn, gatt, wout, gpm, gpf, w1):
    T = x.shape[0]

    def body(t, r, o, a, s):
        x_ref, ya_ref, ys_ref = t
        gatt_r, wout_r, gpm_r, gpf_r, w1_r = r
        y_o, o_o, h1_o, c2_o, s_o, rr_o = o
        ya = ya_ref[...]
        yan = (ya * _rs(ya, AW) * gatt_r[...]).astype(BF16)
        y_o[:, 0:AW] = yan
        y_o[:, AW:] = ys_ref[...]
        ov = _dot(yan, wout_r[0:AW, :]) + _dot(ys_ref[...], wout_r[AW:, :])
        o_o[...] = ov
        h1 = x_ref[...] + ov * _rs(ov, D) * gpm_r[...]
        h1_o[...] = h1
        c2 = (h1 * _rs(h1, D) * gpf_r[...]).astype(BF16)
        c2_o[...] = c2
        for k in range(4):
            rr = jnp.maximum(_dot(c2, w1_r[k]), 0.0)
            rr_o[:, k * D:(k + 1) * D] = rr.astype(BF16)
            s_o[:, k * D:(k + 1) * D] = (rr * rr).astype(BF16)

    return _row_call(
        "tail_fwd1", body, T, TM, [x, yatt, ysgn], [gatt, wout, gpm, gpf, w1],
        [(D, BF16), (D, F32), (D, F32), (D, BF16), (DFF, BF16), (DFF, BF16)], [], vmem_mb=48)


def _tail_fwd2(sact, h1, p, tgt, w2, gpff, wg, bg, wpe):
    T = h1.shape[0]

    def body(t, r, o, a, s):
        s_ref, h1_ref, p_ref, t_ref = t
        w2_r, gpff_r, wg_r, bg_r, wpe_r = r
        ff_o, h2b_o, de_o, dpre_o, dh2_o = o
        loss_a, dbg_a = a
        ff = _dot(s_ref[...], w2_r[...])
        ff_o[...] = ff
        h2 = h1_ref[...] + ff * _rs(ff, D) * gpff_r[...]
        h2b = h2.astype(BF16)
        h2b_o[...] = h2b
        gate = 1.0 / (1.0 + jnp.exp(-(_dot(h2b, wg_r[...]) + bg_r[...])))
        pb = p_ref[...].astype(BF16)
        e = jnp.concatenate([_dot(pb, wpe_r[k]) for k in range(4)], axis=1)
        diff = h2 + gate * e - t_ref[...]
        loss_a[...] += jnp.sum(diff * diff, axis=0, keepdims=True)
        dh3 = diff * (1.0 / D)
        de_o[...] = (dh3 * gate).astype(BF16)
        dpre = dh3 * e * gate * (1.0 - gate)
        dbg_a[...] += jnp.sum(dpre, axis=0, keepdims=True)
        dpb = dpre.astype(BF16)
        dpre_o[...] = dpb
        dh2_o[...] = dh3 + _dot_nt(dpb, wg_r[...])

    return _row_call(
        "tail_fwd2", body, T, TM, [sact, h1, p, tgt], [w2, gpff, wg, bg, wpe],
        [(D, F32), (D, BF16), (D, BF16), (D, BF16), (D, F32)], [(1, D), (1, D)], vmem_mb=48)


def _tail_bwd(dh2, ff, rr, h1, ov, yatt, w2, w1, wout, gpff, gpf, gpm, gatt, gsel, expand):
    T = dh2.shape[0]

    def body(t, r, o, a, s):
        dh2_ref, ff_ref, rr_ref, h1_ref, o_ref, ya_ref = t
        w2_r, w1_r, wout_r, gpff_r, gpf_r, gpm_r, gatt_r, gsel_r, ex_r = r
        dff_o, dr_o, do_o, do8_o, dlt_o, dysg_o, dh1_o = o
        dgpff_a, dgpf_a, dgpm_a, dgatt_a = a
        dh2v = dh2_ref[...]
        ffv = ff_ref[...]
        dff, dg = _rms_bwd(dh2v, ffv, _rs(ffv, D), gpff_r[...], D)
        dgpff_a[...] += dg
        dffb = dff.astype(BF16)
        dff_o[...] = dffb
        drb = (_dot_nt(dffb, w2_r[...]) * (2.0 * rr_ref[...].astype(F32))).astype(BF16)
        dr_o[...] = drb
        dc2 = _dot_nt(drb[:, 0:D], w1_r[0])
        for k in range(1, 4):
            dc2 = dc2 + _dot_nt(drb[:, k * D:(k + 1) * D], w1_r[k])
        h1v = h1_ref[...]
        d1, dg = _rms_bwd(dc2, h1v, _rs(h1v, D), gpf_r[...], D)
        dgpf_a[...] += dg
        dh1 = dh2v + d1
        dh1_o[...] = dh1
        ovv = o_ref[...]
        dov, dg = _rms_bwd(dh1, ovv, _rs(ovv, D), gpm_r[...], D)
        dgpm_a[...] += dg
        dob = dov.astype(BF16)
        do_o[...] = dob
        dysg_o[...] = _dot_nt(dob, wout_r[AW:, :])
        dyan = _dot_nt(dob, wout_r[0:AW, :])
        ya = ya_ref[...]
        dya, dg = _rms_bwd(dyan, ya, _rs(ya, AW), gatt_r[...], AW)
        dgatt_a[...] += dg
        do8_o[...] = _dot(dya.astype(BF16), ex_r[...]).astype(BF16)
        dlt_o[...] = _dot01_r(dya * ya, gsel_r[...])

    return _row_call(
        "tail_bwd", body, T, TM, [dh2, ff, rr, h1, ov, yatt],
        [w2, w1, wout, gpff, gpf, gpm, gatt, gsel, expand],
        [(D, BF16), (DFF, BF16), (D, BF16), (NH * 128, BF16), (128, F32), (AW, F32), (D, F32)],
        [(1, D), (1, D), (1, D), (1, AW)], vmem_mb=56)


def _pre_attn_bwd(x, dh1, dq8, dk8, dv8, flog, zuv, dysg, gpre, win, lng, lnb, wm, wmt, bsg, gsg, gsel, shrink, pick64, pick67):
    T = x.shape[0]
    tm = TM

    def body(t, r, o, a, s):
        x_ref, dh1_ref, dq_ref, dk_ref, dv_ref, fl_ref, zuv_ref, dys_ref = t
        gpre_r, win_r, lng_r, lnb_r, wm_r, wmt_r, bsg_r, gsg_r, gsel_r, sh_r, p64_r, p67_r = r
        dx_o, dz_o = o
        dgpre_a, dfb_a, dgsg_a, dlng_a, dlnb_a, dws_a, dbs_a, dsb_a = a
        carry_ref, mixed_ref, dvv_ref = s
        dq8v = dq_ref[...]
        dk8v = dk_ref[...]
        dcv = _dot01_r(dq8v, p67_r[...]) + _dot01_r(dk8v, p64_r[...])
        rr = lax.broadcasted_iota(jnp.int32, (tm, tm), 0)
        cc = lax.broadcasted_iota(jnp.int32, (tm, tm), 1)
        triu = (cc >= rr).astype(BF16)
        dlogf = _dot01(triu, dcv) + carry_ref[...]
        carry_ref[...] = dlogf[0:1, :]
        dzf = dlogf * (1.0 / (1.0 + jnp.exp(fl_ref[...])))
        dfb_a[...] += jnp.sum(dzf, axis=0, keepdims=True)
        dz_o[:, 5 * AW:] = dzf.astype(BF16)
        zu = zuv_ref[:, 0:AW]
        zv = zuv_ref[:, AW:]
        gu, tu, tv, xhat, rstd, vvb, mixed = _sg_forward(
            zu, zv, wm_r, bsg_r[...], lng_r[...], lnb_r[...], mixed_ref, tm)
        ysg = gu * mixed
        dysg_n = dys_ref[...]
        dys, dg = _rms_bwd(dysg_n, ysg, _rs(ysg, AW), gsg_r[...], AW)
        dgsg_a[...] += dg
        dgu = dys * mixed
        dmix = dys * gu
        dmb = dmix.astype(BF16)
        lane = lax.broadcasted_iota(jnp.int32, (CH, 128), 1)
        lo = lane < DH
        for c in range(tm // CH):
            rows = slice(c * CH, (c + 1) * CH)
            dbs_a[...] += dmix[rows, :]
            for j in range(4):
                cols = slice(j * 128, (j + 1) * 128)
                dmblk = dmb[rows, cols]
                vblk = vvb[rows, cols]
                d0 = _dot(wmt_r[2 * j], dmblk)
                d1 = _dot(wmt_r[2 * j + 1], dmblk)
                dvv_ref[rows, cols] = jnp.where(lo, d0, d1)
                dws_a[2 * j] += _dot_nt(jnp.where(lo, dmblk, jnp.zeros_like(dmblk)), vblk)
                dws_a[2 * j + 1] += _dot_nt(jnp.where(lo, jnp.zeros_like(dmblk), dmblk), vblk)
        dvv = dvv_ref[...]
        dlng_a[...] += jnp.sum(dvv * xhat, axis=0, keepdims=True)
        dlnb_a[...] += jnp.sum(dvv, axis=0, keepdims=True)
        dxh = dvv * lng_r[...]
        dvg = rstd * (dxh - jnp.sum(dxh, axis=-1, keepdims=True) * (1.0 / AW)
                      - xhat * (jnp.sum(dxh * xhat, axis=-1, keepdims=True) * (1.0 / AW)))
        dz_o[:, 3 * AW:4 * AW] = (dgu * _gelu_grad(zu, tu)).astype(BF16)
        dz_o[:, 4 * AW:5 * AW] = (dvg * _gelu_grad(zv, tv)).astype(BF16)
        dz_o[:, 0:AW] = _dot((dq8v * (DH ** -0.5)).astype(BF16), sh_r[...]).astype(BF16)
        dz_o[:, AW:2 * AW] = _dot(dk8v.astype(BF16), sh_r[...]).astype(BF16)
        dz_o[:, 2 * AW:3 * AW] = _dot(dv_ref[...].astype(BF16), sh_r[...]).astype(BF16)
        da = _dot_nt(dz_o[...], win_r[...])
        xv = x_ref[...]
        dxa, dg = _rms_bwd(da, xv, _rs(xv, D), gpre_r[...], D)
        dgpre_a[...] += dg
        dx_o[...] = dh1_ref[...] + dxa

        @pl.when(pl.program_id(0) == T // tm - 1)
        def _():
            dsb_a[...] = _dot01_r(dbs_a[...], gsel_r[...])

    outs = _row_call(
        "pre_attn_bwd", body, T, tm, [x, dh1, dq8, dk8, dv8, flog, zuv, dysg],
        [gpre, win, lng, lnb, wm, wmt, bsg, gsg, gsel, shrink, pick64, pick67],
        [(D, F32), (ZW, BF16)],
        [(1, D), (1, 128), (1, AW), (1, AW), (1, AW), (8, CH, CH), (CH, AW), (CH, 128)],
        scratch=[(1, 128), (tm, AW), (tm, AW)], reverse=True, vmem_mb=48)
    return outs


def _matmul_tn(name, a, b, tn=512, tt=2048, shards=1):
    T, K = a.shape
    N = b.shape[1]
    tk = min(K, 1024)
    tn = min(tn, N // shards)
    tt = min(tt, T)
    nj = N // shards // tn

    def body(a_ref, b_ref, o_ref):
        @pl.when(pl.program_id(2) == 0)
        def _():
            o_ref[...] = jnp.zeros(o_ref.shape, F32)

        o_ref[...] += _dot_tn(a_ref[...].astype(BF16), b_ref[...].astype(BF16))

    if shards == 1:
        out_shape = jax.ShapeDtypeStruct((K, N), F32)
        out_spec = pl.BlockSpec((tk, tn), lambda i, j, t: (i, j))
    else:
        out_shape = jax.ShapeDtypeStruct((shards, K, N // shards), F32)
        out_spec = pl.BlockSpec((None, tk, tn), lambda i, j, t: (j // nj, i, j % nj))
    return pl.pallas_call(
        body, name=name, grid=(K // tk, N // tn, T // tt),
        in_specs=[pl.BlockSpec((tt, tk), lambda i, j, t: (t, i)),
                  pl.BlockSpec((tt, tn), lambda i, j, t: (t, j))],
        out_specs=out_spec, out_shape=out_shape,
        compiler_params=_params(40, ("arbitrary", "arbitrary", "arbitrary")),
    )(a, b)


def _me():
    return lax.axis_index("x"), lax.axis_index("y"), lax.axis_index("c")


HBM_SPEC = pl.BlockSpec(memory_space=pltpu.HBM)


def _gather_weights(mine):
    half = mine.shape[0] // 2

    def body(mine_ref, out_ref, ici_send, ici_recv, d2d_send, d2d_recv):
        x, y, c = _me()
        k_me = 2 * x + y
        chips = [(1 - x, y), (x, 1 - y), (1 - x, 1 - y)]
        my_rows = pl.ds(pl.multiple_of(c * half, 16), half)
        sib_rows = pl.ds(pl.multiple_of((1 - c) * half, 16), half)

        def over_ici(j, k, to):
            src = mine_ref.at[my_rows] if k is None else out_ref.at[k, my_rows]
            return pltpu.make_async_remote_copy(
                src_ref=src, dst_ref=out_ref.at[k_me if k is None else k, my_rows], send_sem=ici_send.at[j],
                recv_sem=ici_recv.at[j], device_id=to, device_id_type=MESH)

        def over_d2d(j, k, rows):
            return pltpu.make_async_remote_copy(
                src_ref=out_ref.at[k, rows], dst_ref=out_ref.at[k, rows], send_sem=d2d_send.at[j],
                recv_sem=d2d_recv.at[j], device_id=(x, y, 1 - c), device_id_type=MESH)

        first = [over_ici(j, None, (cx, cy, c)) for j, (cx, cy) in enumerate(chips)]
        for cp in first:
            cp.start()
        passed = [over_d2d(j, 2 * cx + cy, my_rows) for j, (cx, cy) in enumerate(chips)]
        for j, (cx, cy) in enumerate(chips):
            over_ici(j, 2 * cx + cy, (cx, cy, c)).wait_recv()
            passed[j].start()
        for j, (cx, cy) in enumerate(chips):
            over_d2d(j, 2 * cx + cy, sib_rows).wait_recv()
        for cp in first + passed:
            cp.wait_send()

    return pl.pallas_call(
        body, name="gather_weights", in_specs=[HBM_SPEC], out_specs=HBM_SPEC,
        out_shape=jax.ShapeDtypeStruct((4,) + mine.shape, mine.dtype),
        scratch_shapes=[pltpu.SemaphoreType.DMA((3,)), pltpu.SemaphoreType.DMA((3,)), pltpu.SemaphoreType.DMA((3,)),
                        pltpu.SemaphoreType.DMA((3,))],
    )(mine)


SEM_SPEC = pl.BlockSpec(memory_space=pltpu.SEMAPHORE)
EFFECT = pltpu.SideEffectType.DATAFLOW_SIDE_EFFECTING


def _gather_late_start(mine):
    def body(mine_ref, land_ref, send_sems, recv_sems, mine_thru, land_thru, token):
        x, y, c = _me()
        k_me = 2 * x + y
        for j, (cx, cy) in enumerate([(1 - x, y), (x, 1 - y), (1 - x, 1 - y)]):
            pltpu.make_async_remote_copy(
                src_ref=mine_ref, dst_ref=land_ref.at[k_me], send_sem=send_sems.at[j], recv_sem=recv_sems.at[j],
                device_id=(cx, cy, c), device_id_type=MESH).start()
        token[...] = jnp.zeros(token.shape, F32)

    land = lax.empty((4,) + mine.shape, mine.dtype)
    return pl.pallas_call(
        body, name="gather_late_start",
        out_shape=(pltpu.SemaphoreType.DMA((3,)), pltpu.SemaphoreType.DMA((3,)), pltpu.HBM(mine.shape, mine.dtype),
                   pltpu.HBM(land.shape, land.dtype), jax.ShapeDtypeStruct((8, 128), F32)),
        in_specs=(HBM_SPEC, HBM_SPEC),
        out_specs=(SEM_SPEC, SEM_SPEC, HBM_SPEC, HBM_SPEC, pl.BlockSpec(memory_space=pltpu.VMEM)),
        input_output_aliases={0: 2, 1: 3},
        compiler_params=pltpu.CompilerParams(has_side_effects=EFFECT),
    )(pltpu.with_memory_space_constraint(mine, pltpu.HBM), pltpu.with_memory_space_constraint(land, pltpu.HBM))


def _gather_late_wait(send_sems, recv_sems, mine_thru, land_thru, after):
    def body(mine_ref, land_ref, send_sems, recv_sems, after_ref, mine_dead, got_ref):
        x, y, c = _me()
        for j, (cx, cy) in enumerate([(1 - x, y), (x, 1 - y), (1 - x, 1 - y)]):
            cp = pltpu.make_async_remote_copy(
                src_ref=mine_ref, dst_ref=land_ref.at[2 * cx + cy], send_sem=send_sems.at[j],
                recv_sem=recv_sems.at[j], device_id=(cx, cy, c), device_id_type=MESH)
            cp.wait_send()
            cp.wait_recv()

    return pl.pallas_call(
        body, name="gather_late_wait",
        out_shape=(pltpu.HBM(mine_thru.shape, mine_thru.dtype), pltpu.HBM(land_thru.shape, land_thru.dtype)),
        in_specs=(HBM_SPEC, HBM_SPEC, SEM_SPEC, SEM_SPEC, pl.BlockSpec(memory_space=pl.ANY)),
        out_specs=(HBM_SPEC, HBM_SPEC), input_output_aliases={0: 0, 1: 1},
        compiler_params=pltpu.CompilerParams(has_side_effects=EFFECT),
    )(mine_thru, land_thru, send_sems, recv_sems, after)[1]


def _swap_halves(gs, tag):
    n = len(gs)

    def body(*refs):
        g_refs, got_refs, send_sems, recv_sems = refs[:n], refs[n:2 * n], refs[2 * n], refs[2 * n + 1]
        x, y, c = _me()
        cps = []
        for i, (g_ref, got_ref) in enumerate(zip(g_refs, got_refs)):
            half = g_ref.shape[1] // 2
            theirs = pl.multiple_of((1 - c) * half, 16)
            cps.append(pltpu.make_async_remote_copy(
                src_ref=g_ref.at[:, pl.ds(theirs, half), :], dst_ref=got_ref, send_sem=send_sems.at[i],
                recv_sem=recv_sems.at[i], device_id=(x, y, 1 - c), device_id_type=MESH))
        for cp in cps:
            cp.start()
        for cp in cps:
            cp.wait()

    return pl.pallas_call(
        body, name="swap_halves_" + tag, in_specs=[HBM_SPEC] * n, out_specs=[HBM_SPEC] * n,
        out_shape=[jax.ShapeDtypeStruct((4, g.shape[1] // 2, g.shape[2]), F32) for g in gs],
        scratch_shapes=[pltpu.SemaphoreType.DMA((n,)), pltpu.SemaphoreType.DMA((n,))],
    )(*gs)


def _swap_start(gs, tag):
    n = len(gs)

    def body(*refs):
        g_refs, land_refs, send_sems, recv_sems = refs[:n], refs[n:2 * n], refs[2 * n], refs[2 * n + 1]
        x, y, c = _me()
        for i, (g_ref, land_ref) in enumerate(zip(g_refs, land_refs)):
            half = g_ref.shape[1] // 2
            theirs = pl.multiple_of((1 - c) * half, 16)
            pltpu.make_async_remote_copy(
                src_ref=g_ref.at[:, pl.ds(theirs, half), :], dst_ref=land_ref, send_sem=send_sems.at[i],
                recv_sem=recv_sems.at[i], device_id=(x, y, 1 - c), device_id_type=MESH).start()

    lands = [lax.empty((4, g.shape[1] // 2, g.shape[2]), F32) for g in gs]
    hbm = lambda t: pltpu.HBM(t.shape, t.dtype)
    res = pl.pallas_call(
        body, name="swap_start_" + tag,
        out_shape=(pltpu.SemaphoreType.DMA((n,)), pltpu.SemaphoreType.DMA((n,)), *[hbm(t) for t in gs],
                   *[hbm(t) for t in lands]),
        in_specs=(HBM_SPEC,) * (2 * n), out_specs=(SEM_SPEC, SEM_SPEC) + (HBM_SPEC,) * (2 * n),
        input_output_aliases={i: 2 + i for i in range(2 * n)},
        compiler_params=pltpu.CompilerParams(has_side_effects=EFFECT),
    )(*[pltpu.with_memory_space_constraint(t, pltpu.HBM) for t in list(gs) + lands])
    return res[0], res[1], res[2:2 + n], res[2 + n:2 + 2 * n]


def _swap_wait(send_sems, recv_sems, g_thru, land_thru, after, tag):
    n = len(g_thru)

    def body(*refs):
        g_refs, land_refs, send_sems, recv_sems = refs[:n], refs[n:2 * n], refs[2 * n], refs[2 * n + 1]
        x, y, c = _me()
        for i, (g_ref, land_ref) in enumerate(zip(g_refs, land_refs)):
            half = g_ref.shape[1] // 2
            theirs = pl.multiple_of((1 - c) * half, 16)
            pltpu.make_async_remote_copy(
                src_ref=g_ref.at[:, pl.ds(theirs, half), :], dst_ref=land_ref, send_sem=send_sems.at[i],
                recv_sem=recv_sems.at[i], device_id=(x, y, 1 - c), device_id_type=MESH).wait()

    hbm = lambda t: pltpu.HBM(t.shape, t.dtype)
    res = pl.pallas_call(
        body, name="swap_wait_" + tag, out_shape=tuple(hbm(t) for t in list(g_thru) + list(land_thru)),
        in_specs=(HBM_SPEC,) * (2 * n) + (SEM_SPEC, SEM_SPEC, pl.BlockSpec(memory_space=pl.ANY)),
        out_specs=(HBM_SPEC,) * (2 * n), input_output_aliases={i: i for i in range(2 * n)},
        compiler_params=pltpu.CompilerParams(has_side_effects=EFFECT),
    )(*g_thru, *land_thru, send_sems, recv_sems, after)
    return res[:n], res[n:]


def _pair_sum(name, c1, g, got):
    half, cols = got.shape[1], got.shape[2]

    def body(c_ref, a_ref, b_ref, o_ref):
        o_ref[...] = (a_ref[...] + b_ref[...]).astype(BF16)

    return pl.pallas_call(
        body, name="pair_sum_" + name,
        grid_spec=pltpu.PrefetchScalarGridSpec(
            num_scalar_prefetch=1, grid=(4,),
            in_specs=[pl.BlockSpec((1, half, cols), lambda k, c_ref: (k, c_ref[0], 0)),
                      pl.BlockSpec((1, half, cols), lambda k, c_ref: (k, 0, 0))],
            out_specs=pl.BlockSpec((1, half, cols), lambda k, c_ref: (k, 0, 0))),
        out_shape=jax.ShapeDtypeStruct(got.shape, BF16), compiler_params=_params(32),
    )(c1, g, got)


def _exchange_start(pss, tag):
    n = len(pss)

    def body(*refs):
        ps_refs, land_refs = refs[:n], refs[n:2 * n]
        send_sems, recv_sems = refs[2 * n], refs[2 * n + 1]
        token = refs[4 * n + 2]
        x, y, c = _me()
        k_me = 2 * x + y
        for i, (ps_ref, land_ref) in enumerate(zip(ps_refs, land_refs)):
            for j, (cx, cy) in enumerate([(1 - x, y), (x, 1 - y), (1 - x, 1 - y)]):
                pltpu.make_async_remote_copy(
                    src_ref=ps_ref.at[2 * cx + cy], dst_ref=land_ref.at[k_me], send_sem=send_sems.at[3 * i + j],
                    recv_sem=recv_sems.at[3 * i + j], device_id=(cx, cy, c), device_id_type=MESH).start()
        token[...] = jnp.zeros(token.shape, F32)

    lands = [lax.empty(ps.shape, ps.dtype) for ps in pss]
    hbm = lambda t: pltpu.HBM(t.shape, t.dtype)
    res = pl.pallas_call(
        body, name="exchange_start_" + tag,
        out_shape=(pltpu.SemaphoreType.DMA((3 * n,)), pltpu.SemaphoreType.DMA((3 * n,)), *[hbm(t) for t in pss],
                   *[hbm(t) for t in lands], jax.ShapeDtypeStruct((8, 128), F32)),
        in_specs=(HBM_SPEC,) * (2 * n),
        out_specs=(SEM_SPEC, SEM_SPEC) + (HBM_SPEC,) * (2 * n) + (pl.BlockSpec(memory_space=pltpu.VMEM),),
        input_output_aliases={i: 2 + i for i in range(2 * n)},
        compiler_params=pltpu.CompilerParams(has_side_effects=EFFECT),
    )(*[pltpu.with_memory_space_constraint(t, pltpu.HBM) for t in list(pss) + lands])
    return res[0], res[1], res[2:2 + n], res[2 + n:2 + 2 * n], res[2 + 2 * n]


def _exchange_wait(send_sems, recv_sems, ps_thru, land_thru, after, tag):
    n = len(ps_thru)

    def body(*refs):
        ps_refs, land_refs = refs[:n], refs[n:2 * n]
        send_sems, recv_sems = refs[2 * n], refs[2 * n + 1]
        x, y, c = _me()
        k_me = 2 * x + y
        for i, (ps_ref, land_ref) in enumerate(zip(ps_refs, land_refs)):
            for j, (cx, cy) in enumerate([(1 - x, y), (x, 1 - y), (1 - x, 1 - y)]):
                cp = pltpu.make_async_remote_copy(
                    src_ref=ps_ref.at[k_me], dst_ref=land_ref.at[2 * cx + cy], send_sem=send_sems.at[3 * i + j],
                    recv_sem=recv_sems.at[3 * i + j], device_id=(cx, cy, c), device_id_type=MESH)
                cp.wait_send()
                cp.wait_recv()

    hbm = lambda t: pltpu.HBM(t.shape, t.dtype)
    res = pl.pallas_call(
        body, name="exchange_wait_" + tag,
        out_shape=tuple(hbm(t) for t in list(ps_thru) + list(land_thru)),
        in_specs=(HBM_SPEC,) * (2 * n) + (SEM_SPEC, SEM_SPEC, pl.BlockSpec(memory_space=pl.ANY)),
        out_specs=(HBM_SPEC,) * (2 * n), input_output_aliases={i: i for i in range(2 * n)},
        compiler_params=pltpu.CompilerParams(has_side_effects=EFFECT),
    )(*ps_thru, *land_thru, send_sems, recv_sems, after)
    return res[:n], res[n:]


def _adamw(w, g, m, v):
    m = B1 * m + (1.0 - B1) * g
    v = B2 * v + (1.0 - B2) * (g * g)
    delta = -LR * ((m / BC1) / (jnp.sqrt(v / BC2) + AEPS) + WD * w)
    return delta, m, v


def _reduce_chips(name, parts):
    half, cols = parts.shape[1], parts.shape[2]

    def body(p_ref, o_ref):
        f = lambda k: p_ref[k].astype(F32)
        o_ref[...] = ((f(0) + f(1)) + f(2)) + f(3)

    return pl.pallas_call(
        body, name="reduce_chips_" + name, grid=(1,),
        in_specs=[pl.BlockSpec((4, half, cols), lambda i: (0, 0, 0))],
        out_specs=pl.BlockSpec((half, cols), lambda i: (0, 0)),
        out_shape=jax.ShapeDtypeStruct((half, cols), F32), compiler_params=_params(32),
    )(parts)


def _share_grad(ghs, tag):
    n = len(ghs)

    def body(*refs):
        g_refs, got_refs, send_sems, recv_sems = refs[:n], refs[n:2 * n], refs[2 * n], refs[2 * n + 1]
        x, y, c = _me()
        cps = [pltpu.make_async_remote_copy(
            src_ref=g_ref, dst_ref=got_ref, send_sem=send_sems.at[i], recv_sem=recv_sems.at[i],
            device_id=(x, y, 1 - c), device_id_type=MESH) for i, (g_ref, got_ref) in enumerate(zip(g_refs, got_refs))]
        for cp in cps:
            cp.start()
        for cp in cps:
            cp.wait()

    return pl.pallas_call(
        body, name="share_grad_" + tag, in_specs=[HBM_SPEC] * n, out_specs=[HBM_SPEC] * n,
        out_shape=[jax.ShapeDtypeStruct(g.shape, F32) for g in ghs],
        scratch_shapes=[pltpu.SemaphoreType.DMA((n,)), pltpu.SemaphoreType.DMA((n,))],
    )(*ghs)


def _update(name, c1, gh, got, w, m, v):
    half, cols = gh.shape

    def body(c_ref, gh_ref, got_ref, w_ref, m_ref, v_ref, g_o, d_o, m_o, v_o):
        g = jnp.where(pl.program_id(0) == c_ref[0], gh_ref[...], got_ref[...])
        delta, mn, vn = _adamw(w_ref[...], g, m_ref[...], v_ref[...])
        g_o[...] = g
        d_o[...] = delta
        m_o[...] = mn
        v_o[...] = vn

    same = pl.BlockSpec((half, cols), lambda h, c_ref: (0, 0))
    rows = pl.BlockSpec((half, cols), lambda h, c_ref: (h, 0))
    return pl.pallas_call(
        body, name="update_" + name,
        grid_spec=pltpu.PrefetchScalarGridSpec(
            num_scalar_prefetch=1, grid=(2,), in_specs=[same, same, rows, rows, rows],
            out_specs=[rows, rows, rows, rows]),
        out_shape=[jax.ShapeDtypeStruct(w.shape, F32)] * 4, compiler_params=_params(40),
    )(c1, gh, got, w, m, v)


SMALL_NAMES = ("sg_w",) + VEC_NAMES
VEC_ROWS = 24
VEC_ROW = {"f_bias": 0, "sg_ln_g": 1, "sg_ln_b": 2, "att_out_g": 3, "sg_out_g": 4, "pre_mix_g": 5,
           "post_mix_g": 6, "pre_ffn_g": 7, "sg_b": 8, "post_ffn_g": 16, "ple_gate_b": 17}
LOSS_VEC_ROW = 18


def _small_pack(g, loss_l):
    n = len(SMALL_NAMES)

    def body(*refs):
        g_r = dict(zip(SMALL_NAMES, refs[0:n]))
        loss_r, vec_o, w_o = refs[n:]
        vec_o[...] = jnp.zeros((VEC_ROWS, 1024), F32)
        for name in VEC_NAMES:
            val = g_r[name][...]
            vec_o[pl.ds(VEC_ROW[name], val.shape[0]), pl.ds(0, val.shape[1])] = val
        vec_o[pl.ds(LOSS_VEC_ROW, 1), :] = loss_r[...] * (0.5 / D)
        rr = lax.broadcasted_iota(jnp.int32, (CH, CH), 0)
        cc = lax.broadcasted_iota(jnp.int32, (CH, CH), 1)
        w_o[...] = jnp.where((cc <= rr)[None], g_r["sg_w"][...], 0.0)

    vm = pl.BlockSpec(memory_space=pltpu.VMEM)
    args = [g[k] for k in SMALL_NAMES] + [loss_l]
    return pl.pallas_call(
        body, name="small_pack", in_specs=[vm] * len(args), out_specs=[vm, vm],
        out_shape=[jax.ShapeDtypeStruct((VEC_ROWS, 1024), F32), jax.ShapeDtypeStruct((8, CH, CH), F32)],
    )(*args)


def _small_peers(x, y, c):
    rels = [(rx, ry, rc) for rx in (0, 1) for ry in (0, 1) for rc in (0, 1)][1:]
    return [((x + rx) % 2, (y + ry) % 2, (c + rc) % 2) for rx, ry, rc in rels]


def _small_start(vec, w8):
    def body(vec_ref, w_ref, lv_ref, lw_ref, send_sems, recv_sems, vec_thru, w_thru, lv_thru, lw_thru, token):
        x, y, c = _me()
        me = 4 * x + 2 * y + c
        for j, to in enumerate(_small_peers(x, y, c)):
            for i, (src, land) in enumerate(((vec_ref, lv_ref), (w_ref, lw_ref))):
                pltpu.make_async_remote_copy(
                    src_ref=src, dst_ref=land.at[me], send_sem=send_sems.at[2 * j + i],
                    recv_sem=recv_sems.at[2 * j + i], device_id=to, device_id_type=MESH).start()
        token[...] = jnp.zeros(token.shape, F32)

    ops = [vec, w8, lax.empty((8,) + vec.shape, F32), lax.empty((8,) + w8.shape, F32)]
    hbm = lambda t: pltpu.HBM(t.shape, t.dtype)
    res = pl.pallas_call(
        body, name="small_start",
        out_shape=(pltpu.SemaphoreType.DMA((14,)), pltpu.SemaphoreType.DMA((14,)), *[hbm(t) for t in ops],
                   jax.ShapeDtypeStruct((8, 128), F32)),
        in_specs=(HBM_SPEC,) * 4,
        out_specs=(SEM_SPEC, SEM_SPEC) + (HBM_SPEC,) * 4 + (pl.BlockSpec(memory_space=pltpu.VMEM),),
        input_output_aliases={i: 2 + i for i in range(4)},
        compiler_params=pltpu.CompilerParams(has_side_effects=EFFECT),
    )(*[pltpu.with_memory_space_constraint(t, pltpu.HBM) for t in ops])
    return res[0], res[1], res[2:6], res[6]


def _small_wait(send_sems, recv_sems, thru, after):
    def body(vec_ref, w_ref, lv_ref, lw_ref, send_sems, recv_sems, after_ref, vec_o, w_o, lv_o, lw_o):
        x, y, c = _me()
        for j, (px, py, pc) in enumerate(_small_peers(x, y, c)):
            for i, (src, land) in enumerate(((vec_ref, lv_ref), (w_ref, lw_ref))):
                cp = pltpu.make_async_remote_copy(
                    src_ref=src, dst_ref=land.at[4 * px + 2 * py + pc], send_sem=send_sems.at[2 * j + i],
                    recv_sem=recv_sems.at[2 * j + i], device_id=(px, py, pc), device_id_type=MESH)
                cp.wait_send()
                cp.wait_recv()

    hbm = lambda t: pltpu.HBM(t.shape, t.dtype)
    return pl.pallas_call(
        body, name="small_wait", out_shape=tuple(hbm(t) for t in thru),
        in_specs=(HBM_SPEC,) * 4 + (SEM_SPEC, SEM_SPEC, pl.BlockSpec(memory_space=pl.ANY)),
        out_specs=(HBM_SPEC,) * 4, input_output_aliases={i: i for i in range(4)},
        compiler_params=pltpu.CompilerParams(has_side_effects=EFFECT),
    )(*thru, send_sems, recv_sems, after)


def _small_update(all_v, all_w, w, m, v):
    n = len(SMALL_NAMES)

    def body(*refs):
        allv_r, allw_r = refs[0], refs[1]
        tot_v = allv_r[0]
        tot_w = allw_r[0]
        for d in range(1, 8):
            tot_v = tot_v + allv_r[d]
            tot_w = tot_w + allw_r[d]
        w_r = dict(zip(SMALL_NAMES, refs[2:2 + n]))
        m_r = dict(zip(SMALL_NAMES, refs[2 + n:2 + 2 * n]))
        v_r = dict(zip(SMALL_NAMES, refs[2 + 2 * n:2 + 3 * n]))
        loss_o = refs[2 + 3 * n]
        outs = refs[3 + 3 * n:]
        loss_o[...] = jnp.sum(tot_v[LOSS_VEC_ROW:LOSS_VEC_ROW + 1, :], axis=-1, keepdims=True) + jnp.zeros((1, 128), F32)
        for i, name in enumerate(SMALL_NAMES):
            if name == "sg_w":
                gt = tot_w
            else:
                rows, width = w_r[name].shape
                gt = tot_v[VEC_ROW[name]:VEC_ROW[name] + rows, 0:width]
            delta, mn, vn = _adamw(w_r[name][...], gt, m_r[name][...], v_r[name][...])
            outs[4 * i][...] = gt
            outs[4 * i + 1][...] = delta
            outs[4 * i + 2][...] = mn
            outs[4 * i + 3][...] = vn

    vm = pl.BlockSpec(memory_space=pltpu.VMEM)
    args = [all_v, all_w] + [d[k] for d in (w, m, v) for k in SMALL_NAMES]
    out_shape = [jax.ShapeDtypeStruct((1, 128), F32)]
    out_shape += [jax.ShapeDtypeStruct(w[k].shape, F32) for k in SMALL_NAMES for _ in range(4)]
    res = pl.pallas_call(
        body, name="small_update", in_specs=[vm] * len(args), out_specs=[vm] * len(out_shape), out_shape=out_shape,
        compiler_params=pltpu.CompilerParams(vmem_limit_bytes=32 * 1024 * 1024),
    )(*args)
    return res[0], {k: res[1 + 4 * i:5 + 4 * i] for i, k in enumerate(SMALL_NAMES)}


def _win_kernel_order(gathered):
    w_in = jnp.concatenate([gathered[k].reshape(D, 768)[:, :642] for k in range(4)], axis=1)
    return jnp.concatenate([w_in[:, :3 * AW], w_in[:, 3 * AW + NH:], w_in[:, 3 * AW:3 * AW + NH],
                            jnp.zeros((D, 128 - NH), w_in.dtype)], axis=1)


LATE_ROWS = 256 + 1024 + 1024 + 64 + 256


def _pack_late(w_out, w1, w2, plew, wg):
    return jnp.concatenate([w_out, w1, w2, plew.reshape(64, 1024), wg], axis=0)


def _unpack_late(gathered):
    return (gathered[:, 0:256].reshape(D, D), gathered[:, 256:1280], gathered[:, 1280:2304].reshape(DFF, D),
            gathered[:, 2304:2368].reshape(4, 256, 256), gathered[:, 2368:2624].reshape(D, D))


def _local_step(x, p, tgt, win_k, late_weights, token, on_ff_grads, on_tail_grads, on_small_grads, small):
    T = x.shape[0]
    row = lambda n: small[n].reshape(1, -1)
    fbias = jnp.pad(row("f_bias"), ((0, 0), (0, 128 - NH))) + token[0:1, :]
    wm = _masked_sg_w(small["sg_w"].reshape(8, CH, CH))
    wmb = wm.astype(BF16)
    wmt = jnp.swapaxes(wm, 1, 2).astype(BF16)
    bsg = jnp.repeat(small["sg_b"].reshape(8, CH).T, DH, axis=1)
    ln_g, ln_b, gsg, gatt = row("sg_ln_g"), row("sg_ln_b"), row("sg_out_g"), row("att_out_g")
    gpre, gpm, gpf, gpff, bg = row("pre_mix_g"), row("post_mix_g"), row("pre_ffn_g"), row("post_ffn_g"), row("ple_gate_b")
    gsel = (jnp.arange(AW)[:, None] // DH == jnp.arange(128)[None, :]).astype(BF16)

    expand, shrink, pieces, qconst, one64, one67, pick64, pick67 = _head_consts()
    a, flog, zuv, ysgn, q8, k8, v8 = _pre_attn_fwd(
        x, gpre, win_k, fbias, ln_g, ln_b, wmb, bsg, gsg, expand, pieces, qconst, one67, one64)

    slabs = lambda t: jnp.swapaxes(t.reshape(T // TQ, TQ, NH * 128), 1, 2)
    qt8 = slabs(q8)
    lanes = jnp.arange(128)
    sel = jnp.stack([((lanes[:, None] == lanes[None, :] - DH * j) & (lanes[:, None] < DH)).astype(BF16)
                     for j in (0, 1)])

    yatt, lse = _flash_fwd(qt8, k8, slabs(v8), sel)
    wout, w1, w2, plew, wg = late_weights(lse)
    y, ov, h1, c2, sact, rr = _tail_fwd1(x, yatt, ysgn, gatt, wout, gpm, gpf, w1)
    ff, h2b, de, dpre, dh2, loss_l, dbg = _tail_fwd2(sact, h1, p, tgt, w2, gpff, wg, bg, plew)
    dff, dr, do, do8, dlt, dysg, dh1, dgpff, dgpf, dgpm, dgatt = _tail_bwd(
        dh2, ff, rr, h1, ov, yatt, w2, w1, wout, gpff, gpf, gpm, gatt, gsel, expand)
    dw1 = _matmul_tn("grad_w_ff1", c2, dr, shards=4)
    dw2 = _matmul_tn("grad_w_ff2", sact, dff)
    on_ff_grads(dw1, dw2)
    dwout = _matmul_tn("grad_w_out", y, do)
    dwg = _matmul_tn("grad_ple_gate_w", h2b, dpre)
    dplew = _matmul_tn("grad_ple_w", p, de, tn=256, shards=4)
    tail_token = on_tail_grads((dwout, dplew, dwg))
    dlt4 = jnp.pad(dlt[:, :NH].T.reshape(4, 2, T), ((0, 0), (0, 6), (0, 0))) + tail_token[0, 0]
    dqt, dk8, dv8 = _flash_bwd(q8, qt8, k8, v8, do8, slabs(do8), lse, dlt4)
    dx, dz, dgpre, dfb, dgsg, dlng, dlnb, dws, _, dsbt = _pre_attn_bwd(
        x, dh1, jnp.swapaxes(dqt, 1, 2).reshape(T, NH * 128), dk8, dv8, flog, zuv, dysg,
        gpre, win_k, ln_g, ln_b, wmb, wmt, bsg, gsg, gsel, shrink, pick64, pick67)


    dsb = dsbt[:, :8].T
    gsmall = {"sg_w": dws, "f_bias": dfb, "sg_ln_g": dlng, "sg_ln_b": dlnb, "sg_b": dsb,
              "att_out_g": dgatt, "sg_out_g": dgsg, "pre_mix_g": dgpre, "post_mix_g": dgpm, "pre_ffn_g": dgpf,
              "post_ffn_g": dgpff, "ple_gate_b": dbg}
    on_small_grads(gsmall, loss_l)
    dwin_k = _matmul_tn("grad_w_in", a, dz, tn=384)
    return loss_l, dx, dwin_k, gsmall


def kernel(x, p, w_in, f_bias, sg_ln_g, sg_ln_b, sg_w, sg_b, att_out_g, sg_out_g, w_out, pre_mix_g, post_mix_g, pre_ffn_g, post_ffn_g, w_ff1, w_ff2, ple_w, ple_gate_w, ple_gate_b, loss_target, m_w_in, m_f_bias, m_sg_ln_g, m_sg_ln_b, m_sg_w, m_sg_b, m_att_out_g, m_sg_out_g, m_w_out, m_pre_mix_g, m_post_mix_g, m_pre_ffn_g, m_post_ffn_g, m_w_ff1, m_w_ff2, m_ple_w, m_ple_gate_w, m_ple_gate_b, v_w_in, v_f_bias, v_sg_ln_g, v_sg_ln_b, v_sg_w, v_sg_b, v_att_out_g, v_sg_out_g, v_w_out, v_pre_mix_g, v_post_mix_g, v_pre_ffn_g, v_post_ffn_g, v_w_ff1, v_w_ff2, v_ple_w, v_ple_gate_w, v_ple_gate_b):
    c = lax.axis_index("c")
    big = lambda t: (t[0][0], t[1][0], t[2][0], t[3][0], t[4][0], t[5][0])
    w_big = big((w_in, w_out, w_ff1, w_ff2, ple_w, ple_gate_w))
    m_big = big((m_w_in, m_w_out, m_w_ff1, m_w_ff2, m_ple_w, m_ple_gate_w))
    v_big = big((v_w_in, v_w_out, v_w_ff1, v_w_ff2, v_ple_w, v_ple_gate_w))
    small = {"sg_w": sg_w, "f_bias": f_bias, "sg_ln_g": sg_ln_g, "sg_ln_b": sg_ln_b, "sg_b": sg_b,
             "att_out_g": att_out_g, "sg_out_g": sg_out_g, "pre_mix_g": pre_mix_g, "post_mix_g": post_mix_g,
             "pre_ffn_g": pre_ffn_g, "post_ffn_g": post_ffn_g, "ple_gate_b": ple_gate_b}
    m_small = {"sg_w": m_sg_w, "f_bias": m_f_bias, "sg_ln_g": m_sg_ln_g, "sg_ln_b": m_sg_ln_b, "sg_b": m_sg_b,
               "att_out_g": m_att_out_g, "sg_out_g": m_sg_out_g, "pre_mix_g": m_pre_mix_g,
               "post_mix_g": m_post_mix_g, "pre_ffn_g": m_pre_ffn_g, "post_ffn_g": m_post_ffn_g,
               "ple_gate_b": m_ple_gate_b}
    v_small = {"sg_w": v_sg_w, "f_bias": v_f_bias, "sg_ln_g": v_sg_ln_g, "sg_ln_b": v_sg_ln_b, "sg_b": v_sg_b,
               "att_out_g": v_att_out_g, "sg_out_g": v_sg_out_g, "pre_mix_g": v_pre_mix_g,
               "post_mix_g": v_post_mix_g, "pre_ffn_g": v_pre_ffn_g, "post_ffn_g": v_post_ffn_g,
               "ple_gate_b": v_ple_gate_b}

    k_me = 2 * lax.axis_index("x") + lax.axis_index("y")
    own_slot = lambda got, mine: lax.dynamic_update_slice(got, mine[None], (k_me, 0, 0))
    late_mine = _pack_late(*w_big[1:]).astype(BF16)
    late = _gather_late_start(late_mine)
    win_mine = jnp.pad(w_big[0], ((0, 0), (0, 768 - 642))).reshape(768, 1024).astype(BF16)
    win_k = _win_kernel_order(own_slot(_gather_weights(win_mine), win_mine))
    late_weights = lambda after: _unpack_late(
        own_slot(_gather_late_wait(late[0], late[1], late[2], late[3], after), late_mine))

    names = ("w_in", "w_out", "w_ff1", "w_ff2", "ple_w", "ple_gate_w")
    c1 = jnp.reshape(c, (1,)).astype(jnp.int32)
    own_part = lambda parts, pss: [lax.dynamic_update_slice(pt, lax.dynamic_slice_in_dim(ps, k_me, 1, 0), (k_me, 0, 0))
                                   for pt, ps in zip(parts, pss)]
    tail = {}

    def on_ff_grads(dw1, dw2):
        tail["swap"] = _swap_start([dw1, dw2.reshape(4, D, D)], "ff")

    def on_tail_grads(grads):
        dwout, dplew, dwg = grads
        ws, wr, g_thru, land_thru = tail["swap"]
        (dw1, dw2), (got1, got2) = _swap_wait(ws, wr, g_thru, land_thru, dplew, "ff")
        rest = [dwout.reshape(4, 256, D), dplew, dwg.reshape(4, 256, D)]
        got_out, got_ple, got_gate = _swap_halves(rest, "late")
        gs = [rest[0], dw1, dw2, rest[1], rest[2]]
        gots = [got_out, got1, got2, got_ple, got_gate]
        pss = [_pair_sum(nm, c1, g, got) for nm, g, got in zip(names[1:], gs, gots)]
        tail["xch"] = _exchange_start(pss, "late")
        return tail["xch"][4]

    def on_small_grads(gsmall, loss_l):
        tail["small"] = _small_start(*_small_pack(gsmall, loss_l))

    loss_l, dx, dwin_k, gsmall = _local_step(
        x[0], p[0, 0], loss_target[0], win_k, late_weights, late[4], on_ff_grads, on_tail_grads, on_small_grads,
        small)

    dwin = jnp.concatenate([dwin_k[:, :3 * AW], dwin_k[:, 5 * AW:5 * AW + NH], dwin_k[:, 3 * AW:5 * AW]], axis=1)
    dwin = jnp.pad(jnp.swapaxes(dwin.reshape(D, 4, 642), 0, 1), ((0, 0), (0, 0), (0, 768 - 642)))
    ps_in = [_pair_sum(names[0], c1, dwin, _swap_halves([dwin], "in")[0])]
    ins, inr, in_thru, inland_thru, in_token = _exchange_start(ps_in, "in")
    xs, xr, ps_thru, land_thru, _ = tail["xch"]
    ps_late, landed = _exchange_wait(xs, xr, ps_thru, land_thru, in_token, "late")
    padded = lambda t: (jnp.pad(t[0], ((0, 0), (0, 768 - 642))),) + tuple(t[1:])

    def finish(nms, parts, tag, w, m, v):
        ghs = [_reduce_chips(nm, pt) for nm, pt in zip(nms, parts)]
        got2 = _share_grad(ghs, tag)
        return [_update(nm, c1, gh, g2, wi, mi, vi) for nm, gh, g2, wi, mi, vi in zip(nms, ghs, got2, w, m, v)]

    late_out = finish(names[1:], own_part(landed, ps_late), "late", w_big[1:], m_big[1:], v_big[1:])
    ps_in, landed_in = _exchange_wait(ins, inr, in_thru, inland_thru, late_out[-1][3], "in")
    big_out = finish(names[:1], own_part(landed_in, ps_in), "in", *[padded(t)[:1] for t in (w_big, m_big, v_big)])
    big_out += late_out
    big_out = [[big_out[j][i][:, :642] if j == 0 else big_out[j][i] for j in range(6)] for i in range(4)]

    view = lambda t: t.reshape(t.shape[-3:]) if t.ndim == 4 else t.reshape(t.shape[-2:])
    views = lambda d: {k: view(d[k]) for k in SMALL_NAMES}
    me = 4 * lax.axis_index("x") + 2 * lax.axis_index("y") + c
    ss, sr, sthru, _ = tail["small"]
    vec, w8, lv, lw = _small_wait(ss, sr, sthru, big_out[0][3])
    all_v = lax.dynamic_update_slice(lv, vec[None], (me, 0, 0))
    all_w = lax.dynamic_update_slice(lw, w8[None], (me, 0, 0, 0))
    loss11, res_s = _small_update(all_v, all_w, views(small), views(m_small), views(v_small))
    loss = loss11[0, 0]

    def small_out(i, name):
        return res_s[name][i].reshape(small[name].shape)

    order = ["w_in", "f_bias", "sg_ln_g", "sg_ln_b", "sg_w", "sg_b", "att_out_g", "sg_out_g", "w_out",
             "pre_mix_g", "post_mix_g", "pre_ffn_g", "post_ffn_g", "w_ff1", "w_ff2", "ple_w", "ple_gate_w",
             "ple_gate_b"]
    big_idx = {"w_in": 0, "w_out": 1, "w_ff1": 2, "w_ff2": 3, "ple_w": 4, "ple_gate_w": 5}
    outs = [loss, dx[None]]
    for i in range(4):
        for name in order:
            if name in big_idx:
                outs.append(big_out[i][big_idx[name]][None])
            else:
                outs.append(small_out(i, name))
    return tuple(outs)
```

```python
import math

import jax
import jax.numpy as jnp
from jax import lax
from jax.experimental import pallas as pl
from jax.experimental.pallas import tpu as pltpu

F32 = jnp.float32
BF16 = jnp.bfloat16
MESH = pl.DeviceIdType.MESH

D = 1024
DH = 64
NH = 8
AW = 512
CH = 128
DFF = 4096
ZW = 5 * AW + 128
EPS = 1e-6
NEG = -1e30
MASKED = -2e30

TM = 256
TQ = 256

LR, B1, B2, AEPS, WD, STEP = 0.001, 0.9, 0.999, 1e-08, 0.01, 10
BC1 = 1.0 - B1 ** STEP
BC2 = 1.0 - B2 ** STEP

VEC_NAMES = ("f_bias", "sg_ln_g", "sg_ln_b", "sg_b", "att_out_g", "sg_out_g", "pre_mix_g",
             "post_mix_g", "pre_ffn_g", "post_ffn_g", "ple_gate_b")


def _dot(a, b):
    return jnp.dot(a, b, preferred_element_type=F32)


def _dot_nt(a, b):
    return lax.dot_general(a, b, (((1,), (1,)), ((), ())), preferred_element_type=F32)


def _dot_tn(a, b):
    return lax.dot_general(a, b, (((0,), (0,)), ((), ())), preferred_element_type=F32)


def _split3(x):
    h = x.astype(BF16)
    r = x - h.astype(F32)
    m = r.astype(BF16)
    l = (r - m.astype(F32)).astype(BF16)
    return h, m, l


def _dot01(sel, x):
    h, m, l = _split3(x)
    return _dot(sel, h) + _dot(sel, m) + _dot(sel, l)


def _dot01_r(x, sel):
    h, m, l = _split3(x)
    return _dot(h, sel) + _dot(m, sel) + _dot(l, sel)


def _dot01_tn(x, sel):
    h, m, l = _split3(x)
    return _dot_tn(h, sel) + _dot_tn(m, sel) + _dot_tn(l, sel)


def _rs(x, n):
    return lax.rsqrt(jnp.sum(x * x, axis=-1, keepdims=True) * (1.0 / n) + EPS)


def _rms_bwd(dn, x, rs, g, n):
    w = dn * g
    dx = rs * w - x * ((rs * rs * rs) * (1.0 / n) * jnp.sum(w * x, axis=-1, keepdims=True))
    return dx, jnp.sum(dn * x * rs, axis=0, keepdims=True)


_GC = math.sqrt(2.0 / math.pi)


def _gelu(x):
    t = jnp.tanh(_GC * (x + 0.044715 * x * x * x))
    return 0.5 * x * (1.0 + t), t


def _gelu_grad(x, t):
    return 0.5 * (1.0 + t) + 0.5 * x * (1.0 - t * t) * (_GC * (1.0 + 3.0 * 0.044715 * x * x))


def _params(vmem_mb, sem=("arbitrary",)):
    return pltpu.CompilerParams(dimension_semantics=sem, vmem_limit_bytes=vmem_mb * 1024 * 1024)


def _row_call(name, body, T, tm, tiled, resident, outs, accs, scratch=(), reverse=False, vmem_mb=48):
    nt = T // tm
    n_t, n_r, n_o, n_a = len(tiled), len(resident), len(outs), len(accs)

    def kern(*refs):
        t_refs = refs[:n_t]
        r_hbm = refs[n_t:n_t + n_r]
        o_refs = refs[n_t + n_r:n_t + n_r + n_o]
        a_refs = refs[n_t + n_r + n_o:n_t + n_r + n_o + n_a]
        r_vmem = refs[n_t + n_r + n_o + n_a:n_t + 2 * n_r + n_o + n_a]
        s_refs = refs[n_t + 2 * n_r + n_o + n_a:]

        @pl.when(pl.program_id(0) == 0)
        def _():
            for h, v in zip(r_hbm, r_vmem):
                pltpu.sync_copy(h, v)
            for a in a_refs + s_refs:
                a[...] = jnp.zeros(a.shape, a.dtype)

        body(t_refs, r_vmem, o_refs, a_refs, s_refs)

    if reverse:
        idx = lambda i: (nt - 1 - i, 0)
        idx_t = lambda i: (nt - 1 - i, 0, 0)
    else:
        idx = lambda i: (i, 0)
        idx_t = lambda i: (i, 0, 0)
    arrays, in_specs = [], []
    for a in tiled:
        if isinstance(a, tuple):
            arrays.append(a[0])
            in_specs.append(pl.BlockSpec((None, a[0].shape[1], tm), idx_t))
        else:
            arrays.append(a)
            in_specs.append(pl.BlockSpec((tm, a.shape[1]), idx))
    in_specs += [pl.BlockSpec(memory_space=pl.ANY) for _ in resident]
    out_shape, out_specs = [], []
    for o in outs:
        if len(o) == 3:
            out_shape.append(jax.ShapeDtypeStruct((nt, o[0], tm), o[1]))
            out_specs.append(pl.BlockSpec((None, o[0], tm), idx_t))
        else:
            out_shape.append(jax.ShapeDtypeStruct((T, o[0]), o[1]))
            out_specs.append(pl.BlockSpec((tm, o[0]), idx))
    out_shape += [jax.ShapeDtypeStruct(s, F32) for s in accs]
    out_specs += [pl.BlockSpec(s, lambda i, n=len(s): (0,) * n) for s in accs]
    scratch_shapes = [pltpu.VMEM(r.shape, r.dtype) for r in resident]
    scratch_shapes += [pltpu.VMEM(s, F32) for s in scratch]
    return pl.pallas_call(
        kern, name=name, grid=(nt,), in_specs=in_specs, out_specs=out_specs, out_shape=out_shape,
        scratch_shapes=scratch_shapes, compiler_params=_params(vmem_mb),
    )(*arrays, *resident)


def _sg_forward(zu, zv, wm_ref, bsg, lng, lnb, mixed_ref, tm):
    gu, tu = _gelu(zu)
    vg, tv = _gelu(zv)
    mu = jnp.sum(vg, axis=-1, keepdims=True) * (1.0 / AW)
    xc = vg - mu
    rstd = lax.rsqrt(jnp.sum(xc * xc, axis=-1, keepdims=True) * (1.0 / AW) + EPS)
    xhat = xc * rstd
    vvb = (xhat * lng + lnb).astype(BF16)
    lane = lax.broadcasted_iota(jnp.int32, (CH, 128), 1)
    for c in range(tm // CH):
        for j in range(4):
            blk = vvb[c * CH:(c + 1) * CH, j * 128:(j + 1) * 128]
            m0 = _dot(wm_ref[2 * j], blk)
            m1 = _dot(wm_ref[2 * j + 1], blk)
            mixed_ref[c * CH:(c + 1) * CH, j * 128:(j + 1) * 128] = (
                jnp.where(lane < DH, m0, m1) + bsg[:, j * 128:(j + 1) * 128])
    return gu, tu, tv, xhat, rstd, vvb, mixed_ref[...]


def _head_consts():
    src = jnp.arange(AW)
    dst = (src // DH) * 128 + src % DH
    wide = jnp.arange(NH * 128)
    expand = (dst[:, None] == wide[None, :]).astype(BF16)
    heads = jnp.arange(128)
    pieces = jnp.stack([((heads[:, None] * 128 + DH + i == wide[None, :]) & (heads[:, None] < NH)).astype(BF16)
                        for i in range(3)])
    spare = wide % 128 - DH
    qconst = jnp.where((spare >= 0) & (spare < 3), -1.0, 0.0).astype(F32)[None, :]
    one64 = jnp.where(spare == 0, 1.0, 0.0).astype(F32)[None, :]
    one67 = jnp.where(spare == 3, 1.0, 0.0).astype(F32)[None, :]
    pick64 = ((wide[:, None] == heads[None, :] * 128 + DH) & (heads[None, :] < NH)).astype(BF16)
    pick67 = ((wide[:, None] == heads[None, :] * 128 + DH + 3) & (heads[None, :] < NH)).astype(BF16)
    return expand, expand.T, pieces, qconst, one64, one67, pick64, pick67


def _masked_sg_w(sg_w):
    r = lax.broadcasted_iota(jnp.int32, (CH, CH), 0)
    c = lax.broadcasted_iota(jnp.int32, (CH, CH), 1)
    return jnp.where((c <= r)[None], sg_w, 0.0)


def _pre_attn_fwd(x, gpre, win, fbias, lng, lnb, wm, bsg, gsg, expand, pieces, qconst, kconst, vconst):
    T = x.shape[0]
    tm = TM

    def body(t, r, o, a, s):
        (x_ref,) = t
        gpre_r, win_r, fb_r, lng_r, lnb_r, wm_r, bsg_r, gsg_r, ex_r, pc_r, qc_r, kc_r, vc_r = r
        a_o, flog_o, zuv_o, ysgn_o, q8_o, k8_o, v8_o = o
        carry_ref, mixed_ref = s
        xv = x_ref[...]
        av = (xv * _rs(xv, D) * gpre_r[...]).astype(BF16)
        a_o[...] = av
        z = _dot(av, win_r[...])
        zu = z[:, 3 * AW:4 * AW]
        zv = z[:, 4 * AW:5 * AW]
        zuv_o[:, 0:AW] = zu
        zuv_o[:, AW:2 * AW] = zv
        zf = z[:, 5 * AW:] + fb_r[...]
        flog_o[...] = zf
        lane = lax.broadcasted_iota(jnp.int32, (tm, 128), 1)
        logf = jnp.where(lane < NH, jnp.minimum(zf, 0.0) - jnp.log(1.0 + jnp.exp(-jnp.abs(zf))), 0.0)
        rr = lax.broadcasted_iota(jnp.int32, (tm, tm), 0)
        cc = lax.broadcasted_iota(jnp.int32, (tm, tm), 1)
        tri = (cc <= rr).astype(BF16)
        cum = _dot01(tri, logf) + carry_ref[...]
        carry_ref[...] = cum[tm - 1:tm, :]
        ex = ex_r[...]
        q8_o[...] = (_dot((z[:, 0:AW] * (DH ** -0.5)).astype(BF16), ex) + qc_r[...]).astype(BF16)
        ch, cm, cl = _split3(cum)
        k8_o[...] = (_dot(z[:, AW:2 * AW].astype(BF16), ex) + _dot(ch, pc_r[0]) + _dot(cm, pc_r[1])
                     + _dot(cl, pc_r[2]) + kc_r[...]).astype(BF16)
        v8_o[...] = (_dot(z[:, 2 * AW:3 * AW].astype(BF16), ex) + vc_r[...]).astype(BF16)
        gu, _, _, _, _, _, mixed = _sg_forward(zu, zv, wm_r, bsg_r[...], lng_r[...], lnb_r[...], mixed_ref, tm)
        ysg = gu * mixed
        ysgn_o[...] = (ysg * _rs(ysg, AW) * gsg_r[...]).astype(BF16)

    return _row_call(
        "pre_attn_fwd", body, T, tm, [x],
        [gpre, win, fbias, lng, lnb, wm, bsg, gsg, expand, pieces, qconst, kconst, vconst],
        [(D, BF16), (128, F32), (2 * AW, F32), (AW, BF16), (NH * 128, BF16), (NH * 128, BF16), (NH * 128, BF16)], [],
        scratch=[(1, 128), (tm, AW)], vmem_mb=48)


def _flash_fwd(qt8, k8, vt8, sel):
    T = k8.shape[0]
    nq = T // TQ

    def body(qt_ref, k_ref, vt_ref, sel_ref, o_ref, l_ref, u_scr, p_scr):
        qi = pl.program_id(1)
        qts = (qt_ref[0:128, :], qt_ref[128:256, :])
        dmat = (lax.broadcasted_iota(jnp.int32, (TQ, TQ), 0) - lax.broadcasted_iota(jnp.int32, (TQ, TQ), 1))
        u_scr[1] = jnp.full((2, TQ, TQ), MASKED, F32)
        p_scr[...] = jnp.zeros(p_scr.shape, BF16)

        def sub(t, carry, sc, sb, masked):
            blk_c = jnp.clip(t - 2, 0, qi)
            off_a = pl.multiple_of(jnp.minimum(t, qi) * TQ, TQ)
            new = []
            for j in (0, 1):
                m, al, acc = carry[j]
                acc = al * acc + _dot(vt_ref[blk_c, j * 128:(j + 1) * 128, :], p_scr[sc, j])
                m_new = jnp.maximum(m, jnp.max(u_scr[sb, j], axis=0, keepdims=True))
                p_scr[sb, j] = jnp.exp(u_scr[sb, j] - m_new).astype(BF16)
                u = _dot(k_ref[pl.ds(off_a, TQ), j * 128:(j + 1) * 128], qts[j])
                u_scr[sc, j] = jnp.where(dmat <= (qi - t) * TQ, u, MASKED) if masked else u
                new.append((m_new, jnp.exp(m - m_new), acc))
            return tuple(new)

        def pair(t2, carry, masked):
            return sub(2 * t2 + 1, sub(2 * t2, carry, 0, 1, masked), 1, 0, masked)

        init = tuple((jnp.full((1, TQ), NEG, F32), jnp.ones((1, TQ), F32), jnp.zeros((128, TQ), F32))
                     for _ in (0, 1))
        carry = lax.fori_loop(0, qi // 2, lambda t2, cr: pair(t2, cr, False), init)
        (m0, _, a0), (m1, _, a1) = pair(qi // 2 + 1, pair(qi // 2, carry, True), True)
        l0 = a0[DH:DH + 1, :]
        l1 = a1[DH:DH + 1, :]
        o_ref[...] = _dot01_tn(a0 * (1.0 / l0), sel_ref[0]) + _dot01_tn(a1 * (1.0 / l1), sel_ref[1])
        l_ref[0:1, :] = m0 + jnp.log(l0)
        l_ref[1:2, :] = m1 + jnp.log(l1)
        l_ref[2:8, :] = jnp.zeros((6, TQ), F32)

    return pl.pallas_call(
        body, name="flash_fwd", grid=(4, nq),
        in_specs=[pl.BlockSpec((None, 256, TQ), lambda h, i: (i, h, 0)),
                  pl.BlockSpec((T, 256), lambda h, i: (0, h)),
                  pl.BlockSpec((nq, 256, TQ), lambda h, i: (0, h, 0)),
                  pl.BlockSpec((2, 128, 128), lambda h, i: (0, 0, 0))],
        out_specs=[pl.BlockSpec((TQ, 128), lambda h, i: (i, h)),
                   pl.BlockSpec((None, 8, TQ), lambda h, i: (h, 0, i))],
        out_shape=[jax.ShapeDtypeStruct((T, AW), F32), jax.ShapeDtypeStruct((4, 8, T), F32)],
        scratch_shapes=[pltpu.VMEM((2, 2, TQ, TQ), F32), pltpu.VMEM((2, 2, TQ, TQ), BF16)],
        compiler_params=_params(40, ("arbitrary", "arbitrary")),
    )(qt8, k8, vt8, sel)


def _flash_bwd(q8, qt8, k8, v8, do8, dot8, lse, dlt):
    T = q8.shape[0]
    nk = T // TQ

    def body(q_ref, qt_ref, k_ref, v_ref, do_ref, dot_ref, l_ref, d_ref, dqt_ref, dk_ref, dv_ref,
             u_scr, dp_scr, p_scr, ds_scr):
        kb = pl.program_id(1)
        n = nk - kb

        @pl.when(kb == 0)
        def _():
            dqt_ref[...] = jnp.zeros(dqt_ref.shape, F32)

        dk_ref[...] = jnp.zeros(dk_ref.shape, F32)
        dv_ref[...] = jnp.zeros(dv_ref.shape, F32)
        u_scr[1] = jnp.full((2, TQ, TQ), MASKED, F32)
        dp_scr[1] = jnp.zeros((2, TQ, TQ), F32)
        p_scr[...] = jnp.zeros(p_scr.shape, BF16)
        ds_scr[...] = jnp.zeros(ds_scr.shape, BF16)
        dmat = (lax.broadcasted_iota(jnp.int32, (TQ, TQ), 0) - lax.broadcasted_iota(jnp.int32, (TQ, TQ), 1))
        ks = (k_ref[:, 0:128], k_ref[:, 128:256])
        vs = (v_ref[:, 0:128], v_ref[:, 128:256])

        def sub(t, sc, sb):
            blk_a = kb + jnp.minimum(t, n - 1)
            blk_c = kb + jnp.clip(t - 2, 0, n - 1)
            off_b = pl.multiple_of((kb + jnp.clip(t - 1, 0, n - 1)) * TQ, TQ)
            off_c = pl.multiple_of(blk_c * TQ, TQ)
            lim = jnp.where(t < n, t * TQ, -TQ)
            for j in (0, 1):
                hl = slice(j * 128, (j + 1) * 128)
                dqt_ref[blk_c, hl, :] += _dot_tn(ks[j], ds_scr[sc, j])
                dk_ref[:, hl] += _dot(ds_scr[sc, j], q_ref[pl.ds(off_c, TQ), hl])
                dv_ref[:, hl] += _dot(p_scr[sc, j], do_ref[pl.ds(off_c, TQ), hl])
                p = jnp.exp(u_scr[sb, j] - l_ref[j:j + 1, pl.ds(off_b, TQ)])
                p_scr[sb, j] = p.astype(BF16)
                ds_scr[sb, j] = (p * (dp_scr[sb, j] - d_ref[j:j + 1, pl.ds(off_b, TQ)])).astype(BF16)
                u_scr[sc, j] = jnp.where(dmat <= lim, _dot(ks[j], qt_ref[blk_a, hl, :]), MASKED)
                dp_scr[sc, j] = _dot(vs[j], dot_ref[blk_a, hl, :])

        def it(t2, carry):
            sub(2 * t2, 0, 1)
            sub(2 * t2 + 1, 1, 0)
            return carry

        lax.fori_loop(0, n // 2 + 1, it, 0)

        @pl.when(n % 2 == 1)
        def _():
            sub(n + 1, 0, 1)

    return pl.pallas_call(
        body, name="flash_bwd", grid=(4, nk),
        in_specs=[pl.BlockSpec((T, 256), lambda h, i: (0, h)),
                  pl.BlockSpec((nk, 256, TQ), lambda h, i: (0, h, 0)),
                  pl.BlockSpec((TQ, 256), lambda h, i: (i, h)),
                  pl.BlockSpec((TQ, 256), lambda h, i: (i, h)),
                  pl.BlockSpec((T, 256), lambda h, i: (0, h)),
                  pl.BlockSpec((nk, 256, TQ), lambda h, i: (0, h, 0)),
                  pl.BlockSpec((None, 8, T), lambda h, i: (h, 0, 0)),
                  pl.BlockSpec((None, 8, T), lambda h, i: (h, 0, 0))],
        out_specs=[pl.BlockSpec((nk, 256, TQ), lambda h, i: (0, h, 0)),
                   pl.BlockSpec((TQ, 256), lambda h, i: (i, h)),
                   pl.BlockSpec((TQ, 256), lambda h, i: (i, h))],
        out_shape=[jax.ShapeDtypeStruct((nk, NH * 128, TQ), F32), jax.ShapeDtypeStruct((T, NH * 128), F32),
                   jax.ShapeDtypeStruct((T, NH * 128), F32)],
        scratch_shapes=[pltpu.VMEM((2, 2, TQ, TQ), F32), pltpu.VMEM((2, 2, TQ, TQ), F32),
                        pltpu.VMEM((2, 2, TQ, TQ), BF16), pltpu.VMEM((2, 2, TQ, TQ), BF16)],
        compiler_params=_params(56, ("arbitrary", "arbitrary")),
    )(q8, qt8, k8, v8, do8, dot8, lse, dlt)


def _tail_fwd1(x, yatt, ysgn, gatt, wout, gpm, gpf, w1):
    T = x.shape[0]

    def body(t, r, o, a, s):
        x_ref, ya_ref, ys_ref = t
        gatt_r, wout_r, gpm_r, gpf_r, w1_r = r
        y_o, o_o, h1_o, c2_o, s_o, rr_o = o
        ya = ya_ref[...]
        yan = (ya * _rs(ya, AW) * gatt_r[...]).astype(BF16)
        y_o[:, 0:AW] = yan
        y_o[:, AW:] = ys_ref[...]
        ov = _dot(yan, wout_r[0:AW, :]) + _dot(ys_ref[...], wout_r[AW:, :])
        o_o[...] = ov
        h1 = x_ref[...] + ov * _rs(ov, D) * gpm_r[...]
        h1_o[...] = h1
        c2 = (h1 * _rs(h1, D) * gpf_r[...]).astype(BF16)
        c2_o[...] = c2
        for k in range(4):
            rr = jnp.maximum(_dot(c2, w1_r[k]), 0.0)
            rr_o[:, k * D:(k + 1) * D] = rr.astype(BF16)
            s_o[:, k * D:(k + 1) * D] = (rr * rr).astype(BF16)

    return _row_call(
        "tail_fwd1", body, T, TM, [x, yatt, ysgn], [gatt, wout, gpm, gpf, w1],
        [(D, BF16), (D, F32), (D, F32), (D, BF16), (DFF, BF16), (DFF, BF16)], [], vmem_mb=48)


def _tail_fwd2(sact, h1, p, tgt, w2, gpff, wg, bg, wpe):
    T = h1.shape[0]

    def body(t, r, o, a, s):
        s_ref, h1_ref, p_ref, t_ref = t
        w2_r, gpff_r, wg_r, bg_r, wpe_r = r
        ff_o, h2b_o, de_o, dpre_o, dh2_o = o
        loss_a, dbg_a = a
        ff = _dot(s_ref[...], w2_r[...])
        ff_o[...] = ff
        h2 = h1_ref[...] + ff * _rs(ff, D) * gpff_r[...]
        h2b = h2.astype(BF16)
        h2b_o[...] = h2b
        gate = 1.0 / (1.0 + jnp.exp(-(_dot(h2b, wg_r[...]) + bg_r[...])))
        pb = p_ref[...].astype(BF16)
        e = jnp.concatenate([_dot(pb, wpe_r[k]) for k in range(4)], axis=1)
        diff = h2 + gate * e - t_ref[...]
        loss_a[...] += jnp.sum(diff * diff, axis=0, keepdims=True)
        dh3 = diff * (1.0 / D)
        de_o[...] = (dh3 * gate).astype(BF16)
        dpre = dh3 * e * gate * (1.0 - gate)
        dbg_a[...] += jnp.sum(dpre, axis=0, keepdims=True)
        dpb = dpre.astype(BF16)
        dpre_o[...] = dpb
        dh2_o[...] = dh3 + _dot_nt(dpb, wg_r[...])

    return _row_call(
        "tail_fwd2", body, T, TM, [sact, h1, p, tgt], [w2, gpff, wg, bg, wpe],
        [(D, F32), (D, BF16), (D, BF16), (D, BF16), (D, F32)], [(1, D), (1, D)], vmem_mb=48)


def _tail_bwd(dh2, ff, rr, h1, ov, yatt, w2, w1, wout, gpff, gpf, gpm, gatt, gsel, expand):
    T = dh2.shape[0]

    def body(t, r, o, a, s):
        dh2_ref, ff_ref, rr_ref, h1_ref, o_ref, ya_ref = t
        w2_r, w1_r, wout_r, gpff_r, gpf_r, gpm_r, gatt_r, gsel_r, ex_r = r
        dff_o, dr_o, do_o, do8_o, dlt_o, dysg_o, dh1_o = o
        dgpff_a, dgpf_a, dgpm_a, dgatt_a = a
        dh2v = dh2_ref[...]
        ffv = ff_ref[...]
        dff, dg = _rms_bwd(dh2v, ffv, _rs(ffv, D), gpff_r[...], D)
        dgpff_a[...] += dg
        dffb = dff.astype(BF16)
        dff_o[...] = dffb
        drb = (_dot_nt(dffb, w2_r[...]) * (2.0 * rr_ref[...].astype(F32))).astype(BF16)
        dr_o[...] = drb
        dc2 = _dot_nt(drb[:, 0:D], w1_r[0])
        for k in range(1, 4):
            dc2 = dc2 + _dot_nt(drb[:, k * D:(k + 1) * D], w1_r[k])
        h1v = h1_ref[...]
        d1, dg = _rms_bwd(dc2, h1v, _rs(h1v, D), gpf_r[...], D)
        dgpf_a[...] += dg
        dh1 = dh2v + d1
        dh1_o[...] = dh1
        ovv = o_ref[...]
        dov, dg = _rms_bwd(dh1, ovv, _rs(ovv, D), gpm_r[...], D)
        dgpm_a[...] += dg
        dob = dov.astype(BF16)
        do_o[...] = dob
        dysg_o[...] = _dot_nt(dob, wout_r[AW:, :])
        dyan = _dot_nt(dob, wout_r[0:AW, :])
        ya = ya_ref[...]
        dya, dg = _rms_bwd(dyan, ya, _rs(ya, AW), gatt_r[...], AW)
        dgatt_a[...] += dg
        do8_o[...] = _dot(dya.astype(BF16), ex_r[...]).astype(BF16)
        dlt_o[...] = _dot01_r(dya * ya, gsel_r[...])

    return _row_call(
        "tail_bwd", body, T, TM, [dh2, ff, rr, h1, ov, yatt],
        [w2, w1, wout, gpff, gpf, gpm, gatt, gsel, expand],
        [(D, BF16), (DFF, BF16), (D, BF16), (NH * 128, BF16), (128, F32), (AW, F32), (D, F32)],
        [(1, D), (1, D), (1, D), (1, AW)], vmem_mb=56)


def _pre_attn_bwd(x, dh1, dq8, dk8, dv8, flog, zuv, dysg, gpre, win, lng, lnb, wm, wmt, bsg, gsg, gsel, shrink, pick64, pick67):
    T = x.shape[0]
    tm = TM

    def body(t, r, o, a, s):
        x_ref, dh1_ref, dq_ref, dk_ref, dv_ref, fl_ref, zuv_ref, dys_ref = t
        gpre_r, win_r, lng_r, lnb_r, wm_r, wmt_r, bsg_r, gsg_r, gsel_r, sh_r, p64_r, p67_r = r
        dx_o, dz_o = o
        dgpre_a, dfb_a, dgsg_a, dlng_a, dlnb_a, dws_a, dbs_a, dsb_a = a
        carry_ref, mixed_ref, dvv_ref = s
        dq8v = dq_ref[...]
        dk8v = dk_ref[...]
        dcv = _dot01_r(dq8v, p67_r[...]) + _dot01_r(dk8v, p64_r[...])
        rr = lax.broadcasted_iota(jnp.int32, (tm, tm), 0)
        cc = lax.broadcasted_iota(jnp.int32, (tm, tm), 1)
        triu = (cc >= rr).astype(BF16)
        dlogf = _dot01(triu, dcv) + carry_ref[...]
        carry_ref[...] = dlogf[0:1, :]
        dzf = dlogf * (1.0 / (1.0 + jnp.exp(fl_ref[...])))
        dfb_a[...] += jnp.sum(dzf, axis=0, keepdims=True)
        dz_o[:, 5 * AW:] = dzf.astype(BF16)
        zu = zuv_ref[:, 0:AW]
        zv = zuv_ref[:, AW:]
        gu, tu, tv, xhat, rstd, vvb, mixed = _sg_forward(
            zu, zv, wm_r, bsg_r[...], lng_r[...], lnb_r[...], mixed_ref, tm)
        ysg = gu * mixed
        dysg_n = dys_ref[...]
        dys, dg = _rms_bwd(dysg_n, ysg, _rs(ysg, AW), gsg_r[...], AW)
        dgsg_a[...] += dg
        dgu = dys * mixed
        dmix = dys * gu
        dmb = dmix.astype(BF16)
        lane = lax.broadcasted_iota(jnp.int32, (CH, 128), 1)
        lo = lane < DH
        for c in range(tm // CH):
            rows = slice(c * CH, (c + 1) * CH)
            dbs_a[...] += dmix[rows, :]
            for j in range(4):
                cols = slice(j * 128, (j + 1) * 128)
                dmblk = dmb[rows, cols]
                vblk = vvb[rows, cols]
                d0 = _dot(wmt_r[2 * j], dmblk)
                d1 = _dot(wmt_r[2 * j + 1], dmblk)
                dvv_ref[rows, cols] = jnp.where(lo, d0, d1)
                dws_a[2 * j] += _dot_nt(jnp.where(lo, dmblk, jnp.zeros_like(dmblk)), vblk)
                dws_a[2 * j + 1] += _dot_nt(jnp.where(lo, jnp.zeros_like(dmblk), dmblk), vblk)
        dvv = dvv_ref[...]
        dlng_a[...] += jnp.sum(dvv * xhat, axis=0, keepdims=True)
        dlnb_a[...] += jnp.sum(dvv, axis=0, keepdims=True)
        dxh = dvv * lng_r[...]
        dvg = rstd * (dxh - jnp.sum(dxh, axis=-1, keepdims=True) * (1.0 / AW)
                      - xhat * (jnp.sum(dxh * xhat, axis=-1, keepdims=True) * (1.0 / AW)))
        dz_o[:, 3 * AW:4 * AW] = (dgu * _gelu_grad(zu, tu)).astype(BF16)
        dz_o[:, 4 * AW:5 * AW] = (dvg * _gelu_grad(zv, tv)).astype(BF16)
        dz_o[:, 0:AW] = _dot((dq8v * (DH ** -0.5)).astype(BF16), sh_r[...]).astype(BF16)
        dz_o[:, AW:2 * AW] = _dot(dk8v.astype(BF16), sh_r[...]).astype(BF16)
        dz_o[:, 2 * AW:3 * AW] = _dot(dv_ref[...].astype(BF16), sh_r[...]).astype(BF16)
        da = _dot_nt(dz_o[...], win_r[...])
        xv = x_ref[...]
        dxa, dg = _rms_bwd(da, xv, _rs(xv, D), gpre_r[...], D)
        dgpre_a[...] += dg
        dx_o[...] = dh1_ref[...] + dxa

        @pl.when(pl.program_id(0) == T // tm - 1)
        def _():
            dsb_a[...] = _dot01_r(dbs_a[...], gsel_r[...])

    outs = _row_call(
        "pre_attn_bwd", body, T, tm, [x, dh1, dq8, dk8, dv8, flog, zuv, dysg],
        [gpre, win, lng, lnb, wm, wmt, bsg, gsg, gsel, shrink, pick64, pick67],
        [(D, F32), (ZW, BF16)],
        [(1, D), (1, 128), (1, AW), (1, AW), (1, AW), (8, CH, CH), (CH, AW), (CH, 128)],
        scratch=[(1, 128), (tm, AW), (tm, AW)], reverse=True, vmem_mb=48)
    return outs


def _matmul_tn(name, a, b, tn=512, tt=2048, shards=1, after=None):
    T, K = a.shape
    N = b.shape[1]
    tk = min(K, 1024)
    tn = min(tn, N // shards)
    tt = min(tt, T)
    nj = N // shards // tn

    def body(a_ref, b_ref, *rest):
        o_ref = rest[-1]

        @pl.when(pl.program_id(2) == 0)
        def _():
            o_ref[...] = jnp.zeros(o_ref.shape, F32)

        o_ref[...] += _dot_tn(a_ref[...].astype(BF16), b_ref[...].astype(BF16))

    ordered = after is not None

    if shards == 1:
        out_shape = jax.ShapeDtypeStruct((K, N), F32)
        out_spec = pl.BlockSpec((tk, tn), lambda i, j, t: (i, j))
    else:
        out_shape = jax.ShapeDtypeStruct((shards, K, N // shards), F32)
        out_spec = pl.BlockSpec((None, tk, tn), lambda i, j, t: (j // nj, i, j % nj))
    return pl.pallas_call(
        body, name=name, grid=(K // tk, N // tn, T // tt),
        in_specs=[pl.BlockSpec((tt, tk), lambda i, j, t: (t, i)),
                  pl.BlockSpec((tt, tn), lambda i, j, t: (t, j))] + [pl.BlockSpec(memory_space=pl.ANY)] * ordered,
        out_specs=out_spec, out_shape=out_shape,
        compiler_params=_params(40, ("arbitrary", "arbitrary", "arbitrary")),
    )(a, b, *([after] * ordered))


def _me():
    return lax.axis_index("x"), lax.axis_index("y"), lax.axis_index("c")


HBM_SPEC = pl.BlockSpec(memory_space=pltpu.HBM)


def _gather_weights(mine):
    half = mine.shape[0] // 2

    def body(mine_ref, out_ref, ici_send, ici_recv, d2d_send, d2d_recv):
        x, y, c = _me()
        k_me = 2 * x + y
        chips = [(1 - x, y), (x, 1 - y), (1 - x, 1 - y)]
        my_rows = pl.ds(pl.multiple_of(c * half, 16), half)
        sib_rows = pl.ds(pl.multiple_of((1 - c) * half, 16), half)

        def over_ici(j, k, to):
            src = mine_ref.at[my_rows] if k is None else out_ref.at[k, my_rows]
            return pltpu.make_async_remote_copy(
                src_ref=src, dst_ref=out_ref.at[k_me if k is None else k, my_rows], send_sem=ici_send.at[j],
                recv_sem=ici_recv.at[j], device_id=to, device_id_type=MESH)

        def over_d2d(j, k, rows):
            return pltpu.make_async_remote_copy(
                src_ref=out_ref.at[k, rows], dst_ref=out_ref.at[k, rows], send_sem=d2d_send.at[j],
                recv_sem=d2d_recv.at[j], device_id=(x, y, 1 - c), device_id_type=MESH)

        first = [over_ici(j, None, (cx, cy, c)) for j, (cx, cy) in enumerate(chips)]
        for cp in first:
            cp.start()
        passed = [over_d2d(j, 2 * cx + cy, my_rows) for j, (cx, cy) in enumerate(chips)]
        for j, (cx, cy) in enumerate(chips):
            over_ici(j, 2 * cx + cy, (cx, cy, c)).wait_recv()
            passed[j].start()
        for j, (cx, cy) in enumerate(chips):
            over_d2d(j, 2 * cx + cy, sib_rows).wait_recv()
        for cp in first + passed:
            cp.wait_send()

    return pl.pallas_call(
        body, name="gather_weights", in_specs=[HBM_SPEC], out_specs=HBM_SPEC,
        out_shape=jax.ShapeDtypeStruct((4,) + mine.shape, mine.dtype),
        scratch_shapes=[pltpu.SemaphoreType.DMA((3,)), pltpu.SemaphoreType.DMA((3,)), pltpu.SemaphoreType.DMA((3,)),
                        pltpu.SemaphoreType.DMA((3,))],
    )(mine)


SEM_SPEC = pl.BlockSpec(memory_space=pltpu.SEMAPHORE)
EFFECT = pltpu.SideEffectType.DATAFLOW_SIDE_EFFECTING


def _gather_late_start(mine):
    def body(mine_ref, land_ref, send_sems, recv_sems, mine_thru, land_thru, token):
        x, y, c = _me()
        k_me = 2 * x + y
        for j, (cx, cy) in enumerate([(1 - x, y), (x, 1 - y), (1 - x, 1 - y)]):
            pltpu.make_async_remote_copy(
                src_ref=mine_ref, dst_ref=land_ref.at[k_me], send_sem=send_sems.at[j], recv_sem=recv_sems.at[j],
                device_id=(cx, cy, c), device_id_type=MESH).start()
        token[...] = jnp.zeros(token.shape, F32)

    land = lax.empty((4,) + mine.shape, mine.dtype)
    return pl.pallas_call(
        body, name="gather_late_start",
        out_shape=(pltpu.SemaphoreType.DMA((3,)), pltpu.SemaphoreType.DMA((3,)), pltpu.HBM(mine.shape, mine.dtype),
                   pltpu.HBM(land.shape, land.dtype), jax.ShapeDtypeStruct((8, 128), F32)),
        in_specs=(HBM_SPEC, HBM_SPEC),
        out_specs=(SEM_SPEC, SEM_SPEC, HBM_SPEC, HBM_SPEC, pl.BlockSpec(memory_space=pltpu.VMEM)),
        input_output_aliases={0: 2, 1: 3},
        compiler_params=pltpu.CompilerParams(has_side_effects=EFFECT),
    )(pltpu.with_memory_space_constraint(mine, pltpu.HBM), pltpu.with_memory_space_constraint(land, pltpu.HBM))


def _gather_late_wait(send_sems, recv_sems, mine_thru, land_thru, after):
    def body(mine_ref, land_ref, send_sems, recv_sems, after_ref, mine_dead, got_ref):
        x, y, c = _me()
        for j, (cx, cy) in enumerate([(1 - x, y), (x, 1 - y), (1 - x, 1 - y)]):
            cp = pltpu.make_async_remote_copy(
                src_ref=mine_ref, dst_ref=land_ref.at[2 * cx + cy], send_sem=send_sems.at[j],
                recv_sem=recv_sems.at[j], device_id=(cx, cy, c), device_id_type=MESH)
            cp.wait_send()
            cp.wait_recv()

    return pl.pallas_call(
        body, name="gather_late_wait",
        out_shape=(pltpu.HBM(mine_thru.shape, mine_thru.dtype), pltpu.HBM(land_thru.shape, land_thru.dtype)),
        in_specs=(HBM_SPEC, HBM_SPEC, SEM_SPEC, SEM_SPEC, pl.BlockSpec(memory_space=pl.ANY)),
        out_specs=(HBM_SPEC, HBM_SPEC), input_output_aliases={0: 0, 1: 1},
        compiler_params=pltpu.CompilerParams(has_side_effects=EFFECT),
    )(mine_thru, land_thru, send_sems, recv_sems, after)[1]


def _swap_halves(gs, tag):
    n = len(gs)

    def body(*refs):
        g_refs, got_refs, send_sems, recv_sems = refs[:n], refs[n:2 * n], refs[2 * n], refs[2 * n + 1]
        x, y, c = _me()
        cps = []
        for i, (g_ref, got_ref) in enumerate(zip(g_refs, got_refs)):
            half = g_ref.shape[1] // 2
            theirs = pl.multiple_of((1 - c) * half, 16)
            cps.append(pltpu.make_async_remote_copy(
                src_ref=g_ref.at[:, pl.ds(theirs, half), :], dst_ref=got_ref, send_sem=send_sems.at[i],
                recv_sem=recv_sems.at[i], device_id=(x, y, 1 - c), device_id_type=MESH))
        for cp in cps:
            cp.start()
        for cp in cps:
            cp.wait()

    return pl.pallas_call(
        body, name="swap_halves_" + tag, in_specs=[HBM_SPEC] * n, out_specs=[HBM_SPEC] * n,
        out_shape=[jax.ShapeDtypeStruct((4, g.shape[1] // 2, g.shape[2]), F32) for g in gs],
        scratch_shapes=[pltpu.SemaphoreType.DMA((n,)), pltpu.SemaphoreType.DMA((n,))],
    )(*gs)


def _swap_start(gs, tag):
    n = len(gs)

    def body(*refs):
        g_refs, land_refs, send_sems, recv_sems = refs[:n], refs[n:2 * n], refs[2 * n], refs[2 * n + 1]
        x, y, c = _me()
        for i, (g_ref, land_ref) in enumerate(zip(g_refs, land_refs)):
            half = g_ref.shape[1] // 2
            theirs = pl.multiple_of((1 - c) * half, 16)
            pltpu.make_async_remote_copy(
                src_ref=g_ref.at[:, pl.ds(theirs, half), :], dst_ref=land_ref, send_sem=send_sems.at[i],
                recv_sem=recv_sems.at[i], device_id=(x, y, 1 - c), device_id_type=MESH).start()

    lands = [lax.empty((4, g.shape[1] // 2, g.shape[2]), F32) for g in gs]
    hbm = lambda t: pltpu.HBM(t.shape, t.dtype)
    res = pl.pallas_call(
        body, name="swap_start_" + tag,
        out_shape=(pltpu.SemaphoreType.DMA((n,)), pltpu.SemaphoreType.DMA((n,)), *[hbm(t) for t in gs],
                   *[hbm(t) for t in lands]),
        in_specs=(HBM_SPEC,) * (2 * n), out_specs=(SEM_SPEC, SEM_SPEC) + (HBM_SPEC,) * (2 * n),
        input_output_aliases={i: 2 + i for i in range(2 * n)},
        compiler_params=pltpu.CompilerParams(has_side_effects=EFFECT),
    )(*[pltpu.with_memory_space_constraint(t, pltpu.HBM) for t in list(gs) + lands])
    return res[0], res[1], res[2:2 + n], res[2 + n:2 + 2 * n]


def _swap_wait(send_sems, recv_sems, g_thru, land_thru, after, tag):
    n = len(g_thru)

    def body(*refs):
        g_refs, land_refs, send_sems, recv_sems = refs[:n], refs[n:2 * n], refs[2 * n], refs[2 * n + 1]
        x, y, c = _me()
        for i, (g_ref, land_ref) in enumerate(zip(g_refs, land_refs)):
            half = g_ref.shape[1] // 2
            theirs = pl.multiple_of((1 - c) * half, 16)
            pltpu.make_async_remote_copy(
                src_ref=g_ref.at[:, pl.ds(theirs, half), :], dst_ref=land_ref, send_sem=send_sems.at[i],
                recv_sem=recv_sems.at[i], device_id=(x, y, 1 - c), device_id_type=MESH).wait()

    hbm = lambda t: pltpu.HBM(t.shape, t.dtype)
    res = pl.pallas_call(
        body, name="swap_wait_" + tag, out_shape=tuple(hbm(t) for t in list(g_thru) + list(land_thru)),
        in_specs=(HBM_SPEC,) * (2 * n) + (SEM_SPEC, SEM_SPEC, pl.BlockSpec(memory_space=pl.ANY)),
        out_specs=(HBM_SPEC,) * (2 * n), input_output_aliases={i: i for i in range(2 * n)},
        compiler_params=pltpu.CompilerParams(has_side_effects=EFFECT),
    )(*g_thru, *land_thru, send_sems, recv_sems, after)
    return res[:n], res[n:]


def _pair_sum(name, c1, g, got):
    half, cols = got.shape[1], got.shape[2]

    def body(c_ref, a_ref, b_ref, o_ref):
        o_ref[...] = (a_ref[...] + b_ref[...]).astype(BF16)

    return pl.pallas_call(
        body, name="pair_sum_" + name,
        grid_spec=pltpu.PrefetchScalarGridSpec(
            num_scalar_prefetch=1, grid=(4,),
            in_specs=[pl.BlockSpec((1, half, cols), lambda k, c_ref: (k, c_ref[0], 0)),
                      pl.BlockSpec((1, half, cols), lambda k, c_ref: (k, 0, 0))],
            out_specs=pl.BlockSpec((1, half, cols), lambda k, c_ref: (k, 0, 0))),
        out_shape=jax.ShapeDtypeStruct(got.shape, BF16), compiler_params=_params(32),
    )(c1, g, got)


def _exchange_start(pss, tag):
    n = len(pss)

    def body(*refs):
        ps_refs, land_refs = refs[:n], refs[n:2 * n]
        send_sems, recv_sems = refs[2 * n], refs[2 * n + 1]
        token = refs[4 * n + 2]
        x, y, c = _me()
        k_me = 2 * x + y
        for i, (ps_ref, land_ref) in enumerate(zip(ps_refs, land_refs)):
            for j, (cx, cy) in enumerate([(1 - x, y), (x, 1 - y), (1 - x, 1 - y)]):
                pltpu.make_async_remote_copy(
                    src_ref=ps_ref.at[2 * cx + cy], dst_ref=land_ref.at[k_me], send_sem=send_sems.at[3 * i + j],
                    recv_sem=recv_sems.at[3 * i + j], device_id=(cx, cy, c), device_id_type=MESH).start()
        token[...] = jnp.zeros(token.shape, F32)

    lands = [lax.empty(ps.shape, ps.dtype) for ps in pss]
    hbm = lambda t: pltpu.HBM(t.shape, t.dtype)
    res = pl.pallas_call(
        body, name="exchange_start_" + tag,
        out_shape=(pltpu.SemaphoreType.DMA((3 * n,)), pltpu.SemaphoreType.DMA((3 * n,)), *[hbm(t) for t in pss],
                   *[hbm(t) for t in lands], jax.ShapeDtypeStruct((8, 128), F32)),
        in_specs=(HBM_SPEC,) * (2 * n),
        out_specs=(SEM_SPEC, SEM_SPEC) + (HBM_SPEC,) * (2 * n) + (pl.BlockSpec(memory_space=pltpu.VMEM),),
        input_output_aliases={i: 2 + i for i in range(2 * n)},
        compiler_params=pltpu.CompilerParams(has_side_effects=EFFECT),
    )(*[pltpu.with_memory_space_constraint(t, pltpu.HBM) for t in list(pss) + lands])
    return res[0], res[1], res[2:2 + n], res[2 + n:2 + 2 * n], res[2 + 2 * n]


def _exchange_wait(send_sems, recv_sems, ps_thru, land_thru, after, tag):
    n = len(ps_thru)

    def body(*refs):
        ps_refs, land_refs = refs[:n], refs[n:2 * n]
        send_sems, recv_sems = refs[2 * n], refs[2 * n + 1]
        x, y, c = _me()
        k_me = 2 * x + y
        for i, (ps_ref, land_ref) in enumerate(zip(ps_refs, land_refs)):
            for j, (cx, cy) in enumerate([(1 - x, y), (x, 1 - y), (1 - x, 1 - y)]):
                cp = pltpu.make_async_remote_copy(
                    src_ref=ps_ref.at[k_me], dst_ref=land_ref.at[2 * cx + cy], send_sem=send_sems.at[3 * i + j],
                    recv_sem=recv_sems.at[3 * i + j], device_id=(cx, cy, c), device_id_type=MESH)
                cp.wait_send()
                cp.wait_recv()

    hbm = lambda t: pltpu.HBM(t.shape, t.dtype)
    res = pl.pallas_call(
        body, name="exchange_wait_" + tag,
        out_shape=tuple(hbm(t) for t in list(ps_thru) + list(land_thru)),
        in_specs=(HBM_SPEC,) * (2 * n) + (SEM_SPEC, SEM_SPEC, pl.BlockSpec(memory_space=pl.ANY)),
        out_specs=(HBM_SPEC,) * (2 * n), input_output_aliases={i: i for i in range(2 * n)},
        compiler_params=pltpu.CompilerParams(has_side_effects=EFFECT),
    )(*ps_thru, *land_thru, send_sems, recv_sems, after)
    return res[:n], res[n:]


def _adamw(w, g, m, v):
    m = B1 * m + (1.0 - B1) * g
    v = B2 * v + (1.0 - B2) * (g * g)
    delta = -LR * ((m / BC1) / (jnp.sqrt(v / BC2) + AEPS) + WD * w)
    return delta, m, v


def _reduce_chips(name, parts):
    half, cols = parts.shape[1], parts.shape[2]

    def body(p_ref, o_ref):
        f = lambda k: p_ref[k].astype(F32)
        o_ref[...] = ((f(0) + f(1)) + f(2)) + f(3)

    return pl.pallas_call(
        body, name="reduce_chips_" + name, grid=(1,),
        in_specs=[pl.BlockSpec((4, half, cols), lambda i: (0, 0, 0))],
        out_specs=pl.BlockSpec((half, cols), lambda i: (0, 0)),
        out_shape=jax.ShapeDtypeStruct((half, cols), F32), compiler_params=_params(32),
    )(parts)


def _share_grad(ghs, tag):
    n = len(ghs)

    def body(*refs):
        g_refs, got_refs, send_sems, recv_sems = refs[:n], refs[n:2 * n], refs[2 * n], refs[2 * n + 1]
        x, y, c = _me()
        cps = [pltpu.make_async_remote_copy(
            src_ref=g_ref, dst_ref=got_ref, send_sem=send_sems.at[i], recv_sem=recv_sems.at[i],
            device_id=(x, y, 1 - c), device_id_type=MESH) for i, (g_ref, got_ref) in enumerate(zip(g_refs, got_refs))]
        for cp in cps:
            cp.start()
        for cp in cps:
            cp.wait()

    return pl.pallas_call(
        body, name="share_grad_" + tag, in_specs=[HBM_SPEC] * n, out_specs=[HBM_SPEC] * n,
        out_shape=[jax.ShapeDtypeStruct(g.shape, F32) for g in ghs],
        scratch_shapes=[pltpu.SemaphoreType.DMA((n,)), pltpu.SemaphoreType.DMA((n,))],
    )(*ghs)


def _update(name, c1, gh, got, w, m, v):
    half, cols = gh.shape

    def body(c_ref, gh_ref, got_ref, w_ref, m_ref, v_ref, g_o, d_o, m_o, v_o):
        g = jnp.where(pl.program_id(0) == c_ref[0], gh_ref[...], got_ref[...])
        delta, mn, vn = _adamw(w_ref[...], g, m_ref[...], v_ref[...])
        g_o[...] = g
        d_o[...] = delta
        m_o[...] = mn
        v_o[...] = vn

    same = pl.BlockSpec((half, cols), lambda h, c_ref: (0, 0))
    rows = pl.BlockSpec((half, cols), lambda h, c_ref: (h, 0))
    return pl.pallas_call(
        body, name="update_" + name,
        grid_spec=pltpu.PrefetchScalarGridSpec(
            num_scalar_prefetch=1, grid=(2,), in_specs=[same, same, rows, rows, rows],
            out_specs=[rows, rows, rows, rows]),
        out_shape=[jax.ShapeDtypeStruct(w.shape, F32)] * 4, compiler_params=_params(40),
    )(c1, gh, got, w, m, v)


SMALL_NAMES = ("sg_w",) + VEC_NAMES
VEC_ROWS = 24
VEC_ROW = {"f_bias": 0, "sg_ln_g": 1, "sg_ln_b": 2, "att_out_g": 3, "sg_out_g": 4, "pre_mix_g": 5,
           "post_mix_g": 6, "pre_ffn_g": 7, "sg_b": 8, "post_ffn_g": 16, "ple_gate_b": 17}
LOSS_VEC_ROW = 18


def _small_pack(g, loss_l):
    n = len(SMALL_NAMES)

    def body(*refs):
        g_r = dict(zip(SMALL_NAMES, refs[0:n]))
        loss_r, vec_o, w_o = refs[n:]
        vec_o[...] = jnp.zeros((VEC_ROWS, 1024), F32)
        for name in VEC_NAMES:
            val = g_r[name][...]
            vec_o[pl.ds(VEC_ROW[name], val.shape[0]), pl.ds(0, val.shape[1])] = val
        vec_o[pl.ds(LOSS_VEC_ROW, 1), :] = loss_r[...] * (0.5 / D)
        rr = lax.broadcasted_iota(jnp.int32, (CH, CH), 0)
        cc = lax.broadcasted_iota(jnp.int32, (CH, CH), 1)
        w_o[...] = jnp.where((cc <= rr)[None], g_r["sg_w"][...], 0.0)

    vm = pl.BlockSpec(memory_space=pltpu.VMEM)
    args = [g[k] for k in SMALL_NAMES] + [loss_l]
    return pl.pallas_call(
        body, name="small_pack", in_specs=[vm] * len(args), out_specs=[vm, vm],
        out_shape=[jax.ShapeDtypeStruct((VEC_ROWS, 1024), F32), jax.ShapeDtypeStruct((8, CH, CH), F32)],
    )(*args)


def _small_peers(x, y, c):
    rels = [(rx, ry, rc) for rx in (0, 1) for ry in (0, 1) for rc in (0, 1)][1:]
    return [((x + rx) % 2, (y + ry) % 2, (c + rc) % 2) for rx, ry, rc in rels]


def _small_start(vec, w8):
    def body(vec_ref, w_ref, lv_ref, lw_ref, send_sems, recv_sems, vec_thru, w_thru, lv_thru, lw_thru, token):
        x, y, c = _me()
        me = 4 * x + 2 * y + c
        for j, to in enumerate(_small_peers(x, y, c)):
            for i, (src, land) in enumerate(((vec_ref, lv_ref), (w_ref, lw_ref))):
                pltpu.make_async_remote_copy(
                    src_ref=src, dst_ref=land.at[me], send_sem=send_sems.at[2 * j + i],
                    recv_sem=recv_sems.at[2 * j + i], device_id=to, device_id_type=MESH).start()
        token[...] = jnp.zeros(token.shape, F32)

    ops = [vec, w8, lax.empty((8,) + vec.shape, F32), lax.empty((8,) + w8.shape, F32)]
    hbm = lambda t: pltpu.HBM(t.shape, t.dtype)
    res = pl.pallas_call(
        body, name="small_start",
        out_shape=(pltpu.SemaphoreType.DMA((14,)), pltpu.SemaphoreType.DMA((14,)), *[hbm(t) for t in ops],
                   jax.ShapeDtypeStruct((8, 128), F32)),
        in_specs=(HBM_SPEC,) * 4,
        out_specs=(SEM_SPEC, SEM_SPEC) + (HBM_SPEC,) * 4 + (pl.BlockSpec(memory_space=pltpu.VMEM),),
        input_output_aliases={i: 2 + i for i in range(4)},
        compiler_params=pltpu.CompilerParams(has_side_effects=EFFECT),
    )(*[pltpu.with_memory_space_constraint(t, pltpu.HBM) for t in ops])
    return res[0], res[1], res[2:6], res[6]


def _small_wait(send_sems, recv_sems, thru, after):
    def body(vec_ref, w_ref, lv_ref, lw_ref, send_sems, recv_sems, after_ref, vec_o, w_o, lv_o, lw_o):
        x, y, c = _me()
        for j, (px, py, pc) in enumerate(_small_peers(x, y, c)):
            for i, (src, land) in enumerate(((vec_ref, lv_ref), (w_ref, lw_ref))):
                cp = pltpu.make_async_remote_copy(
                    src_ref=src, dst_ref=land.at[4 * px + 2 * py + pc], send_sem=send_sems.at[2 * j + i],
                    recv_sem=recv_sems.at[2 * j + i], device_id=(px, py, pc), device_id_type=MESH)
                cp.wait_send()
                cp.wait_recv()

    hbm = lambda t: pltpu.HBM(t.shape, t.dtype)
    return pl.pallas_call(
        body, name="small_wait", out_shape=tuple(hbm(t) for t in thru),
        in_specs=(HBM_SPEC,) * 4 + (SEM_SPEC, SEM_SPEC, pl.BlockSpec(memory_space=pl.ANY)),
        out_specs=(HBM_SPEC,) * 4, input_output_aliases={i: i for i in range(4)},
        compiler_params=pltpu.CompilerParams(has_side_effects=EFFECT),
    )(*thru, send_sems, recv_sems, after)


def _small_update(all_v, all_w, w, m, v):
    n = len(SMALL_NAMES)

    def body(*refs):
        allv_r, allw_r = refs[0], refs[1]
        tot_v = allv_r[0]
        tot_w = allw_r[0]
        for d in range(1, 8):
            tot_v = tot_v + allv_r[d]
            tot_w = tot_w + allw_r[d]
        w_r = dict(zip(SMALL_NAMES, refs[2:2 + n]))
        m_r = dict(zip(SMALL_NAMES, refs[2 + n:2 + 2 * n]))
        v_r = dict(zip(SMALL_NAMES, refs[2 + 2 * n:2 + 3 * n]))
        loss_o = refs[2 + 3 * n]
        outs = refs[3 + 3 * n:]
        loss_o[...] = jnp.sum(tot_v[LOSS_VEC_ROW:LOSS_VEC_ROW + 1, :], axis=-1, keepdims=True) + jnp.zeros((1, 128), F32)
        for i, name in enumerate(SMALL_NAMES):
            if name == "sg_w":
                gt = tot_w
            else:
                rows, width = w_r[name].shape
                gt = tot_v[VEC_ROW[name]:VEC_ROW[name] + rows, 0:width]
            delta, mn, vn = _adamw(w_r[name][...], gt, m_r[name][...], v_r[name][...])
            outs[4 * i][...] = gt
            outs[4 * i + 1][...] = delta
            outs[4 * i + 2][...] = mn
            outs[4 * i + 3][...] = vn

    vm = pl.BlockSpec(memory_space=pltpu.VMEM)
    args = [all_v, all_w] + [d[k] for d in (w, m, v) for k in SMALL_NAMES]
    out_shape = [jax.ShapeDtypeStruct((1, 128), F32)]
    out_shape += [jax.ShapeDtypeStruct(w[k].shape, F32) for k in SMALL_NAMES for _ in range(4)]
    res = pl.pallas_call(
        body, name="small_update", in_specs=[vm] * len(args), out_specs=[vm] * len(out_shape), out_shape=out_shape,
        compiler_params=pltpu.CompilerParams(vmem_limit_bytes=32 * 1024 * 1024),
    )(*args)
    return res[0], {k: res[1 + 4 * i:5 + 4 * i] for i, k in enumerate(SMALL_NAMES)}


def _win_kernel_order(gathered):
    w_in = jnp.concatenate([gathered[k].reshape(D, 768)[:, :642] for k in range(4)], axis=1)
    return jnp.concatenate([w_in[:, :3 * AW], w_in[:, 3 * AW + NH:], w_in[:, 3 * AW:3 * AW + NH],
                            jnp.zeros((D, 128 - NH), w_in.dtype)], axis=1)


LATE_ROWS = 256 + 1024 + 1024 + 64 + 256


def _pack_late(w_out, w1, w2, plew, wg):
    return jnp.concatenate([w_out, w1, w2, plew.reshape(64, 1024), wg], axis=0)


def _unpack_late(gathered):
    return (gathered[:, 0:256].reshape(D, D), gathered[:, 256:1280], gathered[:, 1280:2304].reshape(DFF, D),
            gathered[:, 2304:2368].reshape(4, 256, 256), gathered[:, 2368:2624].reshape(D, D))


def _local_step(x, p, tgt, win_k, late_weights, token, on_ff_grads, on_tail_grads, on_small_grads, small):
    T = x.shape[0]
    row = lambda n: small[n].reshape(1, -1)
    fbias = jnp.pad(row("f_bias"), ((0, 0), (0, 128 - NH))) + token[0:1, :]
    wm = _masked_sg_w(small["sg_w"].reshape(8, CH, CH))
    wmb = wm.astype(BF16)
    wmt = jnp.swapaxes(wm, 1, 2).astype(BF16)
    bsg = jnp.repeat(small["sg_b"].reshape(8, CH).T, DH, axis=1)
    ln_g, ln_b, gsg, gatt = row("sg_ln_g"), row("sg_ln_b"), row("sg_out_g"), row("att_out_g")
    gpre, gpm, gpf, gpff, bg = row("pre_mix_g"), row("post_mix_g"), row("pre_ffn_g"), row("post_ffn_g"), row("ple_gate_b")
    gsel = (jnp.arange(AW)[:, None] // DH == jnp.arange(128)[None, :]).astype(BF16)

    expand, shrink, pieces, qconst, one64, one67, pick64, pick67 = _head_consts()
    a, flog, zuv, ysgn, q8, k8, v8 = _pre_attn_fwd(
        x, gpre, win_k, fbias, ln_g, ln_b, wmb, bsg, gsg, expand, pieces, qconst, one67, one64)

    slabs = lambda t: jnp.swapaxes(t.reshape(T // TQ, TQ, NH * 128), 1, 2)
    qt8 = slabs(q8)
    lanes = jnp.arange(128)
    sel = jnp.stack([((lanes[:, None] == lanes[None, :] - DH * j) & (lanes[:, None] < DH)).astype(BF16)
                     for j in (0, 1)])

    yatt, lse = _flash_fwd(qt8, k8, slabs(v8), sel)
    wout, w1, w2, plew, wg = late_weights(lse)
    y, ov, h1, c2, sact, rr = _tail_fwd1(x, yatt, ysgn, gatt, wout, gpm, gpf, w1)
    ff, h2b, de, dpre, dh2, loss_l, dbg = _tail_fwd2(sact, h1, p, tgt, w2, gpff, wg, bg, plew)
    dff, dr, do, do8, dlt, dysg, dh1, dgpff, dgpf, dgpm, dgatt = _tail_bwd(
        dh2, ff, rr, h1, ov, yatt, w2, w1, wout, gpff, gpf, gpm, gatt, gsel, expand)
    dw1 = _matmul_tn("grad_w_ff1", c2, dr, shards=4)
    dw2 = _matmul_tn("grad_w_ff2", sact, dff)
    ff_sent = on_ff_grads(dw1, dw2)
    dwout = _matmul_tn("grad_w_out", y, do, after=ff_sent)
    dwg = _matmul_tn("grad_ple_gate_w", h2b, dpre, after=dwout)
    dplew = _matmul_tn("grad_ple_w", p, de, tn=256, shards=4, after=dwg)
    tail_token = on_tail_grads((dwout, dplew, dwg))
    dlt4 = jnp.pad(dlt[:, :NH].T.reshape(4, 2, T), ((0, 0), (0, 6), (0, 0))) + tail_token[0, 0]
    dqt, dk8, dv8 = _flash_bwd(q8, qt8, k8, v8, do8, slabs(do8), lse, dlt4)
    dx, dz, dgpre, dfb, dgsg, dlng, dlnb, dws, _, dsbt = _pre_attn_bwd(
        x, dh1, jnp.swapaxes(dqt, 1, 2).reshape(T, NH * 128), dk8, dv8, flog, zuv, dysg,
        gpre, win_k, ln_g, ln_b, wmb, wmt, bsg, gsg, gsel, shrink, pick64, pick67)


    dsb = dsbt[:, :8].T
    gsmall = {"sg_w": dws, "f_bias": dfb, "sg_ln_g": dlng, "sg_ln_b": dlnb, "sg_b": dsb,
              "att_out_g": dgatt, "sg_out_g": dgsg, "pre_mix_g": dgpre, "post_mix_g": dgpm, "pre_ffn_g": dgpf,
              "post_ffn_g": dgpff, "ple_gate_b": dbg}
    on_small_grads(gsmall, loss_l)
    dwin_k = _matmul_tn("grad_w_in", a, dz, tn=384)
    return loss_l, dx, dwin_k, gsmall


def kernel(x, p, w_in, f_bias, sg_ln_g, sg_ln_b, sg_w, sg_b, att_out_g, sg_out_g, w_out, pre_mix_g, post_mix_g, pre_ffn_g, post_ffn_g, w_ff1, w_ff2, ple_w, ple_gate_w, ple_gate_b, loss_target, m_w_in, m_f_bias, m_sg_ln_g, m_sg_ln_b, m_sg_w, m_sg_b, m_att_out_g, m_sg_out_g, m_w_out, m_pre_mix_g, m_post_mix_g, m_pre_ffn_g, m_post_ffn_g, m_w_ff1, m_w_ff2, m_ple_w, m_ple_gate_w, m_ple_gate_b, v_w_in, v_f_bias, v_sg_ln_g, v_sg_ln_b, v_sg_w, v_sg_b, v_att_out_g, v_sg_out_g, v_w_out, v_pre_mix_g, v_post_mix_g, v_pre_ffn_g, v_post_ffn_g, v_w_ff1, v_w_ff2, v_ple_w, v_ple_gate_w, v_ple_gate_b):
    c = lax.axis_index("c")
    big = lambda t: (t[0][0], t[1][0], t[2][0], t[3][0], t[4][0], t[5][0])
    w_big = big((w_in, w_out, w_ff1, w_ff2, ple_w, ple_gate_w))
    m_big = big((m_w_in, m_w_out, m_w_ff1, m_w_ff2, m_ple_w, m_ple_gate_w))
    v_big = big((v_w_in, v_w_out, v_w_ff1, v_w_ff2, v_ple_w, v_ple_gate_w))
    small = {"sg_w": sg_w, "f_bias": f_bias, "sg_ln_g": sg_ln_g, "sg_ln_b": sg_ln_b, "sg_b": sg_b,
             "att_out_g": att_out_g, "sg_out_g": sg_out_g, "pre_mix_g": pre_mix_g, "post_mix_g": post_mix_g,
             "pre_ffn_g": pre_ffn_g, "post_ffn_g": post_ffn_g, "ple_gate_b": ple_gate_b}
    m_small = {"sg_w": m_sg_w, "f_bias": m_f_bias, "sg_ln_g": m_sg_ln_g, "sg_ln_b": m_sg_ln_b, "sg_b": m_sg_b,
               "att_out_g": m_att_out_g, "sg_out_g": m_sg_out_g, "pre_mix_g": m_pre_mix_g,
               "post_mix_g": m_post_mix_g, "pre_ffn_g": m_pre_ffn_g, "post_ffn_g": m_post_ffn_g,
               "ple_gate_b": m_ple_gate_b}
    v_small = {"sg_w": v_sg_w, "f_bias": v_f_bias, "sg_ln_g": v_sg_ln_g, "sg_ln_b": v_sg_ln_b, "sg_b": v_sg_b,
               "att_out_g": v_att_out_g, "sg_out_g": v_sg_out_g, "pre_mix_g": v_pre_mix_g,
               "post_mix_g": v_post_mix_g, "pre_ffn_g": v_pre_ffn_g, "post_ffn_g": v_post_ffn_g,
               "ple_gate_b": v_ple_gate_b}

    k_me = 2 * lax.axis_index("x") + lax.axis_index("y")
    own_slot = lambda got, mine: lax.dynamic_update_slice(got, mine[None], (k_me, 0, 0))
    late_mine = _pack_late(*w_big[1:]).astype(BF16)
    late = _gather_late_start(late_mine)
    win_mine = jnp.pad(w_big[0], ((0, 0), (0, 768 - 642))).reshape(768, 1024).astype(BF16)
    win_k = _win_kernel_order(own_slot(_gather_weights(win_mine), win_mine))
    late_weights = lambda after: _unpack_late(
        own_slot(_gather_late_wait(late[0], late[1], late[2], late[3], after), late_mine))

    names = ("w_in", "w_out", "w_ff1", "w_ff2", "ple_w", "ple_gate_w")
    c1 = jnp.reshape(c, (1,)).astype(jnp.int32)
    own_part = lambda parts, pss: [lax.dynamic_update_slice(pt, lax.dynamic_slice_in_dim(ps, k_me, 1, 0), (k_me, 0, 0))
                                   for pt, ps in zip(parts, pss)]
    tail = {}

    def on_ff_grads(dw1, dw2):
        tail["swap"] = _swap_start([dw1, dw2.reshape(4, D, D)], "ff")
        return tail["swap"][2][0]

    def on_tail_grads(grads):
        dwout, dplew, dwg = grads
        ws, wr, g_thru, land_thru = tail["swap"]
        (dw1, dw2), (got1, got2) = _swap_wait(ws, wr, g_thru, land_thru, dplew, "ff")
        rest = [dwout.reshape(4, 256, D), dplew, dwg.reshape(4, 256, D)]
        got_out, got_ple, got_gate = _swap_halves(rest, "late")
        gs = [rest[0], dw1, dw2, rest[1], rest[2]]
        gots = [got_out, got1, got2, got_ple, got_gate]
        pss = [_pair_sum(nm, c1, g, got) for nm, g, got in zip(names[1:], gs, gots)]
        tail["xch"] = _exchange_start(pss, "late")
        return tail["xch"][4]

    def on_small_grads(gsmall, loss_l):
        tail["small"] = _small_start(*_small_pack(gsmall, loss_l))

    loss_l, dx, dwin_k, gsmall = _local_step(
        x[0], p[0, 0], loss_target[0], win_k, late_weights, late[4], on_ff_grads, on_tail_grads, on_small_grads,
        small)

    dwin = jnp.concatenate([dwin_k[:, :3 * AW], dwin_k[:, 5 * AW:5 * AW + NH], dwin_k[:, 3 * AW:5 * AW]], axis=1)
    dwin = jnp.pad(jnp.swapaxes(dwin.reshape(D, 4, 642), 0, 1), ((0, 0), (0, 0), (0, 768 - 642)))
    ps_in = [_pair_sum(names[0], c1, dwin, _swap_halves([dwin], "in")[0])]
    ins, inr, in_thru, inland_thru, in_token = _exchange_start(ps_in, "in")
    xs, xr, ps_thru, land_thru, _ = tail["xch"]
    ps_late, landed = _exchange_wait(xs, xr, ps_thru, land_thru, in_token, "late")
    padded = lambda t: (jnp.pad(t[0], ((0, 0), (0, 768 - 642))),) + tuple(t[1:])

    def finish(nms, parts, tag, w, m, v):
        ghs = [_reduce_chips(nm, pt) for nm, pt in zip(nms, parts)]
        got2 = _share_grad(ghs, tag)
        return [_update(nm, c1, gh, g2, wi, mi, vi) for nm, gh, g2, wi, mi, vi in zip(nms, ghs, got2, w, m, v)]

    late_out = finish(names[1:], own_part(landed, ps_late), "late", w_big[1:], m_big[1:], v_big[1:])
    ps_in, landed_in = _exchange_wait(ins, inr, in_thru, inland_thru, late_out[-1][3], "in")
    big_out = finish(names[:1], own_part(landed_in, ps_in), "in", *[padded(t)[:1] for t in (w_big, m_big, v_big)])
    big_out += late_out
    big_out = [[big_out[j][i][:, :642] if j == 0 else big_out[j][i] for j in range(6)] for i in range(4)]

    view = lambda t: t.reshape(t.shape[-3:]) if t.ndim == 4 else t.reshape(t.shape[-2:])
    views = lambda d: {k: view(d[k]) for k in SMALL_NAMES}
    me = 4 * lax.axis_index("x") + 2 * lax.axis_index("y") + c
    ss, sr, sthru, _ = tail["small"]
    vec, w8, lv, lw = _small_wait(ss, sr, sthru, big_out[0][3])
    all_v = lax.dynamic_update_slice(lv, vec[None], (me, 0, 0))
    all_w = lax.dynamic_update_slice(lw, w8[None], (me, 0, 0, 0))
    loss11, res_s = _small_update(all_v, all_w, views(small), views(m_small), views(v_small))
    loss = loss11[0, 0]

    def small_out(i, name):
        return res_s[name][i].reshape(small[name].shape)

    order = ["w_in", "f_bias", "sg_ln_g", "sg_ln_b", "sg_w", "sg_b", "att_out_g", "sg_out_g", "w_out",
             "pre_mix_g", "post_mix_g", "pre_ffn_g", "post_ffn_g", "w_ff1", "w_ff2", "ple_w", "ple_gate_w",
             "ple_gate_b"]
    big_idx = {"w_in": 0, "w_out": 1, "w_ff1": 2, "w_ff2": 3, "ple_w": 4, "ple_gate_w": 5}
    outs = [loss, dx[None]]
    for i in range(4):
        for name in order:
            if name in big_idx:
                outs.append(big_out[i][big_idx[name]][None])
            else:
                outs.append(small_out(i, name))
    return tuple(outs)
```

```python
import math

import jax
import jax.numpy as jnp
from jax import lax
from jax.experimental import pallas as pl
from jax.experimental.pallas import tpu as pltpu

F32 = jnp.float32
BF16 = jnp.bfloat16
MESH = pl.DeviceIdType.MESH

D = 1024
DH = 64
NH = 8
AW = 512
CH = 128
DFF = 4096
ZW = 5 * AW + 128
EPS = 1e-6
NEG = -1e30
MASKED = -2e30

TM = 256
TQ = 256

LR, B1, B2, AEPS, WD, STEP = 0.001, 0.9, 0.999, 1e-08, 0.01, 10
BC1 = 1.0 - B1 ** STEP
BC2 = 1.0 - B2 ** STEP

VEC_NAMES = ("f_bias", "sg_ln_g", "sg_ln_b", "sg_b", "att_out_g", "sg_out_g", "pre_mix_g",
             "post_mix_g", "pre_ffn_g", "post_ffn_g", "ple_gate_b")


def _dot(a, b):
    return jnp.dot(a, b, preferred_element_type=F32)


def _dot_nt(a, b):
    return lax.dot_general(a, b, (((1,), (1,)), ((), ())), preferred_element_type=F32)


def _dot_tn(a, b):
    return lax.dot_general(a, b, (((0,), (0,)), ((), ())), preferred_element_type=F32)


def _split3(x):
    h = x.astype(BF16)
    r = x - h.astype(F32)
    m = r.astype(BF16)
    l = (r - m.astype(F32)).astype(BF16)
    return h, m, l


def _dot01(sel, x):
    h, m, l = _split3(x)
    return _dot(sel, h) + _dot(sel, m) + _dot(sel, l)


def _dot01_r(x, sel):
    h, m, l = _split3(x)
    return _dot(h, sel) + _dot(m, sel) + _dot(l, sel)


def _dot01_tn(x, sel):
    h, m, l = _split3(x)
    return _dot_tn(h, sel) + _dot_tn(m, sel) + _dot_tn(l, sel)


def _rs(x, n):
    return lax.rsqrt(jnp.sum(x * x, axis=-1, keepdims=True) * (1.0 / n) + EPS)


def _rms_bwd(dn, x, rs, g, n):
    w = dn * g
    dx = rs * w - x * ((rs * rs * rs) * (1.0 / n) * jnp.sum(w * x, axis=-1, keepdims=True))
    return dx, jnp.sum(dn * x * rs, axis=0, keepdims=True)


_GC = math.sqrt(2.0 / math.pi)


def _gelu(x):
    t = jnp.tanh(_GC * (x + 0.044715 * x * x * x))
    return 0.5 * x * (1.0 + t), t


def _gelu_grad(x, t):
    return 0.5 * (1.0 + t) + 0.5 * x * (1.0 - t * t) * (_GC * (1.0 + 3.0 * 0.044715 * x * x))


def _params(vmem_mb, sem=("arbitrary",)):
    return pltpu.CompilerParams(dimension_semantics=sem, vmem_limit_bytes=vmem_mb * 1024 * 1024)


def _row_call(name, body, T, tm, tiled, resident, outs, accs, scratch=(), reverse=False, vmem_mb=48):
    nt = T // tm
    n_t, n_r, n_o, n_a = len(tiled), len(resident), len(outs), len(accs)

    def kern(*refs):
        t_refs = refs[:n_t]
        r_hbm = refs[n_t:n_t + n_r]
        o_refs = refs[n_t + n_r:n_t + n_r + n_o]
        a_refs = refs[n_t + n_r + n_o:n_t + n_r + n_o + n_a]
        r_vmem = refs[n_t + n_r + n_o + n_a:n_t + 2 * n_r + n_o + n_a]
        s_refs = refs[n_t + 2 * n_r + n_o + n_a:]

        @pl.when(pl.program_id(0) == 0)
        def _():
            for h, v in zip(r_hbm, r_vmem):
                pltpu.sync_copy(h, v)
            for a in a_refs + s_refs:
                a[...] = jnp.zeros(a.shape, a.dtype)

        body(t_refs, r_vmem, o_refs, a_refs, s_refs)

    if reverse:
        idx = lambda i: (nt - 1 - i, 0)
        idx_t = lambda i: (nt - 1 - i, 0, 0)
    else:
        idx = lambda i: (i, 0)
        idx_t = lambda i: (i, 0, 0)
    arrays, in_specs = [], []
    for a in tiled:
        if isinstance(a, tuple):
            arrays.append(a[0])
            in_specs.append(pl.BlockSpec((None, a[0].shape[1], tm), idx_t))
        else:
            arrays.append(a)
            in_specs.append(pl.BlockSpec((tm, a.shape[1]), idx))
    in_specs += [pl.BlockSpec(memory_space=pl.ANY) for _ in resident]
    out_shape, out_specs = [], []
    for o in outs:
        if len(o) == 3:
            out_shape.append(jax.ShapeDtypeStruct((nt, o[0], tm), o[1]))
            out_specs.append(pl.BlockSpec((None, o[0], tm), idx_t))
        else:
            out_shape.append(jax.ShapeDtypeStruct((T, o[0]), o[1]))
            out_specs.append(pl.BlockSpec((tm, o[0]), idx))
    out_shape += [jax.ShapeDtypeStruct(s, F32) for s in accs]
    out_specs += [pl.BlockSpec(s, lambda i, n=len(s): (0,) * n) for s in accs]
    scratch_shapes = [pltpu.VMEM(r.shape, r.dtype) for r in resident]
    scratch_shapes += [pltpu.VMEM(s, F32) for s in scratch]
    return pl.pallas_call(
        kern, name=name, grid=(nt,), in_specs=in_specs, out_specs=out_specs, out_shape=out_shape,
        scratch_shapes=scratch_shapes, compiler_params=_params(vmem_mb),
    )(*arrays, *resident)


def _sg_forward(zu, zv, wm_ref, bsg, lng, lnb, mixed_ref, tm):
    gu, tu = _gelu(zu)
    vg, tv = _gelu(zv)
    mu = jnp.sum(vg, axis=-1, keepdims=True) * (1.0 / AW)
    xc = vg - mu
    rstd = lax.rsqrt(jnp.sum(xc * xc, axis=-1, keepdims=True) * (1.0 / AW) + EPS)
    xhat = xc * rstd
    vvb = (xhat * lng + lnb).astype(BF16)
    lane = lax.broadcasted_iota(jnp.int32, (CH, 128), 1)
    for c in range(tm // CH):
        for j in range(4):
            blk = vvb[c * CH:(c + 1) * CH, j * 128:(j + 1) * 128]
            m0 = _dot(wm_ref[2 * j], blk)
            m1 = _dot(wm_ref[2 * j + 1], blk)
            mixed_ref[c * CH:(c + 1) * CH, j * 128:(j + 1) * 128] = (
                jnp.where(lane < DH, m0, m1) + bsg[:, j * 128:(j + 1) * 128])
    return gu, tu, tv, xhat, rstd, vvb, mixed_ref[...]


def _head_consts():
    src = jnp.arange(AW)
    dst = (src // DH) * 128 + src % DH
    wide = jnp.arange(NH * 128)
    expand = (dst[:, None] == wide[None, :]).astype(BF16)
    heads = jnp.arange(128)
    pieces = jnp.stack([((heads[:, None] * 128 + DH + i == wide[None, :]) & (heads[:, None] < NH)).astype(BF16)
                        for i in range(3)])
    spare = wide % 128 - DH
    qconst = jnp.where((spare >= 0) & (spare < 3), -1.0, 0.0).astype(F32)[None, :]
    one64 = jnp.where(spare == 0, 1.0, 0.0).astype(F32)[None, :]
    one67 = jnp.where(spare == 3, 1.0, 0.0).astype(F32)[None, :]
    pick64 = ((wide[:, None] == heads[None, :] * 128 + DH) & (heads[None, :] < NH)).astype(BF16)
    pick67 = ((wide[:, None] == heads[None, :] * 128 + DH + 3) & (heads[None, :] < NH)).astype(BF16)
    return expand, expand.T, pieces, qconst, one64, one67, pick64, pick67


def _masked_sg_w(sg_w):
    r = lax.broadcasted_iota(jnp.int32, (CH, CH), 0)
    c = lax.broadcasted_iota(jnp.int32, (CH, CH), 1)
    return jnp.where((c <= r)[None], sg_w, 0.0)


def _pre_attn_fwd(x, gpre, win, fbias, lng, lnb, wm, bsg, gsg, expand, pieces, qconst, kconst, vconst):
    T = x.shape[0]
    tm = TM

    def body(t, r, o, a, s):
        (x_ref,) = t
        gpre_r, win_r, fb_r, lng_r, lnb_r, wm_r, bsg_r, gsg_r, ex_r, pc_r, qc_r, kc_r, vc_r = r
        a_o, flog_o, zuv_o, ysgn_o, q8_o, k8_o, v8_o = o
        carry_ref, mixed_ref = s
        xv = x_ref[...]
        av = (xv * _rs(xv, D) * gpre_r[...]).astype(BF16)
        a_o[...] = av
        z = _dot(av, win_r[...])
        zu = z[:, 3 * AW:4 * AW]
        zv = z[:, 4 * AW:5 * AW]
        zuv_o[:, 0:AW] = zu
        zuv_o[:, AW:2 * AW] = zv
        zf = z[:, 5 * AW:] + fb_r[...]
        flog_o[...] = zf
        lane = lax.broadcasted_iota(jnp.int32, (tm, 128), 1)
        logf = jnp.where(lane < NH, jnp.minimum(zf, 0.0) - jnp.log(1.0 + jnp.exp(-jnp.abs(zf))), 0.0)
        rr = lax.broadcasted_iota(jnp.int32, (tm, tm), 0)
        cc = lax.broadcasted_iota(jnp.int32, (tm, tm), 1)
        tri = (cc <= rr).astype(BF16)
        cum = _dot01(tri, logf) + carry_ref[...]
        carry_ref[...] = cum[tm - 1:tm, :]
        ex = ex_r[...]
        q8_o[...] = (_dot((z[:, 0:AW] * (DH ** -0.5)).astype(BF16), ex) + qc_r[...]).astype(BF16)
        ch, cm, cl = _split3(cum)
        k8_o[...] = (_dot(z[:, AW:2 * AW].astype(BF16), ex) + _dot(ch, pc_r[0]) + _dot(cm, pc_r[1])
                     + _dot(cl, pc_r[2]) + kc_r[...]).astype(BF16)
        v8_o[...] = (_dot(z[:, 2 * AW:3 * AW].astype(BF16), ex) + vc_r[...]).astype(BF16)
        gu, _, _, _, _, _, mixed = _sg_forward(zu, zv, wm_r, bsg_r[...], lng_r[...], lnb_r[...], mixed_ref, tm)
        ysg = gu * mixed
        ysgn_o[...] = (ysg * _rs(ysg, AW) * gsg_r[...]).astype(BF16)

    return _row_call(
        "pre_attn_fwd", body, T, tm, [x],
        [gpre, win, fbias, lng, lnb, wm, bsg, gsg, expand, pieces, qconst, kconst, vconst],
        [(D, BF16), (128, F32), (2 * AW, F32), (AW, BF16), (NH * 128, BF16), (NH * 128, BF16), (NH * 128, BF16)], [],
        scratch=[(1, 128), (tm, AW)], vmem_mb=48)


def _flash_fwd(qt8, k8, vt8, sel):
    T = k8.shape[0]
    nq = T // TQ

    def body(qt_ref, k_ref, vt_ref, sel_ref, o_ref, l_ref, u_scr, p_scr):
        qi = pl.program_id(1)
        qts = (qt_ref[0:128, :], qt_ref[128:256, :])
        dmat = (lax.broadcasted_iota(jnp.int32, (TQ, TQ), 0) - lax.broadcasted_iota(jnp.int32, (TQ, TQ), 1))
        u_scr[1] = jnp.full((2, TQ, TQ), MASKED, F32)
        p_scr[...] = jnp.zeros(p_scr.shape, BF16)

        def sub(t, carry, sc, sb, masked):
            blk_c = jnp.clip(t - 2, 0, qi)
            off_a = pl.multiple_of(jnp.minimum(t, qi) * TQ, TQ)
            new = []
            for j in (0, 1):
                m, al, acc = carry[j]
                acc = al * acc + _dot(vt_ref[blk_c, j * 128:(j + 1) * 128, :], p_scr[sc, j])
                m_new = jnp.maximum(m, jnp.max(u_scr[sb, j], axis=0, keepdims=True))
                p_scr[sb, j] = jnp.exp(u_scr[sb, j] - m_new).astype(BF16)
                u = _dot(k_ref[pl.ds(off_a, TQ), j * 128:(j + 1) * 128], qts[j])
                u_scr[sc, j] = jnp.where(dmat <= (qi - t) * TQ, u, MASKED) if masked else u
                new.append((m_new, jnp.exp(m - m_new), acc))
            return tuple(new)

        def pair(t2, carry, masked):
            return sub(2 * t2 + 1, sub(2 * t2, carry, 0, 1, masked), 1, 0, masked)

        init = tuple((jnp.full((1, TQ), NEG, F32), jnp.ones((1, TQ), F32), jnp.zeros((128, TQ), F32))
                     for _ in (0, 1))
        carry = lax.fori_loop(0, qi // 2, lambda t2, cr: pair(t2, cr, False), init)
        (m0, _, a0), (m1, _, a1) = pair(qi // 2 + 1, pair(qi // 2, carry, True), True)
        l0 = a0[DH:DH + 1, :]
        l1 = a1[DH:DH + 1, :]
        o_ref[...] = _dot01_tn(a0 * (1.0 / l0), sel_ref[0]) + _dot01_tn(a1 * (1.0 / l1), sel_ref[1])
        l_ref[0:1, :] = m0 + jnp.log(l0)
        l_ref[1:2, :] = m1 + jnp.log(l1)
        l_ref[2:8, :] = jnp.zeros((6, TQ), F32)

    return pl.pallas_call(
        body, name="flash_fwd", grid=(4, nq),
        in_specs=[pl.BlockSpec((None, 256, TQ), lambda h, i: (i, h, 0)),
                  pl.BlockSpec((T, 256), lambda h, i: (0, h)),
                  pl.BlockSpec((nq, 256, TQ), lambda h, i: (0, h, 0)),
                  pl.BlockSpec((2, 128, 128), lambda h, i: (0, 0, 0))],
        out_specs=[pl.BlockSpec((TQ, 128), lambda h, i: (i, h)),
                   pl.BlockSpec((None, 8, TQ), lambda h, i: (h, 0, i))],
        out_shape=[jax.ShapeDtypeStruct((T, AW), F32), jax.ShapeDtypeStruct((4, 8, T), F32)],
        scratch_shapes=[pltpu.VMEM((2, 2, TQ, TQ), F32), pltpu.VMEM((2, 2, TQ, TQ), BF16)],
        compiler_params=_params(40, ("arbitrary", "arbitrary")),
    )(qt8, k8, vt8, sel)


def _flash_bwd(q8, qt8, k8, v8, do8, dot8, lse, dlt):
    T = q8.shape[0]
    nk = T // TQ

    def body(q_ref, qt_ref, k_ref, v_ref, do_ref, dot_ref, l_ref, d_ref, dqt_ref, dk_ref, dv_ref,
             u_scr, dp_scr, p_scr, ds_scr):
        kb = pl.program_id(1)
        n = nk - kb

        @pl.when(kb == 0)
        def _():
            dqt_ref[...] = jnp.zeros(dqt_ref.shape, F32)

        dk_ref[...] = jnp.zeros(dk_ref.shape, F32)
        dv_ref[...] = jnp.zeros(dv_ref.shape, F32)
        u_scr[1] = jnp.full((2, TQ, TQ), MASKED, F32)
        dp_scr[1] = jnp.zeros((2, TQ, TQ), F32)
        p_scr[...] = jnp.zeros(p_scr.shape, BF16)
        ds_scr[...] = jnp.zeros(ds_scr.shape, BF16)
        dmat = (lax.broadcasted_iota(jnp.int32, (TQ, TQ), 0) - lax.broadcasted_iota(jnp.int32, (TQ, TQ), 1))
        ks = (k_ref[:, 0:128], k_ref[:, 128:256])
        vs = (v_ref[:, 0:128], v_ref[:, 128:256])

        def sub(t, sc, sb):
            blk_a = kb + jnp.minimum(t, n - 1)
            blk_c = kb + jnp.clip(t - 2, 0, n - 1)
            off_b = pl.multiple_of((kb + jnp.clip(t - 1, 0, n - 1)) * TQ, TQ)
            off_c = pl.multiple_of(blk_c * TQ, TQ)
            lim = jnp.where(t < n, t * TQ, -TQ)
            for j in (0, 1):
                hl = slice(j * 128, (j + 1) * 128)
                dqt_ref[blk_c, hl, :] += _dot_tn(ks[j], ds_scr[sc, j])
                dk_ref[:, hl] += _dot(ds_scr[sc, j], q_ref[pl.ds(off_c, TQ), hl])
                dv_ref[:, hl] += _dot(p_scr[sc, j], do_ref[pl.ds(off_c, TQ), hl])
                p = jnp.exp(u_scr[sb, j] - l_ref[j:j + 1, pl.ds(off_b, TQ)])
                p_scr[sb, j] = p.astype(BF16)
                ds_scr[sb, j] = (p * (dp_scr[sb, j] - d_ref[j:j + 1, pl.ds(off_b, TQ)])).astype(BF16)
                u_scr[sc, j] = jnp.where(dmat <= lim, _dot(ks[j], qt_ref[blk_a, hl, :]), MASKED)
                dp_scr[sc, j] = _dot(vs[j], dot_ref[blk_a, hl, :])

        def it(t2, carry):
            sub(2 * t2, 0, 1)
            sub(2 * t2 + 1, 1, 0)
            return carry

        lax.fori_loop(0, n // 2 + 1, it, 0)

        @pl.when(n % 2 == 1)
        def _():
            sub(n + 1, 0, 1)

    return pl.pallas_call(
        body, name="flash_bwd", grid=(4, nk),
        in_specs=[pl.BlockSpec((T, 256), lambda h, i: (0, h)),
                  pl.BlockSpec((nk, 256, TQ), lambda h, i: (0, h, 0)),
                  pl.BlockSpec((TQ, 256), lambda h, i: (i, h)),
                  pl.BlockSpec((TQ, 256), lambda h, i: (i, h)),
                  pl.BlockSpec((T, 256), lambda h, i: (0, h)),
                  pl.BlockSpec((nk, 256, TQ), lambda h, i: (0, h, 0)),
                  pl.BlockSpec((None, 8, T), lambda h, i: (h, 0, 0)),
                  pl.BlockSpec((None, 8, T), lambda h, i: (h, 0, 0))],
        out_specs=[pl.BlockSpec((nk, 256, TQ), lambda h, i: (0, h, 0)),
                   pl.BlockSpec((TQ, 256), lambda h, i: (i, h)),
                   pl.BlockSpec((TQ, 256), lambda h, i: (i, h))],
        out_shape=[jax.ShapeDtypeStruct((nk, NH * 128, TQ), F32), jax.ShapeDtypeStruct((T, NH * 128), F32),
                   jax.ShapeDtypeStruct((T, NH * 128), F32)],
        scratch_shapes=[pltpu.VMEM((2, 2, TQ, TQ), F32), pltpu.VMEM((2, 2, TQ, TQ), F32),
                        pltpu.VMEM((2, 2, TQ, TQ), BF16), pltpu.VMEM((2, 2, TQ, TQ), BF16)],
        compiler_params=_params(56, ("arbitrary", "arbitrary")),
    )(q8, qt8, k8, v8, do8, dot8, lse, dlt)


def _tail_fwd1(x, yatt, ysgn, gatt, wout, gpm, gpf, w1):
    T = x.shape[0]

    def body(t, r, o, a, s):
        x_ref, ya_ref, ys_ref = t
        gatt_r, wout_r, gpm_r, gpf_r, w1_r = r
        y_o, o_o, h1_o, c2_o, s_o, rr_o = o
        ya = ya_ref[...]
        yan = (ya * _rs(ya, AW) * gatt_r[...]).astype(BF16)
        y_o[:, 0:AW] = yan
        y_o[:, AW:] = ys_ref[...]
        ov = _dot(yan, wout_r[0:AW, :]) + _dot(ys_ref[...], wout_r[AW:, :])
        o_o[...] = ov
        h1 = x_ref[...] + ov * _rs(ov, D) * gpm_r[...]
        h1_o[...] = h1
        c2 = (h1 * _rs(h1, D) * gpf_r[...]).astype(BF16)
        c2_o[...] = c2
        for k in range(4):
            rr = jnp.maximum(_dot(c2, w1_r[k]), 0.0)
            rr_o[:, k * D:(k + 1) * D] = rr.astype(BF16)
            s_o[:, k * D:(k + 1) * D] = (rr * rr).astype(BF16)

    return _row_call(
        "tail_fwd1", body, T, TM, [x, yatt, ysgn], [gatt, wout, gpm, gpf, w1],
        [(D, BF16), (D, F32), (D, F32), (D, BF16), (DFF, BF16), (DFF, BF16)], [], vmem_mb=48)


def _tail_fwd2(sact, h1, p, tgt, w2, gpff, wg, bg, wpe):
    T = h1.shape[0]

    def body(t, r, o, a, s):
        s_ref, h1_ref, p_ref, t_ref = t
        w2_r, gpff_r, wg_r, bg_r, wpe_r = r
        ff_o, h2b_o, de_o, dpre_o, dh2_o = o
        loss_a, dbg_a = a
        ff = _dot(s_ref[...], w2_r[...])
        ff_o[...] = ff
        h2 = h1_ref[...] + ff * _rs(ff, D) * gpff_r[...]
        h2b = h2.astype(BF16)
        h2b_o[...] = h2b
        gate = 1.0 / (1.0 + jnp.exp(-(_dot(h2b, wg_r[...]) + bg_r[...])))
        pb = p_ref[...].astype(BF16)
        e = jnp.concatenate([_dot(pb, wpe_r[k]) for k in range(4)], axis=1)
        diff = h2 + gate * e - t_ref[...]
        loss_a[...] += jnp.sum(diff * diff, axis=0, keepdims=True)
        dh3 = diff * (1.0 / D)
        de_o[...] = (dh3 * gate).astype(BF16)
        dpre = dh3 * e * gate * (1.0 - gate)
        dbg_a[...] += jnp.sum(dpre, axis=0, keepdims=True)
        dpb = dpre.astype(BF16)
        dpre_o[...] = dpb
        dh2_o[...] = dh3 + _dot_nt(dpb, wg_r[...])

    return _row_call(
        "tail_fwd2", body, T, TM, [sact, h1, p, tgt], [w2, gpff, wg, bg, wpe],
        [(D, F32), (D, BF16), (D, BF16), (D, BF16), (D, F32)], [(1, D), (1, D)], vmem_mb=48)


def _tail_bwd(dh2, ff, rr, h1, ov, yatt, w2, w1, wout, gpff, gpf, gpm, gatt, gsel, expand):
    T = dh2.shape[0]

    def body(t, r, o, a, s):
        dh2_ref, ff_ref, rr_ref, h1_ref, o_ref, ya_ref = t
        w2_r, w1_r, wout_r, gpff_r, gpf_r, gpm_r, gatt_r, gsel_r, ex_r = r
        dff_o, dr_o, do_o, do8_o, dlt_o, dysg_o, dh1_o = o
        dgpff_a, dgpf_a, dgpm_a, dgatt_a = a
        dh2v = dh2_ref[...]
        ffv = ff_ref[...]
        dff, dg = _rms_bwd(dh2v, ffv, _rs(ffv, D), gpff_r[...], D)
        dgpff_a[...] += dg
        dffb = dff.astype(BF16)
        dff_o[...] = dffb
        drb = (_dot_nt(dffb, w2_r[...]) * (2.0 * rr_ref[...].astype(F32))).astype(BF16)
        dr_o[...] = drb
        dc2 = _dot_nt(drb[:, 0:D], w1_r[0])
        for k in range(1, 4):
            dc2 = dc2 + _dot_nt(drb[:, k * D:(k + 1) * D], w1_r[k])
        h1v = h1_ref[...]
        d1, dg = _rms_bwd(dc2, h1v, _rs(h1v, D), gpf_r[...], D)
        dgpf_a[...] += dg
        dh1 = dh2v + d1
        dh1_o[...] = dh1
        ovv = o_ref[...]
        dov, dg = _rms_bwd(dh1, ovv, _rs(ovv, D), gpm_r[...], D)
        dgpm_a[...] += dg
        dob = dov.astype(BF16)
        do_o[...] = dob
        dysg_o[...] = _dot_nt(dob, wout_r[AW:, :])
        dyan = _dot_nt(dob, wout_r[0:AW, :])
        ya = ya_ref[...]
        dya, dg = _rms_bwd(dyan, ya, _rs(ya, AW), gatt_r[...], AW)
        dgatt_a[...] += dg
        do8_o[...] = _dot(dya.astype(BF16), ex_r[...]).astype(BF16)
        dlt_o[...] = _dot01_r(dya * ya, gsel_r[...])

    return _row_call(
        "tail_bwd", body, T, TM, [dh2, ff, rr, h1, ov, yatt],
        [w2, w1, wout, gpff, gpf, gpm, gatt, gsel, expand],
        [(D, BF16), (DFF, BF16), (D, BF16), (NH * 128, BF16), (128, F32), (AW, F32), (D, F32)],
        [(1, D), (1, D), (1, D), (1, AW)], vmem_mb=56)


def _pre_attn_bwd(x, dh1, dq8, dk8, dv8, flog, zuv, dysg, gpre, win, lng, lnb, wm, wmt, bsg, gsg, gsel, shrink, pick64, pick67):
    T = x.shape[0]
    tm = TM

    def body(t, r, o, a, s):
        x_ref, dh1_ref, dq_ref, dk_ref, dv_ref, fl_ref, zuv_ref, dys_ref = t
        gpre_r, win_r, lng_r, lnb_r, wm_r, wmt_r, bsg_r, gsg_r, gsel_r, sh_r, p64_r, p67_r = r
        dx_o, dz_o = o
        dgpre_a, dfb_a, dgsg_a, dlng_a, dlnb_a, dws_a, dbs_a, dsb_a = a
        carry_ref, mixed_ref, dvv_ref = s
        dq8v = dq_ref[...]
        dk8v = dk_ref[...]
        dcv = _dot01_r(dq8v, p67_r[...]) + _dot01_r(dk8v, p64_r[...])
        rr = lax.broadcasted_iota(jnp.int32, (tm, tm), 0)
        cc = lax.broadcasted_iota(jnp.int32, (tm, tm), 1)
        triu = (cc >= rr).astype(BF16)
        dlogf = _dot01(triu, dcv) + carry_ref[...]
        carry_ref[...] = dlogf[0:1, :]
        dzf = dlogf * (1.0 / (1.0 + jnp.exp(fl_ref[...])))
        dfb_a[...] += jnp.sum(dzf, axis=0, keepdims=True)
        dz_o[:, 5 * AW:] = dzf.astype(BF16)
        zu = zuv_ref[:, 0:AW]
        zv = zuv_ref[:, AW:]
        gu, tu, tv, xhat, rstd, vvb, mixed = _sg_forward(
            zu, zv, wm_r, bsg_r[...], lng_r[...], lnb_r[...], mixed_ref, tm)
        ysg = gu * mixed
        dysg_n = dys_ref[...]
        dys, dg = _rms_bwd(dysg_n, ysg, _rs(ysg, AW), gsg_r[...], AW)
        dgsg_a[...] += dg
        dgu = dys * mixed
        dmix = dys * gu
        dmb = dmix.astype(BF16)
        lane = lax.broadcasted_iota(jnp.int32, (CH, 128), 1)
        lo = lane < DH
        for c in range(tm // CH):
            rows = slice(c * CH, (c + 1) * CH)
            dbs_a[...] += dmix[rows, :]
            for j in range(4):
                cols = slice(j * 128, (j + 1) * 128)
                dmblk = dmb[rows, cols]
                vblk = vvb[rows, cols]
                d0 = _dot(wmt_r[2 * j], dmblk)
                d1 = _dot(wmt_r[2 * j + 1], dmblk)
                dvv_ref[rows, cols] = jnp.where(lo, d0, d1)
                dws_a[2 * j] += _dot_nt(jnp.where(lo, dmblk, jnp.zeros_like(dmblk)), vblk)
                dws_a[2 * j + 1] += _dot_nt(jnp.where(lo, jnp.zeros_like(dmblk), dmblk), vblk)
        dvv = dvv_ref[...]
        dlng_a[...] += jnp.sum(dvv * xhat, axis=0, keepdims=True)
        dlnb_a[...] += jnp.sum(dvv, axis=0, keepdims=True)
        dxh = dvv * lng_r[...]
        dvg = rstd * (dxh - jnp.sum(dxh, axis=-1, keepdims=True) * (1.0 / AW)
                      - xhat * (jnp.sum(dxh * xhat, axis=-1, keepdims=True) * (1.0 / AW)))
        dz_o[:, 3 * AW:4 * AW] = (dgu * _gelu_grad(zu, tu)).astype(BF16)
        dz_o[:, 4 * AW:5 * AW] = (dvg * _gelu_grad(zv, tv)).astype(BF16)
        dz_o[:, 0:AW] = _dot((dq8v * (DH ** -0.5)).astype(BF16), sh_r[...]).astype(BF16)
        dz_o[:, AW:2 * AW] = _dot(dk8v.astype(BF16), sh_r[...]).astype(BF16)
        dz_o[:, 2 * AW:3 * AW] = _dot(dv_ref[...].astype(BF16), sh_r[...]).astype(BF16)
        da = _dot_nt(dz_o[...], win_r[...])
        xv = x_ref[...]
        dxa, dg = _rms_bwd(da, xv, _rs(xv, D), gpre_r[...], D)
        dgpre_a[...] += dg
        dx_o[...] = dh1_ref[...] + dxa

        @pl.when(pl.program_id(0) == T // tm - 1)
        def _():
            dsb_a[...] = _dot01_r(dbs_a[...], gsel_r[...])

    outs = _row_call(
        "pre_attn_bwd", body, T, tm, [x, dh1, dq8, dk8, dv8, flog, zuv, dysg],
        [gpre, win, lng, lnb, wm, wmt, bsg, gsg, gsel, shrink, pick64, pick67],
        [(D, F32), (ZW, BF16)],
        [(1, D), (1, 128), (1, AW), (1, AW), (1, AW), (8, CH, CH), (CH, AW), (CH, 128)],
        scratch=[(1, 128), (tm, AW), (tm, AW)], reverse=True, vmem_mb=48)
    return outs


def _matmul_tn(name, a, b, tn=512, tt=2048, shards=1, after=None):
    T, K = a.shape
    N = b.shape[1]
    tk = min(K, 1024)
    tn = min(tn, N // shards)
    tt = min(tt, T)
    nj = N // shards // tn

    def body(a_ref, b_ref, *rest):
        o_ref = rest[-1]

        @pl.when(pl.program_id(2) == 0)
        def _():
            o_ref[...] = jnp.zeros(o_ref.shape, F32)

        o_ref[...] += _dot_tn(a_ref[...].astype(BF16), b_ref[...].astype(BF16))

    ordered = after is not None

    if shards == 1:
        out_shape = jax.ShapeDtypeStruct((K, N), F32)
        out_spec = pl.BlockSpec((tk, tn), lambda i, j, t: (i, j))
    else:
        out_shape = jax.ShapeDtypeStruct((shards, K, N // shards), F32)
        out_spec = pl.BlockSpec((None, tk, tn), lambda i, j, t: (j // nj, i, j % nj))
    return pl.pallas_call(
        body, name=name, grid=(K // tk, N // tn, T // tt),
        in_specs=[pl.BlockSpec((tt, tk), lambda i, j, t: (t, i)),
                  pl.BlockSpec((tt, tn), lambda i, j, t: (t, j))] + [pl.BlockSpec(memory_space=pl.ANY)] * ordered,
        out_specs=out_spec, out_shape=out_shape,
        compiler_params=_params(40, ("arbitrary", "arbitrary", "arbitrary")),
    )(a, b, *([after] * ordered))


def _me():
    return lax.axis_index("x"), lax.axis_index("y"), lax.axis_index("c")


HBM_SPEC = pl.BlockSpec(memory_space=pltpu.HBM)


def _gather_weights(mine):
    half = mine.shape[0] // 2

    def body(mine_ref, out_ref, ici_send, ici_recv, d2d_send, d2d_recv):
        x, y, c = _me()
        k_me = 2 * x + y
        chips = [(1 - x, y), (x, 1 - y), (1 - x, 1 - y)]
        my_rows = pl.ds(pl.multiple_of(c * half, 16), half)
        sib_rows = pl.ds(pl.multiple_of((1 - c) * half, 16), half)

        def over_ici(j, k, to):
            src = mine_ref.at[my_rows] if k is None else out_ref.at[k, my_rows]
            return pltpu.make_async_remote_copy(
                src_ref=src, dst_ref=out_ref.at[k_me if k is None else k, my_rows], send_sem=ici_send.at[j],
                recv_sem=ici_recv.at[j], device_id=to, device_id_type=MESH)

        def over_d2d(j, k, rows):
            return pltpu.make_async_remote_copy(
                src_ref=out_ref.at[k, rows], dst_ref=out_ref.at[k, rows], send_sem=d2d_send.at[j],
                recv_sem=d2d_recv.at[j], device_id=(x, y, 1 - c), device_id_type=MESH)

        first = [over_ici(j, None, (cx, cy, c)) for j, (cx, cy) in enumerate(chips)]
        for cp in first:
            cp.start()
        passed = [over_d2d(j, 2 * cx + cy, my_rows) for j, (cx, cy) in enumerate(chips)]
        for j, (cx, cy) in enumerate(chips):
            over_ici(j, 2 * cx + cy, (cx, cy, c)).wait_recv()
            passed[j].start()
        for j, (cx, cy) in enumerate(chips):
            over_d2d(j, 2 * cx + cy, sib_rows).wait_recv()
        for cp in first + passed:
            cp.wait_send()

    return pl.pallas_call(
        body, name="gather_weights", in_specs=[HBM_SPEC], out_specs=HBM_SPEC,
        out_shape=jax.ShapeDtypeStruct((4,) + mine.shape, mine.dtype),
        scratch_shapes=[pltpu.SemaphoreType.DMA((3,)), pltpu.SemaphoreType.DMA((3,)), pltpu.SemaphoreType.DMA((3,)),
                        pltpu.SemaphoreType.DMA((3,))],
    )(mine)


SEM_SPEC = pl.BlockSpec(memory_space=pltpu.SEMAPHORE)
EFFECT = pltpu.SideEffectType.DATAFLOW_SIDE_EFFECTING


def _gather_late_start(mine):
    def body(mine_ref, land_ref, send_sems, recv_sems, mine_thru, land_thru, token):
        x, y, c = _me()
        k_me = 2 * x + y
        for j, (cx, cy) in enumerate([(1 - x, y), (x, 1 - y), (1 - x, 1 - y)]):
            pltpu.make_async_remote_copy(
                src_ref=mine_ref, dst_ref=land_ref.at[k_me], send_sem=send_sems.at[j], recv_sem=recv_sems.at[j],
                device_id=(cx, cy, c), device_id_type=MESH).start()
        token[...] = jnp.zeros(token.shape, F32)

    land = lax.empty((4,) + mine.shape, mine.dtype)
    return pl.pallas_call(
        body, name="gather_late_start",
        out_shape=(pltpu.SemaphoreType.DMA((3,)), pltpu.SemaphoreType.DMA((3,)), pltpu.HBM(mine.shape, mine.dtype),
                   pltpu.HBM(land.shape, land.dtype), jax.ShapeDtypeStruct((8, 128), F32)),
        in_specs=(HBM_SPEC, HBM_SPEC),
        out_specs=(SEM_SPEC, SEM_SPEC, HBM_SPEC, HBM_SPEC, pl.BlockSpec(memory_space=pltpu.VMEM)),
        input_output_aliases={0: 2, 1: 3},
        compiler_params=pltpu.CompilerParams(has_side_effects=EFFECT),
    )(pltpu.with_memory_space_constraint(mine, pltpu.HBM), pltpu.with_memory_space_constraint(land, pltpu.HBM))


def _gather_late_wait(send_sems, recv_sems, mine_thru, land_thru, after):
    def body(mine_ref, land_ref, send_sems, recv_sems, after_ref, mine_dead, got_ref):
        x, y, c = _me()
        for j, (cx, cy) in enumerate([(1 - x, y), (x, 1 - y), (1 - x, 1 - y)]):
            cp = pltpu.make_async_remote_copy(
                src_ref=mine_ref, dst_ref=land_ref.at[2 * cx + cy], send_sem=send_sems.at[j],
                recv_sem=recv_sems.at[j], device_id=(cx, cy, c), device_id_type=MESH)
            cp.wait_send()
            cp.wait_recv()

    return pl.pallas_call(
        body, name="gather_late_wait",
        out_shape=(pltpu.HBM(mine_thru.shape, mine_thru.dtype), pltpu.HBM(land_thru.shape, land_thru.dtype)),
        in_specs=(HBM_SPEC, HBM_SPEC, SEM_SPEC, SEM_SPEC, pl.BlockSpec(memory_space=pl.ANY)),
        out_specs=(HBM_SPEC, HBM_SPEC), input_output_aliases={0: 0, 1: 1},
        compiler_params=pltpu.CompilerParams(has_side_effects=EFFECT),
    )(mine_thru, land_thru, send_sems, recv_sems, after)[1]


def _swap_halves(gs, tag):
    n = len(gs)

    def body(*refs):
        g_refs, got_refs, send_sems, recv_sems = refs[:n], refs[n:2 * n], refs[2 * n], refs[2 * n + 1]
        x, y, c = _me()
        cps = []
        for i, (g_ref, got_ref) in enumerate(zip(g_refs, got_refs)):
            half = g_ref.shape[1] // 2
            theirs = pl.multiple_of((1 - c) * half, 16)
            cps.append(pltpu.make_async_remote_copy(
                src_ref=g_ref.at[:, pl.ds(theirs, half), :], dst_ref=got_ref, send_sem=send_sems.at[i],
                recv_sem=recv_sems.at[i], device_id=(x, y, 1 - c), device_id_type=MESH))
        for cp in cps:
            cp.start()
        for cp in cps:
            cp.wait()

    return pl.pallas_call(
        body, name="swap_halves_" + tag, in_specs=[HBM_SPEC] * n, out_specs=[HBM_SPEC] * n,
        out_shape=[jax.ShapeDtypeStruct((4, g.shape[1] // 2, g.shape[2]), F32) for g in gs],
        scratch_shapes=[pltpu.SemaphoreType.DMA((n,)), pltpu.SemaphoreType.DMA((n,))],
    )(*gs)


def _swap_start(gs, tag):
    n = len(gs)

    def body(*refs):
        g_refs, land_refs, send_sems, recv_sems = refs[:n], refs[n:2 * n], refs[2 * n], refs[2 * n + 1]
        x, y, c = _me()
        for i, (g_ref, land_ref) in enumerate(zip(g_refs, land_refs)):
            half = g_ref.shape[1] // 2
            theirs = pl.multiple_of((1 - c) * half, 16)
            pltpu.make_async_remote_copy(
                src_ref=g_ref.at[:, pl.ds(theirs, half), :], dst_ref=land_ref, send_sem=send_sems.at[i],
                recv_sem=recv_sems.at[i], device_id=(x, y, 1 - c), device_id_type=MESH).start()

    lands = [lax.empty((4, g.shape[1] // 2, g.shape[2]), F32) for g in gs]
    hbm = lambda t: pltpu.HBM(t.shape, t.dtype)
    res = pl.pallas_call(
        body, name="swap_start_" + tag,
        out_shape=(pltpu.SemaphoreType.DMA((n,)), pltpu.SemaphoreType.DMA((n,)), *[hbm(t) for t in gs],
                   *[hbm(t) for t in lands]),
        in_specs=(HBM_SPEC,) * (2 * n), out_specs=(SEM_SPEC, SEM_SPEC) + (HBM_SPEC,) * (2 * n),
        input_output_aliases={i: 2 + i for i in range(2 * n)},
        compiler_params=pltpu.CompilerParams(has_side_effects=EFFECT),
    )(*[pltpu.with_memory_space_constraint(t, pltpu.HBM) for t in list(gs) + lands])
    return res[0], res[1], res[2:2 + n], res[2 + n:2 + 2 * n]


def _swap_wait(send_sems, recv_sems, g_thru, land_thru, after, tag):
    n = len(g_thru)

    def body(*refs):
        g_refs, land_refs, send_sems, recv_sems = refs[:n], refs[n:2 * n], refs[2 * n], refs[2 * n + 1]
        x, y, c = _me()
        for i, (g_ref, land_ref) in enumerate(zip(g_refs, land_refs)):
            half = g_ref.shape[1] // 2
            theirs = pl.multiple_of((1 - c) * half, 16)
            pltpu.make_async_remote_copy(
                src_ref=g_ref.at[:, pl.ds(theirs, half), :], dst_ref=land_ref, send_sem=send_sems.at[i],
                recv_sem=recv_sems.at[i], device_id=(x, y, 1 - c), device_id_type=MESH).wait()

    hbm = lambda t: pltpu.HBM(t.shape, t.dtype)
    res = pl.pallas_call(
        body, name="swap_wait_" + tag, out_shape=tuple(hbm(t) for t in list(g_thru) + list(land_thru)),
        in_specs=(HBM_SPEC,) * (2 * n) + (SEM_SPEC, SEM_SPEC, pl.BlockSpec(memory_space=pl.ANY)),
        out_specs=(HBM_SPEC,) * (2 * n), input_output_aliases={i: i for i in range(2 * n)},
        compiler_params=pltpu.CompilerParams(has_side_effects=EFFECT),
    )(*g_thru, *land_thru, send_sems, recv_sems, after)
    return res[:n], res[n:]


def _pair_sum(name, c1, g, got):
    half, cols = got.shape[1], got.shape[2]

    def body(c_ref, a_ref, b_ref, o_ref):
        o_ref[...] = (a_ref[...] + b_ref[...]).astype(BF16)

    return pl.pallas_call(
        body, name="pair_sum_" + name,
        grid_spec=pltpu.PrefetchScalarGridSpec(
            num_scalar_prefetch=1, grid=(4,),
            in_specs=[pl.BlockSpec((1, half, cols), lambda k, c_ref: (k, c_ref[0], 0)),
                      pl.BlockSpec((1, half, cols), lambda k, c_ref: (k, 0, 0))],
            out_specs=pl.BlockSpec((1, half, cols), lambda k, c_ref: (k, 0, 0))),
        out_shape=jax.ShapeDtypeStruct(got.shape, BF16), compiler_params=_params(32),
    )(c1, g, got)


def _exchange_start(pss, tag):
    n = len(pss)

    def body(*refs):
        ps_refs, land_refs = refs[:n], refs[n:2 * n]
        send_sems, recv_sems = refs[2 * n], refs[2 * n + 1]
        token = refs[4 * n + 2]
        x, y, c = _me()
        k_me = 2 * x + y
        for i, (ps_ref, land_ref) in enumerate(zip(ps_refs, land_refs)):
            for j, (cx, cy) in enumerate([(1 - x, y), (x, 1 - y), (1 - x, 1 - y)]):
                pltpu.make_async_remote_copy(
                    src_ref=ps_ref.at[2 * cx + cy], dst_ref=land_ref.at[k_me], send_sem=send_sems.at[3 * i + j],
                    recv_sem=recv_sems.at[3 * i + j], device_id=(cx, cy, c), device_id_type=MESH).start()
        token[...] = jnp.zeros(token.shape, F32)

    lands = [lax.empty(ps.shape, ps.dtype) for ps in pss]
    hbm = lambda t: pltpu.HBM(t.shape, t.dtype)
    res = pl.pallas_call(
        body, name="exchange_start_" + tag,
        out_shape=(pltpu.SemaphoreType.DMA((3 * n,)), pltpu.SemaphoreType.DMA((3 * n,)), *[hbm(t) for t in pss],
                   *[hbm(t) for t in lands], jax.ShapeDtypeStruct((8, 128), F32)),
        in_specs=(HBM_SPEC,) * (2 * n),
        out_specs=(SEM_SPEC, SEM_SPEC) + (HBM_SPEC,) * (2 * n) + (pl.BlockSpec(memory_space=pltpu.VMEM),),
        input_output_aliases={i: 2 + i for i in range(2 * n)},
        compiler_params=pltpu.CompilerParams(has_side_effects=EFFECT),
    )(*[pltpu.with_memory_space_constraint(t, pltpu.HBM) for t in list(pss) + lands])
    return res[0], res[1], res[2:2 + n], res[2 + n:2 + 2 * n], res[2 + 2 * n]


def _exchange_wait(send_sems, recv_sems, ps_thru, land_thru, after, tag):
    n = len(ps_thru)

    def body(*refs):
        ps_refs, land_refs = refs[:n], refs[n:2 * n]
        send_sems, recv_sems = refs[2 * n], refs[2 * n + 1]
        x, y, c = _me()
        k_me = 2 * x + y
        for i, (ps_ref, land_ref) in enumerate(zip(ps_refs, land_refs)):
            for j, (cx, cy) in enumerate([(1 - x, y), (x, 1 - y), (1 - x, 1 - y)]):
                cp = pltpu.make_async_remote_copy(
                    src_ref=ps_ref.at[k_me], dst_ref=land_ref.at[2 * cx + cy], send_sem=send_sems.at[3 * i + j],
                    recv_sem=recv_sems.at[3 * i + j], device_id=(cx, cy, c), device_id_type=MESH)
                cp.wait_send()
                cp.wait_recv()

    hbm = lambda t: pltpu.HBM(t.shape, t.dtype)
    res = pl.pallas_call(
        body, name="exchange_wait_" + tag,
        out_shape=tuple(hbm(t) for t in list(ps_thru) + list(land_thru)),
        in_specs=(HBM_SPEC,) * (2 * n) + (SEM_SPEC, SEM_SPEC, pl.BlockSpec(memory_space=pl.ANY)),
        out_specs=(HBM_SPEC,) * (2 * n), input_output_aliases={i: i for i in range(2 * n)},
        compiler_params=pltpu.CompilerParams(has_side_effects=EFFECT),
    )(*ps_thru, *land_thru, send_sems, recv_sems, after)
    return res[:n], res[n:]


def _adamw(w, g, m, v):
    m = B1 * m + (1.0 - B1) * g
    v = B2 * v + (1.0 - B2) * (g * g)
    delta = -LR * ((m / BC1) / (jnp.sqrt(v / BC2) + AEPS) + WD * w)
    return delta, m, v


def _reduce_chips(name, parts):
    half, cols = parts.shape[1], parts.shape[2]

    def body(p_ref, o_ref):
        f = lambda k: p_ref[k].astype(F32)
        o_ref[...] = ((f(0) + f(1)) + f(2)) + f(3)

    return pl.pallas_call(
        body, name="reduce_chips_" + name, grid=(1,),
        in_specs=[pl.BlockSpec((4, half, cols), lambda i: (0, 0, 0))],
        out_specs=pl.BlockSpec((half, cols), lambda i: (0, 0)),
        out_shape=jax.ShapeDtypeStruct((half, cols), F32), compiler_params=_params(32),
    )(parts)


def _share_grad(ghs, tag):
    n = len(ghs)

    def body(*refs):
        g_refs, got_refs, send_sems, recv_sems = refs[:n], refs[n:2 * n], refs[2 * n], refs[2 * n + 1]
        x, y, c = _me()
        cps = [pltpu.make_async_remote_copy(
            src_ref=g_ref, dst_ref=got_ref, send_sem=send_sems.at[i], recv_sem=recv_sems.at[i],
            device_id=(x, y, 1 - c), device_id_type=MESH) for i, (g_ref, got_ref) in enumerate(zip(g_refs, got_refs))]
        for cp in cps:
            cp.start()
        for cp in cps:
            cp.wait()

    return pl.pallas_call(
        body, name="share_grad_" + tag, in_specs=[HBM_SPEC] * n, out_specs=[HBM_SPEC] * n,
        out_shape=[jax.ShapeDtypeStruct(g.shape, F32) for g in ghs],
        scratch_shapes=[pltpu.SemaphoreType.DMA((n,)), pltpu.SemaphoreType.DMA((n,))],
    )(*ghs)


def _update(name, c1, gh, got, w, m, v):
    half, cols = gh.shape

    def body(c_ref, gh_ref, got_ref, w_ref, m_ref, v_ref, g_o, d_o, m_o, v_o):
        g = jnp.where(pl.program_id(0) == c_ref[0], gh_ref[...], got_ref[...])
        delta, mn, vn = _adamw(w_ref[...], g, m_ref[...], v_ref[...])
        g_o[...] = g
        d_o[...] = delta
        m_o[...] = mn
        v_o[...] = vn

    same = pl.BlockSpec((half, cols), lambda h, c_ref: (0, 0))
    rows = pl.BlockSpec((half, cols), lambda h, c_ref: (h, 0))
    return pl.pallas_call(
        body, name="update_" + name,
        grid_spec=pltpu.PrefetchScalarGridSpec(
            num_scalar_prefetch=1, grid=(2,), in_specs=[same, same, rows, rows, rows],
            out_specs=[rows, rows, rows, rows]),
        out_shape=[jax.ShapeDtypeStruct(w.shape, F32)] * 4, compiler_params=_params(40),
    )(c1, gh, got, w, m, v)


SMALL_NAMES = ("sg_w",) + VEC_NAMES
VEC_ROWS = 24
VEC_ROW = {"f_bias": 0, "sg_ln_g": 1, "sg_ln_b": 2, "att_out_g": 3, "sg_out_g": 4, "pre_mix_g": 5,
           "post_mix_g": 6, "pre_ffn_g": 7, "sg_b": 8, "post_ffn_g": 16, "ple_gate_b": 17}
LOSS_VEC_ROW = 18


def _small_pack(g, loss_l):
    n = len(SMALL_NAMES)

    def body(*refs):
        g_r = dict(zip(SMALL_NAMES, refs[0:n]))
        loss_r, vec_o, w_o = refs[n:]
        vec_o[...] = jnp.zeros((VEC_ROWS, 1024), F32)
        for name in VEC_NAMES:
            val = g_r[name][...]
            vec_o[pl.ds(VEC_ROW[name], val.shape[0]), pl.ds(0, val.shape[1])] = val
        vec_o[pl.ds(LOSS_VEC_ROW, 1), :] = loss_r[...] * (0.5 / D)
        rr = lax.broadcasted_iota(jnp.int32, (CH, CH), 0)
        cc = lax.broadcasted_iota(jnp.int32, (CH, CH), 1)
        w_o[...] = jnp.where((cc <= rr)[None], g_r["sg_w"][...], 0.0)

    vm = pl.BlockSpec(memory_space=pltpu.VMEM)
    args = [g[k] for k in SMALL_NAMES] + [loss_l]
    return pl.pallas_call(
        body, name="small_pack", in_specs=[vm] * len(args), out_specs=[vm, vm],
        out_shape=[jax.ShapeDtypeStruct((VEC_ROWS, 1024), F32), jax.ShapeDtypeStruct((8, CH, CH), F32)],
    )(*args)


def _small_peers(x, y, c):
    rels = [(rx, ry, rc) for rx in (0, 1) for ry in (0, 1) for rc in (0, 1)][1:]
    return [((x + rx) % 2, (y + ry) % 2, (c + rc) % 2) for rx, ry, rc in rels]


def _small_start(vec, w8):
    def body(vec_ref, w_ref, lv_ref, lw_ref, send_sems, recv_sems, vec_thru, w_thru, lv_thru, lw_thru, token):
        x, y, c = _me()
        me = 4 * x + 2 * y + c
        for j, to in enumerate(_small_peers(x, y, c)):
            for i, (src, land) in enumerate(((vec_ref, lv_ref), (w_ref, lw_ref))):
                pltpu.make_async_remote_copy(
                    src_ref=src, dst_ref=land.at[me], send_sem=send_sems.at[2 * j + i],
                    recv_sem=recv_sems.at[2 * j + i], device_id=to, device_id_type=MESH).start()
        token[...] = jnp.zeros(token.shape, F32)

    ops = [vec, w8, lax.empty((8,) + vec.shape, F32), lax.empty((8,) + w8.shape, F32)]
    hbm = lambda t: pltpu.HBM(t.shape, t.dtype)
    res = pl.pallas_call(
        body, name="small_start",
        out_shape=(pltpu.SemaphoreType.DMA((14,)), pltpu.SemaphoreType.DMA((14,)), *[hbm(t) for t in ops],
                   jax.ShapeDtypeStruct((8, 128), F32)),
        in_specs=(HBM_SPEC,) * 4,
        out_specs=(SEM_SPEC, SEM_SPEC) + (HBM_SPEC,) * 4 + (pl.BlockSpec(memory_space=pltpu.VMEM),),
        input_output_aliases={i: 2 + i for i in range(4)},
        compiler_params=pltpu.CompilerParams(has_side_effects=EFFECT),
    )(*[pltpu.with_memory_space_constraint(t, pltpu.HBM) for t in ops])
    return res[0], res[1], res[2:6], res[6]


def _small_wait(send_sems, recv_sems, thru, after):
    def body(vec_ref, w_ref, lv_ref, lw_ref, send_sems, recv_sems, after_ref, vec_o, w_o, lv_o, lw_o):
        x, y, c = _me()
        for j, (px, py, pc) in enumerate(_small_peers(x, y, c)):
            for i, (src, land) in enumerate(((vec_ref, lv_ref), (w_ref, lw_ref))):
                cp = pltpu.make_async_remote_copy(
                    src_ref=src, dst_ref=land.at[4 * px + 2 * py + pc], send_sem=send_sems.at[2 * j + i],
                    recv_sem=recv_sems.at[2 * j + i], device_id=(px, py, pc), device_id_type=MESH)
                cp.wait_send()
                cp.wait_recv()

    hbm = lambda t: pltpu.HBM(t.shape, t.dtype)
    return pl.pallas_call(
        body, name="small_wait", out_shape=tuple(hbm(t) for t in thru),
        in_specs=(HBM_SPEC,) * 4 + (SEM_SPEC, SEM_SPEC, pl.BlockSpec(memory_space=pl.ANY)),
        out_specs=(HBM_SPEC,) * 4, input_output_aliases={i: i for i in range(4)},
        compiler_params=pltpu.CompilerParams(has_side_effects=EFFECT),
    )(*thru, send_sems, recv_sems, after)


def _small_update(all_v, all_w, w, m, v):
    n = len(SMALL_NAMES)

    def body(*refs):
        allv_r, allw_r = refs[0], refs[1]
        tot_v = allv_r[0]
        tot_w = allw_r[0]
        for d in range(1, 8):
            tot_v = tot_v + allv_r[d]
            tot_w = tot_w + allw_r[d]
        w_r = dict(zip(SMALL_NAMES, refs[2:2 + n]))
        m_r = dict(zip(SMALL_NAMES, refs[2 + n:2 + 2 * n]))
        v_r = dict(zip(SMALL_NAMES, refs[2 + 2 * n:2 + 3 * n]))
        loss_o = refs[2 + 3 * n]
        outs = refs[3 + 3 * n:]
        loss_o[...] = jnp.sum(tot_v[LOSS_VEC_ROW:LOSS_VEC_ROW + 1, :], axis=-1, keepdims=True) + jnp.zeros((1, 128), F32)
        for i, name in enumerate(SMALL_NAMES):
            if name == "sg_w":
                gt = tot_w
            else:
                rows, width = w_r[name].shape
                gt = tot_v[VEC_ROW[name]:VEC_ROW[name] + rows, 0:width]
            delta, mn, vn = _adamw(w_r[name][...], gt, m_r[name][...], v_r[name][...])
            outs[4 * i][...] = gt
            outs[4 * i + 1][...] = delta
            outs[4 * i + 2][...] = mn
            outs[4 * i + 3][...] = vn

    vm = pl.BlockSpec(memory_space=pltpu.VMEM)
    args = [all_v, all_w] + [d[k] for d in (w, m, v) for k in SMALL_NAMES]
    out_shape = [jax.ShapeDtypeStruct((1, 128), F32)]
    out_shape += [jax.ShapeDtypeStruct(w[k].shape, F32) for k in SMALL_NAMES for _ in range(4)]
    res = pl.pallas_call(
        body, name="small_update", in_specs=[vm] * len(args), out_specs=[vm] * len(out_shape), out_shape=out_shape,
        compiler_params=pltpu.CompilerParams(vmem_limit_bytes=32 * 1024 * 1024),
    )(*args)
    return res[0], {k: res[1 + 4 * i:5 + 4 * i] for i, k in enumerate(SMALL_NAMES)}


def _win_kernel_order(gathered):
    w_in = jnp.concatenate([gathered[k].reshape(D, 768)[:, :642] for k in range(4)], axis=1)
    return jnp.concatenate([w_in[:, :3 * AW], w_in[:, 3 * AW + NH:], w_in[:, 3 * AW:3 * AW + NH],
                            jnp.zeros((D, 128 - NH), w_in.dtype)], axis=1)


LATE_ROWS = 256 + 1024 + 1024 + 64 + 256


def _pack_late(w_out, w1, w2, plew, wg):
    return jnp.concatenate([w_out, w1, w2, plew.reshape(64, 1024), wg], axis=0)


def _unpack_late(gathered):
    return (gathered[:, 0:256].reshape(D, D), gathered[:, 256:1280], gathered[:, 1280:2304].reshape(DFF, D),
            gathered[:, 2304:2368].reshape(4, 256, 256), gathered[:, 2368:2624].reshape(D, D))


def _local_step(x, p, tgt, win_k, late_weights, token, on_ff_grads, on_tail_grads, on_small_grads, small):
    T = x.shape[0]
    row = lambda n: small[n].reshape(1, -1)
    fbias = jnp.pad(row("f_bias"), ((0, 0), (0, 128 - NH))) + token[0:1, :]
    wm = _masked_sg_w(small["sg_w"].reshape(8, CH, CH))
    wmb = wm.astype(BF16)
    wmt = jnp.swapaxes(wm, 1, 2).astype(BF16)
    bsg = jnp.repeat(small["sg_b"].reshape(8, CH).T, DH, axis=1)
    ln_g, ln_b, gsg, gatt = row("sg_ln_g"), row("sg_ln_b"), row("sg_out_g"), row("att_out_g")
    gpre, gpm, gpf, gpff, bg = row("pre_mix_g"), row("post_mix_g"), row("pre_ffn_g"), row("post_ffn_g"), row("ple_gate_b")
    gsel = (jnp.arange(AW)[:, None] // DH == jnp.arange(128)[None, :]).astype(BF16)

    expand, shrink, pieces, qconst, one64, one67, pick64, pick67 = _head_consts()
    a, flog, zuv, ysgn, q8, k8, v8 = _pre_attn_fwd(
        x, gpre, win_k, fbias, ln_g, ln_b, wmb, bsg, gsg, expand, pieces, qconst, one67, one64)

    slabs = lambda t: jnp.swapaxes(t.reshape(T // TQ, TQ, NH * 128), 1, 2)
    qt8 = slabs(q8)
    lanes = jnp.arange(128)
    sel = jnp.stack([((lanes[:, None] == lanes[None, :] - DH * j) & (lanes[:, None] < DH)).astype(BF16)
                     for j in (0, 1)])

    yatt, lse = _flash_fwd(qt8, k8, slabs(v8), sel)
    wout, w1, w2, plew, wg = late_weights(lse)
    y, ov, h1, c2, sact, rr = _tail_fwd1(x, yatt, ysgn, gatt, wout, gpm, gpf, w1)
    ff, h2b, de, dpre, dh2, loss_l, dbg = _tail_fwd2(sact, h1, p, tgt, w2, gpff, wg, bg, plew)
    dff, dr, do, do8, dlt, dysg, dh1, dgpff, dgpf, dgpm, dgatt = _tail_bwd(
        dh2, ff, rr, h1, ov, yatt, w2, w1, wout, gpff, gpf, gpm, gatt, gsel, expand)
    dw1 = _matmul_tn("grad_w_ff1", c2, dr, shards=4)
    dw2 = _matmul_tn("grad_w_ff2", sact, dff)
    ff_sent = on_ff_grads(dw1, dw2)
    dwout = _matmul_tn("grad_w_out", y, do, after=ff_sent)
    dwg = _matmul_tn("grad_ple_gate_w", h2b, dpre, after=dwout)
    dplew = _matmul_tn("grad_ple_w", p, de, tn=256, shards=4, after=dwg)
    tail_token = on_tail_grads((dwout, dplew, dwg))
    dlt4 = jnp.pad(dlt[:, :NH].T.reshape(4, 2, T), ((0, 0), (0, 6), (0, 0))) + tail_token[0, 0]
    dqt, dk8, dv8 = _flash_bwd(q8, qt8, k8, v8, do8, slabs(do8), lse, dlt4)
    dx, dz, dgpre, dfb, dgsg, dlng, dlnb, dws, _, dsbt = _pre_attn_bwd(
        x, dh1, jnp.swapaxes(dqt, 1, 2).reshape(T, NH * 128), dk8, dv8, flog, zuv, dysg,
        gpre, win_k, ln_g, ln_b, wmb, wmt, bsg, gsg, gsel, shrink, pick64, pick67)


    dsb = dsbt[:, :8].T
    gsmall = {"sg_w": dws, "f_bias": dfb, "sg_ln_g": dlng, "sg_ln_b": dlnb, "sg_b": dsb,
              "att_out_g": dgatt, "sg_out_g": dgsg, "pre_mix_g": dgpre, "post_mix_g": dgpm, "pre_ffn_g": dgpf,
              "post_ffn_g": dgpff, "ple_gate_b": dbg}
    on_small_grads(gsmall, loss_l)
    dwin_k = _matmul_tn("grad_w_in", a, dz, tn=384)
    return loss_l, dx, dwin_k, gsmall


def kernel(x, p, w_in, f_bias, sg_ln_g, sg_ln_b, sg_w, sg_b, att_out_g, sg_out_g, w_out, pre_mix_g, post_mix_g, pre_ffn_g, post_ffn_g, w_ff1, w_ff2, ple_w, ple_gate_w, ple_gate_b, loss_target, m_w_in, m_f_bias, m_sg_ln_g, m_sg_ln_b, m_sg_w, m_sg_b, m_att_out_g, m_sg_out_g, m_w_out, m_pre_mix_g, m_post_mix_g, m_pre_ffn_g, m_post_ffn_g, m_w_ff1, m_w_ff2, m_ple_w, m_ple_gate_w, m_ple_gate_b, v_w_in, v_f_bias, v_sg_ln_g, v_sg_ln_b, v_sg_w, v_sg_b, v_att_out_g, v_sg_out_g, v_w_out, v_pre_mix_g, v_post_mix_g, v_pre_ffn_g, v_post_ffn_g, v_w_ff1, v_w_ff2, v_ple_w, v_ple_gate_w, v_ple_gate_b):
    c = lax.axis_index("c")
    big = lambda t: (t[0][0], t[1][0], t[2][0], t[3][0], t[4][0], t[5][0])
    w_big = big((w_in, w_out, w_ff1, w_ff2, ple_w, ple_gate_w))
    m_big = big((m_w_in, m_w_out, m_w_ff1, m_w_ff2, m_ple_w, m_ple_gate_w))
    v_big = big((v_w_in, v_w_out, v_w_ff1, v_w_ff2, v_ple_w, v_ple_gate_w))
    small = {"sg_w": sg_w, "f_bias": f_bias, "sg_ln_g": sg_ln_g, "sg_ln_b": sg_ln_b, "sg_b": sg_b,
             "att_out_g": att_out_g, "sg_out_g": sg_out_g, "pre_mix_g": pre_mix_g, "post_mix_g": post_mix_g,
             "pre_ffn_g": pre_ffn_g, "post_ffn_g": post_ffn_g, "ple_gate_b": ple_gate_b}
    m_small = {"sg_w": m_sg_w, "f_bias": m_f_bias, "sg_ln_g": m_sg_ln_g, "sg_ln_b": m_sg_ln_b, "sg_b": m_sg_b,
               "att_out_g": m_att_out_g, "sg_out_g": m_sg_out_g, "pre_mix_g": m_pre_mix_g,
               "post_mix_g": m_post_mix_g, "pre_ffn_g": m_pre_ffn_g, "post_ffn_g": m_post_ffn_g,
               "ple_gate_b": m_ple_gate_b}
    v_small = {"sg_w": v_sg_w, "f_bias": v_f_bias, "sg_ln_g": v_sg_ln_g, "sg_ln_b": v_sg_ln_b, "sg_b": v_sg_b,
               "att_out_g": v_att_out_g, "sg_out_g": v_sg_out_g, "pre_mix_g": v_pre_mix_g,
               "post_mix_g": v_post_mix_g, "pre_ffn_g": v_pre_ffn_g, "post_ffn_g": v_post_ffn_g,
               "ple_gate_b": v_ple_gate_b}

    k_me = 2 * lax.axis_index("x") + lax.axis_index("y")
    own_slot = lambda got, mine: lax.dynamic_update_slice(got, mine[None], (k_me, 0, 0))
    late_mine = _pack_late(*w_big[1:]).astype(BF16)
    late = _gather_late_start(late_mine)
    win_mine = jnp.pad(w_big[0], ((0, 0), (0, 768 - 642))).reshape(768, 1024).astype(BF16)
    win_k = _win_kernel_order(own_slot(_gather_weights(win_mine), win_mine))
    late_weights = lambda after: _unpack_late(
        own_slot(_gather_late_wait(late[0], late[1], late[2], late[3], after), late_mine))

    names = ("w_in", "w_out", "w_ff1", "w_ff2", "ple_w", "ple_gate_w")
    c1 = jnp.reshape(c, (1,)).astype(jnp.int32)
    own_part = lambda parts, pss: [lax.dynamic_update_slice(pt, lax.dynamic_slice_in_dim(ps, k_me, 1, 0), (k_me, 0, 0))
                                   for pt, ps in zip(parts, pss)]
    tail = {}

    def on_ff_grads(dw1, dw2):
        tail["swap"] = _swap_start([dw1, dw2.reshape(4, D, D)], "ff")
        return tail["swap"][2][0]

    def on_tail_grads(grads):
        dwout, dplew, dwg = grads
        ws, wr, g_thru, land_thru = tail["swap"]
        (dw1, dw2), (got1, got2) = _swap_wait(ws, wr, g_thru, land_thru, dplew, "ff")
        rest = [dwout.reshape(4, 256, D), dplew, dwg.reshape(4, 256, D)]
        got_out, got_ple, got_gate = _swap_halves(rest, "late")
        gs = [rest[0], dw1, dw2, rest[1], rest[2]]
        gots = [got_out, got1, got2, got_ple, got_gate]
        pss = [_pair_sum(nm, c1, g, got) for nm, g, got in zip(names[1:], gs, gots)]
        tail["xch"] = _exchange_start(pss, "late")
        return tail["xch"][4]

    def on_small_grads(gsmall, loss_l):
        tail["small"] = _small_start(*_small_pack(gsmall, loss_l))

    loss_l, dx, dwin_k, gsmall = _local_step(
        x[0], p[0, 0], loss_target[0], win_k, late_weights, late[4], on_ff_grads, on_tail_grads, on_small_grads,
        small)

    dwin = jnp.concatenate([dwin_k[:, :3 * AW], dwin_k[:, 5 * AW:5 * AW + NH], dwin_k[:, 3 * AW:5 * AW]], axis=1)
    dwin = jnp.pad(jnp.swapaxes(dwin.reshape(D, 4, 642), 0, 1), ((0, 0), (0, 0), (0, 768 - 642)))
    ps_in = [_pair_sum(names[0], c1, dwin, _swap_halves([dwin], "in")[0])]
    ins, inr, in_thru, inland_thru, in_token = _exchange_start(ps_in, "in")
    xs, xr, ps_thru, land_thru, _ = tail["xch"]
    ps_late, landed = _exchange_wait(xs, xr, ps_thru, land_thru, in_token, "late")
    padded = lambda t: (jnp.pad(t[0], ((0, 0), (0, 768 - 642))),) + tuple(t[1:])

    def finish(nms, parts, tag, w, m, v):
        ghs = [_reduce_chips(nm, pt) for nm, pt in zip(nms, parts)]
        got2 = _share_grad(ghs, tag)
        return [_update(nm, c1, gh, g2, wi, mi, vi) for nm, gh, g2, wi, mi, vi in zip(nms, ghs, got2, w, m, v)]

    late_out = finish(names[1:], own_part(landed, ps_late), "late", w_big[1:], m_big[1:], v_big[1:])
    late_done = jnp.stack([o[3][0, 0] for o in late_out]).reshape(1, len(late_out))
    ps_in, landed_in = _exchange_wait(ins, inr, in_thru, inland_thru, late_done, "in")
    big_out = finish(names[:1], own_part(landed_in, ps_in), "in", *[padded(t)[:1] for t in (w_big, m_big, v_big)])
    big_out += late_out
    big_out = [[big_out[j][i][:, :642] if j == 0 else big_out[j][i] for j in range(6)] for i in range(4)]

    view = lambda t: t.reshape(t.shape[-3:]) if t.ndim == 4 else t.reshape(t.shape[-2:])
    views = lambda d: {k: view(d[k]) for k in SMALL_NAMES}
    me = 4 * lax.axis_index("x") + 2 * lax.axis_index("y") + c
    ss, sr, sthru, _ = tail["small"]
    vec, w8, lv, lw = _small_wait(ss, sr, sthru, big_out[0][3])
    all_v = lax.dynamic_update_slice(lv, vec[None], (me, 0, 0))
    all_w = lax.dynamic_update_slice(lw, w8[None], (me, 0, 0, 0))
    loss11, res_s = _small_update(all_v, all_w, views(small), views(m_small), views(v_small))
    loss = loss11[0, 0]

    def small_out(i, name):
        return res_s[name][i].reshape(small[name].shape)

    order = ["w_in", "f_bias", "sg_ln_g", "sg_ln_b", "sg_w", "sg_b", "att_out_g", "sg_out_g", "w_out",
             "pre_mix_g", "post_mix_g", "pre_ffn_g", "post_ffn_g", "w_ff1", "w_ff2", "ple_w", "ple_gate_w",
             "ple_gate_b"]
    big_idx = {"w_in": 0, "w_out": 1, "w_ff1": 2, "w_ff2": 3, "ple_w": 4, "ple_gate_w": 5}
    outs = [loss, dx[None]]
    for i in range(4):
        for name in order:
            if name in big_idx:
                outs.append(big_out[i][big_idx[name]][None])
            else:
                outs.append(small_out(i, name))
    return tuple(outs)
```

```python
import math

import jax
import jax.numpy as jnp
from jax import lax
from jax.experimental import pallas as pl
from jax.experimental.pallas import tpu as pltpu

F32 = jnp.float32
BF16 = jnp.bfloat16
MESH = pl.DeviceIdType.MESH

D = 1024
DH = 64
NH = 8
AW = 512
CH = 128
DFF = 4096
ZW = 5 * AW + 128
EPS = 1e-6
NEG = -1e30
MASKED = -2e30

TM = 256
TQ = 256

LR, B1, B2, AEPS, WD, STEP = 0.001, 0.9, 0.999, 1e-08, 0.01, 10
BC1 = 1.0 - B1 ** STEP
BC2 = 1.0 - B2 ** STEP

VEC_NAMES = ("f_bias", "sg_ln_g", "sg_ln_b", "sg_b", "att_out_g", "sg_out_g", "pre_mix_g",
             "post_mix_g", "pre_ffn_g", "post_ffn_g", "ple_gate_b")


def _dot(a, b):
    return jnp.dot(a, b, preferred_element_type=F32)


def _dot_nt(a, b):
    return lax.dot_general(a, b, (((1,), (1,)), ((), ())), preferred_element_type=F32)


def _dot_tn(a, b):
    return lax.dot_general(a, b, (((0,), (0,)), ((), ())), preferred_element_type=F32)


def _split3(x):
    h = x.astype(BF16)
    r = x - h.astype(F32)
    m = r.astype(BF16)
    l = (r - m.astype(F32)).astype(BF16)
    return h, m, l


def _dot01(sel, x):
    h, m, l = _split3(x)
    return _dot(sel, h) + _dot(sel, m) + _dot(sel, l)


def _dot01_r(x, sel):
    h, m, l = _split3(x)
    return _dot(h, sel) + _dot(m, sel) + _dot(l, sel)


def _dot01_tn(x, sel):
    h, m, l = _split3(x)
    return _dot_tn(h, sel) + _dot_tn(m, sel) + _dot_tn(l, sel)


def _rs(x, n):
    return lax.rsqrt(jnp.sum(x * x, axis=-1, keepdims=True) * (1.0 / n) + EPS)


def _rms_bwd(dn, x, rs, g, n):
    w = dn * g
    dx = rs * w - x * ((rs * rs * rs) * (1.0 / n) * jnp.sum(w * x, axis=-1, keepdims=True))
    return dx, jnp.sum(dn * x * rs, axis=0, keepdims=True)


_GC = math.sqrt(2.0 / math.pi)


def _gelu(x):
    t = jnp.tanh(_GC * (x + 0.044715 * x * x * x))
    return 0.5 * x * (1.0 + t), t


def _gelu_grad(x, t):
    return 0.5 * (1.0 + t) + 0.5 * x * (1.0 - t * t) * (_GC * (1.0 + 3.0 * 0.044715 * x * x))


def _params(vmem_mb, sem=("arbitrary",)):
    return pltpu.CompilerParams(dimension_semantics=sem, vmem_limit_bytes=vmem_mb * 1024 * 1024)


def _row_call(name, body, T, tm, tiled, resident, outs, accs, scratch=(), reverse=False, vmem_mb=48):
    nt = T // tm
    n_t, n_r, n_o, n_a = len(tiled), len(resident), len(outs), len(accs)

    def kern(*refs):
        t_refs = refs[:n_t]
        r_hbm = refs[n_t:n_t + n_r]
        o_refs = refs[n_t + n_r:n_t + n_r + n_o]
        a_refs = refs[n_t + n_r + n_o:n_t + n_r + n_o + n_a]
        r_vmem = refs[n_t + n_r + n_o + n_a:n_t + 2 * n_r + n_o + n_a]
        s_refs = refs[n_t + 2 * n_r + n_o + n_a:]

        @pl.when(pl.program_id(0) == 0)
        def _():
            for h, v in zip(r_hbm, r_vmem):
                pltpu.sync_copy(h, v)
            for a in a_refs + s_refs:
                a[...] = jnp.zeros(a.shape, a.dtype)

        body(t_refs, r_vmem, o_refs, a_refs, s_refs)

    if reverse:
        idx = lambda i: (nt - 1 - i, 0)
        idx_t = lambda i: (nt - 1 - i, 0, 0)
    else:
        idx = lambda i: (i, 0)
        idx_t = lambda i: (i, 0, 0)
    arrays, in_specs = [], []
    for a in tiled:
        if isinstance(a, tuple):
            arrays.append(a[0])
            in_specs.append(pl.BlockSpec((None, a[0].shape[1], tm), idx_t))
        else:
            arrays.append(a)
            in_specs.append(pl.BlockSpec((tm, a.shape[1]), idx))
    in_specs += [pl.BlockSpec(memory_space=pl.ANY) for _ in resident]
    out_shape, out_specs = [], []
    for o in outs:
        if len(o) == 3:
            out_shape.append(jax.ShapeDtypeStruct((nt, o[0], tm), o[1]))
            out_specs.append(pl.BlockSpec((None, o[0], tm), idx_t))
        else:
            out_shape.append(jax.ShapeDtypeStruct((T, o[0]), o[1]))
            out_specs.append(pl.BlockSpec((tm, o[0]), idx))
    out_shape += [jax.ShapeDtypeStruct(s, F32) for s in accs]
    out_specs += [pl.BlockSpec(s, lambda i, n=len(s): (0,) * n) for s in accs]
    scratch_shapes = [pltpu.VMEM(r.shape, r.dtype) for r in resident]
    scratch_shapes += [pltpu.VMEM(s, F32) for s in scratch]
    return pl.pallas_call(
        kern, name=name, grid=(nt,), in_specs=in_specs, out_specs=out_specs, out_shape=out_shape,
        scratch_shapes=scratch_shapes, compiler_params=_params(vmem_mb),
    )(*arrays, *resident)


def _sg_forward(zu, zv, wm_ref, bsg, lng, lnb, mixed_ref, tm):
    gu, tu = _gelu(zu)
    vg, tv = _gelu(zv)
    mu = jnp.sum(vg, axis=-1, keepdims=True) * (1.0 / AW)
    xc = vg - mu
    rstd = lax.rsqrt(jnp.sum(xc * xc, axis=-1, keepdims=True) * (1.0 / AW) + EPS)
    xhat = xc * rstd
    vvb = (xhat * lng + lnb).astype(BF16)
    lane = lax.broadcasted_iota(jnp.int32, (CH, 128), 1)
    for c in range(tm // CH):
        for j in range(4):
            blk = vvb[c * CH:(c + 1) * CH, j * 128:(j + 1) * 128]
            m0 = _dot(wm_ref[2 * j], blk)
            m1 = _dot(wm_ref[2 * j + 1], blk)
            mixed_ref[c * CH:(c + 1) * CH, j * 128:(j + 1) * 128] = (
                jnp.where(lane < DH, m0, m1) + bsg[:, j * 128:(j + 1) * 128])
    return gu, tu, tv, xhat, rstd, vvb, mixed_ref[...]


def _head_consts():
    src = jnp.arange(AW)
    dst = (src // DH) * 128 + src % DH
    wide = jnp.arange(NH * 128)
    expand = (dst[:, None] == wide[None, :]).astype(BF16)
    heads = jnp.arange(128)
    pieces = jnp.stack([((heads[:, None] * 128 + DH + i == wide[None, :]) & (heads[:, None] < NH)).astype(BF16)
                        for i in range(3)])
    spare = wide % 128 - DH
    qconst = jnp.where((spare >= 0) & (spare < 3), -1.0, 0.0).astype(F32)[None, :]
    one64 = jnp.where(spare == 0, 1.0, 0.0).astype(F32)[None, :]
    one67 = jnp.where(spare == 3, 1.0, 0.0).astype(F32)[None, :]
    pick64 = ((wide[:, None] == heads[None, :] * 128 + DH) & (heads[None, :] < NH)).astype(BF16)
    pick67 = ((wide[:, None] == heads[None, :] * 128 + DH + 3) & (heads[None, :] < NH)).astype(BF16)
    return expand, expand.T, pieces, qconst, one64, one67, pick64, pick67


def _masked_sg_w(sg_w):
    r = lax.broadcasted_iota(jnp.int32, (CH, CH), 0)
    c = lax.broadcasted_iota(jnp.int32, (CH, CH), 1)
    return jnp.where((c <= r)[None], sg_w, 0.0)


def _pre_attn_fwd(x, gpre, win, fbias, lng, lnb, wm, bsg, gsg, expand, pieces, qconst, kconst, vconst):
    T = x.shape[0]
    tm = TM

    def body(t, r, o, a, s):
        (x_ref,) = t
        gpre_r, win_r, fb_r, lng_r, lnb_r, wm_r, bsg_r, gsg_r, ex_r, pc_r, qc_r, kc_r, vc_r = r
        a_o, flog_o, zuv_o, ysgn_o, q8_o, k8_o, v8_o = o
        carry_ref, mixed_ref = s
        xv = x_ref[...]
        av = (xv * _rs(xv, D) * gpre_r[...]).astype(BF16)
        a_o[...] = av
        z = _dot(av, win_r[...])
        zu = z[:, 3 * AW:4 * AW]
        zv = z[:, 4 * AW:5 * AW]
        zuv_o[:, 0:AW] = zu
        zuv_o[:, AW:2 * AW] = zv
        zf = z[:, 5 * AW:] + fb_r[...]
        flog_o[...] = zf
        lane = lax.broadcasted_iota(jnp.int32, (tm, 128), 1)
        logf = jnp.where(lane < NH, jnp.minimum(zf, 0.0) - jnp.log(1.0 + jnp.exp(-jnp.abs(zf))), 0.0)
        rr = lax.broadcasted_iota(jnp.int32, (tm, tm), 0)
        cc = lax.broadcasted_iota(jnp.int32, (tm, tm), 1)
        tri = (cc <= rr).astype(BF16)
        cum = _dot01(tri, logf) + carry_ref[...]
        carry_ref[...] = cum[tm - 1:tm, :]
        ex = ex_r[...]
        q8_o[...] = (_dot((z[:, 0:AW] * (DH ** -0.5)).astype(BF16), ex) + qc_r[...]).astype(BF16)
        ch, cm, cl = _split3(cum)
        k8_o[...] = (_dot(z[:, AW:2 * AW].astype(BF16), ex) + _dot(ch, pc_r[0]) + _dot(cm, pc_r[1])
                     + _dot(cl, pc_r[2]) + kc_r[...]).astype(BF16)
        v8_o[...] = (_dot(z[:, 2 * AW:3 * AW].astype(BF16), ex) + vc_r[...]).astype(BF16)
        gu, _, _, _, _, _, mixed = _sg_forward(zu, zv, wm_r, bsg_r[...], lng_r[...], lnb_r[...], mixed_ref, tm)
        ysg = gu * mixed
        ysgn_o[...] = (ysg * _rs(ysg, AW) * gsg_r[...]).astype(BF16)

    return _row_call(
        "pre_attn_fwd", body, T, tm, [x],
        [gpre, win, fbias, lng, lnb, wm, bsg, gsg, expand, pieces, qconst, kconst, vconst],
        [(D, BF16), (128, F32), (2 * AW, F32), (AW, BF16), (NH * 128, BF16), (NH * 128, BF16), (NH * 128, BF16)], [],
        scratch=[(1, 128), (tm, AW)], vmem_mb=48)


def _flash_fwd(qt8, k8, vt8, sel):
    T = k8.shape[0]
    nq = T // TQ

    def body(qt_ref, k_ref, vt_ref, sel_ref, o_ref, l_ref, u_scr, p_scr):
        qi = pl.program_id(1)
        qts = (qt_ref[0:128, :], qt_ref[128:256, :])
        dmat = (lax.broadcasted_iota(jnp.int32, (TQ, TQ), 0) - lax.broadcasted_iota(jnp.int32, (TQ, TQ), 1))
        u_scr[1] = jnp.full((2, TQ, TQ), MASKED, F32)
        p_scr[...] = jnp.zeros(p_scr.shape, BF16)

        def sub(t, carry, sc, sb, masked):
            blk_c = jnp.clip(t - 2, 0, qi)
            off_a = pl.multiple_of(jnp.minimum(t, qi) * TQ, TQ)
            new = []
            for j in (0, 1):
                m, al, acc = carry[j]
                acc = al * acc + _dot(vt_ref[blk_c, j * 128:(j + 1) * 128, :], p_scr[sc, j])
                m_new = jnp.maximum(m, jnp.max(u_scr[sb, j], axis=0, keepdims=True))
                p_scr[sb, j] = jnp.exp(u_scr[sb, j] - m_new).astype(BF16)
                u = _dot(k_ref[pl.ds(off_a, TQ), j * 128:(j + 1) * 128], qts[j])
                u_scr[sc, j] = jnp.where(dmat <= (qi - t) * TQ, u, MASKED) if masked else u
                new.append((m_new, jnp.exp(m - m_new), acc))
            return tuple(new)

        def pair(t2, carry, masked):
            return sub(2 * t2 + 1, sub(2 * t2, carry, 0, 1, masked), 1, 0, masked)

        init = tuple((jnp.full((1, TQ), NEG, F32), jnp.ones((1, TQ), F32), jnp.zeros((128, TQ), F32))
                     for _ in (0, 1))
        carry = lax.fori_loop(0, qi // 2, lambda t2, cr: pair(t2, cr, False), init)
        (m0, _, a0), (m1, _, a1) = pair(qi // 2 + 1, pair(qi // 2, carry, True), True)
        l0 = a0[DH:DH + 1, :]
        l1 = a1[DH:DH + 1, :]
        o_ref[...] = _dot01_tn(a0 * (1.0 / l0), sel_ref[0]) + _dot01_tn(a1 * (1.0 / l1), sel_ref[1])
        l_ref[0:1, :] = m0 + jnp.log(l0)
        l_ref[1:2, :] = m1 + jnp.log(l1)
        l_ref[2:8, :] = jnp.zeros((6, TQ), F32)

    return pl.pallas_call(
        body, name="flash_fwd", grid=(4, nq),
        in_specs=[pl.BlockSpec((None, 256, TQ), lambda h, i: (i, h, 0)),
                  pl.BlockSpec((T, 256), lambda h, i: (0, h)),
                  pl.BlockSpec((nq, 256, TQ), lambda h, i: (0, h, 0)),
                  pl.BlockSpec((2, 128, 128), lambda h, i: (0, 0, 0))],
        out_specs=[pl.BlockSpec((TQ, 128), lambda h, i: (i, h)),
                   pl.BlockSpec((None, 8, TQ), lambda h, i: (h, 0, i))],
        out_shape=[jax.ShapeDtypeStruct((T, AW), F32), jax.ShapeDtypeStruct((4, 8, T), F32)],
        scratch_shapes=[pltpu.VMEM((2, 2, TQ, TQ), F32), pltpu.VMEM((2, 2, TQ, TQ), BF16)],
        compiler_params=_params(40, ("arbitrary", "arbitrary")),
    )(qt8, k8, vt8, sel)


def _flash_bwd(q8, qt8, k8, v8, do8, dot8, lse, dlt):
    T = q8.shape[0]
    nk = T // TQ

    def body(q_ref, qt_ref, k_ref, v_ref, do_ref, dot_ref, l_ref, d_ref, dqt_ref, dk_ref, dv_ref,
             u_scr, dp_scr, p_scr, ds_scr):
        kb = pl.program_id(1)
        n = nk - kb

        @pl.when(kb == 0)
        def _():
            dqt_ref[...] = jnp.zeros(dqt_ref.shape, F32)

        dk_ref[...] = jnp.zeros(dk_ref.shape, F32)
        dv_ref[...] = jnp.zeros(dv_ref.shape, F32)
        u_scr[1] = jnp.full((2, TQ, TQ), MASKED, F32)
        dp_scr[1] = jnp.zeros((2, TQ, TQ), F32)
        p_scr[...] = jnp.zeros(p_scr.shape, BF16)
        ds_scr[...] = jnp.zeros(ds_scr.shape, BF16)
        dmat = (lax.broadcasted_iota(jnp.int32, (TQ, TQ), 0) - lax.broadcasted_iota(jnp.int32, (TQ, TQ), 1))
        ks = (k_ref[:, 0:128], k_ref[:, 128:256])
        vs = (v_ref[:, 0:128], v_ref[:, 128:256])

        def sub(t, sc, sb):
            blk_a = kb + jnp.minimum(t, n - 1)
            blk_c = kb + jnp.clip(t - 2, 0, n - 1)
            off_b = pl.multiple_of((kb + jnp.clip(t - 1, 0, n - 1)) * TQ, TQ)
            off_c = pl.multiple_of(blk_c * TQ, TQ)
            lim = jnp.where(t < n, t * TQ, -TQ)
            for j in (0, 1):
                hl = slice(j * 128, (j + 1) * 128)
                dqt_ref[blk_c, hl, :] += _dot_tn(ks[j], ds_scr[sc, j])
                dk_ref[:, hl] += _dot(ds_scr[sc, j], q_ref[pl.ds(off_c, TQ), hl])
                dv_ref[:, hl] += _dot(p_scr[sc, j], do_ref[pl.ds(off_c, TQ), hl])
                p = jnp.exp(u_scr[sb, j] - l_ref[j:j + 1, pl.ds(off_b, TQ)])
                p_scr[sb, j] = p.astype(BF16)
                ds_scr[sb, j] = (p * (dp_scr[sb, j] - d_ref[j:j + 1, pl.ds(off_b, TQ)])).astype(BF16)
                u_scr[sc, j] = jnp.where(dmat <= lim, _dot(ks[j], qt_ref[blk_a, hl, :]), MASKED)
                dp_scr[sc, j] = _dot(vs[j], dot_ref[blk_a, hl, :])

        def it(t2, carry):
            sub(2 * t2, 0, 1)
            sub(2 * t2 + 1, 1, 0)
            return carry

        lax.fori_loop(0, n // 2 + 1, it, 0)

        @pl.when(n % 2 == 1)
        def _():
            sub(n + 1, 0, 1)

    return pl.pallas_call(
        body, name="flash_bwd", grid=(4, nk),
        in_specs=[pl.BlockSpec((T, 256), lambda h, i: (0, h)),
                  pl.BlockSpec((nk, 256, TQ), lambda h, i: (0, h, 0)),
                  pl.BlockSpec((TQ, 256), lambda h, i: (i, h)),
                  pl.BlockSpec((TQ, 256), lambda h, i: (i, h)),
                  pl.BlockSpec((T, 256), lambda h, i: (0, h)),
                  pl.BlockSpec((nk, 256, TQ), lambda h, i: (0, h, 0)),
                  pl.BlockSpec((None, 8, T), lambda h, i: (h, 0, 0)),
                  pl.BlockSpec((None, 8, T), lambda h, i: (h, 0, 0))],
        out_specs=[pl.BlockSpec((nk, 256, TQ), lambda h, i: (0, h, 0)),
                   pl.BlockSpec((TQ, 256), lambda h, i: (i, h)),
                   pl.BlockSpec((TQ, 256), lambda h, i: (i, h))],
        out_shape=[jax.ShapeDtypeStruct((nk, NH * 128, TQ), F32), jax.ShapeDtypeStruct((T, NH * 128), F32),
                   jax.ShapeDtypeStruct((T, NH * 128), F32)],
        scratch_shapes=[pltpu.VMEM((2, 2, TQ, TQ), F32), pltpu.VMEM((2, 2, TQ, TQ), F32),
                        pltpu.VMEM((2, 2, TQ, TQ), BF16), pltpu.VMEM((2, 2, TQ, TQ), BF16)],
        compiler_params=_params(56, ("arbitrary", "arbitrary")),
    )(q8, qt8, k8, v8, do8, dot8, lse, dlt)


def _tail_fwd1(x, yatt, ysgn, gatt, wout, gpm, gpf, w1):
    T = x.shape[0]

    def body(t, r, o, a, s):
        x_ref, ya_ref, ys_ref = t
        gatt_r, wout_r, gpm_r, gpf_r, w1_r = r
        y_o, o_o, h1_o, c2_o, s_o, rr_o = o
        ya = ya_ref[...]
        yan = (ya * _rs(ya, AW) * gatt_r[...]).astype(BF16)
        y_o[:, 0:AW] = yan
        y_o[:, AW:] = ys_ref[...]
        ov = _dot(yan, wout_r[0:AW, :]) + _dot(ys_ref[...], wout_r[AW:, :])
        o_o[...] = ov
        h1 = x_ref[...] + ov * _rs(ov, D) * gpm_r[...]
        h1_o[...] = h1
        c2 = (h1 * _rs(h1, D) * gpf_r[...]).astype(BF16)
        c2_o[...] = c2
        for k in range(4):
            rr = jnp.maximum(_dot(c2, w1_r[k]), 0.0)
            rr_o[:, k * D:(k + 1) * D] = rr.astype(BF16)
            s_o[:, k * D:(k + 1) * D] = (rr * rr).astype(BF16)

    return _row_call(
        "tail_fwd1", body, T, TM, [x, yatt, ysgn], [gatt, wout, gpm, gpf, w1],
        [(D, BF16), (D, F32), (D, F32), (D, BF16), (DFF, BF16), (DFF, BF16)], [], vmem_mb=48)


def _tail_fwd2(sact, h1, p, tgt, w2, gpff, wg, bg, wpe):
    T = h1.shape[0]

    def body(t, r, o, a, s):
        s_ref, h1_ref, p_ref, t_ref = t
        w2_r, gpff_r, wg_r, bg_r, wpe_r = r
        ff_o, h2b_o, de_o, dpre_o, dh2_o = o
        loss_a, dbg_a = a
        ff = _dot(s_ref[...], w2_r[...])
        ff_o[...] = ff
        h2 = h1_ref[...] + ff * _rs(ff, D) * gpff_r[...]
        h2b = h2.astype(BF16)
        h2b_o[...] = h2b
        gate = 1.0 / (1.0 + jnp.exp(-(_dot(h2b, wg_r[...]) + bg_r[...])))
        pb = p_ref[...].astype(BF16)
        e = jnp.concatenate([_dot(pb, wpe_r[k]) for k in range(4)], axis=1)
        diff = h2 + gate * e - t_ref[...]
        loss_a[...] += jnp.sum(diff * diff, axis=0, keepdims=True)
        dh3 = diff * (1.0 / D)
        de_o[...] = (dh3 * gate).astype(BF16)
        dpre = dh3 * e * gate * (1.0 - gate)
        dbg_a[...] += jnp.sum(dpre, axis=0, keepdims=True)
        dpb = dpre.astype(BF16)
        dpre_o[...] = dpb
        dh2_o[...] = dh3 + _dot_nt(dpb, wg_r[...])

    return _row_call(
        "tail_fwd2", body, T, TM, [sact, h1, p, tgt], [w2, gpff, wg, bg, wpe],
        [(D, F32), (D, BF16), (D, BF16), (D, BF16), (D, F32)], [(1, D), (1, D)], vmem_mb=48)


def _tail_bwd(dh2, ff, rr, h1, ov, yatt, w2, w1, wout, gpff, gpf, gpm, gatt, gsel, expand):
    T = dh2.shape[0]

    def body(t, r, o, a, s):
        dh2_ref, ff_ref, rr_ref, h1_ref, o_ref, ya_ref = t
        w2_r, w1_r, wout_r, gpff_r, gpf_r, gpm_r, gatt_r, gsel_r, ex_r = r
        dff_o, dr_o, do_o, do8_o, dlt_o, dysg_o, dh1_o = o
        dgpff_a, dgpf_a, dgpm_a, dgatt_a = a
        dh2v = dh2_ref[...]
        ffv = ff_ref[...]
        dff, dg = _rms_bwd(dh2v, ffv, _rs(ffv, D), gpff_r[...], D)
        dgpff_a[...] += dg
        dffb = dff.astype(BF16)
        dff_o[...] = dffb
        drb = (_dot_nt(dffb, w2_r[...]) * (2.0 * rr_ref[...].astype(F32))).astype(BF16)
        dr_o[...] = drb
        dc2 = _dot_nt(drb[:, 0:D], w1_r[0])
        for k in range(1, 4):
            dc2 = dc2 + _dot_nt(drb[:, k * D:(k + 1) * D], w1_r[k])
        h1v = h1_ref[...]
        d1, dg = _rms_bwd(dc2, h1v, _rs(h1v, D), gpf_r[...], D)
        dgpf_a[...] += dg
        dh1 = dh2v + d1
        dh1_o[...] = dh1
        ovv = o_ref[...]
        dov, dg = _rms_bwd(dh1, ovv, _rs(ovv, D), gpm_r[...], D)
        dgpm_a[...] += dg
        dob = dov.astype(BF16)
        do_o[...] = dob
        dysg_o[...] = _dot_nt(dob, wout_r[AW:, :])
        dyan = _dot_nt(dob, wout_r[0:AW, :])
        ya = ya_ref[...]
        dya, dg = _rms_bwd(dyan, ya, _rs(ya, AW), gatt_r[...], AW)
        dgatt_a[...] += dg
        do8_o[...] = _dot(dya.astype(BF16), ex_r[...]).astype(BF16)
        dlt_o[...] = _dot01_r(dya * ya, gsel_r[...])

    return _row_call(
        "tail_bwd", body, T, TM, [dh2, ff, rr, h1, ov, yatt],
        [w2, w1, wout, gpff, gpf, gpm, gatt, gsel, expand],
        [(D, BF16), (DFF, BF16), (D, BF16), (NH * 128, BF16), (128, F32), (AW, F32), (D, F32)],
        [(1, D), (1, D), (1, D), (1, AW)], vmem_mb=56)


def _pre_attn_bwd(x, dh1, dq8, dk8, dv8, flog, zuv, dysg, gpre, win, lng, lnb, wm, wmt, bsg, gsg, gsel, shrink, pick64, pick67):
    T = x.shape[0]
    tm = TM

    def body(t, r, o, a, s):
        x_ref, dh1_ref, dq_ref, dk_ref, dv_ref, fl_ref, zuv_ref, dys_ref = t
        gpre_r, win_r, lng_r, lnb_r, wm_r, wmt_r, bsg_r, gsg_r, gsel_r, sh_r, p64_r, p67_r = r
        dx_o, dz_o = o
        dgpre_a, dfb_a, dgsg_a, dlng_a, dlnb_a, dws_a, dbs_a, dsb_a = a
        carry_ref, mixed_ref, dvv_ref = s
        dq8v = dq_ref[...]
        dk8v = dk_ref[...]
        dcv = _dot01_r(dq8v, p67_r[...]) + _dot01_r(dk8v, p64_r[...])
        rr = lax.broadcasted_iota(jnp.int32, (tm, tm), 0)
        cc = lax.broadcasted_iota(jnp.int32, (tm, tm), 1)
        triu = (cc >= rr).astype(BF16)
        dlogf = _dot01(triu, dcv) + carry_ref[...]
        carry_ref[...] = dlogf[0:1, :]
        dzf = dlogf * (1.0 / (1.0 + jnp.exp(fl_ref[...])))
        dfb_a[...] += jnp.sum(dzf, axis=0, keepdims=True)
        dz_o[:, 5 * AW:] = dzf.astype(BF16)
        zu = zuv_ref[:, 0:AW]
        zv = zuv_ref[:, AW:]
        gu, tu, tv, xhat, rstd, vvb, mixed = _sg_forward(
            zu, zv, wm_r, bsg_r[...], lng_r[...], lnb_r[...], mixed_ref, tm)
        ysg = gu * mixed
        dysg_n = dys_ref[...]
        dys, dg = _rms_bwd(dysg_n, ysg, _rs(ysg, AW), gsg_r[...], AW)
        dgsg_a[...] += dg
        dgu = dys * mixed
        dmix = dys * gu
        dmb = dmix.astype(BF16)
        lane = lax.broadcasted_iota(jnp.int32, (CH, 128), 1)
        lo = lane < DH
        for c in range(tm // CH):
            rows = slice(c * CH, (c + 1) * CH)
            dbs_a[...] += dmix[rows, :]
            for j in range(4):
                cols = slice(j * 128, (j + 1) * 128)
                dmblk = dmb[rows, cols]
                vblk = vvb[rows, cols]
                d0 = _dot(wmt_r[2 * j], dmblk)
                d1 = _dot(wmt_r[2 * j + 1], dmblk)
                dvv_ref[rows, cols] = jnp.where(lo, d0, d1)
                dws_a[2 * j] += _dot_nt(jnp.where(lo, dmblk, jnp.zeros_like(dmblk)), vblk)
                dws_a[2 * j + 1] += _dot_nt(jnp.where(lo, jnp.zeros_like(dmblk), dmblk), vblk)
        dvv = dvv_ref[...]
        dlng_a[...] += jnp.sum(dvv * xhat, axis=0, keepdims=True)
        dlnb_a[...] += jnp.sum(dvv, axis=0, keepdims=True)
        dxh = dvv * lng_r[...]
        dvg = rstd * (dxh - jnp.sum(dxh, axis=-1, keepdims=True) * (1.0 / AW)
                      - xhat * (jnp.sum(dxh * xhat, axis=-1, keepdims=True) * (1.0 / AW)))
        dz_o[:, 3 * AW:4 * AW] = (dgu * _gelu_grad(zu, tu)).astype(BF16)
        dz_o[:, 4 * AW:5 * AW] = (dvg * _gelu_grad(zv, tv)).astype(BF16)
        dz_o[:, 0:AW] = _dot((dq8v * (DH ** -0.5)).astype(BF16), sh_r[...]).astype(BF16)
        dz_o[:, AW:2 * AW] = _dot(dk8v.astype(BF16), sh_r[...]).astype(BF16)
        dz_o[:, 2 * AW:3 * AW] = _dot(dv_ref[...].astype(BF16), sh_r[...]).astype(BF16)
        da = _dot_nt(dz_o[...], win_r[...])
        xv = x_ref[...]
        dxa, dg = _rms_bwd(da, xv, _rs(xv, D), gpre_r[...], D)
        dgpre_a[...] += dg
        dx_o[...] = dh1_ref[...] + dxa

        @pl.when(pl.program_id(0) == T // tm - 1)
        def _():
            dsb_a[...] = _dot01_r(dbs_a[...], gsel_r[...])

    outs = _row_call(
        "pre_attn_bwd", body, T, tm, [x, dh1, dq8, dk8, dv8, flog, zuv, dysg],
        [gpre, win, lng, lnb, wm, wmt, bsg, gsg, gsel, shrink, pick64, pick67],
        [(D, F32), (ZW, BF16)],
        [(1, D), (1, 128), (1, AW), (1, AW), (1, AW), (8, CH, CH), (CH, AW), (CH, 128)],
        scratch=[(1, 128), (tm, AW), (tm, AW)], reverse=True, vmem_mb=48)
    return outs


def _matmul_tn(name, a, b, tn=512, tt=2048, shards=1, after=None):
    T, K = a.shape
    N = b.shape[1]
    tk = min(K, 1024)
    tn = min(tn, N // shards)
    tt = min(tt, T)
    nj = N // shards // tn

    def body(a_ref, b_ref, *rest):
        o_ref = rest[-1]

        @pl.when(pl.program_id(2) == 0)
        def _():
            o_ref[...] = jnp.zeros(o_ref.shape, F32)

        o_ref[...] += _dot_tn(a_ref[...].astype(BF16), b_ref[...].astype(BF16))

    ordered = after is not None

    if shards == 1:
        out_shape = jax.ShapeDtypeStruct((K, N), F32)
        out_spec = pl.BlockSpec((tk, tn), lambda i, j, t: (i, j))
    else:
        out_shape = jax.ShapeDtypeStruct((shards, K, N // shards), F32)
        out_spec = pl.BlockSpec((None, tk, tn), lambda i, j, t: (j // nj, i, j % nj))
    return pl.pallas_call(
        body, name=name, grid=(K // tk, N // tn, T // tt),
        in_specs=[pl.BlockSpec((tt, tk), lambda i, j, t: (t, i)),
                  pl.BlockSpec((tt, tn), lambda i, j, t: (t, j))] + [pl.BlockSpec(memory_space=pl.ANY)] * ordered,
        out_specs=out_spec, out_shape=out_shape,
        compiler_params=_params(40, ("arbitrary", "arbitrary", "arbitrary")),
    )(a, b, *([after] * ordered))


def _me():
    return lax.axis_index("x"), lax.axis_index("y"), lax.axis_index("c")


HBM_SPEC = pl.BlockSpec(memory_space=pltpu.HBM)


def _gather_weights(mine):
    half = mine.shape[0] // 2

    def body(mine_ref, out_ref, ici_send, ici_recv, d2d_send, d2d_recv):
        x, y, c = _me()
        k_me = 2 * x + y
        chips = [(1 - x, y), (x, 1 - y), (1 - x, 1 - y)]
        my_rows = pl.ds(pl.multiple_of(c * half, 16), half)
        sib_rows = pl.ds(pl.multiple_of((1 - c) * half, 16), half)

        def over_ici(j, k, to):
            src = mine_ref.at[my_rows] if k is None else out_ref.at[k, my_rows]
            return pltpu.make_async_remote_copy(
                src_ref=src, dst_ref=out_ref.at[k_me if k is None else k, my_rows], send_sem=ici_send.at[j],
                recv_sem=ici_recv.at[j], device_id=to, device_id_type=MESH)

        def over_d2d(j, k, rows):
            return pltpu.make_async_remote_copy(
                src_ref=out_ref.at[k, rows], dst_ref=out_ref.at[k, rows], send_sem=d2d_send.at[j],
                recv_sem=d2d_recv.at[j], device_id=(x, y, 1 - c), device_id_type=MESH)

        first = [over_ici(j, None, (cx, cy, c)) for j, (cx, cy) in enumerate(chips)]
        for cp in first:
            cp.start()
        passed = [over_d2d(j, 2 * cx + cy, my_rows) for j, (cx, cy) in enumerate(chips)]
        for j, (cx, cy) in enumerate(chips):
            over_ici(j, 2 * cx + cy, (cx, cy, c)).wait_recv()
            passed[j].start()
        for j, (cx, cy) in enumerate(chips):
            over_d2d(j, 2 * cx + cy, sib_rows).wait_recv()
        for cp in first + passed:
            cp.wait_send()

    return pl.pallas_call(
        body, name="gather_weights", in_specs=[HBM_SPEC], out_specs=HBM_SPEC,
        out_shape=jax.ShapeDtypeStruct((4,) + mine.shape, mine.dtype),
        scratch_shapes=[pltpu.SemaphoreType.DMA((3,)), pltpu.SemaphoreType.DMA((3,)), pltpu.SemaphoreType.DMA((3,)),
                        pltpu.SemaphoreType.DMA((3,))],
    )(mine)


SEM_SPEC = pl.BlockSpec(memory_space=pltpu.SEMAPHORE)
EFFECT = pltpu.SideEffectType.DATAFLOW_SIDE_EFFECTING


def _gather_late_start(mine):
    def body(mine_ref, land_ref, send_sems, recv_sems, mine_thru, land_thru, token):
        x, y, c = _me()
        k_me = 2 * x + y
        for j, (cx, cy) in enumerate([(1 - x, y), (x, 1 - y), (1 - x, 1 - y)]):
            pltpu.make_async_remote_copy(
                src_ref=mine_ref, dst_ref=land_ref.at[k_me], send_sem=send_sems.at[j], recv_sem=recv_sems.at[j],
                device_id=(cx, cy, c), device_id_type=MESH).start()
        token[...] = jnp.zeros(token.shape, F32)

    land = lax.empty((4,) + mine.shape, mine.dtype)
    return pl.pallas_call(
        body, name="gather_late_start",
        out_shape=(pltpu.SemaphoreType.DMA((3,)), pltpu.SemaphoreType.DMA((3,)), pltpu.HBM(mine.shape, mine.dtype),
                   pltpu.HBM(land.shape, land.dtype), jax.ShapeDtypeStruct((8, 128), F32)),
        in_specs=(HBM_SPEC, HBM_SPEC),
        out_specs=(SEM_SPEC, SEM_SPEC, HBM_SPEC, HBM_SPEC, pl.BlockSpec(memory_space=pltpu.VMEM)),
        input_output_aliases={0: 2, 1: 3},
        compiler_params=pltpu.CompilerParams(has_side_effects=EFFECT),
    )(pltpu.with_memory_space_constraint(mine, pltpu.HBM), pltpu.with_memory_space_constraint(land, pltpu.HBM))


def _gather_late_wait(send_sems, recv_sems, mine_thru, land_thru, after):
    def body(mine_ref, land_ref, send_sems, recv_sems, after_ref, mine_dead, got_ref):
        x, y, c = _me()
        for j, (cx, cy) in enumerate([(1 - x, y), (x, 1 - y), (1 - x, 1 - y)]):
            cp = pltpu.make_async_remote_copy(
                src_ref=mine_ref, dst_ref=land_ref.at[2 * cx + cy], send_sem=send_sems.at[j],
                recv_sem=recv_sems.at[j], device_id=(cx, cy, c), device_id_type=MESH)
            cp.wait_send()
            cp.wait_recv()

    return pl.pallas_call(
        body, name="gather_late_wait",
        out_shape=(pltpu.HBM(mine_thru.shape, mine_thru.dtype), pltpu.HBM(land_thru.shape, land_thru.dtype)),
        in_specs=(HBM_SPEC, HBM_SPEC, SEM_SPEC, SEM_SPEC, pl.BlockSpec(memory_space=pl.ANY)),
        out_specs=(HBM_SPEC, HBM_SPEC), input_output_aliases={0: 0, 1: 1},
        compiler_params=pltpu.CompilerParams(has_side_effects=EFFECT),
    )(mine_thru, land_thru, send_sems, recv_sems, after)[1]


def _swap_halves(gs, tag):
    n = len(gs)

    def body(*refs):
        g_refs, got_refs, send_sems, recv_sems = refs[:n], refs[n:2 * n], refs[2 * n], refs[2 * n + 1]
        x, y, c = _me()
        cps = []
        for i, (g_ref, got_ref) in enumerate(zip(g_refs, got_refs)):
            half = g_ref.shape[1] // 2
            theirs = pl.multiple_of((1 - c) * half, 16)
            cps.append(pltpu.make_async_remote_copy(
                src_ref=g_ref.at[:, pl.ds(theirs, half), :], dst_ref=got_ref, send_sem=send_sems.at[i],
                recv_sem=recv_sems.at[i], device_id=(x, y, 1 - c), device_id_type=MESH))
        for cp in cps:
            cp.start()
        for cp in cps:
            cp.wait()

    return pl.pallas_call(
        body, name="swap_halves_" + tag, in_specs=[HBM_SPEC] * n, out_specs=[HBM_SPEC] * n,
        out_shape=[jax.ShapeDtypeStruct((4, g.shape[1] // 2, g.shape[2]), F32) for g in gs],
        scratch_shapes=[pltpu.SemaphoreType.DMA((n,)), pltpu.SemaphoreType.DMA((n,))],
    )(*gs)


def _swap_start(gs, tag):
    n = len(gs)

    def body(*refs):
        g_refs, land_refs, send_sems, recv_sems = refs[:n], refs[n:2 * n], refs[2 * n], refs[2 * n + 1]
        x, y, c = _me()
        for i, (g_ref, land_ref) in enumerate(zip(g_refs, land_refs)):
            half = g_ref.shape[1] // 2
            theirs = pl.multiple_of((1 - c) * half, 16)
            pltpu.make_async_remote_copy(
                src_ref=g_ref.at[:, pl.ds(theirs, half), :], dst_ref=land_ref, send_sem=send_sems.at[i],
                recv_sem=recv_sems.at[i], device_id=(x, y, 1 - c), device_id_type=MESH).start()

    lands = [lax.empty((4, g.shape[1] // 2, g.shape[2]), F32) for g in gs]
    hbm = lambda t: pltpu.HBM(t.shape, t.dtype)
    res = pl.pallas_call(
        body, name="swap_start_" + tag,
        out_shape=(pltpu.SemaphoreType.DMA((n,)), pltpu.SemaphoreType.DMA((n,)), *[hbm(t) for t in gs],
                   *[hbm(t) for t in lands]),
        in_specs=(HBM_SPEC,) * (2 * n), out_specs=(SEM_SPEC, SEM_SPEC) + (HBM_SPEC,) * (2 * n),
        input_output_aliases={i: 2 + i for i in range(2 * n)},
        compiler_params=pltpu.CompilerParams(has_side_effects=EFFECT),
    )(*[pltpu.with_memory_space_constraint(t, pltpu.HBM) for t in list(gs) + lands])
    return res[0], res[1], res[2:2 + n], res[2 + n:2 + 2 * n]


def _swap_wait(send_sems, recv_sems, g_thru, land_thru, after, tag):
    n = len(g_thru)

    def body(*refs):
        g_refs, land_refs, send_sems, recv_sems = refs[:n], refs[n:2 * n], refs[2 * n], refs[2 * n + 1]
        x, y, c = _me()
        for i, (g_ref, land_ref) in enumerate(zip(g_refs, land_refs)):
            half = g_ref.shape[1] // 2
            theirs = pl.multiple_of((1 - c) * half, 16)
            pltpu.make_async_remote_copy(
                src_ref=g_ref.at[:, pl.ds(theirs, half), :], dst_ref=land_ref, send_sem=send_sems.at[i],
                recv_sem=recv_sems.at[i], device_id=(x, y, 1 - c), device_id_type=MESH).wait()

    hbm = lambda t: pltpu.HBM(t.shape, t.dtype)
    res = pl.pallas_call(
        body, name="swap_wait_" + tag, out_shape=tuple(hbm(t) for t in list(g_thru) + list(land_thru)),
        in_specs=(HBM_SPEC,) * (2 * n) + (SEM_SPEC, SEM_SPEC, pl.BlockSpec(memory_space=pl.ANY)),
        out_specs=(HBM_SPEC,) * (2 * n), input_output_aliases={i: i for i in range(2 * n)},
        compiler_params=pltpu.CompilerParams(has_side_effects=EFFECT),
    )(*g_thru, *land_thru, send_sems, recv_sems, after)
    return res[:n], res[n:]


def _pair_sum(name, c1, g, got):
    half, cols = got.shape[1], got.shape[2]

    def body(c_ref, a_ref, b_ref, o_ref):
        o_ref[...] = (a_ref[...] + b_ref[...]).astype(BF16)

    return pl.pallas_call(
        body, name="pair_sum_" + name,
        grid_spec=pltpu.PrefetchScalarGridSpec(
            num_scalar_prefetch=1, grid=(4,),
            in_specs=[pl.BlockSpec((1, half, cols), lambda k, c_ref: (k, c_ref[0], 0)),
                      pl.BlockSpec((1, half, cols), lambda k, c_ref: (k, 0, 0))],
            out_specs=pl.BlockSpec((1, half, cols), lambda k, c_ref: (k, 0, 0))),
        out_shape=jax.ShapeDtypeStruct(got.shape, BF16), compiler_params=_params(32),
    )(c1, g, got)


def _exchange_start(pss, tag):
    n = len(pss)

    def body(*refs):
        ps_refs, land_refs = refs[:n], refs[n:2 * n]
        send_sems, recv_sems = refs[2 * n], refs[2 * n + 1]
        token = refs[4 * n + 2]
        x, y, c = _me()
        k_me = 2 * x + y
        for i, (ps_ref, land_ref) in enumerate(zip(ps_refs, land_refs)):
            for j, (cx, cy) in enumerate([(1 - x, y), (x, 1 - y), (1 - x, 1 - y)]):
                pltpu.make_async_remote_copy(
                    src_ref=ps_ref.at[2 * cx + cy], dst_ref=land_ref.at[k_me], send_sem=send_sems.at[3 * i + j],
                    recv_sem=recv_sems.at[3 * i + j], device_id=(cx, cy, c), device_id_type=MESH).start()
        token[...] = jnp.zeros(token.shape, F32)

    lands = [lax.empty(ps.shape, ps.dtype) for ps in pss]
    hbm = lambda t: pltpu.HBM(t.shape, t.dtype)
    res = pl.pallas_call(
        body, name="exchange_start_" + tag,
        out_shape=(pltpu.SemaphoreType.DMA((3 * n,)), pltpu.SemaphoreType.DMA((3 * n,)), *[hbm(t) for t in pss],
                   *[hbm(t) for t in lands], jax.ShapeDtypeStruct((8, 128), F32)),
        in_specs=(HBM_SPEC,) * (2 * n),
        out_specs=(SEM_SPEC, SEM_SPEC) + (HBM_SPEC,) * (2 * n) + (pl.BlockSpec(memory_space=pltpu.VMEM),),
        input_output_aliases={i: 2 + i for i in range(2 * n)},
        compiler_params=pltpu.CompilerParams(has_side_effects=EFFECT),
    )(*[pltpu.with_memory_space_constraint(t, pltpu.HBM) for t in list(pss) + lands])
    return res[0], res[1], res[2:2 + n], res[2 + n:2 + 2 * n], res[2 + 2 * n]


def _exchange_wait(send_sems, recv_sems, ps_thru, land_thru, after, tag):
    n = len(ps_thru)

    def body(*refs):
        ps_refs, land_refs = refs[:n], refs[n:2 * n]
        send_sems, recv_sems = refs[2 * n], refs[2 * n + 1]
        x, y, c = _me()
        k_me = 2 * x + y
        for i, (ps_ref, land_ref) in enumerate(zip(ps_refs, land_refs)):
            for j, (cx, cy) in enumerate([(1 - x, y), (x, 1 - y), (1 - x, 1 - y)]):
                cp = pltpu.make_async_remote_copy(
                    src_ref=ps_ref.at[k_me], dst_ref=land_ref.at[2 * cx + cy], send_sem=send_sems.at[3 * i + j],
                    recv_sem=recv_sems.at[3 * i + j], device_id=(cx, cy, c), device_id_type=MESH)
                cp.wait_send()
                cp.wait_recv()

    hbm = lambda t: pltpu.HBM(t.shape, t.dtype)
    res = pl.pallas_call(
        body, name="exchange_wait_" + tag,
        out_shape=tuple(hbm(t) for t in list(ps_thru) + list(land_thru)),
        in_specs=(HBM_SPEC,) * (2 * n) + (SEM_SPEC, SEM_SPEC, pl.BlockSpec(memory_space=pl.ANY)),
        out_specs=(HBM_SPEC,) * (2 * n), input_output_aliases={i: i for i in range(2 * n)},
        compiler_params=pltpu.CompilerParams(has_side_effects=EFFECT),
    )(*ps_thru, *land_thru, send_sems, recv_sems, after)
    return res[:n], res[n:]


def _adamw(w, g, m, v):
    m = B1 * m + (1.0 - B1) * g
    v = B2 * v + (1.0 - B2) * (g * g)
    delta = -LR * ((m / BC1) / (jnp.sqrt(v / BC2) + AEPS) + WD * w)
    return delta, m, v


def _reduce_chips(name, parts):
    half, cols = parts.shape[1], parts.shape[2]

    def body(p_ref, o_ref):
        f = lambda k: p_ref[k].astype(F32)
        o_ref[...] = ((f(0) + f(1)) + f(2)) + f(3)

    return pl.pallas_call(
        body, name="reduce_chips_" + name, grid=(1,),
        in_specs=[pl.BlockSpec((4, half, cols), lambda i: (0, 0, 0))],
        out_specs=pl.BlockSpec((half, cols), lambda i: (0, 0)),
        out_shape=jax.ShapeDtypeStruct((half, cols), F32), compiler_params=_params(32),
    )(parts)


def _share_grad(ghs, tag):
    n = len(ghs)

    def body(*refs):
        g_refs, got_refs, send_sems, recv_sems = refs[:n], refs[n:2 * n], refs[2 * n], refs[2 * n + 1]
        x, y, c = _me()
        cps = [pltpu.make_async_remote_copy(
            src_ref=g_ref, dst_ref=got_ref, send_sem=send_sems.at[i], recv_sem=recv_sems.at[i],
            device_id=(x, y, 1 - c), device_id_type=MESH) for i, (g_ref, got_ref) in enumerate(zip(g_refs, got_refs))]
        for cp in cps:
            cp.start()
        for cp in cps:
            cp.wait()

    return pl.pallas_call(
        body, name="share_grad_" + tag, in_specs=[HBM_SPEC] * n, out_specs=[HBM_SPEC] * n,
        out_shape=[jax.ShapeDtypeStruct(g.shape, F32) for g in ghs],
        scratch_shapes=[pltpu.SemaphoreType.DMA((n,)), pltpu.SemaphoreType.DMA((n,))],
    )(*ghs)


def _update(name, c1, gh, got, w, m, v):
    half, cols = gh.shape
    wcols = w.shape[1]

    def body(c_ref, gh_ref, got_ref, w_ref, m_ref, v_ref, g_o, d_o, m_o, v_o):
        g = jnp.where(pl.program_id(0) == c_ref[0], gh_ref[:, :wcols], got_ref[:, :wcols])
        delta, mn, vn = _adamw(w_ref[...], g, m_ref[...], v_ref[...])
        g_o[...] = g
        d_o[...] = delta
        m_o[...] = mn
        v_o[...] = vn

    same = pl.BlockSpec((half, cols), lambda h, c_ref: (0, 0))
    rows = pl.BlockSpec((half, wcols), lambda h, c_ref: (h, 0))
    return pl.pallas_call(
        body, name="update_" + name,
        grid_spec=pltpu.PrefetchScalarGridSpec(
            num_scalar_prefetch=1, grid=(2,), in_specs=[same, same, rows, rows, rows],
            out_specs=[rows, rows, rows, rows]),
        out_shape=[jax.ShapeDtypeStruct(w.shape, F32)] * 4, compiler_params=_params(40),
    )(c1, gh, got, w, m, v)


SMALL_NAMES = ("sg_w",) + VEC_NAMES
VEC_ROWS = 24
VEC_ROW = {"f_bias": 0, "sg_ln_g": 1, "sg_ln_b": 2, "att_out_g": 3, "sg_out_g": 4, "pre_mix_g": 5,
           "post_mix_g": 6, "pre_ffn_g": 7, "sg_b": 8, "post_ffn_g": 16, "ple_gate_b": 17}
LOSS_VEC_ROW = 18


def _small_pack(g, loss_l):
    n = len(SMALL_NAMES)

    def body(*refs):
        g_r = dict(zip(SMALL_NAMES, refs[0:n]))
        loss_r, vec_o, w_o = refs[n:]
        vec_o[...] = jnp.zeros((VEC_ROWS, 1024), F32)
        for name in VEC_NAMES:
            val = g_r[name][...]
            vec_o[pl.ds(VEC_ROW[name], val.shape[0]), pl.ds(0, val.shape[1])] = val
        vec_o[pl.ds(LOSS_VEC_ROW, 1), :] = loss_r[...] * (0.5 / D)
        rr = lax.broadcasted_iota(jnp.int32, (CH, CH), 0)
        cc = lax.broadcasted_iota(jnp.int32, (CH, CH), 1)
        w_o[...] = jnp.where((cc <= rr)[None], g_r["sg_w"][...], 0.0)

    vm = pl.BlockSpec(memory_space=pltpu.VMEM)
    args = [g[k] for k in SMALL_NAMES] + [loss_l]
    return pl.pallas_call(
        body, name="small_pack", in_specs=[vm] * len(args), out_specs=[vm, vm],
        out_shape=[jax.ShapeDtypeStruct((VEC_ROWS, 1024), F32), jax.ShapeDtypeStruct((8, CH, CH), F32)],
    )(*args)


def _small_peers(x, y, c):
    rels = [(rx, ry, rc) for rx in (0, 1) for ry in (0, 1) for rc in (0, 1)][1:]
    return [((x + rx) % 2, (y + ry) % 2, (c + rc) % 2) for rx, ry, rc in rels]


def _small_start(vec, w8):
    def body(vec_ref, w_ref, lv_ref, lw_ref, send_sems, recv_sems, vec_thru, w_thru, lv_thru, lw_thru, token):
        x, y, c = _me()
        me = 4 * x + 2 * y + c
        for j, to in enumerate(_small_peers(x, y, c)):
            for i, (src, land) in enumerate(((vec_ref, lv_ref), (w_ref, lw_ref))):
                pltpu.make_async_remote_copy(
                    src_ref=src, dst_ref=land.at[me], send_sem=send_sems.at[2 * j + i],
                    recv_sem=recv_sems.at[2 * j + i], device_id=to, device_id_type=MESH).start()
        token[...] = jnp.zeros(token.shape, F32)

    ops = [vec, w8, lax.empty((8,) + vec.shape, F32), lax.empty((8,) + w8.shape, F32)]
    hbm = lambda t: pltpu.HBM(t.shape, t.dtype)
    res = pl.pallas_call(
        body, name="small_start",
        out_shape=(pltpu.SemaphoreType.DMA((14,)), pltpu.SemaphoreType.DMA((14,)), *[hbm(t) for t in ops],
                   jax.ShapeDtypeStruct((8, 128), F32)),
        in_specs=(HBM_SPEC,) * 4,
        out_specs=(SEM_SPEC, SEM_SPEC) + (HBM_SPEC,) * 4 + (pl.BlockSpec(memory_space=pltpu.VMEM),),
        input_output_aliases={i: 2 + i for i in range(4)},
        compiler_params=pltpu.CompilerParams(has_side_effects=EFFECT),
    )(*[pltpu.with_memory_space_constraint(t, pltpu.HBM) for t in ops])
    return res[0], res[1], res[2:6], res[6]


def _small_wait(send_sems, recv_sems, thru, after):
    def body(vec_ref, w_ref, lv_ref, lw_ref, send_sems, recv_sems, after_ref, vec_o, w_o, lv_o, lw_o):
        x, y, c = _me()
        for j, (px, py, pc) in enumerate(_small_peers(x, y, c)):
            for i, (src, land) in enumerate(((vec_ref, lv_ref), (w_ref, lw_ref))):
                cp = pltpu.make_async_remote_copy(
                    src_ref=src, dst_ref=land.at[4 * px + 2 * py + pc], send_sem=send_sems.at[2 * j + i],
                    recv_sem=recv_sems.at[2 * j + i], device_id=(px, py, pc), device_id_type=MESH)
                cp.wait_send()
                cp.wait_recv()

    hbm = lambda t: pltpu.HBM(t.shape, t.dtype)
    return pl.pallas_call(
        body, name="small_wait", out_shape=tuple(hbm(t) for t in thru),
        in_specs=(HBM_SPEC,) * 4 + (SEM_SPEC, SEM_SPEC, pl.BlockSpec(memory_space=pl.ANY)),
        out_specs=(HBM_SPEC,) * 4, input_output_aliases={i: i for i in range(4)},
        compiler_params=pltpu.CompilerParams(has_side_effects=EFFECT),
    )(*thru, send_sems, recv_sems, after)


def _small_update(all_v, all_w, w, m, v):
    n = len(SMALL_NAMES)

    def body(*refs):
        allv_r, allw_r = refs[0], refs[1]
        tot_v = allv_r[0]
        tot_w = allw_r[0]
        for d in range(1, 8):
            tot_v = tot_v + allv_r[d]
            tot_w = tot_w + allw_r[d]
        w_r = dict(zip(SMALL_NAMES, refs[2:2 + n]))
        m_r = dict(zip(SMALL_NAMES, refs[2 + n:2 + 2 * n]))
        v_r = dict(zip(SMALL_NAMES, refs[2 + 2 * n:2 + 3 * n]))
        loss_o = refs[2 + 3 * n]
        outs = refs[3 + 3 * n:]
        loss_o[...] = jnp.sum(tot_v[LOSS_VEC_ROW:LOSS_VEC_ROW + 1, :], axis=-1, keepdims=True) + jnp.zeros((1, 128), F32)
        for i, name in enumerate(SMALL_NAMES):
            if name == "sg_w":
                gt = tot_w
            else:
                rows, width = w_r[name].shape
                gt = tot_v[VEC_ROW[name]:VEC_ROW[name] + rows, 0:width]
            delta, mn, vn = _adamw(w_r[name][...], gt, m_r[name][...], v_r[name][...])
            outs[4 * i][...] = gt
            outs[4 * i + 1][...] = delta
            outs[4 * i + 2][...] = mn
            outs[4 * i + 3][...] = vn

    vm = pl.BlockSpec(memory_space=pltpu.VMEM)
    args = [all_v, all_w] + [d[k] for d in (w, m, v) for k in SMALL_NAMES]
    out_shape = [jax.ShapeDtypeStruct((1, 128), F32)]
    out_shape += [jax.ShapeDtypeStruct(w[k].shape, F32) for k in SMALL_NAMES for _ in range(4)]
    res = pl.pallas_call(
        body, name="small_update", in_specs=[vm] * len(args), out_specs=[vm] * len(out_shape), out_shape=out_shape,
        compiler_params=pltpu.CompilerParams(vmem_limit_bytes=32 * 1024 * 1024),
    )(*args)
    return res[0], {k: res[1 + 4 * i:5 + 4 * i] for i, k in enumerate(SMALL_NAMES)}


def _win_kernel_order(gathered):
    w_in = jnp.concatenate([gathered[k].reshape(D, 768)[:, :642] for k in range(4)], axis=1)
    return jnp.concatenate([w_in[:, :3 * AW], w_in[:, 3 * AW + NH:], w_in[:, 3 * AW:3 * AW + NH],
                            jnp.zeros((D, 128 - NH), w_in.dtype)], axis=1)


LATE_ROWS = 256 + 1024 + 1024 + 64 + 256


def _pack_late(w_out, w1, w2, plew, wg):
    return jnp.concatenate([w_out, w1, w2, plew.reshape(64, 1024), wg], axis=0)


def _unpack_late(gathered):
    return (gathered[:, 0:256].reshape(D, D), gathered[:, 256:1280], gathered[:, 1280:2304].reshape(DFF, D),
            gathered[:, 2304:2368].reshape(4, 256, 256), gathered[:, 2368:2624].reshape(D, D))


def _local_step(x, p, tgt, win_k, late_weights, token, on_ff_grads, on_tail_grads, on_small_grads, small):
    T = x.shape[0]
    row = lambda n: small[n].reshape(1, -1)
    fbias = jnp.pad(row("f_bias"), ((0, 0), (0, 128 - NH))) + token[0:1, :]
    wm = _masked_sg_w(small["sg_w"].reshape(8, CH, CH))
    wmb = wm.astype(BF16)
    wmt = jnp.swapaxes(wm, 1, 2).astype(BF16)
    bsg = jnp.repeat(small["sg_b"].reshape(8, CH).T, DH, axis=1)
    ln_g, ln_b, gsg, gatt = row("sg_ln_g"), row("sg_ln_b"), row("sg_out_g"), row("att_out_g")
    gpre, gpm, gpf, gpff, bg = row("pre_mix_g"), row("post_mix_g"), row("pre_ffn_g"), row("post_ffn_g"), row("ple_gate_b")
    gsel = (jnp.arange(AW)[:, None] // DH == jnp.arange(128)[None, :]).astype(BF16)

    expand, shrink, pieces, qconst, one64, one67, pick64, pick67 = _head_consts()
    a, flog, zuv, ysgn, q8, k8, v8 = _pre_attn_fwd(
        x, gpre, win_k, fbias, ln_g, ln_b, wmb, bsg, gsg, expand, pieces, qconst, one67, one64)

    slabs = lambda t: jnp.swapaxes(t.reshape(T // TQ, TQ, NH * 128), 1, 2)
    qt8 = slabs(q8)
    lanes = jnp.arange(128)
    sel = jnp.stack([((lanes[:, None] == lanes[None, :] - DH * j) & (lanes[:, None] < DH)).astype(BF16)
                     for j in (0, 1)])

    yatt, lse = _flash_fwd(qt8, k8, slabs(v8), sel)
    wout, w1, w2, plew, wg = late_weights(lse)
    y, ov, h1, c2, sact, rr = _tail_fwd1(x, yatt, ysgn, gatt, wout, gpm, gpf, w1)
    ff, h2b, de, dpre, dh2, loss_l, dbg = _tail_fwd2(sact, h1, p, tgt, w2, gpff, wg, bg, plew)
    dff, dr, do, do8, dlt, dysg, dh1, dgpff, dgpf, dgpm, dgatt = _tail_bwd(
        dh2, ff, rr, h1, ov, yatt, w2, w1, wout, gpff, gpf, gpm, gatt, gsel, expand)
    dw1 = _matmul_tn("grad_w_ff1", c2, dr, shards=4)
    dw2 = _matmul_tn("grad_w_ff2", sact, dff)
    ff_sent = on_ff_grads(dw1, dw2)
    dwout = _matmul_tn("grad_w_out", y, do, after=ff_sent)
    dwg = _matmul_tn("grad_ple_gate_w", h2b, dpre, after=dwout)
    dplew = _matmul_tn("grad_ple_w", p, de, tn=256, shards=4, after=dwg)
    tail_token = on_tail_grads((dwout, dplew, dwg))
    dlt4 = jnp.pad(dlt[:, :NH].T.reshape(4, 2, T), ((0, 0), (0, 6), (0, 0))) + tail_token[0, 0]
    dqt, dk8, dv8 = _flash_bwd(q8, qt8, k8, v8, do8, slabs(do8), lse, dlt4)
    dx, dz, dgpre, dfb, dgsg, dlng, dlnb, dws, _, dsbt = _pre_attn_bwd(
        x, dh1, jnp.swapaxes(dqt, 1, 2).reshape(T, NH * 128), dk8, dv8, flog, zuv, dysg,
        gpre, win_k, ln_g, ln_b, wmb, wmt, bsg, gsg, gsel, shrink, pick64, pick67)


    dsb = dsbt[:, :8].T
    gsmall = {"sg_w": dws, "f_bias": dfb, "sg_ln_g": dlng, "sg_ln_b": dlnb, "sg_b": dsb,
              "att_out_g": dgatt, "sg_out_g": dgsg, "pre_mix_g": dgpre, "post_mix_g": dgpm, "pre_ffn_g": dgpf,
              "post_ffn_g": dgpff, "ple_gate_b": dbg}
    on_small_grads(gsmall, loss_l)
    dwin_k = _matmul_tn("grad_w_in", a, dz, tn=384)
    return loss_l, dx, dwin_k, gsmall


def kernel(x, p, w_in, f_bias, sg_ln_g, sg_ln_b, sg_w, sg_b, att_out_g, sg_out_g, w_out, pre_mix_g, post_mix_g, pre_ffn_g, post_ffn_g, w_ff1, w_ff2, ple_w, ple_gate_w, ple_gate_b, loss_target, m_w_in, m_f_bias, m_sg_ln_g, m_sg_ln_b, m_sg_w, m_sg_b, m_att_out_g, m_sg_out_g, m_w_out, m_pre_mix_g, m_post_mix_g, m_pre_ffn_g, m_post_ffn_g, m_w_ff1, m_w_ff2, m_ple_w, m_ple_gate_w, m_ple_gate_b, v_w_in, v_f_bias, v_sg_ln_g, v_sg_ln_b, v_sg_w, v_sg_b, v_att_out_g, v_sg_out_g, v_w_out, v_pre_mix_g, v_post_mix_g, v_pre_ffn_g, v_post_ffn_g, v_w_ff1, v_w_ff2, v_ple_w, v_ple_gate_w, v_ple_gate_b):
    c = lax.axis_index("c")
    big = lambda t: (t[0][0], t[1][0], t[2][0], t[3][0], t[4][0], t[5][0])
    w_big = big((w_in, w_out, w_ff1, w_ff2, ple_w, ple_gate_w))
    m_big = big((m_w_in, m_w_out, m_w_ff1, m_w_ff2, m_ple_w, m_ple_gate_w))
    v_big = big((v_w_in, v_w_out, v_w_ff1, v_w_ff2, v_ple_w, v_ple_gate_w))
    small = {"sg_w": sg_w, "f_bias": f_bias, "sg_ln_g": sg_ln_g, "sg_ln_b": sg_ln_b, "sg_b": sg_b,
             "att_out_g": att_out_g, "sg_out_g": sg_out_g, "pre_mix_g": pre_mix_g, "post_mix_g": post_mix_g,
             "pre_ffn_g": pre_ffn_g, "post_ffn_g": post_ffn_g, "ple_gate_b": ple_gate_b}
    m_small = {"sg_w": m_sg_w, "f_bias": m_f_bias, "sg_ln_g": m_sg_ln_g, "sg_ln_b": m_sg_ln_b, "sg_b": m_sg_b,
               "att_out_g": m_att_out_g, "sg_out_g": m_sg_out_g, "pre_mix_g": m_pre_mix_g,
               "post_mix_g": m_post_mix_g, "pre_ffn_g": m_pre_ffn_g, "post_ffn_g": m_post_ffn_g,
               "ple_gate_b": m_ple_gate_b}
    v_small = {"sg_w": v_sg_w, "f_bias": v_f_bias, "sg_ln_g": v_sg_ln_g, "sg_ln_b": v_sg_ln_b, "sg_b": v_sg_b,
               "att_out_g": v_att_out_g, "sg_out_g": v_sg_out_g, "pre_mix_g": v_pre_mix_g,
               "post_mix_g": v_post_mix_g, "pre_ffn_g": v_pre_ffn_g, "post_ffn_g": v_post_ffn_g,
               "ple_gate_b": v_ple_gate_b}

    k_me = 2 * lax.axis_index("x") + lax.axis_index("y")
    own_slot = lambda got, mine: lax.dynamic_update_slice(got, mine[None], (k_me, 0, 0))
    late_mine = _pack_late(*w_big[1:]).astype(BF16)
    late = _gather_late_start(late_mine)
    win_mine = jnp.pad(w_big[0], ((0, 0), (0, 768 - 642))).reshape(768, 1024).astype(BF16)
    win_k = _win_kernel_order(own_slot(_gather_weights(win_mine), win_mine))
    late_weights = lambda after: _unpack_late(
        own_slot(_gather_late_wait(late[0], late[1], late[2], late[3], after), late_mine))

    names = ("w_in", "w_out", "w_ff1", "w_ff2", "ple_w", "ple_gate_w")
    c1 = jnp.reshape(c, (1,)).astype(jnp.int32)
    own_part = lambda parts, pss: [lax.dynamic_update_slice(pt, lax.dynamic_slice_in_dim(ps, k_me, 1, 0), (k_me, 0, 0))
                                   for pt, ps in zip(parts, pss)]
    tail = {}

    def on_ff_grads(dw1, dw2):
        tail["swap"] = _swap_start([dw1, dw2.reshape(4, D, D)], "ff")
        return tail["swap"][2][0]

    def on_tail_grads(grads):
        dwout, dplew, dwg = grads
        ws, wr, g_thru, land_thru = tail["swap"]
        (dw1, dw2), (got1, got2) = _swap_wait(ws, wr, g_thru, land_thru, dplew, "ff")
        rest = [dwout.reshape(4, 256, D), dplew, dwg.reshape(4, 256, D)]
        got_out, got_ple, got_gate = _swap_halves(rest, "late")
        gs = [rest[0], dw1, dw2, rest[1], rest[2]]
        gots = [got_out, got1, got2, got_ple, got_gate]
        pss = [_pair_sum(nm, c1, g, got) for nm, g, got in zip(names[1:], gs, gots)]
        tail["xch"] = _exchange_start(pss, "late")
        return tail["xch"][4]

    def on_small_grads(gsmall, loss_l):
        tail["small"] = _small_start(*_small_pack(gsmall, loss_l))

    loss_l, dx, dwin_k, gsmall = _local_step(
        x[0], p[0, 0], loss_target[0], win_k, late_weights, late[4], on_ff_grads, on_tail_grads, on_small_grads,
        small)

    dwin = jnp.concatenate([dwin_k[:, :3 * AW], dwin_k[:, 5 * AW:5 * AW + NH], dwin_k[:, 3 * AW:5 * AW]], axis=1)
    dwin = jnp.pad(jnp.swapaxes(dwin.reshape(D, 4, 642), 0, 1), ((0, 0), (0, 0), (0, 768 - 642)))
    ps_in = [_pair_sum(names[0], c1, dwin, _swap_halves([dwin], "in")[0])]
    ins, inr, in_thru, inland_thru, in_token = _exchange_start(ps_in, "in")
    xs, xr, ps_thru, land_thru, _ = tail["xch"]
    ps_late, landed = _exchange_wait(xs, xr, ps_thru, land_thru, in_token, "late")

    def finish(nms, parts, tag, w, m, v):
        ghs = [_reduce_chips(nm, pt) for nm, pt in zip(nms, parts)]
        got2 = _share_grad(ghs, tag)
        return [_update(nm, c1, gh, g2, wi, mi, vi) for nm, gh, g2, wi, mi, vi in zip(nms, ghs, got2, w, m, v)]

    late_out = finish(names[1:], own_part(landed, ps_late), "late", w_big[1:], m_big[1:], v_big[1:])
    ps_in, landed_in = _exchange_wait(ins, inr, in_thru, inland_thru, late_out[-1][3], "in")
    big_out = finish(names[:1], own_part(landed_in, ps_in), "in", w_big[:1], m_big[:1], v_big[:1])
    big_out += late_out
    big_out = [[big_out[j][i] for j in range(6)] for i in range(4)]

    view = lambda t: t.reshape(t.shape[-3:]) if t.ndim == 4 else t.reshape(t.shape[-2:])
    views = lambda d: {k: view(d[k]) for k in SMALL_NAMES}
    me = 4 * lax.axis_index("x") + 2 * lax.axis_index("y") + c
    ss, sr, sthru, _ = tail["small"]
    vec, w8, lv, lw = _small_wait(ss, sr, sthru, big_out[0][3])
    all_v = lax.dynamic_update_slice(lv, vec[None], (me, 0, 0))
    all_w = lax.dynamic_update_slice(lw, w8[None], (me, 0, 0, 0))
    loss11, res_s = _small_update(all_v, all_w, views(small), views(m_small), views(v_small))
    loss = loss11[0, 0]

    def small_out(i, name):
        return res_s[name][i].reshape(small[name].shape)

    order = ["w_in", "f_bias", "sg_ln_g", "sg_ln_b", "sg_w", "sg_b", "att_out_g", "sg_out_g", "w_out",
             "pre_mix_g", "post_mix_g", "pre_ffn_g", "post_ffn_g", "w_ff1", "w_ff2", "ple_w", "ple_gate_w",
             "ple_gate_b"]
    big_idx = {"w_in": 0, "w_out": 1, "w_ff1": 2, "w_ff2": 3, "ple_w": 4, "ple_gate_w": 5}
    outs = [loss, dx[None]]
    for i in range(4):
        for name in order:
            if name in big_idx:
                outs.append(big_out[i][big_idx[name]][None])
            else:
                outs.append(small_out(i, name))
    return tuple(outs)
```

```python
import math

import jax
import jax.numpy as jnp
from jax import lax
from jax.experimental import pallas as pl
from jax.experimental.pallas import tpu as pltpu

F32 = jnp.float32
BF16 = jnp.bfloat16
MESH = pl.DeviceIdType.MESH

D = 1024
DH = 64
NH = 8
AW = 512
CH = 128
DFF = 4096
ZW = 5 * AW + 128
EPS = 1e-6
NEG = -1e30
MASKED = -2e30

TM = 256
TQ = 256

LR, B1, B2, AEPS, WD, STEP = 0.001, 0.9, 0.999, 1e-08, 0.01, 10
BC1 = 1.0 - B1 ** STEP
BC2 = 1.0 - B2 ** STEP

VEC_NAMES = ("f_bias", "sg_ln_g", "sg_ln_b", "sg_b", "att_out_g", "sg_out_g", "pre_mix_g",
             "post_mix_g", "pre_ffn_g", "post_ffn_g", "ple_gate_b")


def _dot(a, b):
    return jnp.dot(a, b, preferred_element_type=F32)


def _dot_nt(a, b):
    return lax.dot_general(a, b, (((1,), (1,)), ((), ())), preferred_element_type=F32)


def _dot_tn(a, b):
    return lax.dot_general(a, b, (((0,), (0,)), ((), ())), preferred_element_type=F32)


def _split3(x):
    h = x.astype(BF16)
    r = x - h.astype(F32)
    m = r.astype(BF16)
    l = (r - m.astype(F32)).astype(BF16)
    return h, m, l


def _dot01(sel, x):
    h, m, l = _split3(x)
    return _dot(sel, h) + _dot(sel, m) + _dot(sel, l)


def _dot01_r(x, sel):
    h, m, l = _split3(x)
    return _dot(h, sel) + _dot(m, sel) + _dot(l, sel)


def _dot01_tn(x, sel):
    h, m, l = _split3(x)
    return _dot_tn(h, sel) + _dot_tn(m, sel) + _dot_tn(l, sel)


def _rs(x, n):
    return lax.rsqrt(jnp.sum(x * x, axis=-1, keepdims=True) * (1.0 / n) + EPS)


def _rms_bwd(dn, x, rs, g, n):
    w = dn * g
    dx = rs * w - x * ((rs * rs * rs) * (1.0 / n) * jnp.sum(w * x, axis=-1, keepdims=True))
    return dx, jnp.sum(dn * x * rs, axis=0, keepdims=True)


_GC = math.sqrt(2.0 / math.pi)


def _gelu(x):
    t = jnp.tanh(_GC * (x + 0.044715 * x * x * x))
    return 0.5 * x * (1.0 + t), t


def _gelu_grad(x, t):
    return 0.5 * (1.0 + t) + 0.5 * x * (1.0 - t * t) * (_GC * (1.0 + 3.0 * 0.044715 * x * x))


def _params(vmem_mb, sem=("arbitrary",)):
    return pltpu.CompilerParams(dimension_semantics=sem, vmem_limit_bytes=vmem_mb * 1024 * 1024)


def _row_call(name, body, T, tm, tiled, resident, outs, accs, scratch=(), reverse=False, vmem_mb=48):
    nt = T // tm
    n_t, n_r, n_o, n_a = len(tiled), len(resident), len(outs), len(accs)

    def kern(*refs):
        t_refs = refs[:n_t]
        r_hbm = refs[n_t:n_t + n_r]
        o_refs = refs[n_t + n_r:n_t + n_r + n_o]
        a_refs = refs[n_t + n_r + n_o:n_t + n_r + n_o + n_a]
        r_vmem = refs[n_t + n_r + n_o + n_a:n_t + 2 * n_r + n_o + n_a]
        s_refs = refs[n_t + 2 * n_r + n_o + n_a:]

        @pl.when(pl.program_id(0) == 0)
        def _():
            for h, v in zip(r_hbm, r_vmem):
                pltpu.sync_copy(h, v)
            for a in a_refs + s_refs:
                a[...] = jnp.zeros(a.shape, a.dtype)

        body(t_refs, r_vmem, o_refs, a_refs, s_refs)

    if reverse:
        idx = lambda i: (nt - 1 - i, 0)
        idx_t = lambda i: (nt - 1 - i, 0, 0)
    else:
        idx = lambda i: (i, 0)
        idx_t = lambda i: (i, 0, 0)
    arrays, in_specs = [], []
    for a in tiled:
        if isinstance(a, tuple):
            arrays.append(a[0])
            in_specs.append(pl.BlockSpec((None, a[0].shape[1], tm), idx_t))
        else:
            arrays.append(a)
            in_specs.append(pl.BlockSpec((tm, a.shape[1]), idx))
    in_specs += [pl.BlockSpec(memory_space=pl.ANY) for _ in resident]
    out_shape, out_specs = [], []
    for o in outs:
        if len(o) == 3:
            out_shape.append(jax.ShapeDtypeStruct((nt, o[0], tm), o[1]))
            out_specs.append(pl.BlockSpec((None, o[0], tm), idx_t))
        else:
            out_shape.append(jax.ShapeDtypeStruct((T, o[0]), o[1]))
            out_specs.append(pl.BlockSpec((tm, o[0]), idx))
    out_shape += [jax.ShapeDtypeStruct(s, F32) for s in accs]
    out_specs += [pl.BlockSpec(s, lambda i, n=len(s): (0,) * n) for s in accs]
    scratch_shapes = [pltpu.VMEM(r.shape, r.dtype) for r in resident]
    scratch_shapes += [pltpu.VMEM(s, F32) for s in scratch]
    return pl.pallas_call(
        kern, name=name, grid=(nt,), in_specs=in_specs, out_specs=out_specs, out_shape=out_shape,
        scratch_shapes=scratch_shapes, compiler_params=_params(vmem_mb),
    )(*arrays, *resident)


def _sg_forward(zu, zv, wm_ref, bsg, lng, lnb, mixed_ref, tm):
    gu, tu = _gelu(zu)
    vg, tv = _gelu(zv)
    mu = jnp.sum(vg, axis=-1, keepdims=True) * (1.0 / AW)
    xc = vg - mu
    rstd = lax.rsqrt(jnp.sum(xc * xc, axis=-1, keepdims=True) * (1.0 / AW) + EPS)
    xhat = xc * rstd
    vvb = (xhat * lng + lnb).astype(BF16)
    lane = lax.broadcasted_iota(jnp.int32, (CH, 128), 1)
    for c in range(tm // CH):
        for j in range(4):
            blk = vvb[c * CH:(c + 1) * CH, j * 128:(j + 1) * 128]
            m0 = _dot(wm_ref[2 * j], blk)
            m1 = _dot(wm_ref[2 * j + 1], blk)
            mixed_ref[c * CH:(c + 1) * CH, j * 128:(j + 1) * 128] = (
                jnp.where(lane < DH, m0, m1) + bsg[:, j * 128:(j + 1) * 128])
    return gu, tu, tv, xhat, rstd, vvb, mixed_ref[...]


def _head_consts():
    src = jnp.arange(AW)
    dst = (src // DH) * 128 + src % DH
    wide = jnp.arange(NH * 128)
    expand = (dst[:, None] == wide[None, :]).astype(BF16)
    heads = jnp.arange(128)
    pieces = jnp.stack([((heads[:, None] * 128 + DH + i == wide[None, :]) & (heads[:, None] < NH)).astype(BF16)
                        for i in range(3)])
    spare = wide % 128 - DH
    qconst = jnp.where((spare >= 0) & (spare < 3), -1.0, 0.0).astype(F32)[None, :]
    one64 = jnp.where(spare == 0, 1.0, 0.0).astype(F32)[None, :]
    one67 = jnp.where(spare == 3, 1.0, 0.0).astype(F32)[None, :]
    pick64 = ((wide[:, None] == heads[None, :] * 128 + DH) & (heads[None, :] < NH)).astype(BF16)
    pick67 = ((wide[:, None] == heads[None, :] * 128 + DH + 3) & (heads[None, :] < NH)).astype(BF16)
    return expand, expand.T, pieces, qconst, one64, one67, pick64, pick67


def _masked_sg_w(sg_w):
    r = lax.broadcasted_iota(jnp.int32, (CH, CH), 0)
    c = lax.broadcasted_iota(jnp.int32, (CH, CH), 1)
    return jnp.where((c <= r)[None], sg_w, 0.0)


def _pre_attn_fwd(x, gpre, win, fbias, lng, lnb, wm, bsg, gsg, expand, pieces, qconst, kconst, vconst):
    T = x.shape[0]
    tm = TM

    def body(t, r, o, a, s):
        (x_ref,) = t
        gpre_r, win_r, fb_r, lng_r, lnb_r, wm_r, bsg_r, gsg_r, ex_r, pc_r, qc_r, kc_r, vc_r = r
        a_o, flog_o, zuv_o, ysgn_o, q8_o, k8_o, v8_o = o
        carry_ref, mixed_ref = s
        xv = x_ref[...]
        av = (xv * _rs(xv, D) * gpre_r[...]).astype(BF16)
        a_o[...] = av
        z = _dot(av, win_r[...])
        zu = z[:, 3 * AW:4 * AW]
        zv = z[:, 4 * AW:5 * AW]
        zuv_o[:, 0:AW] = zu
        zuv_o[:, AW:2 * AW] = zv
        zf = z[:, 5 * AW:] + fb_r[...]
        flog_o[...] = zf
        lane = lax.broadcasted_iota(jnp.int32, (tm, 128), 1)
        logf = jnp.where(lane < NH, jnp.minimum(zf, 0.0) - jnp.log(1.0 + jnp.exp(-jnp.abs(zf))), 0.0)
        rr = lax.broadcasted_iota(jnp.int32, (tm, tm), 0)
        cc = lax.broadcasted_iota(jnp.int32, (tm, tm), 1)
        tri = (cc <= rr).astype(BF16)
        cum = _dot01(tri, logf) + carry_ref[...]
        carry_ref[...] = cum[tm - 1:tm, :]
        ex = ex_r[...]
        q8_o[...] = (_dot((z[:, 0:AW] * (DH ** -0.5)).astype(BF16), ex) + qc_r[...]).astype(BF16)
        ch, cm, cl = _split3(cum)
        k8_o[...] = (_dot(z[:, AW:2 * AW].astype(BF16), ex) + _dot(ch, pc_r[0]) + _dot(cm, pc_r[1])
                     + _dot(cl, pc_r[2]) + kc_r[...]).astype(BF16)
        v8_o[...] = (_dot(z[:, 2 * AW:3 * AW].astype(BF16), ex) + vc_r[...]).astype(BF16)
        gu, _, _, _, _, _, mixed = _sg_forward(zu, zv, wm_r, bsg_r[...], lng_r[...], lnb_r[...], mixed_ref, tm)
        ysg = gu * mixed
        ysgn_o[...] = (ysg * _rs(ysg, AW) * gsg_r[...]).astype(BF16)

    return _row_call(
        "pre_attn_fwd", body, T, tm, [x],
        [gpre, win, fbias, lng, lnb, wm, bsg, gsg, expand, pieces, qconst, kconst, vconst],
        [(D, BF16), (128, F32), (2 * AW, F32), (AW, BF16), (NH * 128, BF16), (NH * 128, BF16), (NH * 128, BF16)], [],
        scratch=[(1, 128), (tm, AW)], vmem_mb=48)


def _flash_fwd(qt8, k8, vt8, sel):
    T = k8.shape[0]
    nq = T // TQ

    def body(qt_ref, k_ref, vt_ref, sel_ref, o_ref, l_ref, u_scr, p_scr):
        qi = pl.program_id(1)
        qts = (qt_ref[0:128, :], qt_ref[128:256, :])
        dmat = (lax.broadcasted_iota(jnp.int32, (TQ, TQ), 0) - lax.broadcasted_iota(jnp.int32, (TQ, TQ), 1))
        u_scr[1] = jnp.full((2, TQ, TQ), MASKED, F32)
        p_scr[...] = jnp.zeros(p_scr.shape, BF16)

        def sub(t, carry, sc, sb, masked):
            blk_c = jnp.clip(t - 2, 0, qi)
            off_a = pl.multiple_of(jnp.minimum(t, qi) * TQ, TQ)
            new = []
            for j in (0, 1):
                m, al, acc = carry[j]
                acc = al * acc + _dot(vt_ref[blk_c, j * 128:(j + 1) * 128, :], p_scr[sc, j])
                m_new = jnp.maximum(m, jnp.max(u_scr[sb, j], axis=0, keepdims=True))
                p_scr[sb, j] = jnp.exp(u_scr[sb, j] - m_new).astype(BF16)
                u = _dot(k_ref[pl.ds(off_a, TQ), j * 128:(j + 1) * 128], qts[j])
                u_scr[sc, j] = jnp.where(dmat <= (qi - t) * TQ, u, MASKED) if masked else u
                new.append((m_new, jnp.exp(m - m_new), acc))
            return tuple(new)

        def pair(t2, carry, masked):
            return sub(2 * t2 + 1, sub(2 * t2, carry, 0, 1, masked), 1, 0, masked)

        init = tuple((jnp.full((1, TQ), NEG, F32), jnp.ones((1, TQ), F32), jnp.zeros((128, TQ), F32))
                     for _ in (0, 1))
        carry = lax.fori_loop(0, qi // 2, lambda t2, cr: pair(t2, cr, False), init)
        (m0, _, a0), (m1, _, a1) = pair(qi // 2 + 1, pair(qi // 2, carry, True), True)
        l0 = a0[DH:DH + 1, :]
        l1 = a1[DH:DH + 1, :]
        o_ref[...] = _dot01_tn(a0 * (1.0 / l0), sel_ref[0]) + _dot01_tn(a1 * (1.0 / l1), sel_ref[1])
        l_ref[0:1, :] = m0 + jnp.log(l0)
        l_ref[1:2, :] = m1 + jnp.log(l1)
        l_ref[2:8, :] = jnp.zeros((6, TQ), F32)

    return pl.pallas_call(
        body, name="flash_fwd", grid=(4, nq),
        in_specs=[pl.BlockSpec((None, 256, TQ), lambda h, i: (i, h, 0)),
                  pl.BlockSpec((T, 256), lambda h, i: (0, h)),
                  pl.BlockSpec((nq, 256, TQ), lambda h, i: (0, h, 0)),
                  pl.BlockSpec((2, 128, 128), lambda h, i: (0, 0, 0))],
        out_specs=[pl.BlockSpec((TQ, 128), lambda h, i: (i, h)),
                   pl.BlockSpec((None, 8, TQ), lambda h, i: (h, 0, i))],
        out_shape=[jax.ShapeDtypeStruct((T, AW), F32), jax.ShapeDtypeStruct((4, 8, T), F32)],
        scratch_shapes=[pltpu.VMEM((2, 2, TQ, TQ), F32), pltpu.VMEM((2, 2, TQ, TQ), BF16)],
        compiler_params=_params(40, ("arbitrary", "arbitrary")),
    )(qt8, k8, vt8, sel)


def _flash_bwd(q8, qt8, k8, v8, do8, dot8, lse, dlt):
    T = q8.shape[0]
    nk = T // TQ

    def body(q_ref, qt_ref, k_ref, v_ref, do_ref, dot_ref, l_ref, d_ref, dqt_ref, dk_ref, dv_ref,
             u_scr, dp_scr, p_scr, ds_scr):
        kb = pl.program_id(1)
        n = nk - kb

        @pl.when(kb == 0)
        def _():
            dqt_ref[...] = jnp.zeros(dqt_ref.shape, F32)

        dk_ref[...] = jnp.zeros(dk_ref.shape, F32)
        dv_ref[...] = jnp.zeros(dv_ref.shape, F32)
        u_scr[1] = jnp.full((2, TQ, TQ), MASKED, F32)
        dp_scr[1] = jnp.zeros((2, TQ, TQ), F32)
        p_scr[...] = jnp.zeros(p_scr.shape, BF16)
        ds_scr[...] = jnp.zeros(ds_scr.shape, BF16)
        dmat = (lax.broadcasted_iota(jnp.int32, (TQ, TQ), 0) - lax.broadcasted_iota(jnp.int32, (TQ, TQ), 1))
        ks = (k_ref[:, 0:128], k_ref[:, 128:256])
        vs = (v_ref[:, 0:128], v_ref[:, 128:256])

        def sub(t, sc, sb):
            blk_a = kb + jnp.minimum(t, n - 1)
            blk_c = kb + jnp.clip(t - 2, 0, n - 1)
            off_b = pl.multiple_of((kb + jnp.clip(t - 1, 0, n - 1)) * TQ, TQ)
            off_c = pl.multiple_of(blk_c * TQ, TQ)
            lim = jnp.where(t < n, t * TQ, -TQ)
            for j in (0, 1):
                hl = slice(j * 128, (j + 1) * 128)
                dqt_ref[blk_c, hl, :] += _dot_tn(ks[j], ds_scr[sc, j])
                dk_ref[:, hl] += _dot(ds_scr[sc, j], q_ref[pl.ds(off_c, TQ), hl])
                dv_ref[:, hl] += _dot(p_scr[sc, j], do_ref[pl.ds(off_c, TQ), hl])
                p = jnp.exp(u_scr[sb, j] - l_ref[j:j + 1, pl.ds(off_b, TQ)])
                p_scr[sb, j] = p.astype(BF16)
                ds_scr[sb, j] = (p * (dp_scr[sb, j] - d_ref[j:j + 1, pl.ds(off_b, TQ)])).astype(BF16)
                u_scr[sc, j] = jnp.where(dmat <= lim, _dot(ks[j], qt_ref[blk_a, hl, :]), MASKED)
                dp_scr[sc, j] = _dot(vs[j], dot_ref[blk_a, hl, :])

        def it(t2, carry):
            sub(2 * t2, 0, 1)
            sub(2 * t2 + 1, 1, 0)
            return carry

        lax.fori_loop(0, n // 2 + 1, it, 0)

        @pl.when(n % 2 == 1)
        def _():
            sub(n + 1, 0, 1)

    return pl.pallas_call(
        body, name="flash_bwd", grid=(4, nk),
        in_specs=[pl.BlockSpec((T, 256), lambda h, i: (0, h)),
                  pl.BlockSpec((nk, 256, TQ), lambda h, i: (0, h, 0)),
                  pl.BlockSpec((TQ, 256), lambda h, i: (i, h)),
                  pl.BlockSpec((TQ, 256), lambda h, i: (i, h)),
                  pl.BlockSpec((T, 256), lambda h, i: (0, h)),
                  pl.BlockSpec((nk, 256, TQ), lambda h, i: (0, h, 0)),
                  pl.BlockSpec((None, 8, T), lambda h, i: (h, 0, 0)),
                  pl.BlockSpec((None, 8, T), lambda h, i: (h, 0, 0))],
        out_specs=[pl.BlockSpec((nk, 256, TQ), lambda h, i: (0, h, 0)),
                   pl.BlockSpec((TQ, 256), lambda h, i: (i, h)),
                   pl.BlockSpec((TQ, 256), lambda h, i: (i, h))],
        out_shape=[jax.ShapeDtypeStruct((nk, NH * 128, TQ), F32), jax.ShapeDtypeStruct((T, NH * 128), F32),
                   jax.ShapeDtypeStruct((T, NH * 128), F32)],
        scratch_shapes=[pltpu.VMEM((2, 2, TQ, TQ), F32), pltpu.VMEM((2, 2, TQ, TQ), F32),
                        pltpu.VMEM((2, 2, TQ, TQ), BF16), pltpu.VMEM((2, 2, TQ, TQ), BF16)],
        compiler_params=_params(56, ("arbitrary", "arbitrary")),
    )(q8, qt8, k8, v8, do8, dot8, lse, dlt)


def _tail_fwd1(x, yatt, ysgn, gatt, wout, gpm, gpf, w1):
    T = x.shape[0]

    def body(t, r, o, a, s):
        x_ref, ya_ref, ys_ref = t
        gatt_r, wout_r, gpm_r, gpf_r, w1_r = r
        y_o, o_o, h1_o, c2_o, s_o, rr_o = o
        ya = ya_ref[...]
        yan = (ya * _rs(ya, AW) * gatt_r[...]).astype(BF16)
        y_o[:, 0:AW] = yan
        y_o[:, AW:] = ys_ref[...]
        ov = _dot(yan, wout_r[0:AW, :]) + _dot(ys_ref[...], wout_r[AW:, :])
        o_o[...] = ov
        h1 = x_ref[...] + ov * _rs(ov, D) * gpm_r[...]
        h1_o[...] = h1
        c2 = (h1 * _rs(h1, D) * gpf_r[...]).astype(BF16)
        c2_o[...] = c2
        for k in range(4):
            rr = jnp.maximum(_dot(c2, w1_r[k]), 0.0)
            rr_o[:, k * D:(k + 1) * D] = rr.astype(BF16)
            s_o[:, k * D:(k + 1) * D] = (rr * rr).astype(BF16)

    return _row_call(
        "tail_fwd1", body, T, TM, [x, yatt, ysgn], [gatt, wout, gpm, gpf, w1],
        [(D, BF16), (D, F32), (D, F32), (D, BF16), (DFF, BF16), (DFF, BF16)], [], vmem_mb=48)


def _tail_fwd2(sact, h1, p, tgt, w2, gpff, wg, bg, wpe):
    T = h1.shape[0]

    def body(t, r, o, a, s):
        s_ref, h1_ref, p_ref, t_ref = t
        w2_r, gpff_r, wg_r, bg_r, wpe_r = r
        ff_o, h2b_o, de_o, dpre_o, dh2_o = o
        loss_a, dbg_a = a
        ff = _dot(s_ref[...], w2_r[...])
        ff_o[...] = ff
        h2 = h1_ref[...] + ff * _rs(ff, D) * gpff_r[...]
        h2b = h2.astype(BF16)
        h2b_o[...] = h2b
        gate = 1.0 / (1.0 + jnp.exp(-(_dot(h2b, wg_r[...]) + bg_r[...])))
        pb = p_ref[...].astype(BF16)
        e = jnp.concatenate([_dot(pb, wpe_r[k]) for k in range(4)], axis=1)
        diff = h2 + gate * e - t_ref[...]
        loss_a[...] += jnp.sum(diff * diff, axis=0, keepdims=True)
        dh3 = diff * (1.0 / D)
        de_o[...] = (dh3 * gate).astype(BF16)
        dpre = dh3 * e * gate * (1.0 - gate)
        dbg_a[...] += jnp.sum(dpre, axis=0, keepdims=True)
        dpb = dpre.astype(BF16)
        dpre_o[...] = dpb
        dh2_o[...] = dh3 + _dot_nt(dpb, wg_r[...])

    return _row_call(
        "tail_fwd2", body, T, TM, [sact, h1, p, tgt], [w2, gpff, wg, bg, wpe],
        [(D, F32), (D, BF16), (D, BF16), (D, BF16), (D, F32)], [(1, D), (1, D)], vmem_mb=48)


def _tail_bwd(dh2, ff, rr, h1, ov, yatt, w2, w1, wout, gpff, gpf, gpm, gatt, gsel, expand):
    T = dh2.shape[0]

    def body(t, r, o, a, s):
        dh2_ref, ff_ref, rr_ref, h1_ref, o_ref, ya_ref = t
        w2_r, w1_r, wout_r, gpff_r, gpf_r, gpm_r, gatt_r, gsel_r, ex_r = r
        dff_o, dr_o, do_o, do8_o, dlt_o, dysg_o, dh1_o = o
        dgpff_a, dgpf_a, dgpm_a, dgatt_a = a
        dh2v = dh2_ref[...]
        ffv = ff_ref[...]
        dff, dg = _rms_bwd(dh2v, ffv, _rs(ffv, D), gpff_r[...], D)
        dgpff_a[...] += dg
        dffb = dff.astype(BF16)
        dff_o[...] = dffb
        drb = (_dot_nt(dffb, w2_r[...]) * (2.0 * rr_ref[...].astype(F32))).astype(BF16)
        dr_o[...] = drb
        dc2 = _dot_nt(drb[:, 0:D], w1_r[0])
        for k in range(1, 4):
            dc2 = dc2 + _dot_nt(drb[:, k * D:(k + 1) * D], w1_r[k])
        h1v = h1_ref[...]
        d1, dg = _rms_bwd(dc2, h1v, _rs(h1v, D), gpf_r[...], D)
        dgpf_a[...] += dg
        dh1 = dh2v + d1
        dh1_o[...] = dh1
        ovv = o_ref[...]
        dov, dg = _rms_bwd(dh1, ovv, _rs(ovv, D), gpm_r[...], D)
        dgpm_a[...] += dg
        dob = dov.astype(BF16)
        do_o[...] = dob
        dysg_o[...] = _dot_nt(dob, wout_r[AW:, :])
        dyan = _dot_nt(dob, wout_r[0:AW, :])
        ya = ya_ref[...]
        dya, dg = _rms_bwd(dyan, ya, _rs(ya, AW), gatt_r[...], AW)
        dgatt_a[...] += dg
        do8_o[...] = _dot(dya.astype(BF16), ex_r[...]).astype(BF16)
        dlt_o[...] = _dot01_r(dya * ya, gsel_r[...])

    return _row_call(
        "tail_bwd", body, T, TM, [dh2, ff, rr, h1, ov, yatt],
        [w2, w1, wout, gpff, gpf, gpm, gatt, gsel, expand],
        [(D, BF16), (DFF, BF16), (D, BF16), (NH * 128, BF16), (128, F32), (AW, F32), (D, F32)],
        [(1, D), (1, D), (1, D), (1, AW)], vmem_mb=56)


def _pre_attn_bwd(x, dh1, dq8, dk8, dv8, flog, zuv, dysg, gpre, win, lng, lnb, wm, wmt, bsg, gsg, gsel, shrink, pick64, pick67):
    T = x.shape[0]
    tm = TM

    def body(t, r, o, a, s):
        x_ref, dh1_ref, dq_ref, dk_ref, dv_ref, fl_ref, zuv_ref, dys_ref = t
        gpre_r, win_r, lng_r, lnb_r, wm_r, wmt_r, bsg_r, gsg_r, gsel_r, sh_r, p64_r, p67_r = r
        dx_o, dz_o = o
        dgpre_a, dfb_a, dgsg_a, dlng_a, dlnb_a, dws_a, dbs_a, dsb_a = a
        carry_ref, mixed_ref, dvv_ref = s
        dq8v = dq_ref[...]
        dk8v = dk_ref[...]
        dcv = _dot01_r(dq8v, p67_r[...]) + _dot01_r(dk8v, p64_r[...])
        rr = lax.broadcasted_iota(jnp.int32, (tm, tm), 0)
        cc = lax.broadcasted_iota(jnp.int32, (tm, tm), 1)
        triu = (cc >= rr).astype(BF16)
        dlogf = _dot01(triu, dcv) + carry_ref[...]
        carry_ref[...] = dlogf[0:1, :]
        dzf = dlogf * (1.0 / (1.0 + jnp.exp(fl_ref[...])))
        dfb_a[...] += jnp.sum(dzf, axis=0, keepdims=True)
        dz_o[:, 5 * AW:] = dzf.astype(BF16)
        zu = zuv_ref[:, 0:AW]
        zv = zuv_ref[:, AW:]
        gu, tu, tv, xhat, rstd, vvb, mixed = _sg_forward(
            zu, zv, wm_r, bsg_r[...], lng_r[...], lnb_r[...], mixed_ref, tm)
        ysg = gu * mixed
        dysg_n = dys_ref[...]
        dys, dg = _rms_bwd(dysg_n, ysg, _rs(ysg, AW), gsg_r[...], AW)
        dgsg_a[...] += dg
        dgu = dys * mixed
        dmix = dys * gu
        dmb = dmix.astype(BF16)
        lane = lax.broadcasted_iota(jnp.int32, (CH, 128), 1)
        lo = lane < DH
        for c in range(tm // CH):
            rows = slice(c * CH, (c + 1) * CH)
            dbs_a[...] += dmix[rows, :]
            for j in range(4):
                cols = slice(j * 128, (j + 1) * 128)
                dmblk = dmb[rows, cols]
                vblk = vvb[rows, cols]
                d0 = _dot(wmt_r[2 * j], dmblk)
                d1 = _dot(wmt_r[2 * j + 1], dmblk)
                dvv_ref[rows, cols] = jnp.where(lo, d0, d1)
                dws_a[2 * j] += _dot_nt(jnp.where(lo, dmblk, jnp.zeros_like(dmblk)), vblk)
                dws_a[2 * j + 1] += _dot_nt(jnp.where(lo, jnp.zeros_like(dmblk), dmblk), vblk)
        dvv = dvv_ref[...]
        dlng_a[...] += jnp.sum(dvv * xhat, axis=0, keepdims=True)
        dlnb_a[...] += jnp.sum(dvv, axis=0, keepdims=True)
        dxh = dvv * lng_r[...]
        dvg = rstd * (dxh - jnp.sum(dxh, axis=-1, keepdims=True) * (1.0 / AW)
                      - xhat * (jnp.sum(dxh * xhat, axis=-1, keepdims=True) * (1.0 / AW)))
        dz_o[:, 3 * AW:4 * AW] = (dgu * _gelu_grad(zu, tu)).astype(BF16)
        dz_o[:, 4 * AW:5 * AW] = (dvg * _gelu_grad(zv, tv)).astype(BF16)
        dz_o[:, 0:AW] = _dot((dq8v * (DH ** -0.5)).astype(BF16), sh_r[...]).astype(BF16)
        dz_o[:, AW:2 * AW] = _dot(dk8v.astype(BF16), sh_r[...]).astype(BF16)
        dz_o[:, 2 * AW:3 * AW] = _dot(dv_ref[...].astype(BF16), sh_r[...]).astype(BF16)
        da = _dot_nt(dz_o[...], win_r[...])
        xv = x_ref[...]
        dxa, dg = _rms_bwd(da, xv, _rs(xv, D), gpre_r[...], D)
        dgpre_a[...] += dg
        dx_o[...] = dh1_ref[...] + dxa

        @pl.when(pl.program_id(0) == T // tm - 1)
        def _():
            dsb_a[...] = _dot01_r(dbs_a[...], gsel_r[...])

    outs = _row_call(
        "pre_attn_bwd", body, T, tm, [x, dh1, dq8, dk8, dv8, flog, zuv, dysg],
        [gpre, win, lng, lnb, wm, wmt, bsg, gsg, gsel, shrink, pick64, pick67],
        [(D, F32), (ZW, BF16)],
        [(1, D), (1, 128), (1, AW), (1, AW), (1, AW), (8, CH, CH), (CH, AW), (CH, 128)],
        scratch=[(1, 128), (tm, AW), (tm, AW)], reverse=True, vmem_mb=48)
    return outs


def _matmul_tn(name, a, b, tn=512, tt=2048, shards=1, after=None):
    T, K = a.shape
    N = b.shape[1]
    tk = min(K, 1024)
    tn = min(tn, N // shards)
    tt = min(tt, T)
    nj = N // shards // tn

    def body(a_ref, b_ref, *rest):
        o_ref = rest[-1]

        @pl.when(pl.program_id(2) == 0)
        def _():
            o_ref[...] = jnp.zeros(o_ref.shape, F32)

        o_ref[...] += _dot_tn(a_ref[...].astype(BF16), b_ref[...].astype(BF16))

    ordered = after is not None

    if shards == 1:
        out_shape = jax.ShapeDtypeStruct((K, N), F32)
        out_spec = pl.BlockSpec((tk, tn), lambda i, j, t: (i, j))
    else:
        out_shape = jax.ShapeDtypeStruct((shards, K, N // shards), F32)
        out_spec = pl.BlockSpec((None, tk, tn), lambda i, j, t: (j // nj, i, j % nj))
    return pl.pallas_call(
        body, name=name, grid=(K // tk, N // tn, T // tt),
        in_specs=[pl.BlockSpec((tt, tk), lambda i, j, t: (t, i)),
                  pl.BlockSpec((tt, tn), lambda i, j, t: (t, j))] + [pl.BlockSpec(memory_space=pl.ANY)] * ordered,
        out_specs=out_spec, out_shape=out_shape,
        compiler_params=_params(40, ("arbitrary", "arbitrary", "arbitrary")),
    )(a, b, *([after] * ordered))


def _me():
    return lax.axis_index("x"), lax.axis_index("y"), lax.axis_index("c")


HBM_SPEC = pl.BlockSpec(memory_space=pltpu.HBM)


def _gather_weights(mine):
    half = mine.shape[0] // 2

    def body(mine_ref, out_ref, ici_send, ici_recv, d2d_send, d2d_recv):
        x, y, c = _me()
        k_me = 2 * x + y
        chips = [(1 - x, y), (x, 1 - y), (1 - x, 1 - y)]
        my_rows = pl.ds(pl.multiple_of(c * half, 16), half)
        sib_rows = pl.ds(pl.multiple_of((1 - c) * half, 16), half)

        def over_ici(j, k, to):
            src = mine_ref.at[my_rows] if k is None else out_ref.at[k, my_rows]
            return pltpu.make_async_remote_copy(
                src_ref=src, dst_ref=out_ref.at[k_me if k is None else k, my_rows], send_sem=ici_send.at[j],
                recv_sem=ici_recv.at[j], device_id=to, device_id_type=MESH)

        def over_d2d(j, k, rows):
            return pltpu.make_async_remote_copy(
                src_ref=out_ref.at[k, rows], dst_ref=out_ref.at[k, rows], send_sem=d2d_send.at[j],
                recv_sem=d2d_recv.at[j], device_id=(x, y, 1 - c), device_id_type=MESH)

        first = [over_ici(j, None, (cx, cy, c)) for j, (cx, cy) in enumerate(chips)]
        for cp in first:
            cp.start()
        passed = [over_d2d(j, 2 * cx + cy, my_rows) for j, (cx, cy) in enumerate(chips)]
        for j, (cx, cy) in enumerate(chips):
            over_ici(j, 2 * cx + cy, (cx, cy, c)).wait_recv()
            passed[j].start()
        for j, (cx, cy) in enumerate(chips):
            over_d2d(j, 2 * cx + cy, sib_rows).wait_recv()
        for cp in first + passed:
            cp.wait_send()

    return pl.pallas_call(
        body, name="gather_weights", in_specs=[HBM_SPEC], out_specs=HBM_SPEC,
        out_shape=jax.ShapeDtypeStruct((4,) + mine.shape, mine.dtype),
        scratch_shapes=[pltpu.SemaphoreType.DMA((3,)), pltpu.SemaphoreType.DMA((3,)), pltpu.SemaphoreType.DMA((3,)),
                        pltpu.SemaphoreType.DMA((3,))],
    )(mine)


SEM_SPEC = pl.BlockSpec(memory_space=pltpu.SEMAPHORE)
EFFECT = pltpu.SideEffectType.DATAFLOW_SIDE_EFFECTING


def _gather_late_start(mine):
    def body(mine_ref, land_ref, send_sems, recv_sems, mine_thru, land_thru, token):
        x, y, c = _me()
        k_me = 2 * x + y
        for j, (cx, cy) in enumerate([(1 - x, y), (x, 1 - y), (1 - x, 1 - y)]):
            pltpu.make_async_remote_copy(
                src_ref=mine_ref, dst_ref=land_ref.at[k_me], send_sem=send_sems.at[j], recv_sem=recv_sems.at[j],
                device_id=(cx, cy, c), device_id_type=MESH).start()
        token[...] = jnp.zeros(token.shape, F32)

    land = lax.empty((4,) + mine.shape, mine.dtype)
    return pl.pallas_call(
        body, name="gather_late_start",
        out_shape=(pltpu.SemaphoreType.DMA((3,)), pltpu.SemaphoreType.DMA((3,)), pltpu.HBM(mine.shape, mine.dtype),
                   pltpu.HBM(land.shape, land.dtype), jax.ShapeDtypeStruct((8, 128), F32)),
        in_specs=(HBM_SPEC, HBM_SPEC),
        out_specs=(SEM_SPEC, SEM_SPEC, HBM_SPEC, HBM_SPEC, pl.BlockSpec(memory_space=pltpu.VMEM)),
        input_output_aliases={0: 2, 1: 3},
        compiler_params=pltpu.CompilerParams(has_side_effects=EFFECT),
    )(pltpu.with_memory_space_constraint(mine, pltpu.HBM), pltpu.with_memory_space_constraint(land, pltpu.HBM))


def _gather_late_wait(send_sems, recv_sems, mine_thru, land_thru, after):
    def body(mine_ref, land_ref, send_sems, recv_sems, after_ref, mine_dead, got_ref):
        x, y, c = _me()
        for j, (cx, cy) in enumerate([(1 - x, y), (x, 1 - y), (1 - x, 1 - y)]):
            cp = pltpu.make_async_remote_copy(
                src_ref=mine_ref, dst_ref=land_ref.at[2 * cx + cy], send_sem=send_sems.at[j],
                recv_sem=recv_sems.at[j], device_id=(cx, cy, c), device_id_type=MESH)
            cp.wait_send()
            cp.wait_recv()

    return pl.pallas_call(
        body, name="gather_late_wait",
        out_shape=(pltpu.HBM(mine_thru.shape, mine_thru.dtype), pltpu.HBM(land_thru.shape, land_thru.dtype)),
        in_specs=(HBM_SPEC, HBM_SPEC, SEM_SPEC, SEM_SPEC, pl.BlockSpec(memory_space=pl.ANY)),
        out_specs=(HBM_SPEC, HBM_SPEC), input_output_aliases={0: 0, 1: 1},
        compiler_params=pltpu.CompilerParams(has_side_effects=EFFECT),
    )(mine_thru, land_thru, send_sems, recv_sems, after)[1]


def _swap_halves(gs, tag):
    n = len(gs)

    def body(*refs):
        g_refs, got_refs, send_sems, recv_sems = refs[:n], refs[n:2 * n], refs[2 * n], refs[2 * n + 1]
        x, y, c = _me()
        cps = []
        for i, (g_ref, got_ref) in enumerate(zip(g_refs, got_refs)):
            half = g_ref.shape[1] // 2
            theirs = pl.multiple_of((1 - c) * half, 16)
            cps.append(pltpu.make_async_remote_copy(
                src_ref=g_ref.at[:, pl.ds(theirs, half), :], dst_ref=got_ref, send_sem=send_sems.at[i],
                recv_sem=recv_sems.at[i], device_id=(x, y, 1 - c), device_id_type=MESH))
        for cp in cps:
            cp.start()
        for cp in cps:
            cp.wait()

    return pl.pallas_call(
        body, name="swap_halves_" + tag, in_specs=[HBM_SPEC] * n, out_specs=[HBM_SPEC] * n,
        out_shape=[jax.ShapeDtypeStruct((4, g.shape[1] // 2, g.shape[2]), F32) for g in gs],
        scratch_shapes=[pltpu.SemaphoreType.DMA((n,)), pltpu.SemaphoreType.DMA((n,))],
    )(*gs)


def _swap_start(gs, tag):
    n = len(gs)

    def body(*refs):
        g_refs, land_refs, send_sems, recv_sems = refs[:n], refs[n:2 * n], refs[2 * n], refs[2 * n + 1]
        x, y, c = _me()
        for i, (g_ref, land_ref) in enumerate(zip(g_refs, land_refs)):
            half = g_ref.shape[1] // 2
            theirs = pl.multiple_of((1 - c) * half, 16)
            pltpu.make_async_remote_copy(
                src_ref=g_ref.at[:, pl.ds(theirs, half), :], dst_ref=land_ref, send_sem=send_sems.at[i],
                recv_sem=recv_sems.at[i], device_id=(x, y, 1 - c), device_id_type=MESH).start()

    lands = [lax.empty((4, g.shape[1] // 2, g.shape[2]), F32) for g in gs]
    hbm = lambda t: pltpu.HBM(t.shape, t.dtype)
    res = pl.pallas_call(
        body, name="swap_start_" + tag,
        out_shape=(pltpu.SemaphoreType.DMA((n,)), pltpu.SemaphoreType.DMA((n,)), *[hbm(t) for t in gs],
                   *[hbm(t) for t in lands]),
        in_specs=(HBM_SPEC,) * (2 * n), out_specs=(SEM_SPEC, SEM_SPEC) + (HBM_SPEC,) * (2 * n),
        input_output_aliases={i: 2 + i for i in range(2 * n)},
        compiler_params=pltpu.CompilerParams(has_side_effects=EFFECT),
    )(*[pltpu.with_memory_space_constraint(t, pltpu.HBM) for t in list(gs) + lands])
    return res[0], res[1], res[2:2 + n], res[2 + n:2 + 2 * n]


def _swap_wait(send_sems, recv_sems, g_thru, land_thru, after, tag):
    n = len(g_thru)

    def body(*refs):
        g_refs, land_refs, send_sems, recv_sems = refs[:n], refs[n:2 * n], refs[2 * n], refs[2 * n + 1]
        x, y, c = _me()
        for i, (g_ref, land_ref) in enumerate(zip(g_refs, land_refs)):
            half = g_ref.shape[1] // 2
            theirs = pl.multiple_of((1 - c) * half, 16)
            pltpu.make_async_remote_copy(
                src_ref=g_ref.at[:, pl.ds(theirs, half), :], dst_ref=land_ref, send_sem=send_sems.at[i],
                recv_sem=recv_sems.at[i], device_id=(x, y, 1 - c), device_id_type=MESH).wait()

    hbm = lambda t: pltpu.HBM(t.shape, t.dtype)
    res = pl.pallas_call(
        body, name="swap_wait_" + tag, out_shape=tuple(hbm(t) for t in list(g_thru) + list(land_thru)),
        in_specs=(HBM_SPEC,) * (2 * n) + (SEM_SPEC, SEM_SPEC, pl.BlockSpec(memory_space=pl.ANY)),
        out_specs=(HBM_SPEC,) * (2 * n), input_output_aliases={i: i for i in range(2 * n)},
        compiler_params=pltpu.CompilerParams(has_side_effects=EFFECT),
    )(*g_thru, *land_thru, send_sems, recv_sems, after)
    return res[:n], res[n:]


def _pair_sum(name, c1, g, got):
    half, cols = got.shape[1], got.shape[2]

    def body(c_ref, a_ref, b_ref, o_ref):
        o_ref[...] = (a_ref[...] + b_ref[...]).astype(BF16)

    return pl.pallas_call(
        body, name="pair_sum_" + name,
        grid_spec=pltpu.PrefetchScalarGridSpec(
            num_scalar_prefetch=1, grid=(4,),
            in_specs=[pl.BlockSpec((1, half, cols), lambda k, c_ref: (k, c_ref[0], 0)),
                      pl.BlockSpec((1, half, cols), lambda k, c_ref: (k, 0, 0))],
            out_specs=pl.BlockSpec((1, half, cols), lambda k, c_ref: (k, 0, 0))),
        out_shape=jax.ShapeDtypeStruct(got.shape, BF16), compiler_params=_params(32),
    )(c1, g, got)


def _exchange_start(pss, tag):
    n = len(pss)

    def body(*refs):
        ps_refs, land_refs = refs[:n], refs[n:2 * n]
        send_sems, recv_sems = refs[2 * n], refs[2 * n + 1]
        token = refs[4 * n + 2]
        x, y, c = _me()
        k_me = 2 * x + y
        for i, (ps_ref, land_ref) in enumerate(zip(ps_refs, land_refs)):
            for j, (cx, cy) in enumerate([(1 - x, y), (x, 1 - y), (1 - x, 1 - y)]):
                pltpu.make_async_remote_copy(
                    src_ref=ps_ref.at[2 * cx + cy], dst_ref=land_ref.at[k_me], send_sem=send_sems.at[3 * i + j],
                    recv_sem=recv_sems.at[3 * i + j], device_id=(cx, cy, c), device_id_type=MESH).start()
        token[...] = jnp.zeros(token.shape, F32)

    lands = [lax.empty(ps.shape, ps.dtype) for ps in pss]
    hbm = lambda t: pltpu.HBM(t.shape, t.dtype)
    res = pl.pallas_call(
        body, name="exchange_start_" + tag,
        out_shape=(pltpu.SemaphoreType.DMA((3 * n,)), pltpu.SemaphoreType.DMA((3 * n,)), *[hbm(t) for t in pss],
                   *[hbm(t) for t in lands], jax.ShapeDtypeStruct((8, 128), F32)),
        in_specs=(HBM_SPEC,) * (2 * n),
        out_specs=(SEM_SPEC, SEM_SPEC) + (HBM_SPEC,) * (2 * n) + (pl.BlockSpec(memory_space=pltpu.VMEM),),
        input_output_aliases={i: 2 + i for i in range(2 * n)},
        compiler_params=pltpu.CompilerParams(has_side_effects=EFFECT),
    )(*[pltpu.with_memory_space_constraint(t, pltpu.HBM) for t in list(pss) + lands])
    return res[0], res[1], res[2:2 + n], res[2 + n:2 + 2 * n], res[2 + 2 * n]


def _exchange_wait(send_sems, recv_sems, ps_thru, land_thru, after, tag):
    n = len(ps_thru)

    def body(*refs):
        ps_refs, land_refs = refs[:n], refs[n:2 * n]
        send_sems, recv_sems = refs[2 * n], refs[2 * n + 1]
        x, y, c = _me()
        k_me = 2 * x + y
        for i, (ps_ref, land_ref) in enumerate(zip(ps_refs, land_refs)):
            for j, (cx, cy) in enumerate([(1 - x, y), (x, 1 - y), (1 - x, 1 - y)]):
                cp = pltpu.make_async_remote_copy(
                    src_ref=ps_ref.at[k_me], dst_ref=land_ref.at[2 * cx + cy], send_sem=send_sems.at[3 * i + j],
                    recv_sem=recv_sems.at[3 * i + j], device_id=(cx, cy, c), device_id_type=MESH)
                cp.wait_send()
                cp.wait_recv()

    hbm = lambda t: pltpu.HBM(t.shape, t.dtype)
    res = pl.pallas_call(
        body, name="exchange_wait_" + tag,
        out_shape=tuple(hbm(t) for t in list(ps_thru) + list(land_thru)),
        in_specs=(HBM_SPEC,) * (2 * n) + (SEM_SPEC, SEM_SPEC, pl.BlockSpec(memory_space=pl.ANY)),
        out_specs=(HBM_SPEC,) * (2 * n), input_output_aliases={i: i for i in range(2 * n)},
        compiler_params=pltpu.CompilerParams(has_side_effects=EFFECT),
    )(*ps_thru, *land_thru, send_sems, recv_sems, after)
    return res[:n], res[n:]


def _adamw(w, g, m, v):
    m = B1 * m + (1.0 - B1) * g
    v = B2 * v + (1.0 - B2) * (g * g)
    delta = -LR * ((m / BC1) / (jnp.sqrt(v / BC2) + AEPS) + WD * w)
    return delta, m, v


def _reduce_chips(name, parts):
    half, cols = parts.shape[1], parts.shape[2]

    def body(p_ref, o_ref):
        f = lambda k: p_ref[k].astype(F32)
        o_ref[...] = ((f(0) + f(1)) + f(2)) + f(3)

    return pl.pallas_call(
        body, name="reduce_chips_" + name, grid=(1,),
        in_specs=[pl.BlockSpec((4, half, cols), lambda i: (0, 0, 0))],
        out_specs=pl.BlockSpec((half, cols), lambda i: (0, 0)),
        out_shape=jax.ShapeDtypeStruct((half, cols), F32), compiler_params=_params(32),
    )(parts)


def _share_grad(ghs, tag):
    n = len(ghs)

    def body(*refs):
        g_refs, got_refs, send_sems, recv_sems = refs[:n], refs[n:2 * n], refs[2 * n], refs[2 * n + 1]
        x, y, c = _me()
        cps = [pltpu.make_async_remote_copy(
            src_ref=g_ref, dst_ref=got_ref, send_sem=send_sems.at[i], recv_sem=recv_sems.at[i],
            device_id=(x, y, 1 - c), device_id_type=MESH) for i, (g_ref, got_ref) in enumerate(zip(g_refs, got_refs))]
        for cp in cps:
            cp.start()
        for cp in cps:
            cp.wait()

    return pl.pallas_call(
        body, name="share_grad_" + tag, in_specs=[HBM_SPEC] * n, out_specs=[HBM_SPEC] * n,
        out_shape=[jax.ShapeDtypeStruct(g.shape, F32) for g in ghs],
        scratch_shapes=[pltpu.SemaphoreType.DMA((n,)), pltpu.SemaphoreType.DMA((n,))],
    )(*ghs)


def _update(name, c1, gh, got, w, m, v):
    half, cols = gh.shape
    wcols = w.shape[1]

    def body(c_ref, gh_ref, got_ref, w_ref, m_ref, v_ref, g_o, d_o, m_o, v_o):
        g = jnp.where(pl.program_id(0) == c_ref[0], gh_ref[:, :wcols], got_ref[:, :wcols])
        delta, mn, vn = _adamw(w_ref[...], g, m_ref[...], v_ref[...])
        g_o[...] = g
        d_o[...] = delta
        m_o[...] = mn
        v_o[...] = vn

    same = pl.BlockSpec((half, cols), lambda h, c_ref: (0, 0))
    rows = pl.BlockSpec((half, wcols), lambda h, c_ref: (h, 0))
    return pl.pallas_call(
        body, name="update_" + name,
        grid_spec=pltpu.PrefetchScalarGridSpec(
            num_scalar_prefetch=1, grid=(2,), in_specs=[same, same, rows, rows, rows],
            out_specs=[rows, rows, rows, rows]),
        out_shape=[jax.ShapeDtypeStruct(w.shape, F32)] * 4, compiler_params=_params(40),
    )(c1, gh, got, w, m, v)


SMALL_NAMES = ("sg_w",) + VEC_NAMES
VEC_ROWS = 24
VEC_ROW = {"f_bias": 0, "sg_ln_g": 1, "sg_ln_b": 2, "att_out_g": 3, "sg_out_g": 4, "pre_mix_g": 5,
           "post_mix_g": 6, "pre_ffn_g": 7, "sg_b": 8, "post_ffn_g": 16, "ple_gate_b": 17}
LOSS_VEC_ROW = 18


def _small_pack(g, loss_l):
    n = len(SMALL_NAMES)

    def body(*refs):
        g_r = dict(zip(SMALL_NAMES, refs[0:n]))
        loss_r, vec_o, w_o = refs[n:]
        vec_o[...] = jnp.zeros((VEC_ROWS, 1024), F32)
        for name in VEC_NAMES:
            val = g_r[name][...]
            vec_o[pl.ds(VEC_ROW[name], val.shape[0]), pl.ds(0, val.shape[1])] = val
        vec_o[pl.ds(LOSS_VEC_ROW, 1), :] = loss_r[...] * (0.5 / D)
        rr = lax.broadcasted_iota(jnp.int32, (CH, CH), 0)
        cc = lax.broadcasted_iota(jnp.int32, (CH, CH), 1)
        w_o[...] = jnp.where((cc <= rr)[None], g_r["sg_w"][...], 0.0)

    vm = pl.BlockSpec(memory_space=pltpu.VMEM)
    args = [g[k] for k in SMALL_NAMES] + [loss_l]
    return pl.pallas_call(
        body, name="small_pack", in_specs=[vm] * len(args), out_specs=[vm, vm],
        out_shape=[jax.ShapeDtypeStruct((VEC_ROWS, 1024), F32), jax.ShapeDtypeStruct((8, CH, CH), F32)],
    )(*args)


def _small_peers(x, y, c):
    rels = [(rx, ry, rc) for rx in (0, 1) for ry in (0, 1) for rc in (0, 1)][1:]
    return [((x + rx) % 2, (y + ry) % 2, (c + rc) % 2) for rx, ry, rc in rels]


def _small_start(vec, w8):
    def body(vec_ref, w_ref, lv_ref, lw_ref, send_sems, recv_sems, vec_thru, w_thru, lv_thru, lw_thru, token):
        x, y, c = _me()
        me = 4 * x + 2 * y + c
        for j, to in enumerate(_small_peers(x, y, c)):
            for i, (src, land) in enumerate(((vec_ref, lv_ref), (w_ref, lw_ref))):
                pltpu.make_async_remote_copy(
                    src_ref=src, dst_ref=land.at[me], send_sem=send_sems.at[2 * j + i],
                    recv_sem=recv_sems.at[2 * j + i], device_id=to, device_id_type=MESH).start()
        token[...] = jnp.zeros(token.shape, F32)

    ops = [vec, w8, lax.empty((8,) + vec.shape, F32), lax.empty((8,) + w8.shape, F32)]
    hbm = lambda t: pltpu.HBM(t.shape, t.dtype)
    res = pl.pallas_call(
        body, name="small_start",
        out_shape=(pltpu.SemaphoreType.DMA((14,)), pltpu.SemaphoreType.DMA((14,)), *[hbm(t) for t in ops],
                   jax.ShapeDtypeStruct((8, 128), F32)),
        in_specs=(HBM_SPEC,) * 4,
        out_specs=(SEM_SPEC, SEM_SPEC) + (HBM_SPEC,) * 4 + (pl.BlockSpec(memory_space=pltpu.VMEM),),
        input_output_aliases={i: 2 + i for i in range(4)},
        compiler_params=pltpu.CompilerParams(has_side_effects=EFFECT),
    )(*[pltpu.with_memory_space_constraint(t, pltpu.HBM) for t in ops])
    return res[0], res[1], res[2:6], res[6]


def _small_wait(send_sems, recv_sems, thru, after):
    def body(vec_ref, w_ref, lv_ref, lw_ref, send_sems, recv_sems, after_ref, vec_o, w_o, lv_o, lw_o):
        x, y, c = _me()
        for j, (px, py, pc) in enumerate(_small_peers(x, y, c)):
            for i, (src, land) in enumerate(((vec_ref, lv_ref), (w_ref, lw_ref))):
                cp = pltpu.make_async_remote_copy(
                    src_ref=src, dst_ref=land.at[4 * px + 2 * py + pc], send_sem=send_sems.at[2 * j + i],
                    recv_sem=recv_sems.at[2 * j + i], device_id=(px, py, pc), device_id_type=MESH)
                cp.wait_send()
                cp.wait_recv()

    hbm = lambda t: pltpu.HBM(t.shape, t.dtype)
    return pl.pallas_call(
        body, name="small_wait", out_shape=tuple(hbm(t) for t in thru),
        in_specs=(HBM_SPEC,) * 4 + (SEM_SPEC, SEM_SPEC, pl.BlockSpec(memory_space=pl.ANY)),
        out_specs=(HBM_SPEC,) * 4, input_output_aliases={i: i for i in range(4)},
        compiler_params=pltpu.CompilerParams(has_side_effects=EFFECT),
    )(*thru, send_sems, recv_sems, after)


def _small_update(all_v, all_w, w, m, v):
    n = len(SMALL_NAMES)

    def body(*refs):
        allv_r, allw_r = refs[0], refs[1]
        tot_v = allv_r[0]
        tot_w = allw_r[0]
        for d in range(1, 8):
            tot_v = tot_v + allv_r[d]
            tot_w = tot_w + allw_r[d]
        w_r = dict(zip(SMALL_NAMES, refs[2:2 + n]))
        m_r = dict(zip(SMALL_NAMES, refs[2 + n:2 + 2 * n]))
        v_r = dict(zip(SMALL_NAMES, refs[2 + 2 * n:2 + 3 * n]))
        loss_o = refs[2 + 3 * n]
        outs = refs[3 + 3 * n:]
        loss_o[...] = jnp.sum(tot_v[LOSS_VEC_ROW:LOSS_VEC_ROW + 1, :], axis=-1, keepdims=True) + jnp.zeros((1, 128), F32)
        for i, name in enumerate(SMALL_NAMES):
            if name == "sg_w":
                gt = tot_w
            else:
                rows, width = w_r[name].shape
                gt = tot_v[VEC_ROW[name]:VEC_ROW[name] + rows, 0:width]
            delta, mn, vn = _adamw(w_r[name][...], gt, m_r[name][...], v_r[name][...])
            outs[4 * i][...] = gt
            outs[4 * i + 1][...] = delta
            outs[4 * i + 2][...] = mn
            outs[4 * i + 3][...] = vn

    vm = pl.BlockSpec(memory_space=pltpu.VMEM)
    args = [all_v, all_w] + [d[k] for d in (w, m, v) for k in SMALL_NAMES]
    out_shape = [jax.ShapeDtypeStruct((1, 128), F32)]
    out_shape += [jax.ShapeDtypeStruct(w[k].shape, F32) for k in SMALL_NAMES for _ in range(4)]
    res = pl.pallas_call(
        body, name="small_update", in_specs=[vm] * len(args), out_specs=[vm] * len(out_shape), out_shape=out_shape,
        compiler_params=pltpu.CompilerParams(vmem_limit_bytes=32 * 1024 * 1024),
    )(*args)
    return res[0], {k: res[1 + 4 * i:5 + 4 * i] for i, k in enumerate(SMALL_NAMES)}


def _win_kernel_order(gathered):
    w_in = jnp.concatenate([gathered[k].reshape(D, 768)[:, :642] for k in range(4)], axis=1)
    return jnp.concatenate([w_in[:, :3 * AW], w_in[:, 3 * AW + NH:], w_in[:, 3 * AW:3 * AW + NH],
                            jnp.zeros((D, 128 - NH), w_in.dtype)], axis=1)


LATE_ROWS = 256 + 1024 + 1024 + 64 + 256


def _pack_late(w_out, w1, w2, plew, wg):
    return jnp.concatenate([w_out, w1, w2, plew.reshape(64, 1024), wg], axis=0)


def _unpack_late(gathered):
    return (gathered[:, 0:256].reshape(D, D), gathered[:, 256:1280], gathered[:, 1280:2304].reshape(DFF, D),
            gathered[:, 2304:2368].reshape(4, 256, 256), gathered[:, 2368:2624].reshape(D, D))


def _local_step(x, p, tgt, win_k, late_weights, token, on_ff_grads, on_tail_grads, on_small_grads, small):
    T = x.shape[0]
    row = lambda n: small[n].reshape(1, -1)
    fbias = jnp.pad(row("f_bias"), ((0, 0), (0, 128 - NH))) + token[0:1, :]
    wm = _masked_sg_w(small["sg_w"].reshape(8, CH, CH))
    wmb = wm.astype(BF16)
    wmt = jnp.swapaxes(wm, 1, 2).astype(BF16)
    bsg = jnp.repeat(small["sg_b"].reshape(8, CH).T, DH, axis=1)
    ln_g, ln_b, gsg, gatt = row("sg_ln_g"), row("sg_ln_b"), row("sg_out_g"), row("att_out_g")
    gpre, gpm, gpf, gpff, bg = row("pre_mix_g"), row("post_mix_g"), row("pre_ffn_g"), row("post_ffn_g"), row("ple_gate_b")
    gsel = (jnp.arange(AW)[:, None] // DH == jnp.arange(128)[None, :]).astype(BF16)

    expand, shrink, pieces, qconst, one64, one67, pick64, pick67 = _head_consts()
    a, flog, zuv, ysgn, q8, k8, v8 = _pre_attn_fwd(
        x, gpre, win_k, fbias, ln_g, ln_b, wmb, bsg, gsg, expand, pieces, qconst, one67, one64)

    slabs = lambda t: jnp.swapaxes(t.reshape(T // TQ, TQ, NH * 128), 1, 2)
    qt8 = slabs(q8)
    lanes = jnp.arange(128)
    sel = jnp.stack([((lanes[:, None] == lanes[None, :] - DH * j) & (lanes[:, None] < DH)).astype(BF16)
                     for j in (0, 1)])

    yatt, lse = _flash_fwd(qt8, k8, slabs(v8), sel)
    wout, w1, w2, plew, wg = late_weights(lse)
    y, ov, h1, c2, sact, rr = _tail_fwd1(x, yatt, ysgn, gatt, wout, gpm, gpf, w1)
    ff, h2b, de, dpre, dh2, loss_l, dbg = _tail_fwd2(sact, h1, p, tgt, w2, gpff, wg, bg, plew)
    dff, dr, do, do8, dlt, dysg, dh1, dgpff, dgpf, dgpm, dgatt = _tail_bwd(
        dh2, ff, rr, h1, ov, yatt, w2, w1, wout, gpff, gpf, gpm, gatt, gsel, expand)
    dw1 = _matmul_tn("grad_w_ff1", c2, dr, shards=4)
    dw2 = _matmul_tn("grad_w_ff2", sact, dff)
    ff_sent = on_ff_grads(dw1, dw2)
    dwout = _matmul_tn("grad_w_out", y, do, after=ff_sent)
    dwg = _matmul_tn("grad_ple_gate_w", h2b, dpre, after=dwout)
    dplew = _matmul_tn("grad_ple_w", p, de, tn=256, shards=4, after=dwg)
    tail_token = on_tail_grads((dwout, dplew, dwg))
    dlt4 = jnp.pad(dlt[:, :NH].T.reshape(4, 2, T), ((0, 0), (0, 6), (0, 0))) + tail_token[0, 0]
    dqt, dk8, dv8 = _flash_bwd(q8, qt8, k8, v8, do8, slabs(do8), lse, dlt4)
    dx, dz, dgpre, dfb, dgsg, dlng, dlnb, dws, _, dsbt = _pre_attn_bwd(
        x, dh1, jnp.swapaxes(dqt, 1, 2).reshape(T, NH * 128), dk8, dv8, flog, zuv, dysg,
        gpre, win_k, ln_g, ln_b, wmb, wmt, bsg, gsg, gsel, shrink, pick64, pick67)


    dsb = dsbt[:, :8].T
    gsmall = {"sg_w": dws, "f_bias": dfb, "sg_ln_g": dlng, "sg_ln_b": dlnb, "sg_b": dsb,
              "att_out_g": dgatt, "sg_out_g": dgsg, "pre_mix_g": dgpre, "post_mix_g": dgpm, "pre_ffn_g": dgpf,
              "post_ffn_g": dgpff, "ple_gate_b": dbg}
    on_small_grads(gsmall, loss_l)
    dwin_k = _matmul_tn("grad_w_in", a, dz, tn=384)
    return loss_l, dx, dwin_k, gsmall


def kernel(x, p, w_in, f_bias, sg_ln_g, sg_ln_b, sg_w, sg_b, att_out_g, sg_out_g, w_out, pre_mix_g, post_mix_g, pre_ffn_g, post_ffn_g, w_ff1, w_ff2, ple_w, ple_gate_w, ple_gate_b, loss_target, m_w_in, m_f_bias, m_sg_ln_g, m_sg_ln_b, m_sg_w, m_sg_b, m_att_out_g, m_sg_out_g, m_w_out, m_pre_mix_g, m_post_mix_g, m_pre_ffn_g, m_post_ffn_g, m_w_ff1, m_w_ff2, m_ple_w, m_ple_gate_w, m_ple_gate_b, v_w_in, v_f_bias, v_sg_ln_g, v_sg_ln_b, v_sg_w, v_sg_b, v_att_out_g, v_sg_out_g, v_w_out, v_pre_mix_g, v_post_mix_g, v_pre_ffn_g, v_post_ffn_g, v_w_ff1, v_w_ff2, v_ple_w, v_ple_gate_w, v_ple_gate_b):
    c = lax.axis_index("c")
    big = lambda t: (t[0][0], t[1][0], t[2][0], t[3][0], t[4][0], t[5][0])
    w_big = big((w_in, w_out, w_ff1, w_ff2, ple_w, ple_gate_w))
    m_big = big((m_w_in, m_w_out, m_w_ff1, m_w_ff2, m_ple_w, m_ple_gate_w))
    v_big = big((v_w_in, v_w_out, v_w_ff1, v_w_ff2, v_ple_w, v_ple_gate_w))
    small = {"sg_w": sg_w, "f_bias": f_bias, "sg_ln_g": sg_ln_g, "sg_ln_b": sg_ln_b, "sg_b": sg_b,
             "att_out_g": att_out_g, "sg_out_g": sg_out_g, "pre_mix_g": pre_mix_g, "post_mix_g": post_mix_g,
             "pre_ffn_g": pre_ffn_g, "post_ffn_g": post_ffn_g, "ple_gate_b": ple_gate_b}
    m_small = {"sg_w": m_sg_w, "f_bias": m_f_bias, "sg_ln_g": m_sg_ln_g, "sg_ln_b": m_sg_ln_b, "sg_b": m_sg_b,
               "att_out_g": m_att_out_g, "sg_out_g": m_sg_out_g, "pre_mix_g": m_pre_mix_g,
               "post_mix_g": m_post_mix_g, "pre_ffn_g": m_pre_ffn_g, "post_ffn_g": m_post_ffn_g,
               "ple_gate_b": m_ple_gate_b}
    v_small = {"sg_w": v_sg_w, "f_bias": v_f_bias, "sg_ln_g": v_sg_ln_g, "sg_ln_b": v_sg_ln_b, "sg_b": v_sg_b,
               "att_out_g": v_att_out_g, "sg_out_g": v_sg_out_g, "pre_mix_g": v_pre_mix_g,
               "post_mix_g": v_post_mix_g, "pre_ffn_g": v_pre_ffn_g, "post_ffn_g": v_post_ffn_g,
               "ple_gate_b": v_ple_gate_b}

    k_me = 2 * lax.axis_index("x") + lax.axis_index("y")
    own_slot = lambda got, mine: lax.dynamic_update_slice(got, mine[None], (k_me, 0, 0))
    late_mine = _pack_late(*w_big[1:]).astype(BF16)
    late = _gather_late_start(late_mine)
    win_mine = jnp.pad(w_big[0], ((0, 0), (0, 768 - 642))).reshape(768, 1024).astype(BF16)
    win_k = _win_kernel_order(own_slot(_gather_weights(win_mine), win_mine))
    late_weights = lambda after: _unpack_late(
        own_slot(_gather_late_wait(late[0], late[1], late[2], late[3], after), late_mine))

    names = ("w_in", "w_out", "w_ff1", "w_ff2", "ple_w", "ple_gate_w")
    c1 = jnp.reshape(c, (1,)).astype(jnp.int32)
    own_part = lambda parts, pss: [lax.dynamic_update_slice(pt, lax.dynamic_slice_in_dim(ps, k_me, 1, 0), (k_me, 0, 0))
                                   for pt, ps in zip(parts, pss)]
    tail = {}

    def on_ff_grads(dw1, dw2):
        tail["swap"] = _swap_start([dw1, dw2.reshape(4, D, D)], "ff")
        return tail["swap"][2][0]

    def on_tail_grads(grads):
        dwout, dplew, dwg = grads
        ws, wr, g_thru, land_thru = tail["swap"]
        (dw1, dw2), (got1, got2) = _swap_wait(ws, wr, g_thru, land_thru, dplew, "ff")
        rest = [dwout.reshape(4, 256, D), dplew, dwg.reshape(4, 256, D)]
        got_out, got_ple, got_gate = _swap_halves(rest, "late")
        gs = [rest[0], dw1, dw2, rest[1], rest[2]]
        gots = [got_out, got1, got2, got_ple, got_gate]
        pss = [_pair_sum(nm, c1, g, got) for nm, g, got in zip(names[1:], gs, gots)]
        tail["xch"] = _exchange_start(pss, "late")
        return tail["xch"][4]

    def on_small_grads(gsmall, loss_l):
        tail["small"] = _small_start(*_small_pack(gsmall, loss_l))

    loss_l, dx, dwin_k, gsmall = _local_step(
        x[0], p[0, 0], loss_target[0], win_k, late_weights, late[4], on_ff_grads, on_tail_grads, on_small_grads,
        small)

    dwin = jnp.concatenate([dwin_k[:, :3 * AW], dwin_k[:, 5 * AW:5 * AW + NH], dwin_k[:, 3 * AW:5 * AW]], axis=1)
    dwin = jnp.stack([jnp.pad(dwin[:, 642 * k:642 * (k + 1)], ((0, 0), (0, 768 - 642))) for k in range(4)])
    ps_in = [_pair_sum(names[0], c1, dwin, _swap_halves([dwin], "in")[0])]
    ins, inr, in_thru, inland_thru, in_token = _exchange_start(ps_in, "in")
    xs, xr, ps_thru, land_thru, _ = tail["xch"]
    ps_late, landed = _exchange_wait(xs, xr, ps_thru, land_thru, in_token, "late")

    def finish(nms, parts, tag, w, m, v):
        ghs = [_reduce_chips(nm, pt) for nm, pt in zip(nms, parts)]
        got2 = _share_grad(ghs, tag)
        return [_update(nm, c1, gh, g2, wi, mi, vi) for nm, gh, g2, wi, mi, vi in zip(nms, ghs, got2, w, m, v)]

    late_out = finish(names[1:], own_part(landed, ps_late), "late", w_big[1:], m_big[1:], v_big[1:])
    ps_in, landed_in = _exchange_wait(ins, inr, in_thru, inland_thru, late_out[-1][3], "in")
    big_out = finish(names[:1], own_part(landed_in, ps_in), "in", w_big[:1], m_big[:1], v_big[:1])
    big_out += late_out
    big_out = [[big_out[j][i] for j in range(6)] for i in range(4)]

    view = lambda t: t.reshape(t.shape[-3:]) if t.ndim == 4 else t.reshape(t.shape[-2:])
    views = lambda d: {k: view(d[k]) for k in SMALL_NAMES}
    me = 4 * lax.axis_index("x") + 2 * lax.axis_index("y") + c
    ss, sr, sthru, _ = tail["small"]
    vec, w8, lv, lw = _small_wait(ss, sr, sthru, big_out[0][3])
    all_v = lax.dynamic_update_slice(lv, vec[None], (me, 0, 0))
    all_w = lax.dynamic_update_slice(lw, w8[None], (me, 0, 0, 0))
    loss11, res_s = _small_update(all_v, all_w, views(small), views(m_small), views(v_small))
    loss = loss11[0, 0]

    def small_out(i, name):
        return res_s[name][i].reshape(small[name].shape)

    order = ["w_in", "f_bias", "sg_ln_g", "sg_ln_b", "sg_w", "sg_b", "att_out_g", "sg_out_g", "w_out",
             "pre_mix_g", "post_mix_g", "pre_ffn_g", "post_ffn_g", "w_ff1", "w_ff2", "ple_w", "ple_gate_w",
             "ple_gate_b"]
    big_idx = {"w_in": 0, "w_out": 1, "w_ff1": 2, "w_ff2": 3, "ple_w": 4, "ple_gate_w": 5}
    outs = [loss, dx[None]]
    for i in range(4):
        for name in order:
            if name in big_idx:
                outs.append(big_out[i][big_idx[name]][None])
            else:
                outs.append(small_out(i, name))
    return tuple(outs)
```

```python
import math

import jax
import jax.numpy as jnp
from jax import lax
from jax.experimental import pallas as pl
from jax.experimental.pallas import tpu as pltpu

F32 = jnp.float32
BF16 = jnp.bfloat16
MESH = pl.DeviceIdType.MESH

D = 1024
DH = 64
NH = 8
AW = 512
CH = 128
DFF = 4096
ZW = 5 * AW + 128
EPS = 1e-6
NEG = -1e30
MASKED = -2e30

TM = 256
TQ = 256

LR, B1, B2, AEPS, WD, STEP = 0.001, 0.9, 0.999, 1e-08, 0.01, 10
BC1 = 1.0 - B1 ** STEP
BC2 = 1.0 - B2 ** STEP

VEC_NAMES = ("f_bias", "sg_ln_g", "sg_ln_b", "sg_b", "att_out_g", "sg_out_g", "pre_mix_g",
             "post_mix_g", "pre_ffn_g", "post_ffn_g", "ple_gate_b")


def _dot(a, b):
    return jnp.dot(a, b, preferred_element_type=F32)


def _dot_nt(a, b):
    return lax.dot_general(a, b, (((1,), (1,)), ((), ())), preferred_element_type=F32)


def _dot_tn(a, b):
    return lax.dot_general(a, b, (((0,), (0,)), ((), ())), preferred_element_type=F32)


def _split3(x):
    h = x.astype(BF16)
    r = x - h.astype(F32)
    m = r.astype(BF16)
    l = (r - m.astype(F32)).astype(BF16)
    return h, m, l


def _dot01(sel, x):
    h, m, l = _split3(x)
    return _dot(sel, h) + _dot(sel, m) + _dot(sel, l)


def _dot01_r(x, sel):
    h, m, l = _split3(x)
    return _dot(h, sel) + _dot(m, sel) + _dot(l, sel)


def _dot01_tn(x, sel):
    h, m, l = _split3(x)
    return _dot_tn(h, sel) + _dot_tn(m, sel) + _dot_tn(l, sel)


def _rs(x, n):
    return lax.rsqrt(jnp.sum(x * x, axis=-1, keepdims=True) * (1.0 / n) + EPS)


def _rms_bwd(dn, x, rs, g, n):
    w = dn * g
    dx = rs * w - x * ((rs * rs * rs) * (1.0 / n) * jnp.sum(w * x, axis=-1, keepdims=True))
    return dx, jnp.sum(dn * x * rs, axis=0, keepdims=True)


_GC = math.sqrt(2.0 / math.pi)


def _gelu(x):
    t = jnp.tanh(_GC * (x + 0.044715 * x * x * x))
    return 0.5 * x * (1.0 + t), t


def _gelu_grad(x, t):
    return 0.5 * (1.0 + t) + 0.5 * x * (1.0 - t * t) * (_GC * (1.0 + 3.0 * 0.044715 * x * x))


def _params(vmem_mb, sem=("arbitrary",)):
    return pltpu.CompilerParams(dimension_semantics=sem, vmem_limit_bytes=vmem_mb * 1024 * 1024)


def _row_call(name, body, T, tm, tiled, resident, outs, accs, scratch=(), reverse=False, vmem_mb=48):
    nt = T // tm
    n_t, n_r, n_o, n_a = len(tiled), len(resident), len(outs), len(accs)

    def kern(*refs):
        t_refs = refs[:n_t]
        r_hbm = refs[n_t:n_t + n_r]
        o_refs = refs[n_t + n_r:n_t + n_r + n_o]
        a_refs = refs[n_t + n_r + n_o:n_t + n_r + n_o + n_a]
        r_vmem = refs[n_t + n_r + n_o + n_a:n_t + 2 * n_r + n_o + n_a]
        s_refs = refs[n_t + 2 * n_r + n_o + n_a:]

        @pl.when(pl.program_id(0) == 0)
        def _():
            for h, v in zip(r_hbm, r_vmem):
                pltpu.sync_copy(h, v)
            for a in a_refs + s_refs:
                a[...] = jnp.zeros(a.shape, a.dtype)

        body(t_refs, r_vmem, o_refs, a_refs, s_refs)

    if reverse:
        idx = lambda i: (nt - 1 - i, 0)
        idx_t = lambda i: (nt - 1 - i, 0, 0)
    else:
        idx = lambda i: (i, 0)
        idx_t = lambda i: (i, 0, 0)
    arrays, in_specs = [], []
    for a in tiled:
        if isinstance(a, tuple):
            arrays.append(a[0])
            in_specs.append(pl.BlockSpec((None, a[0].shape[1], tm), idx_t))
        else:
            arrays.append(a)
            in_specs.append(pl.BlockSpec((tm, a.shape[1]), idx))
    in_specs += [pl.BlockSpec(memory_space=pl.ANY) for _ in resident]
    out_shape, out_specs = [], []
    for o in outs:
        if len(o) == 3:
            out_shape.append(jax.ShapeDtypeStruct((nt, o[0], tm), o[1]))
            out_specs.append(pl.BlockSpec((None, o[0], tm), idx_t))
        else:
            out_shape.append(jax.ShapeDtypeStruct((T, o[0]), o[1]))
            out_specs.append(pl.BlockSpec((tm, o[0]), idx))
    out_shape += [jax.ShapeDtypeStruct(s, F32) for s in accs]
    out_specs += [pl.BlockSpec(s, lambda i, n=len(s): (0,) * n) for s in accs]
    scratch_shapes = [pltpu.VMEM(r.shape, r.dtype) for r in resident]
    scratch_shapes += [pltpu.VMEM(s, F32) for s in scratch]
    return pl.pallas_call(
        kern, name=name, grid=(nt,), in_specs=in_specs, out_specs=out_specs, out_shape=out_shape,
        scratch_shapes=scratch_shapes, compiler_params=_params(vmem_mb),
    )(*arrays, *resident)


def _sg_forward(zu, zv, wm_ref, bsg, lng, lnb, mixed_ref, tm):
    gu, tu = _gelu(zu)
    vg, tv = _gelu(zv)
    mu = jnp.sum(vg, axis=-1, keepdims=True) * (1.0 / AW)
    xc = vg - mu
    rstd = lax.rsqrt(jnp.sum(xc * xc, axis=-1, keepdims=True) * (1.0 / AW) + EPS)
    xhat = xc * rstd
    vvb = (xhat * lng + lnb).astype(BF16)
    lane = lax.broadcasted_iota(jnp.int32, (CH, 128), 1)
    for c in range(tm // CH):
        for j in range(4):
            blk = vvb[c * CH:(c + 1) * CH, j * 128:(j + 1) * 128]
            m0 = _dot(wm_ref[2 * j], blk)
            m1 = _dot(wm_ref[2 * j + 1], blk)
            mixed_ref[c * CH:(c + 1) * CH, j * 128:(j + 1) * 128] = (
                jnp.where(lane < DH, m0, m1) + bsg[:, j * 128:(j + 1) * 128])
    return gu, tu, tv, xhat, rstd, vvb, mixed_ref[...]


def _head_consts():
    src = jnp.arange(AW)
    dst = (src // DH) * 128 + src % DH
    wide = jnp.arange(NH * 128)
    expand = (dst[:, None] == wide[None, :]).astype(BF16)
    heads = jnp.arange(128)
    pieces = jnp.stack([((heads[:, None] * 128 + DH + i == wide[None, :]) & (heads[:, None] < NH)).astype(BF16)
                        for i in range(3)])
    spare = wide % 128 - DH
    qconst = jnp.where((spare >= 0) & (spare < 3), -1.0, 0.0).astype(F32)[None, :]
    one64 = jnp.where(spare == 0, 1.0, 0.0).astype(F32)[None, :]
    one67 = jnp.where(spare == 3, 1.0, 0.0).astype(F32)[None, :]
    pick64 = ((wide[:, None] == heads[None, :] * 128 + DH) & (heads[None, :] < NH)).astype(BF16)
    pick67 = ((wide[:, None] == heads[None, :] * 128 + DH + 3) & (heads[None, :] < NH)).astype(BF16)
    return expand, expand.T, pieces, qconst, one64, one67, pick64, pick67


def _masked_sg_w(sg_w):
    r = lax.broadcasted_iota(jnp.int32, (CH, CH), 0)
    c = lax.broadcasted_iota(jnp.int32, (CH, CH), 1)
    return jnp.where((c <= r)[None], sg_w, 0.0)


def _pre_attn_fwd(x, gpre, win, fbias, lng, lnb, wm, bsg, gsg, expand, pieces, qconst, kconst, vconst):
    T = x.shape[0]
    tm = TM

    def body(t, r, o, a, s):
        (x_ref,) = t
        gpre_r, win_r, fb_r, lng_r, lnb_r, wm_r, bsg_r, gsg_r, ex_r, pc_r, qc_r, kc_r, vc_r = r
        a_o, flog_o, zuv_o, ysgn_o, q8_o, k8_o, v8_o = o
        carry_ref, mixed_ref = s
        xv = x_ref[...]
        av = (xv * _rs(xv, D) * gpre_r[...]).astype(BF16)
        a_o[...] = av
        z = _dot(av, win_r[...])
        zu = z[:, 3 * AW:4 * AW]
        zv = z[:, 4 * AW:5 * AW]
        zuv_o[:, 0:AW] = zu
        zuv_o[:, AW:2 * AW] = zv
        zf = z[:, 5 * AW:] + fb_r[...]
        flog_o[...] = zf
        lane = lax.broadcasted_iota(jnp.int32, (tm, 128), 1)
        logf = jnp.where(lane < NH, jnp.minimum(zf, 0.0) - jnp.log(1.0 + jnp.exp(-jnp.abs(zf))), 0.0)
        rr = lax.broadcasted_iota(jnp.int32, (tm, tm), 0)
        cc = lax.broadcasted_iota(jnp.int32, (tm, tm), 1)
        tri = (cc <= rr).astype(BF16)
        cum = _dot01(tri, logf) + carry_ref[...]
        carry_ref[...] = cum[tm - 1:tm, :]
        ex = ex_r[...]
        q8_o[...] = (_dot((z[:, 0:AW] * (DH ** -0.5)).astype(BF16), ex) + qc_r[...]).astype(BF16)
        ch, cm, cl = _split3(cum)
        k8_o[...] = (_dot(z[:, AW:2 * AW].astype(BF16), ex) + _dot(ch, pc_r[0]) + _dot(cm, pc_r[1])
                     + _dot(cl, pc_r[2]) + kc_r[...]).astype(BF16)
        v8_o[...] = (_dot(z[:, 2 * AW:3 * AW].astype(BF16), ex) + vc_r[...]).astype(BF16)
        gu, _, _, _, _, _, mixed = _sg_forward(zu, zv, wm_r, bsg_r[...], lng_r[...], lnb_r[...], mixed_ref, tm)
        ysg = gu * mixed
        ysgn_o[...] = (ysg * _rs(ysg, AW) * gsg_r[...]).astype(BF16)

    return _row_call(
        "pre_attn_fwd", body, T, tm, [x],
        [gpre, win, fbias, lng, lnb, wm, bsg, gsg, expand, pieces, qconst, kconst, vconst],
        [(D, BF16), (128, F32), (2 * AW, F32), (AW, BF16), (NH * 128, BF16), (NH * 128, BF16), (NH * 128, BF16)], [],
        scratch=[(1, 128), (tm, AW)], vmem_mb=48)


def _flash_fwd(qt8, k8, vt8, sel):
    T = k8.shape[0]
    nq = T // TQ

    def body(qt_ref, k_ref, vt_ref, sel_ref, o_ref, l_ref, u_scr, p_scr):
        qi = pl.program_id(1)
        qts = (qt_ref[0:128, :], qt_ref[128:256, :])
        dmat = (lax.broadcasted_iota(jnp.int32, (TQ, TQ), 0) - lax.broadcasted_iota(jnp.int32, (TQ, TQ), 1))
        u_scr[1] = jnp.full((2, TQ, TQ), MASKED, F32)
        p_scr[...] = jnp.zeros(p_scr.shape, BF16)

        def sub(t, carry, sc, sb, masked):
            blk_c = jnp.clip(t - 2, 0, qi)
            off_a = pl.multiple_of(jnp.minimum(t, qi) * TQ, TQ)
            new = []
            for j in (0, 1):
                m, al, acc = carry[j]
                acc = al * acc + _dot(vt_ref[blk_c, j * 128:(j + 1) * 128, :], p_scr[sc, j])
                m_new = jnp.maximum(m, jnp.max(u_scr[sb, j], axis=0, keepdims=True))
                p_scr[sb, j] = jnp.exp(u_scr[sb, j] - m_new).astype(BF16)
                u = _dot(k_ref[pl.ds(off_a, TQ), j * 128:(j + 1) * 128], qts[j])
                u_scr[sc, j] = jnp.where(dmat <= (qi - t) * TQ, u, MASKED) if masked else u
                new.append((m_new, jnp.exp(m - m_new), acc))
            return tuple(new)

        def pair(t2, carry, masked):
            return sub(2 * t2 + 1, sub(2 * t2, carry, 0, 1, masked), 1, 0, masked)

        init = tuple((jnp.full((1, TQ), NEG, F32), jnp.ones((1, TQ), F32), jnp.zeros((128, TQ), F32))
                     for _ in (0, 1))
        carry = lax.fori_loop(0, qi // 2, lambda t2, cr: pair(t2, cr, False), init)
        (m0, _, a0), (m1, _, a1) = pair(qi // 2 + 1, pair(qi // 2, carry, True), True)
        l0 = a0[DH:DH + 1, :]
        l1 = a1[DH:DH + 1, :]
        o_ref[...] = _dot01_tn(a0 * (1.0 / l0), sel_ref[0]) + _dot01_tn(a1 * (1.0 / l1), sel_ref[1])
        l_ref[0:1, :] = m0 + jnp.log(l0)
        l_ref[1:2, :] = m1 + jnp.log(l1)
        l_ref[2:8, :] = jnp.zeros((6, TQ), F32)

    return pl.pallas_call(
        body, name="flash_fwd", grid=(4, nq),
        in_specs=[pl.BlockSpec((None, 256, TQ), lambda h, i: (i, h, 0)),
                  pl.BlockSpec((T, 256), lambda h, i: (0, h)),
                  pl.BlockSpec((nq, 256, TQ), lambda h, i: (0, h, 0)),
                  pl.BlockSpec((2, 128, 128), lambda h, i: (0, 0, 0))],
        out_specs=[pl.BlockSpec((TQ, 128), lambda h, i: (i, h)),
                   pl.BlockSpec((None, 8, TQ), lambda h, i: (h, 0, i))],
        out_shape=[jax.ShapeDtypeStruct((T, AW), F32), jax.ShapeDtypeStruct((4, 8, T), F32)],
        scratch_shapes=[pltpu.VMEM((2, 2, TQ, TQ), F32), pltpu.VMEM((2, 2, TQ, TQ), BF16)],
        compiler_params=_params(40, ("arbitrary", "arbitrary")),
    )(qt8, k8, vt8, sel)


def _flash_bwd(q8, qt8, k8, v8, do8, dot8, lse, dlt):
    T = q8.shape[0]
    nk = T // TQ

    def body(q_ref, qt_ref, k_ref, v_ref, do_ref, dot_ref, l_ref, d_ref, dqt_ref, dk_ref, dv_ref,
             u_scr, dp_scr, p_scr, ds_scr):
        kb = pl.program_id(1)
        n = nk - kb

        @pl.when(kb == 0)
        def _():
            dqt_ref[...] = jnp.zeros(dqt_ref.shape, F32)

        dk_ref[...] = jnp.zeros(dk_ref.shape, F32)
        dv_ref[...] = jnp.zeros(dv_ref.shape, F32)
        u_scr[1] = jnp.full((2, TQ, TQ), MASKED, F32)
        dp_scr[1] = jnp.zeros((2, TQ, TQ), F32)
        p_scr[...] = jnp.zeros(p_scr.shape, BF16)
        ds_scr[...] = jnp.zeros(ds_scr.shape, BF16)
        dmat = (lax.broadcasted_iota(jnp.int32, (TQ, TQ), 0) - lax.broadcasted_iota(jnp.int32, (TQ, TQ), 1))
        ks = (k_ref[:, 0:128], k_ref[:, 128:256])
        vs = (v_ref[:, 0:128], v_ref[:, 128:256])

        def sub(t, sc, sb):
            blk_a = kb + jnp.minimum(t, n - 1)
            blk_c = kb + jnp.clip(t - 2, 0, n - 1)
            off_b = pl.multiple_of((kb + jnp.clip(t - 1, 0, n - 1)) * TQ, TQ)
            off_c = pl.multiple_of(blk_c * TQ, TQ)
            lim = jnp.where(t < n, t * TQ, -TQ)
            for j in (0, 1):
                hl = slice(j * 128, (j + 1) * 128)
                dqt_ref[blk_c, hl, :] += _dot_tn(ks[j], ds_scr[sc, j])
                dk_ref[:, hl] += _dot(ds_scr[sc, j], q_ref[pl.ds(off_c, TQ), hl])
                dv_ref[:, hl] += _dot(p_scr[sc, j], do_ref[pl.ds(off_c, TQ), hl])
                p = jnp.exp(u_scr[sb, j] - l_ref[j:j + 1, pl.ds(off_b, TQ)])
                p_scr[sb, j] = p.astype(BF16)
                ds_scr[sb, j] = (p * (dp_scr[sb, j] - d_ref[j:j + 1, pl.ds(off_b, TQ)])).astype(BF16)
                u_scr[sc, j] = jnp.where(dmat <= lim, _dot(ks[j], qt_ref[blk_a, hl, :]), MASKED)
                dp_scr[sc, j] = _dot(vs[j], dot_ref[blk_a, hl, :])

        def it(t2, carry):
            sub(2 * t2, 0, 1)
            sub(2 * t2 + 1, 1, 0)
            return carry

        lax.fori_loop(0, n // 2 + 1, it, 0)

        @pl.when(n % 2 == 1)
        def _():
            sub(n + 1, 0, 1)

    return pl.pallas_call(
        body, name="flash_bwd", grid=(4, nk),
        in_specs=[pl.BlockSpec((T, 256), lambda h, i: (0, h)),
                  pl.BlockSpec((nk, 256, TQ), lambda h, i: (0, h, 0)),
                  pl.BlockSpec((TQ, 256), lambda h, i: (i, h)),
                  pl.BlockSpec((TQ, 256), lambda h, i: (i, h)),
                  pl.BlockSpec((T, 256), lambda h, i: (0, h)),
                  pl.BlockSpec((nk, 256, TQ), lambda h, i: (0, h, 0)),
                  pl.BlockSpec((None, 8, T), lambda h, i: (h, 0, 0)),
                  pl.BlockSpec((None, 8, T), lambda h, i: (h, 0, 0))],
        out_specs=[pl.BlockSpec((nk, 256, TQ), lambda h, i: (0, h, 0)),
                   pl.BlockSpec((TQ, 256), lambda h, i: (i, h)),
                   pl.BlockSpec((TQ, 256), lambda h, i: (i, h))],
        out_shape=[jax.ShapeDtypeStruct((nk, NH * 128, TQ), F32), jax.ShapeDtypeStruct((T, NH * 128), F32),
                   jax.ShapeDtypeStruct((T, NH * 128), F32)],
        scratch_shapes=[pltpu.VMEM((2, 2, TQ, TQ), F32), pltpu.VMEM((2, 2, TQ, TQ), F32),
                        pltpu.VMEM((2, 2, TQ, TQ), BF16), pltpu.VMEM((2, 2, TQ, TQ), BF16)],
        compiler_params=_params(56, ("arbitrary", "arbitrary")),
    )(q8, qt8, k8, v8, do8, dot8, lse, dlt)


def _tail_fwd1(x, yatt, ysgn, gatt, wout, gpm, gpf, w1):
    T = x.shape[0]

    def body(t, r, o, a, s):
        x_ref, ya_ref, ys_ref = t
        gatt_r, wout_r, gpm_r, gpf_r, w1_r = r
        y_o, o_o, h1_o, c2_o, s_o, rr_o = o
        ya = ya_ref[...]
        yan = (ya * _rs(ya, AW) * gatt_r[...]).astype(BF16)
        y_o[:, 0:AW] = yan
        y_o[:, AW:] = ys_ref[...]
        ov = _dot(yan, wout_r[0:AW, :]) + _dot(ys_ref[...], wout_r[AW:, :])
        o_o[...] = ov
        h1 = x_ref[...] + ov * _rs(ov, D) * gpm_r[...]
        h1_o[...] = h1
        c2 = (h1 * _rs(h1, D) * gpf_r[...]).astype(BF16)
        c2_o[...] = c2
        for k in range(4):
            rr = jnp.maximum(_dot(c2, w1_r[k]), 0.0)
            rr_o[:, k * D:(k + 1) * D] = rr.astype(BF16)
            s_o[:, k * D:(k + 1) * D] = (rr * rr).astype(BF16)

    return _row_call(
        "tail_fwd1", body, T, TM, [x, yatt, ysgn], [gatt, wout, gpm, gpf, w1],
        [(D, BF16), (D, F32), (D, F32), (D, BF16), (DFF, BF16), (DFF, BF16)], [], vmem_mb=48)


def _tail_fwd2(sact, h1, p, tgt, w2, gpff, wg, bg, wpe):
    T = h1.shape[0]

    def body(t, r, o, a, s):
        s_ref, h1_ref, p_ref, t_ref = t
        w2_r, gpff_r, wg_r, bg_r, wpe_r = r
        ff_o, h2b_o, de_o, dpre_o, dh2_o = o
        loss_a, dbg_a = a
        ff = _dot(s_ref[...], w2_r[...])
        ff_o[...] = ff
        h2 = h1_ref[...] + ff * _rs(ff, D) * gpff_r[...]
        h2b = h2.astype(BF16)
        h2b_o[...] = h2b
        gate = 1.0 / (1.0 + jnp.exp(-(_dot(h2b, wg_r[...]) + bg_r[...])))
        pb = p_ref[...].astype(BF16)
        e = jnp.concatenate([_dot(pb, wpe_r[k]) for k in range(4)], axis=1)
        diff = h2 + gate * e - t_ref[...]
        loss_a[...] += jnp.sum(diff * diff, axis=0, keepdims=True)
        dh3 = diff * (1.0 / D)
        de_o[...] = (dh3 * gate).astype(BF16)
        dpre = dh3 * e * gate * (1.0 - gate)
        dbg_a[...] += jnp.sum(dpre, axis=0, keepdims=True)
        dpb = dpre.astype(BF16)
        dpre_o[...] = dpb
        dh2_o[...] = dh3 + _dot_nt(dpb, wg_r[...])

    return _row_call(
        "tail_fwd2", body, T, TM, [sact, h1, p, tgt], [w2, gpff, wg, bg, wpe],
        [(D, F32), (D, BF16), (D, BF16), (D, BF16), (D, F32)], [(1, D), (1, D)], vmem_mb=48)


def _tail_bwd(dh2, ff, rr, h1, ov, yatt, w2, w1, wout, gpff, gpf, gpm, gatt, gsel, expand):
    T = dh2.shape[0]

    def body(t, r, o, a, s):
        dh2_ref, ff_ref, rr_ref, h1_ref, o_ref, ya_ref = t
        w2_r, w1_r, wout_r, gpff_r, gpf_r, gpm_r, gatt_r, gsel_r, ex_r = r
        dff_o, dr_o, do_o, do8_o, dlt_o, dysg_o, dh1_o = o
        dgpff_a, dgpf_a, dgpm_a, dgatt_a = a
        dh2v = dh2_ref[...]
        ffv = ff_ref[...]
        dff, dg = _rms_bwd(dh2v, ffv, _rs(ffv, D), gpff_r[...], D)
        dgpff_a[...] += dg
        dffb = dff.astype(BF16)
        dff_o[...] = dffb
        drb = (_dot_nt(dffb, w2_r[...]) * (2.0 * rr_ref[...].astype(F32))).astype(BF16)
        dr_o[...] = drb
        dc2 = _dot_nt(drb[:, 0:D], w1_r[0])
        for k in range(1, 4):
            dc2 = dc2 + _dot_nt(drb[:, k * D:(k + 1) * D], w1_r[k])
        h1v = h1_ref[...]
        d1, dg = _rms_bwd(dc2, h1v, _rs(h1v, D), gpf_r[...], D)
        dgpf_a[...] += dg
        dh1 = dh2v + d1
        dh1_o[...] = dh1
        ovv = o_ref[...]
        dov, dg = _rms_bwd(dh1, ovv, _rs(ovv, D), gpm_r[...], D)
        dgpm_a[...] += dg
        dob = dov.astype(BF16)
        do_o[...] = dob
        dysg_o[...] = _dot_nt(dob, wout_r[AW:, :])
        dyan = _dot_nt(dob, wout_r[0:AW, :])
        ya = ya_ref[...]
        dya, dg = _rms_bwd(dyan, ya, _rs(ya, AW), gatt_r[...], AW)
        dgatt_a[...] += dg
        do8_o[...] = _dot(dya.astype(BF16), ex_r[...]).astype(BF16)
        dlt_o[...] = _dot01_r(dya * ya, gsel_r[...])

    return _row_call(
        "tail_bwd", body, T, TM, [dh2, ff, rr, h1, ov, yatt],
        [w2, w1, wout, gpff, gpf, gpm, gatt, gsel, expand],
        [(D, BF16), (DFF, BF16), (D, BF16), (NH * 128, BF16), (128, F32), (AW, F32), (D, F32)],
        [(1, D), (1, D), (1, D), (1, AW)], vmem_mb=56)


def _pre_attn_bwd(x, dh1, dq8, dk8, dv8, flog, zuv, dysg, gpre, win, lng, lnb, wm, wmt, bsg, gsg, gsel, shrink, pick64, pick67):
    T = x.shape[0]
    tm = TM

    def body(t, r, o, a, s):
        x_ref, dh1_ref, dq_ref, dk_ref, dv_ref, fl_ref, zuv_ref, dys_ref = t
        gpre_r, win_r, lng_r, lnb_r, wm_r, wmt_r, bsg_r, gsg_r, gsel_r, sh_r, p64_r, p67_r = r
        dx_o, dz_o = o
        dgpre_a, dfb_a, dgsg_a, dlng_a, dlnb_a, dws_a, dbs_a, dsb_a = a
        carry_ref, mixed_ref, dvv_ref = s
        dq8v = dq_ref[...]
        dk8v = dk_ref[...]
        dcv = _dot01_r(dq8v, p67_r[...]) + _dot01_r(dk8v, p64_r[...])
        rr = lax.broadcasted_iota(jnp.int32, (tm, tm), 0)
        cc = lax.broadcasted_iota(jnp.int32, (tm, tm), 1)
        triu = (cc >= rr).astype(BF16)
        dlogf = _dot01(triu, dcv) + carry_ref[...]
        carry_ref[...] = dlogf[0:1, :]
        dzf = dlogf * (1.0 / (1.0 + jnp.exp(fl_ref[...])))
        dfb_a[...] += jnp.sum(dzf, axis=0, keepdims=True)
        dz_o[:, 5 * AW:] = dzf.astype(BF16)
        zu = zuv_ref[:, 0:AW]
        zv = zuv_ref[:, AW:]
        gu, tu, tv, xhat, rstd, vvb, mixed = _sg_forward(
            zu, zv, wm_r, bsg_r[...], lng_r[...], lnb_r[...], mixed_ref, tm)
        ysg = gu * mixed
        dysg_n = dys_ref[...]
        dys, dg = _rms_bwd(dysg_n, ysg, _rs(ysg, AW), gsg_r[...], AW)
        dgsg_a[...] += dg
        dgu = dys * mixed
        dmix = dys * gu
        dmb = dmix.astype(BF16)
        lane = lax.broadcasted_iota(jnp.int32, (CH, 128), 1)
        lo = lane < DH
        for c in range(tm // CH):
            rows = slice(c * CH, (c + 1) * CH)
            dbs_a[...] += dmix[rows, :]
            for j in range(4):
                cols = slice(j * 128, (j + 1) * 128)
                dmblk = dmb[rows, cols]
                vblk = vvb[rows, cols]
                d0 = _dot(wmt_r[2 * j], dmblk)
                d1 = _dot(wmt_r[2 * j + 1], dmblk)
                dvv_ref[rows, cols] = jnp.where(lo, d0, d1)
                dws_a[2 * j] += _dot_nt(jnp.where(lo, dmblk, jnp.zeros_like(dmblk)), vblk)
                dws_a[2 * j + 1] += _dot_nt(jnp.where(lo, jnp.zeros_like(dmblk), dmblk), vblk)
        dvv = dvv_ref[...]
        dlng_a[...] += jnp.sum(dvv * xhat, axis=0, keepdims=True)
        dlnb_a[...] += jnp.sum(dvv, axis=0, keepdims=True)
        dxh = dvv * lng_r[...]
        dvg = rstd * (dxh - jnp.sum(dxh, axis=-1, keepdims=True) * (1.0 / AW)
                      - xhat * (jnp.sum(dxh * xhat, axis=-1, keepdims=True) * (1.0 / AW)))
        dz_o[:, 3 * AW:4 * AW] = (dgu * _gelu_grad(zu, tu)).astype(BF16)
        dz_o[:, 4 * AW:5 * AW] = (dvg * _gelu_grad(zv, tv)).astype(BF16)
        dz_o[:, 0:AW] = _dot((dq8v * (DH ** -0.5)).astype(BF16), sh_r[...]).astype(BF16)
        dz_o[:, AW:2 * AW] = _dot(dk8v.astype(BF16), sh_r[...]).astype(BF16)
        dz_o[:, 2 * AW:3 * AW] = _dot(dv_ref[...].astype(BF16), sh_r[...]).astype(BF16)
        da = _dot_nt(dz_o[...], win_r[...])
        xv = x_ref[...]
        dxa, dg = _rms_bwd(da, xv, _rs(xv, D), gpre_r[...], D)
        dgpre_a[...] += dg
        dx_o[...] = dh1_ref[...] + dxa

        @pl.when(pl.program_id(0) == T // tm - 1)
        def _():
            dsb_a[...] = _dot01_r(dbs_a[...], gsel_r[...])

    outs = _row_call(
        "pre_attn_bwd", body, T, tm, [x, dh1, dq8, dk8, dv8, flog, zuv, dysg],
        [gpre, win, lng, lnb, wm, wmt, bsg, gsg, gsel, shrink, pick64, pick67],
        [(D, F32), (ZW, BF16)],
        [(1, D), (1, 128), (1, AW), (1, AW), (1, AW), (8, CH, CH), (CH, AW), (CH, 128)],
        scratch=[(1, 128), (tm, AW), (tm, AW)], reverse=True, vmem_mb=48)
    return outs


def _matmul_tn(name, a, b, tn=512, tt=2048, shards=1, after=None):
    T, K = a.shape
    N = b.shape[1]
    tk = min(K, 1024)
    tn = min(tn, N // shards)
    tt = min(tt, T)
    nj = N // shards // tn

    def body(a_ref, b_ref, *rest):
        o_ref = rest[-1]

        @pl.when(pl.program_id(2) == 0)
        def _():
            o_ref[...] = jnp.zeros(o_ref.shape, F32)

        o_ref[...] += _dot_tn(a_ref[...].astype(BF16), b_ref[...].astype(BF16))

    ordered = after is not None

    if shards == 1:
        out_shape = jax.ShapeDtypeStruct((K, N), F32)
        out_spec = pl.BlockSpec((tk, tn), lambda i, j, t: (i, j))
    else:
        out_shape = jax.ShapeDtypeStruct((shards, K, N // shards), F32)
        out_spec = pl.BlockSpec((None, tk, tn), lambda i, j, t: (j // nj, i, j % nj))
    return pl.pallas_call(
        body, name=name, grid=(K // tk, N // tn, T // tt),
        in_specs=[pl.BlockSpec((tt, tk), lambda i, j, t: (t, i)),
                  pl.BlockSpec((tt, tn), lambda i, j, t: (t, j))] + [pl.BlockSpec(memory_space=pl.ANY)] * ordered,
        out_specs=out_spec, out_shape=out_shape,
        compiler_params=_params(40, ("arbitrary", "arbitrary", "arbitrary")),
    )(a, b, *([after] * ordered))


def _me():
    return lax.axis_index("x"), lax.axis_index("y"), lax.axis_index("c")


HBM_SPEC = pl.BlockSpec(memory_space=pltpu.HBM)


def _gather_weights(mine):
    half = mine.shape[0] // 2

    def body(mine_ref, out_ref, ici_send, ici_recv, d2d_send, d2d_recv):
        x, y, c = _me()
        k_me = 2 * x + y
        chips = [(1 - x, y), (x, 1 - y), (1 - x, 1 - y)]
        my_rows = pl.ds(pl.multiple_of(c * half, 16), half)
        sib_rows = pl.ds(pl.multiple_of((1 - c) * half, 16), half)

        def over_ici(j, k, to):
            src = mine_ref.at[my_rows] if k is None else out_ref.at[k, my_rows]
            return pltpu.make_async_remote_copy(
                src_ref=src, dst_ref=out_ref.at[k_me if k is None else k, my_rows], send_sem=ici_send.at[j],
                recv_sem=ici_recv.at[j], device_id=to, device_id_type=MESH)

        def over_d2d(j, k, rows):
            return pltpu.make_async_remote_copy(
                src_ref=out_ref.at[k, rows], dst_ref=out_ref.at[k, rows], send_sem=d2d_send.at[j],
                recv_sem=d2d_recv.at[j], device_id=(x, y, 1 - c), device_id_type=MESH)

        first = [over_ici(j, None, (cx, cy, c)) for j, (cx, cy) in enumerate(chips)]
        for cp in first:
            cp.start()
        passed = [over_d2d(j, 2 * cx + cy, my_rows) for j, (cx, cy) in enumerate(chips)]
        for j, (cx, cy) in enumerate(chips):
            over_ici(j, 2 * cx + cy, (cx, cy, c)).wait_recv()
            passed[j].start()
        for j, (cx, cy) in enumerate(chips):
            over_d2d(j, 2 * cx + cy, sib_rows).wait_recv()
        for cp in first + passed:
            cp.wait_send()

    return pl.pallas_call(
        body, name="gather_weights", in_specs=[HBM_SPEC], out_specs=HBM_SPEC,
        out_shape=jax.ShapeDtypeStruct((4,) + mine.shape, mine.dtype),
        scratch_shapes=[pltpu.SemaphoreType.DMA((3,)), pltpu.SemaphoreType.DMA((3,)), pltpu.SemaphoreType.DMA((3,)),
                        pltpu.SemaphoreType.DMA((3,))],
    )(mine)


SEM_SPEC = pl.BlockSpec(memory_space=pltpu.SEMAPHORE)
EFFECT = pltpu.SideEffectType.DATAFLOW_SIDE_EFFECTING


def _gather_late_start(mine):
    def body(mine_ref, land_ref, send_sems, recv_sems, mine_thru, land_thru, token):
        x, y, c = _me()
        k_me = 2 * x + y
        for j, (cx, cy) in enumerate([(1 - x, y), (x, 1 - y), (1 - x, 1 - y)]):
            pltpu.make_async_remote_copy(
                src_ref=mine_ref, dst_ref=land_ref.at[k_me], send_sem=send_sems.at[j], recv_sem=recv_sems.at[j],
                device_id=(cx, cy, c), device_id_type=MESH).start()
        token[...] = jnp.zeros(token.shape, F32)

    land = lax.empty((4,) + mine.shape, mine.dtype)
    return pl.pallas_call(
        body, name="gather_late_start",
        out_shape=(pltpu.SemaphoreType.DMA((3,)), pltpu.SemaphoreType.DMA((3,)), pltpu.HBM(mine.shape, mine.dtype),
                   pltpu.HBM(land.shape, land.dtype), jax.ShapeDtypeStruct((8, 128), F32)),
        in_specs=(HBM_SPEC, HBM_SPEC),
        out_specs=(SEM_SPEC, SEM_SPEC, HBM_SPEC, HBM_SPEC, pl.BlockSpec(memory_space=pltpu.VMEM)),
        input_output_aliases={0: 2, 1: 3},
        compiler_params=pltpu.CompilerParams(has_side_effects=EFFECT),
    )(pltpu.with_memory_space_constraint(mine, pltpu.HBM), pltpu.with_memory_space_constraint(land, pltpu.HBM))


def _gather_late_wait(send_sems, recv_sems, mine_thru, land_thru, after):
    def body(mine_ref, land_ref, send_sems, recv_sems, after_ref, mine_dead, got_ref):
        x, y, c = _me()
        for j, (cx, cy) in enumerate([(1 - x, y), (x, 1 - y), (1 - x, 1 - y)]):
            cp = pltpu.make_async_remote_copy(
                src_ref=mine_ref, dst_ref=land_ref.at[2 * cx + cy], send_sem=send_sems.at[j],
                recv_sem=recv_sems.at[j], device_id=(cx, cy, c), device_id_type=MESH)
            cp.wait_send()
            cp.wait_recv()

    return pl.pallas_call(
        body, name="gather_late_wait",
        out_shape=(pltpu.HBM(mine_thru.shape, mine_thru.dtype), pltpu.HBM(land_thru.shape, land_thru.dtype)),
        in_specs=(HBM_SPEC, HBM_SPEC, SEM_SPEC, SEM_SPEC, pl.BlockSpec(memory_space=pl.ANY)),
        out_specs=(HBM_SPEC, HBM_SPEC), input_output_aliases={0: 0, 1: 1},
        compiler_params=pltpu.CompilerParams(has_side_effects=EFFECT),
    )(mine_thru, land_thru, send_sems, recv_sems, after)[1]


def _swap_halves(gs, tag):
    n = len(gs)

    def body(*refs):
        g_refs, got_refs, send_sems, recv_sems = refs[:n], refs[n:2 * n], refs[2 * n], refs[2 * n + 1]
        x, y, c = _me()
        cps = []
        for i, (g_ref, got_ref) in enumerate(zip(g_refs, got_refs)):
            half = g_ref.shape[1] // 2
            theirs = pl.multiple_of((1 - c) * half, 16)
            cps.append(pltpu.make_async_remote_copy(
                src_ref=g_ref.at[:, pl.ds(theirs, half), :], dst_ref=got_ref, send_sem=send_sems.at[i],
                recv_sem=recv_sems.at[i], device_id=(x, y, 1 - c), device_id_type=MESH))
        for cp in cps:
            cp.start()
        for cp in cps:
            cp.wait()

    return pl.pallas_call(
        body, name="swap_halves_" + tag, in_specs=[HBM_SPEC] * n, out_specs=[HBM_SPEC] * n,
        out_shape=[jax.ShapeDtypeStruct((4, g.shape[1] // 2, g.shape[2]), F32) for g in gs],
        scratch_shapes=[pltpu.SemaphoreType.DMA((n,)), pltpu.SemaphoreType.DMA((n,))],
    )(*gs)


def _swap_start(gs, tag):
    n = len(gs)

    def body(*refs):
        g_refs, land_refs, send_sems, recv_sems = refs[:n], refs[n:2 * n], refs[2 * n], refs[2 * n + 1]
        x, y, c = _me()
        for i, (g_ref, land_ref) in enumerate(zip(g_refs, land_refs)):
            half = g_ref.shape[1] // 2
            theirs = pl.multiple_of((1 - c) * half, 16)
            pltpu.make_async_remote_copy(
                src_ref=g_ref.at[:, pl.ds(theirs, half), :], dst_ref=land_ref, send_sem=send_sems.at[i],
                recv_sem=recv_sems.at[i], device_id=(x, y, 1 - c), device_id_type=MESH).start()

    lands = [lax.empty((4, g.shape[1] // 2, g.shape[2]), F32) for g in gs]
    hbm = lambda t: pltpu.HBM(t.shape, t.dtype)
    res = pl.pallas_call(
        body, name="swap_start_" + tag,
        out_shape=(pltpu.SemaphoreType.DMA((n,)), pltpu.SemaphoreType.DMA((n,)), *[hbm(t) for t in gs],
                   *[hbm(t) for t in lands]),
        in_specs=(HBM_SPEC,) * (2 * n), out_specs=(SEM_SPEC, SEM_SPEC) + (HBM_SPEC,) * (2 * n),
        input_output_aliases={i: 2 + i for i in range(2 * n)},
        compiler_params=pltpu.CompilerParams(has_side_effects=EFFECT),
    )(*[pltpu.with_memory_space_constraint(t, pltpu.HBM) for t in list(gs) + lands])
    return res[0], res[1], res[2:2 + n], res[2 + n:2 + 2 * n]


def _swap_wait(send_sems, recv_sems, g_thru, land_thru, after, tag):
    n = len(g_thru)

    def body(*refs):
        g_refs, land_refs, send_sems, recv_sems = refs[:n], refs[n:2 * n], refs[2 * n], refs[2 * n + 1]
        x, y, c = _me()
        for i, (g_ref, land_ref) in enumerate(zip(g_refs, land_refs)):
            half = g_ref.shape[1] // 2
            theirs = pl.multiple_of((1 - c) * half, 16)
            pltpu.make_async_remote_copy(
                src_ref=g_ref.at[:, pl.ds(theirs, half), :], dst_ref=land_ref, send_sem=send_sems.at[i],
                recv_sem=recv_sems.at[i], device_id=(x, y, 1 - c), device_id_type=MESH).wait()

    hbm = lambda t: pltpu.HBM(t.shape, t.dtype)
    res = pl.pallas_call(
        body, name="swap_wait_" + tag, out_shape=tuple(hbm(t) for t in list(g_thru) + list(land_thru)),
        in_specs=(HBM_SPEC,) * (2 * n) + (SEM_SPEC, SEM_SPEC, pl.BlockSpec(memory_space=pl.ANY)),
        out_specs=(HBM_SPEC,) * (2 * n), input_output_aliases={i: i for i in range(2 * n)},
        compiler_params=pltpu.CompilerParams(has_side_effects=EFFECT),
    )(*g_thru, *land_thru, send_sems, recv_sems, after)
    return res[:n], res[n:]


def _pair_sum(name, c1, g, got):
    half, cols = got.shape[1], got.shape[2]

    def body(c_ref, a_ref, b_ref, o_ref):
        o_ref[...] = (a_ref[...] + b_ref[...]).astype(BF16)

    return pl.pallas_call(
        body, name="pair_sum_" + name,
        grid_spec=pltpu.PrefetchScalarGridSpec(
            num_scalar_prefetch=1, grid=(4,),
            in_specs=[pl.BlockSpec((1, half, cols), lambda k, c_ref: (k, c_ref[0], 0)),
                      pl.BlockSpec((1, half, cols), lambda k, c_ref: (k, 0, 0))],
            out_specs=pl.BlockSpec((1, half, cols), lambda k, c_ref: (k, 0, 0))),
        out_shape=jax.ShapeDtypeStruct(got.shape, BF16), compiler_params=_params(32),
    )(c1, g, got)


def _exchange_start(pss, tag):
    n = len(pss)

    def body(*refs):
        ps_refs, land_refs = refs[:n], refs[n:2 * n]
        send_sems, recv_sems = refs[2 * n], refs[2 * n + 1]
        token = refs[4 * n + 2]
        x, y, c = _me()
        k_me = 2 * x + y
        for i, (ps_ref, land_ref) in enumerate(zip(ps_refs, land_refs)):
            for j, (cx, cy) in enumerate([(1 - x, y), (x, 1 - y), (1 - x, 1 - y)]):
                pltpu.make_async_remote_copy(
                    src_ref=ps_ref.at[2 * cx + cy], dst_ref=land_ref.at[k_me], send_sem=send_sems.at[3 * i + j],
                    recv_sem=recv_sems.at[3 * i + j], device_id=(cx, cy, c), device_id_type=MESH).start()
        token[...] = jnp.zeros(token.shape, F32)

    lands = [lax.empty(ps.shape, ps.dtype) for ps in pss]
    hbm = lambda t: pltpu.HBM(t.shape, t.dtype)
    res = pl.pallas_call(
        body, name="exchange_start_" + tag,
        out_shape=(pltpu.SemaphoreType.DMA((3 * n,)), pltpu.SemaphoreType.DMA((3 * n,)), *[hbm(t) for t in pss],
                   *[hbm(t) for t in lands], jax.ShapeDtypeStruct((8, 128), F32)),
        in_specs=(HBM_SPEC,) * (2 * n),
        out_specs=(SEM_SPEC, SEM_SPEC) + (HBM_SPEC,) * (2 * n) + (pl.BlockSpec(memory_space=pltpu.VMEM),),
        input_output_aliases={i: 2 + i for i in range(2 * n)},
        compiler_params=pltpu.CompilerParams(has_side_effects=EFFECT),
    )(*[pltpu.with_memory_space_constraint(t, pltpu.HBM) for t in list(pss) + lands])
    return res[0], res[1], res[2:2 + n], res[2 + n:2 + 2 * n], res[2 + 2 * n]


def _exchange_wait(send_sems, recv_sems, ps_thru, land_thru, after, tag):
    n = len(ps_thru)

    def body(*refs):
        ps_refs, land_refs = refs[:n], refs[n:2 * n]
        send_sems, recv_sems = refs[2 * n], refs[2 * n + 1]
        x, y, c = _me()
        k_me = 2 * x + y
        for i, (ps_ref, land_ref) in enumerate(zip(ps_refs, land_refs)):
            for j, (cx, cy) in enumerate([(1 - x, y), (x, 1 - y), (1 - x, 1 - y)]):
                cp = pltpu.make_async_remote_copy(
                    src_ref=ps_ref.at[k_me], dst_ref=land_ref.at[2 * cx + cy], send_sem=send_sems.at[3 * i + j],
                    recv_sem=recv_sems.at[3 * i + j], device_id=(cx, cy, c), device_id_type=MESH)
                cp.wait_send()
                cp.wait_recv()

    hbm = lambda t: pltpu.HBM(t.shape, t.dtype)
    res = pl.pallas_call(
        body, name="exchange_wait_" + tag,
        out_shape=tuple(hbm(t) for t in list(ps_thru) + list(land_thru)),
        in_specs=(HBM_SPEC,) * (2 * n) + (SEM_SPEC, SEM_SPEC, pl.BlockSpec(memory_space=pl.ANY)),
        out_specs=(HBM_SPEC,) * (2 * n), input_output_aliases={i: i for i in range(2 * n)},
        compiler_params=pltpu.CompilerParams(has_side_effects=EFFECT),
    )(*ps_thru, *land_thru, send_sems, recv_sems, after)
    return res[:n], res[n:]


def _adamw(w, g, m, v):
    m = B1 * m + (1.0 - B1) * g
    v = B2 * v + (1.0 - B2) * (g * g)
    delta = -LR * ((m / BC1) / (jnp.sqrt(v / BC2) + AEPS) + WD * w)
    return delta, m, v


def _reduce_chips(name, parts):
    half, cols = parts.shape[1], parts.shape[2]

    def body(p_ref, o_ref):
        f = lambda k: p_ref[k].astype(F32)
        o_ref[...] = ((f(0) + f(1)) + f(2)) + f(3)

    return pl.pallas_call(
        body, name="reduce_chips_" + name, grid=(1,),
        in_specs=[pl.BlockSpec((4, half, cols), lambda i: (0, 0, 0))],
        out_specs=pl.BlockSpec((half, cols), lambda i: (0, 0)),
        out_shape=jax.ShapeDtypeStruct((half, cols), F32), compiler_params=_params(32),
    )(parts)


def _share_grad(ghs, tag):
    n = len(ghs)

    def body(*refs):
        g_refs, got_refs, send_sems, recv_sems = refs[:n], refs[n:2 * n], refs[2 * n], refs[2 * n + 1]
        x, y, c = _me()
        cps = [pltpu.make_async_remote_copy(
            src_ref=g_ref, dst_ref=got_ref, send_sem=send_sems.at[i], recv_sem=recv_sems.at[i],
            device_id=(x, y, 1 - c), device_id_type=MESH) for i, (g_ref, got_ref) in enumerate(zip(g_refs, got_refs))]
        for cp in cps:
            cp.start()
        for cp in cps:
            cp.wait()

    return pl.pallas_call(
        body, name="share_grad_" + tag, in_specs=[HBM_SPEC] * n, out_specs=[HBM_SPEC] * n,
        out_shape=[jax.ShapeDtypeStruct(g.shape, F32) for g in ghs],
        scratch_shapes=[pltpu.SemaphoreType.DMA((n,)), pltpu.SemaphoreType.DMA((n,))],
    )(*ghs)


def _update(name, c1, gh, got, w, m, v):
    half, cols = gh.shape
    wcols = w.shape[1]

    def body(c_ref, gh_ref, got_ref, w_ref, m_ref, v_ref, g_o, d_o, m_o, v_o):
        g = jnp.where(pl.program_id(0) == c_ref[0], gh_ref[:, :wcols], got_ref[:, :wcols])
        delta, mn, vn = _adamw(w_ref[...], g, m_ref[...], v_ref[...])
        g_o[...] = g
        d_o[...] = delta
        m_o[...] = mn
        v_o[...] = vn

    same = pl.BlockSpec((half, cols), lambda h, c_ref: (0, 0))
    rows = pl.BlockSpec((half, wcols), lambda h, c_ref: (h, 0))
    return pl.pallas_call(
        body, name="update_" + name,
        grid_spec=pltpu.PrefetchScalarGridSpec(
            num_scalar_prefetch=1, grid=(2,), in_specs=[same, same, rows, rows, rows],
            out_specs=[rows, rows, rows, rows]),
        out_shape=[jax.ShapeDtypeStruct(w.shape, F32)] * 4, compiler_params=_params(40),
    )(c1, gh, got, w, m, v)


SMALL_NAMES = ("sg_w",) + VEC_NAMES
VEC_ROWS = 24
VEC_ROW = {"f_bias": 0, "sg_ln_g": 1, "sg_ln_b": 2, "att_out_g": 3, "sg_out_g": 4, "pre_mix_g": 5,
           "post_mix_g": 6, "pre_ffn_g": 7, "sg_b": 8, "post_ffn_g": 16, "ple_gate_b": 17}
LOSS_VEC_ROW = 18


def _small_pack(g, loss_l):
    n = len(SMALL_NAMES)

    def body(*refs):
        g_r = dict(zip(SMALL_NAMES, refs[0:n]))
        loss_r, vec_o, w_o = refs[n:]
        vec_o[...] = jnp.zeros((VEC_ROWS, 1024), F32)
        for name in VEC_NAMES:
            val = g_r[name][...]
            vec_o[pl.ds(VEC_ROW[name], val.shape[0]), pl.ds(0, val.shape[1])] = val
        vec_o[pl.ds(LOSS_VEC_ROW, 1), :] = loss_r[...] * (0.5 / D)
        rr = lax.broadcasted_iota(jnp.int32, (CH, CH), 0)
        cc = lax.broadcasted_iota(jnp.int32, (CH, CH), 1)
        w_o[...] = jnp.where((cc <= rr)[None], g_r["sg_w"][...], 0.0)

    vm = pl.BlockSpec(memory_space=pltpu.VMEM)
    args = [g[k] for k in SMALL_NAMES] + [loss_l]
    return pl.pallas_call(
        body, name="small_pack", in_specs=[vm] * len(args), out_specs=[vm, vm],
        out_shape=[jax.ShapeDtypeStruct((VEC_ROWS, 1024), F32), jax.ShapeDtypeStruct((8, CH, CH), F32)],
    )(*args)


def _small_peers(x, y, c):
    rels = [(rx, ry, rc) for rx in (0, 1) for ry in (0, 1) for rc in (0, 1)][1:]
    return [((x + rx) % 2, (y + ry) % 2, (c + rc) % 2) for rx, ry, rc in rels]


def _small_start(vec, w8):
    def body(vec_ref, w_ref, lv_ref, lw_ref, send_sems, recv_sems, vec_thru, w_thru, lv_thru, lw_thru, token):
        x, y, c = _me()
        me = 4 * x + 2 * y + c
        for j, to in enumerate(_small_peers(x, y, c)):
            for i, (src, land) in enumerate(((vec_ref, lv_ref), (w_ref, lw_ref))):
                pltpu.make_async_remote_copy(
                    src_ref=src, dst_ref=land.at[me], send_sem=send_sems.at[2 * j + i],
                    recv_sem=recv_sems.at[2 * j + i], device_id=to, device_id_type=MESH).start()
        token[...] = jnp.zeros(token.shape, F32)

    ops = [vec, w8, lax.empty((8,) + vec.shape, F32), lax.empty((8,) + w8.shape, F32)]
    hbm = lambda t: pltpu.HBM(t.shape, t.dtype)
    res = pl.pallas_call(
        body, name="small_start",
        out_shape=(pltpu.SemaphoreType.DMA((14,)), pltpu.SemaphoreType.DMA((14,)), *[hbm(t) for t in ops],
                   jax.ShapeDtypeStruct((8, 128), F32)),
        in_specs=(HBM_SPEC,) * 4,
        out_specs=(SEM_SPEC, SEM_SPEC) + (HBM_SPEC,) * 4 + (pl.BlockSpec(memory_space=pltpu.VMEM),),
        input_output_aliases={i: 2 + i for i in range(4)},
        compiler_params=pltpu.CompilerParams(has_side_effects=EFFECT),
    )(*[pltpu.with_memory_space_constraint(t, pltpu.HBM) for t in ops])
    return res[0], res[1], res[2:6], res[6]


def _small_wait(send_sems, recv_sems, thru, after):
    def body(vec_ref, w_ref, lv_ref, lw_ref, send_sems, recv_sems, after_ref, vec_o, w_o, lv_o, lw_o):
        x, y, c = _me()
        for j, (px, py, pc) in enumerate(_small_peers(x, y, c)):
            for i, (src, land) in enumerate(((vec_ref, lv_ref), (w_ref, lw_ref))):
                cp = pltpu.make_async_remote_copy(
                    src_ref=src, dst_ref=land.at[4 * px + 2 * py + pc], send_sem=send_sems.at[2 * j + i],
                    recv_sem=recv_sems.at[2 * j + i], device_id=(px, py, pc), device_id_type=MESH)
                cp.wait_send()
                cp.wait_recv()

    hbm = lambda t: pltpu.HBM(t.shape, t.dtype)
    return pl.pallas_call(
        body, name="small_wait", out_shape=tuple(hbm(t) for t in thru),
        in_specs=(HBM_SPEC,) * 4 + (SEM_SPEC, SEM_SPEC, pl.BlockSpec(memory_space=pl.ANY)),
        out_specs=(HBM_SPEC,) * 4, input_output_aliases={i: i for i in range(4)},
        compiler_params=pltpu.CompilerParams(has_side_effects=EFFECT),
    )(*thru, send_sems, recv_sems, after)


def _small_update(all_v, all_w, w, m, v):
    n = len(SMALL_NAMES)

    def body(*refs):
        allv_r, allw_r = refs[0], refs[1]
        tot_v = allv_r[0]
        tot_w = allw_r[0]
        for d in range(1, 8):
            tot_v = tot_v + allv_r[d]
            tot_w = tot_w + allw_r[d]
        w_r = dict(zip(SMALL_NAMES, refs[2:2 + n]))
        m_r = dict(zip(SMALL_NAMES, refs[2 + n:2 + 2 * n]))
        v_r = dict(zip(SMALL_NAMES, refs[2 + 2 * n:2 + 3 * n]))
        loss_o = refs[2 + 3 * n]
        outs = refs[3 + 3 * n:]
        loss_o[...] = jnp.sum(tot_v[LOSS_VEC_ROW:LOSS_VEC_ROW + 1, :], axis=-1, keepdims=True) + jnp.zeros((1, 128), F32)
        for i, name in enumerate(SMALL_NAMES):
            if name == "sg_w":
                gt = tot_w
            else:
                rows, width = w_r[name].shape
                gt = tot_v[VEC_ROW[name]:VEC_ROW[name] + rows, 0:width]
            delta, mn, vn = _adamw(w_r[name][...], gt, m_r[name][...], v_r[name][...])
            outs[4 * i][...] = gt
            outs[4 * i + 1][...] = delta
            outs[4 * i + 2][...] = mn
            outs[4 * i + 3][...] = vn

    vm = pl.BlockSpec(memory_space=pltpu.VMEM)
    args = [all_v, all_w] + [d[k] for d in (w, m, v) for k in SMALL_NAMES]
    out_shape = [jax.ShapeDtypeStruct((1, 128), F32)]
    out_shape += [jax.ShapeDtypeStruct(w[k].shape, F32) for k in SMALL_NAMES for _ in range(4)]
    res = pl.pallas_call(
        body, name="small_update", in_specs=[vm] * len(args), out_specs=[vm] * len(out_shape), out_shape=out_shape,
        compiler_params=pltpu.CompilerParams(vmem_limit_bytes=32 * 1024 * 1024),
    )(*args)
    return res[0], {k: res[1 + 4 * i:5 + 4 * i] for i, k in enumerate(SMALL_NAMES)}


def _win_kernel_order(gathered):
    w_in = jnp.concatenate([gathered[k].reshape(D, 768)[:, :642] for k in range(4)], axis=1)
    return jnp.concatenate([w_in[:, :3 * AW], w_in[:, 3 * AW + NH:], w_in[:, 3 * AW:3 * AW + NH],
                            jnp.zeros((D, 128 - NH), w_in.dtype)], axis=1)


LATE_ROWS = 256 + 1024 + 1024 + 64 + 256


def _pack_late(w_out, w1, w2, plew, wg):
    return jnp.concatenate([w_out, w1, w2, plew.reshape(64, 1024), wg], axis=0)


def _unpack_late(gathered):
    return (gathered[:, 0:256].reshape(D, D), gathered[:, 256:1280], gathered[:, 1280:2304].reshape(DFF, D),
            gathered[:, 2304:2368].reshape(4, 256, 256), gathered[:, 2368:2624].reshape(D, D))


def _local_step(x, p, tgt, win_k, late_weights, token, on_ff_grads, on_tail_grads, on_small_grads, small):
    T = x.shape[0]
    row = lambda n: small[n].reshape(1, -1)
    fbias = jnp.pad(row("f_bias"), ((0, 0), (0, 128 - NH))) + token[0:1, :]
    wm = _masked_sg_w(small["sg_w"].reshape(8, CH, CH))
    wmb = wm.astype(BF16)
    wmt = jnp.swapaxes(wm, 1, 2).astype(BF16)
    bsg = jnp.repeat(small["sg_b"].reshape(8, CH).T, DH, axis=1)
    ln_g, ln_b, gsg, gatt = row("sg_ln_g"), row("sg_ln_b"), row("sg_out_g"), row("att_out_g")
    gpre, gpm, gpf, gpff, bg = row("pre_mix_g"), row("post_mix_g"), row("pre_ffn_g"), row("post_ffn_g"), row("ple_gate_b")
    gsel = (jnp.arange(AW)[:, None] // DH == jnp.arange(128)[None, :]).astype(BF16)

    expand, shrink, pieces, qconst, one64, one67, pick64, pick67 = _head_consts()
    a, flog, zuv, ysgn, q8, k8, v8 = _pre_attn_fwd(
        x, gpre, win_k, fbias, ln_g, ln_b, wmb, bsg, gsg, expand, pieces, qconst, one67, one64)

    slabs = lambda t: jnp.swapaxes(t.reshape(T // TQ, TQ, NH * 128), 1, 2)
    qt8 = slabs(q8)
    lanes = jnp.arange(128)
    sel = jnp.stack([((lanes[:, None] == lanes[None, :] - DH * j) & (lanes[:, None] < DH)).astype(BF16)
                     for j in (0, 1)])

    yatt, lse = _flash_fwd(qt8, k8, slabs(v8), sel)
    wout, w1, w2, plew, wg = late_weights(lse)
    y, ov, h1, c2, sact, rr = _tail_fwd1(x, yatt, ysgn, gatt, wout, gpm, gpf, w1)
    ff, h2b, de, dpre, dh2, loss_l, dbg = _tail_fwd2(sact, h1, p, tgt, w2, gpff, wg, bg, plew)
    dff, dr, do, do8, dlt, dysg, dh1, dgpff, dgpf, dgpm, dgatt = _tail_bwd(
        dh2, ff, rr, h1, ov, yatt, w2, w1, wout, gpff, gpf, gpm, gatt, gsel, expand)
    dw1 = _matmul_tn("grad_w_ff1", c2, dr, shards=4)
    dw2 = _matmul_tn("grad_w_ff2", sact, dff)
    ff_sent = on_ff_grads(dw1, dw2)
    dwout = _matmul_tn("grad_w_out", y, do, after=ff_sent)
    dwg = _matmul_tn("grad_ple_gate_w", h2b, dpre, after=dwout)
    dplew = _matmul_tn("grad_ple_w", p, de, tn=256, shards=4, after=dwg)
    tail_token = on_tail_grads((dwout, dplew, dwg))
    dlt4 = jnp.pad(dlt[:, :NH].T.reshape(4, 2, T), ((0, 0), (0, 6), (0, 0))) + tail_token[0, 0]
    dqt, dk8, dv8 = _flash_bwd(q8, qt8, k8, v8, do8, slabs(do8), lse, dlt4)
    dx, dz, dgpre, dfb, dgsg, dlng, dlnb, dws, _, dsbt = _pre_attn_bwd(
        x, dh1, jnp.swapaxes(dqt, 1, 2).reshape(T, NH * 128), dk8, dv8, flog, zuv, dysg,
        gpre, win_k, ln_g, ln_b, wmb, wmt, bsg, gsg, gsel, shrink, pick64, pick67)


    dsb = dsbt[:, :8].T
    gsmall = {"sg_w": dws, "f_bias": dfb, "sg_ln_g": dlng, "sg_ln_b": dlnb, "sg_b": dsb,
              "att_out_g": dgatt, "sg_out_g": dgsg, "pre_mix_g": dgpre, "post_mix_g": dgpm, "pre_ffn_g": dgpf,
              "post_ffn_g": dgpff, "ple_gate_b": dbg}
    on_small_grads(gsmall, loss_l)
    dwin_k = _matmul_tn("grad_w_in", a, dz, tn=384)
    return loss_l, dx, dwin_k, gsmall


def kernel(x, p, w_in, f_bias, sg_ln_g, sg_ln_b, sg_w, sg_b, att_out_g, sg_out_g, w_out, pre_mix_g, post_mix_g, pre_ffn_g, post_ffn_g, w_ff1, w_ff2, ple_w, ple_gate_w, ple_gate_b, loss_target, m_w_in, m_f_bias, m_sg_ln_g, m_sg_ln_b, m_sg_w, m_sg_b, m_att_out_g, m_sg_out_g, m_w_out, m_pre_mix_g, m_post_mix_g, m_pre_ffn_g, m_post_ffn_g, m_w_ff1, m_w_ff2, m_ple_w, m_ple_gate_w, m_ple_gate_b, v_w_in, v_f_bias, v_sg_ln_g, v_sg_ln_b, v_sg_w, v_sg_b, v_att_out_g, v_sg_out_g, v_w_out, v_pre_mix_g, v_post_mix_g, v_pre_ffn_g, v_post_ffn_g, v_w_ff1, v_w_ff2, v_ple_w, v_ple_gate_w, v_ple_gate_b):
    c = lax.axis_index("c")
    big = lambda t: (t[0][0], t[1][0], t[2][0], t[3][0], t[4][0], t[5][0])
    w_big = big((w_in, w_out, w_ff1, w_ff2, ple_w, ple_gate_w))
    m_big = big((m_w_in, m_w_out, m_w_ff1, m_w_ff2, m_ple_w, m_ple_gate_w))
    v_big = big((v_w_in, v_w_out, v_w_ff1, v_w_ff2, v_ple_w, v_ple_gate_w))
    small = {"sg_w": sg_w, "f_bias": f_bias, "sg_ln_g": sg_ln_g, "sg_ln_b": sg_ln_b, "sg_b": sg_b,
             "att_out_g": att_out_g, "sg_out_g": sg_out_g, "pre_mix_g": pre_mix_g, "post_mix_g": post_mix_g,
             "pre_ffn_g": pre_ffn_g, "post_ffn_g": post_ffn_g, "ple_gate_b": ple_gate_b}
    m_small = {"sg_w": m_sg_w, "f_bias": m_f_bias, "sg_ln_g": m_sg_ln_g, "sg_ln_b": m_sg_ln_b, "sg_b": m_sg_b,
               "att_out_g": m_att_out_g, "sg_out_g": m_sg_out_g, "pre_mix_g": m_pre_mix_g,
               "post_mix_g": m_post_mix_g, "pre_ffn_g": m_pre_ffn_g, "post_ffn_g": m_post_ffn_g,
               "ple_gate_b": m_ple_gate_b}
    v_small = {"sg_w": v_sg_w, "f_bias": v_f_bias, "sg_ln_g": v_sg_ln_g, "sg_ln_b": v_sg_ln_b, "sg_b": v_sg_b,
               "att_out_g": v_att_out_g, "sg_out_g": v_sg_out_g, "pre_mix_g": v_pre_mix_g,
               "post_mix_g": v_post_mix_g, "pre_ffn_g": v_pre_ffn_g, "post_ffn_g": v_post_ffn_g,
               "ple_gate_b": v_ple_gate_b}

    k_me = 2 * lax.axis_index("x") + lax.axis_index("y")
    own_slot = lambda got, mine: lax.dynamic_update_slice(got, mine[None], (k_me, 0, 0))
    late_mine = _pack_late(*w_big[1:]).astype(BF16)
    late = _gather_late_start(late_mine)
    win_mine = jnp.pad(w_big[0], ((0, 0), (0, 768 - 642))).reshape(768, 1024).astype(BF16)
    win_k = _win_kernel_order(own_slot(_gather_weights(win_mine), win_mine))
    late_weights = lambda after: _unpack_late(
        own_slot(_gather_late_wait(late[0], late[1], late[2], late[3], after), late_mine))

    names = ("w_in", "w_out", "w_ff1", "w_ff2", "ple_w", "ple_gate_w")
    c1 = jnp.reshape(c, (1,)).astype(jnp.int32)
    own_part = lambda parts, pss: [lax.dynamic_update_slice(pt, lax.dynamic_slice_in_dim(ps, k_me, 1, 0), (k_me, 0, 0))
                                   for pt, ps in zip(parts, pss)]
    tail = {}

    def on_ff_grads(dw1, dw2):
        tail["swap"] = _swap_start([dw1, dw2.reshape(4, D, D)], "ff")
        return tail["swap"][2][0]

    def on_tail_grads(grads):
        dwout, dplew, dwg = grads
        ws, wr, g_thru, land_thru = tail["swap"]
        (dw1, dw2), (got1, got2) = _swap_wait(ws, wr, g_thru, land_thru, dplew, "ff")
        rest = [dwout.reshape(4, 256, D), dplew, dwg.reshape(4, 256, D)]
        got_out, got_ple, got_gate = _swap_halves(rest, "late")
        gs = [rest[0], dw1, dw2, rest[1], rest[2]]
        gots = [got_out, got1, got2, got_ple, got_gate]
        pss = [_pair_sum(nm, c1, g, got) for nm, g, got in zip(names[1:], gs, gots)]
        tail["xch"] = _exchange_start(pss, "late")
        return tail["xch"][4]

    def on_small_grads(gsmall, loss_l):
        tail["small"] = _small_start(*_small_pack(gsmall, loss_l))

    loss_l, dx, dwin_k, gsmall = _local_step(
        x[0], p[0, 0], loss_target[0], win_k, late_weights, late[4], on_ff_grads, on_tail_grads, on_small_grads,
        small)

    dwin = jnp.concatenate([dwin_k[:, :3 * AW], dwin_k[:, 5 * AW:5 * AW + NH], dwin_k[:, 3 * AW:5 * AW]], axis=1)
    dwin = jnp.pad(jnp.swapaxes(dwin.reshape(D, 4, 642), 0, 1), ((0, 0), (0, 0), (0, 768 - 642)))
    ps_in = [_pair_sum(names[0], c1, dwin, _swap_halves([dwin], "in")[0])]
    ins, inr, in_thru, inland_thru, in_token = _exchange_start(ps_in, "in")
    xs, xr, ps_thru, land_thru, _ = tail["xch"]
    ps_late, landed = _exchange_wait(xs, xr, ps_thru, land_thru, in_token, "late")

    def finish(nms, parts, tag, w, m, v):
        ghs = [_reduce_chips(nm, pt) for nm, pt in zip(nms, parts)]
        got2 = _share_grad(ghs, tag)
        return [_update(nm, c1, gh, g2, wi, mi, vi) for nm, gh, g2, wi, mi, vi in zip(nms, ghs, got2, w, m, v)]

    late_out = finish(names[1:], own_part(landed, ps_late), "late", w_big[1:], m_big[1:], v_big[1:])

    view = lambda t: t.reshape(t.shape[-3:]) if t.ndim == 4 else t.reshape(t.shape[-2:])
    views = lambda d: {k: view(d[k]) for k in SMALL_NAMES}
    me = 4 * lax.axis_index("x") + 2 * lax.axis_index("y") + c
    ss, sr, sthru, _ = tail["small"]
    vec, w8, lv, lw = _small_wait(ss, sr, sthru, late_out[-1][3])
    all_v = lax.dynamic_update_slice(lv, vec[None], (me, 0, 0))
    all_w = lax.dynamic_update_slice(lw, w8[None], (me, 0, 0, 0))
    loss11, res_s = _small_update(all_v, all_w, views(small), views(m_small), views(v_small))
    loss = loss11[0, 0]

    ps_in, landed_in = _exchange_wait(ins, inr, in_thru, inland_thru, loss11, "in")
    big_out = finish(names[:1], own_part(landed_in, ps_in), "in", w_big[:1], m_big[:1], v_big[:1])
    big_out += late_out
    big_out = [[big_out[j][i] for j in range(6)] for i in range(4)]

    def small_out(i, name):
        return res_s[name][i].reshape(small[name].shape)

    order = ["w_in", "f_bias", "sg_ln_g", "sg_ln_b", "sg_w", "sg_b", "att_out_g", "sg_out_g", "w_out",
             "pre_mix_g", "post_mix_g", "pre_ffn_g", "post_ffn_g", "w_ff1", "w_ff2", "ple_w", "ple_gate_w",
             "ple_gate_b"]
    big_idx = {"w_in": 0, "w_out": 1, "w_ff1": 2, "w_ff2": 3, "ple_w": 4, "ple_gate_w": 5}
    outs = [loss, dx[None]]
    for i in range(4):
        for name in order:
            if name in big_idx:
                outs.append(big_out[i][big_idx[name]][None])
            else:
                outs.append(small_out(i, name))
    return tuple(outs)
```
